```python
import jax, jax.numpy as jnp
from jax import lax
import numpy as np

D_MODEL = 1024
BATCH = 8
SEQ = 4096
DEPTH = 2

A_WIDTH = D_MODEL // 2
A_GROUPS = 4
A_GROUP_DIM = A_WIDTH // A_GROUPS
CHUNK = 128
B_WIDTH = D_MODEL // 2
B_HEAD_DIM = 64
B_HEADS = B_WIDTH // B_HEAD_DIM
DILATED_PATTERNS = ((128, 1), (512, 4), (2048, 16))
Q_BLOCK = 128
ROPE_THETA = 10000.0
AB_IN = 2 * A_WIDTH + 3 * B_WIDTH
CONV_WIDTH = 31
FFN_DIM = 2816
FFN_CONV_WIDTH = 3
EPS = 1e-6
NEG = -1e30
N_EVEN = (DEPTH + 1) // 2
N_ODD = DEPTH // 2

kernel_name = "hybrid_gmlp_dilated_conformer_convffn"


def rms_norm(x, g):
    xf = x.astype(jnp.float32)
    y = xf * lax.rsqrt(jnp.mean(xf * xf, axis=-1, keepdims=True) + EPS)
    return (y * g.astype(jnp.float32)).astype(x.dtype)


def layer_norm(x, g):
    xf = x.astype(jnp.float32)
    mu = jnp.mean(xf, axis=-1, keepdims=True)
    var = jnp.mean(jnp.square(xf - mu), axis=-1, keepdims=True)
    return ((xf - mu) * lax.rsqrt(var + EPS) * g.astype(jnp.float32)).astype(x.dtype)


def causal_depthwise_conv(x, w, b):
    k = w.shape[0]
    y = lax.conv_general_dilated(
        x, w[:, None, :].astype(x.dtype), window_strides=(1,), padding=((k - 1, 0),),
        dimension_numbers=("NWC", "WIO", "NWC"), feature_group_count=x.shape[-1])
    return y + b


def rotary(x, positions):
    e = x.shape[-1]
    inv_freq = 1.0 / (ROPE_THETA ** (jnp.arange(0, e, 2, dtype=jnp.float32) / e))
    ang = positions.astype(jnp.float32)[..., None] * inv_freq
    cos, sin = jnp.cos(ang)[:, :, None, :], jnp.sin(ang)[:, :, None, :]
    xf = x.astype(jnp.float32)
    x1, x2 = xf[..., : e // 2], xf[..., e // 2:]
    return jnp.concatenate([x1 * cos - x2 * sin, x2 * cos + x1 * sin], -1).astype(x.dtype)


def dilated_branch(q, k, v, window, dilation):
    bn, s, h, e = q.shape
    L = s // dilation
    w = window // dilation
    qb_len = min(Q_BLOCK, L)
    nb = L // qb_len

    def to_sub(t):
        return t.reshape(bn, L, dilation, h, e).transpose(0, 2, 3, 1, 4)

    qs, ks, vs = to_sub(q), to_sub(k), to_sub(v)
    pad = ((0, 0), (0, 0), (0, 0), (w, 0), (0, 0))
    kp, vp = jnp.pad(ks, pad), jnp.pad(vs, pad)
    starts = jnp.arange(nb) * qb_len
    j = jnp.arange(w + qb_len)
    idx = starts[:, None] + j[None, :]
    kb = jnp.take(kp, idx, axis=3)
    vb = jnp.take(vp, idx, axis=3)
    qb = qs.reshape(bn, dilation, h, nb, qb_len, e)
    sc = jnp.einsum("brhnqe,brhnke->brhnqk", qb, kb).astype(jnp.float32) * (e ** -0.5)
    i = jnp.arange(qb_len)
    dist = i[:, None] + w - j[None, :]
    kpos = starts[:, None, None] + j[None, None, :] - w
    mask = (dist >= 0)[None] & (dist <= w)[None] & (kpos >= 0)
    sc = jnp.where(mask, sc, NEG)
    m = jnp.max(sc, axis=-1, keepdims=True)
    p = jnp.exp(sc - m)
    den = jnp.sum(p, axis=-1, keepdims=True)
    o = jnp.einsum("brhnqk,brhnke->brhnqe", p, vb.astype(jnp.float32)) / den
    lse = (m + jnp.log(den))[..., 0]
    o = o.reshape(bn, dilation, h, L, e).transpose(0, 3, 1, 2, 4).reshape(bn, s, h, e)
    lse = lse.reshape(bn, dilation, h, L).transpose(0, 3, 1, 2).reshape(bn, s, h)
    return o, lse


def mixer_ab(h, positions, w_in, a_vnorm_g, a_spatial_w, a_spatial_b, q_norm_g, k_norm_g, w_out):
    bn, s, _ = h.shape
    z = h @ w_in
    ua, va, q, k, v = jnp.split(
        z, [A_WIDTH, 2 * A_WIDTH, 2 * A_WIDTH + B_WIDTH, 2 * A_WIDTH + 2 * B_WIDTH], axis=-1)
    nc = s // CHUNK
    ua = jax.nn.gelu(ua, approximate=False).reshape(bn, nc, CHUNK, A_GROUPS, A_GROUP_DIM)
    va = layer_norm(jax.nn.gelu(va, approximate=False).reshape(bn, s, A_GROUPS, A_GROUP_DIM), a_vnorm_g)
    va = va.reshape(bn, nc, CHUNK, A_GROUPS, A_GROUP_DIM)
    causal = jnp.tril(jnp.ones((CHUNK, CHUNK), dtype=bool))
    ws = jnp.where(causal[None], a_spatial_w, 0.0).astype(va.dtype)
    f = jnp.einsum("gts,bcsgd->bctgd", ws, va) + a_spatial_b.T[None, None, :, :, None]
    ya = (ua * f).reshape(bn, s, A_WIDTH)
    q = rotary(rms_norm(q.reshape(bn, s, B_HEADS, B_HEAD_DIM), q_norm_g), positions)
    k = rotary(rms_norm(k.reshape(bn, s, B_HEADS, B_HEAD_DIM), k_norm_g), positions)
    v = v.reshape(bn, s, B_HEADS, B_HEAD_DIM)
    outs, lses = [], []
    for window, dilation in DILATED_PATTERNS:
        o, lse = dilated_branch(q, k, v, window, dilation)
        outs.append(o)
        lses.append(lse)
    wts = jax.nn.softmax(jnp.stack(lses, 0), axis=0)
    yb = jnp.sum(wts[..., None] * jnp.stack(outs, 0), axis=0)
    yb = yb.astype(h.dtype).reshape(bn, s, B_WIDTH)
    return jnp.concatenate([ya, yb], axis=-1) @ w_out


def conformer_conv(h, pw1_w, pw1_b, dw_w, dw_b, ln_g, ln_b, pw2_w, pw2_b):
    a, g = jnp.split(h @ pw1_w + pw1_b, 2, axis=-1)
    y = a * jax.nn.sigmoid(g)
    y = causal_depthwise_conv(y, dw_w, dw_b)
    y = jax.nn.silu(layer_norm(y, ln_g) + ln_b)
    return y @ pw2_w + pw2_b


def conv_ffn(h, up_w, dw_w, dw_b, down_w):
    z = causal_depthwise_conv(h @ up_w, dw_w, dw_b)
    a, b = jnp.split(z, 2, axis=-1)
    return (jax.nn.silu(a) * b) @ down_w


def modulate(x, g, shift, scale):
    return rms_norm(x, g) * (1.0 + scale[:, None, :]) + shift[:, None, :]


def _fwd_setup_inputs(seed: int = 0) -> dict:
    key = jax.random.key(seed)
    ks = jax.random.split(key, 32)
    D, F = D_MODEL, FFN_DIM

    def nrm(k, shape, scale):
        return jax.random.normal(k, shape, jnp.float32) * scale

    start = jax.random.randint(ks[2], (BATCH, 1), 0, 1024, dtype=jnp.int32)
    positions = (start + jnp.arange(SEQ, dtype=jnp.int32)[None, :]).astype(jnp.int32)
    return {
        "x": nrm(ks[0], (BATCH, SEQ, D), 1.0),
        "c": nrm(ks[1], (BATCH, D), 1.0),
        "positions": positions,
        "ada_w": nrm(ks[3], (DEPTH, D, 6 * D), 0.5 * D ** -0.5),
        "ada_b": nrm(ks[4], (DEPTH, 6 * D), 0.01),
        "norm_mix_g": 1.0 + nrm(ks[5], (DEPTH, D), 0.02),
        "norm_ffn_g": 1.0 + nrm(ks[6], (DEPTH, D), 0.02),
        "ab_w_in": nrm(ks[7], (N_EVEN, D, AB_IN), D ** -0.5),
        "a_vnorm_g": 1.0 + nrm(ks[8], (N_EVEN, A_GROUPS, A_GROUP_DIM), 0.02),
        "a_spatial_w": nrm(ks[9], (N_EVEN, A_GROUPS, CHUNK, CHUNK), CHUNK ** -0.5),
        "a_spatial_b": 1.0 + nrm(ks[10], (N_EVEN, A_GROUPS, CHUNK), 0.1),
        "b_q_norm_g": 1.0 + nrm(ks[11], (N_EVEN, B_HEAD_DIM), 0.02),
        "b_k_norm_g": 1.0 + nrm(ks[12], (N_EVEN, B_HEAD_DIM), 0.02),
        "ab_w_out": nrm(ks[13], (N_EVEN, A_WIDTH + B_WIDTH, D), (A_WIDTH + B_WIDTH) ** -0.5),
        "conv_pw1_w": nrm(ks[14], (N_ODD, D, 2 * D), D ** -0.5),
        "conv_pw1_b": nrm(ks[15], (N_ODD, 2 * D), 0.01),
        "conv_dw_w": nrm(ks[16], (N_ODD, CONV_WIDTH, D), CONV_WIDTH ** -0.5),
        "conv_dw_b": nrm(ks[17], (N_ODD, D), 0.01),
        "conv_ln_g": 1.0 + nrm(ks[18], (N_ODD, D), 0.02),
        "conv_ln_b": nrm(ks[19], (N_ODD, D), 0.01),
        "conv_pw2_w": nrm(ks[20], (N_ODD, D, D), D ** -0.5),
        "conv_pw2_b": nrm(ks[21], (N_ODD, D), 0.01),
        "ffn_up_w": nrm(ks[22], (DEPTH, D, 2 * F), D ** -0.5),
        "ffn_dw_w": nrm(ks[23], (DEPTH, FFN_CONV_WIDTH, 2 * F), FFN_CONV_WIDTH ** -0.5),
        "ffn_dw_b": nrm(ks[24], (DEPTH, 2 * F), 0.01),
        "ffn_down_w": nrm(ks[25], (DEPTH, F, D), F ** -0.5),
    }


def _fwd_reference(x, c, positions, ada_w, ada_b, norm_mix_g, norm_ffn_g, ab_w_in, a_vnorm_g,
              a_spatial_w, a_spatial_b, b_q_norm_g, b_k_norm_g, ab_w_out, conv_pw1_w,
              conv_pw1_b, conv_dw_w, conv_dw_b, conv_ln_g, conv_ln_b, conv_pw2_w, conv_pw2_b,
              ffn_up_w, ffn_dw_w, ffn_dw_b, ffn_down_w):
    c_act = jax.nn.silu(c)
    for layer in range(DEPTH):
        mod = c_act @ ada_w[layer] + ada_b[layer]
        sh_m, sc_m, g_m, sh_f, sc_f, g_f = jnp.split(mod, 6, axis=-1)
        h = modulate(x, norm_mix_g[layer], sh_m, sc_m)
        li = layer // 2
        if layer % 2 == 0:
            y = mixer_ab(h, positions, ab_w_in[li], a_vnorm_g[li], a_spatial_w[li],
                         a_spatial_b[li], b_q_norm_g[li], b_k_norm_g[li], ab_w_out[li])
        else:
            y = conformer_conv(h, conv_pw1_w[li], conv_pw1_b[li], conv_dw_w[li], conv_dw_b[li],
                               conv_ln_g[li], conv_ln_b[li], conv_pw2_w[li], conv_pw2_b[li])
        x = x + g_m[:, None, :] * y
        h = modulate(x, norm_ffn_g[layer], sh_f, sc_f)
        x = x + g_f[:, None, :] * conv_ffn(h, ffn_up_w[layer], ffn_dw_w[layer],
                                           ffn_dw_b[layer], ffn_down_w[layer])
    return x


import jax as _jax
import jax.numpy as _jnp

TWIN_FORMAT = 'train_step'
FWD_PARAMS = ['x', 'c', 'positions', 'ada_w', 'ada_b', 'norm_mix_g', 'norm_ffn_g', 'ab_w_in', 'a_vnorm_g', 'a_spatial_w', 'a_spatial_b', 'b_q_norm_g', 'b_k_norm_g', 'ab_w_out', 'conv_pw1_w', 'conv_pw1_b', 'conv_dw_w', 'conv_dw_b', 'conv_ln_g', 'conv_ln_b', 'conv_pw2_w', 'conv_pw2_b', 'ffn_up_w', 'ffn_dw_w', 'ffn_dw_b', 'ffn_down_w']
TWIN_WEIGHTS = ['ada_w', 'ada_b', 'norm_mix_g', 'norm_ffn_g', 'ab_w_in', 'a_vnorm_g', 'a_spatial_w', 'a_spatial_b', 'b_q_norm_g', 'b_k_norm_g', 'ab_w_out', 'conv_pw1_w', 'conv_pw1_b', 'conv_dw_w', 'conv_dw_b', 'conv_ln_g', 'conv_ln_b', 'conv_pw2_w', 'conv_pw2_b', 'ffn_up_w', 'ffn_dw_w', 'ffn_dw_b', 'ffn_down_w']
TWIN_DIFF_INPUT = 'x'
TWIN_INPUTS = ['x', 'c', 'positions', 'ada_w', 'ada_b', 'norm_mix_g', 'norm_ffn_g', 'ab_w_in', 'a_vnorm_g', 'a_spatial_w', 'a_spatial_b', 'b_q_norm_g', 'b_k_norm_g', 'ab_w_out', 'conv_pw1_w', 'conv_pw1_b', 'conv_dw_w', 'conv_dw_b', 'conv_ln_g', 'conv_ln_b', 'conv_pw2_w', 'conv_pw2_b', 'ffn_up_w', 'ffn_dw_w', 'ffn_dw_b', 'ffn_down_w', 'loss_target', 'm_ada_w', 'm_ada_b', 'm_norm_mix_g', 'm_norm_ffn_g', 'm_ab_w_in', 'm_a_vnorm_g', 'm_a_spatial_w', 'm_a_spatial_b', 'm_b_q_norm_g', 'm_b_k_norm_g', 'm_ab_w_out', 'm_conv_pw1_w', 'm_conv_pw1_b', 'm_conv_dw_w', 'm_conv_dw_b', 'm_conv_ln_g', 'm_conv_ln_b', 'm_conv_pw2_w', 'm_conv_pw2_b', 'm_ffn_up_w', 'm_ffn_dw_w', 'm_ffn_dw_b', 'm_ffn_down_w', 'v_ada_w', 'v_ada_b', 'v_norm_mix_g', 'v_norm_ffn_g', 'v_ab_w_in', 'v_a_vnorm_g', 'v_a_spatial_w', 'v_a_spatial_b', 'v_b_q_norm_g', 'v_b_k_norm_g', 'v_ab_w_out', 'v_conv_pw1_w', 'v_conv_pw1_b', 'v_conv_dw_w', 'v_conv_dw_b', 'v_conv_ln_g', 'v_conv_ln_b', 'v_conv_pw2_w', 'v_conv_pw2_b', 'v_ffn_up_w', 'v_ffn_dw_w', 'v_ffn_dw_b', 'v_ffn_down_w']
TWIN_OUTPUTS = ['loss', 'grad_x', 'grad_ada_w', 'grad_ada_b', 'grad_norm_mix_g', 'grad_norm_ffn_g', 'grad_ab_w_in', 'grad_a_vnorm_g', 'grad_a_spatial_w', 'grad_a_spatial_b', 'grad_b_q_norm_g', 'grad_b_k_norm_g', 'grad_ab_w_out', 'grad_conv_pw1_w', 'grad_conv_pw1_b', 'grad_conv_dw_w', 'grad_conv_dw_b', 'grad_conv_ln_g', 'grad_conv_ln_b', 'grad_conv_pw2_w', 'grad_conv_pw2_b', 'grad_ffn_up_w', 'grad_ffn_dw_w', 'grad_ffn_dw_b', 'grad_ffn_down_w', 'delta_ada_w', 'delta_ada_b', 'delta_norm_mix_g', 'delta_norm_ffn_g', 'delta_ab_w_in', 'delta_a_vnorm_g', 'delta_a_spatial_w', 'delta_a_spatial_b', 'delta_b_q_norm_g', 'delta_b_k_norm_g', 'delta_ab_w_out', 'delta_conv_pw1_w', 'delta_conv_pw1_b', 'delta_conv_dw_w', 'delta_conv_dw_b', 'delta_conv_ln_g', 'delta_conv_ln_b', 'delta_conv_pw2_w', 'delta_conv_pw2_b', 'delta_ffn_up_w', 'delta_ffn_dw_w', 'delta_ffn_dw_b', 'delta_ffn_down_w', 'new_m_ada_w', 'new_m_ada_b', 'new_m_norm_mix_g', 'new_m_norm_ffn_g', 'new_m_ab_w_in', 'new_m_a_vnorm_g', 'new_m_a_spatial_w', 'new_m_a_spatial_b', 'new_m_b_q_norm_g', 'new_m_b_k_norm_g', 'new_m_ab_w_out', 'new_m_conv_pw1_w', 'new_m_conv_pw1_b', 'new_m_conv_dw_w', 'new_m_conv_dw_b', 'new_m_conv_ln_g', 'new_m_conv_ln_b', 'new_m_conv_pw2_w', 'new_m_conv_pw2_b', 'new_m_ffn_up_w', 'new_m_ffn_dw_w', 'new_m_ffn_dw_b', 'new_m_ffn_down_w', 'new_v_ada_w', 'new_v_ada_b', 'new_v_norm_mix_g', 'new_v_norm_ffn_g', 'new_v_ab_w_in', 'new_v_a_vnorm_g', 'new_v_a_spatial_w', 'new_v_a_spatial_b', 'new_v_b_q_norm_g', 'new_v_b_k_norm_g', 'new_v_ab_w_out', 'new_v_conv_pw1_w', 'new_v_conv_pw1_b', 'new_v_conv_dw_w', 'new_v_conv_dw_b', 'new_v_conv_ln_g', 'new_v_conv_ln_b', 'new_v_conv_pw2_w', 'new_v_conv_pw2_b', 'new_v_ffn_up_w', 'new_v_ffn_dw_w', 'new_v_ffn_dw_b', 'new_v_ffn_down_w']
TWIN_LEAF_KINDS = {'loss': 'loss', 'grad_x': 'grad_x', 'grad_ada_w': 'grad_w', 'grad_ada_b': 'grad_w', 'grad_norm_mix_g': 'grad_w', 'grad_norm_ffn_g': 'grad_w', 'grad_ab_w_in': 'grad_w', 'grad_a_vnorm_g': 'grad_w', 'grad_a_spatial_w': 'grad_w', 'grad_a_spatial_b': 'grad_w', 'grad_b_q_norm_g': 'grad_w', 'grad_b_k_norm_g': 'grad_w', 'grad_ab_w_out': 'grad_w', 'grad_conv_pw1_w': 'grad_w', 'grad_conv_pw1_b': 'grad_w', 'grad_conv_dw_w': 'grad_w', 'grad_conv_dw_b': 'grad_w', 'grad_conv_ln_g': 'grad_w', 'grad_conv_ln_b': 'grad_w', 'grad_conv_pw2_w': 'grad_w', 'grad_conv_pw2_b': 'grad_w', 'grad_ffn_up_w': 'grad_w', 'grad_ffn_dw_w': 'grad_w', 'grad_ffn_dw_b': 'grad_w', 'grad_ffn_down_w': 'grad_w', 'delta_ada_w': 'delta_w', 'delta_ada_b': 'delta_w', 'delta_norm_mix_g': 'delta_w', 'delta_norm_ffn_g': 'delta_w', 'delta_ab_w_in': 'delta_w', 'delta_a_vnorm_g': 'delta_w', 'delta_a_spatial_w': 'delta_w', 'delta_a_spatial_b': 'delta_w', 'delta_b_q_norm_g': 'delta_w', 'delta_b_k_norm_g': 'delta_w', 'delta_ab_w_out': 'delta_w', 'delta_conv_pw1_w': 'delta_w', 'delta_conv_pw1_b': 'delta_w', 'delta_conv_dw_w': 'delta_w', 'delta_conv_dw_b': 'delta_w', 'delta_conv_ln_g': 'delta_w', 'delta_conv_ln_b': 'delta_w', 'delta_conv_pw2_w': 'delta_w', 'delta_conv_pw2_b': 'delta_w', 'delta_ffn_up_w': 'delta_w', 'delta_ffn_dw_w': 'delta_w', 'delta_ffn_dw_b': 'delta_w', 'delta_ffn_down_w': 'delta_w', 'new_m_ada_w': 'new_m', 'new_m_ada_b': 'new_m', 'new_m_norm_mix_g': 'new_m', 'new_m_norm_ffn_g': 'new_m', 'new_m_ab_w_in': 'new_m', 'new_m_a_vnorm_g': 'new_m', 'new_m_a_spatial_w': 'new_m', 'new_m_a_spatial_b': 'new_m', 'new_m_b_q_norm_g': 'new_m', 'new_m_b_k_norm_g': 'new_m', 'new_m_ab_w_out': 'new_m', 'new_m_conv_pw1_w': 'new_m', 'new_m_conv_pw1_b': 'new_m', 'new_m_conv_dw_w': 'new_m', 'new_m_conv_dw_b': 'new_m', 'new_m_conv_ln_g': 'new_m', 'new_m_conv_ln_b': 'new_m', 'new_m_conv_pw2_w': 'new_m', 'new_m_conv_pw2_b': 'new_m', 'new_m_ffn_up_w': 'new_m', 'new_m_ffn_dw_w': 'new_m', 'new_m_ffn_dw_b': 'new_m', 'new_m_ffn_down_w': 'new_m', 'new_v_ada_w': 'new_v', 'new_v_ada_b': 'new_v', 'new_v_norm_mix_g': 'new_v', 'new_v_norm_ffn_g': 'new_v', 'new_v_ab_w_in': 'new_v', 'new_v_a_vnorm_g': 'new_v', 'new_v_a_spatial_w': 'new_v', 'new_v_a_spatial_b': 'new_v', 'new_v_b_q_norm_g': 'new_v', 'new_v_b_k_norm_g': 'new_v', 'new_v_ab_w_out': 'new_v', 'new_v_conv_pw1_w': 'new_v', 'new_v_conv_pw1_b': 'new_v', 'new_v_conv_dw_w': 'new_v', 'new_v_conv_dw_b': 'new_v', 'new_v_conv_ln_g': 'new_v', 'new_v_conv_ln_b': 'new_v', 'new_v_conv_pw2_w': 'new_v', 'new_v_conv_pw2_b': 'new_v', 'new_v_ffn_up_w': 'new_v', 'new_v_ffn_dw_w': 'new_v', 'new_v_ffn_dw_b': 'new_v', 'new_v_ffn_down_w': 'new_v'}


def _forward(args):
    return _fwd_reference(*[args[k] for k in FWD_PARAMS])


def _output_shape():
    out = _jax.eval_shape(lambda: _forward(_fwd_setup_inputs(0)))
    return out.shape, out.dtype

N_MICROBATCH = 1
ADAM_LR = 0.001
ADAM_B1 = 0.9
ADAM_B2 = 0.999
ADAM_EPS = 1e-08
ADAM_WD = 0.01
ADAM_STEP = 10
PER_EXAMPLE_BATCH_AXIS = {'x': 0, 'c': 0, 'positions': 0, 'loss_target': 0}
SHARED_INPUTS = []
_WEIGHT_DTYPES = {'ada_w': _jnp.float32, 'ada_b': _jnp.float32, 'norm_mix_g': _jnp.float32, 'norm_ffn_g': _jnp.float32, 'ab_w_in': _jnp.float32, 'a_vnorm_g': _jnp.float32, 'a_spatial_w': _jnp.float32, 'a_spatial_b': _jnp.float32, 'b_q_norm_g': _jnp.float32, 'b_k_norm_g': _jnp.float32, 'ab_w_out': _jnp.float32, 'conv_pw1_w': _jnp.float32, 'conv_pw1_b': _jnp.float32, 'conv_dw_w': _jnp.float32, 'conv_dw_b': _jnp.float32, 'conv_ln_g': _jnp.float32, 'conv_ln_b': _jnp.float32, 'conv_pw2_w': _jnp.float32, 'conv_pw2_b': _jnp.float32, 'ffn_up_w': _jnp.float32, 'ffn_dw_w': _jnp.float32, 'ffn_dw_b': _jnp.float32, 'ffn_down_w': _jnp.float32}
MOMENT_SCALE = {'ada_w': 1.066296e+00, 'ada_b': 2.283187e+00, 'norm_mix_g': 8.814080e-01, 'norm_ffn_g': 3.645744e+00, 'ab_w_in': 1.871547e-01, 'a_vnorm_g': 7.870017e-01, 'a_spatial_w': 8.301434e-02, 'a_spatial_b': 1.596303e+00, 'b_q_norm_g': 1.464427e-01, 'b_k_norm_g': 1.462349e-01, 'ab_w_out': 3.300660e-01, 'conv_pw1_w': 8.203198e-02, 'conv_pw1_b': 3.777639e-01, 'conv_dw_w': 1.198918e-01, 'conv_dw_b': 8.179281e-01, 'conv_ln_g': 1.412754e+00, 'conv_ln_b': 9.735997e-01, 'conv_pw2_w': 2.039247e-01, 'conv_pw2_b': 9.910237e-01, 'ffn_up_w': 1.272506e-01, 'ffn_dw_w': 5.446847e-01, 'ffn_dw_b': 4.513674e-01, 'ffn_down_w': 9.730546e-02}


def _to_microbatches(a, axis):
    t = _jnp.moveaxis(a, axis, 0)
    t = t.reshape((N_MICROBATCH, t.shape[0] // N_MICROBATCH) + t.shape[1:])
    return _jnp.moveaxis(t, 1, axis + 1)


def setup_inputs(seed: int = 0) -> dict:
    inp = _fwd_setup_inputs(seed)
    key = _jax.random.fold_in(_jax.random.key(seed), 7919)
    shape, _ = _output_shape()
    out = dict(inp)
    out["loss_target"] = _jax.random.normal(_jax.random.fold_in(key, 0), shape, _jnp.float32)
    for i, name in enumerate(TWIN_WEIGHTS):
        w = inp[name].astype(_jnp.float32)
        if MOMENT_SCALE is None:
            s = _jnp.sqrt(_jnp.mean(_jnp.square(w)) + 1e-30)
        else:
            s = MOMENT_SCALE[name]
        km, kv = _jax.random.split(_jax.random.fold_in(key, i + 1))
        out[name] = w
        out["m_" + name] = s * _jax.random.normal(km, w.shape, _jnp.float32)
        out["v_" + name] = (s * s) * _jax.random.uniform(kv, w.shape, _jnp.float32, 0.5, 1.5)
    if N_MICROBATCH > 1:
        for name, axis in PER_EXAMPLE_BATCH_AXIS.items():
            out[name] = _to_microbatches(out[name], axis)
    return {'x': out['x'], 'c': out['c'], 'positions': out['positions'], 'ada_w': out['ada_w'], 'ada_b': out['ada_b'], 'norm_mix_g': out['norm_mix_g'], 'norm_ffn_g': out['norm_ffn_g'], 'ab_w_in': out['ab_w_in'], 'a_vnorm_g': out['a_vnorm_g'], 'a_spatial_w': out['a_spatial_w'], 'a_spatial_b': out['a_spatial_b'], 'b_q_norm_g': out['b_q_norm_g'], 'b_k_norm_g': out['b_k_norm_g'], 'ab_w_out': out['ab_w_out'], 'conv_pw1_w': out['conv_pw1_w'], 'conv_pw1_b': out['conv_pw1_b'], 'conv_dw_w': out['conv_dw_w'], 'conv_dw_b': out['conv_dw_b'], 'conv_ln_g': out['conv_ln_g'], 'conv_ln_b': out['conv_ln_b'], 'conv_pw2_w': out['conv_pw2_w'], 'conv_pw2_b': out['conv_pw2_b'], 'ffn_up_w': out['ffn_up_w'], 'ffn_dw_w': out['ffn_dw_w'], 'ffn_dw_b': out['ffn_dw_b'], 'ffn_down_w': out['ffn_down_w'], 'loss_target': out['loss_target'], 'm_ada_w': out['m_ada_w'], 'm_ada_b': out['m_ada_b'], 'm_norm_mix_g': out['m_norm_mix_g'], 'm_norm_ffn_g': out['m_norm_ffn_g'], 'm_ab_w_in': out['m_ab_w_in'], 'm_a_vnorm_g': out['m_a_vnorm_g'], 'm_a_spatial_w': out['m_a_spatial_w'], 'm_a_spatial_b': out['m_a_spatial_b'], 'm_b_q_norm_g': out['m_b_q_norm_g'], 'm_b_k_norm_g': out['m_b_k_norm_g'], 'm_ab_w_out': out['m_ab_w_out'], 'm_conv_pw1_w': out['m_conv_pw1_w'], 'm_conv_pw1_b': out['m_conv_pw1_b'], 'm_conv_dw_w': out['m_conv_dw_w'], 'm_conv_dw_b': out['m_conv_dw_b'], 'm_conv_ln_g': out['m_conv_ln_g'], 'm_conv_ln_b': out['m_conv_ln_b'], 'm_conv_pw2_w': out['m_conv_pw2_w'], 'm_conv_pw2_b': out['m_conv_pw2_b'], 'm_ffn_up_w': out['m_ffn_up_w'], 'm_ffn_dw_w': out['m_ffn_dw_w'], 'm_ffn_dw_b': out['m_ffn_dw_b'], 'm_ffn_down_w': out['m_ffn_down_w'], 'v_ada_w': out['v_ada_w'], 'v_ada_b': out['v_ada_b'], 'v_norm_mix_g': out['v_norm_mix_g'], 'v_norm_ffn_g': out['v_norm_ffn_g'], 'v_ab_w_in': out['v_ab_w_in'], 'v_a_vnorm_g': out['v_a_vnorm_g'], 'v_a_spatial_w': out['v_a_spatial_w'], 'v_a_spatial_b': out['v_a_spatial_b'], 'v_b_q_norm_g': out['v_b_q_norm_g'], 'v_b_k_norm_g': out['v_b_k_norm_g'], 'v_ab_w_out': out['v_ab_w_out'], 'v_conv_pw1_w': out['v_conv_pw1_w'], 'v_conv_pw1_b': out['v_conv_pw1_b'], 'v_conv_dw_w': out['v_conv_dw_w'], 'v_conv_dw_b': out['v_conv_dw_b'], 'v_conv_ln_g': out['v_conv_ln_g'], 'v_conv_ln_b': out['v_conv_ln_b'], 'v_conv_pw2_w': out['v_conv_pw2_w'], 'v_conv_pw2_b': out['v_conv_pw2_b'], 'v_ffn_up_w': out['v_ffn_up_w'], 'v_ffn_dw_w': out['v_ffn_dw_w'], 'v_ffn_dw_b': out['v_ffn_dw_b'], 'v_ffn_down_w': out['v_ffn_down_w']}


def _loss(weights, diff, rest, loss_target):
    with _jax.named_scope("forward"):
        args = {**rest, TWIN_DIFF_INPUT: diff, **{k: w.astype(_WEIGHT_DTYPES[k]) for k, w in weights.items()}}
        y = _forward(args)
    with _jax.named_scope("loss_head"):
        err = _jnp.square(y.astype(_jnp.float32) - loss_target)
        return 0.5 * _jnp.sum(_jnp.mean(err, axis=-1)) if err.ndim else 0.5 * err


def _adamw(w, g, m, v):
    m = ADAM_B1 * m + (1.0 - ADAM_B1) * g
    v = ADAM_B2 * v + (1.0 - ADAM_B2) * _jnp.square(g)
    m_hat = m / (1.0 - ADAM_B1 ** ADAM_STEP)
    v_hat = v / (1.0 - ADAM_B2 ** ADAM_STEP)
    delta = -ADAM_LR * (m_hat / (_jnp.sqrt(v_hat) + ADAM_EPS) + ADAM_WD * w)
    return delta, m, v


def reference(x, c, positions, ada_w, ada_b, norm_mix_g, norm_ffn_g, ab_w_in, a_vnorm_g, a_spatial_w, a_spatial_b, b_q_norm_g, b_k_norm_g, ab_w_out, conv_pw1_w, conv_pw1_b, conv_dw_w, conv_dw_b, conv_ln_g, conv_ln_b, conv_pw2_w, conv_pw2_b, ffn_up_w, ffn_dw_w, ffn_dw_b, ffn_down_w, loss_target, m_ada_w, m_ada_b, m_norm_mix_g, m_norm_ffn_g, m_ab_w_in, m_a_vnorm_g, m_a_spatial_w, m_a_spatial_b, m_b_q_norm_g, m_b_k_norm_g, m_ab_w_out, m_conv_pw1_w, m_conv_pw1_b, m_conv_dw_w, m_conv_dw_b, m_conv_ln_g, m_conv_ln_b, m_conv_pw2_w, m_conv_pw2_b, m_ffn_up_w, m_ffn_dw_w, m_ffn_dw_b, m_ffn_down_w, v_ada_w, v_ada_b, v_norm_mix_g, v_norm_ffn_g, v_ab_w_in, v_a_vnorm_g, v_a_spatial_w, v_a_spatial_b, v_b_q_norm_g, v_b_k_norm_g, v_ab_w_out, v_conv_pw1_w, v_conv_pw1_b, v_conv_dw_w, v_conv_dw_b, v_conv_ln_g, v_conv_ln_b, v_conv_pw2_w, v_conv_pw2_b, v_ffn_up_w, v_ffn_dw_w, v_ffn_dw_b, v_ffn_down_w):
    given = dict(x=x, c=c, positions=positions, ada_w=ada_w, ada_b=ada_b, norm_mix_g=norm_mix_g, norm_ffn_g=norm_ffn_g, ab_w_in=ab_w_in, a_vnorm_g=a_vnorm_g, a_spatial_w=a_spatial_w, a_spatial_b=a_spatial_b, b_q_norm_g=b_q_norm_g, b_k_norm_g=b_k_norm_g, ab_w_out=ab_w_out, conv_pw1_w=conv_pw1_w, conv_pw1_b=conv_pw1_b, conv_dw_w=conv_dw_w, conv_dw_b=conv_dw_b, conv_ln_g=conv_ln_g, conv_ln_b=conv_ln_b, conv_pw2_w=conv_pw2_w, conv_pw2_b=conv_pw2_b, ffn_up_w=ffn_up_w, ffn_dw_w=ffn_dw_w, ffn_dw_b=ffn_dw_b, ffn_down_w=ffn_down_w, loss_target=loss_target, m_ada_w=m_ada_w, m_ada_b=m_ada_b, m_norm_mix_g=m_norm_mix_g, m_norm_ffn_g=m_norm_ffn_g, m_ab_w_in=m_ab_w_in, m_a_vnorm_g=m_a_vnorm_g, m_a_spatial_w=m_a_spatial_w, m_a_spatial_b=m_a_spatial_b, m_b_q_norm_g=m_b_q_norm_g, m_b_k_norm_g=m_b_k_norm_g, m_ab_w_out=m_ab_w_out, m_conv_pw1_w=m_conv_pw1_w, m_conv_pw1_b=m_conv_pw1_b, m_conv_dw_w=m_conv_dw_w, m_conv_dw_b=m_conv_dw_b, m_conv_ln_g=m_conv_ln_g, m_conv_ln_b=m_conv_ln_b, m_conv_pw2_w=m_conv_pw2_w, m_conv_pw2_b=m_conv_pw2_b, m_ffn_up_w=m_ffn_up_w, m_ffn_dw_w=m_ffn_dw_w, m_ffn_dw_b=m_ffn_dw_b, m_ffn_down_w=m_ffn_down_w, v_ada_w=v_ada_w, v_ada_b=v_ada_b, v_norm_mix_g=v_norm_mix_g, v_norm_ffn_g=v_norm_ffn_g, v_ab_w_in=v_ab_w_in, v_a_vnorm_g=v_a_vnorm_g, v_a_spatial_w=v_a_spatial_w, v_a_spatial_b=v_a_spatial_b, v_b_q_norm_g=v_b_q_norm_g, v_b_k_norm_g=v_b_k_norm_g, v_ab_w_out=v_ab_w_out, v_conv_pw1_w=v_conv_pw1_w, v_conv_pw1_b=v_conv_pw1_b, v_conv_dw_w=v_conv_dw_w, v_conv_dw_b=v_conv_dw_b, v_conv_ln_g=v_conv_ln_g, v_conv_ln_b=v_conv_ln_b, v_conv_pw2_w=v_conv_pw2_w, v_conv_pw2_b=v_conv_pw2_b, v_ffn_up_w=v_ffn_up_w, v_ffn_dw_w=v_ffn_dw_w, v_ffn_dw_b=v_ffn_dw_b, v_ffn_down_w=v_ffn_down_w)
    weights = {n: given[n] for n in TWIN_WEIGHTS}
    shared = {n: given[n] for n in SHARED_INPUTS}
    per_example = {n: given[n] for n in ['x', 'c', 'positions']}
    grad_fn = _jax.value_and_grad(_loss, argnums=(0, 1))

    def one_microbatch(ex, loss_target):
        ex = dict(ex)
        diff = ex.pop(TWIN_DIFF_INPUT)
        return grad_fn(weights, diff, {**shared, **ex}, loss_target)

    if N_MICROBATCH == 1:
        loss, (grad_w, grad_x) = one_microbatch(per_example, given["loss_target"])
    else:
        def body(carry, xs):
            loss_sum, grad_sum = carry
            l_k, (gw_k, gx_k) = one_microbatch(xs[0], xs[1])
            with _jax.named_scope("update"):
                return (loss_sum + l_k, _jax.tree.map(_jnp.add, grad_sum, gw_k)), gx_k

        init = (_jnp.zeros((), _jnp.float32), _jax.tree.map(_jnp.zeros_like, weights))
        (loss, grad_w), grad_x = _jax.lax.scan(body, init, (per_example, given["loss_target"]))
    with _jax.named_scope("update"):
        delta_w, new_m, new_v = {}, {}, {}
        for n in TWIN_WEIGHTS:
            delta_w[n], new_m[n], new_v[n] = _adamw(weights[n], grad_w[n], given["m_" + n], given["v_" + n])
    return (loss, grad_x, *[grad_w[n] for n in TWIN_WEIGHTS], *[delta_w[n] for n in TWIN_WEIGHTS],
            *[new_m[n] for n in TWIN_WEIGHTS], *[new_v[n] for n in TWIN_WEIGHTS])
```

```python
import functools

import jax
import jax.numpy as jnp
from jax import lax
from jax.experimental import pallas as pl
from jax.experimental.pallas import tpu as pltpu

F32, BF16 = jnp.float32, jnp.bfloat16
EPS = 1e-6
NEG = -1e30
ROPE_THETA = 10000.0
LANES = 128
VMEM_LIMIT = 56 * 1024 * 1024
ADAM_LR, ADAM_B1, ADAM_B2, ADAM_EPS, ADAM_WD, ADAM_STEP = 0.001, 0.9, 0.999, 1e-08, 0.01, 10
MESH = pl.DeviceIdType.MESH


def _params(sem):
    return pltpu.CompilerParams(dimension_semantics=sem, vmem_limit_bytes=VMEM_LIMIT)


_DN = {"nn": (((1,), (0,)), ((), ())), "nt": (((1,), (1,)), ((), ())), "tn": (((0,), (0,)), ((), ()))}


def _matmul(name, a, b, mode, out_dtype, tm=512, tn=512, tk=1024, bias=None):
    if mode == "nn":
        (M, K), N = a.shape, b.shape[1]
    elif mode == "nt":
        (M, K), N = a.shape, b.shape[0]
    else:
        (K, M), N = a.shape, b.shape[1]
    tm, tn, tk = min(tm, M), min(tn, N), min(tk, K)
    assert M % tm == 0 and N % tn == 0 and K % tk == 0, (name, M, N, K, tm, tn, tk)
    nk = K // tk
    if mode == "tn":
        a_spec = pl.BlockSpec((tk, tm), lambda i, j, k: (k, i))
    else:
        a_spec = pl.BlockSpec((tm, tk), lambda i, j, k: (i, k))
    if mode == "nt":
        b_spec = pl.BlockSpec((tn, tk), lambda i, j, k: (j, k))
    else:
        b_spec = pl.BlockSpec((tk, tn), lambda i, j, k: (k, j))
    in_specs, args = [a_spec, b_spec], [a, b]
    if bias is not None:
        in_specs.append(pl.BlockSpec((1, tn), lambda i, j, k: (0, j)))
        args.append(bias)
    n_in = len(args)

    def body(*refs):
        a_ref, b_ref, o_ref = refs[0], refs[1], refs[n_in]
        p = lax.dot_general(a_ref[...], b_ref[...], _DN[mode], preferred_element_type=F32)

        def finish(acc):
            if bias is not None:
                acc = acc + refs[2][...]
            o_ref[...] = acc.astype(o_ref.dtype)

        if nk == 1:
            finish(p)
        else:
            acc_ref = refs[n_in + 1]
            k = pl.program_id(2)

            @pl.when(k == 0)
            def _():
                acc_ref[...] = p

            @pl.when(k > 0)
            def _():
                acc_ref[...] += p

            @pl.when(k == nk - 1)
            def _():
                finish(acc_ref[...])

    return pl.pallas_call(
        body, name=name, grid=(M // tm, N // tn, nk), in_specs=in_specs,
        out_specs=pl.BlockSpec((tm, tn), lambda i, j, k: (i, j)),
        out_shape=jax.ShapeDtypeStruct((M, N), out_dtype),
        scratch_shapes=[pltpu.VMEM((tm, tn), F32)] if nk > 1 else [],
        compiler_params=_params(("parallel", "parallel", "arbitrary")),
    )(*args)


def _rowcall(name, body, nrows, tm, ins, outs):
    nblk = nrows // tm
    assert nrows % tm == 0

    def spec(kind, shape):
        k = kind[0]
        if k == "row":
            cw, cb = kind[1] or shape[-1], kind[2]
            return pl.BlockSpec((tm, cw), lambda i: (i, cb))
        if k == "prev":
            hb, cw, cb = kind[1], kind[2] or shape[-1], kind[3]
            r = tm // hb
            return pl.BlockSpec((hb, cw), lambda i: (jnp.maximum(i * r - 1, 0), cb))
        if k == "next":
            hb, cw, cb = kind[1], kind[2] or shape[-1], kind[3]
            r, last = tm // hb, nrows // hb - 1
            return pl.BlockSpec((hb, cw), lambda i: (jnp.minimum((i + 1) * r, last), cb))
        if k == "off":
            off, cw, cb = kind[1], kind[2] or shape[-1], kind[3]
            return pl.BlockSpec((tm, cw), lambda i: (jnp.clip(i + off, 0, nblk - 1), cb))
        nd = len(shape)
        return pl.BlockSpec(tuple(shape), lambda i: (0,) * nd)

    has_acc = any(o[2][0] == "acc" for o in outs)
    return pl.pallas_call(
        body, name=name, grid=(nblk,),
        in_specs=[spec(kind, a.shape) for a, kind in ins],
        out_specs=[spec(kind, shape) for shape, _, kind in outs],
        out_shape=[jax.ShapeDtypeStruct(tuple(shape), dt) for shape, dt, _ in outs],
        compiler_params=_params(("arbitrary",) if has_acc else ("parallel",)),
    )(*[a for a, _ in ins])


ROW = ("row", None, 0)
FULL = ("full",)
ACC = ("acc",)


def _colsum(x):
    return jnp.sum(x, axis=0, keepdims=True)


def _acc_add(i, ref, val, rows=None):
    idx = (slice(None),) * len(ref.shape) if rows is None else rows

    @pl.when(i == 0)
    def _():
        ref[idx] = val

    @pl.when(i > 0)
    def _():
        ref[idx] = ref[idx] + val


def _sigmoid(x):
    return 1.0 / (1.0 + jnp.exp(-x))


def _gelu(x):
    return 0.5 * x * (1.0 + lax.erf(x * (2.0 ** -0.5)))


def _gelu_grad(x):
    return 0.5 * (1.0 + lax.erf(x * (2.0 ** -0.5))) + x * jnp.exp(-0.5 * x * x) * ((2.0 * jnp.pi) ** -0.5)


def _shift_prev(ext, s, hb):
    if s == 0:
        return ext[hb:]
    return pltpu.roll(ext, s, axis=0)[hb:]


def _shift_next(ext, s, tm):
    if s == 0:
        return ext[:tm]
    return pltpu.roll(ext, ext.shape[0] - s, axis=0)[:tm]


def _rms_mod_val(x, g, sc, sh):
    r = lax.rsqrt(jnp.mean(x * x, axis=-1, keepdims=True) + EPS)
    return x * r * g * (1.0 + sc) + sh


def _mod_first(name, x, g, sc, sh, tm=256):
    S, D = x.shape

    def body(x_ref, g_ref, sc_ref, sh_ref, h_ref):
        h_ref[...] = _rms_mod_val(x_ref[...], g_ref[...], sc_ref[...], sh_ref[...]).astype(BF16)

    return _rowcall(name, body, S, tm, [(x, ROW), (g, FULL), (sc, FULL), (sh, FULL)], [((S, D), BF16, ROW)])[0]


def _resid_mod(name, x, y, gate, g, sc, sh, tm=256):
    S, D = x.shape

    def body(x_ref, y_ref, gate_ref, g_ref, sc_ref, sh_ref, xo_ref, h_ref):
        xn = x_ref[...] + gate_ref[...] * y_ref[...]
        xo_ref[...] = xn
        h_ref[...] = _rms_mod_val(xn, g_ref[...], sc_ref[...], sh_ref[...]).astype(BF16)

    return _rowcall(name, body, S, tm,
                    [(x, ROW), (y, ROW), (gate, FULL), (g, FULL), (sc, FULL), (sh, FULL)],
                    [((S, D), F32, ROW), ((S, D), BF16, ROW)])


def _loss_head(name, x, y, gate, tgt, tm=256):
    S, D = x.shape

    def body(x_ref, y_ref, gate_ref, t_ref, dx_ref, l_ref):
        i = pl.program_id(0)
        err = x_ref[...] + gate_ref[...] * y_ref[...] - t_ref[...]
        dx_ref[...] = err * (1.0 / D)
        _acc_add(i, l_ref, _colsum(err * err))

    return _rowcall(name, body, S, tm, [(x, ROW), (y, ROW), (gate, FULL), (tgt, ROW)],
                    [((S, D), F32, ROW), ((1, D), F32, ACC)])


def _gate_bwd(name, dxo, y, gate, tm=256):
    S, D = dxo.shape

    def body(d_ref, y_ref, gate_ref, dy_ref, dg_ref, db_ref):
        i = pl.program_id(0)
        d = d_ref[...]
        dy = d * gate_ref[...]
        dy_ref[...] = dy.astype(BF16)
        _acc_add(i, dg_ref, _colsum(d * y_ref[...]))
        _acc_add(i, db_ref, _colsum(dy))

    return _rowcall(name, body, S, tm, [(dxo, ROW), (y, ROW), (gate, FULL)],
                    [((S, D), BF16, ROW), ((1, D), F32, ACC), ((1, D), F32, ACC)])


def _mod_bwd(name, dxo, dh, x, g, sc, tm=256):
    S, D = x.shape

    def body(d_ref, dh_ref, x_ref, g_ref, sc_ref, dx_ref, dsh_ref, dsc_ref, dg_ref):
        i = pl.program_id(0)
        xv, dh_v, gv = x_ref[...], dh_ref[...], g_ref[...]
        r = lax.rsqrt(jnp.mean(xv * xv, axis=-1, keepdims=True) + EPS)
        n = xv * r
        _acc_add(i, dsh_ref, _colsum(dh_v))
        _acc_add(i, dsc_ref, _colsum(dh_v * (n * gv)))
        dy = dh_v * (1.0 + sc_ref[...])
        _acc_add(i, dg_ref, _colsum(dy * n))
        dn = dy * gv
        dx_ref[...] = d_ref[...] + r * (dn - n * jnp.mean(dn * n, axis=-1, keepdims=True))

    return _rowcall(name, body, S, tm, [(dxo, ROW), (dh, ROW), (x, ROW), (g, FULL), (sc, FULL)],
                    [((S, D), F32, ROW), ((1, D), F32, ACC), ((1, D), F32, ACC), ((1, D), F32, ACC)])


HB16 = 16


def _conv3_val(ext, w, b, hb):
    return w[2:3] * ext[hb:] + w[1:2] * _shift_prev(ext, 1, hb) + w[0:1] * _shift_prev(ext, 2, hb) + b


def _glu3_fwd(name, u, w, b, tm=128):
    S, F2 = u.shape
    Fh = F2 // 2

    def body(ua_ref, ub_ref, ha_ref, hb_ref, w_ref, b_ref, o_ref):
        live = (pl.program_id(0) > 0).astype(F32)
        wv, bv = w_ref[...], b_ref[...]
        ea = jnp.concatenate([ha_ref[...].astype(F32) * live, ua_ref[...].astype(F32)], axis=0)
        eb = jnp.concatenate([hb_ref[...].astype(F32) * live, ub_ref[...].astype(F32)], axis=0)
        za = _conv3_val(ea, wv[:, :Fh], bv[:, :Fh], HB16)
        zb = _conv3_val(eb, wv[:, Fh:], bv[:, Fh:], HB16)
        o_ref[...] = (za * _sigmoid(za) * zb).astype(BF16)

    return _rowcall(name, body, S, tm,
                    [(u, ("row", Fh, 0)), (u, ("row", Fh, 1)), (u, ("prev", HB16, Fh, 0)), (u, ("prev", HB16, Fh, 1)),
                     (w, FULL), (b, FULL)],
                    [((S, Fh), BF16, ROW)])[0]


def _glu3_bwd(name, u, dhm, w, b, tm=128):
    S, F2 = u.shape
    Fh = F2 // 2

    def body(ua_ref, ub_ref, ha_ref, hb_ref, d_ref, w_ref, b_ref, dz_ref, acc_ref):
        i = pl.program_id(0)
        live = (i > 0).astype(F32)
        wv, bv, d = w_ref[...], b_ref[...], d_ref[...]
        ea = jnp.concatenate([ha_ref[...].astype(F32) * live, ua_ref[...].astype(F32)], axis=0)
        eb = jnp.concatenate([hb_ref[...].astype(F32) * live, ub_ref[...].astype(F32)], axis=0)
        za = _conv3_val(ea, wv[:, :Fh], bv[:, :Fh], HB16)
        zb = _conv3_val(eb, wv[:, Fh:], bv[:, Fh:], HB16)
        sg = _sigmoid(za)
        da = d * zb * (sg * (1.0 + za * (1.0 - sg)))
        db = d * (za * sg)
        dz_ref[:, :Fh] = da.astype(BF16)
        dz_ref[:, Fh:] = db.astype(BF16)
        for k in range(3):
            row = jnp.concatenate([_colsum(da * _shift_prev(ea, 2 - k, HB16)),
                                   _colsum(db * _shift_prev(eb, 2 - k, HB16))], axis=1)
            _acc_add(i, acc_ref, row, rows=(slice(k, k + 1), slice(None)))
        _acc_add(i, acc_ref, jnp.concatenate([_colsum(da), _colsum(db)], axis=1), rows=(slice(3, 4), slice(None)))

        @pl.when(i == 0)
        def _():
            acc_ref[4:8, :] = jnp.zeros((4, F2), F32)

    return _rowcall(name, body, S, tm,
                    [(u, ("row", Fh, 0)), (u, ("row", Fh, 1)), (u, ("prev", HB16, Fh, 0)), (u, ("prev", HB16, Fh, 1)),
                     (dhm, ROW), (w, FULL), (b, FULL)],
                    [((S, F2), BF16, ROW), ((8, F2), F32, ACC)])


def _conv3_bwd(name, dz, w, tm=128):
    S, F2 = dz.shape
    nblk = S // tm

    def body(d_ref, n_ref, w_ref, o_ref):
        live = (pl.program_id(0) < nblk - 1).astype(F32)
        ext = jnp.concatenate([d_ref[...].astype(F32), n_ref[...].astype(F32) * live], axis=0)
        wv = w_ref[...]
        o_ref[...] = (wv[2:3] * ext[:tm] + wv[1:2] * _shift_next(ext, 1, tm)
                      + wv[0:1] * _shift_next(ext, 2, tm)).astype(BF16)

    return _rowcall(name, body, S, tm, [(dz, ROW), (dz, ("next", HB16, None, 0)), (w, FULL)],
                    [((S, F2), BF16, ROW)])[0]


def _ffn_fwd(name, h, w_up, w_dn, dw_w, dw_b):
    u = _matmul(f"{name}_up", h, w_up, "nn", BF16)
    hm = _glu3_fwd(f"{name}_glu", u, dw_w, dw_b)
    f = _matmul(f"{name}_dn", hm, w_dn, "nn", F32, tk=w_dn.shape[0])
    return f, (u, hm)


def _ffn_bwd(name, dy, h, u, hm, w_up, w_dn, dw_w, dw_b):
    Fh = w_dn.shape[0]
    dhm = _matmul(f"{name}_ddn_x", dy, w_dn, "nt", F32, tn=Fh // 2)
    g_dn = _matmul(f"{name}_ddn_w", hm, dy, "tn", BF16, tm=Fh // 2, tn=512, tk=1024)
    dz, acc = _glu3_bwd(f"{name}_dglu", u, dhm, dw_w, dw_b)
    du = _conv3_bwd(f"{name}_dconv", dz, dw_w)
    g_up = _matmul(f"{name}_dup_w", h, du, "tn", BF16, tm=1024, tn=512, tk=1024)
    dh = _matmul(f"{name}_dup_x", du, w_up, "nt", F32, tn=1024, tk=512)
    return dh, dict(up=g_up, dn=g_dn, dw_w=acc[0:3], dw_b=acc[3:4])


HB32 = 32


def _glu31_fwd(name, p, w, b, tm=256):
    S, D2 = p.shape
    D = D2 // 2
    K = w.shape[0]

    def body(a_ref, g_ref, ha_ref, hg_ref, w_ref, b_ref, o_ref):
        live = (pl.program_id(0) > 0).astype(F32)
        y1 = a_ref[...] * _sigmoid(g_ref[...])
        yh = ha_ref[...] * _sigmoid(hg_ref[...]) * live
        ext = jnp.concatenate([yh, y1], axis=0)
        wv = w_ref[...]
        acc = b_ref[...] + wv[K - 1:K] * y1
        for k in range(K - 1):
            acc = acc + wv[k:k + 1] * _shift_prev(ext, K - 1 - k, HB32)
        o_ref[...] = acc

    return _rowcall(name, body, S, tm,
                    [(p, ("row", D, 0)), (p, ("row", D, 1)), (p, ("prev", HB32, D, 0)), (p, ("prev", HB32, D, 1)),
                     (w, FULL), (b, FULL)],
                    [((S, D), F32, ROW)])[0]


def _ln_silu_fwd(name, y2, g, b, tm=256):
    S, D = y2.shape

    def body(y_ref, g_ref, b_ref, o_ref):
        y = y_ref[...]
        mu = jnp.mean(y, axis=-1, keepdims=True)
        yc = y - mu
        rs = lax.rsqrt(jnp.mean(yc * yc, axis=-1, keepdims=True) + EPS)
        y3 = yc * rs * g_ref[...] + b_ref[...]
        o_ref[...] = (y3 * _sigmoid(y3)).astype(BF16)

    return _rowcall(name, body, S, tm, [(y2, ROW), (g, FULL), (b, FULL)], [((S, D), BF16, ROW)])[0]


def _ln_silu_bwd(name, y2, dy4, g, b, tm=256):
    S, D = y2.shape

    def body(y_ref, d_ref, g_ref, b_ref, o_ref, dg_ref, db_ref):
        i = pl.program_id(0)
        y, gv = y_ref[...], g_ref[...]
        mu = jnp.mean(y, axis=-1, keepdims=True)
        yc = y - mu
        rs = lax.rsqrt(jnp.mean(yc * yc, axis=-1, keepdims=True) + EPS)
        n = yc * rs
        y3 = n * gv + b_ref[...]
        sg = _sigmoid(y3)
        dy3 = d_ref[...] * (sg * (1.0 + y3 * (1.0 - sg)))
        _acc_add(i, db_ref, _colsum(dy3))
        _acc_add(i, dg_ref, _colsum(dy3 * n))
        dn = dy3 * gv
        o_ref[...] = rs * (dn - jnp.mean(dn, axis=-1, keepdims=True) - n * jnp.mean(dn * n, axis=-1, keepdims=True))

    return _rowcall(name, body, S, tm, [(y2, ROW), (dy4, ROW), (g, FULL), (b, FULL)],
                    [((S, D), F32, ROW), ((1, D), F32, ACC), ((1, D), F32, ACC)])


def _glu31_bwd(name, p, dy2, w, tm=256):
    S, D2 = p.shape
    D = D2 // 2
    K = w.shape[0]
    nblk = S // tm

    def body(a_ref, g_ref, ha_ref, hg_ref, d_ref, dn_ref, w_ref, dp_ref, dw_ref, dcb_ref, dpb_ref):
        i = pl.program_id(0)
        live_prev = (i > 0).astype(F32)
        live_next = (i < nblk - 1).astype(F32)
        a, sg, d, wv = a_ref[...], _sigmoid(g_ref[...]), d_ref[...], w_ref[...]
        y1 = a * sg
        ext_y = jnp.concatenate([ha_ref[...] * _sigmoid(hg_ref[...]) * live_prev, y1], axis=0)
        ext_d = jnp.concatenate([d, dn_ref[...] * live_next], axis=0)
        dy1 = wv[K - 1:K] * d
        for k in range(K - 1):
            dy1 = dy1 + wv[k:k + 1] * _shift_next(ext_d, K - 1 - k, tm)
        for k in range(K):
            _acc_add(i, dw_ref, _colsum(d * _shift_prev(ext_y, K - 1 - k, HB32)), rows=(slice(k, k + 1), slice(None)))

        @pl.when(i == 0)
        def _():
            dw_ref[K:, :] = jnp.zeros((dw_ref.shape[0] - K, D), F32)

        _acc_add(i, dcb_ref, _colsum(d))
        da = dy1 * sg
        dg = dy1 * a * sg * (1.0 - sg)
        dp_ref[:, :D] = da.astype(BF16)
        dp_ref[:, D:] = dg.astype(BF16)
        _acc_add(i, dpb_ref, jnp.concatenate([_colsum(da), _colsum(dg)], axis=1))

    return _rowcall(name, body, S, tm,
                    [(p, ("row", D, 0)), (p, ("row", D, 1)), (p, ("prev", HB32, D, 0)), (p, ("prev", HB32, D, 1)),
                     (dy2, ROW), (dy2, ("next", HB32, None, 0)), (w, FULL)],
                    [((S, D2), BF16, ROW), ((HB32, D), F32, ACC), ((1, D), F32, ACC), ((1, D2), F32, ACC)])


CHUNK = 128
A_GROUPS = 4


def _group_ln(gv):
    ns, rss = [], []
    for g in range(A_GROUPS):
        xg = gv[:, g * LANES:(g + 1) * LANES]
        xc = xg - jnp.mean(xg, axis=-1, keepdims=True)
        rs = lax.rsqrt(jnp.mean(xc * xc, axis=-1, keepdims=True) + EPS)
        ns.append(xc * rs)
        rss.append(jnp.broadcast_to(rs, xg.shape))
    return jnp.concatenate(ns, axis=1), jnp.concatenate(rss, axis=1)


def _tril_mask():
    r = lax.broadcasted_iota(jnp.int32, (CHUNK, CHUNK), 0)
    c = lax.broadcasted_iota(jnp.int32, (CHUNK, CHUNK), 1)
    return r >= c


def _spatial(ws_ref, x, dn):
    mask = _tril_mask()
    rows = []
    for ci in range(x.shape[0] // CHUNK):
        cols = []
        for g in range(A_GROUPS):
            wm = jnp.where(mask, ws_ref[g], 0.0).astype(BF16)
            xb = x[ci * CHUNK:(ci + 1) * CHUNK, g * LANES:(g + 1) * LANES]
            cols.append(lax.dot_general(wm, xb, dn, preferred_element_type=F32))
        rows.append(jnp.concatenate(cols, axis=1))
    return jnp.concatenate(rows, axis=0)


def _mixa_fwd(name, z, vg, ws, bias_full, tm=256):
    S = z.shape[0]
    W = A_GROUPS * LANES

    def body(u_ref, v_ref, vg_ref, ws_ref, b_ref, o_ref):
        nh, _ = _group_ln(_gelu(v_ref[...]))
        vn = (nh * vg_ref[...]).astype(BF16)
        f = _spatial(ws_ref, vn, _DN["nn"]) + jnp.concatenate([b_ref[...]] * (tm // CHUNK), axis=0)
        o_ref[...] = (_gelu(u_ref[...]) * f).astype(BF16)

    return _rowcall(name, body, S, tm,
                    [(z, ("row", W, 0)), (z, ("row", W, 1)), (vg, FULL), (ws, FULL), (bias_full, FULL)],
                    [((S, W), BF16, ROW)])[0]


def _mixa_bwd(name, z, dyab, vg, ws, bias_full, tm=256):
    S = z.shape[0]
    W = A_GROUPS * LANES
    nch = tm // CHUNK

    def body(u_ref, v_ref, d_ref, vg_ref, ws_ref, b_ref, dz_ref, dws_ref, dbf_ref, dvg_ref):
        i = pl.program_id(0)
        u, v, d, vgv = u_ref[...], v_ref[...], d_ref[...], vg_ref[...]
        nh, rs = _group_ln(_gelu(v))
        vn = (nh * vgv).astype(BF16)
        f = _spatial(ws_ref, vn, _DN["nn"]) + jnp.concatenate([b_ref[...]] * nch, axis=0)
        dz_ref[:, :W] = (d * f * _gelu_grad(u)).astype(BF16)
        df = d * _gelu(u)
        dbf = df[0:CHUNK]
        for ci in range(1, nch):
            dbf = dbf + df[ci * CHUNK:(ci + 1) * CHUNK]
        _acc_add(i, dbf_ref, dbf)
        dfb = df.astype(BF16)
        mask = _tril_mask()
        for g in range(A_GROUPS):
            acc = jnp.zeros((CHUNK, CHUNK), F32)
            for ci in range(nch):
                blk = (slice(ci * CHUNK, (ci + 1) * CHUNK), slice(g * LANES, (g + 1) * LANES))
                acc = acc + lax.dot_general(dfb[blk], vn[blk], _DN["nt"], preferred_element_type=F32)
            _acc_add(i, dws_ref, jnp.where(mask, acc, 0.0)[None], rows=(slice(g, g + 1), slice(None), slice(None)))
        dvn = _spatial(ws_ref, dfb, _DN["tn"])
        _acc_add(i, dvg_ref, _colsum(dvn * nh))
        dnh = dvn * vgv
        parts = []
        for g in range(A_GROUPS):
            cs = slice(g * LANES, (g + 1) * LANES)
            dg_, ng = dnh[:, cs], nh[:, cs]
            parts.append(dg_ - jnp.mean(dg_, axis=-1, keepdims=True) - ng * jnp.mean(dg_ * ng, axis=-1, keepdims=True))
        dz_ref[:, W:] = (rs * jnp.concatenate(parts, axis=1) * _gelu_grad(v)).astype(BF16)

    return _rowcall(name, body, S, tm,
                    [(z, ("row", W, 0)), (z, ("row", W, 1)), (dyab, ("row", W, 0)), (vg, FULL), (ws, FULL),
                     (bias_full, FULL)],
                    [((S, 2 * W), BF16, ROW), ((A_GROUPS, CHUNK, CHUNK), F32, ACC), ((CHUNK, W), F32, ACC),
                     ((1, W), F32, ACC)])


HEAD = 64
N_HEADS = 8
BW = HEAD * N_HEADS
QB = 128
DILATIONS = (1, 4, 16)
QK_SCALE = HEAD ** -0.5


def _gsum64(x, ones_bd):
    x1 = x.astype(BF16)
    r1 = x - x1.astype(F32)
    x2 = r1.astype(BF16)
    x3 = (r1 - x2.astype(F32)).astype(BF16)
    dot = lambda t: jnp.dot(t, ones_bd, preferred_element_type=F32)
    return dot(x1) + dot(x2) + dot(x3)


def _swap32(x):
    n = x.shape[-1]
    up = pltpu.roll(x, n - HEAD // 2, axis=1)
    dn = pltpu.roll(x, HEAD // 2, axis=1)
    lane = lax.broadcasted_iota(jnp.int32, x.shape, 1)
    return jnp.where((lane % HEAD) < HEAD // 2, up, dn)


def _tile4(t):
    return jnp.concatenate([t] * (BW // LANES), axis=1)


def _qkv_fwd(name, z, cos, sin, ones_bd, qg, kg, tm=256):
    S = z.shape[0]

    def body(q_ref, k_ref, v_ref, c_ref, s_ref, o_ref, qg_ref, kg_ref, qo_ref, ko_ref, vo_ref):
        c, s, ob = _tile4(c_ref[...]), _tile4(s_ref[...]), o_ref[...]

        def norm_rope(x, g):
            r = lax.rsqrt(_gsum64(x * x, ob) * (1.0 / HEAD) + EPS)
            xn = x * r * g
            return xn * c + _swap32(xn) * s

        qo_ref[...] = (norm_rope(q_ref[...], qg_ref[...]) * QK_SCALE).astype(BF16)
        ko_ref[...] = norm_rope(k_ref[...], kg_ref[...]).astype(BF16)
        vo_ref[...] = v_ref[...].astype(BF16)

    return _rowcall(name, body, S, tm,
                    [(z, ("row", BW, 2)), (z, ("row", BW, 3)), (z, ("row", BW, 4)), (cos, ROW), (sin, ROW),
                     (ones_bd, FULL), (qg, FULL), (kg, FULL)],
                    [((S, BW), BF16, ROW)] * 3)


def _scores(q_ref, kp_ref, kc_ref, h, i, seg_blocks):
    hs = slice(h * HEAD, (h + 1) * HEAD)
    qh = q_ref[:, hs]
    kcat = jnp.concatenate([kp_ref[:, hs], kc_ref[:, hs]], axis=0)
    s = lax.dot_general(qh, kcat, _DN["nt"], preferred_element_type=F32)
    qi = lax.broadcasted_iota(jnp.int32, (QB, 2 * QB), 0)
    kj = lax.broadcasted_iota(jnp.int32, (QB, 2 * QB), 1)
    has_prev = (i % seg_blocks) != 0
    valid = (kj >= qi) & (kj <= qi + QB) & ((kj >= QB) | has_prev)
    return qh, kcat, s, valid


def _attn_fwd(name, q, k, v, seg_len):
    S = q.shape[0]
    seg_blocks = seg_len // QB
    PREV = ("off", -1, None, 0)

    def body(q_ref, kp_ref, kc_ref, vp_ref, vc_ref, o_ref, l_ref):
        i = pl.program_id(0)
        outs, lses = [], []
        for h in range(N_HEADS):
            hs = slice(h * HEAD, (h + 1) * HEAD)
            _, _, s, valid = _scores(q_ref, kp_ref, kc_ref, h, i, seg_blocks)
            s = jnp.where(valid, s, NEG)
            m = jnp.max(s, axis=-1, keepdims=True)
            p = jnp.exp(s - m)
            den = jnp.sum(p, axis=-1, keepdims=True)
            vcat = jnp.concatenate([vp_ref[:, hs], vc_ref[:, hs]], axis=0)
            outs.append(jnp.dot(p.astype(BF16), vcat, preferred_element_type=F32) / den)
            lses.append(jnp.broadcast_to(m + jnp.log(den), (QB, HEAD)))
        o_ref[...] = jnp.concatenate(outs, axis=1)
        l_ref[...] = jnp.concatenate(lses, axis=1)

    return _rowcall(name, body, S, QB, [(q, ROW), (k, PREV), (k, ROW), (v, PREV), (v, ROW)],
                    [((S, BW), F32, ROW)] * 2)


def _attn_bwd(name, q, k, v, do, lse, delta, seg_len):
    S = q.shape[0]
    seg_blocks = seg_len // QB
    PREV = ("off", -1, None, 0)

    def body(q_ref, kp_ref, kc_ref, vp_ref, vc_ref, do_ref, l_ref, dl_ref, dq_ref, dkc_ref, dkp_ref, dvc_ref, dvp_ref):
        i = pl.program_id(0)
        dq, dkc, dkp, dvc, dvp = [], [], [], [], []
        for h in range(N_HEADS):
            hs = slice(h * HEAD, (h + 1) * HEAD)
            qh, kcat, s, valid = _scores(q_ref, kp_ref, kc_ref, h, i, seg_blocks)
            p = jnp.where(valid, jnp.exp(s - l_ref[:, h * HEAD:h * HEAD + 1]), 0.0)
            doh = do_ref[:, hs].astype(BF16)
            vcat = jnp.concatenate([vp_ref[:, hs], vc_ref[:, hs]], axis=0)
            dp = lax.dot_general(doh, vcat, _DN["nt"], preferred_element_type=F32)
            ds = (p * (dp - dl_ref[:, h * HEAD:h * HEAD + 1])).astype(BF16)
            dq.append(jnp.dot(ds, kcat, preferred_element_type=F32))
            dk = lax.dot_general(ds, qh, _DN["tn"], preferred_element_type=F32)
            dv = lax.dot_general(p.astype(BF16), doh, _DN["tn"], preferred_element_type=F32)
            dkp.append(dk[:QB])
            dkc.append(dk[QB:])
            dvp.append(dv[:QB])
            dvc.append(dv[QB:])
        dq_ref[...] = jnp.concatenate(dq, axis=1)
        dkc_ref[...] = jnp.concatenate(dkc, axis=1)
        dkp_ref[...] = jnp.concatenate(dkp, axis=1)
        dvc_ref[...] = jnp.concatenate(dvc, axis=1)
        dvp_ref[...] = jnp.concatenate(dvp, axis=1)

    return _rowcall(name, body, S, QB,
                    [(q, ROW), (k, PREV), (k, ROW), (v, PREV), (v, ROW), (do, ROW), (lse, ROW), (delta, ROW)],
                    [((S, BW), F32, ROW)] * 5)


def _merge_fwd(name, branches, tm=256):
    S = branches[0][0].shape[0]

    def body(o1, l1, o2, l2, o3, l3, y_ref, yb_ref, l_ref):
        ls = [l1[...], l2[...], l3[...]]
        m = jnp.maximum(jnp.maximum(ls[0], ls[1]), ls[2])
        es = [jnp.exp(l - m) for l in ls]
        den = es[0] + es[1] + es[2]
        y = (es[0] * o1[...] + es[1] * o2[...] + es[2] * o3[...]) / den
        y_ref[...] = y
        yb_ref[...] = y.astype(BF16)
        l_ref[...] = m + jnp.log(den)

    ins = [(t, ROW) for pair in branches for t in pair]
    return _rowcall(name, body, S, tm, ins, [((S, BW), F32, ROW), ((S, BW), BF16, ROW), ((S, BW), F32, ROW)])


def _delta(name, dyab, yb, ones_bd, tm=256):
    S = yb.shape[0]

    def body(d_ref, y_ref, o_ref, out_ref):
        out_ref[...] = _gsum64(d_ref[...] * y_ref[...], o_ref[...])

    return _rowcall(name, body, S, tm, [(dyab, ("row", BW, 1)), (yb, ROW), (ones_bd, FULL)], [((S, BW), F32, ROW)])[0]


def _qkv_bwd(name, z, cos, sin, ones_bd, qg, kg, pieces):
    S = z.shape[0]
    nblk = S // QB

    def body(q_ref, k_ref, c_ref, s_ref, o_ref, qg_ref, kg_ref, *rest):
        pr, (dz_ref, dqg_ref, dkg_ref) = rest[:15], rest[15:]
        i = pl.program_id(0)
        c, s, ob = _tile4(c_ref[...]), _tile4(s_ref[...]), o_ref[...]
        dq = dk = dv = None
        for b, dil in enumerate(DILATIONS):
            a_q, a_kc, a_kp, a_vc, a_vp = [r[...] for r in pr[5 * b:5 * b + 5]]
            live = ((i + dil) < nblk).astype(F32)
            tq, tk, tv = a_q, a_kc + a_kp * live, a_vc + a_vp * live
            dq, dk, dv = (tq, tk, tv) if b == 0 else (dq + tq, dk + tk, dv + tv)

        def back(x, g, d_rot, acc_ref):
            r = lax.rsqrt(_gsum64(x * x, ob) * (1.0 / HEAD) + EPS)
            n = x * r
            dxn = d_rot * c + _swap32(d_rot * s)
            _acc_add(i, acc_ref, _colsum(dxn * n))
            dn = dxn * g
            return r * (dn - n * (_gsum64(dn * n, ob) * (1.0 / HEAD)))

        dz_ref[:, :BW] = back(q_ref[...], qg_ref[...], dq * QK_SCALE, dqg_ref).astype(BF16)
        dz_ref[:, BW:2 * BW] = back(k_ref[...], kg_ref[...], dk, dkg_ref).astype(BF16)
        dz_ref[:, 2 * BW:] = dv.astype(BF16)

    ins = [(z, ("row", BW, 2)), (z, ("row", BW, 3)), (cos, ROW), (sin, ROW), (ones_bd, FULL), (qg, FULL), (kg, FULL)]
    for (a_q, a_kc, a_kp, a_vc, a_vp), dil in zip(pieces, DILATIONS):
        ins += [(a_q, ROW), (a_kc, ROW), (a_kp, ("off", dil, None, 0)), (a_vc, ROW), (a_vp, ("off", dil, None, 0))]
    return _rowcall(name, body, S, QB, ins,
                    [((S, 3 * BW), BF16, ROW), ((1, BW), F32, ACC), ((1, BW), F32, ACC)])


def _to_sub(t, dil):
    S, W = t.shape
    return t if dil == 1 else t.reshape(S // dil, dil, W).transpose(1, 0, 2).reshape(S, W)


def _from_sub(t, dil):
    S, W = t.shape
    return t if dil == 1 else t.reshape(dil, S // dil, W).transpose(1, 0, 2).reshape(S, W)


def _local_step(x0, tgt, pos, mod, wb, sp):
    S, D = x0.shape
    md = lambda l, j: mod[l, j:j + 1]
    sh_m, sc_m, g_m, sh_f, sc_f, g_f = ([md(l, j) for l in range(2)] for j in range(6))
    nm_g, nf_g = sp["norm_mix_g"], sp["norm_ffn_g"]

    inv_freq = 1.0 / (ROPE_THETA ** (jnp.arange(0, HEAD, 2, dtype=F32) / HEAD))
    ang = pos.astype(F32)[:, None] * inv_freq
    cs, sn = jnp.cos(ang), jnp.sin(ang)
    cos = jnp.concatenate([cs, cs, cs, cs], axis=1)
    sin = jnp.concatenate([-sn, sn, -sn, sn], axis=1)
    head_of = jnp.arange(BW) // HEAD
    ones_bd = (head_of[:, None] == head_of[None, :]).astype(BF16)
    qg = jnp.tile(sp["b_q_norm_g"].reshape(1, HEAD), (1, N_HEADS))
    kg = jnp.tile(sp["b_k_norm_g"].reshape(1, HEAD), (1, N_HEADS))
    vg = sp["a_vnorm_g"].reshape(1, A_GROUPS * LANES)
    ws = sp["a_spatial_w"][0]
    bias_full = jnp.repeat(sp["a_spatial_b"][0].T, LANES, axis=1)
    ffn_w = [(wb["up"][l], wb["dn"][l], sp["ffn_dw_w"][l], sp["ffn_dw_b"][l:l + 1]) for l in range(2)]

    h0 = _mod_first("l0_mod", x0, nm_g[0:1], sc_m[0], sh_m[0])
    z = _matmul("l0_in", h0, wb["w_in"], "nn", F32)
    ya = _mixa_fwd("l0_mixa", z, vg, ws, bias_full)
    qkv = _qkv_fwd("l0_qkv", z, cos, sin, ones_bd, qg, kg)
    qkv_sub, branches = [], []
    for dil in DILATIONS:
        sub = [_to_sub(t, dil) for t in qkv]
        o, l = _attn_fwd(f"l0_att{dil}", *sub, S // dil)
        qkv_sub.append(sub)
        branches.append((_from_sub(o, dil), _from_sub(l, dil)))
    yb, yb16, lse = _merge_fwd("l0_merge", branches)
    yab = jnp.concatenate([ya, yb16], axis=1)
    y0 = _matmul("l0_out", yab, wb["w_out"], "nn", F32)
    x1, h1 = _resid_mod("l0_res1", x0, y0, g_m[0], nf_g[0:1], sc_f[0], sh_f[0])
    f0, (u0, hm0) = _ffn_fwd("l0_ffn", h1, *ffn_w[0])
    x2, h2 = _resid_mod("l0_res2", x1, f0, g_f[0], nm_g[1:2], sc_m[1], sh_m[1])
    p = _matmul("l1_pw1", h2, wb["pw1"], "nn", F32, bias=sp["conv_pw1_b"])
    y2 = _glu31_fwd("l1_glu", p, sp["conv_dw_w"][0], sp["conv_dw_b"])
    y4 = _ln_silu_fwd("l1_ln", y2, sp["conv_ln_g"], sp["conv_ln_b"])
    y1 = _matmul("l1_pw2", y4, wb["pw2"], "nn", F32, bias=sp["conv_pw2_b"])
    x3, h3 = _resid_mod("l1_res1", x2, y1, g_m[1], nf_g[1:2], sc_f[1], sh_f[1])
    f1, (u1, hm1) = _ffn_fwd("l1_ffn", h3, *ffn_w[1])
    dx4, lossv = _loss_head("loss", x3, f1, g_f[1], tgt)

    dy, dgate_f1, _ = _gate_bwd("l1_gate2", dx4, f1, g_f[1])
    dh, gf1 = _ffn_bwd("l1_ffn", dy, h3, u1, hm1, *ffn_w[1])
    dx3, dsh_f1, dsc_f1, dnf1 = _mod_bwd("l1_dmod2", dx4, dh, x3, nf_g[1:2], sc_f[1])
    dy, dgate_m1, dpw2_b = _gate_bwd("l1_gate1", dx3, y1, g_m[1])
    dy4 = _matmul("l1_dpw2_x", dy, wb["pw2"], "nt", F32)
    g_pw2 = _matmul("l1_dpw2_w", y4, dy, "tn", BF16, tm=1024)
    dy2, dln_g, dln_b = _ln_silu_bwd("l1_dln", y2, dy4, sp["conv_ln_g"], sp["conv_ln_b"])
    dp, ddw_w, ddw_b, dpw1_b = _glu31_bwd("l1_dglu", p, dy2, sp["conv_dw_w"][0])
    g_pw1 = _matmul("l1_dpw1_w", h2, dp, "tn", BF16, tm=1024)
    dh = _matmul("l1_dpw1_x", dp, wb["pw1"], "nt", F32, tn=1024)
    dx2, dsh_m1, dsc_m1, dnm1 = _mod_bwd("l1_dmod1", dx3, dh, x2, nm_g[1:2], sc_m[1])
    dy, dgate_f0, _ = _gate_bwd("l0_gate2", dx2, f0, g_f[0])
    dh, gf0 = _ffn_bwd("l0_ffn", dy, h1, u0, hm0, *ffn_w[0])
    dx1, dsh_f0, dsc_f0, dnf0 = _mod_bwd("l0_dmod2", dx2, dh, x1, nf_g[0:1], sc_f[0])
    dy, dgate_m0, _ = _gate_bwd("l0_gate1", dx1, y0, g_m[0])
    dyab = _matmul("l0_dout_x", dy, wb["w_out"], "nt", F32)
    g_out = _matmul("l0_dout_w", yab, dy, "tn", BF16, tm=1024)
    dza, dws, dbf, dvg = _mixa_bwd("l0_dmixa", z, dyab, vg, ws, bias_full)
    delta = _delta("l0_delta", dyab, yb, ones_bd)
    dyb = dyab[:, BW:]
    pieces = []
    for dil, sub in zip(DILATIONS, qkv_sub):
        outs = _attn_bwd(f"l0_datt{dil}", *sub, _to_sub(dyb, dil), _to_sub(lse, dil), _to_sub(delta, dil), S // dil)
        pieces.append([_from_sub(o, dil) for o in outs])
    dzb, dqg, dkg = _qkv_bwd("l0_dqkv", z, cos, sin, ones_bd, qg, kg, pieces)
    dz = jnp.concatenate([dza, dzb], axis=1)
    g_in = _matmul("l0_din_w", h0, dz, "tn", BF16, tm=1024)
    dh = _matmul("l0_din_x", dz, wb["w_in"], "nt", F32, tn=1024, tk=512)
    grad_x, dsh_m0, dsc_m0, dnm0 = _mod_bwd("l0_dmod1", dx1, dh, x0, nm_g[0:1], sc_m[0])

    dmod = jnp.stack([jnp.concatenate([dsh_m0, dsc_m0, dgate_m0, dsh_f0, dsc_f0, dgate_f0], axis=0),
                      jnp.concatenate([dsh_m1, dsc_m1, dgate_m1, dsh_f1, dsc_f1, dgate_f1], axis=0)])
    big = dict(w_in=g_in, w_out=g_out, pw1=g_pw1, pw2=g_pw2,
               up=jnp.stack([gf0["up"], gf1["up"]]), dn=jnp.stack([gf0["dn"], gf1["dn"]]))
    small = dict(
        norm_mix_g=jnp.concatenate([dnm0, dnm1], axis=0),
        norm_ffn_g=jnp.concatenate([dnf0, dnf1], axis=0),
        a_vnorm_g=dvg.reshape(1, A_GROUPS, LANES),
        a_spatial_w=dws[None],
        a_spatial_b=dbf.reshape(CHUNK, A_GROUPS, LANES).sum(-1).T[None],
        b_q_norm_g=dqg.reshape(N_HEADS, HEAD).sum(0)[None],
        b_k_norm_g=dkg.reshape(N_HEADS, HEAD).sum(0)[None],
        conv_pw1_b=dpw1_b, conv_dw_w=ddw_w[None, :sp["conv_dw_w"].shape[1]], conv_dw_b=ddw_b,
        conv_ln_g=dln_g, conv_ln_b=dln_b, conv_pw2_b=dpw2_b,
        ffn_dw_w=jnp.stack([gf0["dw_w"], gf1["dw_w"]]),
        ffn_dw_b=jnp.concatenate([gf0["dw_b"], gf1["dw_b"]], axis=0),
    )
    return lossv, grad_x, big, small, dmod


ADA_TN = 512


def _ada_fwd(name, c_all, ada_w, ada_b_sh):
    L, D, N = ada_w.shape
    B = c_all.shape[0]

    def body(c_ref, w_ref, b_ref, o_ref):
        cv = c_ref[...]
        ca = (cv * _sigmoid(cv)).astype(BF16)
        o_ref[0] = jnp.dot(ca, w_ref[0].astype(BF16), preferred_element_type=F32) + b_ref[0]

    return pl.pallas_call(
        body, name=name, grid=(L, N // ADA_TN),
        in_specs=[pl.BlockSpec((B, D), lambda l, j: (0, 0)), pl.BlockSpec((1, D, ADA_TN), lambda l, j: (l, 0, j)),
                  pl.BlockSpec((1, 1, ADA_TN), lambda l, j: (l, 0, j))],
        out_specs=pl.BlockSpec((1, B, ADA_TN), lambda l, j: (l, 0, j)),
        out_shape=jax.ShapeDtypeStruct((L, B, N), F32),
        compiler_params=_params(("parallel", "parallel")),
    )(c_all, ada_w, ada_b_sh.reshape(L, 1, N))


def _adamw_val(w, g, m, v):
    m2 = ADAM_B1 * m + (1.0 - ADAM_B1) * g
    v2 = ADAM_B2 * v + (1.0 - ADAM_B2) * (g * g)
    m_hat = m2 / (1.0 - ADAM_B1 ** ADAM_STEP)
    v_hat = v2 / (1.0 - ADAM_B2 ** ADAM_STEP)
    delta = -ADAM_LR * (m_hat / (jnp.sqrt(v_hat) + ADAM_EPS) + ADAM_WD * w)
    return delta, m2, v2


def _ada_update(name, c_all, dmod_sh, w, m, v):
    L, D, N = w.shape
    B = c_all.shape[0]

    def body(c_ref, d_ref, w_ref, m_ref, v_ref, g_ref, dl_ref, mo_ref, vo_ref):
        cv = c_ref[...]
        ca = (cv * _sigmoid(cv)).astype(BF16)
        g = lax.dot_general(ca, d_ref[0].astype(BF16), _DN["tn"], preferred_element_type=F32)
        g_ref[0] = g
        dl_ref[0], mo_ref[0], vo_ref[0] = _adamw_val(w_ref[0], g, m_ref[0], v_ref[0])

    wspec = pl.BlockSpec((1, D, ADA_TN), lambda l, j: (l, 0, j))
    return pl.pallas_call(
        body, name=name, grid=(L, N // ADA_TN),
        in_specs=[pl.BlockSpec((B, D), lambda l, j: (0, 0)), pl.BlockSpec((1, B, ADA_TN), lambda l, j: (l, 0, j)),
                  wspec, wspec, wspec],
        out_specs=[wspec] * 4, out_shape=[jax.ShapeDtypeStruct((L, D, N), F32)] * 4,
        compiler_params=_params(("parallel", "parallel")),
    )(c_all, dmod_sh, w, m, v)


def _adamw(name, w, g, m, v):
    R, C = w.shape
    tm = R
    for cand in (256, 128, 64, 32, 16, 8):
        if R % cand == 0 and cand * C * 4 <= (1 << 20):
            tm = cand
            break

    def body(w_ref, g_ref, m_ref, v_ref, d_ref, mo_ref, vo_ref):
        d_ref[...], mo_ref[...], vo_ref[...] = _adamw_val(w_ref[...], g_ref[...], m_ref[...], v_ref[...])

    return _rowcall(name, body, R, tm, [(w, ROW), (g, ROW), (m, ROW), (v, ROW)], [((R, C), F32, ROW)] * 3)


def _cast_bf16(name, a):
    H, R, C = a.shape
    a2 = a.reshape(H * R, C)
    tm = 256 if (H * R) % 256 == 0 else 64

    def body(a_ref, o_ref):
        o_ref[...] = a_ref[...].astype(BF16)

    return _rowcall(name, body, H * R, tm, [(a2, ROW)], [((H * R, C), BF16, ROW)])[0].reshape(H, R, C)


def _add2(name, a, b):
    R, C = a.shape
    tm = 256 if R % 256 == 0 else 64

    def body(a_ref, b_ref, o_ref):
        o_ref[...] = (a_ref[...].astype(F32) + b_ref[...].astype(F32)).astype(BF16)

    return _rowcall(name, body, R, tm, [(a, ROW), (b, ROW)], [((R, C), BF16, ROW)])[0]


def _add4(name, own, rcv):
    R, C = own.shape
    tm = 256 if R % 256 == 0 else 64

    def body(a_ref, r_ref, o_ref):
        acc = a_ref[...].astype(F32)
        for j in range(3):
            acc = acc + r_ref[j].astype(F32)
        o_ref[...] = acc

    return pl.pallas_call(
        body, name=name, grid=(R // tm,),
        in_specs=[pl.BlockSpec((tm, C), lambda i: (i, 0)), pl.BlockSpec((3, tm, C), lambda i: (0, i, 0))],
        out_specs=pl.BlockSpec((tm, C), lambda i: (i, 0)), out_shape=jax.ShapeDtypeStruct((R, C), F32),
        compiler_params=_params(("parallel",)),
    )(own, rcv)


def _sum8(name, gathered):
    R, N = gathered.shape
    P = R // 8

    def body(g_ref, o_ref):
        acc = g_ref[0:P, :]
        for d in range(1, 8):
            acc = acc + g_ref[d * P:(d + 1) * P, :]
        o_ref[...] = acc

    return pl.pallas_call(body, name=name, out_shape=jax.ShapeDtypeStruct((P, N), F32),
                          compiler_params=pltpu.CompilerParams(vmem_limit_bytes=VMEM_LIMIT))(gathered)


ANY = pl.BlockSpec(memory_space=pl.ANY)


def _mesh_pos():
    x, y, c = lax.axis_index("x"), lax.axis_index("y"), lax.axis_index("c")
    other_chips = [(1 - x, y), (x, 1 - y), (1 - x, 1 - y)]
    return x, y, c, other_chips


def _allgather8(name, blk):
    m_per, n = blk.shape

    def body(x_ref, out_ref, send_sems, recv_sems, local_sem):
        x, y, c, chips = _mesh_pos()
        me, sibling = (x, y, c), (x, y, 1 - c)

        def rows(px, py, pc):
            return out_ref.at[pl.ds((4 * px + 2 * py + pc) * m_per, m_per), :]

        def copy(k, block, to, src=None):
            return pltpu.make_async_remote_copy(
                src_ref=rows(*block) if src is None else src, dst_ref=rows(*block),
                send_sem=send_sems.at[k], recv_sem=recv_sems.at[k], device_id=to, device_id_type=MESH)

        mine = pltpu.make_async_copy(x_ref, rows(*me), local_sem)
        mine.start()
        first = [copy(0, me, sibling, src=x_ref)]
        first += [copy(1 + j, me, (*chip, c), src=x_ref) for j, chip in enumerate(chips)]
        for cp in first:
            cp.start()
        passed = [copy(4 + j, (*chip, c), sibling) for j, chip in enumerate(chips)]
        for j, chip in enumerate(chips):
            copy(1 + j, (*chip, c), me).wait_recv()
            passed[j].start()
        copy(0, sibling, me).wait_recv()
        for j, chip in enumerate(chips):
            copy(4 + j, (*chip, 1 - c), me).wait_recv()
        for cp in first + passed:
            cp.wait_send()
        mine.wait()

    return pl.pallas_call(
        body, name=name, out_shape=jax.ShapeDtypeStruct((8 * m_per, n), blk.dtype),
        in_specs=[pl.BlockSpec(memory_space=pltpu.VMEM)], out_specs=pl.BlockSpec(memory_space=pltpu.VMEM),
        scratch_shapes=[pltpu.SemaphoreType.DMA((7,)), pltpu.SemaphoreType.DMA((7,)), pltpu.SemaphoreType.DMA],
        compiler_params=pltpu.CompilerParams(vmem_limit_bytes=VMEM_LIMIT),
    )(blk)


BIG = (("w_in", "col"), ("pw1", "col"), ("up", "col"), ("wo2", "row"), ("dn", "row"))


def _region(kind, ref, q, n, h=None):
    lead = () if h is None else (h,)
    if kind == "col":
        return ref.at[(*lead, slice(None), pl.ds(q * n, n))]
    return ref.at[(*lead, pl.ds(q * n, n), slice(None))]


def _gather_weights(shards):
    na = len(BIG)
    sizes = [s.shape[2] if kind == "col" else s.shape[1] for s, (_, kind) in zip(shards, BIG)]
    full_shapes = [(2, s.shape[1], 4 * s.shape[2]) if kind == "col" else (2, 4 * s.shape[1], s.shape[2])
                   for s, (_, kind) in zip(shards, BIG)]

    def body(*refs):
        sh, full = refs[:na], refs[na:2 * na]
        send_sems, recv_sems, local_sems = refs[2 * na:]
        x, y, c, chips = _mesh_pos()
        q = 2 * x + y
        sibling = (x, y, 1 - c)
        local, sends = [], []
        for a, (_, kind) in enumerate(BIG):
            for h in range(2):
                cp = pltpu.make_async_copy(sh[a].at[h], _region(kind, full[a], q, sizes[a], h), local_sems.at[a, h])
                cp.start()
                local.append(cp)

        def remote(a, k, src, dst, to):
            return pltpu.make_async_remote_copy(src_ref=src, dst_ref=dst, send_sem=send_sems.at[a, k],
                                                recv_sem=recv_sems.at[a, k], device_id=to, device_id_type=MESH)

        for a, (_, kind) in enumerate(BIG):
            for j, chip in enumerate(chips):
                cp = remote(a, j, sh[a].at[c], _region(kind, full[a], q, sizes[a], c), (*chip, c))
                cp.start()
                sends.append(cp)
        for a, (_, kind) in enumerate(BIG):
            for j, chip in enumerate(chips):
                got = _region(kind, full[a], 2 * chip[0] + chip[1], sizes[a], c)
                remote(a, j, got, got, (*chip, c)).wait_recv()
                cp = remote(a, 3 + j, got, got, sibling)
                cp.start()
                sends.append(cp)
        for a, (_, kind) in enumerate(BIG):
            for j, chip in enumerate(chips):
                got = _region(kind, full[a], 2 * chip[0] + chip[1], sizes[a], 1 - c)
                remote(a, 3 + j, got, got, sibling).wait_recv()
        for cp in sends:
            cp.wait_send()
        for cp in local:
            cp.wait()

    return pl.pallas_call(
        body, name="gather_weights", in_specs=[ANY] * na, out_specs=[ANY] * na,
        out_shape=[jax.ShapeDtypeStruct(s, BF16) for s in full_shapes],
        scratch_shapes=[pltpu.SemaphoreType.DMA((na, 6)), pltpu.SemaphoreType.DMA((na, 6)),
                        pltpu.SemaphoreType.DMA((na, 2))],
    )(*shards)


def _swap_halves(grads):
    na = len(grads)

    def body(*refs):
        g, t = refs[:na], refs[na:2 * na]
        send_sems, recv_sems = refs[2 * na:]
        x, y, c, _ = _mesh_pos()
        cps = [pltpu.make_async_remote_copy(src_ref=g[a].at[1 - c], dst_ref=t[a], send_sem=send_sems.at[a],
                                            recv_sem=recv_sems.at[a], device_id=(x, y, 1 - c), device_id_type=MESH)
               for a in range(na)]
        for cp in cps:
            cp.start()
        for cp in cps:
            cp.wait()

    return pl.pallas_call(
        body, name="grad_swap_halves", in_specs=[ANY] * na, out_specs=[ANY] * na,
        out_shape=[jax.ShapeDtypeStruct(g.shape[1:], BF16) for g in grads],
        scratch_shapes=[pltpu.SemaphoreType.DMA((na,)), pltpu.SemaphoreType.DMA((na,))],
    )(*grads)


def _scatter_regions(slabs, sizes):
    na = len(slabs)
    reg_shapes = [(s.shape[0], n) if kind == "col" else (n, s.shape[1]) for s, n, (_, kind) in zip(slabs, sizes, BIG)]

    def body(*refs):
        s, r = refs[:na], refs[na:2 * na]
        send_sems, recv_sems = refs[2 * na:]
        x, y, c, chips = _mesh_pos()
        cps = []
        for a, (_, kind) in enumerate(BIG):
            for j, chip in enumerate(chips):
                cps.append(pltpu.make_async_remote_copy(
                    src_ref=_region(kind, s[a], 2 * chip[0] + chip[1], sizes[a]), dst_ref=r[a].at[j],
                    send_sem=send_sems.at[a, j], recv_sem=recv_sems.at[a, j], device_id=(*chip, c), device_id_type=MESH))
        for cp in cps:
            cp.start()
        for cp in cps:
            cp.wait()

    return pl.pallas_call(
        body, name="grad_scatter", in_specs=[ANY] * na, out_specs=[ANY] * na,
        out_shape=[jax.ShapeDtypeStruct((3, *rs), BF16) for rs in reg_shapes],
        scratch_shapes=[pltpu.SemaphoreType.DMA((na, 3)), pltpu.SemaphoreType.DMA((na, 3))],
    )(*slabs)


def _join_halves(parts):
    na = len(parts)

    def body(*refs):
        p, o = refs[:na], refs[na:2 * na]
        send_sems, recv_sems, local_sems = refs[2 * na:]
        x, y, c, _ = _mesh_pos()
        loc = [pltpu.make_async_copy(p[a], o[a].at[c], local_sems.at[a]) for a in range(na)]
        cps = [pltpu.make_async_remote_copy(src_ref=p[a], dst_ref=o[a].at[c], send_sem=send_sems.at[a],
                                            recv_sem=recv_sems.at[a], device_id=(x, y, 1 - c), device_id_type=MESH)
               for a in range(na)]
        for cp in loc + cps:
            cp.start()
        for a in range(na):
            pltpu.make_async_remote_copy(src_ref=p[a], dst_ref=o[a].at[1 - c], send_sem=send_sems.at[a],
                                         recv_sem=recv_sems.at[a], device_id=(x, y, 1 - c), device_id_type=MESH).wait_recv()
        for cp in cps:
            cp.wait_send()
        for cp in loc:
            cp.wait()

    return pl.pallas_call(
        body, name="grad_join_halves", in_specs=[ANY] * na, out_specs=[ANY] * na,
        out_shape=[jax.ShapeDtypeStruct((2, *p.shape), F32) for p in parts],
        scratch_shapes=[pltpu.SemaphoreType.DMA((na,)), pltpu.SemaphoreType.DMA((na,)), pltpu.SemaphoreType.DMA((na,))],
    )(*parts)


PACK_ROWS = 8


def _pack(arrays):
    flat = jnp.concatenate([a.reshape(-1) for a in arrays])
    n = flat.shape[0]
    padded = -(-n // (PACK_ROWS * LANES)) * (PACK_ROWS * LANES)
    return jnp.pad(flat, (0, padded - n)).reshape(PACK_ROWS, padded // PACK_ROWS)


def _unpack(packed, shapes):
    flat = packed.reshape(-1)
    out, off = [], 0
    for s in shapes:
        n = 1
        for d in s:
            n *= d
        out.append(flat[off:off + n].reshape(s))
        off += n
    return out


REPLICATED = ("ada_b", "norm_mix_g", "norm_ffn_g", "a_vnorm_g", "a_spatial_w", "a_spatial_b", "b_q_norm_g",
              "b_k_norm_g", "ffn_dw_b")
SMALL_SHARDED = ("conv_pw1_b", "conv_dw_w", "conv_dw_b", "conv_ln_g", "conv_ln_b", "conv_pw2_b", "ffn_dw_w")
WEIGHTS = ("ada_w", "ada_b", "norm_mix_g", "norm_ffn_g", "ab_w_in", "a_vnorm_g", "a_spatial_w", "a_spatial_b",
           "b_q_norm_g", "b_k_norm_g", "ab_w_out", "conv_pw1_w", "conv_pw1_b", "conv_dw_w", "conv_dw_b", "conv_ln_g",
           "conv_ln_b", "conv_pw2_w", "conv_pw2_b", "ffn_up_w", "ffn_dw_w", "ffn_dw_b", "ffn_down_w")
N_CHIPS = 4


def _big_views(p):
    two = lambda a: a.reshape(2, a.shape[1] // 2, a.shape[2])
    return [two(p["ab_w_in"]), two(p["conv_pw1_w"]), p["ffn_up_w"],
            jnp.concatenate([p["ab_w_out"], p["conv_pw2_w"]], axis=0), p["ffn_down_w"]]


def _big_unview(views, p):
    w_in, pw1, up, wo2, dn = views
    return dict(ab_w_in=w_in.reshape(p["ab_w_in"].shape), conv_pw1_w=pw1.reshape(p["conv_pw1_w"].shape), ffn_up_w=up,
                ab_w_out=wo2[0:1], conv_pw2_w=wo2[1:2], ffn_down_w=dn)


def kernel(x, c, positions, ada_w, ada_b, norm_mix_g, norm_ffn_g, ab_w_in, a_vnorm_g, a_spatial_w, a_spatial_b, b_q_norm_g, b_k_norm_g, ab_w_out, conv_pw1_w, conv_pw1_b, conv_dw_w, conv_dw_b, conv_ln_g, conv_ln_b, conv_pw2_w, conv_pw2_b, ffn_up_w, ffn_dw_w, ffn_dw_b, ffn_down_w, loss_target, m_ada_w, m_ada_b, m_norm_mix_g, m_norm_ffn_g, m_ab_w_in, m_a_vnorm_g, m_a_spatial_w, m_a_spatial_b, m_b_q_norm_g, m_b_k_norm_g, m_ab_w_out, m_conv_pw1_w, m_conv_pw1_b, m_conv_dw_w, m_conv_dw_b, m_conv_ln_g, m_conv_ln_b, m_conv_pw2_w, m_conv_pw2_b, m_ffn_up_w, m_ffn_dw_w, m_ffn_dw_b, m_ffn_down_w, v_ada_w, v_ada_b, v_norm_mix_g, v_norm_ffn_g, v_ab_w_in, v_a_vnorm_g, v_a_spatial_w, v_a_spatial_b, v_b_q_norm_g, v_b_k_norm_g, v_ab_w_out, v_conv_pw1_w, v_conv_pw1_b, v_conv_dw_w, v_conv_dw_b, v_conv_ln_g, v_conv_ln_b, v_conv_pw2_w, v_conv_pw2_b, v_ffn_up_w, v_ffn_dw_w, v_ffn_dw_b, v_ffn_down_w):
    w = dict(ada_w=ada_w, ada_b=ada_b, norm_mix_g=norm_mix_g, norm_ffn_g=norm_ffn_g, ab_w_in=ab_w_in, a_vnorm_g=a_vnorm_g, a_spatial_w=a_spatial_w, a_spatial_b=a_spatial_b, b_q_norm_g=b_q_norm_g, b_k_norm_g=b_k_norm_g, ab_w_out=ab_w_out, conv_pw1_w=conv_pw1_w, conv_pw1_b=conv_pw1_b, conv_dw_w=conv_dw_w, conv_dw_b=conv_dw_b, conv_ln_g=conv_ln_g, conv_ln_b=conv_ln_b, conv_pw2_w=conv_pw2_w, conv_pw2_b=conv_pw2_b, ffn_up_w=ffn_up_w, ffn_dw_w=ffn_dw_w, ffn_dw_b=ffn_dw_b, ffn_down_w=ffn_down_w)
    m = dict(ada_w=m_ada_w, ada_b=m_ada_b, norm_mix_g=m_norm_mix_g, norm_ffn_g=m_norm_ffn_g, ab_w_in=m_ab_w_in, a_vnorm_g=m_a_vnorm_g, a_spatial_w=m_a_spatial_w, a_spatial_b=m_a_spatial_b, b_q_norm_g=m_b_q_norm_g, b_k_norm_g=m_b_k_norm_g, ab_w_out=m_ab_w_out, conv_pw1_w=m_conv_pw1_w, conv_pw1_b=m_conv_pw1_b, conv_dw_w=m_conv_dw_w, conv_dw_b=m_conv_dw_b, conv_ln_g=m_conv_ln_g, conv_ln_b=m_conv_ln_b, conv_pw2_w=m_conv_pw2_w, conv_pw2_b=m_conv_pw2_b, ffn_up_w=m_ffn_up_w, ffn_dw_w=m_ffn_dw_w, ffn_dw_b=m_ffn_dw_b, ffn_down_w=m_ffn_down_w)
    v = dict(ada_w=v_ada_w, ada_b=v_ada_b, norm_mix_g=v_norm_mix_g, norm_ffn_g=v_norm_ffn_g, ab_w_in=v_ab_w_in, a_vnorm_g=v_a_vnorm_g, a_spatial_w=v_a_spatial_w, a_spatial_b=v_a_spatial_b, b_q_norm_g=v_b_q_norm_g, b_k_norm_g=v_b_k_norm_g, ab_w_out=v_ab_w_out, conv_pw1_w=v_conv_pw1_w, conv_pw1_b=v_conv_pw1_b, conv_dw_w=v_conv_dw_w, conv_dw_b=v_conv_dw_b, conv_ln_g=v_conv_ln_g, conv_ln_b=v_conv_ln_b, conv_pw2_w=v_conv_pw2_w, conv_pw2_b=v_conv_pw2_b, ffn_up_w=v_ffn_up_w, ffn_dw_w=v_ffn_dw_w, ffn_dw_b=v_ffn_dw_b, ffn_down_w=v_ffn_down_w)
    S, D = x.shape[1], x.shape[2]
    xi, yi, ci = lax.axis_index("x"), lax.axis_index("y"), lax.axis_index("c")
    q = 2 * xi + yi
    b = 2 * q + ci
    take_dev = lambda g: g.reshape(8, PACK_ROWS, -1)

    c_all = _allgather8("ag_c", c.reshape(PACK_ROWS, D // PACK_ROWS)).reshape(8, D)
    n_ada = ada_w.shape[2]
    mod_sh = _ada_fwd("ada_fwd", c_all, ada_w, lax.dynamic_slice_in_dim(ada_b, q * n_ada, n_ada, axis=1))
    mod_g = take_dev(_allgather8("ag_mod", mod_sh.reshape(PACK_ROWS, -1))).reshape(8, 2, 8, n_ada)[0::2]
    mod_mine = lax.dynamic_index_in_dim(mod_g, b, axis=2, keepdims=False)
    mod = mod_mine.transpose(1, 0, 2).reshape(2, 6, D)

    full = _gather_weights([_cast_bf16(f"cast_{name}", a) for a, (name, _) in zip(_big_views(w), BIG)])
    wb = dict(w_in=full[0].reshape(D, -1), pw1=full[1].reshape(D, -1), up=full[2], w_out=full[3][0], pw2=full[3][1],
              dn=full[4])
    sh_shapes = [w[n].shape for n in SMALL_SHARDED]
    sm_g = take_dev(_allgather8("ag_small", _pack([w[n] for n in SMALL_SHARDED])))[0::2]
    per_chip = [_unpack(sm_g[k], sh_shapes) for k in range(N_CHIPS)]
    sp = {n: jnp.concatenate([per_chip[k][i] for k in range(N_CHIPS)], axis=-1) for i, n in enumerate(SMALL_SHARDED)}
    sp.update({n: w[n] for n in REPLICATED if n != "ada_b"})

    lossv, grad_x, big, small, dmod = _local_step(x[0], loss_target[0], positions[0], mod, wb, sp)
    loss = lax.psum(0.5 * jnp.sum(lossv) / D, ("x", "y", "c"))

    small_names = [n for n in REPLICATED if n != "ada_b"] + list(SMALL_SHARDED)
    payload = [dmod.reshape(2, 6 * D)] + [small[n] for n in small_names]
    pay_shapes = [p.shape for p in payload]
    gathered = _allgather8("ag_grads", _pack(payload))
    totals = _unpack(_sum8("sum_grads", gathered), pay_shapes)
    grads = dict(zip(["ada_b"] + small_names, totals))
    for n in SMALL_SHARDED:
        n_sh = w[n].shape[-1]
        grads[n] = lax.dynamic_slice_in_dim(grads[n], q * n_sh, n_sh, axis=grads[n].ndim - 1)
    dmod_all = take_dev(gathered).reshape(8, -1)[:, :2 * 6 * D].reshape(8, 2, 6 * D)
    dmod_sh = lax.dynamic_slice_in_dim(dmod_all, q * n_ada, n_ada, axis=2).transpose(1, 0, 2)

    g_views = [big["w_in"].reshape(2, D // 2, -1), big["pw1"].reshape(2, D // 2, -1), big["up"],
               jnp.stack([big["w_out"], big["pw2"]]), big["dn"]]
    from_sibling = _swap_halves(g_views)
    slabs = [_add2(f"presum_{name}", lax.dynamic_index_in_dim(g, ci, 0, keepdims=False), t)
             for g, t, (name, _) in zip(g_views, from_sibling, BIG)]
    sizes = [s.shape[1] // N_CHIPS if kind == "col" else s.shape[0] // N_CHIPS for s, (_, kind) in zip(slabs, BIG)]
    received = _scatter_regions(slabs, sizes)
    parts = []
    for s, r, n, (name, kind) in zip(slabs, received, sizes, BIG):
        own = lax.dynamic_slice_in_dim(s, q * n, n, axis=1 if kind == "col" else 0)
        parts.append(_add4(f"sum_{name}", own, r))
    grads.update(_big_unview(_join_halves(parts), w))

    grads["ada_w"], delta_ada, m_ada, v_ada = _ada_update("ada_update", c_all, dmod_sh, ada_w, m_ada_w, v_ada_w)
    delta, new_m, new_v = dict(ada_w=delta_ada), dict(ada_w=m_ada), dict(ada_w=v_ada)
    gv, wv, mv, vv = _big_views(grads), _big_views(w), _big_views(m), _big_views(v)
    upd = [[], [], []]
    for a, (name, _) in enumerate(BIG):
        to2d = lambda t: t.reshape(-1, t.shape[-1])
        outs = _adamw(f"adamw_{name}", to2d(wv[a]), to2d(gv[a]), to2d(mv[a]), to2d(vv[a]))
        for k in range(3):
            upd[k].append(outs[k].reshape(wv[a].shape))
    for tgt, views in zip((delta, new_m, new_v), upd):
        tgt.update(_big_unview(views, w))
    rest = list(REPLICATED) + list(SMALL_SHARDED)
    rest_shapes = [w[n].shape for n in rest]
    outs = _adamw("adamw_small", *[_pack([src[n].reshape(w[n].shape) for n in rest]) for src in (w, grads, m, v)])
    for tgt, packed in zip((delta, new_m, new_v), outs):
        tgt.update(dict(zip(rest, _unpack(packed, rest_shapes))))
    for n in rest:
        grads[n] = grads[n].reshape(w[n].shape)

    return (loss, grad_x[None], *[grads[n] for n in WEIGHTS], *[delta[n] for n in WEIGHTS],
            *[new_m[n] for n in WEIGHTS], *[new_v[n] for n in WEIGHTS])
```

```python
import functools

import jax
import jax.numpy as jnp
from jax import lax
from jax.experimental import pallas as pl
from jax.experimental.pallas import tpu as pltpu

F32, BF16 = jnp.float32, jnp.bfloat16
EPS = 1e-6
NEG = -1e30
ROPE_THETA = 10000.0
LANES = 128
VMEM_LIMIT = 56 * 1024 * 1024
ADAM_LR, ADAM_B1, ADAM_B2, ADAM_EPS, ADAM_WD, ADAM_STEP = 0.001, 0.9, 0.999, 1e-08, 0.01, 10
MESH = pl.DeviceIdType.MESH


def _params(sem):
    return pltpu.CompilerParams(dimension_semantics=sem, vmem_limit_bytes=VMEM_LIMIT)


_DN = {"nn": (((1,), (0,)), ((), ())), "nt": (((1,), (1,)), ((), ())), "tn": (((0,), (0,)), ((), ()))}


def _matmul(name, a, b, mode, out_dtype, tm=512, tn=512, tk=1024, bias=None):
    if mode == "nn":
        (M, K), N = a.shape, b.shape[1]
    elif mode == "nt":
        (M, K), N = a.shape, b.shape[0]
    else:
        (K, M), N = a.shape, b.shape[1]
    tm, tn, tk = min(tm, M), min(tn, N), min(tk, K)
    assert M % tm == 0 and N % tn == 0 and K % tk == 0, (name, M, N, K, tm, tn, tk)
    nk = K // tk
    if mode == "tn":
        a_spec = pl.BlockSpec((tk, tm), lambda i, j, k: (k, i))
    else:
        a_spec = pl.BlockSpec((tm, tk), lambda i, j, k: (i, k))
    if mode == "nt":
        b_spec = pl.BlockSpec((tn, tk), lambda i, j, k: (j, k))
    else:
        b_spec = pl.BlockSpec((tk, tn), lambda i, j, k: (k, j))
    in_specs, args = [a_spec, b_spec], [a, b]
    if bias is not None:
        in_specs.append(pl.BlockSpec((1, tn), lambda i, j, k: (0, j)))
        args.append(bias)
    n_in = len(args)

    def body(*refs):
        a_ref, b_ref, o_ref = refs[0], refs[1], refs[n_in]
        p = lax.dot_general(a_ref[...], b_ref[...], _DN[mode], preferred_element_type=F32)

        def finish(acc):
            if bias is not None:
                acc = acc + refs[2][...]
            o_ref[...] = acc.astype(o_ref.dtype)

        if nk == 1:
            finish(p)
        else:
            acc_ref = refs[n_in + 1]
            k = pl.program_id(2)

            @pl.when(k == 0)
            def _():
                acc_ref[...] = p

            @pl.when(k > 0)
            def _():
                acc_ref[...] += p

            @pl.when(k == nk - 1)
            def _():
                finish(acc_ref[...])

    return pl.pallas_call(
        body, name=name, grid=(M // tm, N // tn, nk), in_specs=in_specs,
        out_specs=pl.BlockSpec((tm, tn), lambda i, j, k: (i, j)),
        out_shape=jax.ShapeDtypeStruct((M, N), out_dtype),
        scratch_shapes=[pltpu.VMEM((tm, tn), F32)] if nk > 1 else [],
        compiler_params=_params(("parallel", "parallel", "arbitrary")),
    )(*args)


def _rowcall(name, body, nrows, tm, ins, outs):
    nblk = nrows // tm
    assert nrows % tm == 0

    def spec(kind, shape):
        k = kind[0]
        if k == "row":
            cw, cb = kind[1] or shape[-1], kind[2]
            return pl.BlockSpec((tm, cw), lambda i: (i, cb))
        if k == "prev":
            hb, cw, cb = kind[1], kind[2] or shape[-1], kind[3]
            r = tm // hb
            return pl.BlockSpec((hb, cw), lambda i: (jnp.maximum(i * r - 1, 0), cb))
        if k == "next":
            hb, cw, cb = kind[1], kind[2] or shape[-1], kind[3]
            r, last = tm // hb, nrows // hb - 1
            return pl.BlockSpec((hb, cw), lambda i: (jnp.minimum((i + 1) * r, last), cb))
        if k == "off":
            off, cw, cb = kind[1], kind[2] or shape[-1], kind[3]
            return pl.BlockSpec((tm, cw), lambda i: (jnp.clip(i + off, 0, nblk - 1), cb))
        nd = len(shape)
        return pl.BlockSpec(tuple(shape), lambda i: (0,) * nd)

    has_acc = any(o[2][0] == "acc" for o in outs)
    return pl.pallas_call(
        body, name=name, grid=(nblk,),
        in_specs=[spec(kind, a.shape) for a, kind in ins],
        out_specs=[spec(kind, shape) for shape, _, kind in outs],
        out_shape=[jax.ShapeDtypeStruct(tuple(shape), dt) for shape, dt, _ in outs],
        compiler_params=_params(("arbitrary",) if has_acc else ("parallel",)),
    )(*[a for a, _ in ins])


ROW = ("row", None, 0)
FULL = ("full",)
ACC = ("acc",)


def _colsum(x):
    return jnp.sum(x, axis=0, keepdims=True)


def _acc_add(i, ref, val, rows=None):
    idx = (slice(None),) * len(ref.shape) if rows is None else rows

    @pl.when(i == 0)
    def _():
        ref[idx] = val

    @pl.when(i > 0)
    def _():
        ref[idx] = ref[idx] + val


def _sigmoid(x):
    return 1.0 / (1.0 + jnp.exp(-x))


def _gelu(x):
    return 0.5 * x * (1.0 + lax.erf(x * (2.0 ** -0.5)))


def _gelu_grad(x):
    return 0.5 * (1.0 + lax.erf(x * (2.0 ** -0.5))) + x * jnp.exp(-0.5 * x * x) * ((2.0 * jnp.pi) ** -0.5)


def _shift_prev(ext, s, hb):
    if s == 0:
        return ext[hb:]
    return pltpu.roll(ext, s, axis=0)[hb:]


def _shift_next(ext, s, tm):
    if s == 0:
        return ext[:tm]
    return pltpu.roll(ext, ext.shape[0] - s, axis=0)[:tm]


def _rms_mod_val(x, g, sc, sh):
    r = lax.rsqrt(jnp.mean(x * x, axis=-1, keepdims=True) + EPS)
    return x * r * g * (1.0 + sc) + sh


def _mod_first(name, x, g, sc, sh, tm=256):
    S, D = x.shape

    def body(x_ref, g_ref, sc_ref, sh_ref, h_ref):
        h_ref[...] = _rms_mod_val(x_ref[...], g_ref[...], sc_ref[...], sh_ref[...]).astype(BF16)

    return _rowcall(name, body, S, tm, [(x, ROW), (g, FULL), (sc, FULL), (sh, FULL)], [((S, D), BF16, ROW)])[0]


def _resid_mod(name, x, y, gate, g, sc, sh, tm=256):
    S, D = x.shape

    def body(x_ref, y_ref, gate_ref, g_ref, sc_ref, sh_ref, xo_ref, h_ref):
        xn = x_ref[...] + gate_ref[...] * y_ref[...]
        xo_ref[...] = xn
        h_ref[...] = _rms_mod_val(xn, g_ref[...], sc_ref[...], sh_ref[...]).astype(BF16)

    return _rowcall(name, body, S, tm,
                    [(x, ROW), (y, ROW), (gate, FULL), (g, FULL), (sc, FULL), (sh, FULL)],
                    [((S, D), F32, ROW), ((S, D), BF16, ROW)])


def _loss_head(name, x, y, gate, tgt, tm=256):
    S, D = x.shape

    def body(x_ref, y_ref, gate_ref, t_ref, dx_ref, l_ref):
        i = pl.program_id(0)
        err = x_ref[...] + gate_ref[...] * y_ref[...] - t_ref[...]
        dx_ref[...] = err * (1.0 / D)
        _acc_add(i, l_ref, _colsum(err * err))

    return _rowcall(name, body, S, tm, [(x, ROW), (y, ROW), (gate, FULL), (tgt, ROW)],
                    [((S, D), F32, ROW), ((1, D), F32, ACC)])


def _gate_bwd(name, dxo, y, gate, tm=256):
    S, D = dxo.shape

    def body(d_ref, y_ref, gate_ref, dy_ref, dg_ref, db_ref):
        i = pl.program_id(0)
        d = d_ref[...]
        dy = d * gate_ref[...]
        dy_ref[...] = dy.astype(BF16)
        _acc_add(i, dg_ref, _colsum(d * y_ref[...]))
        _acc_add(i, db_ref, _colsum(dy))

    return _rowcall(name, body, S, tm, [(dxo, ROW), (y, ROW), (gate, FULL)],
                    [((S, D), BF16, ROW), ((1, D), F32, ACC), ((1, D), F32, ACC)])


def _mod_bwd(name, dxo, dh, x, g, sc, tm=256):
    S, D = x.shape

    def body(d_ref, dh_ref, x_ref, g_ref, sc_ref, dx_ref, dsh_ref, dsc_ref, dg_ref):
        i = pl.program_id(0)
        xv, dh_v, gv = x_ref[...], dh_ref[...], g_ref[...]
        r = lax.rsqrt(jnp.mean(xv * xv, axis=-1, keepdims=True) + EPS)
        n = xv * r
        _acc_add(i, dsh_ref, _colsum(dh_v))
        _acc_add(i, dsc_ref, _colsum(dh_v * (n * gv)))
        dy = dh_v * (1.0 + sc_ref[...])
        _acc_add(i, dg_ref, _colsum(dy * n))
        dn = dy * gv
        dx_ref[...] = d_ref[...] + r * (dn - n * jnp.mean(dn * n, axis=-1, keepdims=True))

    return _rowcall(name, body, S, tm, [(dxo, ROW), (dh, ROW), (x, ROW), (g, FULL), (sc, FULL)],
                    [((S, D), F32, ROW), ((1, D), F32, ACC), ((1, D), F32, ACC), ((1, D), F32, ACC)])


HB16 = 16


def _conv3_val(ext, w, b, hb):
    return w[2:3] * ext[hb:] + w[1:2] * _shift_prev(ext, 1, hb) + w[0:1] * _shift_prev(ext, 2, hb) + b


def _glu3_fwd(name, u, w, b, tm=128):
    S, F2 = u.shape
    Fh = F2 // 2

    def body(ua_ref, ub_ref, ha_ref, hb_ref, w_ref, b_ref, o_ref):
        live = (pl.program_id(0) > 0).astype(F32)
        wv, bv = w_ref[...], b_ref[...]
        ea = jnp.concatenate([ha_ref[...].astype(F32) * live, ua_ref[...].astype(F32)], axis=0)
        eb = jnp.concatenate([hb_ref[...].astype(F32) * live, ub_ref[...].astype(F32)], axis=0)
        za = _conv3_val(ea, wv[:, :Fh], bv[:, :Fh], HB16)
        zb = _conv3_val(eb, wv[:, Fh:], bv[:, Fh:], HB16)
        o_ref[...] = (za * _sigmoid(za) * zb).astype(BF16)

    return _rowcall(name, body, S, tm,
                    [(u, ("row", Fh, 0)), (u, ("row", Fh, 1)), (u, ("prev", HB16, Fh, 0)), (u, ("prev", HB16, Fh, 1)),
                     (w, FULL), (b, FULL)],
                    [((S, Fh), BF16, ROW)])[0]


def _glu3_bwd(name, u, dhm, w, b, tm=128):
    S, F2 = u.shape
    Fh = F2 // 2

    def body(ua_ref, ub_ref, ha_ref, hb_ref, d_ref, w_ref, b_ref, dz_ref, acc_ref):
        i = pl.program_id(0)
        live = (i > 0).astype(F32)
        wv, bv, d = w_ref[...], b_ref[...], d_ref[...]
        ea = jnp.concatenate([ha_ref[...].astype(F32) * live, ua_ref[...].astype(F32)], axis=0)
        eb = jnp.concatenate([hb_ref[...].astype(F32) * live, ub_ref[...].astype(F32)], axis=0)
        za = _conv3_val(ea, wv[:, :Fh], bv[:, :Fh], HB16)
        zb = _conv3_val(eb, wv[:, Fh:], bv[:, Fh:], HB16)
        sg = _sigmoid(za)
        da = d * zb * (sg * (1.0 + za * (1.0 - sg)))
        db = d * (za * sg)
        dz_ref[:, :Fh] = da.astype(BF16)
        dz_ref[:, Fh:] = db.astype(BF16)
        for k in range(3):
            row = jnp.concatenate([_colsum(da * _shift_prev(ea, 2 - k, HB16)),
                                   _colsum(db * _shift_prev(eb, 2 - k, HB16))], axis=1)
            _acc_add(i, acc_ref, row, rows=(slice(k, k + 1), slice(None)))
        _acc_add(i, acc_ref, jnp.concatenate([_colsum(da), _colsum(db)], axis=1), rows=(slice(3, 4), slice(None)))

        @pl.when(i == 0)
        def _():
            acc_ref[4:8, :] = jnp.zeros((4, F2), F32)

    return _rowcall(name, body, S, tm,
                    [(u, ("row", Fh, 0)), (u, ("row", Fh, 1)), (u, ("prev", HB16, Fh, 0)), (u, ("prev", HB16, Fh, 1)),
                     (dhm, ROW), (w, FULL), (b, FULL)],
                    [((S, F2), BF16, ROW), ((8, F2), F32, ACC)])


def _conv3_bwd(name, dz, w, tm=128):
    S, F2 = dz.shape
    nblk = S // tm

    def body(d_ref, n_ref, w_ref, o_ref):
        live = (pl.program_id(0) < nblk - 1).astype(F32)
        ext = jnp.concatenate([d_ref[...].astype(F32), n_ref[...].astype(F32) * live], axis=0)
        wv = w_ref[...]
        o_ref[...] = (wv[2:3] * ext[:tm] + wv[1:2] * _shift_next(ext, 1, tm)
                      + wv[0:1] * _shift_next(ext, 2, tm)).astype(BF16)

    return _rowcall(name, body, S, tm, [(dz, ROW), (dz, ("next", HB16, None, 0)), (w, FULL)],
                    [((S, F2), BF16, ROW)])[0]


def _ffn_fwd(name, h, w_up, w_dn, dw_w, dw_b):
    u = _matmul(f"{name}_up", h, w_up, "nn", BF16)
    hm = _glu3_fwd(f"{name}_glu", u, dw_w, dw_b)
    f = _matmul(f"{name}_dn", hm, w_dn, "nn", F32, tk=w_dn.shape[0])
    return f, (u, hm)


def _ffn_bwd(name, dy, h, u, hm, w_up, w_dn, dw_w, dw_b):
    Fh = w_dn.shape[0]
    dhm = _matmul(f"{name}_ddn_x", dy, w_dn, "nt", F32, tn=Fh // 2)
    g_dn = _matmul(f"{name}_ddn_w", hm, dy, "tn", BF16, tm=Fh // 2, tn=512, tk=1024)
    dz, acc = _glu3_bwd(f"{name}_dglu", u, dhm, dw_w, dw_b)
    du = _conv3_bwd(f"{name}_dconv", dz, dw_w)
    g_up = _matmul(f"{name}_dup_w", h, du, "tn", BF16, tm=1024, tn=512, tk=1024)
    dh = _matmul(f"{name}_dup_x", du, w_up, "nt", F32, tn=1024, tk=512)
    return dh, dict(up=g_up, dn=g_dn, dw_w=acc[0:3], dw_b=acc[3:4])


HB32 = 32


def _glu31_fwd(name, p, w, b, tm=256):
    S, D2 = p.shape
    D = D2 // 2
    K = w.shape[0]

    def body(a_ref, g_ref, ha_ref, hg_ref, w_ref, b_ref, o_ref):
        live = (pl.program_id(0) > 0).astype(F32)
        y1 = a_ref[...] * _sigmoid(g_ref[...])
        yh = ha_ref[...] * _sigmoid(hg_ref[...]) * live
        ext = jnp.concatenate([yh, y1], axis=0)
        wv = w_ref[...]
        acc = b_ref[...] + wv[K - 1:K] * y1
        for k in range(K - 1):
            acc = acc + wv[k:k + 1] * _shift_prev(ext, K - 1 - k, HB32)
        o_ref[...] = acc

    return _rowcall(name, body, S, tm,
                    [(p, ("row", D, 0)), (p, ("row", D, 1)), (p, ("prev", HB32, D, 0)), (p, ("prev", HB32, D, 1)),
                     (w, FULL), (b, FULL)],
                    [((S, D), F32, ROW)])[0]


def _ln_silu_fwd(name, y2, g, b, tm=256):
    S, D = y2.shape

    def body(y_ref, g_ref, b_ref, o_ref):
        y = y_ref[...]
        mu = jnp.mean(y, axis=-1, keepdims=True)
        yc = y - mu
        rs = lax.rsqrt(jnp.mean(yc * yc, axis=-1, keepdims=True) + EPS)
        y3 = yc * rs * g_ref[...] + b_ref[...]
        o_ref[...] = (y3 * _sigmoid(y3)).astype(BF16)

    return _rowcall(name, body, S, tm, [(y2, ROW), (g, FULL), (b, FULL)], [((S, D), BF16, ROW)])[0]


def _ln_silu_bwd(name, y2, dy4, g, b, tm=256):
    S, D = y2.shape

    def body(y_ref, d_ref, g_ref, b_ref, o_ref, dg_ref, db_ref):
        i = pl.program_id(0)
        y, gv = y_ref[...], g_ref[...]
        mu = jnp.mean(y, axis=-1, keepdims=True)
        yc = y - mu
        rs = lax.rsqrt(jnp.mean(yc * yc, axis=-1, keepdims=True) + EPS)
        n = yc * rs
        y3 = n * gv + b_ref[...]
        sg = _sigmoid(y3)
        dy3 = d_ref[...] * (sg * (1.0 + y3 * (1.0 - sg)))
        _acc_add(i, db_ref, _colsum(dy3))
        _acc_add(i, dg_ref, _colsum(dy3 * n))
        dn = dy3 * gv
        o_ref[...] = rs * (dn - jnp.mean(dn, axis=-1, keepdims=True) - n * jnp.mean(dn * n, axis=-1, keepdims=True))

    return _rowcall(name, body, S, tm, [(y2, ROW), (dy4, ROW), (g, FULL), (b, FULL)],
                    [((S, D), F32, ROW), ((1, D), F32, ACC), ((1, D), F32, ACC)])


def _glu31_bwd(name, p, dy2, w, tm=256):
    S, D2 = p.shape
    D = D2 // 2
    K = w.shape[0]
    nblk = S // tm

    def body(a_ref, g_ref, ha_ref, hg_ref, d_ref, dn_ref, w_ref, dp_ref, dw_ref, dcb_ref, dpb_ref):
        i = pl.program_id(0)
        live_prev = (i > 0).astype(F32)
        live_next = (i < nblk - 1).astype(F32)
        a, sg, d, wv = a_ref[...], _sigmoid(g_ref[...]), d_ref[...], w_ref[...]
        y1 = a * sg
        ext_y = jnp.concatenate([ha_ref[...] * _sigmoid(hg_ref[...]) * live_prev, y1], axis=0)
        ext_d = jnp.concatenate([d, dn_ref[...] * live_next], axis=0)
        dy1 = wv[K - 1:K] * d
        for k in range(K - 1):
            dy1 = dy1 + wv[k:k + 1] * _shift_next(ext_d, K - 1 - k, tm)
        for k in range(K):
            _acc_add(i, dw_ref, _colsum(d * _shift_prev(ext_y, K - 1 - k, HB32)), rows=(slice(k, k + 1), slice(None)))

        @pl.when(i == 0)
        def _():
            dw_ref[K:, :] = jnp.zeros((dw_ref.shape[0] - K, D), F32)

        _acc_add(i, dcb_ref, _colsum(d))
        da = dy1 * sg
        dg = dy1 * a * sg * (1.0 - sg)
        dp_ref[:, :D] = da.astype(BF16)
        dp_ref[:, D:] = dg.astype(BF16)
        _acc_add(i, dpb_ref, jnp.concatenate([_colsum(da), _colsum(dg)], axis=1))

    return _rowcall(name, body, S, tm,
                    [(p, ("row", D, 0)), (p, ("row", D, 1)), (p, ("prev", HB32, D, 0)), (p, ("prev", HB32, D, 1)),
                     (dy2, ROW), (dy2, ("next", HB32, None, 0)), (w, FULL)],
                    [((S, D2), BF16, ROW), ((HB32, D), F32, ACC), ((1, D), F32, ACC), ((1, D2), F32, ACC)])


CHUNK = 128
A_GROUPS = 4


def _group_ln(gv):
    ns, rss = [], []
    for g in range(A_GROUPS):
        xg = gv[:, g * LANES:(g + 1) * LANES]
        xc = xg - jnp.mean(xg, axis=-1, keepdims=True)
        rs = lax.rsqrt(jnp.mean(xc * xc, axis=-1, keepdims=True) + EPS)
        ns.append(xc * rs)
        rss.append(jnp.broadcast_to(rs, xg.shape))
    return jnp.concatenate(ns, axis=1), jnp.concatenate(rss, axis=1)


def _tril_mask():
    r = lax.broadcasted_iota(jnp.int32, (CHUNK, CHUNK), 0)
    c = lax.broadcasted_iota(jnp.int32, (CHUNK, CHUNK), 1)
    return r >= c


def _spatial(ws_ref, x, dn):
    mask = _tril_mask()
    rows = []
    for ci in range(x.shape[0] // CHUNK):
        cols = []
        for g in range(A_GROUPS):
            wm = jnp.where(mask, ws_ref[g], 0.0).astype(BF16)
            xb = x[ci * CHUNK:(ci + 1) * CHUNK, g * LANES:(g + 1) * LANES]
            cols.append(lax.dot_general(wm, xb, dn, preferred_element_type=F32))
        rows.append(jnp.concatenate(cols, axis=1))
    return jnp.concatenate(rows, axis=0)


def _mixa_fwd(name, z, vg, ws, bias_full, tm=256):
    S = z.shape[0]
    W = A_GROUPS * LANES

    def body(u_ref, v_ref, vg_ref, ws_ref, b_ref, o_ref):
        nh, _ = _group_ln(_gelu(v_ref[...]))
        vn = (nh * vg_ref[...]).astype(BF16)
        f = _spatial(ws_ref, vn, _DN["nn"]) + jnp.concatenate([b_ref[...]] * (tm // CHUNK), axis=0)
        o_ref[...] = (_gelu(u_ref[...]) * f).astype(BF16)

    return _rowcall(name, body, S, tm,
                    [(z, ("row", W, 0)), (z, ("row", W, 1)), (vg, FULL), (ws, FULL), (bias_full, FULL)],
                    [((S, W), BF16, ROW)])[0]


def _mixa_bwd(name, z, dyab, vg, ws, bias_full, tm=256):
    S = z.shape[0]
    W = A_GROUPS * LANES
    nch = tm // CHUNK

    def body(u_ref, v_ref, d_ref, vg_ref, ws_ref, b_ref, dz_ref, dws_ref, dbf_ref, dvg_ref):
        i = pl.program_id(0)
        u, v, d, vgv = u_ref[...], v_ref[...], d_ref[...], vg_ref[...]
        nh, rs = _group_ln(_gelu(v))
        vn = (nh * vgv).astype(BF16)
        f = _spatial(ws_ref, vn, _DN["nn"]) + jnp.concatenate([b_ref[...]] * nch, axis=0)
        dz_ref[:, :W] = (d * f * _gelu_grad(u)).astype(BF16)
        df = d * _gelu(u)
        dbf = df[0:CHUNK]
        for ci in range(1, nch):
            dbf = dbf + df[ci * CHUNK:(ci + 1) * CHUNK]
        _acc_add(i, dbf_ref, dbf)
        dfb = df.astype(BF16)
        mask = _tril_mask()
        for g in range(A_GROUPS):
            acc = jnp.zeros((CHUNK, CHUNK), F32)
            for ci in range(nch):
                blk = (slice(ci * CHUNK, (ci + 1) * CHUNK), slice(g * LANES, (g + 1) * LANES))
                acc = acc + lax.dot_general(dfb[blk], vn[blk], _DN["nt"], preferred_element_type=F32)
            _acc_add(i, dws_ref, jnp.where(mask, acc, 0.0)[None], rows=(slice(g, g + 1), slice(None), slice(None)))
        dvn = _spatial(ws_ref, dfb, _DN["tn"])
        _acc_add(i, dvg_ref, _colsum(dvn * nh))
        dnh = dvn * vgv
        parts = []
        for g in range(A_GROUPS):
            cs = slice(g * LANES, (g + 1) * LANES)
            dg_, ng = dnh[:, cs], nh[:, cs]
            parts.append(dg_ - jnp.mean(dg_, axis=-1, keepdims=True) - ng * jnp.mean(dg_ * ng, axis=-1, keepdims=True))
        dz_ref[:, W:] = (rs * jnp.concatenate(parts, axis=1) * _gelu_grad(v)).astype(BF16)

    return _rowcall(name, body, S, tm,
                    [(z, ("row", W, 0)), (z, ("row", W, 1)), (dyab, ("row", W, 0)), (vg, FULL), (ws, FULL),
                     (bias_full, FULL)],
                    [((S, 2 * W), BF16, ROW), ((A_GROUPS, CHUNK, CHUNK), F32, ACC), ((CHUNK, W), F32, ACC),
                     ((1, W), F32, ACC)])


HEAD = 64
N_HEADS = 8
BW = HEAD * N_HEADS
QB = 128
DILATIONS = (1, 4, 16)
QK_SCALE = HEAD ** -0.5


def _gsum64(x, ones_bd):
    x1 = x.astype(BF16)
    r1 = x - x1.astype(F32)
    x2 = r1.astype(BF16)
    x3 = (r1 - x2.astype(F32)).astype(BF16)
    dot = lambda t: jnp.dot(t, ones_bd, preferred_element_type=F32)
    return dot(x1) + dot(x2) + dot(x3)


def _swap32(x):
    n = x.shape[-1]
    up = pltpu.roll(x, n - HEAD // 2, axis=1)
    dn = pltpu.roll(x, HEAD // 2, axis=1)
    lane = lax.broadcasted_iota(jnp.int32, x.shape, 1)
    return jnp.where((lane % HEAD) < HEAD // 2, up, dn)


def _tile4(t):
    return jnp.concatenate([t] * (BW // LANES), axis=1)


def _qkv_fwd(name, z, cos, sin, ones_bd, qg, kg, tm=256):
    S = z.shape[0]

    def body(q_ref, k_ref, v_ref, c_ref, s_ref, o_ref, qg_ref, kg_ref, qo_ref, ko_ref, vo_ref):
        c, s, ob = _tile4(c_ref[...]), _tile4(s_ref[...]), o_ref[...]

        def norm_rope(x, g):
            r = lax.rsqrt(_gsum64(x * x, ob) * (1.0 / HEAD) + EPS)
            xn = x * r * g
            return xn * c + _swap32(xn) * s

        qo_ref[...] = (norm_rope(q_ref[...], qg_ref[...]) * QK_SCALE).astype(BF16)
        ko_ref[...] = norm_rope(k_ref[...], kg_ref[...]).astype(BF16)
        vo_ref[...] = v_ref[...].astype(BF16)

    return _rowcall(name, body, S, tm,
                    [(z, ("row", BW, 2)), (z, ("row", BW, 3)), (z, ("row", BW, 4)), (cos, ROW), (sin, ROW),
                     (ones_bd, FULL), (qg, FULL), (kg, FULL)],
                    [((S, BW), BF16, ROW)] * 3)


def _scores(q_ref, kp_ref, kc_ref, h, i, seg_blocks):
    hs = slice(h * HEAD, (h + 1) * HEAD)
    qh = q_ref[:, hs]
    kcat = jnp.concatenate([kp_ref[:, hs], kc_ref[:, hs]], axis=0)
    s = lax.dot_general(qh, kcat, _DN["nt"], preferred_element_type=F32)
    qi = lax.broadcasted_iota(jnp.int32, (QB, 2 * QB), 0)
    kj = lax.broadcasted_iota(jnp.int32, (QB, 2 * QB), 1)
    has_prev = (i % seg_blocks) != 0
    valid = (kj >= qi) & (kj <= qi + QB) & ((kj >= QB) | has_prev)
    return qh, kcat, s, valid


def _attn_fwd(name, q, k, v, seg_len):
    S = q.shape[0]
    seg_blocks = seg_len // QB
    PREV = ("off", -1, None, 0)

    def body(q_ref, kp_ref, kc_ref, vp_ref, vc_ref, o_ref, l_ref):
        i = pl.program_id(0)
        outs, lses = [], []
        for h in range(N_HEADS):
            hs = slice(h * HEAD, (h + 1) * HEAD)
            _, _, s, valid = _scores(q_ref, kp_ref, kc_ref, h, i, seg_blocks)
            s = jnp.where(valid, s, NEG)
            m = jnp.max(s, axis=-1, keepdims=True)
            p = jnp.exp(s - m)
            den = jnp.sum(p, axis=-1, keepdims=True)
            vcat = jnp.concatenate([vp_ref[:, hs], vc_ref[:, hs]], axis=0)
            outs.append(jnp.dot(p.astype(BF16), vcat, preferred_element_type=F32) / den)
            lses.append(jnp.broadcast_to(m + jnp.log(den), (QB, HEAD)))
        o_ref[...] = jnp.concatenate(outs, axis=1)
        l_ref[...] = jnp.concatenate(lses, axis=1)

    return _rowcall(name, body, S, QB, [(q, ROW), (k, PREV), (k, ROW), (v, PREV), (v, ROW)],
                    [((S, BW), F32, ROW)] * 2)


def _attn_bwd(name, q, k, v, do, lse, delta, seg_len):
    S = q.shape[0]
    seg_blocks = seg_len // QB
    PREV = ("off", -1, None, 0)

    def body(q_ref, kp_ref, kc_ref, vp_ref, vc_ref, do_ref, l_ref, dl_ref, dq_ref, dkc_ref, dkp_ref, dvc_ref, dvp_ref):
        i = pl.program_id(0)
        dq, dkc, dkp, dvc, dvp = [], [], [], [], []
        for h in range(N_HEADS):
            hs = slice(h * HEAD, (h + 1) * HEAD)
            qh, kcat, s, valid = _scores(q_ref, kp_ref, kc_ref, h, i, seg_blocks)
            p = jnp.where(valid, jnp.exp(s - l_ref[:, h * HEAD:h * HEAD + 1]), 0.0)
            doh = do_ref[:, hs].astype(BF16)
            vcat = jnp.concatenate([vp_ref[:, hs], vc_ref[:, hs]], axis=0)
            dp = lax.dot_general(doh, vcat, _DN["nt"], preferred_element_type=F32)
            ds = (p * (dp - dl_ref[:, h * HEAD:h * HEAD + 1])).astype(BF16)
            dq.append(jnp.dot(ds, kcat, preferred_element_type=F32))
            dk = lax.dot_general(ds, qh, _DN["tn"], preferred_element_type=F32)
            dv = lax.dot_general(p.astype(BF16), doh, _DN["tn"], preferred_element_type=F32)
            dkp.append(dk[:QB])
            dkc.append(dk[QB:])
            dvp.append(dv[:QB])
            dvc.append(dv[QB:])
        dq_ref[...] = jnp.concatenate(dq, axis=1)
        dkc_ref[...] = jnp.concatenate(dkc, axis=1)
        dkp_ref[...] = jnp.concatenate(dkp, axis=1)
        dvc_ref[...] = jnp.concatenate(dvc, axis=1)
        dvp_ref[...] = jnp.concatenate(dvp, axis=1)

    return _rowcall(name, body, S, QB,
                    [(q, ROW), (k, PREV), (k, ROW), (v, PREV), (v, ROW), (do, ROW), (lse, ROW), (delta, ROW)],
                    [((S, BW), F32, ROW)] * 5)


def _merge_fwd(name, branches, tm=256):
    S = branches[0][0].shape[0]

    def body(o1, l1, o2, l2, o3, l3, y_ref, yb_ref, l_ref):
        ls = [l1[...], l2[...], l3[...]]
        m = jnp.maximum(jnp.maximum(ls[0], ls[1]), ls[2])
        es = [jnp.exp(l - m) for l in ls]
        den = es[0] + es[1] + es[2]
        y = (es[0] * o1[...] + es[1] * o2[...] + es[2] * o3[...]) / den
        y_ref[...] = y
        yb_ref[...] = y.astype(BF16)
        l_ref[...] = m + jnp.log(den)

    ins = [(t, ROW) for pair in branches for t in pair]
    return _rowcall(name, body, S, tm, ins, [((S, BW), F32, ROW), ((S, BW), BF16, ROW), ((S, BW), F32, ROW)])


def _delta(name, dyab, yb, ones_bd, tm=256):
    S = yb.shape[0]

    def body(d_ref, y_ref, o_ref, out_ref):
        out_ref[...] = _gsum64(d_ref[...] * y_ref[...], o_ref[...])

    return _rowcall(name, body, S, tm, [(dyab, ("row", BW, 1)), (yb, ROW), (ones_bd, FULL)], [((S, BW), F32, ROW)])[0]


def _qkv_bwd(name, z, cos, sin, ones_bd, qg, kg, pieces):
    S = z.shape[0]
    nblk = S // QB

    def body(q_ref, k_ref, c_ref, s_ref, o_ref, qg_ref, kg_ref, *rest):
        pr, (dz_ref, dqg_ref, dkg_ref) = rest[:15], rest[15:]
        i = pl.program_id(0)
        c, s, ob = _tile4(c_ref[...]), _tile4(s_ref[...]), o_ref[...]
        dq = dk = dv = None
        for b, dil in enumerate(DILATIONS):
            a_q, a_kc, a_kp, a_vc, a_vp = [r[...] for r in pr[5 * b:5 * b + 5]]
            live = ((i + dil) < nblk).astype(F32)
            tq, tk, tv = a_q, a_kc + a_kp * live, a_vc + a_vp * live
            dq, dk, dv = (tq, tk, tv) if b == 0 else (dq + tq, dk + tk, dv + tv)

        def back(x, g, d_rot, acc_ref):
            r = lax.rsqrt(_gsum64(x * x, ob) * (1.0 / HEAD) + EPS)
            n = x * r
            dxn = d_rot * c + _swap32(d_rot * s)
            _acc_add(i, acc_ref, _colsum(dxn * n))
            dn = dxn * g
            return r * (dn - n * (_gsum64(dn * n, ob) * (1.0 / HEAD)))

        dz_ref[:, :BW] = back(q_ref[...], qg_ref[...], dq * QK_SCALE, dqg_ref).astype(BF16)
        dz_ref[:, BW:2 * BW] = back(k_ref[...], kg_ref[...], dk, dkg_ref).astype(BF16)
        dz_ref[:, 2 * BW:] = dv.astype(BF16)

    ins = [(z, ("row", BW, 2)), (z, ("row", BW, 3)), (cos, ROW), (sin, ROW), (ones_bd, FULL), (qg, FULL), (kg, FULL)]
    for (a_q, a_kc, a_kp, a_vc, a_vp), dil in zip(pieces, DILATIONS):
        ins += [(a_q, ROW), (a_kc, ROW), (a_kp, ("off", dil, None, 0)), (a_vc, ROW), (a_vp, ("off", dil, None, 0))]
    return _rowcall(name, body, S, QB, ins,
                    [((S, 3 * BW), BF16, ROW), ((1, BW), F32, ACC), ((1, BW), F32, ACC)])


def _to_sub(t, dil):
    S, W = t.shape
    return t if dil == 1 else t.reshape(S // dil, dil, W).transpose(1, 0, 2).reshape(S, W)


def _from_sub(t, dil):
    S, W = t.shape
    return t if dil == 1 else t.reshape(dil, S // dil, W).transpose(1, 0, 2).reshape(S, W)


def _local_step(x0, tgt, pos, mod, wb, sp, pipe):
    S, D = x0.shape
    md = lambda l, j: mod[l, j:j + 1]
    sh_m, sc_m, g_m, sh_f, sc_f, g_f = ([md(l, j) for l in range(2)] for j in range(6))
    nm_g, nf_g = sp["norm_mix_g"], sp["norm_ffn_g"]

    inv_freq = 1.0 / (ROPE_THETA ** (jnp.arange(0, HEAD, 2, dtype=F32) / HEAD))
    ang = pos.astype(F32)[:, None] * inv_freq
    cs, sn = jnp.cos(ang), jnp.sin(ang)
    cos = jnp.concatenate([cs, cs, cs, cs], axis=1)
    sin = jnp.concatenate([-sn, sn, -sn, sn], axis=1)
    head_of = jnp.arange(BW) // HEAD
    ones_bd = (head_of[:, None] == head_of[None, :]).astype(BF16)
    qg = jnp.tile(sp["b_q_norm_g"].reshape(1, HEAD), (1, N_HEADS))
    kg = jnp.tile(sp["b_k_norm_g"].reshape(1, HEAD), (1, N_HEADS))
    vg = sp["a_vnorm_g"].reshape(1, A_GROUPS * LANES)
    ws = sp["a_spatial_w"][0]
    bias_full = jnp.repeat(sp["a_spatial_b"][0].T, LANES, axis=1)
    ffn_s = [(sp["ffn_dw_w"][l], sp["ffn_dw_b"][l:l + 1]) for l in range(2)]

    h0 = _mod_first("l0_mod", x0, nm_g[0:1], sc_m[0], sh_m[0])
    z = _matmul("l0_in", h0, wb.get("w_in", h0), "nn", F32)
    ya = _mixa_fwd("l0_mixa", z, vg, ws, bias_full)
    qkv = _qkv_fwd("l0_qkv", z, cos, sin, ones_bd, qg, kg)
    qkv_sub, branches = [], []
    for dil in DILATIONS:
        sub = [_to_sub(t, dil) for t in qkv]
        o, l = _attn_fwd(f"l0_att{dil}", *sub, S // dil)
        qkv_sub.append(sub)
        branches.append((_from_sub(o, dil), _from_sub(l, dil)))
    yb, yb16, lse = _merge_fwd("l0_merge", branches)
    yab = jnp.concatenate([ya, yb16], axis=1)
    y0 = _matmul("l0_out", yab, wb.get("w_out", yab), "nn", F32)
    x1, h1 = _resid_mod("l0_res1", x0, y0, g_m[0], nf_g[0:1], sc_f[0], sh_f[0])
    ffn_w = [(wb.get("up0", h1), wb.get("dn0", h1), *ffn_s[0])]
    f0, (u0, hm0) = _ffn_fwd("l0_ffn", h1, *ffn_w[0])
    x2, h2 = _resid_mod("l0_res2", x1, f0, g_f[0], nm_g[1:2], sc_m[1], sh_m[1])
    p = _matmul("l1_pw1", h2, wb.get("pw1", h2), "nn", F32, bias=sp["conv_pw1_b"])
    y2 = _glu31_fwd("l1_glu", p, sp["conv_dw_w"][0], sp["conv_dw_b"])
    y4 = _ln_silu_fwd("l1_ln", y2, sp["conv_ln_g"], sp["conv_ln_b"])
    y1 = _matmul("l1_pw2", y4, wb.get("pw2", y4), "nn", F32, bias=sp["conv_pw2_b"])
    x3, h3 = _resid_mod("l1_res1", x2, y1, g_m[1], nf_g[1:2], sc_f[1], sh_f[1])
    ffn_w.append((wb.get("up1", h3), wb.get("dn1", h3), *ffn_s[1]))
    f1, (u1, hm1) = _ffn_fwd("l1_ffn", h3, *ffn_w[1])
    dx4, lossv = _loss_head("loss", x3, f1, g_f[1], tgt)

    dy, dgate_f1, _ = _gate_bwd("l1_gate2", dx4, f1, g_f[1])
    dh, gf1 = _ffn_bwd("l1_ffn", dy, h3, u1, hm1, *ffn_w[1])
    tok = pipe.scatter("g1", dict(dn1=gf1["dn"], up1=gf1["up"]))
    dx3, dsh_f1, dsc_f1, dnf1 = _mod_bwd("l1_dmod2", dx4, dh, x3, nf_g[1:2], sc_f[1] + tok)
    dy, dgate_m1, dpw2_b = _gate_bwd("l1_gate1", dx3, y1, g_m[1])
    dy4 = _matmul("l1_dpw2_x", dy, wb.get("pw2"), "nt", F32)
    g_pw2 = _matmul("l1_dpw2_w", y4, dy, "tn", BF16, tm=1024)
    dy2, dln_g, dln_b = _ln_silu_bwd("l1_dln", y2, dy4, sp["conv_ln_g"], sp["conv_ln_b"])
    dp, ddw_w, ddw_b, dpw1_b = _glu31_bwd("l1_dglu", p, dy2, sp["conv_dw_w"][0])
    g_pw1 = _matmul("l1_dpw1_w", h2, dp, "tn", BF16, tm=1024)
    tok = pipe.scatter("g2", dict(pw2=g_pw2, pw1=g_pw1))
    pipe.collect("g1", g_pw1)
    dh = _matmul("l1_dpw1_x", dp, wb.get("pw1"), "nt", F32, tn=1024)
    dx2, dsh_m1, dsc_m1, dnm1 = _mod_bwd("l1_dmod1", dx3, dh, x2, nm_g[1:2], sc_m[1] + tok)
    dy, dgate_f0, _ = _gate_bwd("l0_gate2", dx2, f0, g_f[0])
    dh, gf0 = _ffn_bwd("l0_ffn", dy, h1, u0, hm0, *ffn_w[0])
    tok = pipe.scatter("g3", dict(dn0=gf0["dn"], up0=gf0["up"]))
    pipe.finish("g1", gf0["up"])
    pipe.collect("g2", gf0["up"])
    dx1, dsh_f0, dsc_f0, dnf0 = _mod_bwd("l0_dmod2", dx2, dh, x1, nf_g[0:1], sc_f[0] + tok)
    dy, dgate_m0, _ = _gate_bwd("l0_gate1", dx1, y0, g_m[0])
    dyab = _matmul("l0_dout_x", dy, wb.get("w_out"), "nt", F32)
    g_out = _matmul("l0_dout_w", yab, dy, "tn", BF16, tm=1024)
    dza, dws, dbf, dvg = _mixa_bwd("l0_dmixa", z, dyab, vg, ws, bias_full)
    delta = _delta("l0_delta", dyab, yb, ones_bd)
    dyb = dyab[:, BW:]
    pieces = []
    for dil, sub in zip(DILATIONS, qkv_sub):
        outs = _attn_bwd(f"l0_datt{dil}", *sub, _to_sub(dyb, dil), _to_sub(lse, dil), _to_sub(delta, dil), S // dil)
        pieces.append([_from_sub(o, dil) for o in outs])
    dzb, dqg, dkg = _qkv_bwd("l0_dqkv", z, cos, sin, ones_bd, qg, kg, pieces)
    dz = jnp.concatenate([dza, dzb], axis=1)
    g_in = _matmul("l0_din_w", h0, dz, "tn", BF16, tm=1024)
    pipe.finish("g2", g_in)
    pipe.collect("g3", g_in)
    dh = _matmul("l0_din_x", dz, wb.get("w_in"), "nt", F32, tn=1024, tk=512)
    grad_x, dsh_m0, dsc_m0, dnm0 = _mod_bwd("l0_dmod1", dx1, dh, x0, nm_g[0:1], sc_m[0])

    dmod = jnp.stack([jnp.concatenate([dsh_m0, dsc_m0, dgate_m0, dsh_f0, dsc_f0, dgate_f0], axis=0),
                      jnp.concatenate([dsh_m1, dsc_m1, dgate_m1, dsh_f1, dsc_f1, dgate_f1], axis=0)])
    small = dict(
        norm_mix_g=jnp.concatenate([dnm0, dnm1], axis=0),
        norm_ffn_g=jnp.concatenate([dnf0, dnf1], axis=0),
        a_vnorm_g=dvg.reshape(1, A_GROUPS, LANES),
        a_spatial_w=dws[None],
        a_spatial_b=dbf.reshape(CHUNK, A_GROUPS, LANES).sum(-1).T[None],
        b_q_norm_g=dqg.reshape(N_HEADS, HEAD).sum(0)[None],
        b_k_norm_g=dkg.reshape(N_HEADS, HEAD).sum(0)[None],
        conv_pw1_b=dpw1_b, conv_dw_w=ddw_w[None, :sp["conv_dw_w"].shape[1]], conv_dw_b=ddw_b,
        conv_ln_g=dln_g, conv_ln_b=dln_b, conv_pw2_b=dpw2_b,
        ffn_dw_w=jnp.stack([gf0["dw_w"], gf1["dw_w"]]),
        ffn_dw_b=jnp.concatenate([gf0["dw_b"], gf1["dw_b"]], axis=0),
    )
    return lossv, grad_x, small, dmod, dict(w_out=g_out, w_in=g_in)


ADA_TN = 512


def _ada_fwd(name, c_all, ada_w, ada_b_sh):
    L, D, N = ada_w.shape
    B = c_all.shape[0]

    def body(c_ref, w_ref, b_ref, o_ref):
        cv = c_ref[...]
        ca = (cv * _sigmoid(cv)).astype(BF16)
        o_ref[0] = jnp.dot(ca, w_ref[0].astype(BF16), preferred_element_type=F32) + b_ref[0]

    return pl.pallas_call(
        body, name=name, grid=(L, N // ADA_TN),
        in_specs=[pl.BlockSpec((B, D), lambda l, j: (0, 0)), pl.BlockSpec((1, D, ADA_TN), lambda l, j: (l, 0, j)),
                  pl.BlockSpec((1, 1, ADA_TN), lambda l, j: (l, 0, j))],
        out_specs=pl.BlockSpec((1, B, ADA_TN), lambda l, j: (l, 0, j)),
        out_shape=jax.ShapeDtypeStruct((L, B, N), F32),
        compiler_params=_params(("parallel", "parallel")),
    )(c_all, ada_w, ada_b_sh.reshape(L, 1, N))


def _adamw_val(w, g, m, v):
    m2 = ADAM_B1 * m + (1.0 - ADAM_B1) * g
    v2 = ADAM_B2 * v + (1.0 - ADAM_B2) * (g * g)
    m_hat = m2 / (1.0 - ADAM_B1 ** ADAM_STEP)
    v_hat = v2 / (1.0 - ADAM_B2 ** ADAM_STEP)
    delta = -ADAM_LR * (m_hat / (jnp.sqrt(v_hat) + ADAM_EPS) + ADAM_WD * w)
    return delta, m2, v2


def _ada_update(name, c_all, dmod_sh, w, m, v):
    L, D, N = w.shape
    B = c_all.shape[0]

    def body(c_ref, d_ref, w_ref, m_ref, v_ref, g_ref, dl_ref, mo_ref, vo_ref):
        cv = c_ref[...]
        ca = (cv * _sigmoid(cv)).astype(BF16)
        g = lax.dot_general(ca, d_ref[0].astype(BF16), _DN["tn"], preferred_element_type=F32)
        g_ref[0] = g
        dl_ref[0], mo_ref[0], vo_ref[0] = _adamw_val(w_ref[0], g, m_ref[0], v_ref[0])

    wspec = pl.BlockSpec((1, D, ADA_TN), lambda l, j: (l, 0, j))
    return pl.pallas_call(
        body, name=name, grid=(L, N // ADA_TN),
        in_specs=[pl.BlockSpec((B, D), lambda l, j: (0, 0)), pl.BlockSpec((1, B, ADA_TN), lambda l, j: (l, 0, j)),
                  wspec, wspec, wspec],
        out_specs=[wspec] * 4, out_shape=[jax.ShapeDtypeStruct((L, D, N), F32)] * 4,
        compiler_params=_params(("parallel", "parallel")),
    )(c_all, dmod_sh, w, m, v)


def _adamw(name, w, g, m, v):
    R, C = w.shape
    tm = R
    for cand in (256, 128, 64, 32, 16, 8):
        if R % cand == 0 and cand * C * 4 <= (1 << 20):
            tm = cand
            break

    def body(w_ref, g_ref, m_ref, v_ref, d_ref, mo_ref, vo_ref):
        d_ref[...], mo_ref[...], vo_ref[...] = _adamw_val(w_ref[...], g_ref[...], m_ref[...], v_ref[...])

    return _rowcall(name, body, R, tm, [(w, ROW), (g, ROW), (m, ROW), (v, ROW)], [((R, C), F32, ROW)] * 3)


def _row_tile(rows, width, itemsize=4, limit=1 << 20):
    for cand in (512, 256, 128, 64, 32, 16):
        if rows % cand == 0 and cand * width * itemsize <= limit:
            return cand
    raise ValueError((rows, width))


def _cast_bf16(name, a, layer):
    L, R, C = a.shape
    tm = _row_tile(R, C)

    def body(a_ref, o_ref):
        o_ref[...] = a_ref[0].astype(BF16)

    return pl.pallas_call(
        body, name=name, grid=(R // tm,), in_specs=[pl.BlockSpec((1, tm, C), lambda i: (layer, i, 0))],
        out_specs=pl.BlockSpec((tm, C), lambda i: (i, 0)), out_shape=jax.ShapeDtypeStruct((R, C), BF16),
        compiler_params=_params(("parallel",)),
    )(a)


def _sum4(name, g, rcv, q, kind, n):
    r, c = rcv.shape[1:]
    tm = _row_tile(r, c)
    if kind == "col":
        g_spec = pl.BlockSpec((tm, n), lambda i, q_ref: (i, q_ref[0]))
    else:
        g_spec = pl.BlockSpec((tm, c), lambda i, q_ref: (q_ref[0] * (n // tm) + i, 0))

    def body(q_ref, g_ref, r_ref, o_ref):
        acc = g_ref[...].astype(F32)
        for j in range(3):
            acc = acc + r_ref[j].astype(F32)
        o_ref[...] = acc

    return pl.pallas_call(
        body, name=name,
        grid_spec=pltpu.PrefetchScalarGridSpec(
            num_scalar_prefetch=1, grid=(r // tm,),
            in_specs=[g_spec, pl.BlockSpec((3, tm, c), lambda i, q_ref: (0, i, 0))],
            out_specs=pl.BlockSpec((tm, c), lambda i, q_ref: (i, 0))),
        out_shape=jax.ShapeDtypeStruct((r, c), F32), compiler_params=_params(("parallel",)),
    )(q.reshape(1).astype(jnp.int32), g, rcv)


def _adamw_sum(name, w, m, v, layer, mine, theirs, prev):
    L, r, c = w.shape
    tm = _row_tile(r, c, limit=1 << 19)
    lay = pl.BlockSpec((1, tm, c), lambda i: (layer, i, 0))
    flat = pl.BlockSpec((tm, c), lambda i: (i, 0))
    n_prev = 0 if prev is None else 4

    def body(w_ref, m_ref, v_ref, a_ref, b_ref, *rest):
        g_ref, d_ref, mo_ref, vo_ref = rest[n_prev:]
        g = a_ref[...] + b_ref[...]
        g_ref[0] = g
        d_ref[0], mo_ref[0], vo_ref[0] = _adamw_val(w_ref[0], g, m_ref[0], v_ref[0])

    return pl.pallas_call(
        body, name=name, grid=(r // tm,),
        in_specs=[lay, lay, lay, flat, flat] + [ANY] * n_prev, out_specs=[lay] * 4,
        out_shape=[jax.ShapeDtypeStruct((L, r, c), F32)] * 4,
        input_output_aliases={5 + k: k for k in range(n_prev)},
        compiler_params=_params(("parallel",)),
    )(w, m, v, mine, theirs, *(prev or ()))


def _sum8(name, gathered):
    R, N = gathered.shape
    P = R // 8

    def body(g_ref, o_ref):
        acc = g_ref[0:P, :]
        for d in range(1, 8):
            acc = acc + g_ref[d * P:(d + 1) * P, :]
        o_ref[...] = acc

    return pl.pallas_call(body, name=name, out_shape=jax.ShapeDtypeStruct((P, N), F32),
                          compiler_params=pltpu.CompilerParams(vmem_limit_bytes=VMEM_LIMIT))(gathered)


ANY = pl.BlockSpec(memory_space=pl.ANY)


def _mesh_pos():
    x, y, c = lax.axis_index("x"), lax.axis_index("y"), lax.axis_index("c")
    other_chips = [(1 - x, y), (x, 1 - y), (1 - x, 1 - y)]
    return x, y, c, other_chips


def _allgather8(name, blk):
    m_per, n = blk.shape

    def body(x_ref, out_ref, send_sems, recv_sems, local_sem):
        x, y, c, chips = _mesh_pos()
        me, sibling = (x, y, c), (x, y, 1 - c)

        def rows(px, py, pc):
            return out_ref.at[pl.ds((4 * px + 2 * py + pc) * m_per, m_per), :]

        def copy(k, block, to, src=None):
            return pltpu.make_async_remote_copy(
                src_ref=rows(*block) if src is None else src, dst_ref=rows(*block),
                send_sem=send_sems.at[k], recv_sem=recv_sems.at[k], device_id=to, device_id_type=MESH)

        mine = pltpu.make_async_copy(x_ref, rows(*me), local_sem)
        mine.start()
        first = [copy(0, me, sibling, src=x_ref)]
        first += [copy(1 + j, me, (*chip, c), src=x_ref) for j, chip in enumerate(chips)]
        for cp in first:
            cp.start()
        passed = [copy(4 + j, (*chip, c), sibling) for j, chip in enumerate(chips)]
        for j, chip in enumerate(chips):
            copy(1 + j, (*chip, c), me).wait_recv()
            passed[j].start()
        copy(0, sibling, me).wait_recv()
        for j, chip in enumerate(chips):
            copy(4 + j, (*chip, 1 - c), me).wait_recv()
        for cp in first + passed:
            cp.wait_send()
        mine.wait()

    return pl.pallas_call(
        body, name=name, out_shape=jax.ShapeDtypeStruct((8 * m_per, n), blk.dtype),
        in_specs=[pl.BlockSpec(memory_space=pltpu.VMEM)], out_specs=pl.BlockSpec(memory_space=pltpu.VMEM),
        scratch_shapes=[pltpu.SemaphoreType.DMA((7,)), pltpu.SemaphoreType.DMA((7,)), pltpu.SemaphoreType.DMA],
        compiler_params=pltpu.CompilerParams(vmem_limit_bytes=VMEM_LIMIT),
    )(blk)


BIG = dict(w_in=("col", "ab_w_in", 0), w_out=("row", "ab_w_out", 0), up0=("col", "ffn_up_w", 0),
           dn0=("row", "ffn_down_w", 0), pw1=("col", "conv_pw1_w", 0), pw2=("row", "conv_pw2_w", 0),
           up1=("col", "ffn_up_w", 1), dn1=("row", "ffn_down_w", 1))
N_CHIPS = 4
HBM = pl.BlockSpec(memory_space=pltpu.HBM)
SEM = pl.BlockSpec(memory_space=pltpu.SEMAPHORE)
EFFECT = pltpu.SideEffectType.DATAFLOW_SIDE_EFFECTING


def _region(kind, ref, q, n):
    if kind == "col":
        return ref.at[:, pl.ds(q * n, n)]
    return ref.at[pl.ds(q * n, n), :]


def _gather_plan(kind, n):
    def remote(src, land, pos):
        x, y, c, chips = pos
        return [(src, _region(kind, land, 2 * x + y, n), (*chip, c)) for chip in chips]

    def local(src, land, pos):
        x, y, _, _ = pos
        return src, _region(kind, land, 2 * x + y, n)

    return ("gather", kind, n), remote, local


def _scatter_plan(kind, n):
    def remote(src, land, pos):
        _, _, c, chips = pos
        return [(_region(kind, src, 2 * chip[0] + chip[1], n), land.at[j], (*chip, c)) for j, chip in enumerate(chips)]

    return ("scatter", kind, n), remote, None


def _sibling_plan():
    def remote(src, land, pos):
        x, y, c, _ = pos
        return [(src, land, (x, y, 1 - c))]

    return ("sibling",), remote, None


def _split_start(name, items, after=None):
    n = len(items)
    plans = [it[2] for it in items]
    n_in = 2 * n + (after is not None)

    def body(*refs):
        srcs, lands = refs[:n], refs[n:2 * n]
        sends, recvs = refs[n_in:n_in + n], refs[n_in + n:n_in + 2 * n]
        token, local_sems = refs[n_in + 4 * n], refs[n_in + 4 * n + 1]
        pos = _mesh_pos()
        local = []
        for a, (_, _, loc) in enumerate(plans):
            if loc is not None:
                s, d = loc(srcs[a], lands[a], pos)
                local.append(pltpu.make_async_copy(s, d, local_sems.at[a]))
                local[-1].start()
        for a, (_, remote, _) in enumerate(plans):
            for k, (s, d, dev) in enumerate(remote(srcs[a], lands[a], pos)):
                pltpu.make_async_remote_copy(src_ref=s, dst_ref=d, send_sem=sends[a].at[k], recv_sem=recvs[a].at[k],
                                             device_id=dev, device_id_type=MESH).start()
        for cp in local:
            cp.wait()
        token[...] = jnp.zeros_like(token)

    sems = [pltpu.SemaphoreType.DMA((it[3],)) for it in items]
    bufs = [pltpu.HBM(it[k].shape, it[k].dtype) for k in (0, 1) for it in items]
    outs = pl.pallas_call(
        body, name=name, out_shape=[*sems, *sems, *bufs, jax.ShapeDtypeStruct((8, LANES), F32)],
        in_specs=[HBM] * (2 * n) + [ANY] * (n_in - 2 * n),
        out_specs=[SEM] * (2 * n) + [HBM] * (2 * n) + [pl.BlockSpec(memory_space=pltpu.VMEM)],
        input_output_aliases={i: 2 * n + i for i in range(2 * n)},
        scratch_shapes=[pltpu.SemaphoreType.DMA((n,))],
        compiler_params=pltpu.CompilerParams(has_side_effects=EFFECT),
    )(*[pltpu.with_memory_space_constraint(it[k], pltpu.HBM) for k in (0, 1) for it in items],
      *(() if after is None else (after,)))
    state = [(items[a][2], items[a][3], outs[2 * n + a], outs[3 * n + a], outs[a], outs[n + a]) for a in range(n)]
    return state, outs[4 * n]


def _split_wait(name, state, after):
    n = len(state)

    def body(*refs):
        srcs, lands = refs[:n], refs[n:2 * n]
        sends, recvs = refs[2 * n:3 * n], refs[3 * n:4 * n]
        pos = _mesh_pos()
        for a, ((_, remote, _), *_) in enumerate(state):
            for k, (s, d, dev) in enumerate(remote(srcs[a], lands[a], pos)):
                cp = pltpu.make_async_remote_copy(src_ref=s, dst_ref=d, send_sem=sends[a].at[k], recv_sem=recvs[a].at[k],
                                                  device_id=dev, device_id_type=MESH)
                cp.wait_send()
                cp.wait_recv()

    bufs = [st[k] for k in (2, 3) for st in state]
    outs = pl.pallas_call(
        body, name=name, out_shape=[pltpu.HBM(b.shape, b.dtype) for b in bufs],
        in_specs=[HBM] * (2 * n) + [SEM] * (2 * n) + [ANY], out_specs=[HBM] * (2 * n),
        input_output_aliases={i: i for i in range(2 * n)},
        compiler_params=pltpu.CompilerParams(has_side_effects=EFFECT),
    )(*bufs, *[st[k] for k in (4, 5) for st in state], after)
    return outs[:n], outs[n:]


class _Weights:
    def __init__(self, w, after):
        items = []
        for name, (kind, pname, layer) in BIG.items():
            shard = _cast_bf16(f"cast_{name}", w[pname], layer)
            r, c = shard.shape
            full = (r, N_CHIPS * c) if kind == "col" else (N_CHIPS * r, c)
            items.append((shard, lax.empty(full, BF16), _gather_plan(kind, c if kind == "col" else r), N_CHIPS - 1))
        state, self.token = _split_start("gw_start", items, after)
        self.pending = dict(zip(BIG, state))
        self.ready = {}

    def get(self, name, after=None):
        if name not in self.ready:
            self.ready[name] = _split_wait(f"gw_wait_{name}", [self.pending.pop(name)], after)[1][0]
        return self.ready[name]


class _GradPipe:
    def __init__(self, q, w, m, v):
        self.q, self.w, self.m, self.v = q, w, m, v
        self.stage, self.results = {}, {}

    def scatter(self, group, grads, after=None):
        items = []
        for name, g in grads.items():
            kind = BIG[name][0]
            rows, cols = g.shape
            n = (cols if kind == "col" else rows) // N_CHIPS
            reg = (rows, n) if kind == "col" else (n, cols)
            items.append((g, lax.empty((N_CHIPS - 1, *reg), BF16), _scatter_plan(kind, n), N_CHIPS - 1))
        state, token = _split_start(f"gs_start_{group}", items, after)
        self.stage[group] = (list(grads), state)
        return token[0, 0]

    def collect(self, group, after):
        names, state = self.stage[group]
        srcs, lands = _split_wait(f"gs_wait_{group}", state, after)
        items = []
        for name, st, g, land in zip(names, state, srcs, lands):
            _, kind, n = st[0][0]
            part = _sum4(f"sum_{name}", g, land, self.q, kind, n)
            items.append((part, lax.empty(part.shape, F32), _sibling_plan(), 1))
        self.stage[group] = (names, _split_start(f"sw_start_{group}", items)[0])

    def finish(self, group, after):
        names, state = self.stage.pop(group)
        srcs, lands = _split_wait(f"sw_wait_{group}", state, after)
        for name, mine, theirs in zip(names, srcs, lands):
            _, pname, layer = BIG[name]
            self.results[pname] = _adamw_sum(f"adamw_{name}", self.w[pname], self.m[pname], self.v[pname], layer,
                                             mine, theirs, self.results.get(pname))


PACK_ROWS = 8


def _pack(arrays):
    flat = jnp.concatenate([a.reshape(-1) for a in arrays])
    n = flat.shape[0]
    padded = -(-n // (PACK_ROWS * LANES)) * (PACK_ROWS * LANES)
    return jnp.pad(flat, (0, padded - n)).reshape(PACK_ROWS, padded // PACK_ROWS)


def _unpack(packed, shapes):
    flat = packed.reshape(-1)
    out, off = [], 0
    for s in shapes:
        n = 1
        for d in s:
            n *= d
        out.append(flat[off:off + n].reshape(s))
        off += n
    return out


REPLICATED = ("ada_b", "norm_mix_g", "norm_ffn_g", "a_vnorm_g", "a_spatial_w", "a_spatial_b", "b_q_norm_g",
              "b_k_norm_g", "ffn_dw_b")
SMALL_SHARDED = ("conv_pw1_b", "conv_dw_w", "conv_dw_b", "conv_ln_g", "conv_ln_b", "conv_pw2_b", "ffn_dw_w")
WEIGHTS = ("ada_w", "ada_b", "norm_mix_g", "norm_ffn_g", "ab_w_in", "a_vnorm_g", "a_spatial_w", "a_spatial_b",
           "b_q_norm_g", "b_k_norm_g", "ab_w_out", "conv_pw1_w", "conv_pw1_b", "conv_dw_w", "conv_dw_b", "conv_ln_g",
           "conv_ln_b", "conv_pw2_w", "conv_pw2_b", "ffn_up_w", "ffn_dw_w", "ffn_dw_b", "ffn_down_w")

def kernel(x, c, positions, ada_w, ada_b, norm_mix_g, norm_ffn_g, ab_w_in, a_vnorm_g, a_spatial_w, a_spatial_b, b_q_norm_g, b_k_norm_g, ab_w_out, conv_pw1_w, conv_pw1_b, conv_dw_w, conv_dw_b, conv_ln_g, conv_ln_b, conv_pw2_w, conv_pw2_b, ffn_up_w, ffn_dw_w, ffn_dw_b, ffn_down_w, loss_target, m_ada_w, m_ada_b, m_norm_mix_g, m_norm_ffn_g, m_ab_w_in, m_a_vnorm_g, m_a_spatial_w, m_a_spatial_b, m_b_q_norm_g, m_b_k_norm_g, m_ab_w_out, m_conv_pw1_w, m_conv_pw1_b, m_conv_dw_w, m_conv_dw_b, m_conv_ln_g, m_conv_ln_b, m_conv_pw2_w, m_conv_pw2_b, m_ffn_up_w, m_ffn_dw_w, m_ffn_dw_b, m_ffn_down_w, v_ada_w, v_ada_b, v_norm_mix_g, v_norm_ffn_g, v_ab_w_in, v_a_vnorm_g, v_a_spatial_w, v_a_spatial_b, v_b_q_norm_g, v_b_k_norm_g, v_ab_w_out, v_conv_pw1_w, v_conv_pw1_b, v_conv_dw_w, v_conv_dw_b, v_conv_ln_g, v_conv_ln_b, v_conv_pw2_w, v_conv_pw2_b, v_ffn_up_w, v_ffn_dw_w, v_ffn_dw_b, v_ffn_down_w):
    w = dict(ada_w=ada_w, ada_b=ada_b, norm_mix_g=norm_mix_g, norm_ffn_g=norm_ffn_g, ab_w_in=ab_w_in, a_vnorm_g=a_vnorm_g, a_spatial_w=a_spatial_w, a_spatial_b=a_spatial_b, b_q_norm_g=b_q_norm_g, b_k_norm_g=b_k_norm_g, ab_w_out=ab_w_out, conv_pw1_w=conv_pw1_w, conv_pw1_b=conv_pw1_b, conv_dw_w=conv_dw_w, conv_dw_b=conv_dw_b, conv_ln_g=conv_ln_g, conv_ln_b=conv_ln_b, conv_pw2_w=conv_pw2_w, conv_pw2_b=conv_pw2_b, ffn_up_w=ffn_up_w, ffn_dw_w=ffn_dw_w, ffn_dw_b=ffn_dw_b, ffn_down_w=ffn_down_w)
    m = dict(ada_w=m_ada_w, ada_b=m_ada_b, norm_mix_g=m_norm_mix_g, norm_ffn_g=m_norm_ffn_g, ab_w_in=m_ab_w_in, a_vnorm_g=m_a_vnorm_g, a_spatial_w=m_a_spatial_w, a_spatial_b=m_a_spatial_b, b_q_norm_g=m_b_q_norm_g, b_k_norm_g=m_b_k_norm_g, ab_w_out=m_ab_w_out, conv_pw1_w=m_conv_pw1_w, conv_pw1_b=m_conv_pw1_b, conv_dw_w=m_conv_dw_w, conv_dw_b=m_conv_dw_b, conv_ln_g=m_conv_ln_g, conv_ln_b=m_conv_ln_b, conv_pw2_w=m_conv_pw2_w, conv_pw2_b=m_conv_pw2_b, ffn_up_w=m_ffn_up_w, ffn_dw_w=m_ffn_dw_w, ffn_dw_b=m_ffn_dw_b, ffn_down_w=m_ffn_down_w)
    v = dict(ada_w=v_ada_w, ada_b=v_ada_b, norm_mix_g=v_norm_mix_g, norm_ffn_g=v_norm_ffn_g, ab_w_in=v_ab_w_in, a_vnorm_g=v_a_vnorm_g, a_spatial_w=v_a_spatial_w, a_spatial_b=v_a_spatial_b, b_q_norm_g=v_b_q_norm_g, b_k_norm_g=v_b_k_norm_g, ab_w_out=v_ab_w_out, conv_pw1_w=v_conv_pw1_w, conv_pw1_b=v_conv_pw1_b, conv_dw_w=v_conv_dw_w, conv_dw_b=v_conv_dw_b, conv_ln_g=v_conv_ln_g, conv_ln_b=v_conv_ln_b, conv_pw2_w=v_conv_pw2_w, conv_pw2_b=v_conv_pw2_b, ffn_up_w=v_ffn_up_w, ffn_dw_w=v_ffn_dw_w, ffn_dw_b=v_ffn_dw_b, ffn_down_w=v_ffn_down_w)
    S, D = x.shape[1], x.shape[2]
    xi, yi, ci = lax.axis_index("x"), lax.axis_index("y"), lax.axis_index("c")
    q = 2 * xi + yi
    b = 2 * q + ci
    take_dev = lambda g: g.reshape(8, PACK_ROWS, -1)

    c_all = _allgather8("ag_c", c.reshape(PACK_ROWS, D // PACK_ROWS)).reshape(8, D)
    n_ada = ada_w.shape[2]
    mod_sh = _ada_fwd("ada_fwd", c_all, ada_w, lax.dynamic_slice_in_dim(ada_b, q * n_ada, n_ada, axis=1))
    sh_shapes = [mod_sh.shape] + [w[n].shape for n in SMALL_SHARDED]
    gathered_mod = _allgather8("ag_mod", _pack([mod_sh] + [w[n] for n in SMALL_SHARDED]))
    per_chip = [_unpack(blk, sh_shapes) for blk in take_dev(gathered_mod)[0::2]]
    mod_g = jnp.stack([pc[0] for pc in per_chip])
    mod_mine = lax.dynamic_index_in_dim(mod_g, b, axis=2, keepdims=False)
    mod = mod_mine.transpose(1, 0, 2).reshape(2, 6, D)
    sp = {n: jnp.concatenate([pc[1 + i] for pc in per_chip], axis=-1) for i, n in enumerate(SMALL_SHARDED)}
    sp.update({n: w[n] for n in REPLICATED if n != "ada_b"})

    wb = _Weights(w, gathered_mod)
    mod = mod + wb.token[0, 0]

    pipe = _GradPipe(q, w, m, v)
    lossv, grad_x, small, dmod, last = _local_step(x[0], loss_target[0], positions[0], mod, wb, sp, pipe)
    loss = lax.psum(0.5 * jnp.sum(lossv) / D, ("x", "y", "c"))

    small_names = [n for n in REPLICATED if n != "ada_b"] + list(SMALL_SHARDED)
    payload = [dmod.reshape(2, 6 * D)] + [small[n] for n in small_names]
    pay_shapes = [p.shape for p in payload]
    gathered = _allgather8("ag_grads", _pack(payload))
    pipe.scatter("g4", last, gathered)
    totals = _unpack(_sum8("sum_grads", gathered), pay_shapes)
    grads = dict(zip(["ada_b"] + small_names, totals))
    for n in SMALL_SHARDED:
        n_sh = w[n].shape[-1]
        grads[n] = lax.dynamic_slice_in_dim(grads[n], q * n_sh, n_sh, axis=grads[n].ndim - 1)
    dmod_all = take_dev(gathered).reshape(8, -1)[:, :2 * 6 * D].reshape(8, 2, 6 * D)
    dmod_sh = lax.dynamic_slice_in_dim(dmod_all, q * n_ada, n_ada, axis=2).transpose(1, 0, 2)

    pipe.finish("g3", dmod_sh)
    pipe.collect("g4", dmod_sh)
    grads["ada_w"], delta_ada, m_ada, v_ada = _ada_update("ada_update", c_all, dmod_sh, ada_w, m_ada_w, v_ada_w)
    delta, new_m, new_v = dict(ada_w=delta_ada), dict(ada_w=m_ada), dict(ada_w=v_ada)
    rest = list(REPLICATED) + list(SMALL_SHARDED)
    rest_shapes = [w[n].shape for n in rest]
    outs = _adamw("adamw_small", *[_pack([src[n].reshape(w[n].shape) for n in rest]) for src in (w, grads, m, v)])
    for tgt, packed in zip((delta, new_m, new_v), outs):
        tgt.update(dict(zip(rest, _unpack(packed, rest_shapes))))
    for n in rest:
        grads[n] = grads[n].reshape(w[n].shape)
    pipe.finish("g4", outs[0])
    for n, res in pipe.results.items():
        grads[n], delta[n], new_m[n], new_v[n] = res

    return (loss, grad_x[None], *[grads[n] for n in WEIGHTS], *[delta[n] for n in WEIGHTS],
            *[new_m[n] for n in WEIGHTS], *[new_v[n] for n in WEIGHTS])
```

```python
import functools

import jax
import jax.numpy as jnp
from jax import lax
from jax.experimental import pallas as pl
from jax.experimental.pallas import tpu as pltpu

F32, BF16 = jnp.float32, jnp.bfloat16
EPS = 1e-6
NEG = -1e30
ROPE_THETA = 10000.0
LANES = 128
VMEM_LIMIT = 56 * 1024 * 1024
ADAM_LR, ADAM_B1, ADAM_B2, ADAM_EPS, ADAM_WD, ADAM_STEP = 0.001, 0.9, 0.999, 1e-08, 0.01, 10
MESH = pl.DeviceIdType.MESH


def _params(sem):
    return pltpu.CompilerParams(dimension_semantics=sem, vmem_limit_bytes=VMEM_LIMIT)


_DN = {"nn": (((1,), (0,)), ((), ())), "nt": (((1,), (1,)), ((), ())), "tn": (((0,), (0,)), ((), ()))}


def _matmul(name, a, b, mode, out_dtype, tm=512, tn=512, tk=1024, bias=None):
    if mode == "nn":
        (M, K), N = a.shape, b.shape[1]
    elif mode == "nt":
        (M, K), N = a.shape, b.shape[0]
    else:
        (K, M), N = a.shape, b.shape[1]
    tm, tn, tk = min(tm, M), min(tn, N), min(tk, K)
    assert M % tm == 0 and N % tn == 0 and K % tk == 0, (name, M, N, K, tm, tn, tk)
    nk = K // tk
    if mode == "tn":
        a_spec = pl.BlockSpec((tk, tm), lambda i, j, k: (k, i))
    else:
        a_spec = pl.BlockSpec((tm, tk), lambda i, j, k: (i, k))
    if mode == "nt":
        b_spec = pl.BlockSpec((tn, tk), lambda i, j, k: (j, k))
    else:
        b_spec = pl.BlockSpec((tk, tn), lambda i, j, k: (k, j))
    in_specs, args = [a_spec, b_spec], [a, b]
    if bias is not None:
        in_specs.append(pl.BlockSpec((1, tn), lambda i, j, k: (0, j)))
        args.append(bias)
    n_in = len(args)

    def body(*refs):
        a_ref, b_ref, o_ref = refs[0], refs[1], refs[n_in]
        p = lax.dot_general(a_ref[...], b_ref[...], _DN[mode], preferred_element_type=F32)

        def finish(acc):
            if bias is not None:
                acc = acc + refs[2][...]
            o_ref[...] = acc.astype(o_ref.dtype)

        if nk == 1:
            finish(p)
        else:
            acc_ref = refs[n_in + 1]
            k = pl.program_id(2)

            @pl.when(k == 0)
            def _():
                acc_ref[...] = p

            @pl.when(k > 0)
            def _():
                acc_ref[...] += p

            @pl.when(k == nk - 1)
            def _():
                finish(acc_ref[...])

    return pl.pallas_call(
        body, name=name, grid=(M // tm, N // tn, nk), in_specs=in_specs,
        out_specs=pl.BlockSpec((tm, tn), lambda i, j, k: (i, j)),
        out_shape=jax.ShapeDtypeStruct((M, N), out_dtype),
        scratch_shapes=[pltpu.VMEM((tm, tn), F32)] if nk > 1 else [],
        compiler_params=_params(("parallel", "parallel", "arbitrary")),
    )(*args)


def _rowcall(name, body, nrows, tm, ins, outs):
    nblk = nrows // tm
    assert nrows % tm == 0

    def spec(kind, shape):
        k = kind[0]
        if k == "row":
            cw, cb = kind[1] or shape[-1], kind[2]
            return pl.BlockSpec((tm, cw), lambda i: (i, cb))
        if k == "prev":
            hb, cw, cb = kind[1], kind[2] or shape[-1], kind[3]
            r = tm // hb
            return pl.BlockSpec((hb, cw), lambda i: (jnp.maximum(i * r - 1, 0), cb))
        if k == "next":
            hb, cw, cb = kind[1], kind[2] or shape[-1], kind[3]
            r, last = tm // hb, nrows // hb - 1
            return pl.BlockSpec((hb, cw), lambda i: (jnp.minimum((i + 1) * r, last), cb))
        if k == "off":
            off, cw, cb = kind[1], kind[2] or shape[-1], kind[3]
            return pl.BlockSpec((tm, cw), lambda i: (jnp.clip(i + off, 0, nblk - 1), cb))
        nd = len(shape)
        return pl.BlockSpec(tuple(shape), lambda i: (0,) * nd)

    has_acc = any(o[2][0] == "acc" for o in outs)
    return pl.pallas_call(
        body, name=name, grid=(nblk,),
        in_specs=[spec(kind, a.shape) for a, kind in ins],
        out_specs=[spec(kind, shape) for shape, _, kind in outs],
        out_shape=[jax.ShapeDtypeStruct(tuple(shape), dt) for shape, dt, _ in outs],
        compiler_params=_params(("arbitrary",) if has_acc else ("parallel",)),
    )(*[a for a, _ in ins])


ROW = ("row", None, 0)
FULL = ("full",)
ACC = ("acc",)


def _colsum(x):
    return jnp.sum(x, axis=0, keepdims=True)


def _acc_add(i, ref, val, rows=None):
    idx = (slice(None),) * len(ref.shape) if rows is None else rows

    @pl.when(i == 0)
    def _():
        ref[idx] = val

    @pl.when(i > 0)
    def _():
        ref[idx] = ref[idx] + val


def _sigmoid(x):
    return 1.0 / (1.0 + jnp.exp(-x))


def _gelu(x):
    return 0.5 * x * (1.0 + lax.erf(x * (2.0 ** -0.5)))


def _gelu_grad(x):
    return 0.5 * (1.0 + lax.erf(x * (2.0 ** -0.5))) + x * jnp.exp(-0.5 * x * x) * ((2.0 * jnp.pi) ** -0.5)


def _shift_prev(ext, s, hb):
    if s == 0:
        return ext[hb:]
    return pltpu.roll(ext, s, axis=0)[hb:]


def _shift_next(ext, s, tm):
    if s == 0:
        return ext[:tm]
    return pltpu.roll(ext, ext.shape[0] - s, axis=0)[:tm]


def _rms_mod_val(x, g, sc, sh):
    r = lax.rsqrt(jnp.mean(x * x, axis=-1, keepdims=True) + EPS)
    return x * r * g * (1.0 + sc) + sh


def _mod_first(name, x, g, sc, sh, tm=256):
    S, D = x.shape

    def body(x_ref, g_ref, sc_ref, sh_ref, h_ref):
        h_ref[...] = _rms_mod_val(x_ref[...], g_ref[...], sc_ref[...], sh_ref[...]).astype(BF16)

    return _rowcall(name, body, S, tm, [(x, ROW), (g, FULL), (sc, FULL), (sh, FULL)], [((S, D), BF16, ROW)])[0]


def _resid_mod(name, x, y, gate, g, sc, sh, tm=256):
    S, D = x.shape

    def body(x_ref, y_ref, gate_ref, g_ref, sc_ref, sh_ref, xo_ref, h_ref):
        xn = x_ref[...] + gate_ref[...] * y_ref[...]
        xo_ref[...] = xn
        h_ref[...] = _rms_mod_val(xn, g_ref[...], sc_ref[...], sh_ref[...]).astype(BF16)

    return _rowcall(name, body, S, tm,
                    [(x, ROW), (y, ROW), (gate, FULL), (g, FULL), (sc, FULL), (sh, FULL)],
                    [((S, D), F32, ROW), ((S, D), BF16, ROW)])


def _loss_head(name, x, y, gate, tgt, tm=256):
    S, D = x.shape

    def body(x_ref, y_ref, gate_ref, t_ref, dx_ref, l_ref):
        i = pl.program_id(0)
        err = x_ref[...] + gate_ref[...] * y_ref[...] - t_ref[...]
        dx_ref[...] = err * (1.0 / D)
        _acc_add(i, l_ref, _colsum(err * err))

    return _rowcall(name, body, S, tm, [(x, ROW), (y, ROW), (gate, FULL), (tgt, ROW)],
                    [((S, D), F32, ROW), ((1, D), F32, ACC)])


def _gate_bwd(name, dxo, y, gate, tm=256):
    S, D = dxo.shape

    def body(d_ref, y_ref, gate_ref, dy_ref, dg_ref, db_ref):
        i = pl.program_id(0)
        d = d_ref[...]
        dy = d * gate_ref[...]
        dy_ref[...] = dy.astype(BF16)
        _acc_add(i, dg_ref, _colsum(d * y_ref[...]))
        _acc_add(i, db_ref, _colsum(dy))

    return _rowcall(name, body, S, tm, [(dxo, ROW), (y, ROW), (gate, FULL)],
                    [((S, D), BF16, ROW), ((1, D), F32, ACC), ((1, D), F32, ACC)])


def _mod_bwd(name, dxo, dh, x, g, sc, tm=256):
    S, D = x.shape

    def body(d_ref, dh_ref, x_ref, g_ref, sc_ref, dx_ref, dsh_ref, dsc_ref, dg_ref):
        i = pl.program_id(0)
        xv, dh_v, gv = x_ref[...], dh_ref[...], g_ref[...]
        r = lax.rsqrt(jnp.mean(xv * xv, axis=-1, keepdims=True) + EPS)
        n = xv * r
        _acc_add(i, dsh_ref, _colsum(dh_v))
        _acc_add(i, dsc_ref, _colsum(dh_v * (n * gv)))
        dy = dh_v * (1.0 + sc_ref[...])
        _acc_add(i, dg_ref, _colsum(dy * n))
        dn = dy * gv
        dx_ref[...] = d_ref[...] + r * (dn - n * jnp.mean(dn * n, axis=-1, keepdims=True))

    return _rowcall(name, body, S, tm, [(dxo, ROW), (dh, ROW), (x, ROW), (g, FULL), (sc, FULL)],
                    [((S, D), F32, ROW), ((1, D), F32, ACC), ((1, D), F32, ACC), ((1, D), F32, ACC)])


HB16 = 16


def _conv3_val(ext, w, b, hb):
    return w[2:3] * ext[hb:] + w[1:2] * _shift_prev(ext, 1, hb) + w[0:1] * _shift_prev(ext, 2, hb) + b


def _glu3_fwd(name, u, w, b, tm=128):
    S, F2 = u.shape
    Fh = F2 // 2

    def body(ua_ref, ub_ref, ha_ref, hb_ref, w_ref, b_ref, o_ref):
        live = (pl.program_id(0) > 0).astype(F32)
        wv, bv = w_ref[...], b_ref[...]
        ea = jnp.concatenate([ha_ref[...].astype(F32) * live, ua_ref[...].astype(F32)], axis=0)
        eb = jnp.concatenate([hb_ref[...].astype(F32) * live, ub_ref[...].astype(F32)], axis=0)
        za = _conv3_val(ea, wv[:, :Fh], bv[:, :Fh], HB16)
        zb = _conv3_val(eb, wv[:, Fh:], bv[:, Fh:], HB16)
        o_ref[...] = (za * _sigmoid(za) * zb).astype(BF16)

    return _rowcall(name, body, S, tm,
                    [(u, ("row", Fh, 0)), (u, ("row", Fh, 1)), (u, ("prev", HB16, Fh, 0)), (u, ("prev", HB16, Fh, 1)),
                     (w, FULL), (b, FULL)],
                    [((S, Fh), BF16, ROW)])[0]


def _glu3_bwd(name, u, dhm, w, b, tm=128):
    S, F2 = u.shape
    Fh = F2 // 2

    def body(ua_ref, ub_ref, ha_ref, hb_ref, d_ref, w_ref, b_ref, dz_ref, acc_ref):
        i = pl.program_id(0)
        live = (i > 0).astype(F32)
        wv, bv, d = w_ref[...], b_ref[...], d_ref[...]
        ea = jnp.concatenate([ha_ref[...].astype(F32) * live, ua_ref[...].astype(F32)], axis=0)
        eb = jnp.concatenate([hb_ref[...].astype(F32) * live, ub_ref[...].astype(F32)], axis=0)
        za = _conv3_val(ea, wv[:, :Fh], bv[:, :Fh], HB16)
        zb = _conv3_val(eb, wv[:, Fh:], bv[:, Fh:], HB16)
        sg = _sigmoid(za)
        da = d * zb * (sg * (1.0 + za * (1.0 - sg)))
        db = d * (za * sg)
        dz_ref[:, :Fh] = da.astype(BF16)
        dz_ref[:, Fh:] = db.astype(BF16)
        for k in range(3):
            row = jnp.concatenate([_colsum(da * _shift_prev(ea, 2 - k, HB16)),
                                   _colsum(db * _shift_prev(eb, 2 - k, HB16))], axis=1)
            _acc_add(i, acc_ref, row, rows=(slice(k, k + 1), slice(None)))
        _acc_add(i, acc_ref, jnp.concatenate([_colsum(da), _colsum(db)], axis=1), rows=(slice(3, 4), slice(None)))

        @pl.when(i == 0)
        def _():
            acc_ref[4:8, :] = jnp.zeros((4, F2), F32)

    return _rowcall(name, body, S, tm,
                    [(u, ("row", Fh, 0)), (u, ("row", Fh, 1)), (u, ("prev", HB16, Fh, 0)), (u, ("prev", HB16, Fh, 1)),
                     (dhm, ROW), (w, FULL), (b, FULL)],
                    [((S, F2), BF16, ROW), ((8, F2), F32, ACC)])


def _conv3_bwd(name, dz, w, tm=128):
    S, F2 = dz.shape
    nblk = S // tm

    def body(d_ref, n_ref, w_ref, o_ref):
        live = (pl.program_id(0) < nblk - 1).astype(F32)
        ext = jnp.concatenate([d_ref[...].astype(F32), n_ref[...].astype(F32) * live], axis=0)
        wv = w_ref[...]
        o_ref[...] = (wv[2:3] * ext[:tm] + wv[1:2] * _shift_next(ext, 1, tm)
                      + wv[0:1] * _shift_next(ext, 2, tm)).astype(BF16)

    return _rowcall(name, body, S, tm, [(dz, ROW), (dz, ("next", HB16, None, 0)), (w, FULL)],
                    [((S, F2), BF16, ROW)])[0]


def _ffn_fwd(name, h, w_up, w_dn, dw_w, dw_b):
    u = _matmul(f"{name}_up", h, w_up, "nn", BF16, tm=2048)
    hm = _glu3_fwd(f"{name}_glu", u, dw_w, dw_b)
    f = _matmul(f"{name}_dn", hm, w_dn, "nn", F32, tm=1024, tn=1024, tk=w_dn.shape[0])
    return f, (u, hm)


def _ffn_bwd(name, dy, h, u, hm, w_up, w_dn, dw_w, dw_b):
    Fh = w_dn.shape[0]
    dhm = _matmul(f"{name}_ddn_x", dy, w_dn, "nt", F32, tm=1024, tn=Fh // 2)
    g_dn = _matmul(f"{name}_ddn_w", hm, dy, "tn", BF16, tm=Fh // 2, tn=1024, tk=1024)
    dz, acc = _glu3_bwd(f"{name}_dglu", u, dhm, dw_w, dw_b)
    du = _conv3_bwd(f"{name}_dconv", dz, dw_w)
    g_up = _matmul(f"{name}_dup_w", h, du, "tn", BF16, tm=1024, tn=Fh // 2, tk=1024)
    dh = _matmul(f"{name}_dup_x", du, w_up, "nt", F32, tm=1024, tn=1024, tk=Fh // 2)
    return dh, dict(up=g_up, dn=g_dn, dw_w=acc[0:3], dw_b=acc[3:4])


HB32 = 32


def _glu31_fwd(name, p, w, b, tm=256):
    S, D2 = p.shape
    D = D2 // 2
    K = w.shape[0]

    def body(a_ref, g_ref, ha_ref, hg_ref, w_ref, b_ref, o_ref):
        live = (pl.program_id(0) > 0).astype(F32)
        y1 = a_ref[...] * _sigmoid(g_ref[...])
        yh = ha_ref[...] * _sigmoid(hg_ref[...]) * live
        ext = jnp.concatenate([yh, y1], axis=0)
        wv = w_ref[...]
        acc = b_ref[...] + wv[K - 1:K] * y1
        for k in range(K - 1):
            acc = acc + wv[k:k + 1] * _shift_prev(ext, K - 1 - k, HB32)
        o_ref[...] = acc

    return _rowcall(name, body, S, tm,
                    [(p, ("row", D, 0)), (p, ("row", D, 1)), (p, ("prev", HB32, D, 0)), (p, ("prev", HB32, D, 1)),
                     (w, FULL), (b, FULL)],
                    [((S, D), F32, ROW)])[0]


def _ln_silu_fwd(name, y2, g, b, tm=256):
    S, D = y2.shape

    def body(y_ref, g_ref, b_ref, o_ref):
        y = y_ref[...]
        mu = jnp.mean(y, axis=-1, keepdims=True)
        yc = y - mu
        rs = lax.rsqrt(jnp.mean(yc * yc, axis=-1, keepdims=True) + EPS)
        y3 = yc * rs * g_ref[...] + b_ref[...]
        o_ref[...] = (y3 * _sigmoid(y3)).astype(BF16)

    return _rowcall(name, body, S, tm, [(y2, ROW), (g, FULL), (b, FULL)], [((S, D), BF16, ROW)])[0]


def _ln_silu_bwd(name, y2, dy4, g, b, tm=256):
    S, D = y2.shape

    def body(y_ref, d_ref, g_ref, b_ref, o_ref, dg_ref, db_ref):
        i = pl.program_id(0)
        y, gv = y_ref[...], g_ref[...]
        mu = jnp.mean(y, axis=-1, keepdims=True)
        yc = y - mu
        rs = lax.rsqrt(jnp.mean(yc * yc, axis=-1, keepdims=True) + EPS)
        n = yc * rs
        y3 = n * gv + b_ref[...]
        sg = _sigmoid(y3)
        dy3 = d_ref[...] * (sg * (1.0 + y3 * (1.0 - sg)))
        _acc_add(i, db_ref, _colsum(dy3))
        _acc_add(i, dg_ref, _colsum(dy3 * n))
        dn = dy3 * gv
        o_ref[...] = rs * (dn - jnp.mean(dn, axis=-1, keepdims=True) - n * jnp.mean(dn * n, axis=-1, keepdims=True))

    return _rowcall(name, body, S, tm, [(y2, ROW), (dy4, ROW), (g, FULL), (b, FULL)],
                    [((S, D), F32, ROW), ((1, D), F32, ACC), ((1, D), F32, ACC)])


def _glu31_bwd(name, p, dy2, w, tm=256):
    S, D2 = p.shape
    D = D2 // 2
    K = w.shape[0]
    nblk = S // tm

    def body(a_ref, g_ref, ha_ref, hg_ref, d_ref, dn_ref, w_ref, dp_ref, dw_ref, dcb_ref, dpb_ref):
        i = pl.program_id(0)
        live_prev = (i > 0).astype(F32)
        live_next = (i < nblk - 1).astype(F32)
        a, sg, d, wv = a_ref[...], _sigmoid(g_ref[...]), d_ref[...], w_ref[...]
        y1 = a * sg
        ext_y = jnp.concatenate([ha_ref[...] * _sigmoid(hg_ref[...]) * live_prev, y1], axis=0)
        ext_d = jnp.concatenate([d, dn_ref[...] * live_next], axis=0)
        dy1 = wv[K - 1:K] * d
        for k in range(K - 1):
            dy1 = dy1 + wv[k:k + 1] * _shift_next(ext_d, K - 1 - k, tm)
        for k in range(K):
            _acc_add(i, dw_ref, _colsum(d * _shift_prev(ext_y, K - 1 - k, HB32)), rows=(slice(k, k + 1), slice(None)))

        @pl.when(i == 0)
        def _():
            dw_ref[K:, :] = jnp.zeros((dw_ref.shape[0] - K, D), F32)

        _acc_add(i, dcb_ref, _colsum(d))
        da = dy1 * sg
        dg = dy1 * a * sg * (1.0 - sg)
        dp_ref[:, :D] = da.astype(BF16)
        dp_ref[:, D:] = dg.astype(BF16)
        _acc_add(i, dpb_ref, jnp.concatenate([_colsum(da), _colsum(dg)], axis=1))

    return _rowcall(name, body, S, tm,
                    [(p, ("row", D, 0)), (p, ("row", D, 1)), (p, ("prev", HB32, D, 0)), (p, ("prev", HB32, D, 1)),
                     (dy2, ROW), (dy2, ("next", HB32, None, 0)), (w, FULL)],
                    [((S, D2), BF16, ROW), ((HB32, D), F32, ACC), ((1, D), F32, ACC), ((1, D2), F32, ACC)])


CHUNK = 128
A_GROUPS = 4


def _group_ln(gv):
    ns, rss = [], []
    for g in range(A_GROUPS):
        xg = gv[:, g * LANES:(g + 1) * LANES]
        xc = xg - jnp.mean(xg, axis=-1, keepdims=True)
        rs = lax.rsqrt(jnp.mean(xc * xc, axis=-1, keepdims=True) + EPS)
        ns.append(xc * rs)
        rss.append(jnp.broadcast_to(rs, xg.shape))
    return jnp.concatenate(ns, axis=1), jnp.concatenate(rss, axis=1)


def _tril_mask():
    r = lax.broadcasted_iota(jnp.int32, (CHUNK, CHUNK), 0)
    c = lax.broadcasted_iota(jnp.int32, (CHUNK, CHUNK), 1)
    return r >= c


def _spatial(ws_ref, x, dn):
    mask = _tril_mask()
    rows = []
    for ci in range(x.shape[0] // CHUNK):
        cols = []
        for g in range(A_GROUPS):
            wm = jnp.where(mask, ws_ref[g], 0.0).astype(BF16)
            xb = x[ci * CHUNK:(ci + 1) * CHUNK, g * LANES:(g + 1) * LANES]
            cols.append(lax.dot_general(wm, xb, dn, preferred_element_type=F32))
        rows.append(jnp.concatenate(cols, axis=1))
    return jnp.concatenate(rows, axis=0)


def _mixa_fwd(name, z, vg, ws, bias_full, tm=256):
    S = z.shape[0]
    W = A_GROUPS * LANES

    def body(u_ref, v_ref, vg_ref, ws_ref, b_ref, o_ref):
        nh, _ = _group_ln(_gelu(v_ref[...]))
        vn = (nh * vg_ref[...]).astype(BF16)
        f = _spatial(ws_ref, vn, _DN["nn"]) + jnp.concatenate([b_ref[...]] * (tm // CHUNK), axis=0)
        o_ref[...] = (_gelu(u_ref[...]) * f).astype(BF16)

    return _rowcall(name, body, S, tm,
                    [(z, ("row", W, 0)), (z, ("row", W, 1)), (vg, FULL), (ws, FULL), (bias_full, FULL)],
                    [((S, W), BF16, ROW)])[0]


def _mixa_bwd(name, z, dyab, vg, ws, bias_full, tm=256):
    S = z.shape[0]
    W = A_GROUPS * LANES
    nch = tm // CHUNK

    def body(u_ref, v_ref, d_ref, vg_ref, ws_ref, b_ref, dz_ref, dws_ref, dbf_ref, dvg_ref):
        i = pl.program_id(0)
        u, v, d, vgv = u_ref[...], v_ref[...], d_ref[...], vg_ref[...]
        nh, rs = _group_ln(_gelu(v))
        vn = (nh * vgv).astype(BF16)
        f = _spatial(ws_ref, vn, _DN["nn"]) + jnp.concatenate([b_ref[...]] * nch, axis=0)
        dz_ref[:, :W] = (d * f * _gelu_grad(u)).astype(BF16)
        df = d * _gelu(u)
        dbf = df[0:CHUNK]
        for ci in range(1, nch):
            dbf = dbf + df[ci * CHUNK:(ci + 1) * CHUNK]
        _acc_add(i, dbf_ref, dbf)
        dfb = df.astype(BF16)
        mask = _tril_mask()
        for g in range(A_GROUPS):
            acc = jnp.zeros((CHUNK, CHUNK), F32)
            for ci in range(nch):
                blk = (slice(ci * CHUNK, (ci + 1) * CHUNK), slice(g * LANES, (g + 1) * LANES))
                acc = acc + lax.dot_general(dfb[blk], vn[blk], _DN["nt"], preferred_element_type=F32)
            _acc_add(i, dws_ref, jnp.where(mask, acc, 0.0)[None], rows=(slice(g, g + 1), slice(None), slice(None)))
        dvn = _spatial(ws_ref, dfb, _DN["tn"])
        _acc_add(i, dvg_ref, _colsum(dvn * nh))
        dnh = dvn * vgv
        parts = []
        for g in range(A_GROUPS):
            cs = slice(g * LANES, (g + 1) * LANES)
            dg_, ng = dnh[:, cs], nh[:, cs]
            parts.append(dg_ - jnp.mean(dg_, axis=-1, keepdims=True) - ng * jnp.mean(dg_ * ng, axis=-1, keepdims=True))
        dz_ref[:, W:] = (rs * jnp.concatenate(parts, axis=1) * _gelu_grad(v)).astype(BF16)

    return _rowcall(name, body, S, tm,
                    [(z, ("row", W, 0)), (z, ("row", W, 1)), (dyab, ("row", W, 0)), (vg, FULL), (ws, FULL),
                     (bias_full, FULL)],
                    [((S, 2 * W), BF16, ROW), ((A_GROUPS, CHUNK, CHUNK), F32, ACC), ((CHUNK, W), F32, ACC),
                     ((1, W), F32, ACC)])


HEAD = 64
N_HEADS = 8
BW = HEAD * N_HEADS
QB = 128
DILATIONS = (1, 4, 16)
QK_SCALE = HEAD ** -0.5


def _gsum64(x, ones_bd):
    x1 = x.astype(BF16)
    r1 = x - x1.astype(F32)
    x2 = r1.astype(BF16)
    x3 = (r1 - x2.astype(F32)).astype(BF16)
    dot = lambda t: jnp.dot(t, ones_bd, preferred_element_type=F32)
    return dot(x1) + dot(x2) + dot(x3)


def _swap32(x):
    n = x.shape[-1]
    up = pltpu.roll(x, n - HEAD // 2, axis=1)
    dn = pltpu.roll(x, HEAD // 2, axis=1)
    lane = lax.broadcasted_iota(jnp.int32, x.shape, 1)
    return jnp.where((lane % HEAD) < HEAD // 2, up, dn)


def _tile4(t):
    return jnp.concatenate([t] * (BW // LANES), axis=1)


def _qkv_fwd(name, z, cos, sin, ones_bd, qg, kg, tm=256):
    S = z.shape[0]

    def body(q_ref, k_ref, v_ref, c_ref, s_ref, o_ref, qg_ref, kg_ref, qo_ref, ko_ref, vo_ref):
        c, s, ob = _tile4(c_ref[...]), _tile4(s_ref[...]), o_ref[...]

        def norm_rope(x, g):
            r = lax.rsqrt(_gsum64(x * x, ob) * (1.0 / HEAD) + EPS)
            xn = x * r * g
            return xn * c + _swap32(xn) * s

        qo_ref[...] = (norm_rope(q_ref[...], qg_ref[...]) * QK_SCALE).astype(BF16)
        ko_ref[...] = norm_rope(k_ref[...], kg_ref[...]).astype(BF16)
        vo_ref[...] = v_ref[...].astype(BF16)

    return _rowcall(name, body, S, tm,
                    [(z, ("row", BW, 2)), (z, ("row", BW, 3)), (z, ("row", BW, 4)), (cos, ROW), (sin, ROW),
                     (ones_bd, FULL), (qg, FULL), (kg, FULL)],
                    [((S, BW), BF16, ROW)] * 3)


def _scores(q_ref, kp_ref, kc_ref, h, i, seg_blocks):
    hs = slice(h * HEAD, (h + 1) * HEAD)
    qh = q_ref[:, hs]
    kcat = jnp.concatenate([kp_ref[:, hs], kc_ref[:, hs]], axis=0)
    s = lax.dot_general(qh, kcat, _DN["nt"], preferred_element_type=F32)
    qi = lax.broadcasted_iota(jnp.int32, (QB, 2 * QB), 0)
    kj = lax.broadcasted_iota(jnp.int32, (QB, 2 * QB), 1)
    has_prev = (i % seg_blocks) != 0
    valid = (kj >= qi) & (kj <= qi + QB) & ((kj >= QB) | has_prev)
    return qh, kcat, s, valid


def _attn_fwd(name, q, k, v, seg_len):
    S = q.shape[0]
    seg_blocks = seg_len // QB
    PREV = ("off", -1, None, 0)

    def body(q_ref, kp_ref, kc_ref, vp_ref, vc_ref, o_ref, l_ref):
        i = pl.program_id(0)
        outs, lses = [], []
        for h in range(N_HEADS):
            hs = slice(h * HEAD, (h + 1) * HEAD)
            _, _, s, valid = _scores(q_ref, kp_ref, kc_ref, h, i, seg_blocks)
            s = jnp.where(valid, s, NEG)
            m = jnp.max(s, axis=-1, keepdims=True)
            p = jnp.exp(s - m)
            den = jnp.sum(p, axis=-1, keepdims=True)
            vcat = jnp.concatenate([vp_ref[:, hs], vc_ref[:, hs]], axis=0)
            outs.append(jnp.dot(p.astype(BF16), vcat, preferred_element_type=F32) / den)
            lses.append(jnp.broadcast_to(m + jnp.log(den), (QB, HEAD)))
        o_ref[...] = jnp.concatenate(outs, axis=1)
        l_ref[...] = jnp.concatenate(lses, axis=1)

    return _rowcall(name, body, S, QB, [(q, ROW), (k, PREV), (k, ROW), (v, PREV), (v, ROW)],
                    [((S, BW), F32, ROW)] * 2)


def _attn_bwd(name, q, k, v, do, lse, delta, seg_len):
    S = q.shape[0]
    seg_blocks = seg_len // QB
    PREV = ("off", -1, None, 0)

    def body(q_ref, kp_ref, kc_ref, vp_ref, vc_ref, do_ref, l_ref, dl_ref, dq_ref, dkc_ref, dkp_ref, dvc_ref, dvp_ref):
        i = pl.program_id(0)
        dq, dkc, dkp, dvc, dvp = [], [], [], [], []
        for h in range(N_HEADS):
            hs = slice(h * HEAD, (h + 1) * HEAD)
            qh, kcat, s, valid = _scores(q_ref, kp_ref, kc_ref, h, i, seg_blocks)
            p = jnp.where(valid, jnp.exp(s - l_ref[:, h * HEAD:h * HEAD + 1]), 0.0)
            doh = do_ref[:, hs].astype(BF16)
            vcat = jnp.concatenate([vp_ref[:, hs], vc_ref[:, hs]], axis=0)
            dp = lax.dot_general(doh, vcat, _DN["nt"], preferred_element_type=F32)
            ds = (p * (dp - dl_ref[:, h * HEAD:h * HEAD + 1])).astype(BF16)
            dq.append(jnp.dot(ds, kcat, preferred_element_type=F32))
            dk = lax.dot_general(ds, qh, _DN["tn"], preferred_element_type=F32)
            dv = lax.dot_general(p.astype(BF16), doh, _DN["tn"], preferred_element_type=F32)
            dkp.append(dk[:QB])
            dkc.append(dk[QB:])
            dvp.append(dv[:QB])
            dvc.append(dv[QB:])
        dq_ref[...] = jnp.concatenate(dq, axis=1)
        dkc_ref[...] = jnp.concatenate(dkc, axis=1)
        dkp_ref[...] = jnp.concatenate(dkp, axis=1)
        dvc_ref[...] = jnp.concatenate(dvc, axis=1)
        dvp_ref[...] = jnp.concatenate(dvp, axis=1)

    return _rowcall(name, body, S, QB,
                    [(q, ROW), (k, PREV), (k, ROW), (v, PREV), (v, ROW), (do, ROW), (lse, ROW), (delta, ROW)],
                    [((S, BW), F32, ROW)] * 5)


def _merge_fwd(name, branches, tm=256):
    S = branches[0][0].shape[0]

    def body(o1, l1, o2, l2, o3, l3, y_ref, yb_ref, l_ref):
        ls = [l1[...], l2[...], l3[...]]
        m = jnp.maximum(jnp.maximum(ls[0], ls[1]), ls[2])
        es = [jnp.exp(l - m) for l in ls]
        den = es[0] + es[1] + es[2]
        y = (es[0] * o1[...] + es[1] * o2[...] + es[2] * o3[...]) / den
        y_ref[...] = y
        yb_ref[...] = y.astype(BF16)
        l_ref[...] = m + jnp.log(den)

    ins = [(t, ROW) for pair in branches for t in pair]
    return _rowcall(name, body, S, tm, ins, [((S, BW), F32, ROW), ((S, BW), BF16, ROW), ((S, BW), F32, ROW)])


def _delta(name, dyab, yb, ones_bd, tm=256):
    S = yb.shape[0]

    def body(d_ref, y_ref, o_ref, out_ref):
        out_ref[...] = _gsum64(d_ref[...] * y_ref[...], o_ref[...])

    return _rowcall(name, body, S, tm, [(dyab, ("row", BW, 1)), (yb, ROW), (ones_bd, FULL)], [((S, BW), F32, ROW)])[0]


def _qkv_bwd(name, z, cos, sin, ones_bd, qg, kg, pieces):
    S = z.shape[0]
    nblk = S // QB

    def body(q_ref, k_ref, c_ref, s_ref, o_ref, qg_ref, kg_ref, *rest):
        pr, (dz_ref, dqg_ref, dkg_ref) = rest[:15], rest[15:]
        i = pl.program_id(0)
        c, s, ob = _tile4(c_ref[...]), _tile4(s_ref[...]), o_ref[...]
        dq = dk = dv = None
        for b, dil in enumerate(DILATIONS):
            a_q, a_kc, a_kp, a_vc, a_vp = [r[...] for r in pr[5 * b:5 * b + 5]]
            live = ((i + dil) < nblk).astype(F32)
            tq, tk, tv = a_q, a_kc + a_kp * live, a_vc + a_vp * live
            dq, dk, dv = (tq, tk, tv) if b == 0 else (dq + tq, dk + tk, dv + tv)

        def back(x, g, d_rot, acc_ref):
            r = lax.rsqrt(_gsum64(x * x, ob) * (1.0 / HEAD) + EPS)
            n = x * r
            dxn = d_rot * c + _swap32(d_rot * s)
            _acc_add(i, acc_ref, _colsum(dxn * n))
            dn = dxn * g
            return r * (dn - n * (_gsum64(dn * n, ob) * (1.0 / HEAD)))

        dz_ref[:, :BW] = back(q_ref[...], qg_ref[...], dq * QK_SCALE, dqg_ref).astype(BF16)
        dz_ref[:, BW:2 * BW] = back(k_ref[...], kg_ref[...], dk, dkg_ref).astype(BF16)
        dz_ref[:, 2 * BW:] = dv.astype(BF16)

    ins = [(z, ("row", BW, 2)), (z, ("row", BW, 3)), (cos, ROW), (sin, ROW), (ones_bd, FULL), (qg, FULL), (kg, FULL)]
    for (a_q, a_kc, a_kp, a_vc, a_vp), dil in zip(pieces, DILATIONS):
        ins += [(a_q, ROW), (a_kc, ROW), (a_kp, ("off", dil, None, 0)), (a_vc, ROW), (a_vp, ("off", dil, None, 0))]
    return _rowcall(name, body, S, QB, ins,
                    [((S, 3 * BW), BF16, ROW), ((1, BW), F32, ACC), ((1, BW), F32, ACC)])


def _to_sub(t, dil):
    S, W = t.shape
    return t if dil == 1 else t.reshape(S // dil, dil, W).transpose(1, 0, 2).reshape(S, W)


def _from_sub(t, dil):
    S, W = t.shape
    return t if dil == 1 else t.reshape(dil, S // dil, W).transpose(1, 0, 2).reshape(S, W)


def _local_step(x0, tgt, pos, mod, wb, sp, pipe):
    S, D = x0.shape
    md = lambda l, j: mod[l, j:j + 1]
    sh_m, sc_m, g_m, sh_f, sc_f, g_f = ([md(l, j) for l in range(2)] for j in range(6))
    nm_g, nf_g = sp["norm_mix_g"], sp["norm_ffn_g"]

    inv_freq = 1.0 / (ROPE_THETA ** (jnp.arange(0, HEAD, 2, dtype=F32) / HEAD))
    ang = pos.astype(F32)[:, None] * inv_freq
    cs, sn = jnp.cos(ang), jnp.sin(ang)
    cos = jnp.concatenate([cs, cs, cs, cs], axis=1)
    sin = jnp.concatenate([-sn, sn, -sn, sn], axis=1)
    head_of = jnp.arange(BW) // HEAD
    ones_bd = (head_of[:, None] == head_of[None, :]).astype(BF16)
    qg = jnp.tile(sp["b_q_norm_g"].reshape(1, HEAD), (1, N_HEADS))
    kg = jnp.tile(sp["b_k_norm_g"].reshape(1, HEAD), (1, N_HEADS))
    vg = sp["a_vnorm_g"].reshape(1, A_GROUPS * LANES)
    ws = sp["a_spatial_w"][0]
    bias_full = jnp.repeat(sp["a_spatial_b"][0].T, LANES, axis=1)
    ffn_s = [(sp["ffn_dw_w"][l], sp["ffn_dw_b"][l:l + 1]) for l in range(2)]

    h0 = _mod_first("l0_mod", x0, nm_g[0:1], sc_m[0], sh_m[0])
    z = _matmul("l0_in", h0, wb.get("w_in", h0), "nn", F32, tm=2048)
    ya = _mixa_fwd("l0_mixa", z, vg, ws, bias_full)
    qkv = _qkv_fwd("l0_qkv", z, cos, sin, ones_bd, qg, kg)
    qkv_sub, branches = [], []
    for dil in DILATIONS:
        sub = [_to_sub(t, dil) for t in qkv]
        o, l = _attn_fwd(f"l0_att{dil}", *sub, S // dil)
        qkv_sub.append(sub)
        branches.append((_from_sub(o, dil), _from_sub(l, dil)))
    yb, yb16, lse = _merge_fwd("l0_merge", branches)
    yab = jnp.concatenate([ya, yb16], axis=1)
    y0 = _matmul("l0_out", yab, wb.get("w_out", yab), "nn", F32, tm=1024, tn=1024)
    x1, h1 = _resid_mod("l0_res1", x0, y0, g_m[0], nf_g[0:1], sc_f[0], sh_f[0])
    ffn_w = [(wb.get("up0", h1), wb.get("dn0", h1), *ffn_s[0])]
    f0, (u0, hm0) = _ffn_fwd("l0_ffn", h1, *ffn_w[0])
    x2, h2 = _resid_mod("l0_res2", x1, f0, g_f[0], nm_g[1:2], sc_m[1], sh_m[1])
    p = _matmul("l1_pw1", h2, wb.get("pw1", h2), "nn", F32, tm=2048, bias=sp["conv_pw1_b"])
    y2 = _glu31_fwd("l1_glu", p, sp["conv_dw_w"][0], sp["conv_dw_b"])
    y4 = _ln_silu_fwd("l1_ln", y2, sp["conv_ln_g"], sp["conv_ln_b"])
    y1 = _matmul("l1_pw2", y4, wb.get("pw2", y4), "nn", F32, tm=1024, tn=1024, bias=sp["conv_pw2_b"])
    x3, h3 = _resid_mod("l1_res1", x2, y1, g_m[1], nf_g[1:2], sc_f[1], sh_f[1])
    ffn_w.append((wb.get("up1", h3), wb.get("dn1", h3), *ffn_s[1]))
    f1, (u1, hm1) = _ffn_fwd("l1_ffn", h3, *ffn_w[1])
    dx4, lossv = _loss_head("loss", x3, f1, g_f[1], tgt)

    dy, dgate_f1, _ = _gate_bwd("l1_gate2", dx4, f1, g_f[1])
    dh, gf1 = _ffn_bwd("l1_ffn", dy, h3, u1, hm1, *ffn_w[1])
    tok = pipe.scatter("g1", dict(dn1=gf1["dn"], up1=gf1["up"]))
    dx3, dsh_f1, dsc_f1, dnf1 = _mod_bwd("l1_dmod2", dx4, dh, x3, nf_g[1:2], sc_f[1] + tok)
    dy, dgate_m1, dpw2_b = _gate_bwd("l1_gate1", dx3, y1, g_m[1])
    dy4 = _matmul("l1_dpw2_x", dy, wb.get("pw2"), "nt", F32, tm=1024, tn=1024)
    g_pw2 = _matmul("l1_dpw2_w", y4, dy, "tn", BF16, tm=1024, tn=1024)
    dy2, dln_g, dln_b = _ln_silu_bwd("l1_dln", y2, dy4, sp["conv_ln_g"], sp["conv_ln_b"])
    dp, ddw_w, ddw_b, dpw1_b = _glu31_bwd("l1_dglu", p, dy2, sp["conv_dw_w"][0])
    g_pw1 = _matmul("l1_dpw1_w", h2, dp, "tn", BF16, tm=1024, tn=1024)
    tok = pipe.scatter("g2", dict(pw2=g_pw2, pw1=g_pw1))
    pipe.collect("g1", g_pw1)
    dh = _matmul("l1_dpw1_x", dp, wb.get("pw1"), "nt", F32, tm=1024, tn=1024, tk=2048)
    dx2, dsh_m1, dsc_m1, dnm1 = _mod_bwd("l1_dmod1", dx3, dh, x2, nm_g[1:2], sc_m[1] + tok)
    dy, dgate_f0, _ = _gate_bwd("l0_gate2", dx2, f0, g_f[0])
    dh, gf0 = _ffn_bwd("l0_ffn", dy, h1, u0, hm0, *ffn_w[0])
    pipe.finish("g1", gf0["up"])
    pipe.collect("g2", gf0["up"])
    dx1, dsh_f0, dsc_f0, dnf0 = _mod_bwd("l0_dmod2", dx2, dh, x1, nf_g[0:1], sc_f[0])
    dy, dgate_m0, _ = _gate_bwd("l0_gate1", dx1, y0, g_m[0])
    dyab = _matmul("l0_dout_x", dy, wb.get("w_out"), "nt", F32, tm=1024, tn=1024)
    g_out = _matmul("l0_dout_w", yab, dy, "tn", BF16, tm=1024, tn=1024)
    tok = pipe.scatter("g3", dict(dn0=gf0["dn"], up0=gf0["up"], w_out=g_out))
    vg = vg + tok
    dza, dws, dbf, dvg = _mixa_bwd("l0_dmixa", z, dyab, vg, ws, bias_full)
    delta = _delta("l0_delta", dyab, yb, ones_bd)
    dyb = dyab[:, BW:]
    pieces = []
    for dil, sub in zip(DILATIONS, qkv_sub):
        outs = _attn_bwd(f"l0_datt{dil}", *sub, _to_sub(dyb, dil), _to_sub(lse, dil), _to_sub(delta, dil), S // dil)
        pieces.append([_from_sub(o, dil) for o in outs])
    dzb, dqg, dkg = _qkv_bwd("l0_dqkv", z, cos, sin, ones_bd, qg, kg, pieces)
    dz = jnp.concatenate([dza, dzb], axis=1)
    g_in = _matmul("l0_din_w", h0, dz, "tn", BF16, tm=1024, tn=1280)
    tok = pipe.scatter("g4", dict(w_in=g_in))
    pipe.finish("g2", g_in)
    pipe.collect("g3", g_in)
    dh = _matmul("l0_din_x", dz, wb.get("w_in"), "nt", F32, tm=1024, tn=1024, tk=2560)
    grad_x, dsh_m0, dsc_m0, dnm0 = _mod_bwd("l0_dmod1", dx1, dh, x0, nm_g[0:1], sc_m[0] + tok)

    dmod = jnp.stack([jnp.concatenate([dsh_m0, dsc_m0, dgate_m0, dsh_f0, dsc_f0, dgate_f0], axis=0),
                      jnp.concatenate([dsh_m1, dsc_m1, dgate_m1, dsh_f1, dsc_f1, dgate_f1], axis=0)])
    small = dict(
        norm_mix_g=jnp.concatenate([dnm0, dnm1], axis=0),
        norm_ffn_g=jnp.concatenate([dnf0, dnf1], axis=0),
        a_vnorm_g=dvg.reshape(1, A_GROUPS, LANES),
        a_spatial_w=dws[None],
        a_spatial_b=dbf.reshape(CHUNK, A_GROUPS, LANES).sum(-1).T[None],
        b_q_norm_g=dqg.reshape(N_HEADS, HEAD).sum(0)[None],
        b_k_norm_g=dkg.reshape(N_HEADS, HEAD).sum(0)[None],
        conv_pw1_b=dpw1_b, conv_dw_w=ddw_w[None, :sp["conv_dw_w"].shape[1]], conv_dw_b=ddw_b,
        conv_ln_g=dln_g, conv_ln_b=dln_b, conv_pw2_b=dpw2_b,
        ffn_dw_w=jnp.stack([gf0["dw_w"], gf1["dw_w"]]),
        ffn_dw_b=jnp.concatenate([gf0["dw_b"], gf1["dw_b"]], axis=0),
    )
    return lossv, grad_x, small, dmod


ADA_TN = 512


def _ada_fwd(name, c_all, ada_w, ada_b_sh):
    L, D, N = ada_w.shape
    B = c_all.shape[0]

    def body(c_ref, w_ref, b_ref, o_ref):
        cv = c_ref[...]
        ca = (cv * _sigmoid(cv)).astype(BF16)
        o_ref[0] = jnp.dot(ca, w_ref[0].astype(BF16), preferred_element_type=F32) + b_ref[0]

    return pl.pallas_call(
        body, name=name, grid=(L, N // ADA_TN),
        in_specs=[pl.BlockSpec((B, D), lambda l, j: (0, 0)), pl.BlockSpec((1, D, ADA_TN), lambda l, j: (l, 0, j)),
                  pl.BlockSpec((1, 1, ADA_TN), lambda l, j: (l, 0, j))],
        out_specs=pl.BlockSpec((1, B, ADA_TN), lambda l, j: (l, 0, j)),
        out_shape=jax.ShapeDtypeStruct((L, B, N), F32),
        compiler_params=_params(("parallel", "parallel")),
    )(c_all, ada_w, ada_b_sh.reshape(L, 1, N))


def _adamw_val(w, g, m, v):
    m2 = ADAM_B1 * m + (1.0 - ADAM_B1) * g
    v2 = ADAM_B2 * v + (1.0 - ADAM_B2) * (g * g)
    m_hat = m2 / (1.0 - ADAM_B1 ** ADAM_STEP)
    v_hat = v2 / (1.0 - ADAM_B2 ** ADAM_STEP)
    delta = -ADAM_LR * (m_hat / (jnp.sqrt(v_hat) + ADAM_EPS) + ADAM_WD * w)
    return delta, m2, v2


def _ada_update(name, c_all, dmod_sh, w, m, v):
    L, D, N = w.shape
    B = c_all.shape[0]

    def body(c_ref, d_ref, w_ref, m_ref, v_ref, g_ref, dl_ref, mo_ref, vo_ref):
        cv = c_ref[...]
        ca = (cv * _sigmoid(cv)).astype(BF16)
        g = lax.dot_general(ca, d_ref[0].astype(BF16), _DN["tn"], preferred_element_type=F32)
        g_ref[0] = g
        dl_ref[0], mo_ref[0], vo_ref[0] = _adamw_val(w_ref[0], g, m_ref[0], v_ref[0])

    wspec = pl.BlockSpec((1, D, ADA_TN), lambda l, j: (l, 0, j))
    return pl.pallas_call(
        body, name=name, grid=(L, N // ADA_TN),
        in_specs=[pl.BlockSpec((B, D), lambda l, j: (0, 0)), pl.BlockSpec((1, B, ADA_TN), lambda l, j: (l, 0, j)),
                  wspec, wspec, wspec],
        out_specs=[wspec] * 4, out_shape=[jax.ShapeDtypeStruct((L, D, N), F32)] * 4,
        compiler_params=_params(("parallel", "parallel")),
    )(c_all, dmod_sh, w, m, v)


def _adamw(name, w, g, m, v):
    R, C = w.shape
    tm = R
    for cand in (256, 128, 64, 32, 16, 8):
        if R % cand == 0 and cand * C * 4 <= (1 << 20):
            tm = cand
            break

    def body(w_ref, g_ref, m_ref, v_ref, d_ref, mo_ref, vo_ref):
        d_ref[...], mo_ref[...], vo_ref[...] = _adamw_val(w_ref[...], g_ref[...], m_ref[...], v_ref[...])

    return _rowcall(name, body, R, tm, [(w, ROW), (g, ROW), (m, ROW), (v, ROW)], [((R, C), F32, ROW)] * 3)


def _row_tile(rows, width, itemsize=4, limit=1 << 20):
    for cand in (512, 256, 128, 64, 32, 16):
        if rows % cand == 0 and cand * width * itemsize <= limit:
            return cand
    raise ValueError((rows, width))


def _cast_into_full(name, a, layer, q, kind):
    L, r, c = a.shape
    tm = _row_tile(r, c)
    if kind == "col":
        full, o_spec = (r, N_CHIPS * c), pl.BlockSpec((tm, c), lambda i, q_ref: (i, q_ref[0]))
    else:
        full, o_spec = (N_CHIPS * r, c), pl.BlockSpec((tm, c), lambda i, q_ref: (q_ref[0] * (r // tm) + i, 0))

    def body(q_ref, a_ref, o_ref):
        o_ref[...] = a_ref[0].astype(BF16)

    return pl.pallas_call(
        body, name=name,
        grid_spec=pltpu.PrefetchScalarGridSpec(
            num_scalar_prefetch=1, grid=(r // tm,),
            in_specs=[pl.BlockSpec((1, tm, c), lambda i, q_ref: (layer, i, 0))], out_specs=o_spec),
        out_shape=jax.ShapeDtypeStruct(full, BF16), compiler_params=_params(("parallel",)),
    )(q.reshape(1).astype(jnp.int32), a)


def _sum4(name, g, rcv, q, kind, n):
    r, c = rcv.shape[1:]
    tm = _row_tile(r, c)
    if kind == "col":
        g_spec = pl.BlockSpec((tm, n), lambda i, q_ref: (i, q_ref[0]))
    else:
        g_spec = pl.BlockSpec((tm, c), lambda i, q_ref: (q_ref[0] * (n // tm) + i, 0))

    def body(q_ref, g_ref, r_ref, o_ref):
        acc = g_ref[...].astype(F32)
        for j in range(3):
            acc = acc + r_ref[j].astype(F32)
        o_ref[...] = acc

    return pl.pallas_call(
        body, name=name,
        grid_spec=pltpu.PrefetchScalarGridSpec(
            num_scalar_prefetch=1, grid=(r // tm,),
            in_specs=[g_spec, pl.BlockSpec((3, tm, c), lambda i, q_ref: (0, i, 0))],
            out_specs=pl.BlockSpec((tm, c), lambda i, q_ref: (i, 0))),
        out_shape=jax.ShapeDtypeStruct((r, c), F32), compiler_params=_params(("parallel",)),
    )(q.reshape(1).astype(jnp.int32), g, rcv)


def _adamw_sum(name, w, m, v, layer, mine, theirs, prev):
    L, r, c = w.shape
    tm = _row_tile(r, c, limit=1 << 19)
    lay = pl.BlockSpec((1, tm, c), lambda i: (layer, i, 0))
    flat = pl.BlockSpec((tm, c), lambda i: (i, 0))
    n_prev = 0 if prev is None else 4

    def body(w_ref, m_ref, v_ref, a_ref, b_ref, *rest):
        g_ref, d_ref, mo_ref, vo_ref = rest[n_prev:]
        g = a_ref[...] + b_ref[...]
        g_ref[0] = g
        d_ref[0], mo_ref[0], vo_ref[0] = _adamw_val(w_ref[0], g, m_ref[0], v_ref[0])

    return pl.pallas_call(
        body, name=name, grid=(r // tm,),
        in_specs=[lay, lay, lay, flat, flat] + [ANY] * n_prev, out_specs=[lay] * 4,
        out_shape=[jax.ShapeDtypeStruct((L, r, c), F32)] * 4,
        input_output_aliases={5 + k: k for k in range(n_prev)},
        compiler_params=_params(("parallel",)),
    )(w, m, v, mine, theirs, *(prev or ()))


def _sum8(name, gathered):
    R, N = gathered.shape
    P = R // 8

    def body(g_ref, o_ref):
        acc = g_ref[0:P, :]
        for d in range(1, 8):
            acc = acc + g_ref[d * P:(d + 1) * P, :]
        o_ref[...] = acc

    return pl.pallas_call(body, name=name, out_shape=jax.ShapeDtypeStruct((P, N), F32),
                          compiler_params=pltpu.CompilerParams(vmem_limit_bytes=VMEM_LIMIT))(gathered)


ANY = pl.BlockSpec(memory_space=pl.ANY)


def _mesh_pos():
    x, y, c = lax.axis_index("x"), lax.axis_index("y"), lax.axis_index("c")
    other_chips = [(1 - x, y), (x, 1 - y), (1 - x, 1 - y)]
    return x, y, c, other_chips


def _allgather8(name, blk):
    m_per, n = blk.shape

    def body(x_ref, out_ref, send_sems, recv_sems, local_sem):
        x, y, c, chips = _mesh_pos()
        me, sibling = (x, y, c), (x, y, 1 - c)

        def rows(px, py, pc):
            return out_ref.at[pl.ds((4 * px + 2 * py + pc) * m_per, m_per), :]

        def copy(k, block, to, src=None):
            return pltpu.make_async_remote_copy(
                src_ref=rows(*block) if src is None else src, dst_ref=rows(*block),
                send_sem=send_sems.at[k], recv_sem=recv_sems.at[k], device_id=to, device_id_type=MESH)

        mine = pltpu.make_async_copy(x_ref, rows(*me), local_sem)
        mine.start()
        first = [copy(0, me, sibling, src=x_ref)]
        first += [copy(1 + j, me, (*chip, c), src=x_ref) for j, chip in enumerate(chips)]
        for cp in first:
            cp.start()
        passed = [copy(4 + j, (*chip, c), sibling) for j, chip in enumerate(chips)]
        for j, chip in enumerate(chips):
            copy(1 + j, (*chip, c), me).wait_recv()
            passed[j].start()
        copy(0, sibling, me).wait_recv()
        for j, chip in enumerate(chips):
            copy(4 + j, (*chip, 1 - c), me).wait_recv()
        for cp in first + passed:
            cp.wait_send()
        mine.wait()

    return pl.pallas_call(
        body, name=name, out_shape=jax.ShapeDtypeStruct((8 * m_per, n), blk.dtype),
        in_specs=[pl.BlockSpec(memory_space=pltpu.VMEM)], out_specs=pl.BlockSpec(memory_space=pltpu.VMEM),
        scratch_shapes=[pltpu.SemaphoreType.DMA((7,)), pltpu.SemaphoreType.DMA((7,)), pltpu.SemaphoreType.DMA],
        compiler_params=pltpu.CompilerParams(vmem_limit_bytes=VMEM_LIMIT),
    )(blk)


BIG = dict(w_in=("col", "ab_w_in", 0), w_out=("row", "ab_w_out", 0), up0=("col", "ffn_up_w", 0),
           dn0=("row", "ffn_down_w", 0), pw1=("col", "conv_pw1_w", 0), pw2=("row", "conv_pw2_w", 0),
           up1=("col", "ffn_up_w", 1), dn1=("row", "ffn_down_w", 1))
N_CHIPS = 4
HBM = pl.BlockSpec(memory_space=pltpu.HBM)
SEM = pl.BlockSpec(memory_space=pltpu.SEMAPHORE)
EFFECT = pltpu.SideEffectType.DATAFLOW_SIDE_EFFECTING


def _region(kind, ref, q, n):
    if kind == "col":
        return ref.at[:, pl.ds(q * n, n)]
    return ref.at[pl.ds(q * n, n), :]


def _gather_plan(kind, n):
    def remote(src, land, pos):
        x, y, c, chips = pos
        mine = _region(kind, land, 2 * x + y, n)
        return [(mine, mine, (*chip, c)) for chip in chips]

    return ("gather", kind, n), remote


def _scatter_plan(kind, n):
    def remote(src, land, pos):
        _, _, c, chips = pos
        return [(_region(kind, src, 2 * chip[0] + chip[1], n), land.at[j], (*chip, c)) for j, chip in enumerate(chips)]

    return ("scatter", kind, n), remote


def _sibling_plan():
    def remote(src, land, pos):
        x, y, c, _ = pos
        return [(src, land, (x, y, 1 - c))]

    return ("sibling",), remote


def _split_start(name, items, after=None):
    n = len(items)
    plans = [it[2] for it in items]
    n_in = 2 * n + (after is not None)

    def body(*refs):
        srcs, lands = refs[:n], refs[n:2 * n]
        sends, recvs = refs[n_in:n_in + n], refs[n_in + n:n_in + 2 * n]
        token = refs[n_in + 4 * n]
        pos = _mesh_pos()
        for a, (_, remote) in enumerate(plans):
            for k, (s, d, dev) in enumerate(remote(srcs[a], lands[a], pos)):
                pltpu.make_async_remote_copy(src_ref=s, dst_ref=d, send_sem=sends[a].at[k], recv_sem=recvs[a].at[k],
                                             device_id=dev, device_id_type=MESH).start()
        token[...] = jnp.zeros_like(token)

    sems = [pltpu.SemaphoreType.DMA((it[3],)) for it in items]
    bufs = [pltpu.HBM(it[k].shape, it[k].dtype) for k in (0, 1) for it in items]
    outs = pl.pallas_call(
        body, name=name, out_shape=[*sems, *sems, *bufs, jax.ShapeDtypeStruct((8, LANES), F32)],
        in_specs=[HBM] * (2 * n) + [ANY] * (n_in - 2 * n),
        out_specs=[SEM] * (2 * n) + [HBM] * (2 * n) + [pl.BlockSpec(memory_space=pltpu.VMEM)],
        input_output_aliases={i: 2 * n + i for i in range(2 * n)},
        compiler_params=pltpu.CompilerParams(has_side_effects=EFFECT),
    )(*[pltpu.with_memory_space_constraint(it[k], pltpu.HBM) for k in (0, 1) for it in items],
      *(() if after is None else (after,)))
    state = [(items[a][2], items[a][3], outs[2 * n + a], outs[3 * n + a], outs[a], outs[n + a]) for a in range(n)]
    return state, outs[4 * n]


def _split_wait(name, state, after):
    n = len(state)

    def body(*refs):
        srcs, lands = refs[:n], refs[n:2 * n]
        sends, recvs = refs[2 * n:3 * n], refs[3 * n:4 * n]
        pos = _mesh_pos()
        for a, ((_, remote), *_) in enumerate(state):
            for k, (s, d, dev) in enumerate(remote(srcs[a], lands[a], pos)):
                cp = pltpu.make_async_remote_copy(src_ref=s, dst_ref=d, send_sem=sends[a].at[k], recv_sem=recvs[a].at[k],
                                                  device_id=dev, device_id_type=MESH)
                cp.wait_send()
                cp.wait_recv()

    bufs = [st[k] for k in (2, 3) for st in state]
    outs = pl.pallas_call(
        body, name=name, out_shape=[pltpu.HBM(b.shape, b.dtype) for b in bufs],
        in_specs=[HBM] * (2 * n) + [SEM] * (2 * n) + [ANY], out_specs=[HBM] * (2 * n),
        input_output_aliases={i: i for i in range(2 * n)},
        compiler_params=pltpu.CompilerParams(has_side_effects=EFFECT),
    )(*bufs, *[st[k] for k in (4, 5) for st in state], after)
    return outs[:n], outs[n:]


class _Weights:
    def __init__(self, w, q, after):
        items = []
        unused = jnp.zeros((16, LANES), BF16)
        for name, (kind, pname, layer) in BIG.items():
            _, r, c = w[pname].shape
            land = _cast_into_full(f"cast_{name}", w[pname], layer, q, kind)
            items.append((unused, land, _gather_plan(kind, c if kind == "col" else r), N_CHIPS - 1))
        state, self.token = _split_start("gw_start", items, after)
        self.pending = dict(zip(BIG, state))
        self.ready = {}

    def get(self, name, after=None):
        if name not in self.ready:
            self.ready[name] = _split_wait(f"gw_wait_{name}", [self.pending.pop(name)], after)[1][0]
        return self.ready[name]


class _GradPipe:
    def __init__(self, q, w, m, v):
        self.q, self.w, self.m, self.v = q, w, m, v
        self.stage, self.results = {}, {}

    def scatter(self, group, grads, after=None):
        items = []
        for name, g in grads.items():
            kind = BIG[name][0]
            rows, cols = g.shape
            n = (cols if kind == "col" else rows) // N_CHIPS
            reg = (rows, n) if kind == "col" else (n, cols)
            items.append((g, lax.empty((N_CHIPS - 1, *reg), BF16), _scatter_plan(kind, n), N_CHIPS - 1))
        state, token = _split_start(f"gs_start_{group}", items, after)
        self.stage[group] = (list(grads), state)
        return token[0, 0]

    def collect(self, group, after):
        names, state = self.stage[group]
        srcs, lands = _split_wait(f"gs_wait_{group}", state, after)
        items = []
        for name, st, g, land in zip(names, state, srcs, lands):
            _, kind, n = st[0][0]
            part = _sum4(f"sum_{name}", g, land, self.q, kind, n)
            items.append((part, lax.empty(part.shape, F32), _sibling_plan(), 1))
        self.stage[group] = (names, _split_start(f"sw_start_{group}", items)[0])

    def finish(self, group, after):
        names, state = self.stage.pop(group)
        srcs, lands = _split_wait(f"sw_wait_{group}", state, after)
        for name, mine, theirs in zip(names, srcs, lands):
            _, pname, layer = BIG[name]
            self.results[pname] = _adamw_sum(f"adamw_{name}", self.w[pname], self.m[pname], self.v[pname], layer,
                                             mine, theirs, self.results.get(pname))


PACK_ROWS = 8


def _pack(arrays):
    flat = jnp.concatenate([a.reshape(-1) for a in arrays])
    n = flat.shape[0]
    padded = -(-n // (PACK_ROWS * LANES)) * (PACK_ROWS * LANES)
    return jnp.pad(flat, (0, padded - n)).reshape(PACK_ROWS, padded // PACK_ROWS)


def _unpack(packed, shapes):
    flat = packed.reshape(-1)
    out, off = [], 0
    for s in shapes:
        n = 1
        for d in s:
            n *= d
        out.append(flat[off:off + n].reshape(s))
        off += n
    return out


REPLICATED = ("ada_b", "norm_mix_g", "norm_ffn_g", "a_vnorm_g", "a_spatial_w", "a_spatial_b", "b_q_norm_g",
              "b_k_norm_g", "ffn_dw_b")
SMALL_SHARDED = ("conv_pw1_b", "conv_dw_w", "conv_dw_b", "conv_ln_g", "conv_ln_b", "conv_pw2_b", "ffn_dw_w")
WEIGHTS = ("ada_w", "ada_b", "norm_mix_g", "norm_ffn_g", "ab_w_in", "a_vnorm_g", "a_spatial_w", "a_spatial_b",
           "b_q_norm_g", "b_k_norm_g", "ab_w_out", "conv_pw1_w", "conv_pw1_b", "conv_dw_w", "conv_dw_b", "conv_ln_g",
           "conv_ln_b", "conv_pw2_w", "conv_pw2_b", "ffn_up_w", "ffn_dw_w", "ffn_dw_b", "ffn_down_w")

def kernel(x, c, positions, ada_w, ada_b, norm_mix_g, norm_ffn_g, ab_w_in, a_vnorm_g, a_spatial_w, a_spatial_b, b_q_norm_g, b_k_norm_g, ab_w_out, conv_pw1_w, conv_pw1_b, conv_dw_w, conv_dw_b, conv_ln_g, conv_ln_b, conv_pw2_w, conv_pw2_b, ffn_up_w, ffn_dw_w, ffn_dw_b, ffn_down_w, loss_target, m_ada_w, m_ada_b, m_norm_mix_g, m_norm_ffn_g, m_ab_w_in, m_a_vnorm_g, m_a_spatial_w, m_a_spatial_b, m_b_q_norm_g, m_b_k_norm_g, m_ab_w_out, m_conv_pw1_w, m_conv_pw1_b, m_conv_dw_w, m_conv_dw_b, m_conv_ln_g, m_conv_ln_b, m_conv_pw2_w, m_conv_pw2_b, m_ffn_up_w, m_ffn_dw_w, m_ffn_dw_b, m_ffn_down_w, v_ada_w, v_ada_b, v_norm_mix_g, v_norm_ffn_g, v_ab_w_in, v_a_vnorm_g, v_a_spatial_w, v_a_spatial_b, v_b_q_norm_g, v_b_k_norm_g, v_ab_w_out, v_conv_pw1_w, v_conv_pw1_b, v_conv_dw_w, v_conv_dw_b, v_conv_ln_g, v_conv_ln_b, v_conv_pw2_w, v_conv_pw2_b, v_ffn_up_w, v_ffn_dw_w, v_ffn_dw_b, v_ffn_down_w):
    w = dict(ada_w=ada_w, ada_b=ada_b, norm_mix_g=norm_mix_g, norm_ffn_g=norm_ffn_g, ab_w_in=ab_w_in, a_vnorm_g=a_vnorm_g, a_spatial_w=a_spatial_w, a_spatial_b=a_spatial_b, b_q_norm_g=b_q_norm_g, b_k_norm_g=b_k_norm_g, ab_w_out=ab_w_out, conv_pw1_w=conv_pw1_w, conv_pw1_b=conv_pw1_b, conv_dw_w=conv_dw_w, conv_dw_b=conv_dw_b, conv_ln_g=conv_ln_g, conv_ln_b=conv_ln_b, conv_pw2_w=conv_pw2_w, conv_pw2_b=conv_pw2_b, ffn_up_w=ffn_up_w, ffn_dw_w=ffn_dw_w, ffn_dw_b=ffn_dw_b, ffn_down_w=ffn_down_w)
    m = dict(ada_w=m_ada_w, ada_b=m_ada_b, norm_mix_g=m_norm_mix_g, norm_ffn_g=m_norm_ffn_g, ab_w_in=m_ab_w_in, a_vnorm_g=m_a_vnorm_g, a_spatial_w=m_a_spatial_w, a_spatial_b=m_a_spatial_b, b_q_norm_g=m_b_q_norm_g, b_k_norm_g=m_b_k_norm_g, ab_w_out=m_ab_w_out, conv_pw1_w=m_conv_pw1_w, conv_pw1_b=m_conv_pw1_b, conv_dw_w=m_conv_dw_w, conv_dw_b=m_conv_dw_b, conv_ln_g=m_conv_ln_g, conv_ln_b=m_conv_ln_b, conv_pw2_w=m_conv_pw2_w, conv_pw2_b=m_conv_pw2_b, ffn_up_w=m_ffn_up_w, ffn_dw_w=m_ffn_dw_w, ffn_dw_b=m_ffn_dw_b, ffn_down_w=m_ffn_down_w)
    v = dict(ada_w=v_ada_w, ada_b=v_ada_b, norm_mix_g=v_norm_mix_g, norm_ffn_g=v_norm_ffn_g, ab_w_in=v_ab_w_in, a_vnorm_g=v_a_vnorm_g, a_spatial_w=v_a_spatial_w, a_spatial_b=v_a_spatial_b, b_q_norm_g=v_b_q_norm_g, b_k_norm_g=v_b_k_norm_g, ab_w_out=v_ab_w_out, conv_pw1_w=v_conv_pw1_w, conv_pw1_b=v_conv_pw1_b, conv_dw_w=v_conv_dw_w, conv_dw_b=v_conv_dw_b, conv_ln_g=v_conv_ln_g, conv_ln_b=v_conv_ln_b, conv_pw2_w=v_conv_pw2_w, conv_pw2_b=v_conv_pw2_b, ffn_up_w=v_ffn_up_w, ffn_dw_w=v_ffn_dw_w, ffn_dw_b=v_ffn_dw_b, ffn_down_w=v_ffn_down_w)
    S, D = x.shape[1], x.shape[2]
    xi, yi, ci = lax.axis_index("x"), lax.axis_index("y"), lax.axis_index("c")
    q = 2 * xi + yi
    b = 2 * q + ci
    take_dev = lambda g: g.reshape(8, PACK_ROWS, -1)

    c_all = _allgather8("ag_c", c.reshape(PACK_ROWS, D // PACK_ROWS)).reshape(8, D)
    n_ada = ada_w.shape[2]
    mod_sh = _ada_fwd("ada_fwd", c_all, ada_w, lax.dynamic_slice_in_dim(ada_b, q * n_ada, n_ada, axis=1))
    sh_shapes = [mod_sh.shape] + [w[n].shape for n in SMALL_SHARDED]
    gathered_mod = _allgather8("ag_mod", _pack([mod_sh] + [w[n] for n in SMALL_SHARDED]))
    per_chip = [_unpack(blk, sh_shapes) for blk in take_dev(gathered_mod)[0::2]]
    mod_g = jnp.stack([pc[0] for pc in per_chip])
    mod_mine = lax.dynamic_index_in_dim(mod_g, b, axis=2, keepdims=False)
    mod = mod_mine.transpose(1, 0, 2).reshape(2, 6, D)
    sp = {n: jnp.concatenate([pc[1 + i] for pc in per_chip], axis=-1) for i, n in enumerate(SMALL_SHARDED)}
    sp.update({n: w[n] for n in REPLICATED if n != "ada_b"})

    wb = _Weights(w, q, gathered_mod)
    mod = mod + wb.token[0, 0]

    pipe = _GradPipe(q, w, m, v)
    lossv, grad_x, small, dmod = _local_step(x[0], loss_target[0], positions[0], mod, wb, sp, pipe)
    loss = lax.psum(0.5 * jnp.sum(lossv) / D, ("x", "y", "c"))

    small_names = [n for n in REPLICATED if n != "ada_b"] + list(SMALL_SHARDED)
    payload = [dmod.reshape(2, 6 * D)] + [small[n] for n in small_names]
    pay_shapes = [p.shape for p in payload]
    gathered = _allgather8("ag_grads", _pack(payload))
    totals = _unpack(_sum8("sum_grads", gathered), pay_shapes)
    grads = dict(zip(["ada_b"] + small_names, totals))
    for n in SMALL_SHARDED:
        n_sh = w[n].shape[-1]
        grads[n] = lax.dynamic_slice_in_dim(grads[n], q * n_sh, n_sh, axis=grads[n].ndim - 1)
    dmod_all = take_dev(gathered).reshape(8, -1)[:, :2 * 6 * D].reshape(8, 2, 6 * D)
    dmod_sh = lax.dynamic_slice_in_dim(dmod_all, q * n_ada, n_ada, axis=2).transpose(1, 0, 2)

    pipe.finish("g3", dmod_sh)
    pipe.collect("g4", dmod_sh)
    grads["ada_w"], delta_ada, m_ada, v_ada = _ada_update("ada_update", c_all, dmod_sh, ada_w, m_ada_w, v_ada_w)
    delta, new_m, new_v = dict(ada_w=delta_ada), dict(ada_w=m_ada), dict(ada_w=v_ada)
    rest = list(REPLICATED) + list(SMALL_SHARDED)
    rest_shapes = [w[n].shape for n in rest]
    outs = _adamw("adamw_small", *[_pack([src[n].reshape(w[n].shape) for n in rest]) for src in (w, grads, m, v)])
    for tgt, packed in zip((delta, new_m, new_v), outs):
        tgt.update(dict(zip(rest, _unpack(packed, rest_shapes))))
    for n in rest:
        grads[n] = grads[n].reshape(w[n].shape)
    pipe.finish("g4", outs[0])
    for n, res in pipe.results.items():
        grads[n], delta[n], new_m[n], new_v[n] = res

    return (loss, grad_x[None], *[grads[n] for n in WEIGHTS], *[delta[n] for n in WEIGHTS],
            *[new_m[n] for n in WEIGHTS], *[new_v[n] for n in WEIGHTS])
```

```python
import functools

import jax
import jax.numpy as jnp
from jax import lax
from jax.experimental import pallas as pl
from jax.experimental.pallas import tpu as pltpu

F32, BF16 = jnp.float32, jnp.bfloat16
EPS = 1e-6
NEG = -1e30
ROPE_THETA = 10000.0
LANES = 128
VMEM_LIMIT = 56 * 1024 * 1024
ADAM_LR, ADAM_B1, ADAM_B2, ADAM_EPS, ADAM_WD, ADAM_STEP = 0.001, 0.9, 0.999, 1e-08, 0.01, 10
MESH = pl.DeviceIdType.MESH


def _params(sem):
    return pltpu.CompilerParams(dimension_semantics=sem, vmem_limit_bytes=VMEM_LIMIT)


_DN = {"nn": (((1,), (0,)), ((), ())), "nt": (((1,), (1,)), ((), ())), "tn": (((0,), (0,)), ((), ()))}


def _matmul(name, a, b, mode, out_dtype, tm=512, tn=512, tk=1024, bias=None):
    if mode == "nn":
        (M, K), N = a.shape, b.shape[1]
    elif mode == "nt":
        (M, K), N = a.shape, b.shape[0]
    else:
        (K, M), N = a.shape, b.shape[1]
    tm, tn, tk = min(tm, M), min(tn, N), min(tk, K)
    assert M % tm == 0 and N % tn == 0 and K % tk == 0, (name, M, N, K, tm, tn, tk)
    nk = K // tk
    if mode == "tn":
        a_spec = pl.BlockSpec((tk, tm), lambda i, j, k: (k, i))
    else:
        a_spec = pl.BlockSpec((tm, tk), lambda i, j, k: (i, k))
    if mode == "nt":
        b_spec = pl.BlockSpec((tn, tk), lambda i, j, k: (j, k))
    else:
        b_spec = pl.BlockSpec((tk, tn), lambda i, j, k: (k, j))
    in_specs, args = [a_spec, b_spec], [a, b]
    if bias is not None:
        in_specs.append(pl.BlockSpec((1, tn), lambda i, j, k: (0, j)))
        args.append(bias)
    n_in = len(args)

    def body(*refs):
        a_ref, b_ref, o_ref = refs[0], refs[1], refs[n_in]
        p = lax.dot_general(a_ref[...], b_ref[...], _DN[mode], preferred_element_type=F32)

        def finish(acc):
            if bias is not None:
                acc = acc + refs[2][...]
            o_ref[...] = acc.astype(o_ref.dtype)

        if nk == 1:
            finish(p)
        else:
            acc_ref = refs[n_in + 1]
            k = pl.program_id(2)

            @pl.when(k == 0)
            def _():
                acc_ref[...] = p

            @pl.when(k > 0)
            def _():
                acc_ref[...] += p

            @pl.when(k == nk - 1)
            def _():
                finish(acc_ref[...])

    return pl.pallas_call(
        body, name=name, grid=(M // tm, N // tn, nk), in_specs=in_specs,
        out_specs=pl.BlockSpec((tm, tn), lambda i, j, k: (i, j)),
        out_shape=jax.ShapeDtypeStruct((M, N), out_dtype),
        scratch_shapes=[pltpu.VMEM((tm, tn), F32)] if nk > 1 else [],
        compiler_params=_params(("parallel", "parallel", "arbitrary")),
    )(*args)


def _rowcall(name, body, nrows, tm, ins, outs, scratch=()):
    nblk = nrows // tm
    assert nrows % tm == 0

    def spec(kind, shape):
        k = kind[0]
        if k == "row":
            cw, cb = kind[1] or shape[-1], kind[2]
            return pl.BlockSpec((tm, cw), lambda i: (i, cb))
        if k == "prev":
            hb, cw, cb = kind[1], kind[2] or shape[-1], kind[3]
            r = tm // hb
            return pl.BlockSpec((hb, cw), lambda i: (jnp.maximum(i * r - 1, 0), cb))
        if k == "next":
            hb, cw, cb = kind[1], kind[2] or shape[-1], kind[3]
            r, last = tm // hb, nrows // hb - 1
            return pl.BlockSpec((hb, cw), lambda i: (jnp.minimum((i + 1) * r, last), cb))
        if k == "off":
            off, cw, cb = kind[1], kind[2] or shape[-1], kind[3]
            return pl.BlockSpec((tm, cw), lambda i: (jnp.clip(i + off, 0, nblk - 1), cb))
        nd = len(shape)
        return pl.BlockSpec(tuple(shape), lambda i: (0,) * nd)

    has_acc = any(o[2][0] == "acc" for o in outs)
    return pl.pallas_call(
        body, name=name, grid=(nblk,),
        in_specs=[spec(kind, a.shape) for a, kind in ins],
        out_specs=[spec(kind, shape) for shape, _, kind in outs],
        out_shape=[jax.ShapeDtypeStruct(tuple(shape), dt) for shape, dt, _ in outs],
        scratch_shapes=list(scratch),
        compiler_params=_params(("arbitrary",) if has_acc else ("parallel",)),
    )(*[a for a, _ in ins])


ROW = ("row", None, 0)
FULL = ("full",)
ACC = ("acc",)


def _colsum(x):
    return jnp.sum(x, axis=0, keepdims=True)


def _acc_add(i, ref, val, rows=None):
    idx = (slice(None),) * len(ref.shape) if rows is None else rows

    @pl.when(i == 0)
    def _():
        ref[idx] = val

    @pl.when(i > 0)
    def _():
        ref[idx] = ref[idx] + val


def _sigmoid(x):
    return 1.0 / (1.0 + jnp.exp(-x))


def _gelu(x):
    return 0.5 * x * (1.0 + lax.erf(x * (2.0 ** -0.5)))


def _gelu_grad(x):
    return 0.5 * (1.0 + lax.erf(x * (2.0 ** -0.5))) + x * jnp.exp(-0.5 * x * x) * ((2.0 * jnp.pi) ** -0.5)


SUBLANES = 8


def _phases(ext, sign):
    n = ext.shape[0]
    return [ext if b == 0 else pltpu.roll(ext, b if sign > 0 else n - b, axis=0) for b in range(SUBLANES)]


def _shift_prev(phases, s, hb):
    a, b = divmod(s, SUBLANES)
    return phases[b][hb - SUBLANES * a:phases[b].shape[0] - SUBLANES * a]


def _shift_next(phases, s, tm):
    a, b = divmod(s, SUBLANES)
    return phases[b][SUBLANES * a:SUBLANES * a + tm]


def _rms_mod_val(x, g, sc, sh):
    r = lax.rsqrt(jnp.mean(x * x, axis=-1, keepdims=True) + EPS)
    return x * r * g * (1.0 + sc) + sh


def _mod_first(name, x, g, sc, sh, tm=256):
    S, D = x.shape

    def body(x_ref, g_ref, sc_ref, sh_ref, h_ref):
        h_ref[...] = _rms_mod_val(x_ref[...], g_ref[...], sc_ref[...], sh_ref[...]).astype(BF16)

    return _rowcall(name, body, S, tm, [(x, ROW), (g, FULL), (sc, FULL), (sh, FULL)], [((S, D), BF16, ROW)])[0]


def _resid_mod(name, x, y, gate, g, sc, sh, tm=256):
    S, D = x.shape

    def body(x_ref, y_ref, gate_ref, g_ref, sc_ref, sh_ref, xo_ref, h_ref):
        xn = x_ref[...] + gate_ref[...] * y_ref[...]
        xo_ref[...] = xn
        h_ref[...] = _rms_mod_val(xn, g_ref[...], sc_ref[...], sh_ref[...]).astype(BF16)

    return _rowcall(name, body, S, tm,
                    [(x, ROW), (y, ROW), (gate, FULL), (g, FULL), (sc, FULL), (sh, FULL)],
                    [((S, D), F32, ROW), ((S, D), BF16, ROW)])


def _loss_head(name, x, y, gate, tgt, tm=256):
    S, D = x.shape

    def body(x_ref, y_ref, gate_ref, t_ref, dx_ref, l_ref):
        i = pl.program_id(0)
        err = x_ref[...] + gate_ref[...] * y_ref[...] - t_ref[...]
        dx_ref[...] = err * (1.0 / D)
        _acc_add(i, l_ref, _colsum(err * err))

    return _rowcall(name, body, S, tm, [(x, ROW), (y, ROW), (gate, FULL), (tgt, ROW)],
                    [((S, D), F32, ROW), ((1, D), F32, ACC)])


def _gate_bwd(name, dxo, y, gate, tm=256):
    S, D = dxo.shape

    def body(d_ref, y_ref, gate_ref, dy_ref, dg_ref, db_ref):
        i = pl.program_id(0)
        d = d_ref[...]
        dy = d * gate_ref[...]
        dy_ref[...] = dy.astype(BF16)
        _acc_add(i, dg_ref, _colsum(d * y_ref[...]))
        _acc_add(i, db_ref, _colsum(dy))

    return _rowcall(name, body, S, tm, [(dxo, ROW), (y, ROW), (gate, FULL)],
                    [((S, D), BF16, ROW), ((1, D), F32, ACC), ((1, D), F32, ACC)])


def _mod_bwd(name, dxo, dh, x, g, sc, tm=256):
    S, D = x.shape

    def body(d_ref, dh_ref, x_ref, g_ref, sc_ref, dx_ref, dsh_ref, dsc_ref, dg_ref):
        i = pl.program_id(0)
        xv, dh_v, gv = x_ref[...], dh_ref[...], g_ref[...]
        r = lax.rsqrt(jnp.mean(xv * xv, axis=-1, keepdims=True) + EPS)
        n = xv * r
        _acc_add(i, dsh_ref, _colsum(dh_v))
        _acc_add(i, dsc_ref, _colsum(dh_v * (n * gv)))
        dy = dh_v * (1.0 + sc_ref[...])
        _acc_add(i, dg_ref, _colsum(dy * n))
        dn = dy * gv
        dx_ref[...] = d_ref[...] + r * (dn - n * jnp.mean(dn * n, axis=-1, keepdims=True))

    return _rowcall(name, body, S, tm, [(dxo, ROW), (dh, ROW), (x, ROW), (g, FULL), (sc, FULL)],
                    [((S, D), F32, ROW), ((1, D), F32, ACC), ((1, D), F32, ACC), ((1, D), F32, ACC)])


HB16 = 16


def _conv3_val(ph, w, b, hb):
    return w[2:3] * _shift_prev(ph, 0, hb) + w[1:2] * _shift_prev(ph, 1, hb) + w[0:1] * _shift_prev(ph, 2, hb) + b


def _halo_first(halo_ref, tile_ref, live):
    return _phases(jnp.concatenate([halo_ref[...].astype(F32) * live, tile_ref[...].astype(F32)], axis=0), 1)


def _glu3_fwd(name, u, w, b, tm=128):
    S, F2 = u.shape
    Fh = F2 // 2

    def body(ua_ref, ub_ref, ha_ref, hb_ref, w_ref, b_ref, o_ref):
        live = (pl.program_id(0) > 0).astype(F32)
        wv, bv = w_ref[...], b_ref[...]
        za = _conv3_val(_halo_first(ha_ref, ua_ref, live), wv[:, :Fh], bv[:, :Fh], HB16)
        zb = _conv3_val(_halo_first(hb_ref, ub_ref, live), wv[:, Fh:], bv[:, Fh:], HB16)
        o_ref[...] = (za * _sigmoid(za) * zb).astype(BF16)

    return _rowcall(name, body, S, tm,
                    [(u, ("row", Fh, 0)), (u, ("row", Fh, 1)), (u, ("prev", HB16, Fh, 0)), (u, ("prev", HB16, Fh, 1)),
                     (w, FULL), (b, FULL)],
                    [((S, Fh), BF16, ROW)])[0]


def _glu3_bwd(name, u, dhm, w, b, tm=128):
    S, F2 = u.shape
    Fh = F2 // 2

    def body(ua_ref, ub_ref, ha_ref, hb_ref, d_ref, w_ref, b_ref, dz_ref, acc_ref):
        i = pl.program_id(0)
        live = (i > 0).astype(F32)
        wv, bv, d = w_ref[...], b_ref[...], d_ref[...]
        ea, eb = _halo_first(ha_ref, ua_ref, live), _halo_first(hb_ref, ub_ref, live)
        za = _conv3_val(ea, wv[:, :Fh], bv[:, :Fh], HB16)
        zb = _conv3_val(eb, wv[:, Fh:], bv[:, Fh:], HB16)
        sg = _sigmoid(za)
        da = d * zb * (sg * (1.0 + za * (1.0 - sg)))
        db = d * (za * sg)
        dz_ref[:, :Fh] = da.astype(BF16)
        dz_ref[:, Fh:] = db.astype(BF16)
        for k in range(3):
            row = jnp.concatenate([_colsum(da * _shift_prev(ea, 2 - k, HB16)),
                                   _colsum(db * _shift_prev(eb, 2 - k, HB16))], axis=1)
            _acc_add(i, acc_ref, row, rows=(slice(k, k + 1), slice(None)))
        _acc_add(i, acc_ref, jnp.concatenate([_colsum(da), _colsum(db)], axis=1), rows=(slice(3, 4), slice(None)))

        @pl.when(i == 0)
        def _():
            acc_ref[4:8, :] = jnp.zeros((4, F2), F32)

    return _rowcall(name, body, S, tm,
                    [(u, ("row", Fh, 0)), (u, ("row", Fh, 1)), (u, ("prev", HB16, Fh, 0)), (u, ("prev", HB16, Fh, 1)),
                     (dhm, ROW), (w, FULL), (b, FULL)],
                    [((S, F2), BF16, ROW), ((8, F2), F32, ACC)])


def _conv3_bwd(name, dz, w, tm=128):
    S, F2 = dz.shape
    nblk = S // tm

    def body(d_ref, n_ref, w_ref, o_ref):
        live = (pl.program_id(0) < nblk - 1).astype(F32)
        ph = _phases(jnp.concatenate([d_ref[...].astype(F32), n_ref[...].astype(F32) * live], axis=0), -1)
        wv = w_ref[...]
        o_ref[...] = (wv[2:3] * _shift_next(ph, 0, tm) + wv[1:2] * _shift_next(ph, 1, tm)
                      + wv[0:1] * _shift_next(ph, 2, tm)).astype(BF16)

    return _rowcall(name, body, S, tm, [(dz, ROW), (dz, ("next", HB16, None, 0)), (w, FULL)],
                    [((S, F2), BF16, ROW)])[0]


def _ffn_fwd(name, h, w_up, w_dn, dw_w, dw_b):
    u = _matmul(f"{name}_up", h, w_up, "nn", BF16, tm=2048)
    hm = _glu3_fwd(f"{name}_glu", u, dw_w, dw_b)
    f = _matmul(f"{name}_dn", hm, w_dn, "nn", F32, tm=1024, tn=1024, tk=w_dn.shape[0])
    return f, (u, hm)


def _ffn_bwd(name, dy, h, u, hm, w_up, w_dn, dw_w, dw_b):
    Fh = w_dn.shape[0]
    dhm = _matmul(f"{name}_ddn_x", dy, w_dn, "nt", F32, tm=1024, tn=Fh // 2)
    g_dn = _matmul(f"{name}_ddn_w", hm, dy, "tn", BF16, tm=Fh // 2, tn=1024, tk=1024)
    dz, acc = _glu3_bwd(f"{name}_dglu", u, dhm, dw_w, dw_b)
    du = _conv3_bwd(f"{name}_dconv", dz, dw_w)
    g_up = _matmul(f"{name}_dup_w", h, du, "tn", BF16, tm=1024, tn=Fh // 2, tk=1024)
    dh = _matmul(f"{name}_dup_x", du, w_up, "nt", F32, tm=1024, tn=1024, tk=Fh // 2)
    return dh, dict(up=g_up, dn=g_dn, dw_w=acc[0:3], dw_b=acc[3:4])


HB32 = 32


def _glu31_fwd(name, p, w, b, tm=256):
    S, D2 = p.shape
    D = D2 // 2
    K = w.shape[0]

    def body(a_ref, g_ref, ha_ref, hg_ref, w_ref, b_ref, o_ref):
        live = (pl.program_id(0) > 0).astype(F32)
        y1 = a_ref[...] * _sigmoid(g_ref[...])
        ph = _phases(jnp.concatenate([ha_ref[...] * _sigmoid(hg_ref[...]) * live, y1], axis=0), 1)
        wv = w_ref[...]
        acc = b_ref[...] + wv[K - 1:K] * y1
        for k in range(K - 1):
            acc = acc + wv[k:k + 1] * _shift_prev(ph, K - 1 - k, HB32)
        o_ref[...] = acc

    return _rowcall(name, body, S, tm,
                    [(p, ("row", D, 0)), (p, ("row", D, 1)), (p, ("prev", HB32, D, 0)), (p, ("prev", HB32, D, 1)),
                     (w, FULL), (b, FULL)],
                    [((S, D), F32, ROW)])[0]


def _ln_silu_fwd(name, y2, g, b, tm=256):
    S, D = y2.shape

    def body(y_ref, g_ref, b_ref, o_ref):
        y = y_ref[...]
        mu = jnp.mean(y, axis=-1, keepdims=True)
        yc = y - mu
        rs = lax.rsqrt(jnp.mean(yc * yc, axis=-1, keepdims=True) + EPS)
        y3 = yc * rs * g_ref[...] + b_ref[...]
        o_ref[...] = (y3 * _sigmoid(y3)).astype(BF16)

    return _rowcall(name, body, S, tm, [(y2, ROW), (g, FULL), (b, FULL)], [((S, D), BF16, ROW)])[0]


def _ln_silu_bwd(name, y2, dy4, g, b, tm=256):
    S, D = y2.shape

    def body(y_ref, d_ref, g_ref, b_ref, o_ref, dg_ref, db_ref):
        i = pl.program_id(0)
        y, gv = y_ref[...], g_ref[...]
        mu = jnp.mean(y, axis=-1, keepdims=True)
        yc = y - mu
        rs = lax.rsqrt(jnp.mean(yc * yc, axis=-1, keepdims=True) + EPS)
        n = yc * rs
        y3 = n * gv + b_ref[...]
        sg = _sigmoid(y3)
        dy3 = d_ref[...] * (sg * (1.0 + y3 * (1.0 - sg)))
        _acc_add(i, db_ref, _colsum(dy3))
        _acc_add(i, dg_ref, _colsum(dy3 * n))
        dn = dy3 * gv
        o_ref[...] = rs * (dn - jnp.mean(dn, axis=-1, keepdims=True) - n * jnp.mean(dn * n, axis=-1, keepdims=True))

    return _rowcall(name, body, S, tm, [(y2, ROW), (dy4, ROW), (g, FULL), (b, FULL)],
                    [((S, D), F32, ROW), ((1, D), F32, ACC), ((1, D), F32, ACC)])


def _glu31_bwd(name, p, dy2, w, tm=256):
    S, D2 = p.shape
    D = D2 // 2
    K = w.shape[0]
    nblk = S // tm

    def body(a_ref, g_ref, ha_ref, hg_ref, d_ref, dn_ref, w_ref, dp_ref, dw_ref, dcb_ref, dpb_ref):
        i = pl.program_id(0)
        live_prev = (i > 0).astype(F32)
        live_next = (i < nblk - 1).astype(F32)
        a, sg, d, wv = a_ref[...], _sigmoid(g_ref[...]), d_ref[...], w_ref[...]
        ext_y = _phases(jnp.concatenate([ha_ref[...] * _sigmoid(hg_ref[...]) * live_prev, a * sg], axis=0), 1)
        ext_d = _phases(jnp.concatenate([d, dn_ref[...] * live_next], axis=0), -1)
        dy1 = wv[K - 1:K] * d
        for k in range(K - 1):
            dy1 = dy1 + wv[k:k + 1] * _shift_next(ext_d, K - 1 - k, tm)
        for k in range(K):
            _acc_add(i, dw_ref, _colsum(d * _shift_prev(ext_y, K - 1 - k, HB32)), rows=(slice(k, k + 1), slice(None)))

        @pl.when(i == 0)
        def _():
            dw_ref[K:, :] = jnp.zeros((dw_ref.shape[0] - K, D), F32)

        _acc_add(i, dcb_ref, _colsum(d))
        da = dy1 * sg
        dg = dy1 * a * sg * (1.0 - sg)
        dp_ref[:, :D] = da.astype(BF16)
        dp_ref[:, D:] = dg.astype(BF16)
        _acc_add(i, dpb_ref, jnp.concatenate([_colsum(da), _colsum(dg)], axis=1))

    return _rowcall(name, body, S, tm,
                    [(p, ("row", D, 0)), (p, ("row", D, 1)), (p, ("prev", HB32, D, 0)), (p, ("prev", HB32, D, 1)),
                     (dy2, ROW), (dy2, ("next", HB32, None, 0)), (w, FULL)],
                    [((S, D2), BF16, ROW), ((HB32, D), F32, ACC), ((1, D), F32, ACC), ((1, D2), F32, ACC)])


CHUNK = 128
A_GROUPS = 4


def _group_ln(gv):
    ns, rss = [], []
    for g in range(A_GROUPS):
        xg = gv[:, g * LANES:(g + 1) * LANES]
        xc = xg - jnp.mean(xg, axis=-1, keepdims=True)
        rs = lax.rsqrt(jnp.mean(xc * xc, axis=-1, keepdims=True) + EPS)
        ns.append(xc * rs)
        rss.append(jnp.broadcast_to(rs, xg.shape))
    return jnp.concatenate(ns, axis=1), jnp.concatenate(rss, axis=1)


def _tril_mask():
    r = lax.broadcasted_iota(jnp.int32, (CHUNK, CHUNK), 0)
    c = lax.broadcasted_iota(jnp.int32, (CHUNK, CHUNK), 1)
    return r >= c


def _spatial(ws_ref, x, dn):
    mask = _tril_mask()
    rows = []
    for ci in range(x.shape[0] // CHUNK):
        cols = []
        for g in range(A_GROUPS):
            wm = jnp.where(mask, ws_ref[g], 0.0).astype(BF16)
            xb = x[ci * CHUNK:(ci + 1) * CHUNK, g * LANES:(g + 1) * LANES]
            cols.append(lax.dot_general(wm, xb, dn, preferred_element_type=F32))
        rows.append(jnp.concatenate(cols, axis=1))
    return jnp.concatenate(rows, axis=0)


def _mixa_fwd(name, z, vg, ws, bias_full, tm=256):
    S = z.shape[0]
    W = A_GROUPS * LANES

    def body(u_ref, v_ref, vg_ref, ws_ref, b_ref, o_ref):
        nh, _ = _group_ln(_gelu(v_ref[...]))
        vn = (nh * vg_ref[...]).astype(BF16)
        f = _spatial(ws_ref, vn, _DN["nn"]) + jnp.concatenate([b_ref[...]] * (tm // CHUNK), axis=0)
        o_ref[...] = (_gelu(u_ref[...]) * f).astype(BF16)

    return _rowcall(name, body, S, tm,
                    [(z, ("row", W, 0)), (z, ("row", W, 1)), (vg, FULL), (ws, FULL), (bias_full, FULL)],
                    [((S, W), BF16, ROW)])[0]


def _mixa_bwd(name, z, dyab, vg, ws, bias_full, tm=256):
    S = z.shape[0]
    W = A_GROUPS * LANES
    nch = tm // CHUNK

    def body(u_ref, v_ref, d_ref, vg_ref, ws_ref, b_ref, dz_ref, dws_ref, dbf_ref, dvg_ref):
        i = pl.program_id(0)
        u, v, d, vgv = u_ref[...], v_ref[...], d_ref[...], vg_ref[...]
        nh, rs = _group_ln(_gelu(v))
        vn = (nh * vgv).astype(BF16)
        f = _spatial(ws_ref, vn, _DN["nn"]) + jnp.concatenate([b_ref[...]] * nch, axis=0)
        dz_ref[:, :W] = (d * f * _gelu_grad(u)).astype(BF16)
        df = d * _gelu(u)
        dbf = df[0:CHUNK]
        for ci in range(1, nch):
            dbf = dbf + df[ci * CHUNK:(ci + 1) * CHUNK]
        _acc_add(i, dbf_ref, dbf)
        dfb = df.astype(BF16)
        mask = _tril_mask()
        for g in range(A_GROUPS):
            acc = jnp.zeros((CHUNK, CHUNK), F32)
            for ci in range(nch):
                blk = (slice(ci * CHUNK, (ci + 1) * CHUNK), slice(g * LANES, (g + 1) * LANES))
                acc = acc + lax.dot_general(dfb[blk], vn[blk], _DN["nt"], preferred_element_type=F32)
            _acc_add(i, dws_ref, jnp.where(mask, acc, 0.0)[None], rows=(slice(g, g + 1), slice(None), slice(None)))
        dvn = _spatial(ws_ref, dfb, _DN["tn"])
        _acc_add(i, dvg_ref, _colsum(dvn * nh))
        dnh = dvn * vgv
        parts = []
        for g in range(A_GROUPS):
            cs = slice(g * LANES, (g + 1) * LANES)
            dg_, ng = dnh[:, cs], nh[:, cs]
            parts.append(dg_ - jnp.mean(dg_, axis=-1, keepdims=True) - ng * jnp.mean(dg_ * ng, axis=-1, keepdims=True))
        dz_ref[:, W:] = (rs * jnp.concatenate(parts, axis=1) * _gelu_grad(v)).astype(BF16)

    return _rowcall(name, body, S, tm,
                    [(z, ("row", W, 0)), (z, ("row", W, 1)), (dyab, ("row", W, 0)), (vg, FULL), (ws, FULL),
                     (bias_full, FULL)],
                    [((S, 2 * W), BF16, ROW), ((A_GROUPS, CHUNK, CHUNK), F32, ACC), ((CHUNK, W), F32, ACC),
                     ((1, W), F32, ACC)])


HEAD = 64
N_HEADS = 8
BW = HEAD * N_HEADS
QB = 128
DILATIONS = (1, 4, 16)
QK_SCALE = HEAD ** -0.5


def _gsum64(x, ones_bd):
    x1 = x.astype(BF16)
    r1 = x - x1.astype(F32)
    x2 = r1.astype(BF16)
    x3 = (r1 - x2.astype(F32)).astype(BF16)
    dot = lambda t: jnp.dot(t, ones_bd, preferred_element_type=F32)
    return dot(x1) + dot(x2) + dot(x3)


def _swap32(x):
    n = x.shape[-1]
    up = pltpu.roll(x, n - HEAD // 2, axis=1)
    dn = pltpu.roll(x, HEAD // 2, axis=1)
    lane = lax.broadcasted_iota(jnp.int32, x.shape, 1)
    return jnp.where((lane % HEAD) < HEAD // 2, up, dn)


def _tile4(t):
    return jnp.concatenate([t] * (BW // LANES), axis=1)


def _qkv_fwd(name, z, cos, sin, ones_bd, qg, kg, tm=256):
    S = z.shape[0]

    def body(q_ref, k_ref, v_ref, c_ref, s_ref, o_ref, qg_ref, kg_ref, qo_ref, ko_ref, vo_ref):
        c, s, ob = _tile4(c_ref[...]), _tile4(s_ref[...]), o_ref[...]

        def norm_rope(x, g):
            r = lax.rsqrt(_gsum64(x * x, ob) * (1.0 / HEAD) + EPS)
            xn = x * r * g
            return xn * c + _swap32(xn) * s

        qo_ref[...] = (norm_rope(q_ref[...], qg_ref[...]) * QK_SCALE).astype(BF16)
        ko_ref[...] = norm_rope(k_ref[...], kg_ref[...]).astype(BF16)
        vo_ref[...] = v_ref[...].astype(BF16)

    return _rowcall(name, body, S, tm,
                    [(z, ("row", BW, 2)), (z, ("row", BW, 3)), (z, ("row", BW, 4)), (cos, ROW), (sin, ROW),
                     (ones_bd, FULL), (qg, FULL), (kg, FULL)],
                    [((S, BW), BF16, ROW)] * 3)


def _scores(q_ref, kp_ref, kc_ref, h):
    hs = slice(h * HEAD, (h + 1) * HEAD)
    qh = q_ref[:, hs]
    kcat = jnp.concatenate([kp_ref[:, hs], kc_ref[:, hs]], axis=0)
    s = lax.dot_general(qh, kcat, _DN["nt"], preferred_element_type=F32)
    qi = lax.broadcasted_iota(jnp.int32, (QB, 2 * QB), 0)
    kj = lax.broadcasted_iota(jnp.int32, (QB, 2 * QB), 1)
    has_prev = pl.program_id(1) != 0
    valid = (kj >= qi) & (kj <= qi + QB) & ((kj >= QB) | has_prev)
    return qh, kcat, s, valid


def _dilated_call(name, body, S, dil, ins, n_out):
    L = S // dil

    def spec(per_token, cb, prev):
        if prev:
            return pl.BlockSpec((QB, BW), lambda r, i: (jnp.maximum(i - 1, 0), r * per_token + cb))
        return pl.BlockSpec((QB, BW), lambda r, i: (i, r * per_token + cb))

    outs = pl.pallas_call(
        body, name=name, grid=(dil, L // QB),
        in_specs=[spec(*rest) for _, *rest in ins], out_specs=[spec(1, 0, False)] * n_out,
        out_shape=[jax.ShapeDtypeStruct((L, dil * BW), F32)] * n_out,
        compiler_params=_params(("parallel", "parallel")),
    )(*[a.reshape(L, -1) for a, *_ in ins])
    return [o.reshape(S, BW) for o in outs]


def _attn_fwd(name, q, k, v, dil):
    def body(q_ref, kp_ref, kc_ref, vp_ref, vc_ref, o_ref, l_ref):
        outs, lses = [], []
        for h in range(N_HEADS):
            hs = slice(h * HEAD, (h + 1) * HEAD)
            _, _, s, valid = _scores(q_ref, kp_ref, kc_ref, h)
            s = jnp.where(valid, s, NEG)
            m = jnp.max(s, axis=-1, keepdims=True)
            p = jnp.exp(s - m)
            den = jnp.sum(p, axis=-1, keepdims=True)
            vcat = jnp.concatenate([vp_ref[:, hs], vc_ref[:, hs]], axis=0)
            outs.append(jnp.dot(p.astype(BF16), vcat, preferred_element_type=F32) / den)
            lses.append(jnp.broadcast_to(m + jnp.log(den), (QB, HEAD)))
        o_ref[...] = jnp.concatenate(outs, axis=1)
        l_ref[...] = jnp.concatenate(lses, axis=1)

    ins = [(q, 1, 0, False), (k, 1, 0, True), (k, 1, 0, False), (v, 1, 0, True), (v, 1, 0, False)]
    return _dilated_call(name, body, q.shape[0], dil, ins, 2)


def _attn_bwd(name, q, k, v, dyab, lse, delta, dil):
    def body(q_ref, kp_ref, kc_ref, vp_ref, vc_ref, do_ref, l_ref, dl_ref, dq_ref, dkc_ref, dkp_ref, dvc_ref, dvp_ref):
        dq, dkc, dkp, dvc, dvp = [], [], [], [], []
        for h in range(N_HEADS):
            hs = slice(h * HEAD, (h + 1) * HEAD)
            qh, kcat, s, valid = _scores(q_ref, kp_ref, kc_ref, h)
            p = jnp.where(valid, jnp.exp(s - l_ref[:, h * HEAD:h * HEAD + 1]), 0.0)
            doh = do_ref[:, hs].astype(BF16)
            vcat = jnp.concatenate([vp_ref[:, hs], vc_ref[:, hs]], axis=0)
            dp = lax.dot_general(doh, vcat, _DN["nt"], preferred_element_type=F32)
            ds = (p * (dp - dl_ref[:, h * HEAD:h * HEAD + 1])).astype(BF16)
            dq.append(jnp.dot(ds, kcat, preferred_element_type=F32))
            dk = lax.dot_general(ds, qh, _DN["tn"], preferred_element_type=F32)
            dv = lax.dot_general(p.astype(BF16), doh, _DN["tn"], preferred_element_type=F32)
            dkp.append(dk[:QB])
            dkc.append(dk[QB:])
            dvp.append(dv[:QB])
            dvc.append(dv[QB:])
        dq_ref[...] = jnp.concatenate(dq, axis=1)
        dkc_ref[...] = jnp.concatenate(dkc, axis=1)
        dkp_ref[...] = jnp.concatenate(dkp, axis=1)
        dvc_ref[...] = jnp.concatenate(dvc, axis=1)
        dvp_ref[...] = jnp.concatenate(dvp, axis=1)

    ins = [(q, 1, 0, False), (k, 1, 0, True), (k, 1, 0, False), (v, 1, 0, True), (v, 1, 0, False),
           (dyab, 2, 1, False), (lse, 1, 0, False), (delta, 1, 0, False)]
    return _dilated_call(name, body, q.shape[0], dil, ins, 5)


def _merge_fwd(name, branches, tm=256):
    S = branches[0][0].shape[0]

    def body(o1, l1, o2, l2, o3, l3, y_ref, yb_ref, l_ref):
        ls = [l1[...], l2[...], l3[...]]
        m = jnp.maximum(jnp.maximum(ls[0], ls[1]), ls[2])
        es = [jnp.exp(l - m) for l in ls]
        den = es[0] + es[1] + es[2]
        y = (es[0] * o1[...] + es[1] * o2[...] + es[2] * o3[...]) / den
        y_ref[...] = y
        yb_ref[...] = y.astype(BF16)
        l_ref[...] = m + jnp.log(den)

    ins = [(t, ROW) for pair in branches for t in pair]
    return _rowcall(name, body, S, tm, ins, [((S, BW), F32, ROW), ((S, BW), BF16, ROW), ((S, BW), F32, ROW)])


def _delta(name, dyab, yb, ones_bd, tm=256):
    S = yb.shape[0]

    def body(d_ref, y_ref, o_ref, out_ref):
        out_ref[...] = _gsum64(d_ref[...] * y_ref[...], o_ref[...])

    return _rowcall(name, body, S, tm, [(dyab, ("row", BW, 1)), (yb, ROW), (ones_bd, FULL)], [((S, BW), F32, ROW)])[0]


def _qkv_bwd(name, z, cos, sin, ones_bd, qg, kg, pieces):
    S = z.shape[0]
    nblk = S // QB

    def body(q_ref, k_ref, c_ref, s_ref, o_ref, qg_ref, kg_ref, *rest):
        pr, (dz_ref, dqg_ref, dkg_ref) = rest[:15], rest[15:]
        i = pl.program_id(0)
        c, s, ob = _tile4(c_ref[...]), _tile4(s_ref[...]), o_ref[...]
        dq = dk = dv = None
        for b, dil in enumerate(DILATIONS):
            a_q, a_kc, a_kp, a_vc, a_vp = [r[...] for r in pr[5 * b:5 * b + 5]]
            live = ((i + dil) < nblk).astype(F32)
            tq, tk, tv = a_q, a_kc + a_kp * live, a_vc + a_vp * live
            dq, dk, dv = (tq, tk, tv) if b == 0 else (dq + tq, dk + tk, dv + tv)

        def back(x, g, d_rot, acc_ref):
            r = lax.rsqrt(_gsum64(x * x, ob) * (1.0 / HEAD) + EPS)
            n = x * r
            dxn = d_rot * c + _swap32(d_rot * s)
            _acc_add(i, acc_ref, _colsum(dxn * n))
            dn = dxn * g
            return r * (dn - n * (_gsum64(dn * n, ob) * (1.0 / HEAD)))

        dz_ref[:, :BW] = back(q_ref[...], qg_ref[...], dq * QK_SCALE, dqg_ref).astype(BF16)
        dz_ref[:, BW:2 * BW] = back(k_ref[...], kg_ref[...], dk, dkg_ref).astype(BF16)
        dz_ref[:, 2 * BW:] = dv.astype(BF16)

    ins = [(z, ("row", BW, 2)), (z, ("row", BW, 3)), (cos, ROW), (sin, ROW), (ones_bd, FULL), (qg, FULL), (kg, FULL)]
    for (a_q, a_kc, a_kp, a_vc, a_vp), dil in zip(pieces, DILATIONS):
        ins += [(a_q, ROW), (a_kc, ROW), (a_kp, ("off", dil, None, 0)), (a_vc, ROW), (a_vp, ("off", dil, None, 0))]
    return _rowcall(name, body, S, QB, ins,
                    [((S, 3 * BW), BF16, ROW), ((1, BW), F32, ACC), ((1, BW), F32, ACC)])


def _local_step(x0, tgt, pos, mod, wb, sp, pipe):
    S, D = x0.shape
    md = lambda l, j: mod[l, j:j + 1]
    sh_m, sc_m, g_m, sh_f, sc_f, g_f = ([md(l, j) for l in range(2)] for j in range(6))
    nm_g, nf_g = sp["norm_mix_g"], sp["norm_ffn_g"]

    inv_freq = 1.0 / (ROPE_THETA ** (jnp.arange(0, HEAD, 2, dtype=F32) / HEAD))
    ang = pos.astype(F32)[:, None] * inv_freq
    cs, sn = jnp.cos(ang), jnp.sin(ang)
    cos = jnp.concatenate([cs, cs, cs, cs], axis=1)
    sin = jnp.concatenate([-sn, sn, -sn, sn], axis=1)
    head_of = jnp.arange(BW) // HEAD
    ones_bd = (head_of[:, None] == head_of[None, :]).astype(BF16)
    qg = jnp.tile(sp["b_q_norm_g"].reshape(1, HEAD), (1, N_HEADS))
    kg = jnp.tile(sp["b_k_norm_g"].reshape(1, HEAD), (1, N_HEADS))
    vg = sp["a_vnorm_g"].reshape(1, A_GROUPS * LANES)
    ws = sp["a_spatial_w"][0]
    bias_full = jnp.repeat(sp["a_spatial_b"][0].T, LANES, axis=1)
    ffn_s = [(sp["ffn_dw_w"][l], sp["ffn_dw_b"][l:l + 1]) for l in range(2)]

    h0 = _mod_first("l0_mod", x0, nm_g[0:1], sc_m[0], sh_m[0])
    z = _matmul("l0_in", h0, wb.get("w_in", h0), "nn", F32, tm=2048)
    ya = _mixa_fwd("l0_mixa", z, vg, ws, bias_full)
    qkv = _qkv_fwd("l0_qkv", z, cos, sin, ones_bd, qg, kg)
    branches = [_attn_fwd(f"l0_att{dil}", *qkv, dil) for dil in DILATIONS]
    yb, yb16, lse = _merge_fwd("l0_merge", branches)
    yab = jnp.concatenate([ya, yb16], axis=1)
    y0 = _matmul("l0_out", yab, wb.get("w_out", yab), "nn", F32, tm=1024, tn=1024)
    x1, h1 = _resid_mod("l0_res1", x0, y0, g_m[0], nf_g[0:1], sc_f[0], sh_f[0])
    ffn_w = [(wb.get("up0", h1), wb.get("dn0", h1), *ffn_s[0])]
    f0, (u0, hm0) = _ffn_fwd("l0_ffn", h1, *ffn_w[0])
    x2, h2 = _resid_mod("l0_res2", x1, f0, g_f[0], nm_g[1:2], sc_m[1], sh_m[1])
    p = _matmul("l1_pw1", h2, wb.get("pw1", h2), "nn", F32, tm=2048, bias=sp["conv_pw1_b"])
    y2 = _glu31_fwd("l1_glu", p, sp["conv_dw_w"][0], sp["conv_dw_b"])
    y4 = _ln_silu_fwd("l1_ln", y2, sp["conv_ln_g"], sp["conv_ln_b"])
    y1 = _matmul("l1_pw2", y4, wb.get("pw2", y4), "nn", F32, tm=1024, tn=1024, bias=sp["conv_pw2_b"])
    x3, h3 = _resid_mod("l1_res1", x2, y1, g_m[1], nf_g[1:2], sc_f[1], sh_f[1])
    ffn_w.append((wb.get("up1", h3), wb.get("dn1", h3), *ffn_s[1]))
    f1, (u1, hm1) = _ffn_fwd("l1_ffn", h3, *ffn_w[1])
    dx4, lossv = _loss_head("loss", x3, f1, g_f[1], tgt)

    dy, dgate_f1, _ = _gate_bwd("l1_gate2", dx4, f1, g_f[1])
    dh, gf1 = _ffn_bwd("l1_ffn", dy, h3, u1, hm1, *ffn_w[1])
    tok = pipe.scatter("g1", dict(dn1=gf1["dn"], up1=gf1["up"]))
    dx3, dsh_f1, dsc_f1, dnf1 = _mod_bwd("l1_dmod2", dx4, dh, x3, nf_g[1:2], sc_f[1] + tok)
    dy, dgate_m1, dpw2_b = _gate_bwd("l1_gate1", dx3, y1, g_m[1])
    dy4 = _matmul("l1_dpw2_x", dy, wb.get("pw2"), "nt", F32, tm=1024, tn=1024)
    g_pw2 = _matmul("l1_dpw2_w", y4, dy, "tn", BF16, tm=1024, tn=1024)
    dy2, dln_g, dln_b = _ln_silu_bwd("l1_dln", y2, dy4, sp["conv_ln_g"], sp["conv_ln_b"])
    dp, ddw_w, ddw_b, dpw1_b = _glu31_bwd("l1_dglu", p, dy2, sp["conv_dw_w"][0])
    g_pw1 = _matmul("l1_dpw1_w", h2, dp, "tn", BF16, tm=1024, tn=1024)
    tok = pipe.scatter("g2", dict(pw2=g_pw2, pw1=g_pw1))
    pipe.collect("g1", g_pw1)
    dh = _matmul("l1_dpw1_x", dp, wb.get("pw1"), "nt", F32, tm=1024, tn=1024, tk=2048)
    dx2, dsh_m1, dsc_m1, dnm1 = _mod_bwd("l1_dmod1", dx3, dh, x2, nm_g[1:2], sc_m[1] + tok)
    dy, dgate_f0, _ = _gate_bwd("l0_gate2", dx2, f0, g_f[0])
    dh, gf0 = _ffn_bwd("l0_ffn", dy, h1, u0, hm0, *ffn_w[0])
    pipe.finish("g1", gf0["up"])
    pipe.collect("g2", gf0["up"])
    dx1, dsh_f0, dsc_f0, dnf0 = _mod_bwd("l0_dmod2", dx2, dh, x1, nf_g[0:1], sc_f[0])
    dy, dgate_m0, _ = _gate_bwd("l0_gate1", dx1, y0, g_m[0])
    dyab = _matmul("l0_dout_x", dy, wb.get("w_out"), "nt", F32, tm=1024, tn=1024)
    g_out = _matmul("l0_dout_w", yab, dy, "tn", BF16, tm=1024, tn=1024)
    tok = pipe.scatter("g3", dict(dn0=gf0["dn"], up0=gf0["up"], w_out=g_out))
    vg = vg + tok
    dza, dws, dbf, dvg = _mixa_bwd("l0_dmixa", z, dyab, vg, ws, bias_full)
    delta = _delta("l0_delta", dyab, yb, ones_bd)
    pieces = [_attn_bwd(f"l0_datt{dil}", *qkv, dyab, lse, delta, dil) for dil in DILATIONS]
    dzb, dqg, dkg = _qkv_bwd("l0_dqkv", z, cos, sin, ones_bd, qg, kg, pieces)
    dz = jnp.concatenate([dza, dzb], axis=1)
    g_in = _matmul("l0_din_w", h0, dz, "tn", BF16, tm=1024, tn=1280)
    tok = pipe.scatter("g4", dict(w_in=g_in))
    pipe.finish("g2", g_in)
    pipe.collect("g3", g_in)
    dh = _matmul("l0_din_x", dz, wb.get("w_in"), "nt", F32, tm=1024, tn=1024, tk=2560)
    grad_x, dsh_m0, dsc_m0, dnm0 = _mod_bwd("l0_dmod1", dx1, dh, x0, nm_g[0:1], sc_m[0] + tok)

    dmod = jnp.stack([jnp.concatenate([dsh_m0, dsc_m0, dgate_m0, dsh_f0, dsc_f0, dgate_f0], axis=0),
                      jnp.concatenate([dsh_m1, dsc_m1, dgate_m1, dsh_f1, dsc_f1, dgate_f1], axis=0)])
    small = dict(
        norm_mix_g=jnp.concatenate([dnm0, dnm1], axis=0),
        norm_ffn_g=jnp.concatenate([dnf0, dnf1], axis=0),
        a_vnorm_g=dvg.reshape(1, A_GROUPS, LANES),
        a_spatial_w=dws[None],
        a_spatial_b=dbf.reshape(CHUNK, A_GROUPS, LANES).sum(-1).T[None],
        b_q_norm_g=dqg.reshape(N_HEADS, HEAD).sum(0)[None],
        b_k_norm_g=dkg.reshape(N_HEADS, HEAD).sum(0)[None],
        conv_pw1_b=dpw1_b, conv_dw_w=ddw_w[None, :sp["conv_dw_w"].shape[1]], conv_dw_b=ddw_b,
        conv_ln_g=dln_g, conv_ln_b=dln_b, conv_pw2_b=dpw2_b,
        ffn_dw_w=jnp.stack([gf0["dw_w"], gf1["dw_w"]]),
        ffn_dw_b=jnp.concatenate([gf0["dw_b"], gf1["dw_b"]], axis=0),
    )
    return lossv, grad_x, small, dmod


ADA_TN = 512


def _ada_fwd(name, c_all, ada_w, ada_b_sh):
    L, D, N = ada_w.shape
    B = c_all.shape[0]

    def body(c_ref, w_ref, b_ref, o_ref):
        cv = c_ref[...]
        ca = (cv * _sigmoid(cv)).astype(BF16)
        o_ref[0] = jnp.dot(ca, w_ref[0].astype(BF16), preferred_element_type=F32) + b_ref[0]

    return pl.pallas_call(
        body, name=name, grid=(L, N // ADA_TN),
        in_specs=[pl.BlockSpec((B, D), lambda l, j: (0, 0)), pl.BlockSpec((1, D, ADA_TN), lambda l, j: (l, 0, j)),
                  pl.BlockSpec((1, 1, ADA_TN), lambda l, j: (l, 0, j))],
        out_specs=pl.BlockSpec((1, B, ADA_TN), lambda l, j: (l, 0, j)),
        out_shape=jax.ShapeDtypeStruct((L, B, N), F32),
        compiler_params=_params(("parallel", "parallel")),
    )(c_all, ada_w, ada_b_sh.reshape(L, 1, N))


def _adamw_val(w, g, m, v):
    m2 = ADAM_B1 * m + (1.0 - ADAM_B1) * g
    v2 = ADAM_B2 * v + (1.0 - ADAM_B2) * (g * g)
    m_hat = m2 / (1.0 - ADAM_B1 ** ADAM_STEP)
    v_hat = v2 / (1.0 - ADAM_B2 ** ADAM_STEP)
    delta = -ADAM_LR * (m_hat / (jnp.sqrt(v_hat) + ADAM_EPS) + ADAM_WD * w)
    return delta, m2, v2


def _ada_update(name, c_all, dmod_sh, w, m, v):
    L, D, N = w.shape
    B = c_all.shape[0]

    def body(c_ref, d_ref, w_ref, m_ref, v_ref, g_ref, dl_ref, mo_ref, vo_ref):
        cv = c_ref[...]
        ca = (cv * _sigmoid(cv)).astype(BF16)
        g = lax.dot_general(ca, d_ref[0].astype(BF16), _DN["tn"], preferred_element_type=F32)
        g_ref[0] = g
        dl_ref[0], mo_ref[0], vo_ref[0] = _adamw_val(w_ref[0], g, m_ref[0], v_ref[0])

    wspec = pl.BlockSpec((1, D, ADA_TN), lambda l, j: (l, 0, j))
    return pl.pallas_call(
        body, name=name, grid=(L, N // ADA_TN),
        in_specs=[pl.BlockSpec((B, D), lambda l, j: (0, 0)), pl.BlockSpec((1, B, ADA_TN), lambda l, j: (l, 0, j)),
                  wspec, wspec, wspec],
        out_specs=[wspec] * 4, out_shape=[jax.ShapeDtypeStruct((L, D, N), F32)] * 4,
        compiler_params=_params(("parallel", "parallel")),
    )(c_all, dmod_sh, w, m, v)


def _adamw(name, w, g, m, v):
    R, C = w.shape
    tm = R
    for cand in (256, 128, 64, 32, 16, 8):
        if R % cand == 0 and cand * C * 4 <= (1 << 20):
            tm = cand
            break

    def body(w_ref, g_ref, m_ref, v_ref, d_ref, mo_ref, vo_ref):
        d_ref[...], mo_ref[...], vo_ref[...] = _adamw_val(w_ref[...], g_ref[...], m_ref[...], v_ref[...])

    return _rowcall(name, body, R, tm, [(w, ROW), (g, ROW), (m, ROW), (v, ROW)], [((R, C), F32, ROW)] * 3)


def _row_tile(rows, width, itemsize=4, limit=1 << 20):
    for cand in (512, 256, 128, 64, 32, 16):
        if rows % cand == 0 and cand * width * itemsize <= limit:
            return cand
    raise ValueError((rows, width))


def _cast_into_full(name, a, layer, q, kind):
    L, r, c = a.shape
    tm = _row_tile(r, c)
    if kind == "col":
        full, o_spec = (r, N_CHIPS * c), pl.BlockSpec((tm, c), lambda i, q_ref: (i, q_ref[0]))
    else:
        full, o_spec = (N_CHIPS * r, c), pl.BlockSpec((tm, c), lambda i, q_ref: (q_ref[0] * (r // tm) + i, 0))

    def body(q_ref, a_ref, o_ref):
        o_ref[...] = a_ref[0].astype(BF16)

    return pl.pallas_call(
        body, name=name,
        grid_spec=pltpu.PrefetchScalarGridSpec(
            num_scalar_prefetch=1, grid=(r // tm,),
            in_specs=[pl.BlockSpec((1, tm, c), lambda i, q_ref: (layer, i, 0))], out_specs=o_spec),
        out_shape=jax.ShapeDtypeStruct(full, BF16), compiler_params=_params(("parallel",)),
    )(q.reshape(1).astype(jnp.int32), a)


def _sum4(name, g, rcv, q, kind, n):
    r, c = rcv.shape[1:]
    tm = _row_tile(r, c)
    if kind == "col":
        g_spec = pl.BlockSpec((tm, n), lambda i, q_ref: (i, q_ref[0]))
    else:
        g_spec = pl.BlockSpec((tm, c), lambda i, q_ref: (q_ref[0] * (n // tm) + i, 0))

    def body(q_ref, g_ref, r_ref, o_ref):
        acc = g_ref[...].astype(F32)
        for j in range(3):
            acc = acc + r_ref[j].astype(F32)
        o_ref[...] = acc

    return pl.pallas_call(
        body, name=name,
        grid_spec=pltpu.PrefetchScalarGridSpec(
            num_scalar_prefetch=1, grid=(r // tm,),
            in_specs=[g_spec, pl.BlockSpec((3, tm, c), lambda i, q_ref: (0, i, 0))],
            out_specs=pl.BlockSpec((tm, c), lambda i, q_ref: (i, 0))),
        out_shape=jax.ShapeDtypeStruct((r, c), F32), compiler_params=_params(("parallel",)),
    )(q.reshape(1).astype(jnp.int32), g, rcv)


def _adamw_sum(name, w, m, v, layer, mine, theirs, prev):
    L, r, c = w.shape
    tm = _row_tile(r, c, limit=1 << 19)
    lay = pl.BlockSpec((1, tm, c), lambda i: (layer, i, 0))
    flat = pl.BlockSpec((tm, c), lambda i: (i, 0))
    n_prev = 0 if prev is None else 4

    def body(w_ref, m_ref, v_ref, a_ref, b_ref, *rest):
        g_ref, d_ref, mo_ref, vo_ref = rest[n_prev:]
        g = a_ref[...] + b_ref[...]
        g_ref[0] = g
        d_ref[0], mo_ref[0], vo_ref[0] = _adamw_val(w_ref[0], g, m_ref[0], v_ref[0])

    return pl.pallas_call(
        body, name=name, grid=(r // tm,),
        in_specs=[lay, lay, lay, flat, flat] + [ANY] * n_prev, out_specs=[lay] * 4,
        out_shape=[jax.ShapeDtypeStruct((L, r, c), F32)] * 4,
        input_output_aliases={5 + k: k for k in range(n_prev)},
        compiler_params=_params(("parallel",)),
    )(w, m, v, mine, theirs, *(prev or ()))


def _sum8(name, gathered):
    R, N = gathered.shape
    P = R // 8

    def body(g_ref, o_ref):
        acc = g_ref[0:P, :]
        for d in range(1, 8):
            acc = acc + g_ref[d * P:(d + 1) * P, :]
        o_ref[...] = acc

    return pl.pallas_call(body, name=name, out_shape=jax.ShapeDtypeStruct((P, N), F32),
                          compiler_params=pltpu.CompilerParams(vmem_limit_bytes=VMEM_LIMIT))(gathered)


ANY = pl.BlockSpec(memory_space=pl.ANY)


def _mesh_pos():
    x, y, c = lax.axis_index("x"), lax.axis_index("y"), lax.axis_index("c")
    other_chips = [(1 - x, y), (x, 1 - y), (1 - x, 1 - y)]
    return x, y, c, other_chips


def _allgather8(name, blk):
    m_per, n = blk.shape

    def body(x_ref, out_ref, send_sems, recv_sems, local_sem):
        x, y, c, chips = _mesh_pos()
        me, sibling = (x, y, c), (x, y, 1 - c)

        def rows(px, py, pc):
            return out_ref.at[pl.ds((4 * px + 2 * py + pc) * m_per, m_per), :]

        def copy(k, block, to, src=None):
            return pltpu.make_async_remote_copy(
                src_ref=rows(*block) if src is None else src, dst_ref=rows(*block),
                send_sem=send_sems.at[k], recv_sem=recv_sems.at[k], device_id=to, device_id_type=MESH)

        mine = pltpu.make_async_copy(x_ref, rows(*me), local_sem)
        mine.start()
        first = [copy(0, me, sibling, src=x_ref)]
        first += [copy(1 + j, me, (*chip, c), src=x_ref) for j, chip in enumerate(chips)]
        for cp in first:
            cp.start()
        passed = [copy(4 + j, (*chip, c), sibling) for j, chip in enumerate(chips)]
        for j, chip in enumerate(chips):
            copy(1 + j, (*chip, c), me).wait_recv()
            passed[j].start()
        copy(0, sibling, me).wait_recv()
        for j, chip in enumerate(chips):
            copy(4 + j, (*chip, 1 - c), me).wait_recv()
        for cp in first + passed:
            cp.wait_send()
        mine.wait()

    return pl.pallas_call(
        body, name=name, out_shape=jax.ShapeDtypeStruct((8 * m_per, n), blk.dtype),
        in_specs=[pl.BlockSpec(memory_space=pltpu.VMEM)], out_specs=pl.BlockSpec(memory_space=pltpu.VMEM),
        scratch_shapes=[pltpu.SemaphoreType.DMA((7,)), pltpu.SemaphoreType.DMA((7,)), pltpu.SemaphoreType.DMA],
        compiler_params=pltpu.CompilerParams(vmem_limit_bytes=VMEM_LIMIT),
    )(blk)


BIG = dict(w_in=("col", "ab_w_in", 0), w_out=("row", "ab_w_out", 0), up0=("col", "ffn_up_w", 0),
           dn0=("row", "ffn_down_w", 0), pw1=("col", "conv_pw1_w", 0), pw2=("row", "conv_pw2_w", 0),
           up1=("col", "ffn_up_w", 1), dn1=("row", "ffn_down_w", 1))
N_CHIPS = 4
HBM = pl.BlockSpec(memory_space=pltpu.HBM)
SEM = pl.BlockSpec(memory_space=pltpu.SEMAPHORE)
EFFECT = pltpu.SideEffectType.DATAFLOW_SIDE_EFFECTING


def _region(kind, ref, q, n):
    if kind == "col":
        return ref.at[:, pl.ds(q * n, n)]
    return ref.at[pl.ds(q * n, n), :]


def _gather_plan(kind, n):
    def remote(src, land, pos):
        x, y, c, chips = pos
        mine = _region(kind, land, 2 * x + y, n)
        return [(mine, mine, (*chip, c)) for chip in chips]

    return ("gather", kind, n), remote


def _scatter_plan(kind, n):
    def remote(src, land, pos):
        _, _, c, chips = pos
        return [(_region(kind, src, 2 * chip[0] + chip[1], n), land.at[j], (*chip, c)) for j, chip in enumerate(chips)]

    return ("scatter", kind, n), remote


def _sibling_plan():
    def remote(src, land, pos):
        x, y, c, _ = pos
        return [(src, land, (x, y, 1 - c))]

    return ("sibling",), remote


def _split_start(name, items, after=None):
    n = len(items)
    plans = [it[2] for it in items]
    n_in = 2 * n + (after is not None)

    def body(*refs):
        srcs, lands = refs[:n], refs[n:2 * n]
        sends, recvs = refs[n_in:n_in + n], refs[n_in + n:n_in + 2 * n]
        token = refs[n_in + 4 * n]
        pos = _mesh_pos()
        for a, (_, remote) in enumerate(plans):
            for k, (s, d, dev) in enumerate(remote(srcs[a], lands[a], pos)):
                pltpu.make_async_remote_copy(src_ref=s, dst_ref=d, send_sem=sends[a].at[k], recv_sem=recvs[a].at[k],
                                             device_id=dev, device_id_type=MESH).start()
        token[...] = jnp.zeros_like(token)

    sems = [pltpu.SemaphoreType.DMA((it[3],)) for it in items]
    bufs = [pltpu.HBM(it[k].shape, it[k].dtype) for k in (0, 1) for it in items]
    outs = pl.pallas_call(
        body, name=name, out_shape=[*sems, *sems, *bufs, jax.ShapeDtypeStruct((8, LANES), F32)],
        in_specs=[HBM] * (2 * n) + [ANY] * (n_in - 2 * n),
        out_specs=[SEM] * (2 * n) + [HBM] * (2 * n) + [pl.BlockSpec(memory_space=pltpu.VMEM)],
        input_output_aliases={i: 2 * n + i for i in range(2 * n)},
        compiler_params=pltpu.CompilerParams(has_side_effects=EFFECT),
    )(*[pltpu.with_memory_space_constraint(it[k], pltpu.HBM) for k in (0, 1) for it in items],
      *(() if after is None else (after,)))
    state = [(items[a][2], items[a][3], outs[2 * n + a], outs[3 * n + a], outs[a], outs[n + a]) for a in range(n)]
    return state, outs[4 * n]


def _split_wait(name, state, after):
    n = len(state)

    def body(*refs):
        srcs, lands = refs[:n], refs[n:2 * n]
        sends, recvs = refs[2 * n:3 * n], refs[3 * n:4 * n]
        pos = _mesh_pos()
        for a, ((_, remote), *_) in enumerate(state):
            for k, (s, d, dev) in enumerate(remote(srcs[a], lands[a], pos)):
                cp = pltpu.make_async_remote_copy(src_ref=s, dst_ref=d, send_sem=sends[a].at[k], recv_sem=recvs[a].at[k],
                                                  device_id=dev, device_id_type=MESH)
                cp.wait_send()
                cp.wait_recv()

    bufs = [st[k] for k in (2, 3) for st in state]
    outs = pl.pallas_call(
        body, name=name, out_shape=[pltpu.HBM(b.shape, b.dtype) for b in bufs],
        in_specs=[HBM] * (2 * n) + [SEM] * (2 * n) + [ANY], out_specs=[HBM] * (2 * n),
        input_output_aliases={i: i for i in range(2 * n)},
        compiler_params=pltpu.CompilerParams(has_side_effects=EFFECT),
    )(*bufs, *[st[k] for k in (4, 5) for st in state], after)
    return outs[:n], outs[n:]


class _Weights:
    def __init__(self, w, q, after):
        items = []
        unused = jnp.zeros((16, LANES), BF16)
        for name, (kind, pname, layer) in BIG.items():
            _, r, c = w[pname].shape
            land = _cast_into_full(f"cast_{name}", w[pname], layer, q, kind)
            items.append((unused, land, _gather_plan(kind, c if kind == "col" else r), N_CHIPS - 1))
        state, self.token = _split_start("gw_start", items, after)
        self.pending = dict(zip(BIG, state))
        self.ready = {}

    def get(self, name, after=None):
        if name not in self.ready:
            self.ready[name] = _split_wait(f"gw_wait_{name}", [self.pending.pop(name)], after)[1][0]
        return self.ready[name]


class _GradPipe:
    def __init__(self, q, w, m, v):
        self.q, self.w, self.m, self.v = q, w, m, v
        self.stage, self.results = {}, {}

    def scatter(self, group, grads, after=None):
        items = []
        for name, g in grads.items():
            kind = BIG[name][0]
            rows, cols = g.shape
            n = (cols if kind == "col" else rows) // N_CHIPS
            reg = (rows, n) if kind == "col" else (n, cols)
            items.append((g, lax.empty((N_CHIPS - 1, *reg), BF16), _scatter_plan(kind, n), N_CHIPS - 1))
        state, token = _split_start(f"gs_start_{group}", items, after)
        self.stage[group] = (list(grads), state)
        return token[0, 0]

    def collect(self, group, after):
        names, state = self.stage[group]
        srcs, lands = _split_wait(f"gs_wait_{group}", state, after)
        items = []
        for name, st, g, land in zip(names, state, srcs, lands):
            _, kind, n = st[0][0]
            part = _sum4(f"sum_{name}", g, land, self.q, kind, n)
            items.append((part, lax.empty(part.shape, F32), _sibling_plan(), 1))
        self.stage[group] = (names, _split_start(f"sw_start_{group}", items)[0])

    def finish(self, group, after):
        names, state = self.stage.pop(group)
        srcs, lands = _split_wait(f"sw_wait_{group}", state, after)
        for name, mine, theirs in zip(names, srcs, lands):
            _, pname, layer = BIG[name]
            self.results[pname] = _adamw_sum(f"adamw_{name}", self.w[pname], self.m[pname], self.v[pname], layer,
                                             mine, theirs, self.results.get(pname))


PACK_ROWS = 8


def _pack(arrays):
    flat = jnp.concatenate([a.reshape(-1) for a in arrays])
    n = flat.shape[0]
    padded = -(-n // (PACK_ROWS * LANES)) * (PACK_ROWS * LANES)
    return jnp.pad(flat, (0, padded - n)).reshape(PACK_ROWS, padded // PACK_ROWS)


def _unpack(packed, shapes):
    flat = packed.reshape(-1)
    out, off = [], 0
    for s in shapes:
        n = 1
        for d in s:
            n *= d
        out.append(flat[off:off + n].reshape(s))
        off += n
    return out


REPLICATED = ("ada_b", "norm_mix_g", "norm_ffn_g", "a_vnorm_g", "a_spatial_w", "a_spatial_b", "b_q_norm_g",
              "b_k_norm_g", "ffn_dw_b")
SMALL_SHARDED = ("conv_pw1_b", "conv_dw_w", "conv_dw_b", "conv_ln_g", "conv_ln_b", "conv_pw2_b", "ffn_dw_w")
WEIGHTS = ("ada_w", "ada_b", "norm_mix_g", "norm_ffn_g", "ab_w_in", "a_vnorm_g", "a_spatial_w", "a_spatial_b",
           "b_q_norm_g", "b_k_norm_g", "ab_w_out", "conv_pw1_w", "conv_pw1_b", "conv_dw_w", "conv_dw_b", "conv_ln_g",
           "conv_ln_b", "conv_pw2_w", "conv_pw2_b", "ffn_up_w", "ffn_dw_w", "ffn_dw_b", "ffn_down_w")

def kernel(x, c, positions, ada_w, ada_b, norm_mix_g, norm_ffn_g, ab_w_in, a_vnorm_g, a_spatial_w, a_spatial_b, b_q_norm_g, b_k_norm_g, ab_w_out, conv_pw1_w, conv_pw1_b, conv_dw_w, conv_dw_b, conv_ln_g, conv_ln_b, conv_pw2_w, conv_pw2_b, ffn_up_w, ffn_dw_w, ffn_dw_b, ffn_down_w, loss_target, m_ada_w, m_ada_b, m_norm_mix_g, m_norm_ffn_g, m_ab_w_in, m_a_vnorm_g, m_a_spatial_w, m_a_spatial_b, m_b_q_norm_g, m_b_k_norm_g, m_ab_w_out, m_conv_pw1_w, m_conv_pw1_b, m_conv_dw_w, m_conv_dw_b, m_conv_ln_g, m_conv_ln_b, m_conv_pw2_w, m_conv_pw2_b, m_ffn_up_w, m_ffn_dw_w, m_ffn_dw_b, m_ffn_down_w, v_ada_w, v_ada_b, v_norm_mix_g, v_norm_ffn_g, v_ab_w_in, v_a_vnorm_g, v_a_spatial_w, v_a_spatial_b, v_b_q_norm_g, v_b_k_norm_g, v_ab_w_out, v_conv_pw1_w, v_conv_pw1_b, v_conv_dw_w, v_conv_dw_b, v_conv_ln_g, v_conv_ln_b, v_conv_pw2_w, v_conv_pw2_b, v_ffn_up_w, v_ffn_dw_w, v_ffn_dw_b, v_ffn_down_w):
    w = dict(ada_w=ada_w, ada_b=ada_b, norm_mix_g=norm_mix_g, norm_ffn_g=norm_ffn_g, ab_w_in=ab_w_in, a_vnorm_g=a_vnorm_g, a_spatial_w=a_spatial_w, a_spatial_b=a_spatial_b, b_q_norm_g=b_q_norm_g, b_k_norm_g=b_k_norm_g, ab_w_out=ab_w_out, conv_pw1_w=conv_pw1_w, conv_pw1_b=conv_pw1_b, conv_dw_w=conv_dw_w, conv_dw_b=conv_dw_b, conv_ln_g=conv_ln_g, conv_ln_b=conv_ln_b, conv_pw2_w=conv_pw2_w, conv_pw2_b=conv_pw2_b, ffn_up_w=ffn_up_w, ffn_dw_w=ffn_dw_w, ffn_dw_b=ffn_dw_b, ffn_down_w=ffn_down_w)
    m = dict(ada_w=m_ada_w, ada_b=m_ada_b, norm_mix_g=m_norm_mix_g, norm_ffn_g=m_norm_ffn_g, ab_w_in=m_ab_w_in, a_vnorm_g=m_a_vnorm_g, a_spatial_w=m_a_spatial_w, a_spatial_b=m_a_spatial_b, b_q_norm_g=m_b_q_norm_g, b_k_norm_g=m_b_k_norm_g, ab_w_out=m_ab_w_out, conv_pw1_w=m_conv_pw1_w, conv_pw1_b=m_conv_pw1_b, conv_dw_w=m_conv_dw_w, conv_dw_b=m_conv_dw_b, conv_ln_g=m_conv_ln_g, conv_ln_b=m_conv_ln_b, conv_pw2_w=m_conv_pw2_w, conv_pw2_b=m_conv_pw2_b, ffn_up_w=m_ffn_up_w, ffn_dw_w=m_ffn_dw_w, ffn_dw_b=m_ffn_dw_b, ffn_down_w=m_ffn_down_w)
    v = dict(ada_w=v_ada_w, ada_b=v_ada_b, norm_mix_g=v_norm_mix_g, norm_ffn_g=v_norm_ffn_g, ab_w_in=v_ab_w_in, a_vnorm_g=v_a_vnorm_g, a_spatial_w=v_a_spatial_w, a_spatial_b=v_a_spatial_b, b_q_norm_g=v_b_q_norm_g, b_k_norm_g=v_b_k_norm_g, ab_w_out=v_ab_w_out, conv_pw1_w=v_conv_pw1_w, conv_pw1_b=v_conv_pw1_b, conv_dw_w=v_conv_dw_w, conv_dw_b=v_conv_dw_b, conv_ln_g=v_conv_ln_g, conv_ln_b=v_conv_ln_b, conv_pw2_w=v_conv_pw2_w, conv_pw2_b=v_conv_pw2_b, ffn_up_w=v_ffn_up_w, ffn_dw_w=v_ffn_dw_w, ffn_dw_b=v_ffn_dw_b, ffn_down_w=v_ffn_down_w)
    S, D = x.shape[1], x.shape[2]
    xi, yi, ci = lax.axis_index("x"), lax.axis_index("y"), lax.axis_index("c")
    q = 2 * xi + yi
    b = 2 * q + ci
    take_dev = lambda g: g.reshape(8, PACK_ROWS, -1)

    c_all = _allgather8("ag_c", c.reshape(PACK_ROWS, D // PACK_ROWS)).reshape(8, D)
    n_ada = ada_w.shape[2]
    mod_sh = _ada_fwd("ada_fwd", c_all, ada_w, lax.dynamic_slice_in_dim(ada_b, q * n_ada, n_ada, axis=1))
    sh_shapes = [mod_sh.shape] + [w[n].shape for n in SMALL_SHARDED]
    gathered_mod = _allgather8("ag_mod", _pack([mod_sh] + [w[n] for n in SMALL_SHARDED]))
    per_chip = [_unpack(blk, sh_shapes) for blk in take_dev(gathered_mod)[0::2]]
    mod_g = jnp.stack([pc[0] for pc in per_chip])
    mod_mine = lax.dynamic_index_in_dim(mod_g, b, axis=2, keepdims=False)
    mod = mod_mine.transpose(1, 0, 2).reshape(2, 6, D)
    sp = {n: jnp.concatenate([pc[1 + i] for pc in per_chip], axis=-1) for i, n in enumerate(SMALL_SHARDED)}
    sp.update({n: w[n] for n in REPLICATED if n != "ada_b"})

    wb = _Weights(w, q, gathered_mod)
    mod = mod + wb.token[0, 0]

    pipe = _GradPipe(q, w, m, v)
    lossv, grad_x, small, dmod = _local_step(x[0], loss_target[0], positions[0], mod, wb, sp, pipe)
    loss = lax.psum(0.5 * jnp.sum(lossv) / D, ("x", "y", "c"))

    small_names = [n for n in REPLICATED if n != "ada_b"] + list(SMALL_SHARDED)
    payload = [dmod.reshape(2, 6 * D)] + [small[n] for n in small_names]
    pay_shapes = [p.shape for p in payload]
    gathered = _allgather8("ag_grads", _pack(payload))
    totals = _unpack(_sum8("sum_grads", gathered), pay_shapes)
    grads = dict(zip(["ada_b"] + small_names, totals))
    for n in SMALL_SHARDED:
        n_sh = w[n].shape[-1]
        grads[n] = lax.dynamic_slice_in_dim(grads[n], q * n_sh, n_sh, axis=grads[n].ndim - 1)
    dmod_all = take_dev(gathered).reshape(8, -1)[:, :2 * 6 * D].reshape(8, 2, 6 * D)
    dmod_sh = lax.dynamic_slice_in_dim(dmod_all, q * n_ada, n_ada, axis=2).transpose(1, 0, 2)

    pipe.finish("g3", dmod_sh)
    pipe.collect("g4", dmod_sh)
    grads["ada_w"], delta_ada, m_ada, v_ada = _ada_update("ada_update", c_all, dmod_sh, ada_w, m_ada_w, v_ada_w)
    delta, new_m, new_v = dict(ada_w=delta_ada), dict(ada_w=m_ada), dict(ada_w=v_ada)
    rest = list(REPLICATED) + list(SMALL_SHARDED)
    rest_shapes = [w[n].shape for n in rest]
    outs = _adamw("adamw_small", *[_pack([src[n].reshape(w[n].shape) for n in rest]) for src in (w, grads, m, v)])
    for tgt, packed in zip((delta, new_m, new_v), outs):
        tgt.update(dict(zip(rest, _unpack(packed, rest_shapes))))
    for n in rest:
        grads[n] = grads[n].reshape(w[n].shape)
    pipe.finish("g4", outs[0])
    for n, res in pipe.results.items():
        grads[n], delta[n], new_m[n], new_v[n] = res

    return (loss, grad_x[None], *[grads[n] for n in WEIGHTS], *[delta[n] for n in WEIGHTS],
            *[new_m[n] for n in WEIGHTS], *[new_v[n] for n in WEIGHTS])
```

```python
import functools

import jax
import jax.numpy as jnp
from jax import lax
from jax.experimental import pallas as pl
from jax.experimental.pallas import tpu as pltpu

F32, BF16 = jnp.float32, jnp.bfloat16
EPS = 1e-6
NEG = -1e30
ROPE_THETA = 10000.0
LANES = 128
VMEM_LIMIT = 56 * 1024 * 1024
ADAM_LR, ADAM_B1, ADAM_B2, ADAM_EPS, ADAM_WD, ADAM_STEP = 0.001, 0.9, 0.999, 1e-08, 0.01, 10
MESH = pl.DeviceIdType.MESH


def _params(sem):
    return pltpu.CompilerParams(dimension_semantics=sem, vmem_limit_bytes=VMEM_LIMIT)


_DN = {"nn": (((1,), (0,)), ((), ())), "nt": (((1,), (1,)), ((), ())), "tn": (((0,), (0,)), ((), ()))}


def _matmul(name, a, b, mode, out_dtype, tm=512, tn=512, tk=1024, bias=None):
    if mode == "nn":
        (M, K), N = a.shape, b.shape[1]
    elif mode == "nt":
        (M, K), N = a.shape, b.shape[0]
    else:
        (K, M), N = a.shape, b.shape[1]
    tm, tn, tk = min(tm, M), min(tn, N), min(tk, K)
    assert M % tm == 0 and N % tn == 0 and K % tk == 0, (name, M, N, K, tm, tn, tk)
    nk = K // tk
    if mode == "tn":
        a_spec = pl.BlockSpec((tk, tm), lambda i, j, k: (k, i))
    else:
        a_spec = pl.BlockSpec((tm, tk), lambda i, j, k: (i, k))
    if mode == "nt":
        b_spec = pl.BlockSpec((tn, tk), lambda i, j, k: (j, k))
    else:
        b_spec = pl.BlockSpec((tk, tn), lambda i, j, k: (k, j))
    in_specs, args = [a_spec, b_spec], [a, b]
    if bias is not None:
        in_specs.append(pl.BlockSpec((1, tn), lambda i, j, k: (0, j)))
        args.append(bias)
    n_in = len(args)

    def body(*refs):
        a_ref, b_ref, o_ref = refs[0], refs[1], refs[n_in]
        p = lax.dot_general(a_ref[...], b_ref[...], _DN[mode], preferred_element_type=F32)

        def finish(acc):
            if bias is not None:
                acc = acc + refs[2][...]
            o_ref[...] = acc.astype(o_ref.dtype)

        if nk == 1:
            finish(p)
        else:
            acc_ref = refs[n_in + 1]
            k = pl.program_id(2)

            @pl.when(k == 0)
            def _():
                acc_ref[...] = p

            @pl.when(k > 0)
            def _():
                acc_ref[...] += p

            @pl.when(k == nk - 1)
            def _():
                finish(acc_ref[...])

    return pl.pallas_call(
        body, name=name, grid=(M // tm, N // tn, nk), in_specs=in_specs,
        out_specs=pl.BlockSpec((tm, tn), lambda i, j, k: (i, j)),
        out_shape=jax.ShapeDtypeStruct((M, N), out_dtype),
        scratch_shapes=[pltpu.VMEM((tm, tn), F32)] if nk > 1 else [],
        compiler_params=_params(("parallel", "parallel", "arbitrary")),
    )(*args)


def _rowcall(name, body, nrows, tm, ins, outs, scratch=()):
    nblk = nrows // tm
    assert nrows % tm == 0

    def spec(kind, shape):
        k = kind[0]
        if k == "row":
            cw, cb = kind[1] or shape[-1], kind[2]
            return pl.BlockSpec((tm, cw), lambda i: (i, cb))
        if k == "prev":
            hb, cw, cb = kind[1], kind[2] or shape[-1], kind[3]
            r = tm // hb
            return pl.BlockSpec((hb, cw), lambda i: (jnp.maximum(i * r - 1, 0), cb))
        if k == "next":
            hb, cw, cb = kind[1], kind[2] or shape[-1], kind[3]
            r, last = tm // hb, nrows // hb - 1
            return pl.BlockSpec((hb, cw), lambda i: (jnp.minimum((i + 1) * r, last), cb))
        if k == "off":
            off, cw, cb = kind[1], kind[2] or shape[-1], kind[3]
            return pl.BlockSpec((tm, cw), lambda i: (jnp.clip(i + off, 0, nblk - 1), cb))
        if k == "cls":
            dil, off = kind[1], kind[2]
            return pl.BlockSpec((dil, tm // dil, shape[-1]), lambda i: (0, jnp.clip(i + off, 0, nblk - 1), 0))
        nd = len(shape)
        return pl.BlockSpec(tuple(shape), lambda i: (0,) * nd)

    has_acc = any(o[2][0] == "acc" for o in outs)
    return pl.pallas_call(
        body, name=name, grid=(nblk,),
        in_specs=[spec(kind, a.shape) for a, kind in ins],
        out_specs=[spec(kind, shape) for shape, _, kind in outs],
        out_shape=[jax.ShapeDtypeStruct(tuple(shape), dt) for shape, dt, _ in outs],
        scratch_shapes=list(scratch),
        compiler_params=_params(("arbitrary",) if has_acc else ("parallel",)),
    )(*[a for a, _ in ins])


ROW = ("row", None, 0)
FULL = ("full",)
ACC = ("acc",)


def _colsum(x):
    return jnp.sum(x, axis=0, keepdims=True)


def _acc_add(i, ref, val, rows=None):
    idx = (slice(None),) * len(ref.shape) if rows is None else rows

    @pl.when(i == 0)
    def _():
        ref[idx] = val

    @pl.when(i > 0)
    def _():
        ref[idx] = ref[idx] + val


def _sigmoid(x):
    return 1.0 / (1.0 + jnp.exp(-x))


def _gelu(x):
    return 0.5 * x * (1.0 + lax.erf(x * (2.0 ** -0.5)))


def _gelu_grad(x):
    return 0.5 * (1.0 + lax.erf(x * (2.0 ** -0.5))) + x * jnp.exp(-0.5 * x * x) * ((2.0 * jnp.pi) ** -0.5)


SUBLANES = 8


def _phases(ext, sign):
    n = ext.shape[0]
    return [ext if b == 0 else pltpu.roll(ext, b if sign > 0 else n - b, axis=0) for b in range(SUBLANES)]


def _shift_prev(phases, s, hb):
    a, b = divmod(s, SUBLANES)
    return phases[b][hb - SUBLANES * a:phases[b].shape[0] - SUBLANES * a]


def _shift_next(phases, s, tm):
    a, b = divmod(s, SUBLANES)
    return phases[b][SUBLANES * a:SUBLANES * a + tm]


def _rms_mod_val(x, g, sc, sh):
    r = lax.rsqrt(jnp.mean(x * x, axis=-1, keepdims=True) + EPS)
    return x * r * g * (1.0 + sc) + sh


def _mod_first(name, x, g, sc, sh, tm=256):
    S, D = x.shape

    def body(x_ref, g_ref, sc_ref, sh_ref, h_ref):
        h_ref[...] = _rms_mod_val(x_ref[...], g_ref[...], sc_ref[...], sh_ref[...]).astype(BF16)

    return _rowcall(name, body, S, tm, [(x, ROW), (g, FULL), (sc, FULL), (sh, FULL)], [((S, D), BF16, ROW)])[0]


def _resid_mod(name, x, y, gate, g, sc, sh, tm=256):
    S, D = x.shape

    def body(x_ref, y_ref, gate_ref, g_ref, sc_ref, sh_ref, xo_ref, h_ref):
        xn = x_ref[...] + gate_ref[...] * y_ref[...]
        xo_ref[...] = xn
        h_ref[...] = _rms_mod_val(xn, g_ref[...], sc_ref[...], sh_ref[...]).astype(BF16)

    return _rowcall(name, body, S, tm,
                    [(x, ROW), (y, ROW), (gate, FULL), (g, FULL), (sc, FULL), (sh, FULL)],
                    [((S, D), F32, ROW), ((S, D), BF16, ROW)])


def _gate_bwd_val(i, d, y_ref, gate_ref, dy_ref, dg_ref, db_ref):
    dy = d * gate_ref[...]
    dy_ref[...] = dy.astype(BF16)
    _acc_add(i, dg_ref, _colsum(d * y_ref[...]))
    _acc_add(i, db_ref, _colsum(dy))


GATE_OUTS = lambda S, D: [((S, D), BF16, ROW), ((1, D), F32, ACC), ((1, D), F32, ACC)]


def _loss_head(name, x, y, gate, tgt, tm=256):
    S, D = x.shape

    def body(x_ref, y_ref, gate_ref, t_ref, dx_ref, l_ref, dy_ref, dg_ref, db_ref):
        i = pl.program_id(0)
        err = x_ref[...] + gate_ref[...] * y_ref[...] - t_ref[...]
        d = err * (1.0 / D)
        dx_ref[...] = d
        _acc_add(i, l_ref, _colsum(err * err))
        _gate_bwd_val(i, d, y_ref, gate_ref, dy_ref, dg_ref, db_ref)

    return _rowcall(name, body, S, tm, [(x, ROW), (y, ROW), (gate, FULL), (tgt, ROW)],
                    [((S, D), F32, ROW), ((1, D), F32, ACC)] + GATE_OUTS(S, D))


def _mod_bwd(name, dxo, dh, x, g, sc, y=None, gate=None, tm=256):
    S, D = x.shape
    gated = y is not None

    def body(d_ref, dh_ref, x_ref, g_ref, sc_ref, *rest):
        dx_ref, dsh_ref, dsc_ref, dg_ref = rest[2 * gated:2 * gated + 4]
        i = pl.program_id(0)
        xv, dh_v, gv = x_ref[...], dh_ref[...], g_ref[...]
        r = lax.rsqrt(jnp.mean(xv * xv, axis=-1, keepdims=True) + EPS)
        n = xv * r
        _acc_add(i, dsh_ref, _colsum(dh_v))
        _acc_add(i, dsc_ref, _colsum(dh_v * (n * gv)))
        dy = dh_v * (1.0 + sc_ref[...])
        _acc_add(i, dg_ref, _colsum(dy * n))
        dn = dy * gv
        dx = d_ref[...] + r * (dn - n * jnp.mean(dn * n, axis=-1, keepdims=True))
        dx_ref[...] = dx
        if gated:
            _gate_bwd_val(i, dx, rest[0], rest[1], *rest[6:9])

    ins = [(dxo, ROW), (dh, ROW), (x, ROW), (g, FULL), (sc, FULL)] + ([(y, ROW), (gate, FULL)] if gated else [])
    outs = [((S, D), F32, ROW), ((1, D), F32, ACC), ((1, D), F32, ACC), ((1, D), F32, ACC)]
    return _rowcall(name, body, S, tm, ins, outs + (GATE_OUTS(S, D) if gated else []))


HB16 = 16


def _conv3_val(ph, w, b, hb):
    return w[2:3] * _shift_prev(ph, 0, hb) + w[1:2] * _shift_prev(ph, 1, hb) + w[0:1] * _shift_prev(ph, 2, hb) + b


def _halo_first(halo_ref, tile_ref, live):
    return _phases(jnp.concatenate([halo_ref[...].astype(F32) * live, tile_ref[...].astype(F32)], axis=0), 1)


def _glu3_fwd(name, u, w, b, tm=128):
    S, F2 = u.shape
    Fh = F2 // 2

    def body(ua_ref, ub_ref, ha_ref, hb_ref, w_ref, b_ref, o_ref):
        live = (pl.program_id(0) > 0).astype(F32)
        wv, bv = w_ref[...], b_ref[...]
        za = _conv3_val(_halo_first(ha_ref, ua_ref, live), wv[:, :Fh], bv[:, :Fh], HB16)
        zb = _conv3_val(_halo_first(hb_ref, ub_ref, live), wv[:, Fh:], bv[:, Fh:], HB16)
        o_ref[...] = (za * _sigmoid(za) * zb).astype(BF16)

    return _rowcall(name, body, S, tm,
                    [(u, ("row", Fh, 0)), (u, ("row", Fh, 1)), (u, ("prev", HB16, Fh, 0)), (u, ("prev", HB16, Fh, 1)),
                     (w, FULL), (b, FULL)],
                    [((S, Fh), BF16, ROW)])[0]


def _glu3_bwd(name, u, dhm, w, b, tm=128):
    S, F2 = u.shape
    Fh = F2 // 2

    def body(ua_ref, ub_ref, ha_ref, hb_ref, d_ref, w_ref, b_ref, dz_ref, acc_ref):
        i = pl.program_id(0)
        live = (i > 0).astype(F32)
        wv, bv, d = w_ref[...], b_ref[...], d_ref[...]
        ea, eb = _halo_first(ha_ref, ua_ref, live), _halo_first(hb_ref, ub_ref, live)
        za = _conv3_val(ea, wv[:, :Fh], bv[:, :Fh], HB16)
        zb = _conv3_val(eb, wv[:, Fh:], bv[:, Fh:], HB16)
        sg = _sigmoid(za)
        da = d * zb * (sg * (1.0 + za * (1.0 - sg)))
        db = d * (za * sg)
        dz_ref[:, :Fh] = da.astype(BF16)
        dz_ref[:, Fh:] = db.astype(BF16)
        for k in range(3):
            row = jnp.concatenate([_colsum(da * _shift_prev(ea, 2 - k, HB16)),
                                   _colsum(db * _shift_prev(eb, 2 - k, HB16))], axis=1)
            _acc_add(i, acc_ref, row, rows=(slice(k, k + 1), slice(None)))
        _acc_add(i, acc_ref, jnp.concatenate([_colsum(da), _colsum(db)], axis=1), rows=(slice(3, 4), slice(None)))

        @pl.when(i == 0)
        def _():
            acc_ref[4:8, :] = jnp.zeros((4, F2), F32)

    return _rowcall(name, body, S, tm,
                    [(u, ("row", Fh, 0)), (u, ("row", Fh, 1)), (u, ("prev", HB16, Fh, 0)), (u, ("prev", HB16, Fh, 1)),
                     (dhm, ROW), (w, FULL), (b, FULL)],
                    [((S, F2), BF16, ROW), ((8, F2), F32, ACC)])


def _conv3_bwd(name, dz, w, tm=128):
    S, F2 = dz.shape
    nblk = S // tm

    def body(d_ref, n_ref, w_ref, o_ref):
        live = (pl.program_id(0) < nblk - 1).astype(F32)
        ph = _phases(jnp.concatenate([d_ref[...].astype(F32), n_ref[...].astype(F32) * live], axis=0), -1)
        wv = w_ref[...]
        o_ref[...] = (wv[2:3] * _shift_next(ph, 0, tm) + wv[1:2] * _shift_next(ph, 1, tm)
                      + wv[0:1] * _shift_next(ph, 2, tm)).astype(BF16)

    return _rowcall(name, body, S, tm, [(dz, ROW), (dz, ("next", HB16, None, 0)), (w, FULL)],
                    [((S, F2), BF16, ROW)])[0]


def _ffn_fwd(name, h, w_up, w_dn, dw_w, dw_b):
    u = _matmul(f"{name}_up", h, w_up, "nn", BF16, tm=2048)
    hm = _glu3_fwd(f"{name}_glu", u, dw_w, dw_b)
    f = _matmul(f"{name}_dn", hm, w_dn, "nn", F32, tm=1024, tn=1024, tk=w_dn.shape[0])
    return f, (u, hm)


def _ffn_bwd(name, dy, h, u, hm, w_up, w_dn, dw_w, dw_b):
    Fh = w_dn.shape[0]
    dhm = _matmul(f"{name}_ddn_x", dy, w_dn, "nt", F32, tm=1024, tn=Fh // 2)
    g_dn = _matmul(f"{name}_ddn_w", hm, dy, "tn", BF16, tm=Fh // 2, tn=1024, tk=1024)
    dz, acc = _glu3_bwd(f"{name}_dglu", u, dhm, dw_w, dw_b)
    du = _conv3_bwd(f"{name}_dconv", dz, dw_w)
    g_up = _matmul(f"{name}_dup_w", h, du, "tn", BF16, tm=1024, tn=Fh // 2, tk=1024)
    dh = _matmul(f"{name}_dup_x", du, w_up, "nt", F32, tm=1024, tn=1024, tk=Fh // 2)
    return dh, dict(up=g_up, dn=g_dn, dw_w=acc[0:3], dw_b=acc[3:4])


HB32 = 32


def _glu31_fwd(name, p, w, b, tm=256):
    S, D2 = p.shape
    D = D2 // 2
    K = w.shape[0]

    def body(a_ref, g_ref, ha_ref, hg_ref, w_ref, b_ref, o_ref):
        live = (pl.program_id(0) > 0).astype(F32)
        y1 = a_ref[...] * _sigmoid(g_ref[...])
        ph = _phases(jnp.concatenate([ha_ref[...] * _sigmoid(hg_ref[...]) * live, y1], axis=0), 1)
        wv = w_ref[...]
        acc = b_ref[...] + wv[K - 1:K] * y1
        for k in range(K - 1):
            acc = acc + wv[k:k + 1] * _shift_prev(ph, K - 1 - k, HB32)
        o_ref[...] = acc

    return _rowcall(name, body, S, tm,
                    [(p, ("row", D, 0)), (p, ("row", D, 1)), (p, ("prev", HB32, D, 0)), (p, ("prev", HB32, D, 1)),
                     (w, FULL), (b, FULL)],
                    [((S, D), F32, ROW)])[0]


def _ln_silu_fwd(name, y2, g, b, tm=256):
    S, D = y2.shape

    def body(y_ref, g_ref, b_ref, o_ref):
        y = y_ref[...]
        mu = jnp.mean(y, axis=-1, keepdims=True)
        yc = y - mu
        rs = lax.rsqrt(jnp.mean(yc * yc, axis=-1, keepdims=True) + EPS)
        y3 = yc * rs * g_ref[...] + b_ref[...]
        o_ref[...] = (y3 * _sigmoid(y3)).astype(BF16)

    return _rowcall(name, body, S, tm, [(y2, ROW), (g, FULL), (b, FULL)], [((S, D), BF16, ROW)])[0]


def _ln_silu_bwd(name, y2, dy4, g, b, tm=256):
    S, D = y2.shape

    def body(y_ref, d_ref, g_ref, b_ref, o_ref, dg_ref, db_ref):
        i = pl.program_id(0)
        y, gv = y_ref[...], g_ref[...]
        mu = jnp.mean(y, axis=-1, keepdims=True)
        yc = y - mu
        rs = lax.rsqrt(jnp.mean(yc * yc, axis=-1, keepdims=True) + EPS)
        n = yc * rs
        y3 = n * gv + b_ref[...]
        sg = _sigmoid(y3)
        dy3 = d_ref[...] * (sg * (1.0 + y3 * (1.0 - sg)))
        _acc_add(i, db_ref, _colsum(dy3))
        _acc_add(i, dg_ref, _colsum(dy3 * n))
        dn = dy3 * gv
        o_ref[...] = rs * (dn - jnp.mean(dn, axis=-1, keepdims=True) - n * jnp.mean(dn * n, axis=-1, keepdims=True))

    return _rowcall(name, body, S, tm, [(y2, ROW), (dy4, ROW), (g, FULL), (b, FULL)],
                    [((S, D), F32, ROW), ((1, D), F32, ACC), ((1, D), F32, ACC)])


def _glu31_bwd(name, p, dy2, w, tm=256):
    S, D2 = p.shape
    D = D2 // 2
    K = w.shape[0]
    nblk = S // tm

    def body(a_ref, g_ref, ha_ref, hg_ref, d_ref, dn_ref, w_ref, dp_ref, dw_ref, dcb_ref, dpb_ref):
        i = pl.program_id(0)
        live_prev = (i > 0).astype(F32)
        live_next = (i < nblk - 1).astype(F32)
        a, sg, d, wv = a_ref[...], _sigmoid(g_ref[...]), d_ref[...], w_ref[...]
        ext_y = _phases(jnp.concatenate([ha_ref[...] * _sigmoid(hg_ref[...]) * live_prev, a * sg], axis=0), 1)
        ext_d = _phases(jnp.concatenate([d, dn_ref[...] * live_next], axis=0), -1)
        dy1 = wv[K - 1:K] * d
        for k in range(K - 1):
            dy1 = dy1 + wv[k:k + 1] * _shift_next(ext_d, K - 1 - k, tm)
        for k in range(K):
            _acc_add(i, dw_ref, _colsum(d * _shift_prev(ext_y, K - 1 - k, HB32)), rows=(slice(k, k + 1), slice(None)))

        @pl.when(i == 0)
        def _():
            dw_ref[K:, :] = jnp.zeros((dw_ref.shape[0] - K, D), F32)

        _acc_add(i, dcb_ref, _colsum(d))
        da = dy1 * sg
        dg = dy1 * a * sg * (1.0 - sg)
        dp_ref[:, :D] = da.astype(BF16)
        dp_ref[:, D:] = dg.astype(BF16)
        _acc_add(i, dpb_ref, jnp.concatenate([_colsum(da), _colsum(dg)], axis=1))

    return _rowcall(name, body, S, tm,
                    [(p, ("row", D, 0)), (p, ("row", D, 1)), (p, ("prev", HB32, D, 0)), (p, ("prev", HB32, D, 1)),
                     (dy2, ROW), (dy2, ("next", HB32, None, 0)), (w, FULL)],
                    [((S, D2), BF16, ROW), ((HB32, D), F32, ACC), ((1, D), F32, ACC), ((1, D2), F32, ACC)])


CHUNK = 128
A_GROUPS = 4


def _group_ln(gv):
    ns, rss = [], []
    for g in range(A_GROUPS):
        xg = gv[:, g * LANES:(g + 1) * LANES]
        xc = xg - jnp.mean(xg, axis=-1, keepdims=True)
        rs = lax.rsqrt(jnp.mean(xc * xc, axis=-1, keepdims=True) + EPS)
        ns.append(xc * rs)
        rss.append(jnp.broadcast_to(rs, xg.shape))
    return jnp.concatenate(ns, axis=1), jnp.concatenate(rss, axis=1)


def _tril_mask():
    r = lax.broadcasted_iota(jnp.int32, (CHUNK, CHUNK), 0)
    c = lax.broadcasted_iota(jnp.int32, (CHUNK, CHUNK), 1)
    return r >= c


def _spatial(ws_ref, x, dn):
    mask = _tril_mask()
    rows = []
    for ci in range(x.shape[0] // CHUNK):
        cols = []
        for g in range(A_GROUPS):
            wm = jnp.where(mask, ws_ref[g], 0.0).astype(BF16)
            xb = x[ci * CHUNK:(ci + 1) * CHUNK, g * LANES:(g + 1) * LANES]
            cols.append(lax.dot_general(wm, xb, dn, preferred_element_type=F32))
        rows.append(jnp.concatenate(cols, axis=1))
    return jnp.concatenate(rows, axis=0)


def _mixa_fwd(name, z, vg, ws, bias_full, tm=256):
    S = z.shape[0]
    W = A_GROUPS * LANES

    def body(u_ref, v_ref, vg_ref, ws_ref, b_ref, o_ref):
        nh, _ = _group_ln(_gelu(v_ref[...]))
        vn = (nh * vg_ref[...]).astype(BF16)
        f = _spatial(ws_ref, vn, _DN["nn"]) + jnp.concatenate([b_ref[...]] * (tm // CHUNK), axis=0)
        o_ref[...] = (_gelu(u_ref[...]) * f).astype(BF16)

    return _rowcall(name, body, S, tm,
                    [(z, ("row", W, 0)), (z, ("row", W, 1)), (vg, FULL), (ws, FULL), (bias_full, FULL)],
                    [((S, W), BF16, ROW)])[0]


def _mixa_bwd(name, z, dyab, vg, ws, bias_full, tm=256):
    S = z.shape[0]
    W = A_GROUPS * LANES
    nch = tm // CHUNK

    def body(u_ref, v_ref, d_ref, vg_ref, ws_ref, b_ref, dz_ref, dws_ref, dbf_ref, dvg_ref):
        i = pl.program_id(0)
        u, v, d, vgv = u_ref[...], v_ref[...], d_ref[...], vg_ref[...]
        nh, rs = _group_ln(_gelu(v))
        vn = (nh * vgv).astype(BF16)
        f = _spatial(ws_ref, vn, _DN["nn"]) + jnp.concatenate([b_ref[...]] * nch, axis=0)
        dz_ref[:, :W] = (d * f * _gelu_grad(u)).astype(BF16)
        df = d * _gelu(u)
        dbf = df[0:CHUNK]
        for ci in range(1, nch):
            dbf = dbf + df[ci * CHUNK:(ci + 1) * CHUNK]
        _acc_add(i, dbf_ref, dbf)
        dfb = df.astype(BF16)
        mask = _tril_mask()
        for g in range(A_GROUPS):
            acc = jnp.zeros((CHUNK, CHUNK), F32)
            for ci in range(nch):
                blk = (slice(ci * CHUNK, (ci + 1) * CHUNK), slice(g * LANES, (g + 1) * LANES))
                acc = acc + lax.dot_general(dfb[blk], vn[blk], _DN["nt"], preferred_element_type=F32)
            _acc_add(i, dws_ref, jnp.where(mask, acc, 0.0)[None], rows=(slice(g, g + 1), slice(None), slice(None)))
        dvn = _spatial(ws_ref, dfb, _DN["tn"])
        _acc_add(i, dvg_ref, _colsum(dvn * nh))
        dnh = dvn * vgv
        parts = []
        for g in range(A_GROUPS):
            cs = slice(g * LANES, (g + 1) * LANES)
            dg_, ng = dnh[:, cs], nh[:, cs]
            parts.append(dg_ - jnp.mean(dg_, axis=-1, keepdims=True) - ng * jnp.mean(dg_ * ng, axis=-1, keepdims=True))
        dz_ref[:, W:] = (rs * jnp.concatenate(parts, axis=1) * _gelu_grad(v)).astype(BF16)

    return _rowcall(name, body, S, tm,
                    [(z, ("row", W, 0)), (z, ("row", W, 1)), (dyab, ("row", W, 0)), (vg, FULL), (ws, FULL),
                     (bias_full, FULL)],
                    [((S, 2 * W), BF16, ROW), ((A_GROUPS, CHUNK, CHUNK), F32, ACC), ((CHUNK, W), F32, ACC),
                     ((1, W), F32, ACC)])


HEAD = 64
N_HEADS = 8
BW = HEAD * N_HEADS
QB = 128
DILATIONS = (1, 4, 16)
QK_SCALE = HEAD ** -0.5


def _gsum64(x, ones_bd):
    x1 = x.astype(BF16)
    r1 = x - x1.astype(F32)
    x2 = r1.astype(BF16)
    x3 = (r1 - x2.astype(F32)).astype(BF16)
    dot = lambda t: jnp.dot(t, ones_bd, preferred_element_type=F32)
    return dot(x1) + dot(x2) + dot(x3)


def _swap32(x):
    n = x.shape[-1]
    up = pltpu.roll(x, n - HEAD // 2, axis=1)
    dn = pltpu.roll(x, HEAD // 2, axis=1)
    lane = lax.broadcasted_iota(jnp.int32, x.shape, 1)
    return jnp.where((lane % HEAD) < HEAD // 2, up, dn)


def _tile4(t):
    return jnp.concatenate([t] * (BW // LANES), axis=1)


def _stage_spec(tm):
    return pltpu.VMEM((BW // LANES, tm, LANES), F32)


def _to_classes(stage, x, dil):
    tm = x.shape[0]
    for j in range(BW // LANES):
        stage[j] = x[:, j * LANES:(j + 1) * LANES]
    return [jnp.concatenate([stage.at[j][pl.ds(r, tm // dil, stride=dil), :] for j in range(BW // LANES)], axis=1)
            for r in range(dil)]


def _from_classes(stage, cls, dil):
    rows = cls.shape[1]
    for r in range(dil):
        for j in range(BW // LANES):
            stage.at[j][pl.ds(r, rows, stride=dil), :] = cls[r, :, j * LANES:(j + 1) * LANES]
    return jnp.concatenate([stage[j] for j in range(BW // LANES)], axis=1)


def _cls_view(t, dil):
    return t if dil == 1 else t.reshape(dil, t.shape[0] // dil, t.shape[1])


def _cls_kind(dil, off=0):
    return ("off", off, None, 0) if dil == 1 else ("cls", dil, off)


def _cls_out(S, dil, dtype):
    return ((S, BW) if dil == 1 else (dil, S // dil, BW), dtype, _cls_kind(dil))


def _flat(t):
    return t.reshape(-1, t.shape[-1])


def _qkv_fwd(name, z, cos, sin, ones_bd, qg, kg, tm=256):
    S = z.shape[0]
    nd = len(DILATIONS)

    def body(q_ref, k_ref, v_ref, c_ref, s_ref, o_ref, qg_ref, kg_ref, *rest):
        outs, stage = rest[:3 * nd], rest[3 * nd]
        c, s, ob = _tile4(c_ref[...]), _tile4(s_ref[...]), o_ref[...]

        def norm_rope(x, g):
            r = lax.rsqrt(_gsum64(x * x, ob) * (1.0 / HEAD) + EPS)
            xn = x * r * g
            return xn * c + _swap32(xn) * s

        vals = [norm_rope(q_ref[...], qg_ref[...]) * QK_SCALE, norm_rope(k_ref[...], kg_ref[...]), v_ref[...]]
        for a, val in enumerate(vals):
            for b, dil in enumerate(DILATIONS):
                if dil == 1:
                    outs[nd * a + b][...] = val.astype(BF16)
                else:
                    for r, rows in enumerate(_to_classes(stage, val, dil)):
                        outs[nd * a + b][r] = rows.astype(BF16)

    outs = _rowcall(name, body, S, tm,
                    [(z, ("row", BW, 2)), (z, ("row", BW, 3)), (z, ("row", BW, 4)), (cos, ROW), (sin, ROW),
                     (ones_bd, FULL), (qg, FULL), (kg, FULL)],
                    [_cls_out(S, dil, BF16) for _ in range(3) for dil in DILATIONS], scratch=[_stage_spec(tm)])
    return [[_flat(outs[nd * a + b]) for a in range(3)] for b in range(nd)]


PAIR = 2 * HEAD


def _pair_scores(q_ref, kp_ref, kc_ref, hp, half, seg_blocks):
    ps = slice(hp * PAIR, (hp + 1) * PAIR)
    mine = (lax.broadcasted_iota(jnp.int32, (1, PAIR), 1) >= HEAD) == (half == 1)
    qm = jnp.where(mine, q_ref[:, ps], jnp.zeros((), BF16))
    kcat = jnp.concatenate([kp_ref[:, ps], kc_ref[:, ps]], axis=0)
    s = lax.dot_general(qm, kcat, _DN["nt"], preferred_element_type=F32)
    qi = lax.broadcasted_iota(jnp.int32, (QB, 2 * QB), 0)
    kj = lax.broadcasted_iota(jnp.int32, (QB, 2 * QB), 1)
    has_prev = (pl.program_id(0) % seg_blocks) != 0
    valid = (kj >= qi) & (kj <= qi + QB) & ((kj >= QB) | has_prev)
    return mine, qm, kcat, s, valid


def _attn_fwd(name, q, k, v, dil):
    S = q.shape[0]
    seg_blocks = S // dil // QB
    PREV = ("off", -1, None, 0)

    def body(q_ref, kp_ref, kc_ref, vp_ref, vc_ref, o_ref, l_ref):
        for hp in range(N_HEADS // 2):
            ps = slice(hp * PAIR, (hp + 1) * PAIR)
            vcat = jnp.concatenate([vp_ref[:, ps], vc_ref[:, ps]], axis=0)
            o_pair = l_pair = None
            for half in range(2):
                mine, _, _, s, valid = _pair_scores(q_ref, kp_ref, kc_ref, hp, half, seg_blocks)
                s = jnp.where(valid, s, NEG)
                m = jnp.max(s, axis=-1, keepdims=True)
                p = jnp.exp(s - m)
                den = jnp.sum(p, axis=-1, keepdims=True)
                o = jnp.dot(p.astype(BF16), vcat, preferred_element_type=F32) / den
                lse = jnp.broadcast_to(m + jnp.log(den), (QB, PAIR))
                o_pair, l_pair = (o, lse) if half == 0 else (jnp.where(mine, o, o_pair), jnp.where(mine, lse, l_pair))
            o_ref[:, ps] = o_pair
            l_ref[:, ps] = l_pair

    return _rowcall(name, body, S, QB, [(q, ROW), (k, PREV), (k, ROW), (v, PREV), (v, ROW)],
                    [((S, BW), F32, ROW)] * 2)


def _attn_bwd(name, q, k, v, do, lse, delta, dil):
    S = q.shape[0]
    seg_blocks = S // dil // QB
    PREV = ("off", -1, None, 0)

    def body(q_ref, kp_ref, kc_ref, vp_ref, vc_ref, do_ref, l_ref, dl_ref, dq_ref, dkc_ref, dkp_ref, dvc_ref, dvp_ref):
        for hp in range(N_HEADS // 2):
            ps = slice(hp * PAIR, (hp + 1) * PAIR)
            vcat = jnp.concatenate([vp_ref[:, ps], vc_ref[:, ps]], axis=0)
            do_p = do_ref[:, ps].astype(BF16)
            dq = dk = dv = None
            for half in range(2):
                mine, qm, kcat, s, valid = _pair_scores(q_ref, kp_ref, kc_ref, hp, half, seg_blocks)
                col = hp * PAIR + half * HEAD
                p = jnp.where(valid, jnp.exp(s - l_ref[:, col:col + 1]), 0.0)
                dom = jnp.where(mine, do_p, jnp.zeros((), BF16))
                dp = lax.dot_general(dom, vcat, _DN["nt"], preferred_element_type=F32)
                ds = (p * (dp - dl_ref[:, col:col + 1])).astype(BF16)
                dq_h = jnp.dot(ds, kcat, preferred_element_type=F32)
                dk_h = lax.dot_general(ds, qm, _DN["tn"], preferred_element_type=F32)
                dv_h = lax.dot_general(p.astype(BF16), dom, _DN["tn"], preferred_element_type=F32)
                dq, dk, dv = (dq_h, dk_h, dv_h) if half == 0 else (jnp.where(mine, dq_h, dq), dk + dk_h, dv + dv_h)
            dq_ref[:, ps] = dq
            dkp_ref[:, ps] = dk[:QB]
            dkc_ref[:, ps] = dk[QB:]
            dvp_ref[:, ps] = dv[:QB]
            dvc_ref[:, ps] = dv[QB:]

    return _rowcall(name, body, S, QB,
                    [(q, ROW), (k, PREV), (k, ROW), (v, PREV), (v, ROW), do, (lse, ROW), (delta, ROW)],
                    [((S, BW), F32, ROW)] * 5)


def _merge_fwd(name, branches, tm=256):
    S = branches[0][0].shape[0]
    nd = len(DILATIONS)

    def body(*refs):
        ins, (y_ref, yb_ref), l_refs, stage = refs[:2 * nd], refs[2 * nd:2 * nd + 2], refs[2 * nd + 2:3 * nd + 2], refs[-1]
        os_, ls = [], []
        for b, dil in enumerate(DILATIONS):
            o, l = ins[2 * b][...], ins[2 * b + 1][...]
            os_.append(o if dil == 1 else _from_classes(stage, o, dil))
            ls.append(l if dil == 1 else _from_classes(stage, l, dil))
        m = functools.reduce(jnp.maximum, ls)
        es = [jnp.exp(l - m) for l in ls]
        den = functools.reduce(lambda a, e: a + e, es)
        y = functools.reduce(lambda a, t: a + t, [e * o for e, o in zip(es, os_)]) / den
        y_ref[...] = y
        yb_ref[...] = y.astype(BF16)
        lse = m + jnp.log(den)
        for b, dil in enumerate(DILATIONS):
            if dil == 1:
                l_refs[b][...] = lse
            else:
                for r, rows in enumerate(_to_classes(stage, lse, dil)):
                    l_refs[b][r] = rows

    ins = [(_cls_view(t, dil), _cls_kind(dil)) for pair, dil in zip(branches, DILATIONS) for t in pair]
    outs = _rowcall(name, body, S, tm, ins,
                    [((S, BW), F32, ROW), ((S, BW), BF16, ROW)] + [_cls_out(S, dil, F32) for dil in DILATIONS],
                    scratch=[_stage_spec(tm)])
    return outs[0], outs[1], [_flat(t) for t in outs[2:]]


def _delta(name, dyab, yb, ones_bd, tm=256):
    S = yb.shape[0]
    nd = len(DILATIONS)

    def body(d_ref, y_ref, o_ref, *rest):
        dl_refs, do_refs, stage = rest[:nd], rest[nd:2 * nd - 1], rest[-1]
        d = d_ref[...]
        dl = _gsum64(d * y_ref[...], o_ref[...])
        for b, dil in enumerate(DILATIONS):
            if dil == 1:
                dl_refs[b][...] = dl
            else:
                for r, rows in enumerate(_to_classes(stage, dl, dil)):
                    dl_refs[b][r] = rows
                for r, rows in enumerate(_to_classes(stage, d, dil)):
                    do_refs[b - 1][r] = rows.astype(BF16)

    outs = _rowcall(name, body, S, tm, [(dyab, ("row", BW, 1)), (yb, ROW), (ones_bd, FULL)],
                    [_cls_out(S, dil, F32) for dil in DILATIONS] + [_cls_out(S, dil, BF16) for dil in DILATIONS[1:]],
                    scratch=[_stage_spec(tm)])
    return [_flat(t) for t in outs[:nd]], [_flat(t) for t in outs[nd:]]


def _qkv_bwd(name, z, cos, sin, ones_bd, qg, kg, pieces):
    S = z.shape[0]
    nblk = S // QB

    def body(q_ref, k_ref, c_ref, s_ref, o_ref, qg_ref, kg_ref, *rest):
        pr, (dz_ref, dqg_ref, dkg_ref), stage = rest[:15], rest[15:18], rest[18]
        i = pl.program_id(0)
        c, s, ob = _tile4(c_ref[...]), _tile4(s_ref[...]), o_ref[...]
        dq = dk = dv = None
        for b, dil in enumerate(DILATIONS):
            a_q, a_kc, a_kp, a_vc, a_vp = [r[...] for r in pr[5 * b:5 * b + 5]]
            live = ((i + dil) < nblk).astype(F32)
            tq, tk, tv = a_q, a_kc + a_kp * live, a_vc + a_vp * live
            if dil > 1:
                tq, tk, tv = (_from_classes(stage, t, dil) for t in (tq, tk, tv))
            dq, dk, dv = (tq, tk, tv) if b == 0 else (dq + tq, dk + tk, dv + tv)

        def back(x, g, d_rot, acc_ref):
            r = lax.rsqrt(_gsum64(x * x, ob) * (1.0 / HEAD) + EPS)
            n = x * r
            dxn = d_rot * c + _swap32(d_rot * s)
            _acc_add(i, acc_ref, _colsum(dxn * n))
            dn = dxn * g
            return r * (dn - n * (_gsum64(dn * n, ob) * (1.0 / HEAD)))

        dz_ref[:, :BW] = back(q_ref[...], qg_ref[...], dq * QK_SCALE, dqg_ref).astype(BF16)
        dz_ref[:, BW:2 * BW] = back(k_ref[...], kg_ref[...], dk, dkg_ref).astype(BF16)
        dz_ref[:, 2 * BW:] = dv.astype(BF16)

    ins = [(z, ("row", BW, 2)), (z, ("row", BW, 3)), (cos, ROW), (sin, ROW), (ones_bd, FULL), (qg, FULL), (kg, FULL)]
    for piece, dil in zip(pieces, DILATIONS):
        a_q, a_kc, a_kp, a_vc, a_vp = (_cls_view(t, dil) for t in piece)
        own, prev = _cls_kind(dil), _cls_kind(dil, dil)
        ins += [(a_q, own), (a_kc, own), (a_kp, prev), (a_vc, own), (a_vp, prev)]
    return _rowcall(name, body, S, QB, ins,
                    [((S, 3 * BW), BF16, ROW), ((1, BW), F32, ACC), ((1, BW), F32, ACC)], scratch=[_stage_spec(QB)])


def _local_step(x0, tgt, pos, mod, wb, sp, pipe):
    S, D = x0.shape
    md = lambda l, j: mod[l, j:j + 1]
    sh_m, sc_m, g_m, sh_f, sc_f, g_f = ([md(l, j) for l in range(2)] for j in range(6))
    nm_g, nf_g = sp["norm_mix_g"], sp["norm_ffn_g"]

    inv_freq = 1.0 / (ROPE_THETA ** (jnp.arange(0, HEAD, 2, dtype=F32) / HEAD))
    ang = pos.astype(F32)[:, None] * inv_freq
    cs, sn = jnp.cos(ang), jnp.sin(ang)
    cos = jnp.concatenate([cs, cs, cs, cs], axis=1)
    sin = jnp.concatenate([-sn, sn, -sn, sn], axis=1)
    head_of = jnp.arange(BW) // HEAD
    ones_bd = (head_of[:, None] == head_of[None, :]).astype(BF16)
    qg = jnp.tile(sp["b_q_norm_g"].reshape(1, HEAD), (1, N_HEADS))
    kg = jnp.tile(sp["b_k_norm_g"].reshape(1, HEAD), (1, N_HEADS))
    vg = sp["a_vnorm_g"].reshape(1, A_GROUPS * LANES)
    ws = sp["a_spatial_w"][0]
    bias_full = jnp.repeat(sp["a_spatial_b"][0].T, LANES, axis=1)
    ffn_s = [(sp["ffn_dw_w"][l], sp["ffn_dw_b"][l:l + 1]) for l in range(2)]

    h0 = _mod_first("l0_mod", x0, nm_g[0:1], sc_m[0], sh_m[0])
    z = _matmul("l0_in", h0, wb.get("w_in", h0), "nn", F32, tm=2048)
    ya = _mixa_fwd("l0_mixa", z, vg, ws, bias_full)
    qkv = _qkv_fwd("l0_qkv", z, cos, sin, ones_bd, qg, kg)
    branches = [_attn_fwd(f"l0_att{dil}", *qkv[b], dil) for b, dil in enumerate(DILATIONS)]
    yb, yb16, lses = _merge_fwd("l0_merge", branches)
    yab = jnp.concatenate([ya, yb16], axis=1)
    y0 = _matmul("l0_out", yab, wb.get("w_out", yab), "nn", F32, tm=1024, tn=1024)
    x1, h1 = _resid_mod("l0_res1", x0, y0, g_m[0], nf_g[0:1], sc_f[0], sh_f[0])
    ffn_w = [(wb.get("up0", h1), wb.get("dn0", h1), *ffn_s[0])]
    f0, (u0, hm0) = _ffn_fwd("l0_ffn", h1, *ffn_w[0])
    x2, h2 = _resid_mod("l0_res2", x1, f0, g_f[0], nm_g[1:2], sc_m[1], sh_m[1])
    p = _matmul("l1_pw1", h2, wb.get("pw1", h2), "nn", F32, tm=2048, bias=sp["conv_pw1_b"])
    y2 = _glu31_fwd("l1_glu", p, sp["conv_dw_w"][0], sp["conv_dw_b"])
    y4 = _ln_silu_fwd("l1_ln", y2, sp["conv_ln_g"], sp["conv_ln_b"])
    y1 = _matmul("l1_pw2", y4, wb.get("pw2", y4), "nn", F32, tm=1024, tn=1024, bias=sp["conv_pw2_b"])
    x3, h3 = _resid_mod("l1_res1", x2, y1, g_m[1], nf_g[1:2], sc_f[1], sh_f[1])
    ffn_w.append((wb.get("up1", h3), wb.get("dn1", h3), *ffn_s[1]))
    f1, (u1, hm1) = _ffn_fwd("l1_ffn", h3, *ffn_w[1])
    dx4, lossv, dy, dgate_f1, _ = _loss_head("loss", x3, f1, g_f[1], tgt)

    dh, gf1 = _ffn_bwd("l1_ffn", dy, h3, u1, hm1, *ffn_w[1])
    tok = pipe.scatter("g1", dict(dn1=gf1["dn"], up1=gf1["up"]))
    dx3, dsh_f1, dsc_f1, dnf1, dy, dgate_m1, dpw2_b = _mod_bwd("l1_dmod2", dx4, dh, x3, nf_g[1:2], sc_f[1] + tok,
                                                             y1, g_m[1])
    dy4 = _matmul("l1_dpw2_x", dy, wb.get("pw2"), "nt", F32, tm=1024, tn=1024)
    g_pw2 = _matmul("l1_dpw2_w", y4, dy, "tn", BF16, tm=1024, tn=1024)
    dy2, dln_g, dln_b = _ln_silu_bwd("l1_dln", y2, dy4, sp["conv_ln_g"], sp["conv_ln_b"])
    dp, ddw_w, ddw_b, dpw1_b = _glu31_bwd("l1_dglu", p, dy2, sp["conv_dw_w"][0])
    g_pw1 = _matmul("l1_dpw1_w", h2, dp, "tn", BF16, tm=1024, tn=1024)
    tok = pipe.scatter("g2", dict(pw2=g_pw2, pw1=g_pw1))
    pipe.collect("g1", g_pw1)
    dh = _matmul("l1_dpw1_x", dp, wb.get("pw1"), "nt", F32, tm=1024, tn=1024, tk=2048)
    dx2, dsh_m1, dsc_m1, dnm1, dy, dgate_f0, _ = _mod_bwd("l1_dmod1", dx3, dh, x2, nm_g[1:2], sc_m[1] + tok,
                                                        f0, g_f[0])
    dh, gf0 = _ffn_bwd("l0_ffn", dy, h1, u0, hm0, *ffn_w[0])
    pipe.finish("g1", gf0["up"])
    pipe.collect("g2", gf0["up"])
    dx1, dsh_f0, dsc_f0, dnf0, dy, dgate_m0, _ = _mod_bwd("l0_dmod2", dx2, dh, x1, nf_g[0:1], sc_f[0], y0, g_m[0])
    dyab = _matmul("l0_dout_x", dy, wb.get("w_out"), "nt", F32, tm=1024, tn=1024)
    g_out = _matmul("l0_dout_w", yab, dy, "tn", BF16, tm=1024, tn=1024)
    tok = pipe.scatter("g3", dict(dn0=gf0["dn"], up0=gf0["up"], w_out=g_out))
    vg = vg + tok
    dza, dws, dbf, dvg = _mixa_bwd("l0_dmixa", z, dyab, vg, ws, bias_full)
    deltas, dos = _delta("l0_delta", dyab, yb, ones_bd)
    pieces = []
    for b, dil in enumerate(DILATIONS):
        do = (dyab, ("row", BW, 1)) if dil == 1 else (dos[b - 1], ROW)
        pieces.append(_attn_bwd(f"l0_datt{dil}", *qkv[b], do, lses[b], deltas[b], dil))
    dzb, dqg, dkg = _qkv_bwd("l0_dqkv", z, cos, sin, ones_bd, qg, kg, pieces)
    dz = jnp.concatenate([dza, dzb], axis=1)
    g_in = _matmul("l0_din_w", h0, dz, "tn", BF16, tm=1024, tn=1280)
    tok = pipe.scatter("g4", dict(w_in=g_in))
    pipe.finish("g2", g_in)
    pipe.collect("g3", g_in)
    dh = _matmul("l0_din_x", dz, wb.get("w_in"), "nt", F32, tm=1024, tn=1024, tk=2560)
    grad_x, dsh_m0, dsc_m0, dnm0 = _mod_bwd("l0_dmod1", dx1, dh, x0, nm_g[0:1], sc_m[0] + tok)

    dmod = jnp.stack([jnp.concatenate([dsh_m0, dsc_m0, dgate_m0, dsh_f0, dsc_f0, dgate_f0], axis=0),
                      jnp.concatenate([dsh_m1, dsc_m1, dgate_m1, dsh_f1, dsc_f1, dgate_f1], axis=0)])
    small = dict(
        norm_mix_g=jnp.concatenate([dnm0, dnm1], axis=0),
        norm_ffn_g=jnp.concatenate([dnf0, dnf1], axis=0),
        a_vnorm_g=dvg.reshape(1, A_GROUPS, LANES),
        a_spatial_w=dws[None],
        a_spatial_b=dbf.reshape(CHUNK, A_GROUPS, LANES).sum(-1).T[None],
        b_q_norm_g=dqg.reshape(N_HEADS, HEAD).sum(0)[None],
        b_k_norm_g=dkg.reshape(N_HEADS, HEAD).sum(0)[None],
        conv_pw1_b=dpw1_b, conv_dw_w=ddw_w[None, :sp["conv_dw_w"].shape[1]], conv_dw_b=ddw_b,
        conv_ln_g=dln_g, conv_ln_b=dln_b, conv_pw2_b=dpw2_b,
        ffn_dw_w=jnp.stack([gf0["dw_w"], gf1["dw_w"]]),
        ffn_dw_b=jnp.concatenate([gf0["dw_b"], gf1["dw_b"]], axis=0),
    )
    return lossv, grad_x, small, dmod


ADA_TN = 512


def _ada_fwd(name, c_all, ada_w, ada_b_sh):
    L, D, N = ada_w.shape
    B = c_all.shape[0]

    def body(c_ref, w_ref, b_ref, o_ref):
        cv = c_ref[...]
        ca = (cv * _sigmoid(cv)).astype(BF16)
        o_ref[0] = jnp.dot(ca, w_ref[0].astype(BF16), preferred_element_type=F32) + b_ref[0]

    return pl.pallas_call(
        body, name=name, grid=(L, N // ADA_TN),
        in_specs=[pl.BlockSpec((B, D), lambda l, j: (0, 0)), pl.BlockSpec((1, D, ADA_TN), lambda l, j: (l, 0, j)),
                  pl.BlockSpec((1, 1, ADA_TN), lambda l, j: (l, 0, j))],
        out_specs=pl.BlockSpec((1, B, ADA_TN), lambda l, j: (l, 0, j)),
        out_shape=jax.ShapeDtypeStruct((L, B, N), F32),
        compiler_params=_params(("parallel", "parallel")),
    )(c_all, ada_w, ada_b_sh.reshape(L, 1, N))


def _adamw_val(w, g, m, v):
    m2 = ADAM_B1 * m + (1.0 - ADAM_B1) * g
    v2 = ADAM_B2 * v + (1.0 - ADAM_B2) * (g * g)
    m_hat = m2 / (1.0 - ADAM_B1 ** ADAM_STEP)
    v_hat = v2 / (1.0 - ADAM_B2 ** ADAM_STEP)
    delta = -ADAM_LR * (m_hat / (jnp.sqrt(v_hat) + ADAM_EPS) + ADAM_WD * w)
    return delta, m2, v2


def _ada_update(name, c_all, dmod_sh, w, m, v):
    L, D, N = w.shape
    B = c_all.shape[0]

    def body(c_ref, d_ref, w_ref, m_ref, v_ref, g_ref, dl_ref, mo_ref, vo_ref):
        cv = c_ref[...]
        ca = (cv * _sigmoid(cv)).astype(BF16)
        g = lax.dot_general(ca, d_ref[0].astype(BF16), _DN["tn"], preferred_element_type=F32)
        g_ref[0] = g
        dl_ref[0], mo_ref[0], vo_ref[0] = _adamw_val(w_ref[0], g, m_ref[0], v_ref[0])

    wspec = pl.BlockSpec((1, D, ADA_TN), lambda l, j: (l, 0, j))
    return pl.pallas_call(
        body, name=name, grid=(L, N // ADA_TN),
        in_specs=[pl.BlockSpec((B, D), lambda l, j: (0, 0)), pl.BlockSpec((1, B, ADA_TN), lambda l, j: (l, 0, j)),
                  wspec, wspec, wspec],
        out_specs=[wspec] * 4, out_shape=[jax.ShapeDtypeStruct((L, D, N), F32)] * 4,
        compiler_params=_params(("parallel", "parallel")),
    )(c_all, dmod_sh, w, m, v)


def _adamw(name, w, g, m, v):
    R, C = w.shape
    tm = R
    for cand in (256, 128, 64, 32, 16, 8):
        if R % cand == 0 and cand * C * 4 <= (1 << 20):
            tm = cand
            break

    def body(w_ref, g_ref, m_ref, v_ref, d_ref, mo_ref, vo_ref):
        d_ref[...], mo_ref[...], vo_ref[...] = _adamw_val(w_ref[...], g_ref[...], m_ref[...], v_ref[...])

    return _rowcall(name, body, R, tm, [(w, ROW), (g, ROW), (m, ROW), (v, ROW)], [((R, C), F32, ROW)] * 3)


def _row_tile(rows, width, itemsize=4, limit=1 << 20):
    for cand in (512, 256, 128, 64, 32, 16):
        if rows % cand == 0 and cand * width * itemsize <= limit:
            return cand
    raise ValueError((rows, width))


def _cast_into_full(name, a, layer, q, kind, after):
    L, r, c = a.shape
    tm = _row_tile(r, c)
    if kind == "col":
        full, o_spec = (r, N_CHIPS * c), pl.BlockSpec((tm, c), lambda i, q_ref: (i, q_ref[0]))
    else:
        full, o_spec = (N_CHIPS * r, c), pl.BlockSpec((tm, c), lambda i, q_ref: (q_ref[0] * (r // tm) + i, 0))

    def body(q_ref, a_ref, after_ref, o_ref):
        o_ref[...] = a_ref[0].astype(BF16)

    return pl.pallas_call(
        body, name=name,
        grid_spec=pltpu.PrefetchScalarGridSpec(
            num_scalar_prefetch=1, grid=(r // tm,),
            in_specs=[pl.BlockSpec((1, tm, c), lambda i, q_ref: (layer, i, 0)), ANY], out_specs=o_spec),
        out_shape=jax.ShapeDtypeStruct(full, BF16), compiler_params=_params(("parallel",)),
    )(q.reshape(1).astype(jnp.int32), a, after)


def _sum4(name, g, rcv, q, kind, n):
    r, c = rcv.shape[1:]
    tm = _row_tile(r, c)
    if kind == "col":
        g_spec = pl.BlockSpec((tm, n), lambda i, q_ref: (i, q_ref[0]))
    else:
        g_spec = pl.BlockSpec((tm, c), lambda i, q_ref: (q_ref[0] * (n // tm) + i, 0))

    def body(q_ref, g_ref, r_ref, o_ref):
        acc = g_ref[...].astype(F32)
        for j in range(3):
            acc = acc + r_ref[j].astype(F32)
        o_ref[...] = acc

    return pl.pallas_call(
        body, name=name,
        grid_spec=pltpu.PrefetchScalarGridSpec(
            num_scalar_prefetch=1, grid=(r // tm,),
            in_specs=[g_spec, pl.BlockSpec((3, tm, c), lambda i, q_ref: (0, i, 0))],
            out_specs=pl.BlockSpec((tm, c), lambda i, q_ref: (i, 0))),
        out_shape=jax.ShapeDtypeStruct((r, c), F32), compiler_params=_params(("parallel",)),
    )(q.reshape(1).astype(jnp.int32), g, rcv)


def _adamw_sum(name, w, m, v, layer, mine, theirs, prev):
    L, r, c = w.shape
    tm = _row_tile(r, c, limit=1 << 19)
    lay = pl.BlockSpec((1, tm, c), lambda i: (layer, i, 0))
    flat = pl.BlockSpec((tm, c), lambda i: (i, 0))
    n_prev = 0 if prev is None else 4

    def body(w_ref, m_ref, v_ref, a_ref, b_ref, *rest):
        g_ref, d_ref, mo_ref, vo_ref = rest[n_prev:]
        g = a_ref[...] + b_ref[...]
        g_ref[0] = g
        d_ref[0], mo_ref[0], vo_ref[0] = _adamw_val(w_ref[0], g, m_ref[0], v_ref[0])

    return pl.pallas_call(
        body, name=name, grid=(r // tm,),
        in_specs=[lay, lay, lay, flat, flat] + [ANY] * n_prev, out_specs=[lay] * 4,
        out_shape=[jax.ShapeDtypeStruct((L, r, c), F32)] * 4,
        input_output_aliases={5 + k: k for k in range(n_prev)},
        compiler_params=_params(("parallel",)),
    )(w, m, v, mine, theirs, *(prev or ()))


def _sum8(name, gathered):
    R, N = gathered.shape
    P = R // 8

    def body(g_ref, o_ref):
        acc = g_ref[0:P, :]
        for d in range(1, 8):
            acc = acc + g_ref[d * P:(d + 1) * P, :]
        o_ref[...] = acc

    return pl.pallas_call(body, name=name, out_shape=jax.ShapeDtypeStruct((P, N), F32),
                          compiler_params=pltpu.CompilerParams(vmem_limit_bytes=VMEM_LIMIT))(gathered)


ANY = pl.BlockSpec(memory_space=pl.ANY)


def _mesh_pos():
    x, y, c = lax.axis_index("x"), lax.axis_index("y"), lax.axis_index("c")
    other_chips = [(1 - x, y), (x, 1 - y), (1 - x, 1 - y)]
    return x, y, c, other_chips


def _allgather8(name, blk):
    m_per, n = blk.shape

    def body(x_ref, out_ref, send_sems, recv_sems, local_sem):
        x, y, c, chips = _mesh_pos()
        me, sibling = (x, y, c), (x, y, 1 - c)

        def rows(px, py, pc):
            return out_ref.at[pl.ds((4 * px + 2 * py + pc) * m_per, m_per), :]

        def copy(k, block, to, src=None):
            return pltpu.make_async_remote_copy(
                src_ref=rows(*block) if src is None else src, dst_ref=rows(*block),
                send_sem=send_sems.at[k], recv_sem=recv_sems.at[k], device_id=to, device_id_type=MESH)

        mine = pltpu.make_async_copy(x_ref, rows(*me), local_sem)
        mine.start()
        first = [copy(0, me, sibling, src=x_ref)]
        first += [copy(1 + j, me, (*chip, c), src=x_ref) for j, chip in enumerate(chips)]
        for cp in first:
            cp.start()
        passed = [copy(4 + j, (*chip, c), sibling) for j, chip in enumerate(chips)]
        for j, chip in enumerate(chips):
            copy(1 + j, (*chip, c), me).wait_recv()
            passed[j].start()
        copy(0, sibling, me).wait_recv()
        for j, chip in enumerate(chips):
            copy(4 + j, (*chip, 1 - c), me).wait_recv()
        for cp in first + passed:
            cp.wait_send()
        mine.wait()

    return pl.pallas_call(
        body, name=name, out_shape=jax.ShapeDtypeStruct((8 * m_per, n), blk.dtype),
        in_specs=[pl.BlockSpec(memory_space=pltpu.VMEM)], out_specs=pl.BlockSpec(memory_space=pltpu.VMEM),
        scratch_shapes=[pltpu.SemaphoreType.DMA((7,)), pltpu.SemaphoreType.DMA((7,)), pltpu.SemaphoreType.DMA],
        compiler_params=pltpu.CompilerParams(vmem_limit_bytes=VMEM_LIMIT),
    )(blk)


BIG = dict(w_in=("col", "ab_w_in", 0), w_out=("row", "ab_w_out", 0), up0=("col", "ffn_up_w", 0),
           dn0=("row", "ffn_down_w", 0), pw1=("col", "conv_pw1_w", 0), pw2=("row", "conv_pw2_w", 0),
           up1=("col", "ffn_up_w", 1), dn1=("row", "ffn_down_w", 1))
N_CHIPS = 4
HBM = pl.BlockSpec(memory_space=pltpu.HBM)
SEM = pl.BlockSpec(memory_space=pltpu.SEMAPHORE)
EFFECT = pltpu.SideEffectType.DATAFLOW_SIDE_EFFECTING


def _region(kind, ref, q, n):
    if kind == "col":
        return ref.at[:, pl.ds(q * n, n)]
    return ref.at[pl.ds(q * n, n), :]


def _gather_plan(kind, n):
    def remote(src, land, pos):
        x, y, c, chips = pos
        mine = _region(kind, land, 2 * x + y, n)
        return [(mine, mine, (*chip, c)) for chip in chips]

    return ("gather", kind, n), remote


def _scatter_plan(kind, n):
    def remote(src, land, pos):
        _, _, c, chips = pos
        return [(_region(kind, src, 2 * chip[0] + chip[1], n), land.at[j], (*chip, c)) for j, chip in enumerate(chips)]

    return ("scatter", kind, n), remote


def _sibling_plan():
    def remote(src, land, pos):
        x, y, c, _ = pos
        return [(src, land, (x, y, 1 - c))]

    return ("sibling",), remote


def _split_start(name, items, after=None):
    n = len(items)
    plans = [it[2] for it in items]
    n_in = 2 * n + (after is not None)

    def body(*refs):
        srcs, lands = refs[:n], refs[n:2 * n]
        sends, recvs = refs[n_in:n_in + n], refs[n_in + n:n_in + 2 * n]
        token = refs[n_in + 4 * n]
        pos = _mesh_pos()
        for a, (_, remote) in enumerate(plans):
            for k, (s, d, dev) in enumerate(remote(srcs[a], lands[a], pos)):
                pltpu.make_async_remote_copy(src_ref=s, dst_ref=d, send_sem=sends[a].at[k], recv_sem=recvs[a].at[k],
                                             device_id=dev, device_id_type=MESH).start()
        token[...] = jnp.zeros_like(token)

    sems = [pltpu.SemaphoreType.DMA((it[3],)) for it in items]
    bufs = [pltpu.HBM(it[k].shape, it[k].dtype) for k in (0, 1) for it in items]
    outs = pl.pallas_call(
        body, name=name, out_shape=[*sems, *sems, *bufs, jax.ShapeDtypeStruct((8, LANES), F32)],
        in_specs=[HBM] * (2 * n) + [ANY] * (n_in - 2 * n),
        out_specs=[SEM] * (2 * n) + [HBM] * (2 * n) + [pl.BlockSpec(memory_space=pltpu.VMEM)],
        input_output_aliases={i: 2 * n + i for i in range(2 * n)},
        compiler_params=pltpu.CompilerParams(has_side_effects=EFFECT),
    )(*[pltpu.with_memory_space_constraint(it[k], pltpu.HBM) for k in (0, 1) for it in items],
      *(() if after is None else (after,)))
    state = [(items[a][2], items[a][3], outs[2 * n + a], outs[3 * n + a], outs[a], outs[n + a]) for a in range(n)]
    return state, outs[4 * n]


def _split_wait(name, state, after):
    n = len(state)

    def body(*refs):
        srcs, lands = refs[:n], refs[n:2 * n]
        sends, recvs = refs[2 * n:3 * n], refs[3 * n:4 * n]
        pos = _mesh_pos()
        for a, ((_, remote), *_) in enumerate(state):
            for k, (s, d, dev) in enumerate(remote(srcs[a], lands[a], pos)):
                cp = pltpu.make_async_remote_copy(src_ref=s, dst_ref=d, send_sem=sends[a].at[k], recv_sem=recvs[a].at[k],
                                                  device_id=dev, device_id_type=MESH)
                cp.wait_send()
                cp.wait_recv()

    bufs = [st[k] for k in (2, 3) for st in state]
    outs = pl.pallas_call(
        body, name=name, out_shape=[pltpu.HBM(b.shape, b.dtype) for b in bufs],
        in_specs=[HBM] * (2 * n) + [SEM] * (2 * n) + [ANY], out_specs=[HBM] * (2 * n),
        input_output_aliases={i: i for i in range(2 * n)},
        compiler_params=pltpu.CompilerParams(has_side_effects=EFFECT),
    )(*bufs, *[st[k] for k in (4, 5) for st in state], after)
    return outs[:n], outs[n:]


class _Weights:
    def __init__(self, w, q, after):
        unused = jnp.zeros((16, LANES), BF16)

        def item(name, after):
            kind, pname, layer = BIG[name]
            _, r, c = w[pname].shape
            land = _cast_into_full(f"cast_{name}", w[pname], layer, q, kind, after)
            return unused, land, _gather_plan(kind, c if kind == "col" else r), N_CHIPS - 1

        first, *rest = BIG
        state1, token1 = _split_start("gw_start_first", [item(first, after)], after)
        state2, self.token = _split_start("gw_start_rest", [item(name, token1) for name in rest], token1)
        self.pending = dict(zip(BIG, state1 + state2))
        self.ready = {}

    def get(self, name, after=None):
        if name not in self.ready:
            self.ready[name] = _split_wait(f"gw_wait_{name}", [self.pending.pop(name)], after)[1][0]
        return self.ready[name]


class _GradPipe:
    def __init__(self, q, w, m, v):
        self.q, self.w, self.m, self.v = q, w, m, v
        self.stage, self.results = {}, {}

    def scatter(self, group, grads, after=None):
        items = []
        for name, g in grads.items():
            kind = BIG[name][0]
            rows, cols = g.shape
            n = (cols if kind == "col" else rows) // N_CHIPS
            reg = (rows, n) if kind == "col" else (n, cols)
            items.append((g, lax.empty((N_CHIPS - 1, *reg), BF16), _scatter_plan(kind, n), N_CHIPS - 1))
        state, token = _split_start(f"gs_start_{group}", items, after)
        self.stage[group] = (list(grads), state)
        return token[0, 0]

    def collect(self, group, after):
        names, state = self.stage[group]
        srcs, lands = _split_wait(f"gs_wait_{group}", state, after)
        items = []
        for name, st, g, land in zip(names, state, srcs, lands):
            _, kind, n = st[0][0]
            part = _sum4(f"sum_{name}", g, land, self.q, kind, n)
            items.append((part, lax.empty(part.shape, F32), _sibling_plan(), 1))
        self.stage[group] = (names, _split_start(f"sw_start_{group}", items)[0])

    def finish(self, group, after):
        names, state = self.stage.pop(group)
        srcs, lands = _split_wait(f"sw_wait_{group}", state, after)
        for name, mine, theirs in zip(names, srcs, lands):
            _, pname, layer = BIG[name]
            self.results[pname] = _adamw_sum(f"adamw_{name}", self.w[pname], self.m[pname], self.v[pname], layer,
                                             mine, theirs, self.results.get(pname))


PACK_ROWS = 8


def _pack(arrays):
    flat = jnp.concatenate([a.reshape(-1) for a in arrays])
    n = flat.shape[0]
    padded = -(-n // (PACK_ROWS * LANES)) * (PACK_ROWS * LANES)
    return jnp.pad(flat, (0, padded - n)).reshape(PACK_ROWS, padded // PACK_ROWS)


def _unpack(packed, shapes):
    flat = packed.reshape(-1)
    out, off = [], 0
    for s in shapes:
        n = 1
        for d in s:
            n *= d
        out.append(flat[off:off + n].reshape(s))
        off += n
    return out


REPLICATED = ("ada_b", "norm_mix_g", "norm_ffn_g", "a_vnorm_g", "a_spatial_w", "a_spatial_b", "b_q_norm_g",
              "b_k_norm_g", "ffn_dw_b")
SMALL_SHARDED = ("conv_pw1_b", "conv_dw_w", "conv_dw_b", "conv_ln_g", "conv_ln_b", "conv_pw2_b", "ffn_dw_w")
WEIGHTS = ("ada_w", "ada_b", "norm_mix_g", "norm_ffn_g", "ab_w_in", "a_vnorm_g", "a_spatial_w", "a_spatial_b",
           "b_q_norm_g", "b_k_norm_g", "ab_w_out", "conv_pw1_w", "conv_pw1_b", "conv_dw_w", "conv_dw_b", "conv_ln_g",
           "conv_ln_b", "conv_pw2_w", "conv_pw2_b", "ffn_up_w", "ffn_dw_w", "ffn_dw_b", "ffn_down_w")

def kernel(x, c, positions, ada_w, ada_b, norm_mix_g, norm_ffn_g, ab_w_in, a_vnorm_g, a_spatial_w, a_spatial_b, b_q_norm_g, b_k_norm_g, ab_w_out, conv_pw1_w, conv_pw1_b, conv_dw_w, conv_dw_b, conv_ln_g, conv_ln_b, conv_pw2_w, conv_pw2_b, ffn_up_w, ffn_dw_w, ffn_dw_b, ffn_down_w, loss_target, m_ada_w, m_ada_b, m_norm_mix_g, m_norm_ffn_g, m_ab_w_in, m_a_vnorm_g, m_a_spatial_w, m_a_spatial_b, m_b_q_norm_g, m_b_k_norm_g, m_ab_w_out, m_conv_pw1_w, m_conv_pw1_b, m_conv_dw_w, m_conv_dw_b, m_conv_ln_g, m_conv_ln_b, m_conv_pw2_w, m_conv_pw2_b, m_ffn_up_w, m_ffn_dw_w, m_ffn_dw_b, m_ffn_down_w, v_ada_w, v_ada_b, v_norm_mix_g, v_norm_ffn_g, v_ab_w_in, v_a_vnorm_g, v_a_spatial_w, v_a_spatial_b, v_b_q_norm_g, v_b_k_norm_g, v_ab_w_out, v_conv_pw1_w, v_conv_pw1_b, v_conv_dw_w, v_conv_dw_b, v_conv_ln_g, v_conv_ln_b, v_conv_pw2_w, v_conv_pw2_b, v_ffn_up_w, v_ffn_dw_w, v_ffn_dw_b, v_ffn_down_w):
    w = dict(ada_w=ada_w, ada_b=ada_b, norm_mix_g=norm_mix_g, norm_ffn_g=norm_ffn_g, ab_w_in=ab_w_in, a_vnorm_g=a_vnorm_g, a_spatial_w=a_spatial_w, a_spatial_b=a_spatial_b, b_q_norm_g=b_q_norm_g, b_k_norm_g=b_k_norm_g, ab_w_out=ab_w_out, conv_pw1_w=conv_pw1_w, conv_pw1_b=conv_pw1_b, conv_dw_w=conv_dw_w, conv_dw_b=conv_dw_b, conv_ln_g=conv_ln_g, conv_ln_b=conv_ln_b, conv_pw2_w=conv_pw2_w, conv_pw2_b=conv_pw2_b, ffn_up_w=ffn_up_w, ffn_dw_w=ffn_dw_w, ffn_dw_b=ffn_dw_b, ffn_down_w=ffn_down_w)
    m = dict(ada_w=m_ada_w, ada_b=m_ada_b, norm_mix_g=m_norm_mix_g, norm_ffn_g=m_norm_ffn_g, ab_w_in=m_ab_w_in, a_vnorm_g=m_a_vnorm_g, a_spatial_w=m_a_spatial_w, a_spatial_b=m_a_spatial_b, b_q_norm_g=m_b_q_norm_g, b_k_norm_g=m_b_k_norm_g, ab_w_out=m_ab_w_out, conv_pw1_w=m_conv_pw1_w, conv_pw1_b=m_conv_pw1_b, conv_dw_w=m_conv_dw_w, conv_dw_b=m_conv_dw_b, conv_ln_g=m_conv_ln_g, conv_ln_b=m_conv_ln_b, conv_pw2_w=m_conv_pw2_w, conv_pw2_b=m_conv_pw2_b, ffn_up_w=m_ffn_up_w, ffn_dw_w=m_ffn_dw_w, ffn_dw_b=m_ffn_dw_b, ffn_down_w=m_ffn_down_w)
    v = dict(ada_w=v_ada_w, ada_b=v_ada_b, norm_mix_g=v_norm_mix_g, norm_ffn_g=v_norm_ffn_g, ab_w_in=v_ab_w_in, a_vnorm_g=v_a_vnorm_g, a_spatial_w=v_a_spatial_w, a_spatial_b=v_a_spatial_b, b_q_norm_g=v_b_q_norm_g, b_k_norm_g=v_b_k_norm_g, ab_w_out=v_ab_w_out, conv_pw1_w=v_conv_pw1_w, conv_pw1_b=v_conv_pw1_b, conv_dw_w=v_conv_dw_w, conv_dw_b=v_conv_dw_b, conv_ln_g=v_conv_ln_g, conv_ln_b=v_conv_ln_b, conv_pw2_w=v_conv_pw2_w, conv_pw2_b=v_conv_pw2_b, ffn_up_w=v_ffn_up_w, ffn_dw_w=v_ffn_dw_w, ffn_dw_b=v_ffn_dw_b, ffn_down_w=v_ffn_down_w)
    S, D = x.shape[1], x.shape[2]
    xi, yi, ci = lax.axis_index("x"), lax.axis_index("y"), lax.axis_index("c")
    q = 2 * xi + yi
    b = 2 * q + ci
    take_dev = lambda g: g.reshape(8, PACK_ROWS, -1)

    c_all = _allgather8("ag_c", c.reshape(PACK_ROWS, D // PACK_ROWS)).reshape(8, D)
    n_ada = ada_w.shape[2]
    mod_sh = _ada_fwd("ada_fwd", c_all, ada_w, lax.dynamic_slice_in_dim(ada_b, q * n_ada, n_ada, axis=1))
    sh_shapes = [mod_sh.shape] + [w[n].shape for n in SMALL_SHARDED]
    gathered_mod = _allgather8("ag_mod", _pack([mod_sh] + [w[n] for n in SMALL_SHARDED]))
    per_chip = [_unpack(blk, sh_shapes) for blk in take_dev(gathered_mod)[0::2]]
    mod_g = jnp.stack([pc[0] for pc in per_chip])
    mod_mine = lax.dynamic_index_in_dim(mod_g, b, axis=2, keepdims=False)
    mod = mod_mine.transpose(1, 0, 2).reshape(2, 6, D)
    sp = {n: jnp.concatenate([pc[1 + i] for pc in per_chip], axis=-1) for i, n in enumerate(SMALL_SHARDED)}
    sp.update({n: w[n] for n in REPLICATED if n != "ada_b"})

    wb = _Weights(w, q, gathered_mod)
    mod = mod + wb.token[0, 0]

    pipe = _GradPipe(q, w, m, v)
    lossv, grad_x, small, dmod = _local_step(x[0], loss_target[0], positions[0], mod, wb, sp, pipe)
    loss = lax.psum(0.5 * jnp.sum(lossv) / D, ("x", "y", "c"))

    small_names = [n for n in REPLICATED if n != "ada_b"] + list(SMALL_SHARDED)
    payload = [dmod.reshape(2, 6 * D)] + [small[n] for n in small_names]
    pay_shapes = [p.shape for p in payload]
    gathered = _allgather8("ag_grads", _pack(payload))
    totals = _unpack(_sum8("sum_grads", gathered), pay_shapes)
    grads = dict(zip(["ada_b"] + small_names, totals))
    for n in SMALL_SHARDED:
        n_sh = w[n].shape[-1]
        grads[n] = lax.dynamic_slice_in_dim(grads[n], q * n_sh, n_sh, axis=grads[n].ndim - 1)
    dmod_all = take_dev(gathered).reshape(8, -1)[:, :2 * 6 * D].reshape(8, 2, 6 * D)
    dmod_sh = lax.dynamic_slice_in_dim(dmod_all, q * n_ada, n_ada, axis=2).transpose(1, 0, 2)

    pipe.finish("g3", dmod_sh)
    pipe.collect("g4", dmod_sh)
    grads["ada_w"], delta_ada, m_ada, v_ada = _ada_update("ada_update", c_all, dmod_sh, ada_w, m_ada_w, v_ada_w)
    delta, new_m, new_v = dict(ada_w=delta_ada), dict(ada_w=m_ada), dict(ada_w=v_ada)
    rest = list(REPLICATED) + list(SMALL_SHARDED)
    rest_shapes = [w[n].shape for n in rest]
    outs = _adamw("adamw_small", *[_pack([src[n].reshape(w[n].shape) for n in rest]) for src in (w, grads, m, v)])
    for tgt, packed in zip((delta, new_m, new_v), outs):
        tgt.update(dict(zip(rest, _unpack(packed, rest_shapes))))
    for n in rest:
        grads[n] = grads[n].reshape(w[n].shape)
    pipe.finish("g4", outs[0])
    for n, res in pipe.results.items():
        grads[n], delta[n], new_m[n], new_v[n] = res

    return (loss, grad_x[None], *[grads[n] for n in WEIGHTS], *[delta[n] for n in WEIGHTS],
            *[new_m[n] for n in WEIGHTS], *[new_v[n] for n in WEIGHTS])
```

```python
import functools

import jax
import jax.numpy as jnp
from jax import lax
from jax.experimental import pallas as pl
from jax.experimental.pallas import tpu as pltpu

F32, BF16 = jnp.float32, jnp.bfloat16
EPS = 1e-6
NEG = -1e30
ROPE_THETA = 10000.0
LANES = 128
VMEM_LIMIT = 56 * 1024 * 1024
ADAM_LR, ADAM_B1, ADAM_B2, ADAM_EPS, ADAM_WD, ADAM_STEP = 0.001, 0.9, 0.999, 1e-08, 0.01, 10
MESH = pl.DeviceIdType.MESH


def _params(sem):
    return pltpu.CompilerParams(dimension_semantics=sem, vmem_limit_bytes=VMEM_LIMIT)


_DN = {"nn": (((1,), (0,)), ((), ())), "nt": (((1,), (1,)), ((), ())), "tn": (((0,), (0,)), ((), ()))}


def _matmul(name, a, b, mode, out_dtype, tm=512, tn=512, tk=1024, bias=None):
    if mode == "nn":
        (M, K), N = a.shape, b.shape[1]
    elif mode == "nt":
        (M, K), N = a.shape, b.shape[0]
    else:
        (K, M), N = a.shape, b.shape[1]
    tm, tn, tk = min(tm, M), min(tn, N), min(tk, K)
    assert M % tm == 0 and N % tn == 0 and K % tk == 0, (name, M, N, K, tm, tn, tk)
    nk = K // tk
    if mode == "tn":
        a_spec = pl.BlockSpec((tk, tm), lambda i, j, k: (k, i))
    else:
        a_spec = pl.BlockSpec((tm, tk), lambda i, j, k: (i, k))
    if mode == "nt":
        b_spec = pl.BlockSpec((tn, tk), lambda i, j, k: (j, k))
    else:
        b_spec = pl.BlockSpec((tk, tn), lambda i, j, k: (k, j))
    in_specs, args = [a_spec, b_spec], [a, b]
    if bias is not None:
        in_specs.append(pl.BlockSpec((1, tn), lambda i, j, k: (0, j)))
        args.append(bias)
    n_in = len(args)

    def body(*refs):
        a_ref, b_ref, o_ref = refs[0], refs[1], refs[n_in]
        p = lax.dot_general(a_ref[...], b_ref[...], _DN[mode], preferred_element_type=F32)

        def finish(acc):
            if bias is not None:
                acc = acc + refs[2][...]
            o_ref[...] = acc.astype(o_ref.dtype)

        if nk == 1:
            finish(p)
        else:
            acc_ref = refs[n_in + 1]
            k = pl.program_id(2)

            @pl.when(k == 0)
            def _():
                acc_ref[...] = p

            @pl.when(k > 0)
            def _():
                acc_ref[...] += p

            @pl.when(k == nk - 1)
            def _():
                finish(acc_ref[...])

    return pl.pallas_call(
        body, name=name, grid=(M // tm, N // tn, nk), in_specs=in_specs,
        out_specs=pl.BlockSpec((tm, tn), lambda i, j, k: (i, j)),
        out_shape=jax.ShapeDtypeStruct((M, N), out_dtype),
        scratch_shapes=[pltpu.VMEM((tm, tn), F32)] if nk > 1 else [],
        compiler_params=_params(("parallel", "parallel", "arbitrary")),
    )(*args)


def _rowcall(name, body, nrows, tm, ins, outs, scratch=()):
    nblk = nrows // tm
    assert nrows % tm == 0

    def spec(kind, shape):
        k = kind[0]
        if k == "row":
            cw, cb = kind[1] or shape[-1], kind[2]
            return pl.BlockSpec((tm, cw), lambda i: (i, cb))
        if k == "prev":
            hb, cw, cb = kind[1], kind[2] or shape[-1], kind[3]
            r = tm // hb
            return pl.BlockSpec((hb, cw), lambda i: (jnp.maximum(i * r - 1, 0), cb))
        if k == "next":
            hb, cw, cb = kind[1], kind[2] or shape[-1], kind[3]
            r, last = tm // hb, nrows // hb - 1
            return pl.BlockSpec((hb, cw), lambda i: (jnp.minimum((i + 1) * r, last), cb))
        if k == "off":
            off, cw, cb = kind[1], kind[2] or shape[-1], kind[3]
            return pl.BlockSpec((tm, cw), lambda i: (jnp.clip(i + off, 0, nblk - 1), cb))
        if k == "cls":
            dil, off = kind[1], kind[2]
            return pl.BlockSpec((dil, tm // dil, shape[-1]), lambda i: (0, jnp.clip(i + off, 0, nblk - 1), 0))
        nd = len(shape)
        return pl.BlockSpec(tuple(shape), lambda i: (0,) * nd)

    has_acc = any(o[2][0] == "acc" for o in outs)
    return pl.pallas_call(
        body, name=name, grid=(nblk,),
        in_specs=[spec(kind, a.shape) for a, kind in ins],
        out_specs=[spec(kind, shape) for shape, _, kind in outs],
        out_shape=[jax.ShapeDtypeStruct(tuple(shape), dt) for shape, dt, _ in outs],
        scratch_shapes=list(scratch),
        compiler_params=_params(("arbitrary",) if has_acc else ("parallel",)),
    )(*[a for a, _ in ins])


ROW = ("row", None, 0)
FULL = ("full",)
ACC = ("acc",)


def _colsum(x):
    return jnp.sum(x, axis=0, keepdims=True)


def _acc_add(i, ref, val, rows=None):
    idx = (slice(None),) * len(ref.shape) if rows is None else rows

    @pl.when(i == 0)
    def _():
        ref[idx] = val

    @pl.when(i > 0)
    def _():
        ref[idx] = ref[idx] + val


def _sigmoid(x):
    return 1.0 / (1.0 + jnp.exp(-x))


def _gelu(x):
    return 0.5 * x * (1.0 + lax.erf(x * (2.0 ** -0.5)))


def _gelu_grad(x):
    return 0.5 * (1.0 + lax.erf(x * (2.0 ** -0.5))) + x * jnp.exp(-0.5 * x * x) * ((2.0 * jnp.pi) ** -0.5)


SUBLANES = 8


def _phases(ext, sign):
    n = ext.shape[0]
    return [ext if b == 0 else pltpu.roll(ext, b if sign > 0 else n - b, axis=0) for b in range(SUBLANES)]


def _shift_prev(phases, s, hb):
    a, b = divmod(s, SUBLANES)
    return phases[b][hb - SUBLANES * a:phases[b].shape[0] - SUBLANES * a]


def _shift_next(phases, s, tm):
    a, b = divmod(s, SUBLANES)
    return phases[b][SUBLANES * a:SUBLANES * a + tm]


def _rms_mod_val(x, g, sc, sh):
    r = lax.rsqrt(jnp.mean(x * x, axis=-1, keepdims=True) + EPS)
    return x * r * g * (1.0 + sc) + sh


def _mod_first(name, x, g, sc, sh, tm=256):
    S, D = x.shape

    def body(x_ref, g_ref, sc_ref, sh_ref, h_ref):
        h_ref[...] = _rms_mod_val(x_ref[...], g_ref[...], sc_ref[...], sh_ref[...]).astype(BF16)

    return _rowcall(name, body, S, tm, [(x, ROW), (g, FULL), (sc, FULL), (sh, FULL)], [((S, D), BF16, ROW)])[0]


def _resid_mod(name, x, y, gate, g, sc, sh, tm=256):
    S, D = x.shape

    def body(x_ref, y_ref, gate_ref, g_ref, sc_ref, sh_ref, xo_ref, h_ref):
        xn = x_ref[...] + gate_ref[...] * y_ref[...]
        xo_ref[...] = xn
        h_ref[...] = _rms_mod_val(xn, g_ref[...], sc_ref[...], sh_ref[...]).astype(BF16)

    return _rowcall(name, body, S, tm,
                    [(x, ROW), (y, ROW), (gate, FULL), (g, FULL), (sc, FULL), (sh, FULL)],
                    [((S, D), F32, ROW), ((S, D), BF16, ROW)])


def _gate_bwd_val(i, d, y_ref, gate_ref, dy_ref, dg_ref, db_ref):
    dy = d * gate_ref[...]
    dy_ref[...] = dy.astype(BF16)
    _acc_add(i, dg_ref, _colsum(d * y_ref[...]))
    _acc_add(i, db_ref, _colsum(dy))


GATE_OUTS = lambda S, D: [((S, D), BF16, ROW), ((1, D), F32, ACC), ((1, D), F32, ACC)]


def _loss_head(name, x, y, gate, tgt, tm=256):
    S, D = x.shape

    def body(x_ref, y_ref, gate_ref, t_ref, dx_ref, l_ref, dy_ref, dg_ref, db_ref):
        i = pl.program_id(0)
        err = x_ref[...] + gate_ref[...] * y_ref[...] - t_ref[...]
        d = err * (1.0 / D)
        dx_ref[...] = d
        _acc_add(i, l_ref, _colsum(err * err))
        _gate_bwd_val(i, d, y_ref, gate_ref, dy_ref, dg_ref, db_ref)

    return _rowcall(name, body, S, tm, [(x, ROW), (y, ROW), (gate, FULL), (tgt, ROW)],
                    [((S, D), F32, ROW), ((1, D), F32, ACC)] + GATE_OUTS(S, D))


def _mod_bwd(name, dxo, dh, x, g, sc, y=None, gate=None, tm=256):
    S, D = x.shape
    gated = y is not None

    def body(d_ref, dh_ref, x_ref, g_ref, sc_ref, *rest):
        dx_ref, dsh_ref, dsc_ref, dg_ref = rest[2 * gated:2 * gated + 4]
        i = pl.program_id(0)
        xv, dh_v, gv = x_ref[...], dh_ref[...], g_ref[...]
        r = lax.rsqrt(jnp.mean(xv * xv, axis=-1, keepdims=True) + EPS)
        n = xv * r
        _acc_add(i, dsh_ref, _colsum(dh_v))
        _acc_add(i, dsc_ref, _colsum(dh_v * (n * gv)))
        dy = dh_v * (1.0 + sc_ref[...])
        _acc_add(i, dg_ref, _colsum(dy * n))
        dn = dy * gv
        dx = d_ref[...] + r * (dn - n * jnp.mean(dn * n, axis=-1, keepdims=True))
        dx_ref[...] = dx
        if gated:
            _gate_bwd_val(i, dx, rest[0], rest[1], *rest[6:9])

    ins = [(dxo, ROW), (dh, ROW), (x, ROW), (g, FULL), (sc, FULL)] + ([(y, ROW), (gate, FULL)] if gated else [])
    outs = [((S, D), F32, ROW), ((1, D), F32, ACC), ((1, D), F32, ACC), ((1, D), F32, ACC)]
    return _rowcall(name, body, S, tm, ins, outs + (GATE_OUTS(S, D) if gated else []))


HB16 = 16


def _conv3_val(ph, w, b, hb):
    return w[2:3] * _shift_prev(ph, 0, hb) + w[1:2] * _shift_prev(ph, 1, hb) + w[0:1] * _shift_prev(ph, 2, hb) + b


def _halo_first(halo_ref, tile_ref, live):
    return _phases(jnp.concatenate([halo_ref[...].astype(F32) * live, tile_ref[...].astype(F32)], axis=0), 1)


def _glu3_fwd(name, u, w, b, tm=128):
    S, F2 = u.shape
    Fh = F2 // 2

    def body(ua_ref, ub_ref, ha_ref, hb_ref, w_ref, b_ref, o_ref, z_ref):
        live = (pl.program_id(0) > 0).astype(F32)
        wv, bv = w_ref[...], b_ref[...]
        za = _conv3_val(_halo_first(ha_ref, ua_ref, live), wv[:, :Fh], bv[:, :Fh], HB16)
        zb = _conv3_val(_halo_first(hb_ref, ub_ref, live), wv[:, Fh:], bv[:, Fh:], HB16)
        o_ref[...] = (za * _sigmoid(za) * zb).astype(BF16)
        z_ref[:, :Fh] = za.astype(BF16)
        z_ref[:, Fh:] = zb.astype(BF16)

    return _rowcall(name, body, S, tm,
                    [(u, ("row", Fh, 0)), (u, ("row", Fh, 1)), (u, ("prev", HB16, Fh, 0)), (u, ("prev", HB16, Fh, 1)),
                     (w, FULL), (b, FULL)],
                    [((S, Fh), BF16, ROW), ((S, F2), BF16, ROW)])


def _glu3_bwd(name, z, dhm, tm=128):
    S, F2 = z.shape
    Fh = F2 // 2

    def body(za_ref, zb_ref, d_ref, dz_ref, db_ref):
        i = pl.program_id(0)
        za, zb, d = za_ref[...].astype(F32), zb_ref[...].astype(F32), d_ref[...]
        sg = _sigmoid(za)
        da = d * zb * (sg * (1.0 + za * (1.0 - sg)))
        db = d * (za * sg)
        dz_ref[:, :Fh] = da.astype(BF16)
        dz_ref[:, Fh:] = db.astype(BF16)
        _acc_add(i, db_ref, jnp.concatenate([_colsum(da), _colsum(db)], axis=1))

    return _rowcall(name, body, S, tm, [(z, ("row", Fh, 0)), (z, ("row", Fh, 1)), (dhm, ROW)],
                    [((S, F2), BF16, ROW), ((1, F2), F32, ACC)])


def _conv3_bwd(name, dz, u, w, tm=128):
    S, F2 = dz.shape
    nblk = S // tm
    K = w.shape[0]

    def body(d_ref, n_ref, u_ref, w_ref, o_ref, dw_ref):
        i = pl.program_id(0)
        live = (i < nblk - 1).astype(F32)
        ph = _phases(jnp.concatenate([d_ref[...].astype(F32), n_ref[...].astype(F32) * live], axis=0), -1)
        wv, uv = w_ref[...], u_ref[...].astype(F32)
        shifted = [_shift_next(ph, K - 1 - k, tm) for k in range(K)]
        o_ref[...] = functools.reduce(lambda a, t: a + t, [wv[k:k + 1] * shifted[k] for k in range(K)]).astype(BF16)
        for k in range(K):
            _acc_add(i, dw_ref, _colsum(uv * shifted[k]), rows=(slice(k, k + 1), slice(None)))

        @pl.when(i == 0)
        def _():
            dw_ref[K:, :] = jnp.zeros((dw_ref.shape[0] - K, F2), F32)

    return _rowcall(name, body, S, tm, [(dz, ROW), (dz, ("next", HB16, None, 0)), (u, ROW), (w, FULL)],
                    [((S, F2), BF16, ROW), ((SUBLANES, F2), F32, ACC)])


def _ffn_fwd(name, h, w_up, w_dn, dw_w, dw_b):
    u = _matmul(f"{name}_up", h, w_up, "nn", BF16, tm=2048)
    hm, z = _glu3_fwd(f"{name}_glu", u, dw_w, dw_b)
    f = _matmul(f"{name}_dn", hm, w_dn, "nn", F32, tm=1024, tn=1024, tk=w_dn.shape[0])
    return f, (u, hm, z)


def _ffn_bwd(name, dy, h, u, hm, z, w_up, w_dn, dw_w, dw_b):
    Fh = w_dn.shape[0]
    dhm = _matmul(f"{name}_ddn_x", dy, w_dn, "nt", F32, tm=1024, tn=Fh // 2)
    g_dn = _matmul(f"{name}_ddn_w", hm, dy, "tn", BF16, tm=Fh // 2, tn=1024, tk=1024)
    dz, g_dw_b = _glu3_bwd(f"{name}_dglu", z, dhm)
    du, taps = _conv3_bwd(f"{name}_dconv", dz, u, dw_w)
    g_up = _matmul(f"{name}_dup_w", h, du, "tn", BF16, tm=1024, tn=Fh // 2, tk=1024)
    dh = _matmul(f"{name}_dup_x", du, w_up, "nt", F32, tm=1024, tn=1024, tk=Fh // 2)
    return dh, dict(up=g_up, dn=g_dn, dw_w=taps[0:dw_w.shape[0]], dw_b=g_dw_b)


HB32 = 32


def _glu31_fwd(name, p, w, b, tm=256):
    S, D2 = p.shape
    D = D2 // 2
    K = w.shape[0]

    def body(a_ref, g_ref, ha_ref, hg_ref, w_ref, b_ref, o_ref):
        live = (pl.program_id(0) > 0).astype(F32)
        y1 = a_ref[...] * _sigmoid(g_ref[...])
        ph = _phases(jnp.concatenate([ha_ref[...] * _sigmoid(hg_ref[...]) * live, y1], axis=0), 1)
        wv = w_ref[...]
        acc = b_ref[...] + wv[K - 1:K] * y1
        for k in range(K - 1):
            acc = acc + wv[k:k + 1] * _shift_prev(ph, K - 1 - k, HB32)
        o_ref[...] = acc

    return _rowcall(name, body, S, tm,
                    [(p, ("row", D, 0)), (p, ("row", D, 1)), (p, ("prev", HB32, D, 0)), (p, ("prev", HB32, D, 1)),
                     (w, FULL), (b, FULL)],
                    [((S, D), F32, ROW)])[0]


def _ln_silu_fwd(name, y2, g, b, tm=256):
    S, D = y2.shape

    def body(y_ref, g_ref, b_ref, o_ref):
        y = y_ref[...]
        mu = jnp.mean(y, axis=-1, keepdims=True)
        yc = y - mu
        rs = lax.rsqrt(jnp.mean(yc * yc, axis=-1, keepdims=True) + EPS)
        y3 = yc * rs * g_ref[...] + b_ref[...]
        o_ref[...] = (y3 * _sigmoid(y3)).astype(BF16)

    return _rowcall(name, body, S, tm, [(y2, ROW), (g, FULL), (b, FULL)], [((S, D), BF16, ROW)])[0]


def _ln_silu_bwd(name, y2, dy4, g, b, tm=256):
    S, D = y2.shape

    def body(y_ref, d_ref, g_ref, b_ref, o_ref, dg_ref, db_ref):
        i = pl.program_id(0)
        y, gv = y_ref[...], g_ref[...]
        mu = jnp.mean(y, axis=-1, keepdims=True)
        yc = y - mu
        rs = lax.rsqrt(jnp.mean(yc * yc, axis=-1, keepdims=True) + EPS)
        n = yc * rs
        y3 = n * gv + b_ref[...]
        sg = _sigmoid(y3)
        dy3 = d_ref[...] * (sg * (1.0 + y3 * (1.0 - sg)))
        _acc_add(i, db_ref, _colsum(dy3))
        _acc_add(i, dg_ref, _colsum(dy3 * n))
        dn = dy3 * gv
        o_ref[...] = rs * (dn - jnp.mean(dn, axis=-1, keepdims=True) - n * jnp.mean(dn * n, axis=-1, keepdims=True))

    return _rowcall(name, body, S, tm, [(y2, ROW), (dy4, ROW), (g, FULL), (b, FULL)],
                    [((S, D), F32, ROW), ((1, D), F32, ACC), ((1, D), F32, ACC)])


def _glu31_bwd(name, p, dy2, w, tm=256):
    S, D2 = p.shape
    D = D2 // 2
    K = w.shape[0]
    nblk = S // tm

    conv_rows, tap_rows, tap_group, tap_unroll = 16, SUBLANES, 4, 4

    def body(a_ref, g_ref, d_ref, dn_ref, w_ref, dp_ref, dw_ref, dcb_ref, dpb_ref, ph_d, y1_ref, dy1_ref, wb_ref):
        i = pl.program_id(0)
        live_next = (i < nblk - 1).astype(F32)
        a, sg, d = a_ref[...], _sigmoid(g_ref[...]), d_ref[...]
        y1_ref[...] = a * sg
        for b, ph in enumerate(_phases(jnp.concatenate([d, dn_ref[...] * live_next], axis=0), -1)):
            ph_d[b] = ph
        taps = [divmod(K - 1 - k, SUBLANES) for k in range(K)]

        @pl.when(i == 0)
        def _():
            for k in range(K):
                wb_ref[k] = jnp.broadcast_to(w_ref[k:k + 1, :], (SUBLANES, D))

        def conv_rows_at(rb, carry):
            r0 = pl.multiple_of(rb * conv_rows, conv_rows)
            accs = [jnp.zeros((SUBLANES, D), F32) for _ in range(conv_rows // SUBLANES)]
            for k, (rows8, phase) in enumerate(taps):
                wk = wb_ref[k]
                for u in range(len(accs)):
                    accs[u] = accs[u] + wk * ph_d[phase, pl.ds(r0 + SUBLANES * (rows8 + u), SUBLANES), :]
            for u, acc in enumerate(accs):
                dy1_ref[pl.ds(r0 + SUBLANES * u, SUBLANES), :] = acc
            return carry

        lax.fori_loop(0, tm // conv_rows, conv_rows_at, 0)
        for k0 in range(0, K, tap_group):
            group = taps[k0:k0 + tap_group]

            def tap_rows_at(rb, accs, group=group):
                for u in range(tap_unroll):
                    r0 = pl.multiple_of((rb * tap_unroll + u) * tap_rows, tap_rows)
                    yv = y1_ref[pl.ds(r0, tap_rows), :]
                    accs = tuple(acc + yv * ph_d[phase, pl.ds(r0 + SUBLANES * rows8, tap_rows), :]
                                 for acc, (rows8, phase) in zip(accs, group))
                return accs

            accs = lax.fori_loop(0, tm // (tap_rows * tap_unroll), tap_rows_at,
                                 tuple(jnp.zeros((tap_rows, D), F32) for _ in group))
            for j, acc in enumerate(accs):
                _acc_add(i, dw_ref, _colsum(acc), rows=(slice(k0 + j, k0 + j + 1), slice(None)))

        @pl.when(i == 0)
        def _():
            dw_ref[K:, :] = jnp.zeros((dw_ref.shape[0] - K, D), F32)

        _acc_add(i, dcb_ref, _colsum(d))
        dy1 = dy1_ref[...]
        da = dy1 * sg
        dg = dy1 * a * sg * (1.0 - sg)
        dp_ref[:, :D] = da.astype(BF16)
        dp_ref[:, D:] = dg.astype(BF16)
        _acc_add(i, dpb_ref, jnp.concatenate([_colsum(da), _colsum(dg)], axis=1))

    return _rowcall(name, body, S, tm,
                    [(p, ("row", D, 0)), (p, ("row", D, 1)), (dy2, ROW), (dy2, ("next", HB32, None, 0)), (w, FULL)],
                    [((S, D2), BF16, ROW), ((HB32, D), F32, ACC), ((1, D), F32, ACC), ((1, D2), F32, ACC)],
                    scratch=[pltpu.VMEM((SUBLANES, tm + HB32, D), F32), pltpu.VMEM((tm, D), F32),
                             pltpu.VMEM((tm, D), F32), pltpu.VMEM((K, SUBLANES, D), F32)])


CHUNK = 128
A_GROUPS = 4


def _group_ln(gv):
    ns, rss = [], []
    for g in range(A_GROUPS):
        xg = gv[:, g * LANES:(g + 1) * LANES]
        xc = xg - jnp.mean(xg, axis=-1, keepdims=True)
        rs = lax.rsqrt(jnp.mean(xc * xc, axis=-1, keepdims=True) + EPS)
        ns.append(xc * rs)
        rss.append(jnp.broadcast_to(rs, xg.shape))
    return jnp.concatenate(ns, axis=1), jnp.concatenate(rss, axis=1)


def _tril_mask():
    r = lax.broadcasted_iota(jnp.int32, (CHUNK, CHUNK), 0)
    c = lax.broadcasted_iota(jnp.int32, (CHUNK, CHUNK), 1)
    return r >= c


def _spatial(ws_ref, x, dn):
    mask = _tril_mask()
    rows = []
    for ci in range(x.shape[0] // CHUNK):
        cols = []
        for g in range(A_GROUPS):
            wm = jnp.where(mask, ws_ref[g], 0.0).astype(BF16)
            xb = x[ci * CHUNK:(ci + 1) * CHUNK, g * LANES:(g + 1) * LANES]
            cols.append(lax.dot_general(wm, xb, dn, preferred_element_type=F32))
        rows.append(jnp.concatenate(cols, axis=1))
    return jnp.concatenate(rows, axis=0)


def _mixa_fwd(name, z, vg, ws, bias_full, tm=256):
    S = z.shape[0]
    W = A_GROUPS * LANES

    def body(u_ref, v_ref, vg_ref, ws_ref, b_ref, o_ref):
        nh, _ = _group_ln(_gelu(v_ref[...]))
        vn = (nh * vg_ref[...]).astype(BF16)
        f = _spatial(ws_ref, vn, _DN["nn"]) + jnp.concatenate([b_ref[...]] * (tm // CHUNK), axis=0)
        o_ref[...] = (_gelu(u_ref[...]) * f).astype(BF16)

    return _rowcall(name, body, S, tm,
                    [(z, ("row", W, 0)), (z, ("row", W, 1)), (vg, FULL), (ws, FULL), (bias_full, FULL)],
                    [((S, W), BF16, ROW)])[0]


def _mixa_bwd(name, z, dyab, vg, ws, bias_full, tm=256):
    S = z.shape[0]
    W = A_GROUPS * LANES
    nch = tm // CHUNK

    def body(u_ref, v_ref, d_ref, vg_ref, ws_ref, b_ref, dz_ref, dws_ref, dbf_ref, dvg_ref):
        i = pl.program_id(0)
        u, v, d, vgv = u_ref[...], v_ref[...], d_ref[...], vg_ref[...]
        nh, rs = _group_ln(_gelu(v))
        vn = (nh * vgv).astype(BF16)
        f = _spatial(ws_ref, vn, _DN["nn"]) + jnp.concatenate([b_ref[...]] * nch, axis=0)
        dz_ref[:, :W] = (d * f * _gelu_grad(u)).astype(BF16)
        df = d * _gelu(u)
        dbf = df[0:CHUNK]
        for ci in range(1, nch):
            dbf = dbf + df[ci * CHUNK:(ci + 1) * CHUNK]
        _acc_add(i, dbf_ref, dbf)
        dfb = df.astype(BF16)
        mask = _tril_mask()
        for g in range(A_GROUPS):
            acc = jnp.zeros((CHUNK, CHUNK), F32)
            for ci in range(nch):
                blk = (slice(ci * CHUNK, (ci + 1) * CHUNK), slice(g * LANES, (g + 1) * LANES))
                acc = acc + lax.dot_general(dfb[blk], vn[blk], _DN["nt"], preferred_element_type=F32)
            _acc_add(i, dws_ref, jnp.where(mask, acc, 0.0)[None], rows=(slice(g, g + 1), slice(None), slice(None)))
        dvn = _spatial(ws_ref, dfb, _DN["tn"])
        _acc_add(i, dvg_ref, _colsum(dvn * nh))
        dnh = dvn * vgv
        parts = []
        for g in range(A_GROUPS):
            cs = slice(g * LANES, (g + 1) * LANES)
            dg_, ng = dnh[:, cs], nh[:, cs]
            parts.append(dg_ - jnp.mean(dg_, axis=-1, keepdims=True) - ng * jnp.mean(dg_ * ng, axis=-1, keepdims=True))
        dz_ref[:, W:] = (rs * jnp.concatenate(parts, axis=1) * _gelu_grad(v)).astype(BF16)

    return _rowcall(name, body, S, tm,
                    [(z, ("row", W, 0)), (z, ("row", W, 1)), (dyab, ("row", W, 0)), (vg, FULL), (ws, FULL),
                     (bias_full, FULL)],
                    [((S, 2 * W), BF16, ROW), ((A_GROUPS, CHUNK, CHUNK), F32, ACC), ((CHUNK, W), F32, ACC),
                     ((1, W), F32, ACC)])


HEAD = 64
N_HEADS = 8
BW = HEAD * N_HEADS
QB = 128
DILATIONS = (1, 4, 16)
QK_SCALE = HEAD ** -0.5


def _gsum64(x, ones_bd):
    x1 = x.astype(BF16)
    r1 = x - x1.astype(F32)
    x2 = r1.astype(BF16)
    x3 = (r1 - x2.astype(F32)).astype(BF16)
    dot = lambda t: jnp.dot(t, ones_bd, preferred_element_type=F32)
    return dot(x1) + dot(x2) + dot(x3)


def _swap32(x):
    n = x.shape[-1]
    up = pltpu.roll(x, n - HEAD // 2, axis=1)
    dn = pltpu.roll(x, HEAD // 2, axis=1)
    lane = lax.broadcasted_iota(jnp.int32, x.shape, 1)
    return jnp.where((lane % HEAD) < HEAD // 2, up, dn)


def _tile4(t):
    return jnp.concatenate([t] * (BW // LANES), axis=1)


def _stage_spec(tm):
    return pltpu.VMEM((BW // LANES, tm, LANES), F32)


def _to_classes(stage, x, dil):
    tm = x.shape[0]
    for j in range(BW // LANES):
        stage[j] = x[:, j * LANES:(j + 1) * LANES]
    return [jnp.concatenate([stage.at[j][pl.ds(r, tm // dil, stride=dil), :] for j in range(BW // LANES)], axis=1)
            for r in range(dil)]


def _from_classes(stage, cls, dil):
    rows = cls.shape[1]
    for r in range(dil):
        for j in range(BW // LANES):
            stage.at[j][pl.ds(r, rows, stride=dil), :] = cls[r, :, j * LANES:(j + 1) * LANES]
    return jnp.concatenate([stage[j] for j in range(BW // LANES)], axis=1)


def _cls_view(t, dil):
    return t if dil == 1 else t.reshape(dil, t.shape[0] // dil, t.shape[1])


def _cls_kind(dil, off=0):
    return ("off", off, None, 0) if dil == 1 else ("cls", dil, off)


def _cls_out(S, dil, dtype):
    return ((S, BW) if dil == 1 else (dil, S // dil, BW), dtype, _cls_kind(dil))


def _flat(t):
    return t.reshape(-1, t.shape[-1])


def _qkv_fwd(name, z, cos, sin, ones_bd, qg, kg, tm=256):
    S = z.shape[0]
    nd = len(DILATIONS)

    def body(q_ref, k_ref, v_ref, c_ref, s_ref, o_ref, qg_ref, kg_ref, *rest):
        outs, stage = rest[:3 * nd], rest[3 * nd]
        c, s, ob = _tile4(c_ref[...]), _tile4(s_ref[...]), o_ref[...]

        def norm_rope(x, g):
            r = lax.rsqrt(_gsum64(x * x, ob) * (1.0 / HEAD) + EPS)
            xn = x * r * g
            return xn * c + _swap32(xn) * s

        vals = [norm_rope(q_ref[...], qg_ref[...]) * QK_SCALE, norm_rope(k_ref[...], kg_ref[...]), v_ref[...]]
        for a, val in enumerate(vals):
            for b, dil in enumerate(DILATIONS):
                if dil == 1:
                    outs[nd * a + b][...] = val.astype(BF16)
                else:
                    for r, rows in enumerate(_to_classes(stage, val, dil)):
                        outs[nd * a + b][r] = rows.astype(BF16)

    outs = _rowcall(name, body, S, tm,
                    [(z, ("row", BW, 2)), (z, ("row", BW, 3)), (z, ("row", BW, 4)), (cos, ROW), (sin, ROW),
                     (ones_bd, FULL), (qg, FULL), (kg, FULL)],
                    [_cls_out(S, dil, BF16) for _ in range(3) for dil in DILATIONS], scratch=[_stage_spec(tm)])
    return [[_flat(outs[nd * a + b]) for a in range(3)] for b in range(nd)]


PAIR = 2 * HEAD


def _pair_scores(q_ref, kp_ref, kc_ref, hp, half, seg_blocks):
    ps = slice(hp * PAIR, (hp + 1) * PAIR)
    mine = (lax.broadcasted_iota(jnp.int32, (1, PAIR), 1) >= HEAD) == (half == 1)
    qm = jnp.where(mine, q_ref[:, ps], jnp.zeros((), BF16))
    kcat = jnp.concatenate([kp_ref[:, ps], kc_ref[:, ps]], axis=0)
    s = lax.dot_general(qm, kcat, _DN["nt"], preferred_element_type=F32)
    qi = lax.broadcasted_iota(jnp.int32, (QB, 2 * QB), 0)
    kj = lax.broadcasted_iota(jnp.int32, (QB, 2 * QB), 1)
    has_prev = (pl.program_id(0) % seg_blocks) != 0
    valid = (kj >= qi) & (kj <= qi + QB) & ((kj >= QB) | has_prev)
    return mine, qm, kcat, s, valid


def _attn_fwd(name, q, k, v, dil):
    S = q.shape[0]
    seg_blocks = S // dil // QB
    PREV = ("off", -1, None, 0)

    def body(q_ref, kp_ref, kc_ref, vp_ref, vc_ref, o_ref, l_ref):
        for hp in range(N_HEADS // 2):
            ps = slice(hp * PAIR, (hp + 1) * PAIR)
            vcat = jnp.concatenate([vp_ref[:, ps], vc_ref[:, ps]], axis=0)
            o_pair = l_pair = None
            for half in range(2):
                mine, _, _, s, valid = _pair_scores(q_ref, kp_ref, kc_ref, hp, half, seg_blocks)
                s = jnp.where(valid, s, NEG)
                m = jnp.max(s, axis=-1, keepdims=True)
                p = jnp.exp(s - m)
                den = jnp.sum(p, axis=-1, keepdims=True)
                o = jnp.dot(p.astype(BF16), vcat, preferred_element_type=F32) / den
                lse = jnp.broadcast_to(m + jnp.log(den), (QB, PAIR))
                o_pair, l_pair = (o, lse) if half == 0 else (jnp.where(mine, o, o_pair), jnp.where(mine, lse, l_pair))
            o_ref[:, ps] = o_pair
            l_ref[:, ps] = l_pair

    return _rowcall(name, body, S, QB, [(q, ROW), (k, PREV), (k, ROW), (v, PREV), (v, ROW)],
                    [((S, BW), F32, ROW)] * 2)


def _attn_bwd(name, q, k, v, do, lse, delta, dil):
    S = q.shape[0]
    seg_blocks = S // dil // QB
    PREV = ("off", -1, None, 0)

    def body(q_ref, kp_ref, kc_ref, vp_ref, vc_ref, do_ref, l_ref, dl_ref, dq_ref, dkc_ref, dkp_ref, dvc_ref, dvp_ref):
        for hp in range(N_HEADS // 2):
            ps = slice(hp * PAIR, (hp + 1) * PAIR)
            vcat = jnp.concatenate([vp_ref[:, ps], vc_ref[:, ps]], axis=0)
            do_p = do_ref[:, ps].astype(BF16)
            dq = dk = dv = None
            for half in range(2):
                mine, qm, kcat, s, valid = _pair_scores(q_ref, kp_ref, kc_ref, hp, half, seg_blocks)
                col = hp * PAIR + half * HEAD
                p = jnp.where(valid, jnp.exp(s - l_ref[:, col:col + 1]), 0.0)
                dom = jnp.where(mine, do_p, jnp.zeros((), BF16))
                dp = lax.dot_general(dom, vcat, _DN["nt"], preferred_element_type=F32)
                ds = (p * (dp - dl_ref[:, col:col + 1])).astype(BF16)
                dq_h = jnp.dot(ds, kcat, preferred_element_type=F32)
                dk_h = lax.dot_general(ds, qm, _DN["tn"], preferred_element_type=F32)
                dv_h = lax.dot_general(p.astype(BF16), dom, _DN["tn"], preferred_element_type=F32)
                dq, dk, dv = (dq_h, dk_h, dv_h) if half == 0 else (jnp.where(mine, dq_h, dq), dk + dk_h, dv + dv_h)
            dq_ref[:, ps] = dq
            dkp_ref[:, ps] = dk[:QB]
            dkc_ref[:, ps] = dk[QB:]
            dvp_ref[:, ps] = dv[:QB]
            dvc_ref[:, ps] = dv[QB:]

    return _rowcall(name, body, S, QB,
                    [(q, ROW), (k, PREV), (k, ROW), (v, PREV), (v, ROW), do, (lse, ROW), (delta, ROW)],
                    [((S, BW), F32, ROW)] * 5)


def _merge_fwd(name, branches, tm=256):
    S = branches[0][0].shape[0]
    nd = len(DILATIONS)

    def body(*refs):
        ins, (y_ref, yb_ref), l_refs, stage = refs[:2 * nd], refs[2 * nd:2 * nd + 2], refs[2 * nd + 2:3 * nd + 2], refs[-1]
        os_, ls = [], []
        for b, dil in enumerate(DILATIONS):
            o, l = ins[2 * b][...], ins[2 * b + 1][...]
            os_.append(o if dil == 1 else _from_classes(stage, o, dil))
            ls.append(l if dil == 1 else _from_classes(stage, l, dil))
        m = functools.reduce(jnp.maximum, ls)
        es = [jnp.exp(l - m) for l in ls]
        den = functools.reduce(lambda a, e: a + e, es)
        y = functools.reduce(lambda a, t: a + t, [e * o for e, o in zip(es, os_)]) / den
        y_ref[...] = y
        yb_ref[...] = y.astype(BF16)
        lse = m + jnp.log(den)
        for b, dil in enumerate(DILATIONS):
            if dil == 1:
                l_refs[b][...] = lse
            else:
                for r, rows in enumerate(_to_classes(stage, lse, dil)):
                    l_refs[b][r] = rows

    ins = [(_cls_view(t, dil), _cls_kind(dil)) for pair, dil in zip(branches, DILATIONS) for t in pair]
    outs = _rowcall(name, body, S, tm, ins,
                    [((S, BW), F32, ROW), ((S, BW), BF16, ROW)] + [_cls_out(S, dil, F32) for dil in DILATIONS],
                    scratch=[_stage_spec(tm)])
    return outs[0], outs[1], [_flat(t) for t in outs[2:]]


def _delta(name, dyab, yb, ones_bd, tm=256):
    S = yb.shape[0]
    nd = len(DILATIONS)

    def body(d_ref, y_ref, o_ref, *rest):
        dl_refs, do_refs, stage = rest[:nd], rest[nd:2 * nd - 1], rest[-1]
        d = d_ref[...]
        dl = _gsum64(d * y_ref[...], o_ref[...])
        for b, dil in enumerate(DILATIONS):
            if dil == 1:
                dl_refs[b][...] = dl
            else:
                for r, rows in enumerate(_to_classes(stage, dl, dil)):
                    dl_refs[b][r] = rows
                for r, rows in enumerate(_to_classes(stage, d, dil)):
                    do_refs[b - 1][r] = rows.astype(BF16)

    outs = _rowcall(name, body, S, tm, [(dyab, ("row", BW, 1)), (yb, ROW), (ones_bd, FULL)],
                    [_cls_out(S, dil, F32) for dil in DILATIONS] + [_cls_out(S, dil, BF16) for dil in DILATIONS[1:]],
                    scratch=[_stage_spec(tm)])
    return [_flat(t) for t in outs[:nd]], [_flat(t) for t in outs[nd:]]


def _qkv_bwd(name, z, cos, sin, ones_bd, qg, kg, pieces):
    S = z.shape[0]
    nblk = S // QB

    def body(q_ref, k_ref, c_ref, s_ref, o_ref, qg_ref, kg_ref, *rest):
        pr, (dz_ref, dqg_ref, dkg_ref), stage = rest[:15], rest[15:18], rest[18]
        i = pl.program_id(0)
        c, s, ob = _tile4(c_ref[...]), _tile4(s_ref[...]), o_ref[...]
        dq = dk = dv = None
        for b, dil in enumerate(DILATIONS):
            a_q, a_kc, a_kp, a_vc, a_vp = [r[...] for r in pr[5 * b:5 * b + 5]]
            live = ((i + dil) < nblk).astype(F32)
            tq, tk, tv = a_q, a_kc + a_kp * live, a_vc + a_vp * live
            if dil > 1:
                tq, tk, tv = (_from_classes(stage, t, dil) for t in (tq, tk, tv))
            dq, dk, dv = (tq, tk, tv) if b == 0 else (dq + tq, dk + tk, dv + tv)

        def back(x, g, d_rot, acc_ref):
            r = lax.rsqrt(_gsum64(x * x, ob) * (1.0 / HEAD) + EPS)
            n = x * r
            dxn = d_rot * c + _swap32(d_rot * s)
            _acc_add(i, acc_ref, _colsum(dxn * n))
            dn = dxn * g
            return r * (dn - n * (_gsum64(dn * n, ob) * (1.0 / HEAD)))

        dz_ref[:, :BW] = back(q_ref[...], qg_ref[...], dq * QK_SCALE, dqg_ref).astype(BF16)
        dz_ref[:, BW:2 * BW] = back(k_ref[...], kg_ref[...], dk, dkg_ref).astype(BF16)
        dz_ref[:, 2 * BW:] = dv.astype(BF16)

    ins = [(z, ("row", BW, 2)), (z, ("row", BW, 3)), (cos, ROW), (sin, ROW), (ones_bd, FULL), (qg, FULL), (kg, FULL)]
    for piece, dil in zip(pieces, DILATIONS):
        a_q, a_kc, a_kp, a_vc, a_vp = (_cls_view(t, dil) for t in piece)
        own, prev = _cls_kind(dil), _cls_kind(dil, dil)
        ins += [(a_q, own), (a_kc, own), (a_kp, prev), (a_vc, own), (a_vp, prev)]
    return _rowcall(name, body, S, QB, ins,
                    [((S, 3 * BW), BF16, ROW), ((1, BW), F32, ACC), ((1, BW), F32, ACC)], scratch=[_stage_spec(QB)])


def _local_step(x0, tgt, pos, mod, wb, sp, pipe):
    S, D = x0.shape
    md = lambda l, j: mod[l, j:j + 1]
    sh_m, sc_m, g_m, sh_f, sc_f, g_f = ([md(l, j) for l in range(2)] for j in range(6))
    nm_g, nf_g = sp["norm_mix_g"], sp["norm_ffn_g"]

    inv_freq = 1.0 / (ROPE_THETA ** (jnp.arange(0, HEAD, 2, dtype=F32) / HEAD))
    ang = pos.astype(F32)[:, None] * inv_freq
    cs, sn = jnp.cos(ang), jnp.sin(ang)
    cos = jnp.concatenate([cs, cs, cs, cs], axis=1)
    sin = jnp.concatenate([-sn, sn, -sn, sn], axis=1)
    head_of = jnp.arange(BW) // HEAD
    ones_bd = (head_of[:, None] == head_of[None, :]).astype(BF16)
    qg = jnp.tile(sp["b_q_norm_g"].reshape(1, HEAD), (1, N_HEADS))
    kg = jnp.tile(sp["b_k_norm_g"].reshape(1, HEAD), (1, N_HEADS))
    vg = sp["a_vnorm_g"].reshape(1, A_GROUPS * LANES)
    ws = sp["a_spatial_w"][0]
    bias_full = jnp.repeat(sp["a_spatial_b"][0].T, LANES, axis=1)
    ffn_s = [(sp["ffn_dw_w"][l], sp["ffn_dw_b"][l:l + 1]) for l in range(2)]

    h0 = _mod_first("l0_mod", x0, nm_g[0:1], sc_m[0], sh_m[0])
    z = _matmul("l0_in", h0, wb.get("w_in", h0), "nn", F32, tm=2048)
    ya = _mixa_fwd("l0_mixa", z, vg, ws, bias_full)
    qkv = _qkv_fwd("l0_qkv", z, cos, sin, ones_bd, qg, kg)
    branches = [_attn_fwd(f"l0_att{dil}", *qkv[b], dil) for b, dil in enumerate(DILATIONS)]
    yb, yb16, lses = _merge_fwd("l0_merge", branches)
    yab = jnp.concatenate([ya, yb16], axis=1)
    y0 = _matmul("l0_out", yab, wb.get("w_out", yab), "nn", F32, tm=1024, tn=1024)
    x1, h1 = _resid_mod("l0_res1", x0, y0, g_m[0], nf_g[0:1], sc_f[0], sh_f[0])
    ffn_w = [(wb.get("up0", h1), wb.get("dn0", h1), *ffn_s[0])]
    f0, saved0 = _ffn_fwd("l0_ffn", h1, *ffn_w[0])
    x2, h2 = _resid_mod("l0_res2", x1, f0, g_f[0], nm_g[1:2], sc_m[1], sh_m[1])
    p = _matmul("l1_pw1", h2, wb.get("pw1", h2), "nn", F32, tm=2048, bias=sp["conv_pw1_b"])
    y2 = _glu31_fwd("l1_glu", p, sp["conv_dw_w"][0], sp["conv_dw_b"])
    y4 = _ln_silu_fwd("l1_ln", y2, sp["conv_ln_g"], sp["conv_ln_b"])
    y1 = _matmul("l1_pw2", y4, wb.get("pw2", y4), "nn", F32, tm=1024, tn=1024, bias=sp["conv_pw2_b"])
    x3, h3 = _resid_mod("l1_res1", x2, y1, g_m[1], nf_g[1:2], sc_f[1], sh_f[1])
    ffn_w.append((wb.get("up1", h3), wb.get("dn1", h3), *ffn_s[1]))
    f1, saved1 = _ffn_fwd("l1_ffn", h3, *ffn_w[1])
    dx4, lossv, dy, dgate_f1, _ = _loss_head("loss", x3, f1, g_f[1], tgt)

    dh, gf1 = _ffn_bwd("l1_ffn", dy, h3, *saved1, *ffn_w[1])
    tok = pipe.scatter("g1", dict(dn1=gf1["dn"], up1=gf1["up"]))
    dx3, dsh_f1, dsc_f1, dnf1, dy, dgate_m1, dpw2_b = _mod_bwd("l1_dmod2", dx4, dh, x3, nf_g[1:2], sc_f[1] + tok,
                                                             y1, g_m[1])
    dy4 = _matmul("l1_dpw2_x", dy, wb.get("pw2"), "nt", F32, tm=1024, tn=1024)
    g_pw2 = _matmul("l1_dpw2_w", y4, dy, "tn", BF16, tm=1024, tn=1024)
    dy2, dln_g, dln_b = _ln_silu_bwd("l1_dln", y2, dy4, sp["conv_ln_g"], sp["conv_ln_b"])
    dp, ddw_w, ddw_b, dpw1_b = _glu31_bwd("l1_dglu", p, dy2, sp["conv_dw_w"][0])
    g_pw1 = _matmul("l1_dpw1_w", h2, dp, "tn", BF16, tm=1024, tn=1024)
    tok = pipe.scatter("g2", dict(pw2=g_pw2, pw1=g_pw1))
    pipe.collect("g1", g_pw1)
    dh = _matmul("l1_dpw1_x", dp, wb.get("pw1"), "nt", F32, tm=1024, tn=1024, tk=2048)
    dx2, dsh_m1, dsc_m1, dnm1, dy, dgate_f0, _ = _mod_bwd("l1_dmod1", dx3, dh, x2, nm_g[1:2], sc_m[1] + tok,
                                                        f0, g_f[0])
    dh, gf0 = _ffn_bwd("l0_ffn", dy, h1, *saved0, *ffn_w[0])
    pipe.finish("g1", gf0["up"])
    pipe.collect("g2", gf0["up"])
    dx1, dsh_f0, dsc_f0, dnf0, dy, dgate_m0, _ = _mod_bwd("l0_dmod2", dx2, dh, x1, nf_g[0:1], sc_f[0], y0, g_m[0])
    dyab = _matmul("l0_dout_x", dy, wb.get("w_out"), "nt", F32, tm=1024, tn=1024)
    g_out = _matmul("l0_dout_w", yab, dy, "tn", BF16, tm=1024, tn=1024)
    tok = pipe.scatter("g3", dict(dn0=gf0["dn"], up0=gf0["up"], w_out=g_out))
    vg = vg + tok
    dza, dws, dbf, dvg = _mixa_bwd("l0_dmixa", z, dyab, vg, ws, bias_full)
    deltas, dos = _delta("l0_delta", dyab, yb, ones_bd)
    pieces = []
    for b, dil in enumerate(DILATIONS):
        do = (dyab, ("row", BW, 1)) if dil == 1 else (dos[b - 1], ROW)
        pieces.append(_attn_bwd(f"l0_datt{dil}", *qkv[b], do, lses[b], deltas[b], dil))
    dzb, dqg, dkg = _qkv_bwd("l0_dqkv", z, cos, sin, ones_bd, qg, kg, pieces)
    dz = jnp.concatenate([dza, dzb], axis=1)
    g_in = _matmul("l0_din_w", h0, dz, "tn", BF16, tm=1024, tn=1280)
    tok = pipe.scatter("g4", dict(w_in=g_in))
    pipe.finish("g2", g_in)
    pipe.collect("g3", g_in)
    dh = _matmul("l0_din_x", dz, wb.get("w_in"), "nt", F32, tm=1024, tn=1024, tk=2560)
    grad_x, dsh_m0, dsc_m0, dnm0 = _mod_bwd("l0_dmod1", dx1, dh, x0, nm_g[0:1], sc_m[0] + tok)

    dmod = jnp.stack([jnp.concatenate([dsh_m0, dsc_m0, dgate_m0, dsh_f0, dsc_f0, dgate_f0], axis=0),
                      jnp.concatenate([dsh_m1, dsc_m1, dgate_m1, dsh_f1, dsc_f1, dgate_f1], axis=0)])
    small = dict(
        norm_mix_g=jnp.concatenate([dnm0, dnm1], axis=0),
        norm_ffn_g=jnp.concatenate([dnf0, dnf1], axis=0),
        a_vnorm_g=dvg.reshape(1, A_GROUPS, LANES),
        a_spatial_w=dws[None],
        a_spatial_b=dbf.reshape(CHUNK, A_GROUPS, LANES).sum(-1).T[None],
        b_q_norm_g=dqg.reshape(N_HEADS, HEAD).sum(0)[None],
        b_k_norm_g=dkg.reshape(N_HEADS, HEAD).sum(0)[None],
        conv_pw1_b=dpw1_b, conv_dw_w=ddw_w[None, :sp["conv_dw_w"].shape[1]], conv_dw_b=ddw_b,
        conv_ln_g=dln_g, conv_ln_b=dln_b, conv_pw2_b=dpw2_b,
        ffn_dw_w=jnp.stack([gf0["dw_w"], gf1["dw_w"]]),
        ffn_dw_b=jnp.concatenate([gf0["dw_b"], gf1["dw_b"]], axis=0),
    )
    return lossv, grad_x, small, dmod


ADA_TN = 512


def _ada_fwd(name, c_all, ada_w, ada_b_sh):
    L, D, N = ada_w.shape
    B = c_all.shape[0]

    def body(c_ref, w_ref, b_ref, o_ref):
        cv = c_ref[...]
        ca = (cv * _sigmoid(cv)).astype(BF16)
        o_ref[0] = jnp.dot(ca, w_ref[0].astype(BF16), preferred_element_type=F32) + b_ref[0]

    return pl.pallas_call(
        body, name=name, grid=(L, N // ADA_TN),
        in_specs=[pl.BlockSpec((B, D), lambda l, j: (0, 0)), pl.BlockSpec((1, D, ADA_TN), lambda l, j: (l, 0, j)),
                  pl.BlockSpec((1, 1, ADA_TN), lambda l, j: (l, 0, j))],
        out_specs=pl.BlockSpec((1, B, ADA_TN), lambda l, j: (l, 0, j)),
        out_shape=jax.ShapeDtypeStruct((L, B, N), F32),
        compiler_params=_params(("parallel", "parallel")),
    )(c_all, ada_w, ada_b_sh.reshape(L, 1, N))


def _adamw_val(w, g, m, v):
    m2 = ADAM_B1 * m + (1.0 - ADAM_B1) * g
    v2 = ADAM_B2 * v + (1.0 - ADAM_B2) * (g * g)
    m_hat = m2 / (1.0 - ADAM_B1 ** ADAM_STEP)
    v_hat = v2 / (1.0 - ADAM_B2 ** ADAM_STEP)
    delta = -ADAM_LR * (m_hat / (jnp.sqrt(v_hat) + ADAM_EPS) + ADAM_WD * w)
    return delta, m2, v2


def _ada_update(name, c_all, dmod_sh, w, m, v):
    L, D, N = w.shape
    B = c_all.shape[0]

    def body(c_ref, d_ref, w_ref, m_ref, v_ref, g_ref, dl_ref, mo_ref, vo_ref):
        cv = c_ref[...]
        ca = (cv * _sigmoid(cv)).astype(BF16)
        g = lax.dot_general(ca, d_ref[0].astype(BF16), _DN["tn"], preferred_element_type=F32)
        g_ref[0] = g
        dl_ref[0], mo_ref[0], vo_ref[0] = _adamw_val(w_ref[0], g, m_ref[0], v_ref[0])

    wspec = pl.BlockSpec((1, D, ADA_TN), lambda l, j: (l, 0, j))
    return pl.pallas_call(
        body, name=name, grid=(L, N // ADA_TN),
        in_specs=[pl.BlockSpec((B, D), lambda l, j: (0, 0)), pl.BlockSpec((1, B, ADA_TN), lambda l, j: (l, 0, j)),
                  wspec, wspec, wspec],
        out_specs=[wspec] * 4, out_shape=[jax.ShapeDtypeStruct((L, D, N), F32)] * 4,
        compiler_params=_params(("parallel", "parallel")),
    )(c_all, dmod_sh, w, m, v)


def _adamw(name, w, g, m, v):
    R, C = w.shape
    tm = R
    for cand in (256, 128, 64, 32, 16, 8):
        if R % cand == 0 and cand * C * 4 <= (1 << 20):
            tm = cand
            break

    def body(w_ref, g_ref, m_ref, v_ref, d_ref, mo_ref, vo_ref):
        d_ref[...], mo_ref[...], vo_ref[...] = _adamw_val(w_ref[...], g_ref[...], m_ref[...], v_ref[...])

    return _rowcall(name, body, R, tm, [(w, ROW), (g, ROW), (m, ROW), (v, ROW)], [((R, C), F32, ROW)] * 3)


def _row_tile(rows, width, itemsize=4, limit=1 << 20):
    for cand in (512, 256, 128, 64, 32, 16):
        if rows % cand == 0 and cand * width * itemsize <= limit:
            return cand
    raise ValueError((rows, width))


def _cast_into_full(name, a, layer, q, kind, after):
    L, r, c = a.shape
    tm = _row_tile(r, c)
    if kind == "col":
        full, o_spec = (r, N_CHIPS * c), pl.BlockSpec((tm, c), lambda i, q_ref: (i, q_ref[0]))
    else:
        full, o_spec = (N_CHIPS * r, c), pl.BlockSpec((tm, c), lambda i, q_ref: (q_ref[0] * (r // tm) + i, 0))

    def body(q_ref, a_ref, after_ref, o_ref):
        o_ref[...] = a_ref[0].astype(BF16)

    return pl.pallas_call(
        body, name=name,
        grid_spec=pltpu.PrefetchScalarGridSpec(
            num_scalar_prefetch=1, grid=(r // tm,),
            in_specs=[pl.BlockSpec((1, tm, c), lambda i, q_ref: (layer, i, 0)), ANY], out_specs=o_spec),
        out_shape=jax.ShapeDtypeStruct(full, BF16), compiler_params=_params(("parallel",)),
    )(q.reshape(1).astype(jnp.int32), a, after)


def _sum4(name, g, rcv, q, kind, n):
    r, c = rcv.shape[1:]
    tm = _row_tile(r, c)
    if kind == "col":
        g_spec = pl.BlockSpec((tm, n), lambda i, q_ref: (i, q_ref[0]))
    else:
        g_spec = pl.BlockSpec((tm, c), lambda i, q_ref: (q_ref[0] * (n // tm) + i, 0))

    def body(q_ref, g_ref, r_ref, o_ref):
        acc = g_ref[...].astype(F32)
        for j in range(3):
            acc = acc + r_ref[j].astype(F32)
        o_ref[...] = acc

    return pl.pallas_call(
        body, name=name,
        grid_spec=pltpu.PrefetchScalarGridSpec(
            num_scalar_prefetch=1, grid=(r // tm,),
            in_specs=[g_spec, pl.BlockSpec((3, tm, c), lambda i, q_ref: (0, i, 0))],
            out_specs=pl.BlockSpec((tm, c), lambda i, q_ref: (i, 0))),
        out_shape=jax.ShapeDtypeStruct((r, c), F32), compiler_params=_params(("parallel",)),
    )(q.reshape(1).astype(jnp.int32), g, rcv)


def _adamw_sum(name, w, m, v, layer, mine, theirs, prev):
    L, r, c = w.shape
    tm = _row_tile(r, c, limit=1 << 19)
    lay = pl.BlockSpec((1, tm, c), lambda i: (layer, i, 0))
    flat = pl.BlockSpec((tm, c), lambda i: (i, 0))
    n_prev = 0 if prev is None else 4

    def body(w_ref, m_ref, v_ref, a_ref, b_ref, *rest):
        g_ref, d_ref, mo_ref, vo_ref = rest[n_prev:]
        g = a_ref[...] + b_ref[...]
        g_ref[0] = g
        d_ref[0], mo_ref[0], vo_ref[0] = _adamw_val(w_ref[0], g, m_ref[0], v_ref[0])

    return pl.pallas_call(
        body, name=name, grid=(r // tm,),
        in_specs=[lay, lay, lay, flat, flat] + [ANY] * n_prev, out_specs=[lay] * 4,
        out_shape=[jax.ShapeDtypeStruct((L, r, c), F32)] * 4,
        input_output_aliases={5 + k: k for k in range(n_prev)},
        compiler_params=_params(("parallel",)),
    )(w, m, v, mine, theirs, *(prev or ()))


def _sum8(name, gathered):
    R, N = gathered.shape
    P = R // 8

    def body(g_ref, o_ref):
        acc = g_ref[0:P, :]
        for d in range(1, 8):
            acc = acc + g_ref[d * P:(d + 1) * P, :]
        o_ref[...] = acc

    return pl.pallas_call(body, name=name, out_shape=jax.ShapeDtypeStruct((P, N), F32),
                          compiler_params=pltpu.CompilerParams(vmem_limit_bytes=VMEM_LIMIT))(gathered)


ANY = pl.BlockSpec(memory_space=pl.ANY)


def _mesh_pos():
    x, y, c = lax.axis_index("x"), lax.axis_index("y"), lax.axis_index("c")
    other_chips = [(1 - x, y), (x, 1 - y), (1 - x, 1 - y)]
    return x, y, c, other_chips


def _allgather8(name, blk):
    m_per, n = blk.shape

    def body(x_ref, out_ref, send_sems, recv_sems, local_sem):
        x, y, c, chips = _mesh_pos()
        me, sibling = (x, y, c), (x, y, 1 - c)

        def rows(px, py, pc):
            return out_ref.at[pl.ds((4 * px + 2 * py + pc) * m_per, m_per), :]

        def copy(k, block, to, src=None):
            return pltpu.make_async_remote_copy(
                src_ref=rows(*block) if src is None else src, dst_ref=rows(*block),
                send_sem=send_sems.at[k], recv_sem=recv_sems.at[k], device_id=to, device_id_type=MESH)

        mine = pltpu.make_async_copy(x_ref, rows(*me), local_sem)
        mine.start()
        first = [copy(0, me, sibling, src=x_ref)]
        first += [copy(1 + j, me, (*chip, c), src=x_ref) for j, chip in enumerate(chips)]
        for cp in first:
            cp.start()
        passed = [copy(4 + j, (*chip, c), sibling) for j, chip in enumerate(chips)]
        for j, chip in enumerate(chips):
            copy(1 + j, (*chip, c), me).wait_recv()
            passed[j].start()
        copy(0, sibling, me).wait_recv()
        for j, chip in enumerate(chips):
            copy(4 + j, (*chip, 1 - c), me).wait_recv()
        for cp in first + passed:
            cp.wait_send()
        mine.wait()

    return pl.pallas_call(
        body, name=name, out_shape=jax.ShapeDtypeStruct((8 * m_per, n), blk.dtype),
        in_specs=[pl.BlockSpec(memory_space=pltpu.VMEM)], out_specs=pl.BlockSpec(memory_space=pltpu.VMEM),
        scratch_shapes=[pltpu.SemaphoreType.DMA((7,)), pltpu.SemaphoreType.DMA((7,)), pltpu.SemaphoreType.DMA],
        compiler_params=pltpu.CompilerParams(vmem_limit_bytes=VMEM_LIMIT),
    )(blk)


BIG = dict(w_in=("col", "ab_w_in", 0), w_out=("row", "ab_w_out", 0), up0=("col", "ffn_up_w", 0),
           dn0=("row", "ffn_down_w", 0), pw1=("col", "conv_pw1_w", 0), pw2=("row", "conv_pw2_w", 0),
           up1=("col", "ffn_up_w", 1), dn1=("row", "ffn_down_w", 1))
N_CHIPS = 4
HBM = pl.BlockSpec(memory_space=pltpu.HBM)
SEM = pl.BlockSpec(memory_space=pltpu.SEMAPHORE)
EFFECT = pltpu.SideEffectType.DATAFLOW_SIDE_EFFECTING


def _region(kind, ref, q, n):
    if kind == "col":
        return ref.at[:, pl.ds(q * n, n)]
    return ref.at[pl.ds(q * n, n), :]


def _gather_plan(kind, n):
    def remote(src, land, pos):
        x, y, c, chips = pos
        mine = _region(kind, land, 2 * x + y, n)
        return [(mine, mine, (*chip, c)) for chip in chips]

    return ("gather", kind, n), remote


def _scatter_plan(kind, n):
    def remote(src, land, pos):
        _, _, c, chips = pos
        return [(_region(kind, src, 2 * chip[0] + chip[1], n), land.at[j], (*chip, c)) for j, chip in enumerate(chips)]

    return ("scatter", kind, n), remote


def _sibling_plan():
    def remote(src, land, pos):
        x, y, c, _ = pos
        return [(src, land, (x, y, 1 - c))]

    return ("sibling",), remote


def _split_start(name, items, after=None):
    n = len(items)
    plans = [it[2] for it in items]
    n_in = 2 * n + (after is not None)

    def body(*refs):
        srcs, lands = refs[:n], refs[n:2 * n]
        sends, recvs = refs[n_in:n_in + n], refs[n_in + n:n_in + 2 * n]
        token = refs[n_in + 4 * n]
        pos = _mesh_pos()
        for a, (_, remote) in enumerate(plans):
            for k, (s, d, dev) in enumerate(remote(srcs[a], lands[a], pos)):
                pltpu.make_async_remote_copy(src_ref=s, dst_ref=d, send_sem=sends[a].at[k], recv_sem=recvs[a].at[k],
                                             device_id=dev, device_id_type=MESH).start()
        token[...] = jnp.zeros_like(token)

    sems = [pltpu.SemaphoreType.DMA((it[3],)) for it in items]
    bufs = [pltpu.HBM(it[k].shape, it[k].dtype) for k in (0, 1) for it in items]
    outs = pl.pallas_call(
        body, name=name, out_shape=[*sems, *sems, *bufs, jax.ShapeDtypeStruct((8, LANES), F32)],
        in_specs=[HBM] * (2 * n) + [ANY] * (n_in - 2 * n),
        out_specs=[SEM] * (2 * n) + [HBM] * (2 * n) + [pl.BlockSpec(memory_space=pltpu.VMEM)],
        input_output_aliases={i: 2 * n + i for i in range(2 * n)},
        compiler_params=pltpu.CompilerParams(has_side_effects=EFFECT),
    )(*[pltpu.with_memory_space_constraint(it[k], pltpu.HBM) for k in (0, 1) for it in items],
      *(() if after is None else (after,)))
    state = [(items[a][2], items[a][3], outs[2 * n + a], outs[3 * n + a], outs[a], outs[n + a]) for a in range(n)]
    return state, outs[4 * n]


def _split_wait(name, state, after):
    n = len(state)

    def body(*refs):
        srcs, lands = refs[:n], refs[n:2 * n]
        sends, recvs = refs[2 * n:3 * n], refs[3 * n:4 * n]
        pos = _mesh_pos()
        for a, ((_, remote), *_) in enumerate(state):
            for k, (s, d, dev) in enumerate(remote(srcs[a], lands[a], pos)):
                cp = pltpu.make_async_remote_copy(src_ref=s, dst_ref=d, send_sem=sends[a].at[k], recv_sem=recvs[a].at[k],
                                                  device_id=dev, device_id_type=MESH)
                cp.wait_send()
                cp.wait_recv()

    bufs = [st[k] for k in (2, 3) for st in state]
    outs = pl.pallas_call(
        body, name=name, out_shape=[pltpu.HBM(b.shape, b.dtype) for b in bufs],
        in_specs=[HBM] * (2 * n) + [SEM] * (2 * n) + [ANY], out_specs=[HBM] * (2 * n),
        input_output_aliases={i: i for i in range(2 * n)},
        compiler_params=pltpu.CompilerParams(has_side_effects=EFFECT),
    )(*bufs, *[st[k] for k in (4, 5) for st in state], after)
    return outs[:n], outs[n:]


class _Weights:
    def __init__(self, w, q, after):
        unused = jnp.zeros((16, LANES), BF16)

        def item(name, after):
            kind, pname, layer = BIG[name]
            _, r, c = w[pname].shape
            land = _cast_into_full(f"cast_{name}", w[pname], layer, q, kind, after)
            return unused, land, _gather_plan(kind, c if kind == "col" else r), N_CHIPS - 1

        first, *rest = BIG
        state1, token1 = _split_start("gw_start_first", [item(first, after)], after)
        state2, self.token = _split_start("gw_start_rest", [item(name, token1) for name in rest], token1)
        self.pending = dict(zip(BIG, state1 + state2))
        self.ready = {}

    def get(self, name, after=None):
        if name not in self.ready:
            self.ready[name] = _split_wait(f"gw_wait_{name}", [self.pending.pop(name)], after)[1][0]
        return self.ready[name]


class _GradPipe:
    def __init__(self, q, w, m, v):
        self.q, self.w, self.m, self.v = q, w, m, v
        self.stage, self.results = {}, {}

    def scatter(self, group, grads, after=None):
        items = []
        for name, g in grads.items():
            kind = BIG[name][0]
            rows, cols = g.shape
            n = (cols if kind == "col" else rows) // N_CHIPS
            reg = (rows, n) if kind == "col" else (n, cols)
            items.append((g, lax.empty((N_CHIPS - 1, *reg), BF16), _scatter_plan(kind, n), N_CHIPS - 1))
        state, token = _split_start(f"gs_start_{group}", items, after)
        self.stage[group] = (list(grads), state)
        return token[0, 0]

    def collect(self, group, after):
        names, state = self.stage[group]
        srcs, lands = _split_wait(f"gs_wait_{group}", state, after)
        items = []
        for name, st, g, land in zip(names, state, srcs, lands):
            _, kind, n = st[0][0]
            part = _sum4(f"sum_{name}", g, land, self.q, kind, n)
            items.append((part, lax.empty(part.shape, F32), _sibling_plan(), 1))
        self.stage[group] = (names, _split_start(f"sw_start_{group}", items)[0])

    def finish(self, group, after):
        names, state = self.stage.pop(group)
        srcs, lands = _split_wait(f"sw_wait_{group}", state, after)
        for name, mine, theirs in zip(names, srcs, lands):
            _, pname, layer = BIG[name]
            self.results[pname] = _adamw_sum(f"adamw_{name}", self.w[pname], self.m[pname], self.v[pname], layer,
                                             mine, theirs, self.results.get(pname))


PACK_ROWS = 8


def _pack(arrays):
    flat = jnp.concatenate([a.reshape(-1) for a in arrays])
    n = flat.shape[0]
    padded = -(-n // (PACK_ROWS * LANES)) * (PACK_ROWS * LANES)
    return jnp.pad(flat, (0, padded - n)).reshape(PACK_ROWS, padded // PACK_ROWS)


def _unpack(packed, shapes):
    flat = packed.reshape(-1)
    out, off = [], 0
    for s in shapes:
        n = 1
        for d in s:
            n *= d
        out.append(flat[off:off + n].reshape(s))
        off += n
    return out


REPLICATED = ("ada_b", "norm_mix_g", "norm_ffn_g", "a_vnorm_g", "a_spatial_w", "a_spatial_b", "b_q_norm_g",
              "b_k_norm_g", "ffn_dw_b")
SMALL_SHARDED = ("conv_pw1_b", "conv_dw_w", "conv_dw_b", "conv_ln_g", "conv_ln_b", "conv_pw2_b", "ffn_dw_w")
WEIGHTS = ("ada_w", "ada_b", "norm_mix_g", "norm_ffn_g", "ab_w_in", "a_vnorm_g", "a_spatial_w", "a_spatial_b",
           "b_q_norm_g", "b_k_norm_g", "ab_w_out", "conv_pw1_w", "conv_pw1_b", "conv_dw_w", "conv_dw_b", "conv_ln_g",
           "conv_ln_b", "conv_pw2_w", "conv_pw2_b", "ffn_up_w", "ffn_dw_w", "ffn_dw_b", "ffn_down_w")

def kernel(x, c, positions, ada_w, ada_b, norm_mix_g, norm_ffn_g, ab_w_in, a_vnorm_g, a_spatial_w, a_spatial_b, b_q_norm_g, b_k_norm_g, ab_w_out, conv_pw1_w, conv_pw1_b, conv_dw_w, conv_dw_b, conv_ln_g, conv_ln_b, conv_pw2_w, conv_pw2_b, ffn_up_w, ffn_dw_w, ffn_dw_b, ffn_down_w, loss_target, m_ada_w, m_ada_b, m_norm_mix_g, m_norm_ffn_g, m_ab_w_in, m_a_vnorm_g, m_a_spatial_w, m_a_spatial_b, m_b_q_norm_g, m_b_k_norm_g, m_ab_w_out, m_conv_pw1_w, m_conv_pw1_b, m_conv_dw_w, m_conv_dw_b, m_conv_ln_g, m_conv_ln_b, m_conv_pw2_w, m_conv_pw2_b, m_ffn_up_w, m_ffn_dw_w, m_ffn_dw_b, m_ffn_down_w, v_ada_w, v_ada_b, v_norm_mix_g, v_norm_ffn_g, v_ab_w_in, v_a_vnorm_g, v_a_spatial_w, v_a_spatial_b, v_b_q_norm_g, v_b_k_norm_g, v_ab_w_out, v_conv_pw1_w, v_conv_pw1_b, v_conv_dw_w, v_conv_dw_b, v_conv_ln_g, v_conv_ln_b, v_conv_pw2_w, v_conv_pw2_b, v_ffn_up_w, v_ffn_dw_w, v_ffn_dw_b, v_ffn_down_w):
    w = dict(ada_w=ada_w, ada_b=ada_b, norm_mix_g=norm_mix_g, norm_ffn_g=norm_ffn_g, ab_w_in=ab_w_in, a_vnorm_g=a_vnorm_g, a_spatial_w=a_spatial_w, a_spatial_b=a_spatial_b, b_q_norm_g=b_q_norm_g, b_k_norm_g=b_k_norm_g, ab_w_out=ab_w_out, conv_pw1_w=conv_pw1_w, conv_pw1_b=conv_pw1_b, conv_dw_w=conv_dw_w, conv_dw_b=conv_dw_b, conv_ln_g=conv_ln_g, conv_ln_b=conv_ln_b, conv_pw2_w=conv_pw2_w, conv_pw2_b=conv_pw2_b, ffn_up_w=ffn_up_w, ffn_dw_w=ffn_dw_w, ffn_dw_b=ffn_dw_b, ffn_down_w=ffn_down_w)
    m = dict(ada_w=m_ada_w, ada_b=m_ada_b, norm_mix_g=m_norm_mix_g, norm_ffn_g=m_norm_ffn_g, ab_w_in=m_ab_w_in, a_vnorm_g=m_a_vnorm_g, a_spatial_w=m_a_spatial_w, a_spatial_b=m_a_spatial_b, b_q_norm_g=m_b_q_norm_g, b_k_norm_g=m_b_k_norm_g, ab_w_out=m_ab_w_out, conv_pw1_w=m_conv_pw1_w, conv_pw1_b=m_conv_pw1_b, conv_dw_w=m_conv_dw_w, conv_dw_b=m_conv_dw_b, conv_ln_g=m_conv_ln_g, conv_ln_b=m_conv_ln_b, conv_pw2_w=m_conv_pw2_w, conv_pw2_b=m_conv_pw2_b, ffn_up_w=m_ffn_up_w, ffn_dw_w=m_ffn_dw_w, ffn_dw_b=m_ffn_dw_b, ffn_down_w=m_ffn_down_w)
    v = dict(ada_w=v_ada_w, ada_b=v_ada_b, norm_mix_g=v_norm_mix_g, norm_ffn_g=v_norm_ffn_g, ab_w_in=v_ab_w_in, a_vnorm_g=v_a_vnorm_g, a_spatial_w=v_a_spatial_w, a_spatial_b=v_a_spatial_b, b_q_norm_g=v_b_q_norm_g, b_k_norm_g=v_b_k_norm_g, ab_w_out=v_ab_w_out, conv_pw1_w=v_conv_pw1_w, conv_pw1_b=v_conv_pw1_b, conv_dw_w=v_conv_dw_w, conv_dw_b=v_conv_dw_b, conv_ln_g=v_conv_ln_g, conv_ln_b=v_conv_ln_b, conv_pw2_w=v_conv_pw2_w, conv_pw2_b=v_conv_pw2_b, ffn_up_w=v_ffn_up_w, ffn_dw_w=v_ffn_dw_w, ffn_dw_b=v_ffn_dw_b, ffn_down_w=v_ffn_down_w)
    S, D = x.shape[1], x.shape[2]
    xi, yi, ci = lax.axis_index("x"), lax.axis_index("y"), lax.axis_index("c")
    q = 2 * xi + yi
    b = 2 * q + ci
    take_dev = lambda g: g.reshape(8, PACK_ROWS, -1)

    c_all = _allgather8("ag_c", c.reshape(PACK_ROWS, D // PACK_ROWS)).reshape(8, D)
    n_ada = ada_w.shape[2]
    mod_sh = _ada_fwd("ada_fwd", c_all, ada_w, lax.dynamic_slice_in_dim(ada_b, q * n_ada, n_ada, axis=1))
    sh_shapes = [mod_sh.shape] + [w[n].shape for n in SMALL_SHARDED]
    gathered_mod = _allgather8("ag_mod", _pack([mod_sh] + [w[n] for n in SMALL_SHARDED]))
    per_chip = [_unpack(blk, sh_shapes) for blk in take_dev(gathered_mod)[0::2]]
    mod_g = jnp.stack([pc[0] for pc in per_chip])
    mod_mine = lax.dynamic_index_in_dim(mod_g, b, axis=2, keepdims=False)
    mod = mod_mine.transpose(1, 0, 2).reshape(2, 6, D)
    sp = {n: jnp.concatenate([pc[1 + i] for pc in per_chip], axis=-1) for i, n in enumerate(SMALL_SHARDED)}
    sp.update({n: w[n] for n in REPLICATED if n != "ada_b"})

    wb = _Weights(w, q, gathered_mod)
    mod = mod + wb.token[0, 0]

    pipe = _GradPipe(q, w, m, v)
    lossv, grad_x, small, dmod = _local_step(x[0], loss_target[0], positions[0], mod, wb, sp, pipe)
    loss = lax.psum(0.5 * jnp.sum(lossv) / D, ("x", "y", "c"))

    small_names = [n for n in REPLICATED if n != "ada_b"] + list(SMALL_SHARDED)
    payload = [dmod.reshape(2, 6 * D)] + [small[n] for n in small_names]
    pay_shapes = [p.shape for p in payload]
    gathered = _allgather8("ag_grads", _pack(payload))
    totals = _unpack(_sum8("sum_grads", gathered), pay_shapes)
    grads = dict(zip(["ada_b"] + small_names, totals))
    for n in SMALL_SHARDED:
        n_sh = w[n].shape[-1]
        grads[n] = lax.dynamic_slice_in_dim(grads[n], q * n_sh, n_sh, axis=grads[n].ndim - 1)
    dmod_all = take_dev(gathered).reshape(8, -1)[:, :2 * 6 * D].reshape(8, 2, 6 * D)
    dmod_sh = lax.dynamic_slice_in_dim(dmod_all, q * n_ada, n_ada, axis=2).transpose(1, 0, 2)

    pipe.finish("g3", dmod_sh)
    pipe.collect("g4", dmod_sh)
    grads["ada_w"], delta_ada, m_ada, v_ada = _ada_update("ada_update", c_all, dmod_sh, ada_w, m_ada_w, v_ada_w)
    delta, new_m, new_v = dict(ada_w=delta_ada), dict(ada_w=m_ada), dict(ada_w=v_ada)
    rest = list(REPLICATED) + list(SMALL_SHARDED)
    rest_shapes = [w[n].shape for n in rest]
    outs = _adamw("adamw_small", *[_pack([src[n].reshape(w[n].shape) for n in rest]) for src in (w, grads, m, v)])
    for tgt, packed in zip((delta, new_m, new_v), outs):
        tgt.update(dict(zip(rest, _unpack(packed, rest_shapes))))
    for n in rest:
        grads[n] = grads[n].reshape(w[n].shape)
    pipe.finish("g4", outs[0])
    for n, res in pipe.results.items():
        grads[n], delta[n], new_m[n], new_v[n] = res

    return (loss, grad_x[None], *[grads[n] for n in WEIGHTS], *[delta[n] for n in WEIGHTS],
            *[new_m[n] for n in WEIGHTS], *[new_v[n] for n in WEIGHTS])
```

```python
import functools

import jax
import jax.numpy as jnp
from jax import lax
from jax.experimental import pallas as pl
from jax.experimental.pallas import tpu as pltpu

F32, BF16 = jnp.float32, jnp.bfloat16
EPS = 1e-6
NEG = -1e30
ROPE_THETA = 10000.0
LANES = 128
VMEM_LIMIT = 56 * 1024 * 1024
ADAM_LR, ADAM_B1, ADAM_B2, ADAM_EPS, ADAM_WD, ADAM_STEP = 0.001, 0.9, 0.999, 1e-08, 0.01, 10
MESH = pl.DeviceIdType.MESH


def _params(sem):
    return pltpu.CompilerParams(dimension_semantics=sem, vmem_limit_bytes=VMEM_LIMIT)


_DN = {"nn": (((1,), (0,)), ((), ())), "nt": (((1,), (1,)), ((), ())), "tn": (((0,), (0,)), ((), ()))}


def _matmul(name, a, b, mode, out_dtype, tm=512, tn=512, tk=1024, bias=None):
    if mode == "nn":
        (M, K), N = a.shape, b.shape[1]
    elif mode == "nt":
        (M, K), N = a.shape, b.shape[0]
    else:
        (K, M), N = a.shape, b.shape[1]
    tm, tn, tk = min(tm, M), min(tn, N), min(tk, K)
    assert M % tm == 0 and N % tn == 0 and K % tk == 0, (name, M, N, K, tm, tn, tk)
    nk = K // tk
    if mode == "tn":
        a_spec = pl.BlockSpec((tk, tm), lambda i, j, k: (k, i))
    else:
        a_spec = pl.BlockSpec((tm, tk), lambda i, j, k: (i, k))
    if mode == "nt":
        b_spec = pl.BlockSpec((tn, tk), lambda i, j, k: (j, k))
    else:
        b_spec = pl.BlockSpec((tk, tn), lambda i, j, k: (k, j))
    in_specs, args = [a_spec, b_spec], [a, b]
    if bias is not None:
        in_specs.append(pl.BlockSpec((1, tn), lambda i, j, k: (0, j)))
        args.append(bias)
    n_in = len(args)

    def body(*refs):
        a_ref, b_ref, o_ref = refs[0], refs[1], refs[n_in]
        p = lax.dot_general(a_ref[...], b_ref[...], _DN[mode], preferred_element_type=F32)

        def finish(acc):
            if bias is not None:
                acc = acc + refs[2][...]
            o_ref[...] = acc.astype(o_ref.dtype)

        if nk == 1:
            finish(p)
        else:
            acc_ref = refs[n_in + 1]
            k = pl.program_id(2)

            @pl.when(k == 0)
            def _():
                acc_ref[...] = p

            @pl.when(k > 0)
            def _():
                acc_ref[...] += p

            @pl.when(k == nk - 1)
            def _():
                finish(acc_ref[...])

    return pl.pallas_call(
        body, name=name, grid=(M // tm, N // tn, nk), in_specs=in_specs,
        out_specs=pl.BlockSpec((tm, tn), lambda i, j, k: (i, j)),
        out_shape=jax.ShapeDtypeStruct((M, N), out_dtype),
        scratch_shapes=[pltpu.VMEM((tm, tn), F32)] if nk > 1 else [],
        compiler_params=_params(("parallel", "parallel", "arbitrary")),
    )(*args)


def _rowcall(name, body, nrows, tm, ins, outs, scratch=()):
    nblk = nrows // tm
    assert nrows % tm == 0

    def spec(kind, shape):
        k = kind[0]
        if k == "row":
            cw, cb = kind[1] or shape[-1], kind[2]
            return pl.BlockSpec((tm, cw), lambda i: (i, cb))
        if k == "prev":
            hb, cw, cb = kind[1], kind[2] or shape[-1], kind[3]
            r = tm // hb
            return pl.BlockSpec((hb, cw), lambda i: (jnp.maximum(i * r - 1, 0), cb))
        if k == "next":
            hb, cw, cb = kind[1], kind[2] or shape[-1], kind[3]
            r, last = tm // hb, nrows // hb - 1
            return pl.BlockSpec((hb, cw), lambda i: (jnp.minimum((i + 1) * r, last), cb))
        if k == "off":
            off, cw, cb = kind[1], kind[2] or shape[-1], kind[3]
            return pl.BlockSpec((tm, cw), lambda i: (jnp.clip(i + off, 0, nblk - 1), cb))
        if k == "cls":
            dil, off = kind[1], kind[2]
            return pl.BlockSpec((dil, tm // dil, shape[-1]), lambda i: (0, jnp.clip(i + off, 0, nblk - 1), 0))
        nd = len(shape)
        return pl.BlockSpec(tuple(shape), lambda i: (0,) * nd)

    has_acc = any(o[2][0] == "acc" for o in outs)
    return pl.pallas_call(
        body, name=name, grid=(nblk,),
        in_specs=[spec(kind, a.shape) for a, kind in ins],
        out_specs=[spec(kind, shape) for shape, _, kind in outs],
        out_shape=[jax.ShapeDtypeStruct(tuple(shape), dt) for shape, dt, _ in outs],
        scratch_shapes=list(scratch),
        compiler_params=_params(("arbitrary",) if has_acc else ("parallel",)),
    )(*[a for a, _ in ins])


ROW = ("row", None, 0)
FULL = ("full",)
ACC = ("acc",)


def _colsum(x):
    return jnp.sum(x, axis=0, keepdims=True)


def _acc_add(i, ref, val, rows=None):
    idx = (slice(None),) * len(ref.shape) if rows is None else rows

    @pl.when(i == 0)
    def _():
        ref[idx] = val

    @pl.when(i > 0)
    def _():
        ref[idx] = ref[idx] + val


def _sigmoid(x):
    return 1.0 / (1.0 + jnp.exp(-x))


def _gelu(x):
    return 0.5 * x * (1.0 + lax.erf(x * (2.0 ** -0.5)))


def _gelu_grad(x):
    return 0.5 * (1.0 + lax.erf(x * (2.0 ** -0.5))) + x * jnp.exp(-0.5 * x * x) * ((2.0 * jnp.pi) ** -0.5)


SUBLANES = 8


def _phases(ext, sign):
    n = ext.shape[0]
    return [ext if b == 0 else pltpu.roll(ext, b if sign > 0 else n - b, axis=0) for b in range(SUBLANES)]


def _shift_prev(phases, s, hb):
    a, b = divmod(s, SUBLANES)
    return phases[b][hb - SUBLANES * a:phases[b].shape[0] - SUBLANES * a]


def _shift_next(phases, s, tm):
    a, b = divmod(s, SUBLANES)
    return phases[b][SUBLANES * a:SUBLANES * a + tm]


def _rms_mod_val(x, g, sc, sh):
    r = lax.rsqrt(jnp.mean(x * x, axis=-1, keepdims=True) + EPS)
    return x * r * g * (1.0 + sc) + sh


def _mod_first(name, x, g, sc, sh, tm=256):
    S, D = x.shape

    def body(x_ref, g_ref, sc_ref, sh_ref, h_ref):
        h_ref[...] = _rms_mod_val(x_ref[...], g_ref[...], sc_ref[...], sh_ref[...]).astype(BF16)

    return _rowcall(name, body, S, tm, [(x, ROW), (g, FULL), (sc, FULL), (sh, FULL)], [((S, D), BF16, ROW)])[0]


def _resid_mod(name, x, y, gate, g, sc, sh, tm=256):
    S, D = x.shape

    def body(x_ref, y_ref, gate_ref, g_ref, sc_ref, sh_ref, xo_ref, h_ref):
        xn = x_ref[...] + gate_ref[...] * y_ref[...]
        xo_ref[...] = xn
        h_ref[...] = _rms_mod_val(xn, g_ref[...], sc_ref[...], sh_ref[...]).astype(BF16)

    return _rowcall(name, body, S, tm,
                    [(x, ROW), (y, ROW), (gate, FULL), (g, FULL), (sc, FULL), (sh, FULL)],
                    [((S, D), F32, ROW), ((S, D), BF16, ROW)])


def _gate_bwd_val(i, d, y_ref, gate_ref, dy_ref, dg_ref, db_ref):
    dy = d * gate_ref[...]
    dy_ref[...] = dy.astype(BF16)
    _acc_add(i, dg_ref, _colsum(d * y_ref[...]))
    _acc_add(i, db_ref, _colsum(dy))


GATE_OUTS = lambda S, D: [((S, D), BF16, ROW), ((1, D), F32, ACC), ((1, D), F32, ACC)]


def _loss_head(name, x, y, gate, tgt, tm=256):
    S, D = x.shape

    def body(x_ref, y_ref, gate_ref, t_ref, dx_ref, l_ref, dy_ref, dg_ref, db_ref):
        i = pl.program_id(0)
        err = x_ref[...] + gate_ref[...] * y_ref[...] - t_ref[...]
        d = err * (1.0 / D)
        dx_ref[...] = d
        _acc_add(i, l_ref, _colsum(err * err))
        _gate_bwd_val(i, d, y_ref, gate_ref, dy_ref, dg_ref, db_ref)

    return _rowcall(name, body, S, tm, [(x, ROW), (y, ROW), (gate, FULL), (tgt, ROW)],
                    [((S, D), F32, ROW), ((1, D), F32, ACC)] + GATE_OUTS(S, D))


def _mod_bwd(name, dxo, dh, x, g, sc, y=None, gate=None, tm=256):
    S, D = x.shape
    gated = y is not None

    def body(d_ref, dh_ref, x_ref, g_ref, sc_ref, *rest):
        dx_ref, dsh_ref, dsc_ref, dg_ref = rest[2 * gated:2 * gated + 4]
        i = pl.program_id(0)
        xv, dh_v, gv = x_ref[...], dh_ref[...], g_ref[...]
        r = lax.rsqrt(jnp.mean(xv * xv, axis=-1, keepdims=True) + EPS)
        n = xv * r
        _acc_add(i, dsh_ref, _colsum(dh_v))
        _acc_add(i, dsc_ref, _colsum(dh_v * (n * gv)))
        dy = dh_v * (1.0 + sc_ref[...])
        _acc_add(i, dg_ref, _colsum(dy * n))
        dn = dy * gv
        dx = d_ref[...] + r * (dn - n * jnp.mean(dn * n, axis=-1, keepdims=True))
        dx_ref[...] = dx
        if gated:
            _gate_bwd_val(i, dx, rest[0], rest[1], *rest[6:9])

    ins = [(dxo, ROW), (dh, ROW), (x, ROW), (g, FULL), (sc, FULL)] + ([(y, ROW), (gate, FULL)] if gated else [])
    outs = [((S, D), F32, ROW), ((1, D), F32, ACC), ((1, D), F32, ACC), ((1, D), F32, ACC)]
    return _rowcall(name, body, S, tm, ins, outs + (GATE_OUTS(S, D) if gated else []))


HB16 = 16


def _conv3_val(ph, w, b, hb):
    return w[2:3] * _shift_prev(ph, 0, hb) + w[1:2] * _shift_prev(ph, 1, hb) + w[0:1] * _shift_prev(ph, 2, hb) + b


def _halo_first(halo_ref, tile_ref, live):
    return _phases(jnp.concatenate([halo_ref[...].astype(F32) * live, tile_ref[...].astype(F32)], axis=0), 1)


def _glu3_fwd(name, u, w, b, tm=128):
    S, F2 = u.shape
    Fh = F2 // 2

    def body(ua_ref, ub_ref, ha_ref, hb_ref, w_ref, b_ref, o_ref, z_ref):
        live = (pl.program_id(0) > 0).astype(F32)
        wv, bv = w_ref[...], b_ref[...]
        za = _conv3_val(_halo_first(ha_ref, ua_ref, live), wv[:, :Fh], bv[:, :Fh], HB16)
        zb = _conv3_val(_halo_first(hb_ref, ub_ref, live), wv[:, Fh:], bv[:, Fh:], HB16)
        o_ref[...] = (za * _sigmoid(za) * zb).astype(BF16)
        z_ref[:, :Fh] = za.astype(BF16)
        z_ref[:, Fh:] = zb.astype(BF16)

    return _rowcall(name, body, S, tm,
                    [(u, ("row", Fh, 0)), (u, ("row", Fh, 1)), (u, ("prev", HB16, Fh, 0)), (u, ("prev", HB16, Fh, 1)),
                     (w, FULL), (b, FULL)],
                    [((S, Fh), BF16, ROW), ((S, F2), BF16, ROW)])


def _glu3_bwd(name, z, dhm, tm=128):
    S, F2 = z.shape
    Fh = F2 // 2

    def body(za_ref, zb_ref, d_ref, dz_ref, db_ref):
        i = pl.program_id(0)
        za, zb, d = za_ref[...].astype(F32), zb_ref[...].astype(F32), d_ref[...]
        sg = _sigmoid(za)
        da = d * zb * (sg * (1.0 + za * (1.0 - sg)))
        db = d * (za * sg)
        dz_ref[:, :Fh] = da.astype(BF16)
        dz_ref[:, Fh:] = db.astype(BF16)
        _acc_add(i, db_ref, jnp.concatenate([_colsum(da), _colsum(db)], axis=1))

    return _rowcall(name, body, S, tm, [(z, ("row", Fh, 0)), (z, ("row", Fh, 1)), (dhm, ROW)],
                    [((S, F2), BF16, ROW), ((1, F2), F32, ACC)])


def _conv3_bwd(name, dz, u, w, tm=128):
    S, F2 = dz.shape
    nblk = S // tm
    K = w.shape[0]

    def body(d_ref, n_ref, u_ref, w_ref, o_ref, dw_ref):
        i = pl.program_id(0)
        live = (i < nblk - 1).astype(F32)
        ph = _phases(jnp.concatenate([d_ref[...].astype(F32), n_ref[...].astype(F32) * live], axis=0), -1)
        wv, uv = w_ref[...], u_ref[...].astype(F32)
        shifted = [_shift_next(ph, K - 1 - k, tm) for k in range(K)]
        o_ref[...] = functools.reduce(lambda a, t: a + t, [wv[k:k + 1] * shifted[k] for k in range(K)]).astype(BF16)
        for k in range(K):
            _acc_add(i, dw_ref, _colsum(uv * shifted[k]), rows=(slice(k, k + 1), slice(None)))

        @pl.when(i == 0)
        def _():
            dw_ref[K:, :] = jnp.zeros((dw_ref.shape[0] - K, F2), F32)

    return _rowcall(name, body, S, tm, [(dz, ROW), (dz, ("next", HB16, None, 0)), (u, ROW), (w, FULL)],
                    [((S, F2), BF16, ROW), ((SUBLANES, F2), F32, ACC)])


def _ffn_fwd(name, h, w_up, w_dn, dw_w, dw_b):
    u = _matmul(f"{name}_up", h, w_up, "nn", BF16, tm=2048)
    hm, z = _glu3_fwd(f"{name}_glu", u, dw_w, dw_b)
    f = _matmul(f"{name}_dn", hm, w_dn, "nn", F32, tm=1024, tn=1024, tk=w_dn.shape[0])
    return f, (u, hm, z)


def _ffn_bwd(name, dy, h, u, hm, z, w_up, w_dn, dw_w, dw_b):
    Fh = w_dn.shape[0]
    dhm = _matmul(f"{name}_ddn_x", dy, w_dn, "nt", F32, tm=1024, tn=Fh // 2)
    g_dn = _matmul(f"{name}_ddn_w", hm, dy, "tn", BF16, tm=Fh // 2, tn=1024, tk=2048)
    dz, g_dw_b = _glu3_bwd(f"{name}_dglu", z, dhm)
    du, taps = _conv3_bwd(f"{name}_dconv", dz, u, dw_w)
    g_up = _matmul(f"{name}_dup_w", h, du, "tn", BF16, tm=1024, tn=Fh // 2, tk=2048)
    dh = _matmul(f"{name}_dup_x", du, w_up, "nt", F32, tm=1024, tn=1024, tk=Fh)
    return dh, dict(up=g_up, dn=g_dn, dw_w=taps[0:dw_w.shape[0]], dw_b=g_dw_b)


HB32 = 32


def _glu31_fwd(name, p, w, b, tm=256):
    S, D2 = p.shape
    D = D2 // 2
    K = w.shape[0]

    def body(a_ref, g_ref, ha_ref, hg_ref, w_ref, b_ref, o_ref):
        live = (pl.program_id(0) > 0).astype(F32)
        y1 = a_ref[...] * _sigmoid(g_ref[...])
        ph = _phases(jnp.concatenate([ha_ref[...] * _sigmoid(hg_ref[...]) * live, y1], axis=0), 1)
        wv = w_ref[...]
        acc = b_ref[...] + wv[K - 1:K] * y1
        for k in range(K - 1):
            acc = acc + wv[k:k + 1] * _shift_prev(ph, K - 1 - k, HB32)
        o_ref[...] = acc

    return _rowcall(name, body, S, tm,
                    [(p, ("row", D, 0)), (p, ("row", D, 1)), (p, ("prev", HB32, D, 0)), (p, ("prev", HB32, D, 1)),
                     (w, FULL), (b, FULL)],
                    [((S, D), F32, ROW)])[0]


def _ln_silu_fwd(name, y2, g, b, tm=256):
    S, D = y2.shape

    def body(y_ref, g_ref, b_ref, o_ref):
        y = y_ref[...]
        mu = jnp.mean(y, axis=-1, keepdims=True)
        yc = y - mu
        rs = lax.rsqrt(jnp.mean(yc * yc, axis=-1, keepdims=True) + EPS)
        y3 = yc * rs * g_ref[...] + b_ref[...]
        o_ref[...] = (y3 * _sigmoid(y3)).astype(BF16)

    return _rowcall(name, body, S, tm, [(y2, ROW), (g, FULL), (b, FULL)], [((S, D), BF16, ROW)])[0]


def _ln_silu_bwd(name, y2, dy4, g, b, tm=256):
    S, D = y2.shape

    def body(y_ref, d_ref, g_ref, b_ref, o_ref, dg_ref, db_ref):
        i = pl.program_id(0)
        y, gv = y_ref[...], g_ref[...]
        mu = jnp.mean(y, axis=-1, keepdims=True)
        yc = y - mu
        rs = lax.rsqrt(jnp.mean(yc * yc, axis=-1, keepdims=True) + EPS)
        n = yc * rs
        y3 = n * gv + b_ref[...]
        sg = _sigmoid(y3)
        dy3 = d_ref[...] * (sg * (1.0 + y3 * (1.0 - sg)))
        _acc_add(i, db_ref, _colsum(dy3))
        _acc_add(i, dg_ref, _colsum(dy3 * n))
        dn = dy3 * gv
        o_ref[...] = rs * (dn - jnp.mean(dn, axis=-1, keepdims=True) - n * jnp.mean(dn * n, axis=-1, keepdims=True))

    return _rowcall(name, body, S, tm, [(y2, ROW), (dy4, ROW), (g, FULL), (b, FULL)],
                    [((S, D), F32, ROW), ((1, D), F32, ACC), ((1, D), F32, ACC)])


def _glu31_bwd(name, p, dy2, w, tm=256):
    S, D2 = p.shape
    D = D2 // 2
    K = w.shape[0]
    nblk = S // tm

    conv_rows, tap_rows, tap_group, tap_unroll = 16, SUBLANES, 4, 4

    def body(a_ref, g_ref, d_ref, dn_ref, w_ref, dp_ref, dw_ref, dcb_ref, dpb_ref, ph_d, y1_ref, dy1_ref, wb_ref):
        i = pl.program_id(0)
        live_next = (i < nblk - 1).astype(F32)
        a, sg, d = a_ref[...], _sigmoid(g_ref[...]), d_ref[...]
        y1_ref[...] = a * sg
        for b, ph in enumerate(_phases(jnp.concatenate([d, dn_ref[...] * live_next], axis=0), -1)):
            ph_d[b] = ph
        taps = [divmod(K - 1 - k, SUBLANES) for k in range(K)]

        @pl.when(i == 0)
        def _():
            for k in range(K):
                wb_ref[k] = jnp.broadcast_to(w_ref[k:k + 1, :], (SUBLANES, D))

        def conv_rows_at(rb, carry):
            r0 = pl.multiple_of(rb * conv_rows, conv_rows)
            accs = [jnp.zeros((SUBLANES, D), F32) for _ in range(conv_rows // SUBLANES)]
            for k, (rows8, phase) in enumerate(taps):
                wk = wb_ref[k]
                for u in range(len(accs)):
                    accs[u] = accs[u] + wk * ph_d[phase, pl.ds(r0 + SUBLANES * (rows8 + u), SUBLANES), :]
            for u, acc in enumerate(accs):
                dy1_ref[pl.ds(r0 + SUBLANES * u, SUBLANES), :] = acc
            return carry

        lax.fori_loop(0, tm // conv_rows, conv_rows_at, 0)
        for k0 in range(0, K, tap_group):
            group = taps[k0:k0 + tap_group]

            def tap_rows_at(rb, accs, group=group):
                for u in range(tap_unroll):
                    r0 = pl.multiple_of((rb * tap_unroll + u) * tap_rows, tap_rows)
                    yv = y1_ref[pl.ds(r0, tap_rows), :]
                    accs = tuple(acc + yv * ph_d[phase, pl.ds(r0 + SUBLANES * rows8, tap_rows), :]
                                 for acc, (rows8, phase) in zip(accs, group))
                return accs

            accs = lax.fori_loop(0, tm // (tap_rows * tap_unroll), tap_rows_at,
                                 tuple(jnp.zeros((tap_rows, D), F32) for _ in group))
            for j, acc in enumerate(accs):
                _acc_add(i, dw_ref, _colsum(acc), rows=(slice(k0 + j, k0 + j + 1), slice(None)))

        @pl.when(i == 0)
        def _():
            dw_ref[K:, :] = jnp.zeros((dw_ref.shape[0] - K, D), F32)

        _acc_add(i, dcb_ref, _colsum(d))
        dy1 = dy1_ref[...]
        da = dy1 * sg
        dg = dy1 * a * sg * (1.0 - sg)
        dp_ref[:, :D] = da.astype(BF16)
        dp_ref[:, D:] = dg.astype(BF16)
        _acc_add(i, dpb_ref, jnp.concatenate([_colsum(da), _colsum(dg)], axis=1))

    return _rowcall(name, body, S, tm,
                    [(p, ("row", D, 0)), (p, ("row", D, 1)), (dy2, ROW), (dy2, ("next", HB32, None, 0)), (w, FULL)],
                    [((S, D2), BF16, ROW), ((HB32, D), F32, ACC), ((1, D), F32, ACC), ((1, D2), F32, ACC)],
                    scratch=[pltpu.VMEM((SUBLANES, tm + HB32, D), F32), pltpu.VMEM((tm, D), F32),
                             pltpu.VMEM((tm, D), F32), pltpu.VMEM((K, SUBLANES, D), F32)])


CHUNK = 128
A_GROUPS = 4


def _group_ln(gv):
    ns, rss = [], []
    for g in range(A_GROUPS):
        xg = gv[:, g * LANES:(g + 1) * LANES]
        xc = xg - jnp.mean(xg, axis=-1, keepdims=True)
        rs = lax.rsqrt(jnp.mean(xc * xc, axis=-1, keepdims=True) + EPS)
        ns.append(xc * rs)
        rss.append(jnp.broadcast_to(rs, xg.shape))
    return jnp.concatenate(ns, axis=1), jnp.concatenate(rss, axis=1)


def _tril_mask():
    r = lax.broadcasted_iota(jnp.int32, (CHUNK, CHUNK), 0)
    c = lax.broadcasted_iota(jnp.int32, (CHUNK, CHUNK), 1)
    return r >= c


def _spatial(ws_ref, x, dn):
    mask = _tril_mask()
    rows = []
    for ci in range(x.shape[0] // CHUNK):
        cols = []
        for g in range(A_GROUPS):
            wm = jnp.where(mask, ws_ref[g], 0.0).astype(BF16)
            xb = x[ci * CHUNK:(ci + 1) * CHUNK, g * LANES:(g + 1) * LANES]
            cols.append(lax.dot_general(wm, xb, dn, preferred_element_type=F32))
        rows.append(jnp.concatenate(cols, axis=1))
    return jnp.concatenate(rows, axis=0)


def _mixa_fwd(name, z, vg, ws, bias_full, tm=256):
    S = z.shape[0]
    W = A_GROUPS * LANES

    def body(u_ref, v_ref, vg_ref, ws_ref, b_ref, o_ref):
        nh, _ = _group_ln(_gelu(v_ref[...]))
        vn = (nh * vg_ref[...]).astype(BF16)
        f = _spatial(ws_ref, vn, _DN["nn"]) + jnp.concatenate([b_ref[...]] * (tm // CHUNK), axis=0)
        o_ref[...] = (_gelu(u_ref[...]) * f).astype(BF16)

    return _rowcall(name, body, S, tm,
                    [(z, ("row", W, 0)), (z, ("row", W, 1)), (vg, FULL), (ws, FULL), (bias_full, FULL)],
                    [((S, W), BF16, ROW)])[0]


def _mixa_bwd(name, z, dyab, vg, ws, bias_full, tm=256):
    S = z.shape[0]
    W = A_GROUPS * LANES
    nch = tm // CHUNK

    def body(u_ref, v_ref, d_ref, vg_ref, ws_ref, b_ref, dz_ref, dws_ref, dbf_ref, dvg_ref):
        i = pl.program_id(0)
        u, v, d, vgv = u_ref[...], v_ref[...], d_ref[...], vg_ref[...]
        nh, rs = _group_ln(_gelu(v))
        vn = (nh * vgv).astype(BF16)
        f = _spatial(ws_ref, vn, _DN["nn"]) + jnp.concatenate([b_ref[...]] * nch, axis=0)
        dz_ref[:, :W] = (d * f * _gelu_grad(u)).astype(BF16)
        df = d * _gelu(u)
        dbf = df[0:CHUNK]
        for ci in range(1, nch):
            dbf = dbf + df[ci * CHUNK:(ci + 1) * CHUNK]
        _acc_add(i, dbf_ref, dbf)
        dfb = df.astype(BF16)
        mask = _tril_mask()
        for g in range(A_GROUPS):
            acc = jnp.zeros((CHUNK, CHUNK), F32)
            for ci in range(nch):
                blk = (slice(ci * CHUNK, (ci + 1) * CHUNK), slice(g * LANES, (g + 1) * LANES))
                acc = acc + lax.dot_general(dfb[blk], vn[blk], _DN["nt"], preferred_element_type=F32)
            _acc_add(i, dws_ref, jnp.where(mask, acc, 0.0)[None], rows=(slice(g, g + 1), slice(None), slice(None)))
        dvn = _spatial(ws_ref, dfb, _DN["tn"])
        _acc_add(i, dvg_ref, _colsum(dvn * nh))
        dnh = dvn * vgv
        parts = []
        for g in range(A_GROUPS):
            cs = slice(g * LANES, (g + 1) * LANES)
            dg_, ng = dnh[:, cs], nh[:, cs]
            parts.append(dg_ - jnp.mean(dg_, axis=-1, keepdims=True) - ng * jnp.mean(dg_ * ng, axis=-1, keepdims=True))
        dz_ref[:, W:] = (rs * jnp.concatenate(parts, axis=1) * _gelu_grad(v)).astype(BF16)

    return _rowcall(name, body, S, tm,
                    [(z, ("row", W, 0)), (z, ("row", W, 1)), (dyab, ("row", W, 0)), (vg, FULL), (ws, FULL),
                     (bias_full, FULL)],
                    [((S, 2 * W), BF16, ROW), ((A_GROUPS, CHUNK, CHUNK), F32, ACC), ((CHUNK, W), F32, ACC),
                     ((1, W), F32, ACC)])


HEAD = 64
N_HEADS = 8
BW = HEAD * N_HEADS
QB = 128
DILATIONS = (1, 4, 16)
QK_SCALE = HEAD ** -0.5


def _gsum64(x, ones_bd):
    x1 = x.astype(BF16)
    r1 = x - x1.astype(F32)
    x2 = r1.astype(BF16)
    x3 = (r1 - x2.astype(F32)).astype(BF16)
    dot = lambda t: jnp.dot(t, ones_bd, preferred_element_type=F32)
    return dot(x1) + dot(x2) + dot(x3)


def _swap32(x):
    n = x.shape[-1]
    up = pltpu.roll(x, n - HEAD // 2, axis=1)
    dn = pltpu.roll(x, HEAD // 2, axis=1)
    lane = lax.broadcasted_iota(jnp.int32, x.shape, 1)
    return jnp.where((lane % HEAD) < HEAD // 2, up, dn)


def _tile4(t):
    return jnp.concatenate([t] * (BW // LANES), axis=1)


def _stage_spec(tm):
    return pltpu.VMEM((BW // LANES, tm, LANES), F32)


def _to_classes(stage, x, dil):
    tm = x.shape[0]
    for j in range(BW // LANES):
        stage[j] = x[:, j * LANES:(j + 1) * LANES]
    return [jnp.concatenate([stage.at[j][pl.ds(r, tm // dil, stride=dil), :] for j in range(BW // LANES)], axis=1)
            for r in range(dil)]


def _from_classes(stage, cls, dil):
    rows = cls.shape[1]
    for r in range(dil):
        for j in range(BW // LANES):
            stage.at[j][pl.ds(r, rows, stride=dil), :] = cls[r, :, j * LANES:(j + 1) * LANES]
    return jnp.concatenate([stage[j] for j in range(BW // LANES)], axis=1)


def _cls_view(t, dil):
    return t if dil == 1 else t.reshape(dil, t.shape[0] // dil, t.shape[1])


def _cls_kind(dil, off=0):
    return ("off", off, None, 0) if dil == 1 else ("cls", dil, off)


def _cls_out(S, dil, dtype):
    return ((S, BW) if dil == 1 else (dil, S // dil, BW), dtype, _cls_kind(dil))


def _flat(t):
    return t.reshape(-1, t.shape[-1])


def _qkv_fwd(name, z, cos, sin, ones_bd, qg, kg, tm=256):
    S = z.shape[0]
    nd = len(DILATIONS)

    def body(q_ref, k_ref, v_ref, c_ref, s_ref, o_ref, qg_ref, kg_ref, *rest):
        outs, stage = rest[:3 * nd], rest[3 * nd]
        c, s, ob = _tile4(c_ref[...]), _tile4(s_ref[...]), o_ref[...]

        def norm_rope(x, g):
            r = lax.rsqrt(_gsum64(x * x, ob) * (1.0 / HEAD) + EPS)
            xn = x * r * g
            return xn * c + _swap32(xn) * s

        vals = [norm_rope(q_ref[...], qg_ref[...]) * QK_SCALE, norm_rope(k_ref[...], kg_ref[...]), v_ref[...]]
        for a, val in enumerate(vals):
            for b, dil in enumerate(DILATIONS):
                if dil == 1:
                    outs[nd * a + b][...] = val.astype(BF16)
                else:
                    for r, rows in enumerate(_to_classes(stage, val, dil)):
                        outs[nd * a + b][r] = rows.astype(BF16)

    outs = _rowcall(name, body, S, tm,
                    [(z, ("row", BW, 2)), (z, ("row", BW, 3)), (z, ("row", BW, 4)), (cos, ROW), (sin, ROW),
                     (ones_bd, FULL), (qg, FULL), (kg, FULL)],
                    [_cls_out(S, dil, BF16) for _ in range(3) for dil in DILATIONS], scratch=[_stage_spec(tm)])
    return [[_flat(outs[nd * a + b]) for a in range(3)] for b in range(nd)]


PAIR = 2 * HEAD


ATT_BLOCKS = 2
ATT_TM = ATT_BLOCKS * QB
ATT_PREV = ("prev", QB, None, 0)


def _key_rows(prev_ref, cur_ref, sb, ps):
    before = prev_ref[:, ps] if sb == 0 else cur_ref[(sb - 1) * QB:sb * QB, ps]
    return jnp.concatenate([before, cur_ref[sb * QB:(sb + 1) * QB, ps]], axis=0)


def _pair_scores(q_ref, kp_ref, kc_ref, sb, hp, half, seg_blocks):
    ps = slice(hp * PAIR, (hp + 1) * PAIR)
    mine = (lax.broadcasted_iota(jnp.int32, (1, PAIR), 1) >= HEAD) == (half == 1)
    qm = jnp.where(mine, q_ref[sb * QB:(sb + 1) * QB, ps], jnp.zeros((), BF16))
    kcat = _key_rows(kp_ref, kc_ref, sb, ps)
    s = lax.dot_general(qm, kcat, _DN["nt"], preferred_element_type=F32)
    qi = lax.broadcasted_iota(jnp.int32, (QB, 2 * QB), 0)
    kj = lax.broadcasted_iota(jnp.int32, (QB, 2 * QB), 1)
    has_prev = ((pl.program_id(0) * ATT_BLOCKS + sb) % seg_blocks) != 0
    valid = (kj >= qi) & (kj <= qi + QB) & ((kj >= QB) | has_prev)
    return mine, qm, kcat, s, valid


def _attn_fwd(name, q, k, v, dil):
    S = q.shape[0]
    seg_blocks = S // dil // QB

    def body(q_ref, kp_ref, kc_ref, vp_ref, vc_ref, o_ref, l_ref):
        for sb in range(ATT_BLOCKS):
            rows = slice(sb * QB, (sb + 1) * QB)
            for hp in range(N_HEADS // 2):
                ps = slice(hp * PAIR, (hp + 1) * PAIR)
                vcat = _key_rows(vp_ref, vc_ref, sb, ps)
                o_pair = l_pair = None
                for half in range(2):
                    mine, _, _, s, valid = _pair_scores(q_ref, kp_ref, kc_ref, sb, hp, half, seg_blocks)
                    s = jnp.where(valid, s, NEG)
                    m = jnp.max(s, axis=-1, keepdims=True)
                    p = jnp.exp(s - m)
                    den = jnp.sum(p, axis=-1, keepdims=True)
                    o = jnp.dot(p.astype(BF16), vcat, preferred_element_type=F32) / den
                    lse = jnp.broadcast_to(m + jnp.log(den), (QB, PAIR))
                    o_pair, l_pair = (o, lse) if half == 0 else (jnp.where(mine, o, o_pair), jnp.where(mine, lse, l_pair))
                o_ref[rows, ps] = o_pair
                l_ref[rows, ps] = l_pair

    return _rowcall(name, body, S, ATT_TM, [(q, ROW), (k, ATT_PREV), (k, ROW), (v, ATT_PREV), (v, ROW)],
                    [((S, BW), F32, ROW)] * 2)


def _attn_bwd(name, q, k, v, do, lse, delta, dil):
    S = q.shape[0]
    seg_blocks = S // dil // QB

    def body(q_ref, kp_ref, kc_ref, vp_ref, vc_ref, do_ref, l_ref, dl_ref, dq_ref, dkc_ref, dkp_ref, dvc_ref, dvp_ref):
        for sb in range(ATT_BLOCKS):
            rows = slice(sb * QB, (sb + 1) * QB)
            for hp in range(N_HEADS // 2):
                ps = slice(hp * PAIR, (hp + 1) * PAIR)
                vcat = _key_rows(vp_ref, vc_ref, sb, ps)
                do_p = do_ref[rows, ps].astype(BF16)
                dq = dk = dv = None
                for half in range(2):
                    mine, qm, kcat, s, valid = _pair_scores(q_ref, kp_ref, kc_ref, sb, hp, half, seg_blocks)
                    col = hp * PAIR + half * HEAD
                    p = jnp.where(valid, jnp.exp(s - l_ref[rows, col:col + 1]), 0.0)
                    dom = jnp.where(mine, do_p, jnp.zeros((), BF16))
                    dp = lax.dot_general(dom, vcat, _DN["nt"], preferred_element_type=F32)
                    ds = (p * (dp - dl_ref[rows, col:col + 1])).astype(BF16)
                    dq_h = jnp.dot(ds, kcat, preferred_element_type=F32)
                    dk_h = lax.dot_general(ds, qm, _DN["tn"], preferred_element_type=F32)
                    dv_h = lax.dot_general(p.astype(BF16), dom, _DN["tn"], preferred_element_type=F32)
                    dq, dk, dv = (dq_h, dk_h, dv_h) if half == 0 else (jnp.where(mine, dq_h, dq), dk + dk_h, dv + dv_h)
                dq_ref[rows, ps] = dq
                dkp_ref[rows, ps] = dk[:QB]
                dkc_ref[rows, ps] = dk[QB:]
                dvp_ref[rows, ps] = dv[:QB]
                dvc_ref[rows, ps] = dv[QB:]

    return _rowcall(name, body, S, ATT_TM,
                    [(q, ROW), (k, ATT_PREV), (k, ROW), (v, ATT_PREV), (v, ROW), do, (lse, ROW), (delta, ROW)],
                    [((S, BW), F32, ROW)] * 5)


def _merge_fwd(name, branches, tm=256):
    S = branches[0][0].shape[0]
    nd = len(DILATIONS)

    def body(*refs):
        ins, (y_ref, yb_ref), l_refs, stage = refs[:2 * nd], refs[2 * nd:2 * nd + 2], refs[2 * nd + 2:3 * nd + 2], refs[-1]
        os_, ls = [], []
        for b, dil in enumerate(DILATIONS):
            o, l = ins[2 * b][...], ins[2 * b + 1][...]
            os_.append(o if dil == 1 else _from_classes(stage, o, dil))
            ls.append(l if dil == 1 else _from_classes(stage, l, dil))
        m = functools.reduce(jnp.maximum, ls)
        es = [jnp.exp(l - m) for l in ls]
        den = functools.reduce(lambda a, e: a + e, es)
        y = functools.reduce(lambda a, t: a + t, [e * o for e, o in zip(es, os_)]) / den
        y_ref[...] = y
        yb_ref[...] = y.astype(BF16)
        lse = m + jnp.log(den)
        for b, dil in enumerate(DILATIONS):
            if dil == 1:
                l_refs[b][...] = lse
            else:
                for r, rows in enumerate(_to_classes(stage, lse, dil)):
                    l_refs[b][r] = rows

    ins = [(_cls_view(t, dil), _cls_kind(dil)) for pair, dil in zip(branches, DILATIONS) for t in pair]
    outs = _rowcall(name, body, S, tm, ins,
                    [((S, BW), F32, ROW), ((S, BW), BF16, ROW)] + [_cls_out(S, dil, F32) for dil in DILATIONS],
                    scratch=[_stage_spec(tm)])
    return outs[0], outs[1], [_flat(t) for t in outs[2:]]


def _delta(name, dyab, yb, ones_bd, tm=256):
    S = yb.shape[0]
    nd = len(DILATIONS)

    def body(d_ref, y_ref, o_ref, *rest):
        dl_refs, do_refs, stage = rest[:nd], rest[nd:2 * nd - 1], rest[-1]
        d = d_ref[...]
        dl = _gsum64(d * y_ref[...], o_ref[...])
        for b, dil in enumerate(DILATIONS):
            if dil == 1:
                dl_refs[b][...] = dl
            else:
                for r, rows in enumerate(_to_classes(stage, dl, dil)):
                    dl_refs[b][r] = rows
                for r, rows in enumerate(_to_classes(stage, d, dil)):
                    do_refs[b - 1][r] = rows.astype(BF16)

    outs = _rowcall(name, body, S, tm, [(dyab, ("row", BW, 1)), (yb, ROW), (ones_bd, FULL)],
                    [_cls_out(S, dil, F32) for dil in DILATIONS] + [_cls_out(S, dil, BF16) for dil in DILATIONS[1:]],
                    scratch=[_stage_spec(tm)])
    return [_flat(t) for t in outs[:nd]], [_flat(t) for t in outs[nd:]]


def _qkv_bwd(name, z, cos, sin, ones_bd, qg, kg, pieces):
    S = z.shape[0]
    nblk = S // QB

    def body(q_ref, k_ref, c_ref, s_ref, o_ref, qg_ref, kg_ref, *rest):
        pr, (dz_ref, dqg_ref, dkg_ref), stage = rest[:15], rest[15:18], rest[18]
        i = pl.program_id(0)
        c, s, ob = _tile4(c_ref[...]), _tile4(s_ref[...]), o_ref[...]
        dq = dk = dv = None
        for b, dil in enumerate(DILATIONS):
            a_q, a_kc, a_kp, a_vc, a_vp = [r[...] for r in pr[5 * b:5 * b + 5]]
            live = ((i + dil) < nblk).astype(F32)
            tq, tk, tv = a_q, a_kc + a_kp * live, a_vc + a_vp * live
            if dil > 1:
                tq, tk, tv = (_from_classes(stage, t, dil) for t in (tq, tk, tv))
            dq, dk, dv = (tq, tk, tv) if b == 0 else (dq + tq, dk + tk, dv + tv)

        def back(x, g, d_rot, acc_ref):
            r = lax.rsqrt(_gsum64(x * x, ob) * (1.0 / HEAD) + EPS)
            n = x * r
            dxn = d_rot * c + _swap32(d_rot * s)
            _acc_add(i, acc_ref, _colsum(dxn * n))
            dn = dxn * g
            return r * (dn - n * (_gsum64(dn * n, ob) * (1.0 / HEAD)))

        dz_ref[:, :BW] = back(q_ref[...], qg_ref[...], dq * QK_SCALE, dqg_ref).astype(BF16)
        dz_ref[:, BW:2 * BW] = back(k_ref[...], kg_ref[...], dk, dkg_ref).astype(BF16)
        dz_ref[:, 2 * BW:] = dv.astype(BF16)

    ins = [(z, ("row", BW, 2)), (z, ("row", BW, 3)), (cos, ROW), (sin, ROW), (ones_bd, FULL), (qg, FULL), (kg, FULL)]
    for piece, dil in zip(pieces, DILATIONS):
        a_q, a_kc, a_kp, a_vc, a_vp = (_cls_view(t, dil) for t in piece)
        own, prev = _cls_kind(dil), _cls_kind(dil, dil)
        ins += [(a_q, own), (a_kc, own), (a_kp, prev), (a_vc, own), (a_vp, prev)]
    return _rowcall(name, body, S, QB, ins,
                    [((S, 3 * BW), BF16, ROW), ((1, BW), F32, ACC), ((1, BW), F32, ACC)], scratch=[_stage_spec(QB)])


def _local_step(x0, tgt, pos, mod, wb, sp, pipe):
    S, D = x0.shape
    md = lambda l, j: mod[l, j:j + 1]
    sh_m, sc_m, g_m, sh_f, sc_f, g_f = ([md(l, j) for l in range(2)] for j in range(6))
    nm_g, nf_g = sp["norm_mix_g"], sp["norm_ffn_g"]

    inv_freq = 1.0 / (ROPE_THETA ** (jnp.arange(0, HEAD, 2, dtype=F32) / HEAD))
    ang = pos.astype(F32)[:, None] * inv_freq
    cs, sn = jnp.cos(ang), jnp.sin(ang)
    cos = jnp.concatenate([cs, cs, cs, cs], axis=1)
    sin = jnp.concatenate([-sn, sn, -sn, sn], axis=1)
    head_of = jnp.arange(BW) // HEAD
    ones_bd = (head_of[:, None] == head_of[None, :]).astype(BF16)
    qg = jnp.tile(sp["b_q_norm_g"].reshape(1, HEAD), (1, N_HEADS))
    kg = jnp.tile(sp["b_k_norm_g"].reshape(1, HEAD), (1, N_HEADS))
    vg = sp["a_vnorm_g"].reshape(1, A_GROUPS * LANES)
    ws = sp["a_spatial_w"][0]
    bias_full = jnp.repeat(sp["a_spatial_b"][0].T, LANES, axis=1)
    ffn_s = [(sp["ffn_dw_w"][l], sp["ffn_dw_b"][l:l + 1]) for l in range(2)]

    h0 = _mod_first("l0_mod", x0, nm_g[0:1], sc_m[0], sh_m[0])
    z = _matmul("l0_in", h0, wb.get("w_in", h0), "nn", F32, tm=2048)
    ya = _mixa_fwd("l0_mixa", z, vg, ws, bias_full)
    qkv = _qkv_fwd("l0_qkv", z, cos, sin, ones_bd, qg, kg)
    branches = [_attn_fwd(f"l0_att{dil}", *qkv[b], dil) for b, dil in enumerate(DILATIONS)]
    yb, yb16, lses = _merge_fwd("l0_merge", branches)
    yab = jnp.concatenate([ya, yb16], axis=1)
    y0 = _matmul("l0_out", yab, wb.get("w_out", yab), "nn", F32, tm=1024, tn=1024)
    x1, h1 = _resid_mod("l0_res1", x0, y0, g_m[0], nf_g[0:1], sc_f[0], sh_f[0])
    ffn_w = [(wb.get("up0", h1), wb.get("dn0", h1), *ffn_s[0])]
    f0, saved0 = _ffn_fwd("l0_ffn", h1, *ffn_w[0])
    x2, h2 = _resid_mod("l0_res2", x1, f0, g_f[0], nm_g[1:2], sc_m[1], sh_m[1])
    p = _matmul("l1_pw1", h2, wb.get("pw1", h2), "nn", F32, tm=2048, bias=sp["conv_pw1_b"])
    y2 = _glu31_fwd("l1_glu", p, sp["conv_dw_w"][0], sp["conv_dw_b"])
    y4 = _ln_silu_fwd("l1_ln", y2, sp["conv_ln_g"], sp["conv_ln_b"])
    y1 = _matmul("l1_pw2", y4, wb.get("pw2", y4), "nn", F32, tm=1024, tn=1024, bias=sp["conv_pw2_b"])
    x3, h3 = _resid_mod("l1_res1", x2, y1, g_m[1], nf_g[1:2], sc_f[1], sh_f[1])
    ffn_w.append((wb.get("up1", h3), wb.get("dn1", h3), *ffn_s[1]))
    f1, saved1 = _ffn_fwd("l1_ffn", h3, *ffn_w[1])
    dx4, lossv, dy, dgate_f1, _ = _loss_head("loss", x3, f1, g_f[1], tgt)

    dh, gf1 = _ffn_bwd("l1_ffn", dy, h3, *saved1, *ffn_w[1])
    tok = pipe.scatter("g1", dict(dn1=gf1["dn"], up1=gf1["up"]))
    dx3, dsh_f1, dsc_f1, dnf1, dy, dgate_m1, dpw2_b = _mod_bwd("l1_dmod2", dx4, dh, x3, nf_g[1:2], sc_f[1] + tok,
                                                             y1, g_m[1])
    dy4 = _matmul("l1_dpw2_x", dy, wb.get("pw2"), "nt", F32, tm=1024, tn=1024)
    g_pw2 = _matmul("l1_dpw2_w", y4, dy, "tn", BF16, tm=1024, tn=1024, tk=2048)
    dy2, dln_g, dln_b = _ln_silu_bwd("l1_dln", y2, dy4, sp["conv_ln_g"], sp["conv_ln_b"])
    dp, ddw_w, ddw_b, dpw1_b = _glu31_bwd("l1_dglu", p, dy2, sp["conv_dw_w"][0])
    g_pw1 = _matmul("l1_dpw1_w", h2, dp, "tn", BF16, tm=1024, tn=1024, tk=2048)
    tok = pipe.scatter("g2", dict(pw2=g_pw2, pw1=g_pw1))
    pipe.collect("g1", g_pw1)
    dh = _matmul("l1_dpw1_x", dp, wb.get("pw1"), "nt", F32, tm=1024, tn=1024, tk=2048)
    dx2, dsh_m1, dsc_m1, dnm1, dy, dgate_f0, _ = _mod_bwd("l1_dmod1", dx3, dh, x2, nm_g[1:2], sc_m[1] + tok,
                                                        f0, g_f[0])
    dh, gf0 = _ffn_bwd("l0_ffn", dy, h1, *saved0, *ffn_w[0])
    pipe.finish("g1", gf0["up"])
    pipe.collect("g2", gf0["up"])
    dx1, dsh_f0, dsc_f0, dnf0, dy, dgate_m0, _ = _mod_bwd("l0_dmod2", dx2, dh, x1, nf_g[0:1], sc_f[0], y0, g_m[0])
    dyab = _matmul("l0_dout_x", dy, wb.get("w_out"), "nt", F32, tm=1024, tn=1024)
    g_out = _matmul("l0_dout_w", yab, dy, "tn", BF16, tm=1024, tn=1024, tk=2048)
    tok = pipe.scatter("g3", dict(dn0=gf0["dn"], up0=gf0["up"], w_out=g_out))
    vg = vg + tok
    dza, dws, dbf, dvg = _mixa_bwd("l0_dmixa", z, dyab, vg, ws, bias_full)
    deltas, dos = _delta("l0_delta", dyab, yb, ones_bd)
    pieces = []
    for b, dil in enumerate(DILATIONS):
        do = (dyab, ("row", BW, 1)) if dil == 1 else (dos[b - 1], ROW)
        pieces.append(_attn_bwd(f"l0_datt{dil}", *qkv[b], do, lses[b], deltas[b], dil))
    dzb, dqg, dkg = _qkv_bwd("l0_dqkv", z, cos, sin, ones_bd, qg, kg, pieces)
    dz = jnp.concatenate([dza, dzb], axis=1)
    g_in = _matmul("l0_din_w", h0, dz, "tn", BF16, tm=1024, tn=1280, tk=2048)
    tok = pipe.scatter("g4", dict(w_in=g_in))
    pipe.finish("g2", g_in)
    pipe.collect("g3", g_in)
    dh = _matmul("l0_din_x", dz, wb.get("w_in"), "nt", F32, tm=1024, tn=1024, tk=2560)
    grad_x, dsh_m0, dsc_m0, dnm0 = _mod_bwd("l0_dmod1", dx1, dh, x0, nm_g[0:1], sc_m[0] + tok)

    dmod = jnp.stack([jnp.concatenate([dsh_m0, dsc_m0, dgate_m0, dsh_f0, dsc_f0, dgate_f0], axis=0),
                      jnp.concatenate([dsh_m1, dsc_m1, dgate_m1, dsh_f1, dsc_f1, dgate_f1], axis=0)])
    small = dict(
        norm_mix_g=jnp.concatenate([dnm0, dnm1], axis=0),
        norm_ffn_g=jnp.concatenate([dnf0, dnf1], axis=0),
        a_vnorm_g=dvg.reshape(1, A_GROUPS, LANES),
        a_spatial_w=dws[None],
        a_spatial_b=dbf.reshape(CHUNK, A_GROUPS, LANES).sum(-1).T[None],
        b_q_norm_g=dqg.reshape(N_HEADS, HEAD).sum(0)[None],
        b_k_norm_g=dkg.reshape(N_HEADS, HEAD).sum(0)[None],
        conv_pw1_b=dpw1_b, conv_dw_w=ddw_w[None, :sp["conv_dw_w"].shape[1]], conv_dw_b=ddw_b,
        conv_ln_g=dln_g, conv_ln_b=dln_b, conv_pw2_b=dpw2_b,
        ffn_dw_w=jnp.stack([gf0["dw_w"], gf1["dw_w"]]),
        ffn_dw_b=jnp.concatenate([gf0["dw_b"], gf1["dw_b"]], axis=0),
    )
    return lossv, grad_x, small, dmod


ADA_TN = 512


def _ada_fwd(name, c_all, ada_w, ada_b_sh):
    L, D, N = ada_w.shape
    B = c_all.shape[0]

    def body(c_ref, w_ref, b_ref, o_ref):
        cv = c_ref[...]
        ca = (cv * _sigmoid(cv)).astype(BF16)
        o_ref[0] = jnp.dot(ca, w_ref[0].astype(BF16), preferred_element_type=F32) + b_ref[0]

    return pl.pallas_call(
        body, name=name, grid=(L, N // ADA_TN),
        in_specs=[pl.BlockSpec((B, D), lambda l, j: (0, 0)), pl.BlockSpec((1, D, ADA_TN), lambda l, j: (l, 0, j)),
                  pl.BlockSpec((1, 1, ADA_TN), lambda l, j: (l, 0, j))],
        out_specs=pl.BlockSpec((1, B, ADA_TN), lambda l, j: (l, 0, j)),
        out_shape=jax.ShapeDtypeStruct((L, B, N), F32),
        compiler_params=_params(("parallel", "parallel")),
    )(c_all, ada_w, ada_b_sh.reshape(L, 1, N))


def _adamw_val(w, g, m, v):
    m2 = ADAM_B1 * m + (1.0 - ADAM_B1) * g
    v2 = ADAM_B2 * v + (1.0 - ADAM_B2) * (g * g)
    m_hat = m2 / (1.0 - ADAM_B1 ** ADAM_STEP)
    v_hat = v2 / (1.0 - ADAM_B2 ** ADAM_STEP)
    delta = -ADAM_LR * (m_hat / (jnp.sqrt(v_hat) + ADAM_EPS) + ADAM_WD * w)
    return delta, m2, v2


def _ada_update(name, c_all, dmod_sh, w, m, v):
    L, D, N = w.shape
    B = c_all.shape[0]

    def body(c_ref, d_ref, w_ref, m_ref, v_ref, g_ref, dl_ref, mo_ref, vo_ref):
        cv = c_ref[...]
        ca = (cv * _sigmoid(cv)).astype(BF16)
        g = lax.dot_general(ca, d_ref[0].astype(BF16), _DN["tn"], preferred_element_type=F32)
        g_ref[0] = g
        dl_ref[0], mo_ref[0], vo_ref[0] = _adamw_val(w_ref[0], g, m_ref[0], v_ref[0])

    wspec = pl.BlockSpec((1, D, ADA_TN), lambda l, j: (l, 0, j))
    return pl.pallas_call(
        body, name=name, grid=(L, N // ADA_TN),
        in_specs=[pl.BlockSpec((B, D), lambda l, j: (0, 0)), pl.BlockSpec((1, B, ADA_TN), lambda l, j: (l, 0, j)),
                  wspec, wspec, wspec],
        out_specs=[wspec] * 4, out_shape=[jax.ShapeDtypeStruct((L, D, N), F32)] * 4,
        compiler_params=_params(("parallel", "parallel")),
    )(c_all, dmod_sh, w, m, v)


def _adamw(name, w, g, m, v):
    R, C = w.shape
    tm = R
    for cand in (256, 128, 64, 32, 16, 8):
        if R % cand == 0 and cand * C * 4 <= (1 << 20):
            tm = cand
            break

    def body(w_ref, g_ref, m_ref, v_ref, d_ref, mo_ref, vo_ref):
        d_ref[...], mo_ref[...], vo_ref[...] = _adamw_val(w_ref[...], g_ref[...], m_ref[...], v_ref[...])

    return _rowcall(name, body, R, tm, [(w, ROW), (g, ROW), (m, ROW), (v, ROW)], [((R, C), F32, ROW)] * 3)


def _row_tile(rows, width, itemsize=4, limit=1 << 20):
    for cand in (512, 256, 128, 64, 32, 16):
        if rows % cand == 0 and cand * width * itemsize <= limit:
            return cand
    raise ValueError((rows, width))


def _cast_into_full(name, a, layer, q, kind, after):
    L, r, c = a.shape
    tm = _row_tile(r, c)
    if kind == "col":
        full, o_spec = (r, N_CHIPS * c), pl.BlockSpec((tm, c), lambda i, q_ref: (i, q_ref[0]))
    else:
        full, o_spec = (N_CHIPS * r, c), pl.BlockSpec((tm, c), lambda i, q_ref: (q_ref[0] * (r // tm) + i, 0))

    def body(q_ref, a_ref, after_ref, o_ref):
        o_ref[...] = a_ref[0].astype(BF16)

    return pl.pallas_call(
        body, name=name,
        grid_spec=pltpu.PrefetchScalarGridSpec(
            num_scalar_prefetch=1, grid=(r // tm,),
            in_specs=[pl.BlockSpec((1, tm, c), lambda i, q_ref: (layer, i, 0)), ANY], out_specs=o_spec),
        out_shape=jax.ShapeDtypeStruct(full, BF16), compiler_params=_params(("parallel",)),
    )(q.reshape(1).astype(jnp.int32), a, after)


def _sum4(name, g, rcv, q, kind, n):
    r, c = rcv.shape[1:]
    tm = _row_tile(r, c)
    if kind == "col":
        g_spec = pl.BlockSpec((tm, n), lambda i, q_ref: (i, q_ref[0]))
    else:
        g_spec = pl.BlockSpec((tm, c), lambda i, q_ref: (q_ref[0] * (n // tm) + i, 0))

    def body(q_ref, g_ref, r_ref, o_ref):
        acc = g_ref[...].astype(F32)
        for j in range(3):
            acc = acc + r_ref[j].astype(F32)
        o_ref[...] = acc

    return pl.pallas_call(
        body, name=name,
        grid_spec=pltpu.PrefetchScalarGridSpec(
            num_scalar_prefetch=1, grid=(r // tm,),
            in_specs=[g_spec, pl.BlockSpec((3, tm, c), lambda i, q_ref: (0, i, 0))],
            out_specs=pl.BlockSpec((tm, c), lambda i, q_ref: (i, 0))),
        out_shape=jax.ShapeDtypeStruct((r, c), F32), compiler_params=_params(("parallel",)),
    )(q.reshape(1).astype(jnp.int32), g, rcv)


def _adamw_sum(name, w, m, v, layer, mine, theirs, prev):
    L, r, c = w.shape
    tm = _row_tile(r, c, limit=3 << 19)
    lay = pl.BlockSpec((1, tm, c), lambda i: (layer, i, 0))
    flat = pl.BlockSpec((tm, c), lambda i: (i, 0))
    n_prev = 0 if prev is None else 4

    def body(w_ref, m_ref, v_ref, a_ref, b_ref, *rest):
        g_ref, d_ref, mo_ref, vo_ref = rest[n_prev:]
        g = a_ref[...] + b_ref[...]
        g_ref[0] = g
        d_ref[0], mo_ref[0], vo_ref[0] = _adamw_val(w_ref[0], g, m_ref[0], v_ref[0])

    return pl.pallas_call(
        body, name=name, grid=(r // tm,),
        in_specs=[lay, lay, lay, flat, flat] + [ANY] * n_prev, out_specs=[lay] * 4,
        out_shape=[jax.ShapeDtypeStruct((L, r, c), F32)] * 4,
        input_output_aliases={5 + k: k for k in range(n_prev)},
        compiler_params=_params(("parallel",)),
    )(w, m, v, mine, theirs, *(prev or ()))


def _sum8(name, gathered):
    R, N = gathered.shape
    P = R // 8

    def body(g_ref, o_ref):
        acc = g_ref[0:P, :]
        for d in range(1, 8):
            acc = acc + g_ref[d * P:(d + 1) * P, :]
        o_ref[...] = acc

    return pl.pallas_call(body, name=name, out_shape=jax.ShapeDtypeStruct((P, N), F32),
                          compiler_params=pltpu.CompilerParams(vmem_limit_bytes=VMEM_LIMIT))(gathered)


ANY = pl.BlockSpec(memory_space=pl.ANY)


def _mesh_pos():
    x, y, c = lax.axis_index("x"), lax.axis_index("y"), lax.axis_index("c")
    other_chips = [(1 - x, y), (x, 1 - y), (1 - x, 1 - y)]
    return x, y, c, other_chips


def _allgather8(name, blk):
    m_per, n = blk.shape

    def body(x_ref, out_ref, send_sems, recv_sems, local_sem):
        x, y, c, chips = _mesh_pos()
        me, sibling = (x, y, c), (x, y, 1 - c)

        def rows(px, py, pc):
            return out_ref.at[pl.ds((4 * px + 2 * py + pc) * m_per, m_per), :]

        def copy(k, block, to, src=None):
            return pltpu.make_async_remote_copy(
                src_ref=rows(*block) if src is None else src, dst_ref=rows(*block),
                send_sem=send_sems.at[k], recv_sem=recv_sems.at[k], device_id=to, device_id_type=MESH)

        mine = pltpu.make_async_copy(x_ref, rows(*me), local_sem)
        mine.start()
        first = [copy(0, me, sibling, src=x_ref)]
        first += [copy(1 + j, me, (*chip, c), src=x_ref) for j, chip in enumerate(chips)]
        for cp in first:
            cp.start()
        passed = [copy(4 + j, (*chip, c), sibling) for j, chip in enumerate(chips)]
        for j, chip in enumerate(chips):
            copy(1 + j, (*chip, c), me).wait_recv()
            passed[j].start()
        copy(0, sibling, me).wait_recv()
        for j, chip in enumerate(chips):
            copy(4 + j, (*chip, 1 - c), me).wait_recv()
        for cp in first + passed:
            cp.wait_send()
        mine.wait()

    return pl.pallas_call(
        body, name=name, out_shape=jax.ShapeDtypeStruct((8 * m_per, n), blk.dtype),
        in_specs=[pl.BlockSpec(memory_space=pltpu.VMEM)], out_specs=pl.BlockSpec(memory_space=pltpu.VMEM),
        scratch_shapes=[pltpu.SemaphoreType.DMA((7,)), pltpu.SemaphoreType.DMA((7,)), pltpu.SemaphoreType.DMA],
        compiler_params=pltpu.CompilerParams(vmem_limit_bytes=VMEM_LIMIT),
    )(blk)


BIG = dict(w_in=("col", "ab_w_in", 0), w_out=("row", "ab_w_out", 0), up0=("col", "ffn_up_w", 0),
           dn0=("row", "ffn_down_w", 0), pw1=("col", "conv_pw1_w", 0), pw2=("row", "conv_pw2_w", 0),
           up1=("col", "ffn_up_w", 1), dn1=("row", "ffn_down_w", 1))
N_CHIPS = 4
HBM = pl.BlockSpec(memory_space=pltpu.HBM)
SEM = pl.BlockSpec(memory_space=pltpu.SEMAPHORE)
EFFECT = pltpu.SideEffectType.DATAFLOW_SIDE_EFFECTING


def _region(kind, ref, q, n):
    if kind == "col":
        return ref.at[:, pl.ds(q * n, n)]
    return ref.at[pl.ds(q * n, n), :]


def _gather_plan(kind, n):
    def remote(src, land, pos):
        x, y, c, chips = pos
        mine = _region(kind, land, 2 * x + y, n)
        return [(mine, mine, (*chip, c)) for chip in chips]

    return ("gather", kind, n), remote


def _scatter_plan(kind, n):
    def remote(src, land, pos):
        _, _, c, chips = pos
        return [(_region(kind, src, 2 * chip[0] + chip[1], n), land.at[j], (*chip, c)) for j, chip in enumerate(chips)]

    return ("scatter", kind, n), remote


def _sibling_plan():
    def remote(src, land, pos):
        x, y, c, _ = pos
        return [(src, land, (x, y, 1 - c))]

    return ("sibling",), remote


def _split_start(name, items, after=None):
    n = len(items)
    plans = [it[2] for it in items]
    n_in = 2 * n + (after is not None)

    def body(*refs):
        srcs, lands = refs[:n], refs[n:2 * n]
        sends, recvs = refs[n_in:n_in + n], refs[n_in + n:n_in + 2 * n]
        token = refs[n_in + 4 * n]
        pos = _mesh_pos()
        for a, (_, remote) in enumerate(plans):
            for k, (s, d, dev) in enumerate(remote(srcs[a], lands[a], pos)):
                pltpu.make_async_remote_copy(src_ref=s, dst_ref=d, send_sem=sends[a].at[k], recv_sem=recvs[a].at[k],
                                             device_id=dev, device_id_type=MESH).start()
        token[...] = jnp.zeros_like(token)

    sems = [pltpu.SemaphoreType.DMA((it[3],)) for it in items]
    bufs = [pltpu.HBM(it[k].shape, it[k].dtype) for k in (0, 1) for it in items]
    outs = pl.pallas_call(
        body, name=name, out_shape=[*sems, *sems, *bufs, jax.ShapeDtypeStruct((8, LANES), F32)],
        in_specs=[HBM] * (2 * n) + [ANY] * (n_in - 2 * n),
        out_specs=[SEM] * (2 * n) + [HBM] * (2 * n) + [pl.BlockSpec(memory_space=pltpu.VMEM)],
        input_output_aliases={i: 2 * n + i for i in range(2 * n)},
        compiler_params=pltpu.CompilerParams(has_side_effects=EFFECT),
    )(*[pltpu.with_memory_space_constraint(it[k], pltpu.HBM) for k in (0, 1) for it in items],
      *(() if after is None else (after,)))
    state = [(items[a][2], items[a][3], outs[2 * n + a], outs[3 * n + a], outs[a], outs[n + a]) for a in range(n)]
    return state, outs[4 * n]


def _split_wait(name, state, after):
    n = len(state)

    def body(*refs):
        srcs, lands = refs[:n], refs[n:2 * n]
        sends, recvs = refs[2 * n:3 * n], refs[3 * n:4 * n]
        pos = _mesh_pos()
        for a, ((_, remote), *_) in enumerate(state):
            for k, (s, d, dev) in enumerate(remote(srcs[a], lands[a], pos)):
                cp = pltpu.make_async_remote_copy(src_ref=s, dst_ref=d, send_sem=sends[a].at[k], recv_sem=recvs[a].at[k],
                                                  device_id=dev, device_id_type=MESH)
                cp.wait_send()
                cp.wait_recv()

    bufs = [st[k] for k in (2, 3) for st in state]
    outs = pl.pallas_call(
        body, name=name, out_shape=[pltpu.HBM(b.shape, b.dtype) for b in bufs],
        in_specs=[HBM] * (2 * n) + [SEM] * (2 * n) + [ANY], out_specs=[HBM] * (2 * n),
        input_output_aliases={i: i for i in range(2 * n)},
        compiler_params=pltpu.CompilerParams(has_side_effects=EFFECT),
    )(*bufs, *[st[k] for k in (4, 5) for st in state], after)
    return outs[:n], outs[n:]


class _Weights:
    def __init__(self, w, q, after):
        unused = jnp.zeros((16, LANES), BF16)

        def item(name, after):
            kind, pname, layer = BIG[name]
            _, r, c = w[pname].shape
            land = _cast_into_full(f"cast_{name}", w[pname], layer, q, kind, after)
            return unused, land, _gather_plan(kind, c if kind == "col" else r), N_CHIPS - 1

        first, *rest = BIG
        state1, token1 = _split_start("gw_start_first", [item(first, after)], after)
        state2, self.token = _split_start("gw_start_rest", [item(name, token1) for name in rest], token1)
        self.pending = dict(zip(BIG, state1 + state2))
        self.ready = {}

    def get(self, name, after=None):
        if name not in self.ready:
            self.ready[name] = _split_wait(f"gw_wait_{name}", [self.pending.pop(name)], after)[1][0]
        return self.ready[name]


class _GradPipe:
    def __init__(self, q, w, m, v):
        self.q, self.w, self.m, self.v = q, w, m, v
        self.stage, self.results = {}, {}

    def scatter(self, group, grads, after=None):
        items = []
        for name, g in grads.items():
            kind = BIG[name][0]
            rows, cols = g.shape
            n = (cols if kind == "col" else rows) // N_CHIPS
            reg = (rows, n) if kind == "col" else (n, cols)
            items.append((g, lax.empty((N_CHIPS - 1, *reg), BF16), _scatter_plan(kind, n), N_CHIPS - 1))
        state, token = _split_start(f"gs_start_{group}", items, after)
        self.stage[group] = (list(grads), state)
        return token[0, 0]

    def collect(self, group, after):
        names, state = self.stage[group]
        srcs, lands = _split_wait(f"gs_wait_{group}", state, after)
        items = []
        for name, st, g, land in zip(names, state, srcs, lands):
            _, kind, n = st[0][0]
            part = _sum4(f"sum_{name}", g, land, self.q, kind, n)
            items.append((part, lax.empty(part.shape, F32), _sibling_plan(), 1))
        self.stage[group] = (names, _split_start(f"sw_start_{group}", items)[0])

    def finish(self, group, after):
        names, state = self.stage.pop(group)
        srcs, lands = _split_wait(f"sw_wait_{group}", state, after)
        for name, mine, theirs in zip(names, srcs, lands):
            _, pname, layer = BIG[name]
            self.results[pname] = _adamw_sum(f"adamw_{name}", self.w[pname], self.m[pname], self.v[pname], layer,
                                             mine, theirs, self.results.get(pname))


PACK_ROWS = 8


def _pack(arrays):
    flat = jnp.concatenate([a.reshape(-1) for a in arrays])
    n = flat.shape[0]
    padded = -(-n // (PACK_ROWS * LANES)) * (PACK_ROWS * LANES)
    return jnp.pad(flat, (0, padded - n)).reshape(PACK_ROWS, padded // PACK_ROWS)


def _unpack(packed, shapes):
    flat = packed.reshape(-1)
    out, off = [], 0
    for s in shapes:
        n = 1
        for d in s:
            n *= d
        out.append(flat[off:off + n].reshape(s))
        off += n
    return out


REPLICATED = ("ada_b", "norm_mix_g", "norm_ffn_g", "a_vnorm_g", "a_spatial_w", "a_spatial_b", "b_q_norm_g",
              "b_k_norm_g", "ffn_dw_b")
SMALL_SHARDED = ("conv_pw1_b", "conv_dw_w", "conv_dw_b", "conv_ln_g", "conv_ln_b", "conv_pw2_b", "ffn_dw_w")
WEIGHTS = ("ada_w", "ada_b", "norm_mix_g", "norm_ffn_g", "ab_w_in", "a_vnorm_g", "a_spatial_w", "a_spatial_b",
           "b_q_norm_g", "b_k_norm_g", "ab_w_out", "conv_pw1_w", "conv_pw1_b", "conv_dw_w", "conv_dw_b", "conv_ln_g",
           "conv_ln_b", "conv_pw2_w", "conv_pw2_b", "ffn_up_w", "ffn_dw_w", "ffn_dw_b", "ffn_down_w")

def kernel(x, c, positions, ada_w, ada_b, norm_mix_g, norm_ffn_g, ab_w_in, a_vnorm_g, a_spatial_w, a_spatial_b, b_q_norm_g, b_k_norm_g, ab_w_out, conv_pw1_w, conv_pw1_b, conv_dw_w, conv_dw_b, conv_ln_g, conv_ln_b, conv_pw2_w, conv_pw2_b, ffn_up_w, ffn_dw_w, ffn_dw_b, ffn_down_w, loss_target, m_ada_w, m_ada_b, m_norm_mix_g, m_norm_ffn_g, m_ab_w_in, m_a_vnorm_g, m_a_spatial_w, m_a_spatial_b, m_b_q_norm_g, m_b_k_norm_g, m_ab_w_out, m_conv_pw1_w, m_conv_pw1_b, m_conv_dw_w, m_conv_dw_b, m_conv_ln_g, m_conv_ln_b, m_conv_pw2_w, m_conv_pw2_b, m_ffn_up_w, m_ffn_dw_w, m_ffn_dw_b, m_ffn_down_w, v_ada_w, v_ada_b, v_norm_mix_g, v_norm_ffn_g, v_ab_w_in, v_a_vnorm_g, v_a_spatial_w, v_a_spatial_b, v_b_q_norm_g, v_b_k_norm_g, v_ab_w_out, v_conv_pw1_w, v_conv_pw1_b, v_conv_dw_w, v_conv_dw_b, v_conv_ln_g, v_conv_ln_b, v_conv_pw2_w, v_conv_pw2_b, v_ffn_up_w, v_ffn_dw_w, v_ffn_dw_b, v_ffn_down_w):
    w = dict(ada_w=ada_w, ada_b=ada_b, norm_mix_g=norm_mix_g, norm_ffn_g=norm_ffn_g, ab_w_in=ab_w_in, a_vnorm_g=a_vnorm_g, a_spatial_w=a_spatial_w, a_spatial_b=a_spatial_b, b_q_norm_g=b_q_norm_g, b_k_norm_g=b_k_norm_g, ab_w_out=ab_w_out, conv_pw1_w=conv_pw1_w, conv_pw1_b=conv_pw1_b, conv_dw_w=conv_dw_w, conv_dw_b=conv_dw_b, conv_ln_g=conv_ln_g, conv_ln_b=conv_ln_b, conv_pw2_w=conv_pw2_w, conv_pw2_b=conv_pw2_b, ffn_up_w=ffn_up_w, ffn_dw_w=ffn_dw_w, ffn_dw_b=ffn_dw_b, ffn_down_w=ffn_down_w)
    m = dict(ada_w=m_ada_w, ada_b=m_ada_b, norm_mix_g=m_norm_mix_g, norm_ffn_g=m_norm_ffn_g, ab_w_in=m_ab_w_in, a_vnorm_g=m_a_vnorm_g, a_spatial_w=m_a_spatial_w, a_spatial_b=m_a_spatial_b, b_q_norm_g=m_b_q_norm_g, b_k_norm_g=m_b_k_norm_g, ab_w_out=m_ab_w_out, conv_pw1_w=m_conv_pw1_w, conv_pw1_b=m_conv_pw1_b, conv_dw_w=m_conv_dw_w, conv_dw_b=m_conv_dw_b, conv_ln_g=m_conv_ln_g, conv_ln_b=m_conv_ln_b, conv_pw2_w=m_conv_pw2_w, conv_pw2_b=m_conv_pw2_b, ffn_up_w=m_ffn_up_w, ffn_dw_w=m_ffn_dw_w, ffn_dw_b=m_ffn_dw_b, ffn_down_w=m_ffn_down_w)
    v = dict(ada_w=v_ada_w, ada_b=v_ada_b, norm_mix_g=v_norm_mix_g, norm_ffn_g=v_norm_ffn_g, ab_w_in=v_ab_w_in, a_vnorm_g=v_a_vnorm_g, a_spatial_w=v_a_spatial_w, a_spatial_b=v_a_spatial_b, b_q_norm_g=v_b_q_norm_g, b_k_norm_g=v_b_k_norm_g, ab_w_out=v_ab_w_out, conv_pw1_w=v_conv_pw1_w, conv_pw1_b=v_conv_pw1_b, conv_dw_w=v_conv_dw_w, conv_dw_b=v_conv_dw_b, conv_ln_g=v_conv_ln_g, conv_ln_b=v_conv_ln_b, conv_pw2_w=v_conv_pw2_w, conv_pw2_b=v_conv_pw2_b, ffn_up_w=v_ffn_up_w, ffn_dw_w=v_ffn_dw_w, ffn_dw_b=v_ffn_dw_b, ffn_down_w=v_ffn_down_w)
    S, D = x.shape[1], x.shape[2]
    xi, yi, ci = lax.axis_index("x"), lax.axis_index("y"), lax.axis_index("c")
    q = 2 * xi + yi
    b = 2 * q + ci
    take_dev = lambda g: g.reshape(8, PACK_ROWS, -1)

    c_all = _allgather8("ag_c", c.reshape(PACK_ROWS, D // PACK_ROWS)).reshape(8, D)
    n_ada = ada_w.shape[2]
    mod_sh = _ada_fwd("ada_fwd", c_all, ada_w, lax.dynamic_slice_in_dim(ada_b, q * n_ada, n_ada, axis=1))
    sh_shapes = [mod_sh.shape] + [w[n].shape for n in SMALL_SHARDED]
    gathered_mod = _allgather8("ag_mod", _pack([mod_sh] + [w[n] for n in SMALL_SHARDED]))
    per_chip = [_unpack(blk, sh_shapes) for blk in take_dev(gathered_mod)[0::2]]
    mod_g = jnp.stack([pc[0] for pc in per_chip])
    mod_mine = lax.dynamic_index_in_dim(mod_g, b, axis=2, keepdims=False)
    mod = mod_mine.transpose(1, 0, 2).reshape(2, 6, D)
    sp = {n: jnp.concatenate([pc[1 + i] for pc in per_chip], axis=-1) for i, n in enumerate(SMALL_SHARDED)}
    sp.update({n: w[n] for n in REPLICATED if n != "ada_b"})

    wb = _Weights(w, q, gathered_mod)
    mod = mod + wb.token[0, 0]

    pipe = _GradPipe(q, w, m, v)
    lossv, grad_x, small, dmod = _local_step(x[0], loss_target[0], positions[0], mod, wb, sp, pipe)
    loss = lax.psum(0.5 * jnp.sum(lossv) / D, ("x", "y", "c"))

    small_names = [n for n in REPLICATED if n != "ada_b"] + list(SMALL_SHARDED)
    payload = [dmod.reshape(2, 6 * D)] + [small[n] for n in small_names]
    pay_shapes = [p.shape for p in payload]
    gathered = _allgather8("ag_grads", _pack(payload))
    totals = _unpack(_sum8("sum_grads", gathered), pay_shapes)
    grads = dict(zip(["ada_b"] + small_names, totals))
    for n in SMALL_SHARDED:
        n_sh = w[n].shape[-1]
        grads[n] = lax.dynamic_slice_in_dim(grads[n], q * n_sh, n_sh, axis=grads[n].ndim - 1)
    dmod_all = take_dev(gathered).reshape(8, -1)[:, :2 * 6 * D].reshape(8, 2, 6 * D)
    dmod_sh = lax.dynamic_slice_in_dim(dmod_all, q * n_ada, n_ada, axis=2).transpose(1, 0, 2)

    pipe.finish("g3", dmod_sh)
    pipe.collect("g4", dmod_sh)
    grads["ada_w"], delta_ada, m_ada, v_ada = _ada_update("ada_update", c_all, dmod_sh, ada_w, m_ada_w, v_ada_w)
    delta, new_m, new_v = dict(ada_w=delta_ada), dict(ada_w=m_ada), dict(ada_w=v_ada)
    rest = list(REPLICATED) + list(SMALL_SHARDED)
    rest_shapes = [w[n].shape for n in rest]
    outs = _adamw("adamw_small", *[_pack([src[n].reshape(w[n].shape) for n in rest]) for src in (w, grads, m, v)])
    for tgt, packed in zip((delta, new_m, new_v), outs):
        tgt.update(dict(zip(rest, _unpack(packed, rest_shapes))))
    for n in rest:
        grads[n] = grads[n].reshape(w[n].shape)
    pipe.finish("g4", outs[0])
    for n, res in pipe.results.items():
        grads[n], delta[n], new_m[n], new_v[n] = res

    return (loss, grad_x[None], *[grads[n] for n in WEIGHTS], *[delta[n] for n in WEIGHTS],
            *[new_m[n] for n in WEIGHTS], *[new_v[n] for n in WEIGHTS])
```

```python
import functools

import jax
import jax.numpy as jnp
from jax import lax
from jax.experimental import pallas as pl
from jax.experimental.pallas import tpu as pltpu

F32, BF16 = jnp.float32, jnp.bfloat16
EPS = 1e-6
NEG = -1e30
ROPE_THETA = 10000.0
LANES = 128
VMEM_LIMIT = 56 * 1024 * 1024
ADAM_LR, ADAM_B1, ADAM_B2, ADAM_EPS, ADAM_WD, ADAM_STEP = 0.001, 0.9, 0.999, 1e-08, 0.01, 10
MESH = pl.DeviceIdType.MESH


def _params(sem):
    return pltpu.CompilerParams(dimension_semantics=sem, vmem_limit_bytes=VMEM_LIMIT)


_DN = {"nn": (((1,), (0,)), ((), ())), "nt": (((1,), (1,)), ((), ())), "tn": (((0,), (0,)), ((), ()))}


def _matmul(name, a, b, mode, out_dtype, tm=512, tn=512, tk=1024, bias=None):
    if mode == "nn":
        (M, K), N = a.shape, b.shape[1]
    elif mode == "nt":
        (M, K), N = a.shape, b.shape[0]
    else:
        (K, M), N = a.shape, b.shape[1]
    tm, tn, tk = min(tm, M), min(tn, N), min(tk, K)
    assert M % tm == 0 and N % tn == 0 and K % tk == 0, (name, M, N, K, tm, tn, tk)
    nk = K // tk
    if mode == "tn":
        a_spec = pl.BlockSpec((tk, tm), lambda i, j, k: (k, i))
    else:
        a_spec = pl.BlockSpec((tm, tk), lambda i, j, k: (i, k))
    if mode == "nt":
        b_spec = pl.BlockSpec((tn, tk), lambda i, j, k: (j, k))
    else:
        b_spec = pl.BlockSpec((tk, tn), lambda i, j, k: (k, j))
    in_specs, args = [a_spec, b_spec], [a, b]
    if bias is not None:
        in_specs.append(pl.BlockSpec((1, tn), lambda i, j, k: (0, j)))
        args.append(bias)
    n_in = len(args)

    def body(*refs):
        a_ref, b_ref, o_ref = refs[0], refs[1], refs[n_in]
        p = lax.dot_general(a_ref[...], b_ref[...], _DN[mode], preferred_element_type=F32)

        def finish(acc):
            if bias is not None:
                acc = acc + refs[2][...]
            o_ref[...] = acc.astype(o_ref.dtype)

        if nk == 1:
            finish(p)
        else:
            acc_ref = refs[n_in + 1]
            k = pl.program_id(2)

            @pl.when(k == 0)
            def _():
                acc_ref[...] = p

            @pl.when(k > 0)
            def _():
                acc_ref[...] += p

            @pl.when(k == nk - 1)
            def _():
                finish(acc_ref[...])

    return pl.pallas_call(
        body, name=name, grid=(M // tm, N // tn, nk), in_specs=in_specs,
        out_specs=pl.BlockSpec((tm, tn), lambda i, j, k: (i, j)),
        out_shape=jax.ShapeDtypeStruct((M, N), out_dtype),
        scratch_shapes=[pltpu.VMEM((tm, tn), F32)] if nk > 1 else [],
        compiler_params=_params(("parallel", "parallel", "arbitrary")),
    )(*args)


def _rowcall(name, body, nrows, tm, ins, outs, scratch=()):
    nblk = nrows // tm
    assert nrows % tm == 0

    def spec(kind, shape):
        k = kind[0]
        if k == "row":
            cw, cb = kind[1] or shape[-1], kind[2]
            return pl.BlockSpec((tm, cw), lambda i: (i, cb))
        if k == "prev":
            hb, cw, cb = kind[1], kind[2] or shape[-1], kind[3]
            r = tm // hb
            return pl.BlockSpec((hb, cw), lambda i: (jnp.maximum(i * r - 1, 0), cb))
        if k == "next":
            hb, cw, cb = kind[1], kind[2] or shape[-1], kind[3]
            r, last = tm // hb, nrows // hb - 1
            return pl.BlockSpec((hb, cw), lambda i: (jnp.minimum((i + 1) * r, last), cb))
        if k == "off":
            off, cw, cb = kind[1], kind[2] or shape[-1], kind[3]
            return pl.BlockSpec((tm, cw), lambda i: (jnp.clip(i + off, 0, nblk - 1), cb))
        if k == "cls":
            dil, off = kind[1], kind[2]
            return pl.BlockSpec((dil, tm // dil, shape[-1]), lambda i: (0, jnp.clip(i + off, 0, nblk - 1), 0))
        nd = len(shape)
        return pl.BlockSpec(tuple(shape), lambda i: (0,) * nd)

    has_acc = any(o[2][0] == "acc" for o in outs)
    return pl.pallas_call(
        body, name=name, grid=(nblk,),
        in_specs=[spec(kind, a.shape) for a, kind in ins],
        out_specs=[spec(kind, shape) for shape, _, kind in outs],
        out_shape=[jax.ShapeDtypeStruct(tuple(shape), dt) for shape, dt, _ in outs],
        scratch_shapes=list(scratch),
        compiler_params=_params(("arbitrary",) if has_acc else ("parallel",)),
    )(*[a for a, _ in ins])


ROW = ("row", None, 0)
FULL = ("full",)
ACC = ("acc",)


def _colsum(x):
    return jnp.sum(x, axis=0, keepdims=True)


def _acc_add(i, ref, val, rows=None):
    idx = (slice(None),) * len(ref.shape) if rows is None else rows

    @pl.when(i == 0)
    def _():
        ref[idx] = val

    @pl.when(i > 0)
    def _():
        ref[idx] = ref[idx] + val


def _sigmoid(x):
    return 1.0 / (1.0 + jnp.exp(-x))


def _gelu(x):
    return 0.5 * x * (1.0 + lax.erf(x * (2.0 ** -0.5)))


def _gelu_grad(x):
    return 0.5 * (1.0 + lax.erf(x * (2.0 ** -0.5))) + x * jnp.exp(-0.5 * x * x) * ((2.0 * jnp.pi) ** -0.5)


SUBLANES = 8


def _phases(ext, sign):
    n = ext.shape[0]
    return [ext if b == 0 else pltpu.roll(ext, b if sign > 0 else n - b, axis=0) for b in range(SUBLANES)]


def _shift_prev(phases, s, hb):
    a, b = divmod(s, SUBLANES)
    return phases[b][hb - SUBLANES * a:phases[b].shape[0] - SUBLANES * a]


def _shift_next(phases, s, tm):
    a, b = divmod(s, SUBLANES)
    return phases[b][SUBLANES * a:SUBLANES * a + tm]


def _rms_mod_val(x, g, sc, sh):
    r = lax.rsqrt(jnp.mean(x * x, axis=-1, keepdims=True) + EPS)
    return x * r * g * (1.0 + sc) + sh


def _mod_first(name, x, g, sc, sh, tm=256):
    S, D = x.shape

    def body(x_ref, g_ref, sc_ref, sh_ref, h_ref):
        h_ref[...] = _rms_mod_val(x_ref[...], g_ref[...], sc_ref[...], sh_ref[...]).astype(BF16)

    return _rowcall(name, body, S, tm, [(x, ROW), (g, FULL), (sc, FULL), (sh, FULL)], [((S, D), BF16, ROW)])[0]


def _resid_mod(name, x, y, gate, g, sc, sh, tm=256):
    S, D = x.shape

    def body(x_ref, y_ref, gate_ref, g_ref, sc_ref, sh_ref, xo_ref, h_ref):
        xn = x_ref[...] + gate_ref[...] * y_ref[...]
        xo_ref[...] = xn
        h_ref[...] = _rms_mod_val(xn, g_ref[...], sc_ref[...], sh_ref[...]).astype(BF16)

    return _rowcall(name, body, S, tm,
                    [(x, ROW), (y, ROW), (gate, FULL), (g, FULL), (sc, FULL), (sh, FULL)],
                    [((S, D), F32, ROW), ((S, D), BF16, ROW)])


def _gate_bwd_val(i, d, y_ref, gate_ref, dy_ref, dg_ref, db_ref):
    dy = d * gate_ref[...]
    dy_ref[...] = dy.astype(BF16)
    _acc_add(i, dg_ref, _colsum(d * y_ref[...]))
    _acc_add(i, db_ref, _colsum(dy))


GATE_OUTS = lambda S, D: [((S, D), BF16, ROW), ((1, D), F32, ACC), ((1, D), F32, ACC)]


def _loss_head(name, x, y, gate, tgt, tm=256):
    S, D = x.shape

    def body(x_ref, y_ref, gate_ref, t_ref, dx_ref, l_ref, dy_ref, dg_ref, db_ref):
        i = pl.program_id(0)
        err = x_ref[...] + gate_ref[...] * y_ref[...] - t_ref[...]
        d = err * (1.0 / D)
        dx_ref[...] = d
        _acc_add(i, l_ref, _colsum(err * err))
        _gate_bwd_val(i, d, y_ref, gate_ref, dy_ref, dg_ref, db_ref)

    return _rowcall(name, body, S, tm, [(x, ROW), (y, ROW), (gate, FULL), (tgt, ROW)],
                    [((S, D), F32, ROW), ((1, D), F32, ACC)] + GATE_OUTS(S, D))


def _mod_bwd(name, dxo, dh, x, g, sc, y=None, gate=None, tm=256):
    S, D = x.shape
    gated = y is not None

    def body(d_ref, dh_ref, x_ref, g_ref, sc_ref, *rest):
        dx_ref, dsh_ref, dsc_ref, dg_ref = rest[2 * gated:2 * gated + 4]
        i = pl.program_id(0)
        xv, dh_v, gv = x_ref[...], dh_ref[...], g_ref[...]
        r = lax.rsqrt(jnp.mean(xv * xv, axis=-1, keepdims=True) + EPS)
        n = xv * r
        _acc_add(i, dsh_ref, _colsum(dh_v))
        _acc_add(i, dsc_ref, _colsum(dh_v * (n * gv)))
        dy = dh_v * (1.0 + sc_ref[...])
        _acc_add(i, dg_ref, _colsum(dy * n))
        dn = dy * gv
        dx = d_ref[...] + r * (dn - n * jnp.mean(dn * n, axis=-1, keepdims=True))
        dx_ref[...] = dx
        if gated:
            _gate_bwd_val(i, dx, rest[0], rest[1], *rest[6:9])

    ins = [(dxo, ROW), (dh, ROW), (x, ROW), (g, FULL), (sc, FULL)] + ([(y, ROW), (gate, FULL)] if gated else [])
    outs = [((S, D), F32, ROW), ((1, D), F32, ACC), ((1, D), F32, ACC), ((1, D), F32, ACC)]
    return _rowcall(name, body, S, tm, ins, outs + (GATE_OUTS(S, D) if gated else []))


HB16 = 16


def _conv3_val(ph, w, b, hb):
    return w[2:3] * _shift_prev(ph, 0, hb) + w[1:2] * _shift_prev(ph, 1, hb) + w[0:1] * _shift_prev(ph, 2, hb) + b


def _halo_first(halo_ref, tile_ref, live):
    return _phases(jnp.concatenate([halo_ref[...].astype(F32) * live, tile_ref[...].astype(F32)], axis=0), 1)


def _glu3_fwd(name, u, w, b, tm=128):
    S, F2 = u.shape
    Fh = F2 // 2

    def body(ua_ref, ub_ref, ha_ref, hb_ref, w_ref, b_ref, o_ref, z_ref):
        live = (pl.program_id(0) > 0).astype(F32)
        wv, bv = w_ref[...], b_ref[...]
        za = _conv3_val(_halo_first(ha_ref, ua_ref, live), wv[:, :Fh], bv[:, :Fh], HB16)
        zb = _conv3_val(_halo_first(hb_ref, ub_ref, live), wv[:, Fh:], bv[:, Fh:], HB16)
        o_ref[...] = (za * _sigmoid(za) * zb).astype(BF16)
        z_ref[:, :Fh] = za.astype(BF16)
        z_ref[:, Fh:] = zb.astype(BF16)

    return _rowcall(name, body, S, tm,
                    [(u, ("row", Fh, 0)), (u, ("row", Fh, 1)), (u, ("prev", HB16, Fh, 0)), (u, ("prev", HB16, Fh, 1)),
                     (w, FULL), (b, FULL)],
                    [((S, Fh), BF16, ROW), ((S, F2), BF16, ROW)])


def _glu3_bwd(name, z, dhm, tm=128):
    S, F2 = z.shape
    Fh = F2 // 2

    def body(za_ref, zb_ref, d_ref, dz_ref, db_ref):
        i = pl.program_id(0)
        za, zb, d = za_ref[...].astype(F32), zb_ref[...].astype(F32), d_ref[...]
        sg = _sigmoid(za)
        da = d * zb * (sg * (1.0 + za * (1.0 - sg)))
        db = d * (za * sg)
        dz_ref[:, :Fh] = da.astype(BF16)
        dz_ref[:, Fh:] = db.astype(BF16)
        _acc_add(i, db_ref, jnp.concatenate([_colsum(da), _colsum(db)], axis=1))

    return _rowcall(name, body, S, tm, [(z, ("row", Fh, 0)), (z, ("row", Fh, 1)), (dhm, ROW)],
                    [((S, F2), BF16, ROW), ((1, F2), F32, ACC)])


def _conv3_bwd(name, dz, u, w, tm=128):
    S, F2 = dz.shape
    nblk = S // tm
    K = w.shape[0]

    def body(d_ref, n_ref, u_ref, w_ref, o_ref, dw_ref):
        i = pl.program_id(0)
        live = (i < nblk - 1).astype(F32)
        ph = _phases(jnp.concatenate([d_ref[...].astype(F32), n_ref[...].astype(F32) * live], axis=0), -1)
        wv, uv = w_ref[...], u_ref[...].astype(F32)
        shifted = [_shift_next(ph, K - 1 - k, tm) for k in range(K)]
        o_ref[...] = functools.reduce(lambda a, t: a + t, [wv[k:k + 1] * shifted[k] for k in range(K)]).astype(BF16)
        for k in range(K):
            _acc_add(i, dw_ref, _colsum(uv * shifted[k]), rows=(slice(k, k + 1), slice(None)))

        @pl.when(i == 0)
        def _():
            dw_ref[K:, :] = jnp.zeros((dw_ref.shape[0] - K, F2), F32)

    return _rowcall(name, body, S, tm, [(dz, ROW), (dz, ("next", HB16, None, 0)), (u, ROW), (w, FULL)],
                    [((S, F2), BF16, ROW), ((SUBLANES, F2), F32, ACC)])


def _ffn_fwd(name, h, w_up, w_dn, dw_w, dw_b):
    u = _matmul(f"{name}_up", h, w_up, "nn", BF16, tm=2048)
    hm, z = _glu3_fwd(f"{name}_glu", u, dw_w, dw_b)
    f = _matmul(f"{name}_dn", hm, w_dn, "nn", F32, tm=1024, tn=1024, tk=w_dn.shape[0])
    return f, (u, hm, z)


def _ffn_bwd(name, dy, h, u, hm, z, w_up, w_dn, dw_w, dw_b):
    Fh = w_dn.shape[0]
    dhm = _matmul(f"{name}_ddn_x", dy, w_dn, "nt", F32, tm=1024, tn=Fh // 2)
    g_dn = _matmul(f"{name}_ddn_w", hm, dy, "tn", BF16, tm=Fh // 2, tn=1024, tk=2048)
    dz, g_dw_b = _glu3_bwd(f"{name}_dglu", z, dhm)
    du, taps = _conv3_bwd(f"{name}_dconv", dz, u, dw_w)
    g_up = _matmul(f"{name}_dup_w", h, du, "tn", BF16, tm=1024, tn=Fh // 2, tk=2048)
    dh = _matmul(f"{name}_dup_x", du, w_up, "nt", F32, tm=1024, tn=1024, tk=Fh)
    return dh, dict(up=g_up, dn=g_dn, dw_w=taps[0:dw_w.shape[0]], dw_b=g_dw_b)


HB32 = 32


def _glu31_fwd(name, p, w, b, tm=256):
    S, D2 = p.shape
    D = D2 // 2
    K = w.shape[0]

    def body(a_ref, g_ref, ha_ref, hg_ref, w_ref, b_ref, o_ref):
        live = (pl.program_id(0) > 0).astype(F32)
        y1 = a_ref[...] * _sigmoid(g_ref[...])
        ph = _phases(jnp.concatenate([ha_ref[...] * _sigmoid(hg_ref[...]) * live, y1], axis=0), 1)
        wv = w_ref[...]
        acc = b_ref[...] + wv[K - 1:K] * y1
        for k in range(K - 1):
            acc = acc + wv[k:k + 1] * _shift_prev(ph, K - 1 - k, HB32)
        o_ref[...] = acc

    return _rowcall(name, body, S, tm,
                    [(p, ("row", D, 0)), (p, ("row", D, 1)), (p, ("prev", HB32, D, 0)), (p, ("prev", HB32, D, 1)),
                     (w, FULL), (b, FULL)],
                    [((S, D), F32, ROW)])[0]


def _ln_silu_fwd(name, y2, g, b, tm=256):
    S, D = y2.shape

    def body(y_ref, g_ref, b_ref, o_ref):
        y = y_ref[...]
        mu = jnp.mean(y, axis=-1, keepdims=True)
        yc = y - mu
        rs = lax.rsqrt(jnp.mean(yc * yc, axis=-1, keepdims=True) + EPS)
        y3 = yc * rs * g_ref[...] + b_ref[...]
        o_ref[...] = (y3 * _sigmoid(y3)).astype(BF16)

    return _rowcall(name, body, S, tm, [(y2, ROW), (g, FULL), (b, FULL)], [((S, D), BF16, ROW)])[0]


def _ln_silu_bwd(name, y2, dy4, g, b, tm=256):
    S, D = y2.shape

    def body(y_ref, d_ref, g_ref, b_ref, o_ref, dg_ref, db_ref):
        i = pl.program_id(0)
        y, gv = y_ref[...], g_ref[...]
        mu = jnp.mean(y, axis=-1, keepdims=True)
        yc = y - mu
        rs = lax.rsqrt(jnp.mean(yc * yc, axis=-1, keepdims=True) + EPS)
        n = yc * rs
        y3 = n * gv + b_ref[...]
        sg = _sigmoid(y3)
        dy3 = d_ref[...] * (sg * (1.0 + y3 * (1.0 - sg)))
        _acc_add(i, db_ref, _colsum(dy3))
        _acc_add(i, dg_ref, _colsum(dy3 * n))
        dn = dy3 * gv
        o_ref[...] = rs * (dn - jnp.mean(dn, axis=-1, keepdims=True) - n * jnp.mean(dn * n, axis=-1, keepdims=True))

    return _rowcall(name, body, S, tm, [(y2, ROW), (dy4, ROW), (g, FULL), (b, FULL)],
                    [((S, D), F32, ROW), ((1, D), F32, ACC), ((1, D), F32, ACC)])


def _glu31_bwd(name, p, dy2, w, tm=256):
    S, D2 = p.shape
    D = D2 // 2
    K = w.shape[0]
    nblk = S // tm

    conv_rows, tap_rows, tap_group, tap_unroll = 16, SUBLANES, 4, 4

    def body(a_ref, g_ref, d_ref, dn_ref, w_ref, dp_ref, dw_ref, dcb_ref, dpb_ref, ph_d, y1_ref, dy1_ref, wb_ref):
        i = pl.program_id(0)
        live_next = (i < nblk - 1).astype(F32)
        a, sg, d = a_ref[...], _sigmoid(g_ref[...]), d_ref[...]
        y1_ref[...] = a * sg
        for b, ph in enumerate(_phases(jnp.concatenate([d, dn_ref[...] * live_next], axis=0), -1)):
            ph_d[b] = ph
        taps = [divmod(K - 1 - k, SUBLANES) for k in range(K)]

        @pl.when(i == 0)
        def _():
            for k in range(K):
                wb_ref[k] = jnp.broadcast_to(w_ref[k:k + 1, :], (SUBLANES, D))

        def conv_rows_at(rb, carry):
            r0 = pl.multiple_of(rb * conv_rows, conv_rows)
            accs = [jnp.zeros((SUBLANES, D), F32) for _ in range(conv_rows // SUBLANES)]
            for k, (rows8, phase) in enumerate(taps):
                wk = wb_ref[k]
                for u in range(len(accs)):
                    accs[u] = accs[u] + wk * ph_d[phase, pl.ds(r0 + SUBLANES * (rows8 + u), SUBLANES), :]
            for u, acc in enumerate(accs):
                dy1_ref[pl.ds(r0 + SUBLANES * u, SUBLANES), :] = acc
            return carry

        lax.fori_loop(0, tm // conv_rows, conv_rows_at, 0)
        for k0 in range(0, K, tap_group):
            group = taps[k0:k0 + tap_group]

            def tap_rows_at(rb, accs, group=group):
                for u in range(tap_unroll):
                    r0 = pl.multiple_of((rb * tap_unroll + u) * tap_rows, tap_rows)
                    yv = y1_ref[pl.ds(r0, tap_rows), :]
                    accs = tuple(acc + yv * ph_d[phase, pl.ds(r0 + SUBLANES * rows8, tap_rows), :]
                                 for acc, (rows8, phase) in zip(accs, group))
                return accs

            accs = lax.fori_loop(0, tm // (tap_rows * tap_unroll), tap_rows_at,
                                 tuple(jnp.zeros((tap_rows, D), F32) for _ in group))
            for j, acc in enumerate(accs):
                _acc_add(i, dw_ref, _colsum(acc), rows=(slice(k0 + j, k0 + j + 1), slice(None)))

        @pl.when(i == 0)
        def _():
            dw_ref[K:, :] = jnp.zeros((dw_ref.shape[0] - K, D), F32)

        _acc_add(i, dcb_ref, _colsum(d))
        dy1 = dy1_ref[...]
        da = dy1 * sg
        dg = dy1 * a * sg * (1.0 - sg)
        dp_ref[:, :D] = da.astype(BF16)
        dp_ref[:, D:] = dg.astype(BF16)
        _acc_add(i, dpb_ref, jnp.concatenate([_colsum(da), _colsum(dg)], axis=1))

    return _rowcall(name, body, S, tm,
                    [(p, ("row", D, 0)), (p, ("row", D, 1)), (dy2, ROW), (dy2, ("next", HB32, None, 0)), (w, FULL)],
                    [((S, D2), BF16, ROW), ((HB32, D), F32, ACC), ((1, D), F32, ACC), ((1, D2), F32, ACC)],
                    scratch=[pltpu.VMEM((SUBLANES, tm + HB32, D), F32), pltpu.VMEM((tm, D), F32),
                             pltpu.VMEM((tm, D), F32), pltpu.VMEM((K, SUBLANES, D), F32)])


CHUNK = 128
A_GROUPS = 4


def _group_ln(gv):
    ns, rss = [], []
    for g in range(A_GROUPS):
        xg = gv[:, g * LANES:(g + 1) * LANES]
        xc = xg - jnp.mean(xg, axis=-1, keepdims=True)
        rs = lax.rsqrt(jnp.mean(xc * xc, axis=-1, keepdims=True) + EPS)
        ns.append(xc * rs)
        rss.append(jnp.broadcast_to(rs, xg.shape))
    return jnp.concatenate(ns, axis=1), jnp.concatenate(rss, axis=1)


def _tril_mask():
    r = lax.broadcasted_iota(jnp.int32, (CHUNK, CHUNK), 0)
    c = lax.broadcasted_iota(jnp.int32, (CHUNK, CHUNK), 1)
    return r >= c


def _spatial(ws_ref, x, dn):
    mask = _tril_mask()
    rows = []
    for ci in range(x.shape[0] // CHUNK):
        cols = []
        for g in range(A_GROUPS):
            wm = jnp.where(mask, ws_ref[g], 0.0).astype(BF16)
            xb = x[ci * CHUNK:(ci + 1) * CHUNK, g * LANES:(g + 1) * LANES]
            cols.append(lax.dot_general(wm, xb, dn, preferred_element_type=F32))
        rows.append(jnp.concatenate(cols, axis=1))
    return jnp.concatenate(rows, axis=0)


def _mixa_fwd(name, z, vg, ws, bias_full, tm=256):
    S = z.shape[0]
    W = A_GROUPS * LANES

    def body(u_ref, v_ref, vg_ref, ws_ref, b_ref, o_ref):
        nh, _ = _group_ln(_gelu(v_ref[...]))
        vn = (nh * vg_ref[...]).astype(BF16)
        f = _spatial(ws_ref, vn, _DN["nn"]) + jnp.concatenate([b_ref[...]] * (tm // CHUNK), axis=0)
        o_ref[...] = (_gelu(u_ref[...]) * f).astype(BF16)

    return _rowcall(name, body, S, tm,
                    [(z, ("row", W, 0)), (z, ("row", W, 1)), (vg, FULL), (ws, FULL), (bias_full, FULL)],
                    [((S, W), BF16, ROW)])[0]


def _mixa_bwd(name, z, dyab, vg, ws, bias_full, tm=256):
    S = z.shape[0]
    W = A_GROUPS * LANES
    nch = tm // CHUNK

    def body(u_ref, v_ref, d_ref, vg_ref, ws_ref, b_ref, dz_ref, dws_ref, dbf_ref, dvg_ref):
        i = pl.program_id(0)
        u, v, d, vgv = u_ref[...], v_ref[...], d_ref[...], vg_ref[...]
        nh, rs = _group_ln(_gelu(v))
        vn = (nh * vgv).astype(BF16)
        f = _spatial(ws_ref, vn, _DN["nn"]) + jnp.concatenate([b_ref[...]] * nch, axis=0)
        dz_ref[:, :W] = (d * f * _gelu_grad(u)).astype(BF16)
        df = d * _gelu(u)
        dbf = df[0:CHUNK]
        for ci in range(1, nch):
            dbf = dbf + df[ci * CHUNK:(ci + 1) * CHUNK]
        _acc_add(i, dbf_ref, dbf)
        dfb = df.astype(BF16)
        mask = _tril_mask()
        for g in range(A_GROUPS):
            acc = jnp.zeros((CHUNK, CHUNK), F32)
            for ci in range(nch):
                blk = (slice(ci * CHUNK, (ci + 1) * CHUNK), slice(g * LANES, (g + 1) * LANES))
                acc = acc + lax.dot_general(dfb[blk], vn[blk], _DN["nt"], preferred_element_type=F32)
            _acc_add(i, dws_ref, jnp.where(mask, acc, 0.0)[None], rows=(slice(g, g + 1), slice(None), slice(None)))
        dvn = _spatial(ws_ref, dfb, _DN["tn"])
        _acc_add(i, dvg_ref, _colsum(dvn * nh))
        dnh = dvn * vgv
        parts = []
        for g in range(A_GROUPS):
            cs = slice(g * LANES, (g + 1) * LANES)
            dg_, ng = dnh[:, cs], nh[:, cs]
            parts.append(dg_ - jnp.mean(dg_, axis=-1, keepdims=True) - ng * jnp.mean(dg_ * ng, axis=-1, keepdims=True))
        dz_ref[:, W:] = (rs * jnp.concatenate(parts, axis=1) * _gelu_grad(v)).astype(BF16)

    return _rowcall(name, body, S, tm,
                    [(z, ("row", W, 0)), (z, ("row", W, 1)), (dyab, ("row", W, 0)), (vg, FULL), (ws, FULL),
                     (bias_full, FULL)],
                    [((S, 2 * W), BF16, ROW), ((A_GROUPS, CHUNK, CHUNK), F32, ACC), ((CHUNK, W), F32, ACC),
                     ((1, W), F32, ACC)])


HEAD = 64
N_HEADS = 8
BW = HEAD * N_HEADS
QB = 128
DILATIONS = (1, 4, 16)
QK_SCALE = HEAD ** -0.5


def _gsum64(x, ones_bd):
    x1 = x.astype(BF16)
    r1 = x - x1.astype(F32)
    x2 = r1.astype(BF16)
    x3 = (r1 - x2.astype(F32)).astype(BF16)
    dot = lambda t: jnp.dot(t, ones_bd, preferred_element_type=F32)
    return dot(x1) + dot(x2) + dot(x3)


def _swap32(x):
    n = x.shape[-1]
    up = pltpu.roll(x, n - HEAD // 2, axis=1)
    dn = pltpu.roll(x, HEAD // 2, axis=1)
    lane = lax.broadcasted_iota(jnp.int32, x.shape, 1)
    return jnp.where((lane % HEAD) < HEAD // 2, up, dn)


def _tile4(t):
    return jnp.concatenate([t] * (BW // LANES), axis=1)


def _stage_spec(tm):
    return pltpu.VMEM((BW // LANES, tm, LANES), F32)


def _to_classes(stage, x, dil):
    tm = x.shape[0]
    for j in range(BW // LANES):
        stage[j] = x[:, j * LANES:(j + 1) * LANES]
    return [jnp.concatenate([stage.at[j][pl.ds(r, tm // dil, stride=dil), :] for j in range(BW // LANES)], axis=1)
            for r in range(dil)]


def _from_classes(stage, cls, dil):
    rows = cls.shape[1]
    for r in range(dil):
        for j in range(BW // LANES):
            stage.at[j][pl.ds(r, rows, stride=dil), :] = cls[r, :, j * LANES:(j + 1) * LANES]
    return jnp.concatenate([stage[j] for j in range(BW // LANES)], axis=1)


def _cls_view(t, dil):
    return t if dil == 1 else t.reshape(dil, t.shape[0] // dil, t.shape[1])


def _cls_kind(dil, off=0):
    return ("off", off, None, 0) if dil == 1 else ("cls", dil, off)


def _cls_out(S, dil, dtype):
    return ((S, BW) if dil == 1 else (dil, S // dil, BW), dtype, _cls_kind(dil))


def _flat(t):
    return t.reshape(-1, t.shape[-1])


def _qkv_fwd(name, z, cos, sin, ones_bd, qg, kg, tm=256):
    S = z.shape[0]
    nd = len(DILATIONS)

    def body(q_ref, k_ref, v_ref, c_ref, s_ref, o_ref, qg_ref, kg_ref, *rest):
        outs, stage = rest[:3 * nd], rest[3 * nd]
        c, s, ob = _tile4(c_ref[...]), _tile4(s_ref[...]), o_ref[...]

        def norm_rope(x, g):
            r = lax.rsqrt(_gsum64(x * x, ob) * (1.0 / HEAD) + EPS)
            xn = x * r * g
            return xn * c + _swap32(xn) * s

        vals = [norm_rope(q_ref[...], qg_ref[...]) * QK_SCALE, norm_rope(k_ref[...], kg_ref[...]), v_ref[...]]
        for a, val in enumerate(vals):
            for b, dil in enumerate(DILATIONS):
                if dil == 1:
                    outs[nd * a + b][...] = val.astype(BF16)
                else:
                    for r, rows in enumerate(_to_classes(stage, val, dil)):
                        outs[nd * a + b][r] = rows.astype(BF16)

    outs = _rowcall(name, body, S, tm,
                    [(z, ("row", BW, 2)), (z, ("row", BW, 3)), (z, ("row", BW, 4)), (cos, ROW), (sin, ROW),
                     (ones_bd, FULL), (qg, FULL), (kg, FULL)],
                    [_cls_out(S, dil, BF16) for _ in range(3) for dil in DILATIONS], scratch=[_stage_spec(tm)])
    return [[_flat(outs[nd * a + b]) for a in range(3)] for b in range(nd)]


PAIR = 2 * HEAD


ATT_BLOCKS = 2
ATT_TM = ATT_BLOCKS * QB
ATT_PREV = ("prev", QB, None, 0)


def _key_rows(prev_ref, cur_ref, sb, ps):
    before = prev_ref[:, ps] if sb == 0 else cur_ref[(sb - 1) * QB:sb * QB, ps]
    return jnp.concatenate([before, cur_ref[sb * QB:(sb + 1) * QB, ps]], axis=0)


def _pair_scores(q_ref, kp_ref, kc_ref, sb, hp, half, seg_blocks):
    ps = slice(hp * PAIR, (hp + 1) * PAIR)
    mine = (lax.broadcasted_iota(jnp.int32, (1, PAIR), 1) >= HEAD) == (half == 1)
    qm = jnp.where(mine, q_ref[sb * QB:(sb + 1) * QB, ps], jnp.zeros((), BF16))
    kcat = _key_rows(kp_ref, kc_ref, sb, ps)
    s = lax.dot_general(qm, kcat, _DN["nt"], preferred_element_type=F32)
    qi = lax.broadcasted_iota(jnp.int32, (QB, 2 * QB), 0)
    kj = lax.broadcasted_iota(jnp.int32, (QB, 2 * QB), 1)
    has_prev = ((pl.program_id(0) * ATT_BLOCKS + sb) % seg_blocks) != 0
    valid = (kj >= qi) & (kj <= qi + QB) & ((kj >= QB) | has_prev)
    return mine, qm, kcat, s, valid


def _attn_fwd(name, q, k, v, dil):
    S = q.shape[0]
    seg_blocks = S // dil // QB

    def body(q_ref, kp_ref, kc_ref, vp_ref, vc_ref, o_ref, l_ref):
        for hp in range(N_HEADS // 2):
            ps = slice(hp * PAIR, (hp + 1) * PAIR)
            chains = [(sb, half) for sb in range(ATT_BLOCKS) for half in range(2)]
            sc = [_pair_scores(q_ref, kp_ref, kc_ref, sb, hp, half, seg_blocks) for sb, half in chains]
            ss = [jnp.where(valid, s, NEG) for _, _, _, s, valid in sc]
            ms = [jnp.max(s, axis=-1, keepdims=True) for s in ss]
            pv = [jnp.exp(s - m) for s, m in zip(ss, ms)]
            dens = [jnp.sum(p, axis=-1, keepdims=True) for p in pv]
            vcats = [_key_rows(vp_ref, vc_ref, sb, ps) for sb in range(ATT_BLOCKS)]
            outs = [jnp.dot(p.astype(BF16), vcats[sb], preferred_element_type=F32) / den
                    for p, den, (sb, _) in zip(pv, dens, chains)]
            lses = [jnp.broadcast_to(m + jnp.log(den), (QB, PAIR)) for m, den in zip(ms, dens)]
            for sb in range(ATT_BLOCKS):
                rows, upper = slice(sb * QB, (sb + 1) * QB), sc[2 * sb + 1][0]
                o_ref[rows, ps] = jnp.where(upper, outs[2 * sb + 1], outs[2 * sb])
                l_ref[rows, ps] = jnp.where(upper, lses[2 * sb + 1], lses[2 * sb])

    return _rowcall(name, body, S, ATT_TM, [(q, ROW), (k, ATT_PREV), (k, ROW), (v, ATT_PREV), (v, ROW)],
                    [((S, BW), F32, ROW)] * 2)


def _attn_bwd(name, q, k, v, do, lse, delta, dil):
    S = q.shape[0]
    seg_blocks = S // dil // QB

    def body(q_ref, kp_ref, kc_ref, vp_ref, vc_ref, do_ref, l_ref, dl_ref, dq_ref, dkc_ref, dkp_ref, dvc_ref, dvp_ref):
        for hp in range(N_HEADS // 2):
            ps = slice(hp * PAIR, (hp + 1) * PAIR)
            chains = [(sb, half) for sb in range(ATT_BLOCKS) for half in range(2)]
            rows = [slice(sb * QB, (sb + 1) * QB) for sb, _ in chains]
            cols = [hp * PAIR + half * HEAD for _, half in chains]
            sc = [_pair_scores(q_ref, kp_ref, kc_ref, sb, hp, half, seg_blocks) for sb, half in chains]
            pv = [jnp.where(valid, jnp.exp(s - l_ref[r, c:c + 1]), 0.0) for (_, _, _, s, valid), r, c in zip(sc, rows, cols)]
            vcats = [_key_rows(vp_ref, vc_ref, sb, ps) for sb in range(ATT_BLOCKS)]
            doms = [jnp.where(mine, do_ref[r, ps].astype(BF16), jnp.zeros((), BF16)) for (mine, *_), r in zip(sc, rows)]
            dps = [lax.dot_general(dom, vcats[sb], _DN["nt"], preferred_element_type=F32) for dom, (sb, _) in zip(doms, chains)]
            dss = [(p * (dp - dl_ref[r, c:c + 1])).astype(BF16) for p, dp, r, c in zip(pv, dps, rows, cols)]
            dqs = [jnp.dot(ds, kcat, preferred_element_type=F32) for ds, (_, _, kcat, _, _) in zip(dss, sc)]
            dks = [lax.dot_general(ds, qm, _DN["tn"], preferred_element_type=F32) for ds, (_, qm, *_) in zip(dss, sc)]
            dvs = [lax.dot_general(p.astype(BF16), dom, _DN["tn"], preferred_element_type=F32) for p, dom in zip(pv, doms)]
            for sb in range(ATT_BLOCKS):
                lo, hi = 2 * sb, 2 * sb + 1
                dk, dv = dks[lo] + dks[hi], dvs[lo] + dvs[hi]
                dq_ref[rows[lo], ps] = jnp.where(sc[hi][0], dqs[hi], dqs[lo])
                dkp_ref[rows[lo], ps] = dk[:QB]
                dkc_ref[rows[lo], ps] = dk[QB:]
                dvp_ref[rows[lo], ps] = dv[:QB]
                dvc_ref[rows[lo], ps] = dv[QB:]

    return _rowcall(name, body, S, ATT_TM,
                    [(q, ROW), (k, ATT_PREV), (k, ROW), (v, ATT_PREV), (v, ROW), do, (lse, ROW), (delta, ROW)],
                    [((S, BW), F32, ROW)] * 5)


def _merge_fwd(name, branches, tm=256):
    S = branches[0][0].shape[0]
    nd = len(DILATIONS)

    def body(*refs):
        ins, (y_ref, yb_ref), l_refs, stage = refs[:2 * nd], refs[2 * nd:2 * nd + 2], refs[2 * nd + 2:3 * nd + 2], refs[-1]
        os_, ls = [], []
        for b, dil in enumerate(DILATIONS):
            o, l = ins[2 * b][...], ins[2 * b + 1][...]
            os_.append(o if dil == 1 else _from_classes(stage, o, dil))
            ls.append(l if dil == 1 else _from_classes(stage, l, dil))
        m = functools.reduce(jnp.maximum, ls)
        es = [jnp.exp(l - m) for l in ls]
        den = functools.reduce(lambda a, e: a + e, es)
        y = functools.reduce(lambda a, t: a + t, [e * o for e, o in zip(es, os_)]) / den
        y_ref[...] = y
        yb_ref[...] = y.astype(BF16)
        lse = m + jnp.log(den)
        for b, dil in enumerate(DILATIONS):
            if dil == 1:
                l_refs[b][...] = lse
            else:
                for r, rows in enumerate(_to_classes(stage, lse, dil)):
                    l_refs[b][r] = rows

    ins = [(_cls_view(t, dil), _cls_kind(dil)) for pair, dil in zip(branches, DILATIONS) for t in pair]
    outs = _rowcall(name, body, S, tm, ins,
                    [((S, BW), F32, ROW), ((S, BW), BF16, ROW)] + [_cls_out(S, dil, F32) for dil in DILATIONS],
                    scratch=[_stage_spec(tm)])
    return outs[0], outs[1], [_flat(t) for t in outs[2:]]


def _delta(name, dyab, yb, ones_bd, tm=256):
    S = yb.shape[0]
    nd = len(DILATIONS)

    def body(d_ref, y_ref, o_ref, *rest):
        dl_refs, do_refs, stage = rest[:nd], rest[nd:2 * nd - 1], rest[-1]
        d = d_ref[...]
        dl = _gsum64(d * y_ref[...], o_ref[...])
        for b, dil in enumerate(DILATIONS):
            if dil == 1:
                dl_refs[b][...] = dl
            else:
                for r, rows in enumerate(_to_classes(stage, dl, dil)):
                    dl_refs[b][r] = rows
                for r, rows in enumerate(_to_classes(stage, d, dil)):
                    do_refs[b - 1][r] = rows.astype(BF16)

    outs = _rowcall(name, body, S, tm, [(dyab, ("row", BW, 1)), (yb, ROW), (ones_bd, FULL)],
                    [_cls_out(S, dil, F32) for dil in DILATIONS] + [_cls_out(S, dil, BF16) for dil in DILATIONS[1:]],
                    scratch=[_stage_spec(tm)])
    return [_flat(t) for t in outs[:nd]], [_flat(t) for t in outs[nd:]]


def _qkv_bwd(name, z, cos, sin, ones_bd, qg, kg, pieces):
    S = z.shape[0]
    nblk = S // QB

    def body(q_ref, k_ref, c_ref, s_ref, o_ref, qg_ref, kg_ref, *rest):
        pr, (dz_ref, dqg_ref, dkg_ref), stage = rest[:15], rest[15:18], rest[18]
        i = pl.program_id(0)
        c, s, ob = _tile4(c_ref[...]), _tile4(s_ref[...]), o_ref[...]
        dq = dk = dv = None
        for b, dil in enumerate(DILATIONS):
            a_q, a_kc, a_kp, a_vc, a_vp = [r[...] for r in pr[5 * b:5 * b + 5]]
            live = ((i + dil) < nblk).astype(F32)
            tq, tk, tv = a_q, a_kc + a_kp * live, a_vc + a_vp * live
            if dil > 1:
                tq, tk, tv = (_from_classes(stage, t, dil) for t in (tq, tk, tv))
            dq, dk, dv = (tq, tk, tv) if b == 0 else (dq + tq, dk + tk, dv + tv)

        def back(x, g, d_rot, acc_ref):
            r = lax.rsqrt(_gsum64(x * x, ob) * (1.0 / HEAD) + EPS)
            n = x * r
            dxn = d_rot * c + _swap32(d_rot * s)
            _acc_add(i, acc_ref, _colsum(dxn * n))
            dn = dxn * g
            return r * (dn - n * (_gsum64(dn * n, ob) * (1.0 / HEAD)))

        dz_ref[:, :BW] = back(q_ref[...], qg_ref[...], dq * QK_SCALE, dqg_ref).astype(BF16)
        dz_ref[:, BW:2 * BW] = back(k_ref[...], kg_ref[...], dk, dkg_ref).astype(BF16)
        dz_ref[:, 2 * BW:] = dv.astype(BF16)

    ins = [(z, ("row", BW, 2)), (z, ("row", BW, 3)), (cos, ROW), (sin, ROW), (ones_bd, FULL), (qg, FULL), (kg, FULL)]
    for piece, dil in zip(pieces, DILATIONS):
        a_q, a_kc, a_kp, a_vc, a_vp = (_cls_view(t, dil) for t in piece)
        own, prev = _cls_kind(dil), _cls_kind(dil, dil)
        ins += [(a_q, own), (a_kc, own), (a_kp, prev), (a_vc, own), (a_vp, prev)]
    return _rowcall(name, body, S, QB, ins,
                    [((S, 3 * BW), BF16, ROW), ((1, BW), F32, ACC), ((1, BW), F32, ACC)], scratch=[_stage_spec(QB)])


def _local_step(x0, tgt, pos, mod, wb, sp, pipe):
    S, D = x0.shape
    md = lambda l, j: mod[l, j:j + 1]
    sh_m, sc_m, g_m, sh_f, sc_f, g_f = ([md(l, j) for l in range(2)] for j in range(6))
    nm_g, nf_g = sp["norm_mix_g"], sp["norm_ffn_g"]

    inv_freq = 1.0 / (ROPE_THETA ** (jnp.arange(0, HEAD, 2, dtype=F32) / HEAD))
    ang = pos.astype(F32)[:, None] * inv_freq
    cs, sn = jnp.cos(ang), jnp.sin(ang)
    cos = jnp.concatenate([cs, cs, cs, cs], axis=1)
    sin = jnp.concatenate([-sn, sn, -sn, sn], axis=1)
    head_of = jnp.arange(BW) // HEAD
    ones_bd = (head_of[:, None] == head_of[None, :]).astype(BF16)
    qg = jnp.tile(sp["b_q_norm_g"].reshape(1, HEAD), (1, N_HEADS))
    kg = jnp.tile(sp["b_k_norm_g"].reshape(1, HEAD), (1, N_HEADS))
    vg = sp["a_vnorm_g"].reshape(1, A_GROUPS * LANES)
    ws = sp["a_spatial_w"][0]
    bias_full = jnp.repeat(sp["a_spatial_b"][0].T, LANES, axis=1)
    ffn_s = [(sp["ffn_dw_w"][l], sp["ffn_dw_b"][l:l + 1]) for l in range(2)]

    h0 = _mod_first("l0_mod", x0, nm_g[0:1], sc_m[0], sh_m[0])
    z = _matmul("l0_in", h0, wb.get("w_in", h0), "nn", F32, tm=2048)
    ya = _mixa_fwd("l0_mixa", z, vg, ws, bias_full)
    qkv = _qkv_fwd("l0_qkv", z, cos, sin, ones_bd, qg, kg)
    branches = [_attn_fwd(f"l0_att{dil}", *qkv[b], dil) for b, dil in enumerate(DILATIONS)]
    yb, yb16, lses = _merge_fwd("l0_merge", branches)
    yab = jnp.concatenate([ya, yb16], axis=1)
    y0 = _matmul("l0_out", yab, wb.get("w_out", yab), "nn", F32, tm=1024, tn=1024)
    x1, h1 = _resid_mod("l0_res1", x0, y0, g_m[0], nf_g[0:1], sc_f[0], sh_f[0])
    ffn_w = [(wb.get("up0", h1), wb.get("dn0", h1), *ffn_s[0])]
    f0, saved0 = _ffn_fwd("l0_ffn", h1, *ffn_w[0])
    x2, h2 = _resid_mod("l0_res2", x1, f0, g_f[0], nm_g[1:2], sc_m[1], sh_m[1])
    p = _matmul("l1_pw1", h2, wb.get("pw1", h2), "nn", F32, tm=2048, bias=sp["conv_pw1_b"])
    y2 = _glu31_fwd("l1_glu", p, sp["conv_dw_w"][0], sp["conv_dw_b"])
    y4 = _ln_silu_fwd("l1_ln", y2, sp["conv_ln_g"], sp["conv_ln_b"])
    y1 = _matmul("l1_pw2", y4, wb.get("pw2", y4), "nn", F32, tm=1024, tn=1024, bias=sp["conv_pw2_b"])
    x3, h3 = _resid_mod("l1_res1", x2, y1, g_m[1], nf_g[1:2], sc_f[1], sh_f[1])
    ffn_w.append((wb.get("up1", h3), wb.get("dn1", h3), *ffn_s[1]))
    f1, saved1 = _ffn_fwd("l1_ffn", h3, *ffn_w[1])
    dx4, lossv, dy, dgate_f1, _ = _loss_head("loss", x3, f1, g_f[1], tgt)

    dh, gf1 = _ffn_bwd("l1_ffn", dy, h3, *saved1, *ffn_w[1])
    tok = pipe.scatter("g1", dict(dn1=gf1["dn"], up1=gf1["up"]))
    dx3, dsh_f1, dsc_f1, dnf1, dy, dgate_m1, dpw2_b = _mod_bwd("l1_dmod2", dx4, dh, x3, nf_g[1:2], sc_f[1] + tok,
                                                             y1, g_m[1])
    dy4 = _matmul("l1_dpw2_x", dy, wb.get("pw2"), "nt", F32, tm=1024, tn=1024)
    g_pw2 = _matmul("l1_dpw2_w", y4, dy, "tn", BF16, tm=1024, tn=1024, tk=2048)
    dy2, dln_g, dln_b = _ln_silu_bwd("l1_dln", y2, dy4, sp["conv_ln_g"], sp["conv_ln_b"])
    dp, ddw_w, ddw_b, dpw1_b = _glu31_bwd("l1_dglu", p, dy2, sp["conv_dw_w"][0])
    g_pw1 = _matmul("l1_dpw1_w", h2, dp, "tn", BF16, tm=1024, tn=1024, tk=2048)
    tok = pipe.scatter("g2", dict(pw2=g_pw2, pw1=g_pw1))
    pipe.collect("g1", g_pw1)
    dh = _matmul("l1_dpw1_x", dp, wb.get("pw1"), "nt", F32, tm=1024, tn=1024, tk=2048)
    dx2, dsh_m1, dsc_m1, dnm1, dy, dgate_f0, _ = _mod_bwd("l1_dmod1", dx3, dh, x2, nm_g[1:2], sc_m[1] + tok,
                                                        f0, g_f[0])
    dh, gf0 = _ffn_bwd("l0_ffn", dy, h1, *saved0, *ffn_w[0])
    pipe.finish("g1", gf0["up"])
    pipe.collect("g2", gf0["up"])
    dx1, dsh_f0, dsc_f0, dnf0, dy, dgate_m0, _ = _mod_bwd("l0_dmod2", dx2, dh, x1, nf_g[0:1], sc_f[0], y0, g_m[0])
    dyab = _matmul("l0_dout_x", dy, wb.get("w_out"), "nt", F32, tm=1024, tn=1024)
    g_out = _matmul("l0_dout_w", yab, dy, "tn", BF16, tm=1024, tn=1024, tk=2048)
    tok = pipe.scatter("g3", dict(dn0=gf0["dn"], up0=gf0["up"], w_out=g_out))
    vg = vg + tok
    dza, dws, dbf, dvg = _mixa_bwd("l0_dmixa", z, dyab, vg, ws, bias_full)
    deltas, dos = _delta("l0_delta", dyab, yb, ones_bd)
    pieces = []
    for b, dil in enumerate(DILATIONS):
        do = (dyab, ("row", BW, 1)) if dil == 1 else (dos[b - 1], ROW)
        pieces.append(_attn_bwd(f"l0_datt{dil}", *qkv[b], do, lses[b], deltas[b], dil))
    dzb, dqg, dkg = _qkv_bwd("l0_dqkv", z, cos, sin, ones_bd, qg, kg, pieces)
    _start_small_grads(pipe, dzb, sp, ((dgate_m0, dsh_f0, dsc_f0, dgate_f0),
                                       (dsh_m1, dsc_m1, dgate_m1, dsh_f1, dsc_f1, dgate_f1)), (dnm1, dnf0, dnf1),
                       dvg, dws, dbf, dqg, dkg, dpw1_b, ddw_w, ddw_b, dln_g, dln_b, dpw2_b, gf0, gf1)
    dz = jnp.concatenate([dza, dzb], axis=1)
    g_in = _matmul("l0_din_w", h0, dz, "tn", BF16, tm=1024, tn=1280, tk=2048)
    tok = pipe.scatter("g4", dict(w_in=g_in))
    pipe.finish("g2", g_in)
    pipe.collect("g3", g_in)
    dh = _matmul("l0_din_x", dz, wb.get("w_in"), "nt", F32, tm=1024, tn=1024, tk=2560)
    grad_x, dsh_m0, dsc_m0, dnm0 = _mod_bwd("l0_dmod1", dx1, dh, x0, nm_g[0:1], sc_m[0] + tok)
    return lossv, grad_x, (dsh_m0, dsc_m0, dnm0)


def _start_small_grads(pipe, after, sp, mods, norms, dvg, dws, dbf, dqg, dkg, dpw1_b, ddw_w, ddw_b, dln_g, dln_b,
                       dpw2_b, gf0, gf1):
    (dgate_m0, dsh_f0, dsc_f0, dgate_f0), (dsh_m1, dsc_m1, dgate_m1, dsh_f1, dsc_f1, dgate_f1) = mods
    dnm1, dnf0, dnf1 = norms
    zero = jnp.zeros_like(dnm1)
    dmod = jnp.stack([jnp.concatenate([zero, zero, dgate_m0, dsh_f0, dsc_f0, dgate_f0], axis=0),
                      jnp.concatenate([dsh_m1, dsc_m1, dgate_m1, dsh_f1, dsc_f1, dgate_f1], axis=0)])
    small = dict(
        norm_mix_g=jnp.concatenate([zero, dnm1], axis=0),
        norm_ffn_g=jnp.concatenate([dnf0, dnf1], axis=0),
        a_vnorm_g=dvg.reshape(1, A_GROUPS, LANES),
        a_spatial_w=dws[None],
        a_spatial_b=dbf.reshape(CHUNK, A_GROUPS, LANES).sum(-1).T[None],
        b_q_norm_g=dqg.reshape(N_HEADS, HEAD).sum(0)[None],
        b_k_norm_g=dkg.reshape(N_HEADS, HEAD).sum(0)[None],
        conv_pw1_b=dpw1_b, conv_dw_w=ddw_w[None, :sp["conv_dw_w"].shape[1]], conv_dw_b=ddw_b,
        conv_ln_g=dln_g, conv_ln_b=dln_b, conv_pw2_b=dpw2_b,
        ffn_dw_w=jnp.stack([gf0["dw_w"], gf1["dw_w"]]),
        ffn_dw_b=jnp.concatenate([gf0["dw_b"], gf1["dw_b"]], axis=0),
    )
    pipe.start_small(dmod, small, after)


ADA_TN = 512


def _ada_fwd(name, c_all, ada_w, ada_b_sh):
    L, D, N = ada_w.shape
    B = c_all.shape[0]

    def body(c_ref, w_ref, b_ref, o_ref):
        cv = c_ref[...]
        ca = (cv * _sigmoid(cv)).astype(BF16)
        o_ref[0] = jnp.dot(ca, w_ref[0].astype(BF16), preferred_element_type=F32) + b_ref[0]

    return pl.pallas_call(
        body, name=name, grid=(L, N // ADA_TN),
        in_specs=[pl.BlockSpec((B, D), lambda l, j: (0, 0)), pl.BlockSpec((1, D, ADA_TN), lambda l, j: (l, 0, j)),
                  pl.BlockSpec((1, 1, ADA_TN), lambda l, j: (l, 0, j))],
        out_specs=pl.BlockSpec((1, B, ADA_TN), lambda l, j: (l, 0, j)),
        out_shape=jax.ShapeDtypeStruct((L, B, N), F32),
        compiler_params=_params(("parallel", "parallel")),
    )(c_all, ada_w, ada_b_sh.reshape(L, 1, N))


def _adamw_val(w, g, m, v):
    m2 = ADAM_B1 * m + (1.0 - ADAM_B1) * g
    v2 = ADAM_B2 * v + (1.0 - ADAM_B2) * (g * g)
    m_hat = m2 / (1.0 - ADAM_B1 ** ADAM_STEP)
    v_hat = v2 / (1.0 - ADAM_B2 ** ADAM_STEP)
    delta = -ADAM_LR * (m_hat / (jnp.sqrt(v_hat) + ADAM_EPS) + ADAM_WD * w)
    return delta, m2, v2


def _ada_update(name, c_all, dmod_sh, w, m, v):
    L, D, N = w.shape
    B = c_all.shape[0]

    def body(c_ref, d_ref, w_ref, m_ref, v_ref, g_ref, dl_ref, mo_ref, vo_ref):
        cv = c_ref[...]
        ca = (cv * _sigmoid(cv)).astype(BF16)
        g = lax.dot_general(ca, d_ref[0].astype(BF16), _DN["tn"], preferred_element_type=F32)
        g_ref[0] = g
        dl_ref[0], mo_ref[0], vo_ref[0] = _adamw_val(w_ref[0], g, m_ref[0], v_ref[0])

    wspec = pl.BlockSpec((1, D, ADA_TN), lambda l, j: (l, 0, j))
    return pl.pallas_call(
        body, name=name, grid=(L, N // ADA_TN),
        in_specs=[pl.BlockSpec((B, D), lambda l, j: (0, 0)), pl.BlockSpec((1, B, ADA_TN), lambda l, j: (l, 0, j)),
                  wspec, wspec, wspec],
        out_specs=[wspec] * 4, out_shape=[jax.ShapeDtypeStruct((L, D, N), F32)] * 4,
        compiler_params=_params(("parallel", "parallel")),
    )(c_all, dmod_sh, w, m, v)


def _adamw(name, w, g, m, v):
    R, C = w.shape
    tm = R
    for cand in (256, 128, 64, 32, 16, 8):
        if R % cand == 0 and cand * C * 4 <= (1 << 20):
            tm = cand
            break

    def body(w_ref, g_ref, m_ref, v_ref, d_ref, mo_ref, vo_ref):
        d_ref[...], mo_ref[...], vo_ref[...] = _adamw_val(w_ref[...], g_ref[...], m_ref[...], v_ref[...])

    return _rowcall(name, body, R, tm, [(w, ROW), (g, ROW), (m, ROW), (v, ROW)], [((R, C), F32, ROW)] * 3)


def _row_tile(rows, width, itemsize=4, limit=1 << 20):
    for cand in (512, 256, 128, 64, 32, 16):
        if rows % cand == 0 and cand * width * itemsize <= limit:
            return cand
    raise ValueError((rows, width))


def _cast_into_full(name, a, layer, q, kind, after):
    L, r, c = a.shape
    tm = _row_tile(r, c)
    if kind == "col":
        full, o_spec = (r, N_CHIPS * c), pl.BlockSpec((tm, c), lambda i, q_ref: (i, q_ref[0]))
    else:
        full, o_spec = (N_CHIPS * r, c), pl.BlockSpec((tm, c), lambda i, q_ref: (q_ref[0] * (r // tm) + i, 0))

    def body(q_ref, a_ref, after_ref, o_ref):
        o_ref[...] = a_ref[0].astype(BF16)

    return pl.pallas_call(
        body, name=name,
        grid_spec=pltpu.PrefetchScalarGridSpec(
            num_scalar_prefetch=1, grid=(r // tm,),
            in_specs=[pl.BlockSpec((1, tm, c), lambda i, q_ref: (layer, i, 0)), ANY], out_specs=o_spec),
        out_shape=jax.ShapeDtypeStruct(full, BF16), compiler_params=_params(("parallel",)),
    )(q.reshape(1).astype(jnp.int32), a, after)


def _sum4(name, g, rcv, q, kind, n):
    r, c = rcv.shape[1:]
    tm = _row_tile(r, c)
    if kind == "col":
        g_spec = pl.BlockSpec((tm, n), lambda i, q_ref: (i, q_ref[0]))
    else:
        g_spec = pl.BlockSpec((tm, c), lambda i, q_ref: (q_ref[0] * (n // tm) + i, 0))

    def body(q_ref, g_ref, r_ref, o_ref):
        acc = g_ref[...].astype(F32)
        for j in range(3):
            acc = acc + r_ref[j].astype(F32)
        o_ref[...] = acc

    return pl.pallas_call(
        body, name=name,
        grid_spec=pltpu.PrefetchScalarGridSpec(
            num_scalar_prefetch=1, grid=(r // tm,),
            in_specs=[g_spec, pl.BlockSpec((3, tm, c), lambda i, q_ref: (0, i, 0))],
            out_specs=pl.BlockSpec((tm, c), lambda i, q_ref: (i, 0))),
        out_shape=jax.ShapeDtypeStruct((r, c), F32), compiler_params=_params(("parallel",)),
    )(q.reshape(1).astype(jnp.int32), g, rcv)


def _adamw_sum(name, w, m, v, layer, mine, theirs, prev):
    L, r, c = w.shape
    tm = _row_tile(r, c, limit=3 << 19)
    lay = pl.BlockSpec((1, tm, c), lambda i: (layer, i, 0))
    flat = pl.BlockSpec((tm, c), lambda i: (i, 0))
    n_prev = 0 if prev is None else 4

    def body(w_ref, m_ref, v_ref, a_ref, b_ref, *rest):
        g_ref, d_ref, mo_ref, vo_ref = rest[n_prev:]
        g = a_ref[...] + b_ref[...]
        g_ref[0] = g
        d_ref[0], mo_ref[0], vo_ref[0] = _adamw_val(w_ref[0], g, m_ref[0], v_ref[0])

    return pl.pallas_call(
        body, name=name, grid=(r // tm,),
        in_specs=[lay, lay, lay, flat, flat] + [ANY] * n_prev, out_specs=[lay] * 4,
        out_shape=[jax.ShapeDtypeStruct((L, r, c), F32)] * 4,
        input_output_aliases={5 + k: k for k in range(n_prev)},
        compiler_params=_params(("parallel",)),
    )(w, m, v, mine, theirs, *(prev or ()))


def _sum8(name, gathered, own=None):
    R, N = gathered.shape
    P = R // 8

    def body(g_ref, *rest):
        o_ref = rest[-1]
        me = 4 * lax.axis_index("x") + 2 * lax.axis_index("y") + lax.axis_index("c")
        acc = None
        for d in range(8):
            blk = g_ref[d * P:(d + 1) * P, :]
            if own is not None:
                blk = jnp.where(me == d, rest[0][...], blk)
            acc = blk if d == 0 else acc + blk
        o_ref[...] = acc

    return pl.pallas_call(body, name=name, out_shape=jax.ShapeDtypeStruct((P, N), F32),
                          compiler_params=pltpu.CompilerParams(vmem_limit_bytes=VMEM_LIMIT),
                          )(gathered, *(() if own is None else (own,)))


ANY = pl.BlockSpec(memory_space=pl.ANY)


def _mesh_pos():
    x, y, c = lax.axis_index("x"), lax.axis_index("y"), lax.axis_index("c")
    other_chips = [(1 - x, y), (x, 1 - y), (1 - x, 1 - y)]
    return x, y, c, other_chips


def _allgather8(name, blk):
    m_per, n = blk.shape

    def body(x_ref, out_ref, send_sems, recv_sems, local_sem):
        x, y, c, chips = _mesh_pos()
        me, sibling = (x, y, c), (x, y, 1 - c)

        def rows(px, py, pc):
            return out_ref.at[pl.ds((4 * px + 2 * py + pc) * m_per, m_per), :]

        def copy(k, block, to, src=None):
            return pltpu.make_async_remote_copy(
                src_ref=rows(*block) if src is None else src, dst_ref=rows(*block),
                send_sem=send_sems.at[k], recv_sem=recv_sems.at[k], device_id=to, device_id_type=MESH)

        mine = pltpu.make_async_copy(x_ref, rows(*me), local_sem)
        mine.start()
        first = [copy(0, me, sibling, src=x_ref)]
        first += [copy(1 + j, me, (*chip, c), src=x_ref) for j, chip in enumerate(chips)]
        for cp in first:
            cp.start()
        passed = [copy(4 + j, (*chip, c), sibling) for j, chip in enumerate(chips)]
        for j, chip in enumerate(chips):
            copy(1 + j, (*chip, c), me).wait_recv()
            passed[j].start()
        copy(0, sibling, me).wait_recv()
        for j, chip in enumerate(chips):
            copy(4 + j, (*chip, 1 - c), me).wait_recv()
        for cp in first + passed:
            cp.wait_send()
        mine.wait()

    return pl.pallas_call(
        body, name=name, out_shape=jax.ShapeDtypeStruct((8 * m_per, n), blk.dtype),
        in_specs=[pl.BlockSpec(memory_space=pltpu.VMEM)], out_specs=pl.BlockSpec(memory_space=pltpu.VMEM),
        scratch_shapes=[pltpu.SemaphoreType.DMA((7,)), pltpu.SemaphoreType.DMA((7,)), pltpu.SemaphoreType.DMA],
        compiler_params=pltpu.CompilerParams(vmem_limit_bytes=VMEM_LIMIT),
    )(blk)


BIG = dict(w_in=("col", "ab_w_in", 0), w_out=("row", "ab_w_out", 0), up0=("col", "ffn_up_w", 0),
           dn0=("row", "ffn_down_w", 0), pw1=("col", "conv_pw1_w", 0), pw2=("row", "conv_pw2_w", 0),
           up1=("col", "ffn_up_w", 1), dn1=("row", "ffn_down_w", 1))
N_CHIPS = 4
HBM = pl.BlockSpec(memory_space=pltpu.HBM)
SEM = pl.BlockSpec(memory_space=pltpu.SEMAPHORE)
EFFECT = pltpu.SideEffectType.DATAFLOW_SIDE_EFFECTING


def _region(kind, ref, q, n):
    if kind == "col":
        return ref.at[:, pl.ds(q * n, n)]
    return ref.at[pl.ds(q * n, n), :]


def _gather_plan(kind, n):
    def remote(src, land, pos):
        x, y, c, chips = pos
        mine = _region(kind, land, 2 * x + y, n)
        return [(mine, mine, (*chip, c)) for chip in chips]

    return ("gather", kind, n), remote


def _scatter_plan(kind, n):
    def remote(src, land, pos):
        _, _, c, chips = pos
        return [(_region(kind, src, 2 * chip[0] + chip[1], n), land.at[j], (*chip, c)) for j, chip in enumerate(chips)]

    return ("scatter", kind, n), remote


def _everyone_plan(rows):
    def remote(src, land, pos):
        x, y, c, _ = pos
        mine = land.at[pl.ds((4 * x + 2 * y + c) * rows, rows), :]
        flip = lambda v, bit: 1 - v if bit else v
        return [(src, mine, (flip(x, k & 4), flip(y, k & 2), flip(c, k & 1))) for k in range(1, 8)]

    return ("everyone", rows), remote


def _sibling_plan():
    def remote(src, land, pos):
        x, y, c, _ = pos
        return [(src, land, (x, y, 1 - c))]

    return ("sibling",), remote


def _split_start(name, items, after=None):
    n = len(items)
    plans = [it[2] for it in items]
    n_in = 2 * n + (after is not None)

    def body(*refs):
        srcs, lands = refs[:n], refs[n:2 * n]
        sends, recvs = refs[n_in:n_in + n], refs[n_in + n:n_in + 2 * n]
        token = refs[n_in + 4 * n]
        pos = _mesh_pos()
        for a, (_, remote) in enumerate(plans):
            for k, (s, d, dev) in enumerate(remote(srcs[a], lands[a], pos)):
                pltpu.make_async_remote_copy(src_ref=s, dst_ref=d, send_sem=sends[a].at[k], recv_sem=recvs[a].at[k],
                                             device_id=dev, device_id_type=MESH).start()
        token[...] = jnp.zeros_like(token)

    sems = [pltpu.SemaphoreType.DMA((it[3],)) for it in items]
    bufs = [pltpu.HBM(it[k].shape, it[k].dtype) for k in (0, 1) for it in items]
    outs = pl.pallas_call(
        body, name=name, out_shape=[*sems, *sems, *bufs, jax.ShapeDtypeStruct((8, LANES), F32)],
        in_specs=[HBM] * (2 * n) + [ANY] * (n_in - 2 * n),
        out_specs=[SEM] * (2 * n) + [HBM] * (2 * n) + [pl.BlockSpec(memory_space=pltpu.VMEM)],
        input_output_aliases={i: 2 * n + i for i in range(2 * n)},
        compiler_params=pltpu.CompilerParams(has_side_effects=EFFECT),
    )(*[pltpu.with_memory_space_constraint(it[k], pltpu.HBM) for k in (0, 1) for it in items],
      *(() if after is None else (after,)))
    state = [(items[a][2], items[a][3], outs[2 * n + a], outs[3 * n + a], outs[a], outs[n + a]) for a in range(n)]
    return state, outs[4 * n]


def _split_wait(name, state, after):
    n = len(state)

    def body(*refs):
        srcs, lands = refs[:n], refs[n:2 * n]
        sends, recvs = refs[2 * n:3 * n], refs[3 * n:4 * n]
        pos = _mesh_pos()
        for a, ((_, remote), *_) in enumerate(state):
            for k, (s, d, dev) in enumerate(remote(srcs[a], lands[a], pos)):
                cp = pltpu.make_async_remote_copy(src_ref=s, dst_ref=d, send_sem=sends[a].at[k], recv_sem=recvs[a].at[k],
                                                  device_id=dev, device_id_type=MESH)
                cp.wait_send()
                cp.wait_recv()

    bufs = [st[k] for k in (2, 3) for st in state]
    outs = pl.pallas_call(
        body, name=name, out_shape=[pltpu.HBM(b.shape, b.dtype) for b in bufs],
        in_specs=[HBM] * (2 * n) + [SEM] * (2 * n) + [ANY], out_specs=[HBM] * (2 * n),
        input_output_aliases={i: i for i in range(2 * n)},
        compiler_params=pltpu.CompilerParams(has_side_effects=EFFECT),
    )(*bufs, *[st[k] for k in (4, 5) for st in state], after)
    return outs[:n], outs[n:]


class _Weights:
    def __init__(self, w, q, after):
        unused = jnp.zeros((16, LANES), BF16)

        def item(name, after):
            kind, pname, layer = BIG[name]
            _, r, c = w[pname].shape
            land = _cast_into_full(f"cast_{name}", w[pname], layer, q, kind, after)
            return unused, land, _gather_plan(kind, c if kind == "col" else r), N_CHIPS - 1

        first, *rest = BIG
        state1, token1 = _split_start("gw_start_first", [item(first, after)], after)
        state2, self.token = _split_start("gw_start_rest", [item(name, token1) for name in rest], token1)
        self.pending = dict(zip(BIG, state1 + state2))
        self.ready = {}

    def get(self, name, after=None):
        if name not in self.ready:
            self.ready[name] = _split_wait(f"gw_wait_{name}", [self.pending.pop(name)], after)[1][0]
        return self.ready[name]


class _GradPipe:
    def __init__(self, q, w, m, v):
        self.q, self.w, self.m, self.v = q, w, m, v
        self.stage, self.results = {}, {}

    def start_small(self, dmod, small, after):
        self.small_names = [n for n in REPLICATED if n != "ada_b"] + list(SMALL_SHARDED)
        payload = [dmod.reshape(2, -1)] + [small[n] for n in self.small_names]
        self.small_shapes = [p.shape for p in payload]
        packed = _pack(payload)
        land = jnp.zeros((8 * PACK_ROWS, packed.shape[1]), F32)
        self.small_state = _split_start("ag_grads_start", [(packed, land, _everyone_plan(PACK_ROWS), 7)], after)[0]

    def wait_small(self, after):
        srcs, lands = _split_wait("ag_grads_wait", self.small_state, after)
        return srcs[0], lands[0]

    def scatter(self, group, grads, after=None):
        items = []
        for name, g in grads.items():
            kind = BIG[name][0]
            rows, cols = g.shape
            n = (cols if kind == "col" else rows) // N_CHIPS
            reg = (rows, n) if kind == "col" else (n, cols)
            items.append((g, lax.empty((N_CHIPS - 1, *reg), BF16), _scatter_plan(kind, n), N_CHIPS - 1))
        state, token = _split_start(f"gs_start_{group}", items, after)
        self.stage[group] = (list(grads), state)
        return token[0, 0]

    def collect(self, group, after):
        names, state = self.stage[group]
        srcs, lands = _split_wait(f"gs_wait_{group}", state, after)
        items = []
        for name, st, g, land in zip(names, state, srcs, lands):
            _, kind, n = st[0][0]
            part = _sum4(f"sum_{name}", g, land, self.q, kind, n)
            items.append((part, lax.empty(part.shape, F32), _sibling_plan(), 1))
        self.stage[group] = (names, _split_start(f"sw_start_{group}", items)[0])

    def finish(self, group, after):
        names, state = self.stage.pop(group)
        srcs, lands = _split_wait(f"sw_wait_{group}", state, after)
        for name, mine, theirs in zip(names, srcs, lands):
            _, pname, layer = BIG[name]
            self.results[pname] = _adamw_sum(f"adamw_{name}", self.w[pname], self.m[pname], self.v[pname], layer,
                                             mine, theirs, self.results.get(pname))


PACK_ROWS = 8


def _pack(arrays):
    flat = jnp.concatenate([a.reshape(-1) for a in arrays])
    n = flat.shape[0]
    padded = -(-n // (PACK_ROWS * LANES)) * (PACK_ROWS * LANES)
    return jnp.pad(flat, (0, padded - n)).reshape(PACK_ROWS, padded // PACK_ROWS)


def _unpack(packed, shapes):
    flat = packed.reshape(-1)
    out, off = [], 0
    for s in shapes:
        n = 1
        for d in s:
            n *= d
        out.append(flat[off:off + n].reshape(s))
        off += n
    return out


REPLICATED = ("ada_b", "norm_mix_g", "norm_ffn_g", "a_vnorm_g", "a_spatial_w", "a_spatial_b", "b_q_norm_g",
              "b_k_norm_g", "ffn_dw_b")
SMALL_SHARDED = ("conv_pw1_b", "conv_dw_w", "conv_dw_b", "conv_ln_g", "conv_ln_b", "conv_pw2_b", "ffn_dw_w")
WEIGHTS = ("ada_w", "ada_b", "norm_mix_g", "norm_ffn_g", "ab_w_in", "a_vnorm_g", "a_spatial_w", "a_spatial_b",
           "b_q_norm_g", "b_k_norm_g", "ab_w_out", "conv_pw1_w", "conv_pw1_b", "conv_dw_w", "conv_dw_b", "conv_ln_g",
           "conv_ln_b", "conv_pw2_w", "conv_pw2_b", "ffn_up_w", "ffn_dw_w", "ffn_dw_b", "ffn_down_w")

def kernel(x, c, positions, ada_w, ada_b, norm_mix_g, norm_ffn_g, ab_w_in, a_vnorm_g, a_spatial_w, a_spatial_b, b_q_norm_g, b_k_norm_g, ab_w_out, conv_pw1_w, conv_pw1_b, conv_dw_w, conv_dw_b, conv_ln_g, conv_ln_b, conv_pw2_w, conv_pw2_b, ffn_up_w, ffn_dw_w, ffn_dw_b, ffn_down_w, loss_target, m_ada_w, m_ada_b, m_norm_mix_g, m_norm_ffn_g, m_ab_w_in, m_a_vnorm_g, m_a_spatial_w, m_a_spatial_b, m_b_q_norm_g, m_b_k_norm_g, m_ab_w_out, m_conv_pw1_w, m_conv_pw1_b, m_conv_dw_w, m_conv_dw_b, m_conv_ln_g, m_conv_ln_b, m_conv_pw2_w, m_conv_pw2_b, m_ffn_up_w, m_ffn_dw_w, m_ffn_dw_b, m_ffn_down_w, v_ada_w, v_ada_b, v_norm_mix_g, v_norm_ffn_g, v_ab_w_in, v_a_vnorm_g, v_a_spatial_w, v_a_spatial_b, v_b_q_norm_g, v_b_k_norm_g, v_ab_w_out, v_conv_pw1_w, v_conv_pw1_b, v_conv_dw_w, v_conv_dw_b, v_conv_ln_g, v_conv_ln_b, v_conv_pw2_w, v_conv_pw2_b, v_ffn_up_w, v_ffn_dw_w, v_ffn_dw_b, v_ffn_down_w):
    w = dict(ada_w=ada_w, ada_b=ada_b, norm_mix_g=norm_mix_g, norm_ffn_g=norm_ffn_g, ab_w_in=ab_w_in, a_vnorm_g=a_vnorm_g, a_spatial_w=a_spatial_w, a_spatial_b=a_spatial_b, b_q_norm_g=b_q_norm_g, b_k_norm_g=b_k_norm_g, ab_w_out=ab_w_out, conv_pw1_w=conv_pw1_w, conv_pw1_b=conv_pw1_b, conv_dw_w=conv_dw_w, conv_dw_b=conv_dw_b, conv_ln_g=conv_ln_g, conv_ln_b=conv_ln_b, conv_pw2_w=conv_pw2_w, conv_pw2_b=conv_pw2_b, ffn_up_w=ffn_up_w, ffn_dw_w=ffn_dw_w, ffn_dw_b=ffn_dw_b, ffn_down_w=ffn_down_w)
    m = dict(ada_w=m_ada_w, ada_b=m_ada_b, norm_mix_g=m_norm_mix_g, norm_ffn_g=m_norm_ffn_g, ab_w_in=m_ab_w_in, a_vnorm_g=m_a_vnorm_g, a_spatial_w=m_a_spatial_w, a_spatial_b=m_a_spatial_b, b_q_norm_g=m_b_q_norm_g, b_k_norm_g=m_b_k_norm_g, ab_w_out=m_ab_w_out, conv_pw1_w=m_conv_pw1_w, conv_pw1_b=m_conv_pw1_b, conv_dw_w=m_conv_dw_w, conv_dw_b=m_conv_dw_b, conv_ln_g=m_conv_ln_g, conv_ln_b=m_conv_ln_b, conv_pw2_w=m_conv_pw2_w, conv_pw2_b=m_conv_pw2_b, ffn_up_w=m_ffn_up_w, ffn_dw_w=m_ffn_dw_w, ffn_dw_b=m_ffn_dw_b, ffn_down_w=m_ffn_down_w)
    v = dict(ada_w=v_ada_w, ada_b=v_ada_b, norm_mix_g=v_norm_mix_g, norm_ffn_g=v_norm_ffn_g, ab_w_in=v_ab_w_in, a_vnorm_g=v_a_vnorm_g, a_spatial_w=v_a_spatial_w, a_spatial_b=v_a_spatial_b, b_q_norm_g=v_b_q_norm_g, b_k_norm_g=v_b_k_norm_g, ab_w_out=v_ab_w_out, conv_pw1_w=v_conv_pw1_w, conv_pw1_b=v_conv_pw1_b, conv_dw_w=v_conv_dw_w, conv_dw_b=v_conv_dw_b, conv_ln_g=v_conv_ln_g, conv_ln_b=v_conv_ln_b, conv_pw2_w=v_conv_pw2_w, conv_pw2_b=v_conv_pw2_b, ffn_up_w=v_ffn_up_w, ffn_dw_w=v_ffn_dw_w, ffn_dw_b=v_ffn_dw_b, ffn_down_w=v_ffn_down_w)
    S, D = x.shape[1], x.shape[2]
    xi, yi, ci = lax.axis_index("x"), lax.axis_index("y"), lax.axis_index("c")
    q = 2 * xi + yi
    b = 2 * q + ci
    take_dev = lambda g: g.reshape(8, PACK_ROWS, -1)

    c_all = _allgather8("ag_c", c.reshape(PACK_ROWS, D // PACK_ROWS)).reshape(8, D)
    n_ada = ada_w.shape[2]
    mod_sh = _ada_fwd("ada_fwd", c_all, ada_w, lax.dynamic_slice_in_dim(ada_b, q * n_ada, n_ada, axis=1))
    sh_shapes = [mod_sh.shape] + [w[n].shape for n in SMALL_SHARDED]
    gathered_mod = _allgather8("ag_mod", _pack([mod_sh] + [w[n] for n in SMALL_SHARDED]))
    per_chip = [_unpack(blk, sh_shapes) for blk in take_dev(gathered_mod)[0::2]]
    mod_g = jnp.stack([pc[0] for pc in per_chip])
    mod_mine = lax.dynamic_index_in_dim(mod_g, b, axis=2, keepdims=False)
    mod = mod_mine.transpose(1, 0, 2).reshape(2, 6, D)
    sp = {n: jnp.concatenate([pc[1 + i] for pc in per_chip], axis=-1) for i, n in enumerate(SMALL_SHARDED)}
    sp.update({n: w[n] for n in REPLICATED if n != "ada_b"})

    wb = _Weights(w, q, gathered_mod)
    mod = mod + wb.token[0, 0]

    pipe = _GradPipe(q, w, m, v)
    lossv, grad_x, late = _local_step(x[0], loss_target[0], positions[0], mod, wb, sp, pipe)
    loss = lax.psum(0.5 * jnp.sum(lossv) / D, ("x", "y", "c"))

    own, gathered = pipe.wait_small(grad_x)
    totals = _unpack(_sum8("sum_grads", gathered, own), pipe.small_shapes)
    grads = dict(zip(["ada_b"] + pipe.small_names, totals))
    late_g = _allgather8("ag_late", _pack(list(late)))
    late_tot = _unpack(_sum8("sum_late", late_g), [(3, D)])[0]
    grads["ada_b"] = grads["ada_b"].at[0, :2 * D].add(late_tot[:2].reshape(-1))
    grads["norm_mix_g"] = grads["norm_mix_g"].at[0].add(late_tot[2])
    for n in SMALL_SHARDED:
        n_sh = w[n].shape[-1]
        grads[n] = lax.dynamic_slice_in_dim(grads[n], q * n_sh, n_sh, axis=grads[n].ndim - 1)
    dmod_of = lambda packed: packed.reshape(packed.shape[0] // PACK_ROWS, -1)[:, :2 * 6 * D].reshape(-1, 2, 6 * D)
    dmod_all = jnp.where((jnp.arange(8) == b)[:, None, None], dmod_of(own), dmod_of(gathered))
    late_all = take_dev(late_g).reshape(8, -1)[:, :3 * D].reshape(8, 3, D)
    dmod_all = dmod_all.at[:, 0, :2 * D].add(late_all[:, :2].reshape(8, 2 * D))
    dmod_sh = lax.dynamic_slice_in_dim(dmod_all, q * n_ada, n_ada, axis=2).transpose(1, 0, 2)

    pipe.finish("g3", dmod_sh)
    pipe.collect("g4", dmod_sh)
    grads["ada_w"], delta_ada, m_ada, v_ada = _ada_update("ada_update", c_all, dmod_sh, ada_w, m_ada_w, v_ada_w)
    delta, new_m, new_v = dict(ada_w=delta_ada), dict(ada_w=m_ada), dict(ada_w=v_ada)
    rest = list(REPLICATED) + list(SMALL_SHARDED)
    rest_shapes = [w[n].shape for n in rest]
    outs = _adamw("adamw_small", *[_pack([src[n].reshape(w[n].shape) for n in rest]) for src in (w, grads, m, v)])
    for tgt, packed in zip((delta, new_m, new_v), outs):
        tgt.update(dict(zip(rest, _unpack(packed, rest_shapes))))
    for n in rest:
        grads[n] = grads[n].reshape(w[n].shape)
    pipe.finish("g4", outs[0])
    for n, res in pipe.results.items():
        grads[n], delta[n], new_m[n], new_v[n] = res

    return (loss, grad_x[None], *[grads[n] for n in WEIGHTS], *[delta[n] for n in WEIGHTS],
            *[new_m[n] for n in WEIGHTS], *[new_v[n] for n in WEIGHTS])
```

```python
import functools

import jax
import jax.numpy as jnp
from jax import lax
from jax.experimental import pallas as pl
from jax.experimental.pallas import tpu as pltpu

F32, BF16 = jnp.float32, jnp.bfloat16
EPS = 1e-6
NEG = -1e30
ROPE_THETA = 10000.0
LANES = 128
VMEM_LIMIT = 56 * 1024 * 1024
ADAM_LR, ADAM_B1, ADAM_B2, ADAM_EPS, ADAM_WD, ADAM_STEP = 0.001, 0.9, 0.999, 1e-08, 0.01, 10
MESH = pl.DeviceIdType.MESH


def _params(sem):
    return pltpu.CompilerParams(dimension_semantics=sem, vmem_limit_bytes=VMEM_LIMIT)


_DN = {"nn": (((1,), (0,)), ((), ())), "nt": (((1,), (1,)), ((), ())), "tn": (((0,), (0,)), ((), ()))}


def _matmul(name, a, b, mode, out_dtype, tm=512, tn=512, tk=1024, bias=None, after=None):
    if mode == "nn":
        (M, K), N = a.shape, b.shape[1]
    elif mode == "nt":
        (M, K), N = a.shape, b.shape[0]
    else:
        (K, M), N = a.shape, b.shape[1]
    tm, tn, tk = min(tm, M), min(tn, N), min(tk, K)
    assert M % tm == 0 and N % tn == 0 and K % tk == 0, (name, M, N, K, tm, tn, tk)
    nk = K // tk
    if mode == "tn":
        a_spec = pl.BlockSpec((tk, tm), lambda i, j, k: (k, i))
    else:
        a_spec = pl.BlockSpec((tm, tk), lambda i, j, k: (i, k))
    if mode == "nt":
        b_spec = pl.BlockSpec((tn, tk), lambda i, j, k: (j, k))
    else:
        b_spec = pl.BlockSpec((tk, tn), lambda i, j, k: (k, j))
    in_specs, args = [a_spec, b_spec], [a, b]
    if bias is not None:
        in_specs.append(pl.BlockSpec((1, tn), lambda i, j, k: (0, j)))
        args.append(bias)
    if after is not None:
        in_specs.append(pl.BlockSpec(memory_space=pl.ANY))
        args.append(after)
    n_in = len(args)

    def body(*refs):
        a_ref, b_ref, o_ref = refs[0], refs[1], refs[n_in]
        p = lax.dot_general(a_ref[...], b_ref[...], _DN[mode], preferred_element_type=F32)

        def finish(acc):
            if bias is not None:
                acc = acc + refs[2][...]
            o_ref[...] = acc.astype(o_ref.dtype)

        if nk == 1:
            finish(p)
        else:
            acc_ref = refs[n_in + 1]
            k = pl.program_id(2)

            @pl.when(k == 0)
            def _():
                acc_ref[...] = p

            @pl.when(k > 0)
            def _():
                acc_ref[...] += p

            @pl.when(k == nk - 1)
            def _():
                finish(acc_ref[...])

    return pl.pallas_call(
        body, name=name, grid=(M // tm, N // tn, nk), in_specs=in_specs,
        out_specs=pl.BlockSpec((tm, tn), lambda i, j, k: (i, j)),
        out_shape=jax.ShapeDtypeStruct((M, N), out_dtype),
        scratch_shapes=[pltpu.VMEM((tm, tn), F32)] if nk > 1 else [],
        compiler_params=_params(("parallel", "parallel", "arbitrary")),
    )(*args)


def _rowcall(name, body, nrows, tm, ins, outs, scratch=()):
    nblk = nrows // tm
    assert nrows % tm == 0

    def spec(kind, shape):
        k = kind[0]
        if k == "row":
            cw, cb = kind[1] or shape[-1], kind[2]
            return pl.BlockSpec((tm, cw), lambda i: (i, cb))
        if k == "prev":
            hb, cw, cb = kind[1], kind[2] or shape[-1], kind[3]
            r = tm // hb
            return pl.BlockSpec((hb, cw), lambda i: (jnp.maximum(i * r - 1, 0), cb))
        if k == "next":
            hb, cw, cb = kind[1], kind[2] or shape[-1], kind[3]
            r, last = tm // hb, nrows // hb - 1
            return pl.BlockSpec((hb, cw), lambda i: (jnp.minimum((i + 1) * r, last), cb))
        if k == "off":
            off, cw, cb = kind[1], kind[2] or shape[-1], kind[3]
            return pl.BlockSpec((tm, cw), lambda i: (jnp.clip(i + off, 0, nblk - 1), cb))
        if k == "cls":
            dil, off = kind[1], kind[2]
            return pl.BlockSpec((dil, tm // dil, shape[-1]), lambda i: (0, jnp.clip(i + off, 0, nblk - 1), 0))
        nd = len(shape)
        return pl.BlockSpec(tuple(shape), lambda i: (0,) * nd)

    has_acc = any(o[2][0] == "acc" for o in outs)
    return pl.pallas_call(
        body, name=name, grid=(nblk,),
        in_specs=[spec(kind, a.shape) for a, kind in ins],
        out_specs=[spec(kind, shape) for shape, _, kind in outs],
        out_shape=[jax.ShapeDtypeStruct(tuple(shape), dt) for shape, dt, _ in outs],
        scratch_shapes=list(scratch),
        compiler_params=_params(("arbitrary",) if has_acc else ("parallel",)),
    )(*[a for a, _ in ins])


ROW = ("row", None, 0)
FULL = ("full",)
ACC = ("acc",)


def _colsum(x):
    return jnp.sum(x, axis=0, keepdims=True)


def _acc_add(i, ref, val, rows=None):
    idx = (slice(None),) * len(ref.shape) if rows is None else rows

    @pl.when(i == 0)
    def _():
        ref[idx] = val

    @pl.when(i > 0)
    def _():
        ref[idx] = ref[idx] + val


def _sigmoid(x):
    return 1.0 / (1.0 + jnp.exp(-x))


def _gelu(x):
    return 0.5 * x * (1.0 + lax.erf(x * (2.0 ** -0.5)))


def _gelu_grad(x):
    return 0.5 * (1.0 + lax.erf(x * (2.0 ** -0.5))) + x * jnp.exp(-0.5 * x * x) * ((2.0 * jnp.pi) ** -0.5)


SUBLANES = 8


def _phases(ext, sign):
    n = ext.shape[0]
    return [ext if b == 0 else pltpu.roll(ext, b if sign > 0 else n - b, axis=0) for b in range(SUBLANES)]


def _shift_prev(phases, s, hb):
    a, b = divmod(s, SUBLANES)
    return phases[b][hb - SUBLANES * a:phases[b].shape[0] - SUBLANES * a]


def _shift_next(phases, s, tm):
    a, b = divmod(s, SUBLANES)
    return phases[b][SUBLANES * a:SUBLANES * a + tm]


def _rms_mod_val(x, g, sc, sh):
    r = lax.rsqrt(jnp.mean(x * x, axis=-1, keepdims=True) + EPS)
    return x * r * g * (1.0 + sc) + sh


def _mod_first(name, x, g, sc, sh, tm=256):
    S, D = x.shape

    def body(x_ref, g_ref, sc_ref, sh_ref, h_ref):
        h_ref[...] = _rms_mod_val(x_ref[...], g_ref[...], sc_ref[...], sh_ref[...]).astype(BF16)

    return _rowcall(name, body, S, tm, [(x, ROW), (g, FULL), (sc, FULL), (sh, FULL)], [((S, D), BF16, ROW)])[0]


def _resid_mod(name, x, y, gate, g, sc, sh, tm=256):
    S, D = x.shape

    def body(x_ref, y_ref, gate_ref, g_ref, sc_ref, sh_ref, xo_ref, h_ref):
        xn = x_ref[...] + gate_ref[...] * y_ref[...]
        xo_ref[...] = xn
        h_ref[...] = _rms_mod_val(xn, g_ref[...], sc_ref[...], sh_ref[...]).astype(BF16)

    return _rowcall(name, body, S, tm,
                    [(x, ROW), (y, ROW), (gate, FULL), (g, FULL), (sc, FULL), (sh, FULL)],
                    [((S, D), F32, ROW), ((S, D), BF16, ROW)])


def _gate_bwd_val(i, d, y_ref, gate_ref, dy_ref, dg_ref, db_ref):
    dy = d * gate_ref[...]
    dy_ref[...] = dy.astype(BF16)
    _acc_add(i, dg_ref, _colsum(d * y_ref[...]))
    _acc_add(i, db_ref, _colsum(dy))


GATE_OUTS = lambda S, D: [((S, D), BF16, ROW), ((1, D), F32, ACC), ((1, D), F32, ACC)]


def _loss_head(name, x, y, gate, tgt, tm=256):
    S, D = x.shape

    def body(x_ref, y_ref, gate_ref, t_ref, dx_ref, l_ref, dy_ref, dg_ref, db_ref):
        i = pl.program_id(0)
        err = x_ref[...] + gate_ref[...] * y_ref[...] - t_ref[...]
        d = err * (1.0 / D)
        dx_ref[...] = d
        _acc_add(i, l_ref, _colsum(err * err))
        _gate_bwd_val(i, d, y_ref, gate_ref, dy_ref, dg_ref, db_ref)

    return _rowcall(name, body, S, tm, [(x, ROW), (y, ROW), (gate, FULL), (tgt, ROW)],
                    [((S, D), F32, ROW), ((1, D), F32, ACC)] + GATE_OUTS(S, D))


def _mod_bwd(name, dxo, dh, x, g, sc, y=None, gate=None, tm=256):
    S, D = x.shape
    gated = y is not None

    def body(d_ref, dh_ref, x_ref, g_ref, sc_ref, *rest):
        dx_ref, dsh_ref, dsc_ref, dg_ref = rest[2 * gated:2 * gated + 4]
        i = pl.program_id(0)
        xv, dh_v, gv = x_ref[...], dh_ref[...], g_ref[...]
        r = lax.rsqrt(jnp.mean(xv * xv, axis=-1, keepdims=True) + EPS)
        n = xv * r
        _acc_add(i, dsh_ref, _colsum(dh_v))
        _acc_add(i, dsc_ref, _colsum(dh_v * (n * gv)))
        dy = dh_v * (1.0 + sc_ref[...])
        _acc_add(i, dg_ref, _colsum(dy * n))
        dn = dy * gv
        dx = d_ref[...] + r * (dn - n * jnp.mean(dn * n, axis=-1, keepdims=True))
        dx_ref[...] = dx
        if gated:
            _gate_bwd_val(i, dx, rest[0], rest[1], *rest[6:9])

    ins = [(dxo, ROW), (dh, ROW), (x, ROW), (g, FULL), (sc, FULL)] + ([(y, ROW), (gate, FULL)] if gated else [])
    outs = [((S, D), F32, ROW), ((1, D), F32, ACC), ((1, D), F32, ACC), ((1, D), F32, ACC)]
    return _rowcall(name, body, S, tm, ins, outs + (GATE_OUTS(S, D) if gated else []))


HB16 = 16


def _conv3_val(ph, w, b, hb):
    return w[2:3] * _shift_prev(ph, 0, hb) + w[1:2] * _shift_prev(ph, 1, hb) + w[0:1] * _shift_prev(ph, 2, hb) + b


def _halo_first(halo_ref, tile_ref, live):
    return _phases(jnp.concatenate([halo_ref[...].astype(F32) * live, tile_ref[...].astype(F32)], axis=0), 1)


def _glu3_fwd(name, u, w, b, tm=128):
    S, F2 = u.shape
    Fh = F2 // 2

    def body(ua_ref, ub_ref, ha_ref, hb_ref, w_ref, b_ref, o_ref, z_ref):
        live = (pl.program_id(0) > 0).astype(F32)
        wv, bv = w_ref[...], b_ref[...]
        za = _conv3_val(_halo_first(ha_ref, ua_ref, live), wv[:, :Fh], bv[:, :Fh], HB16)
        zb = _conv3_val(_halo_first(hb_ref, ub_ref, live), wv[:, Fh:], bv[:, Fh:], HB16)
        o_ref[...] = (za * _sigmoid(za) * zb).astype(BF16)
        z_ref[:, :Fh] = za.astype(BF16)
        z_ref[:, Fh:] = zb.astype(BF16)

    return _rowcall(name, body, S, tm,
                    [(u, ("row", Fh, 0)), (u, ("row", Fh, 1)), (u, ("prev", HB16, Fh, 0)), (u, ("prev", HB16, Fh, 1)),
                     (w, FULL), (b, FULL)],
                    [((S, Fh), BF16, ROW), ((S, F2), BF16, ROW)])


def _glu3_bwd(name, z, dhm, tm=128):
    S, F2 = z.shape
    Fh = F2 // 2

    def body(za_ref, zb_ref, d_ref, dz_ref, db_ref):
        i = pl.program_id(0)
        za, zb, d = za_ref[...].astype(F32), zb_ref[...].astype(F32), d_ref[...]
        sg = _sigmoid(za)
        da = d * zb * (sg * (1.0 + za * (1.0 - sg)))
        db = d * (za * sg)
        dz_ref[:, :Fh] = da.astype(BF16)
        dz_ref[:, Fh:] = db.astype(BF16)
        _acc_add(i, db_ref, jnp.concatenate([_colsum(da), _colsum(db)], axis=1))

    return _rowcall(name, body, S, tm, [(z, ("row", Fh, 0)), (z, ("row", Fh, 1)), (dhm, ROW)],
                    [((S, F2), BF16, ROW), ((1, F2), F32, ACC)])


def _conv3_bwd(name, dz, u, w, tm=128):
    S, F2 = dz.shape
    nblk = S // tm
    K = w.shape[0]

    def body(d_ref, n_ref, u_ref, w_ref, o_ref, dw_ref):
        i = pl.program_id(0)
        live = (i < nblk - 1).astype(F32)
        ph = _phases(jnp.concatenate([d_ref[...].astype(F32), n_ref[...].astype(F32) * live], axis=0), -1)
        wv, uv = w_ref[...], u_ref[...].astype(F32)
        shifted = [_shift_next(ph, K - 1 - k, tm) for k in range(K)]
        o_ref[...] = functools.reduce(lambda a, t: a + t, [wv[k:k + 1] * shifted[k] for k in range(K)]).astype(BF16)
        for k in range(K):
            _acc_add(i, dw_ref, _colsum(uv * shifted[k]), rows=(slice(k, k + 1), slice(None)))

        @pl.when(i == 0)
        def _():
            dw_ref[K:, :] = jnp.zeros((dw_ref.shape[0] - K, F2), F32)

    return _rowcall(name, body, S, tm, [(dz, ROW), (dz, ("next", HB16, None, 0)), (u, ROW), (w, FULL)],
                    [((S, F2), BF16, ROW), ((SUBLANES, F2), F32, ACC)])


def _ffn_fwd(name, h, w_up, w_dn, dw_w, dw_b):
    u = _matmul(f"{name}_up", h, w_up, "nn", BF16, tm=2048)
    hm, z = _glu3_fwd(f"{name}_glu", u, dw_w, dw_b)
    f = _matmul(f"{name}_dn", hm, w_dn, "nn", F32, tm=1024, tn=1024, tk=w_dn.shape[0])
    return f, (u, hm, z)


def _ffn_bwd(name, dy, h, u, hm, z, w_up, w_dn, dw_w, dw_b):
    Fh = w_dn.shape[0]
    dhm = _matmul(f"{name}_ddn_x", dy, w_dn, "nt", F32, tm=1024, tn=Fh // 2)
    g_dn = _matmul(f"{name}_ddn_w", hm, dy, "tn", BF16, tm=Fh // 2, tn=1024, tk=2048)
    dz, g_dw_b = _glu3_bwd(f"{name}_dglu", z, dhm)
    du, taps = _conv3_bwd(f"{name}_dconv", dz, u, dw_w)
    g_up = _matmul(f"{name}_dup_w", h, du, "tn", BF16, tm=1024, tn=Fh // 2, tk=2048)
    dh = _matmul(f"{name}_dup_x", du, w_up, "nt", F32, tm=1024, tn=1024, tk=Fh)
    return dh, dict(up=g_up, dn=g_dn, dw_w=taps[0:dw_w.shape[0]], dw_b=g_dw_b)


HB32 = 32


def _glu31_fwd(name, p, w, b, tm=256):
    S, D2 = p.shape
    D = D2 // 2
    K = w.shape[0]

    def body(a_ref, g_ref, ha_ref, hg_ref, w_ref, b_ref, o_ref):
        live = (pl.program_id(0) > 0).astype(F32)
        y1 = a_ref[...] * _sigmoid(g_ref[...])
        ph = _phases(jnp.concatenate([ha_ref[...] * _sigmoid(hg_ref[...]) * live, y1], axis=0), 1)
        wv = w_ref[...]
        acc = b_ref[...] + wv[K - 1:K] * y1
        for k in range(K - 1):
            acc = acc + wv[k:k + 1] * _shift_prev(ph, K - 1 - k, HB32)
        o_ref[...] = acc

    return _rowcall(name, body, S, tm,
                    [(p, ("row", D, 0)), (p, ("row", D, 1)), (p, ("prev", HB32, D, 0)), (p, ("prev", HB32, D, 1)),
                     (w, FULL), (b, FULL)],
                    [((S, D), F32, ROW)])[0]


def _ln_silu_fwd(name, y2, g, b, tm=256):
    S, D = y2.shape

    def body(y_ref, g_ref, b_ref, o_ref):
        y = y_ref[...]
        mu = jnp.mean(y, axis=-1, keepdims=True)
        yc = y - mu
        rs = lax.rsqrt(jnp.mean(yc * yc, axis=-1, keepdims=True) + EPS)
        y3 = yc * rs * g_ref[...] + b_ref[...]
        o_ref[...] = (y3 * _sigmoid(y3)).astype(BF16)

    return _rowcall(name, body, S, tm, [(y2, ROW), (g, FULL), (b, FULL)], [((S, D), BF16, ROW)])[0]


def _ln_silu_bwd(name, y2, dy4, g, b, tm=256):
    S, D = y2.shape

    def body(y_ref, d_ref, g_ref, b_ref, o_ref, dg_ref, db_ref):
        i = pl.program_id(0)
        y, gv = y_ref[...], g_ref[...]
        mu = jnp.mean(y, axis=-1, keepdims=True)
        yc = y - mu
        rs = lax.rsqrt(jnp.mean(yc * yc, axis=-1, keepdims=True) + EPS)
        n = yc * rs
        y3 = n * gv + b_ref[...]
        sg = _sigmoid(y3)
        dy3 = d_ref[...] * (sg * (1.0 + y3 * (1.0 - sg)))
        _acc_add(i, db_ref, _colsum(dy3))
        _acc_add(i, dg_ref, _colsum(dy3 * n))
        dn = dy3 * gv
        o_ref[...] = rs * (dn - jnp.mean(dn, axis=-1, keepdims=True) - n * jnp.mean(dn * n, axis=-1, keepdims=True))

    return _rowcall(name, body, S, tm, [(y2, ROW), (dy4, ROW), (g, FULL), (b, FULL)],
                    [((S, D), F32, ROW), ((1, D), F32, ACC), ((1, D), F32, ACC)])


def _glu31_bwd(name, p, dy2, w, tm=256):
    S, D2 = p.shape
    D = D2 // 2
    K = w.shape[0]
    nblk = S // tm

    conv_rows, tap_rows, tap_group, tap_unroll = 16, SUBLANES, 4, 4

    def body(a_ref, g_ref, d_ref, dn_ref, w_ref, dp_ref, dw_ref, dcb_ref, dpb_ref, ph_d, y1_ref, dy1_ref, wb_ref):
        i = pl.program_id(0)
        live_next = (i < nblk - 1).astype(F32)
        a, sg, d = a_ref[...], _sigmoid(g_ref[...]), d_ref[...]
        y1_ref[...] = a * sg
        for b, ph in enumerate(_phases(jnp.concatenate([d, dn_ref[...] * live_next], axis=0), -1)):
            ph_d[b] = ph
        taps = [divmod(K - 1 - k, SUBLANES) for k in range(K)]

        @pl.when(i == 0)
        def _():
            for k in range(K):
                wb_ref[k] = jnp.broadcast_to(w_ref[k:k + 1, :], (SUBLANES, D))

        def conv_rows_at(rb, carry):
            r0 = pl.multiple_of(rb * conv_rows, conv_rows)
            accs = [jnp.zeros((SUBLANES, D), F32) for _ in range(conv_rows // SUBLANES)]
            for k, (rows8, phase) in enumerate(taps):
                wk = wb_ref[k]
                for u in range(len(accs)):
                    accs[u] = accs[u] + wk * ph_d[phase, pl.ds(r0 + SUBLANES * (rows8 + u), SUBLANES), :]
            for u, acc in enumerate(accs):
                dy1_ref[pl.ds(r0 + SUBLANES * u, SUBLANES), :] = acc
            return carry

        lax.fori_loop(0, tm // conv_rows, conv_rows_at, 0)
        for k0 in range(0, K, tap_group):
            group = taps[k0:k0 + tap_group]

            def tap_rows_at(rb, accs, group=group):
                for u in range(tap_unroll):
                    r0 = pl.multiple_of((rb * tap_unroll + u) * tap_rows, tap_rows)
                    yv = y1_ref[pl.ds(r0, tap_rows), :]
                    accs = tuple(acc + yv * ph_d[phase, pl.ds(r0 + SUBLANES * rows8, tap_rows), :]
                                 for acc, (rows8, phase) in zip(accs, group))
                return accs

            accs = lax.fori_loop(0, tm // (tap_rows * tap_unroll), tap_rows_at,
                                 tuple(jnp.zeros((tap_rows, D), F32) for _ in group))
            for j, acc in enumerate(accs):
                _acc_add(i, dw_ref, _colsum(acc), rows=(slice(k0 + j, k0 + j + 1), slice(None)))

        @pl.when(i == 0)
        def _():
            dw_ref[K:, :] = jnp.zeros((dw_ref.shape[0] - K, D), F32)

        _acc_add(i, dcb_ref, _colsum(d))
        dy1 = dy1_ref[...]
        da = dy1 * sg
        dg = dy1 * a * sg * (1.0 - sg)
        dp_ref[:, :D] = da.astype(BF16)
        dp_ref[:, D:] = dg.astype(BF16)
        _acc_add(i, dpb_ref, jnp.concatenate([_colsum(da), _colsum(dg)], axis=1))

    return _rowcall(name, body, S, tm,
                    [(p, ("row", D, 0)), (p, ("row", D, 1)), (dy2, ROW), (dy2, ("next", HB32, None, 0)), (w, FULL)],
                    [((S, D2), BF16, ROW), ((HB32, D), F32, ACC), ((1, D), F32, ACC), ((1, D2), F32, ACC)],
                    scratch=[pltpu.VMEM((SUBLANES, tm + HB32, D), F32), pltpu.VMEM((tm, D), F32),
                             pltpu.VMEM((tm, D), F32), pltpu.VMEM((K, SUBLANES, D), F32)])


CHUNK = 128
A_GROUPS = 4


def _group_ln(gv):
    ns, rss = [], []
    for g in range(A_GROUPS):
        xg = gv[:, g * LANES:(g + 1) * LANES]
        xc = xg - jnp.mean(xg, axis=-1, keepdims=True)
        rs = lax.rsqrt(jnp.mean(xc * xc, axis=-1, keepdims=True) + EPS)
        ns.append(xc * rs)
        rss.append(jnp.broadcast_to(rs, xg.shape))
    return jnp.concatenate(ns, axis=1), jnp.concatenate(rss, axis=1)


def _tril_mask():
    r = lax.broadcasted_iota(jnp.int32, (CHUNK, CHUNK), 0)
    c = lax.broadcasted_iota(jnp.int32, (CHUNK, CHUNK), 1)
    return r >= c


def _spatial(ws_ref, x, dn):
    mask = _tril_mask()
    rows = []
    for ci in range(x.shape[0] // CHUNK):
        cols = []
        for g in range(A_GROUPS):
            wm = jnp.where(mask, ws_ref[g], 0.0).astype(BF16)
            xb = x[ci * CHUNK:(ci + 1) * CHUNK, g * LANES:(g + 1) * LANES]
            cols.append(lax.dot_general(wm, xb, dn, preferred_element_type=F32))
        rows.append(jnp.concatenate(cols, axis=1))
    return jnp.concatenate(rows, axis=0)


def _mixa_fwd(name, z, vg, ws, bias_full, tm=256):
    S = z.shape[0]
    W = A_GROUPS * LANES

    def body(u_ref, v_ref, vg_ref, ws_ref, b_ref, o_ref):
        nh, _ = _group_ln(_gelu(v_ref[...]))
        vn = (nh * vg_ref[...]).astype(BF16)
        f = _spatial(ws_ref, vn, _DN["nn"]) + jnp.concatenate([b_ref[...]] * (tm // CHUNK), axis=0)
        o_ref[...] = (_gelu(u_ref[...]) * f).astype(BF16)

    return _rowcall(name, body, S, tm,
                    [(z, ("row", W, 0)), (z, ("row", W, 1)), (vg, FULL), (ws, FULL), (bias_full, FULL)],
                    [((S, W), BF16, ROW)])[0]


def _mixa_bwd(name, z, dyab, vg, ws, bias_full, tm=256):
    S = z.shape[0]
    W = A_GROUPS * LANES
    nch = tm // CHUNK

    def body(u_ref, v_ref, d_ref, vg_ref, ws_ref, b_ref, dz_ref, dws_ref, dbf_ref, dvg_ref):
        i = pl.program_id(0)
        u, v, d, vgv = u_ref[...], v_ref[...], d_ref[...], vg_ref[...]
        nh, rs = _group_ln(_gelu(v))
        vn = (nh * vgv).astype(BF16)
        f = _spatial(ws_ref, vn, _DN["nn"]) + jnp.concatenate([b_ref[...]] * nch, axis=0)
        dz_ref[:, :W] = (d * f * _gelu_grad(u)).astype(BF16)
        df = d * _gelu(u)
        dbf = df[0:CHUNK]
        for ci in range(1, nch):
            dbf = dbf + df[ci * CHUNK:(ci + 1) * CHUNK]
        _acc_add(i, dbf_ref, dbf)
        dfb = df.astype(BF16)
        mask = _tril_mask()
        for g in range(A_GROUPS):
            acc = jnp.zeros((CHUNK, CHUNK), F32)
            for ci in range(nch):
                blk = (slice(ci * CHUNK, (ci + 1) * CHUNK), slice(g * LANES, (g + 1) * LANES))
                acc = acc + lax.dot_general(dfb[blk], vn[blk], _DN["nt"], preferred_element_type=F32)
            _acc_add(i, dws_ref, jnp.where(mask, acc, 0.0)[None], rows=(slice(g, g + 1), slice(None), slice(None)))
        dvn = _spatial(ws_ref, dfb, _DN["tn"])
        _acc_add(i, dvg_ref, _colsum(dvn * nh))
        dnh = dvn * vgv
        parts = []
        for g in range(A_GROUPS):
            cs = slice(g * LANES, (g + 1) * LANES)
            dg_, ng = dnh[:, cs], nh[:, cs]
            parts.append(dg_ - jnp.mean(dg_, axis=-1, keepdims=True) - ng * jnp.mean(dg_ * ng, axis=-1, keepdims=True))
        dz_ref[:, W:] = (rs * jnp.concatenate(parts, axis=1) * _gelu_grad(v)).astype(BF16)

    return _rowcall(name, body, S, tm,
                    [(z, ("row", W, 0)), (z, ("row", W, 1)), (dyab, ("row", W, 0)), (vg, FULL), (ws, FULL),
                     (bias_full, FULL)],
                    [((S, 2 * W), BF16, ROW), ((A_GROUPS, CHUNK, CHUNK), F32, ACC), ((CHUNK, W), F32, ACC),
                     ((1, W), F32, ACC)])


HEAD = 64
N_HEADS = 8
BW = HEAD * N_HEADS
QB = 128
DILATIONS = (1, 4, 16)
QK_SCALE = HEAD ** -0.5


def _gsum64(x, ones_bd):
    x1 = x.astype(BF16)
    r1 = x - x1.astype(F32)
    x2 = r1.astype(BF16)
    x3 = (r1 - x2.astype(F32)).astype(BF16)
    dot = lambda t: jnp.dot(t, ones_bd, preferred_element_type=F32)
    return dot(x1) + dot(x2) + dot(x3)


def _swap32(x):
    n = x.shape[-1]
    up = pltpu.roll(x, n - HEAD // 2, axis=1)
    dn = pltpu.roll(x, HEAD // 2, axis=1)
    lane = lax.broadcasted_iota(jnp.int32, x.shape, 1)
    return jnp.where((lane % HEAD) < HEAD // 2, up, dn)


def _tile4(t):
    return jnp.concatenate([t] * (BW // LANES), axis=1)


def _stage_spec(tm):
    return pltpu.VMEM((BW // LANES, tm, LANES), F32)


def _to_classes(stage, x, dil):
    tm = x.shape[0]
    for j in range(BW // LANES):
        stage[j] = x[:, j * LANES:(j + 1) * LANES]
    return [jnp.concatenate([stage.at[j][pl.ds(r, tm // dil, stride=dil), :] for j in range(BW // LANES)], axis=1)
            for r in range(dil)]


def _from_classes(stage, cls, dil):
    rows = cls.shape[1]
    for r in range(dil):
        for j in range(BW // LANES):
            stage.at[j][pl.ds(r, rows, stride=dil), :] = cls[r, :, j * LANES:(j + 1) * LANES]
    return jnp.concatenate([stage[j] for j in range(BW // LANES)], axis=1)


def _cls_view(t, dil):
    return t if dil == 1 else t.reshape(dil, t.shape[0] // dil, t.shape[1])


def _cls_kind(dil, off=0):
    return ("off", off, None, 0) if dil == 1 else ("cls", dil, off)


def _cls_out(S, dil, dtype):
    return ((S, BW) if dil == 1 else (dil, S // dil, BW), dtype, _cls_kind(dil))


def _flat(t):
    return t.reshape(-1, t.shape[-1])


def _qkv_fwd(name, z, cos, sin, ones_bd, qg, kg, tm=256):
    S = z.shape[0]
    nd = len(DILATIONS)

    def body(q_ref, k_ref, v_ref, c_ref, s_ref, o_ref, qg_ref, kg_ref, *rest):
        outs, stage = rest[:3 * nd], rest[3 * nd]
        c, s, ob = _tile4(c_ref[...]), _tile4(s_ref[...]), o_ref[...]

        def norm_rope(x, g):
            r = lax.rsqrt(_gsum64(x * x, ob) * (1.0 / HEAD) + EPS)
            xn = x * r * g
            return xn * c + _swap32(xn) * s

        vals = [norm_rope(q_ref[...], qg_ref[...]) * QK_SCALE, norm_rope(k_ref[...], kg_ref[...]), v_ref[...]]
        for a, val in enumerate(vals):
            for b, dil in enumerate(DILATIONS):
                if dil == 1:
                    outs[nd * a + b][...] = val.astype(BF16)
                else:
                    for r, rows in enumerate(_to_classes(stage, val, dil)):
                        outs[nd * a + b][r] = rows.astype(BF16)

    outs = _rowcall(name, body, S, tm,
                    [(z, ("row", BW, 2)), (z, ("row", BW, 3)), (z, ("row", BW, 4)), (cos, ROW), (sin, ROW),
                     (ones_bd, FULL), (qg, FULL), (kg, FULL)],
                    [_cls_out(S, dil, BF16) for _ in range(3) for dil in DILATIONS], scratch=[_stage_spec(tm)])
    return [[_flat(outs[nd * a + b]) for a in range(3)] for b in range(nd)]


PAIR = 2 * HEAD


ATT_BLOCKS = 2
ATT_TM = ATT_BLOCKS * QB
ATT_PREV = ("prev", QB, None, 0)


def _key_rows(prev_ref, cur_ref, sb, ps):
    before = prev_ref[:, ps] if sb == 0 else cur_ref[(sb - 1) * QB:sb * QB, ps]
    return jnp.concatenate([before, cur_ref[sb * QB:(sb + 1) * QB, ps]], axis=0)


def _pair_scores(q_ref, kp_ref, kc_ref, sb, hp, half, seg_blocks):
    ps = slice(hp * PAIR, (hp + 1) * PAIR)
    mine = (lax.broadcasted_iota(jnp.int32, (1, PAIR), 1) >= HEAD) == (half == 1)
    qm = jnp.where(mine, q_ref[sb * QB:(sb + 1) * QB, ps], jnp.zeros((), BF16))
    kcat = _key_rows(kp_ref, kc_ref, sb, ps)
    s = lax.dot_general(qm, kcat, _DN["nt"], preferred_element_type=F32)
    qi = lax.broadcasted_iota(jnp.int32, (QB, 2 * QB), 0)
    kj = lax.broadcasted_iota(jnp.int32, (QB, 2 * QB), 1)
    has_prev = ((pl.program_id(0) * ATT_BLOCKS + sb) % seg_blocks) != 0
    valid = (kj >= qi) & (kj <= qi + QB) & ((kj >= QB) | has_prev)
    return mine, qm, kcat, s, valid


def _attn_fwd(name, q, k, v, dil):
    S = q.shape[0]
    seg_blocks = S // dil // QB

    def body(q_ref, kp_ref, kc_ref, vp_ref, vc_ref, o_ref, l_ref):
        for hp in range(N_HEADS // 2):
            ps = slice(hp * PAIR, (hp + 1) * PAIR)
            chains = [(sb, half) for sb in range(ATT_BLOCKS) for half in range(2)]
            sc = [_pair_scores(q_ref, kp_ref, kc_ref, sb, hp, half, seg_blocks) for sb, half in chains]
            ss = [jnp.where(valid, s, NEG) for _, _, _, s, valid in sc]
            ms = [jnp.max(s, axis=-1, keepdims=True) for s in ss]
            pv = [jnp.exp(s - m) for s, m in zip(ss, ms)]
            dens = [jnp.sum(p, axis=-1, keepdims=True) for p in pv]
            vcats = [_key_rows(vp_ref, vc_ref, sb, ps) for sb in range(ATT_BLOCKS)]
            outs = [jnp.dot(p.astype(BF16), vcats[sb], preferred_element_type=F32) / den
                    for p, den, (sb, _) in zip(pv, dens, chains)]
            lses = [jnp.broadcast_to(m + jnp.log(den), (QB, PAIR)) for m, den in zip(ms, dens)]
            for sb in range(ATT_BLOCKS):
                rows, upper = slice(sb * QB, (sb + 1) * QB), sc[2 * sb + 1][0]
                o_ref[rows, ps] = jnp.where(upper, outs[2 * sb + 1], outs[2 * sb])
                l_ref[rows, ps] = jnp.where(upper, lses[2 * sb + 1], lses[2 * sb])

    return _rowcall(name, body, S, ATT_TM, [(q, ROW), (k, ATT_PREV), (k, ROW), (v, ATT_PREV), (v, ROW)],
                    [((S, BW), F32, ROW)] * 2)


def _attn_bwd(name, q, k, v, do, lse, delta, dil):
    S = q.shape[0]
    seg_blocks = S // dil // QB

    def body(q_ref, kp_ref, kc_ref, vp_ref, vc_ref, do_ref, l_ref, dl_ref, dq_ref, dkc_ref, dkp_ref, dvc_ref, dvp_ref):
        for hp in range(N_HEADS // 2):
            ps = slice(hp * PAIR, (hp + 1) * PAIR)
            chains = [(sb, half) for sb in range(ATT_BLOCKS) for half in range(2)]
            rows = [slice(sb * QB, (sb + 1) * QB) for sb, _ in chains]
            cols = [hp * PAIR + half * HEAD for _, half in chains]
            sc = [_pair_scores(q_ref, kp_ref, kc_ref, sb, hp, half, seg_blocks) for sb, half in chains]
            pv = [jnp.where(valid, jnp.exp(s - l_ref[r, c:c + 1]), 0.0) for (_, _, _, s, valid), r, c in zip(sc, rows, cols)]
            vcats = [_key_rows(vp_ref, vc_ref, sb, ps) for sb in range(ATT_BLOCKS)]
            doms = [jnp.where(mine, do_ref[r, ps].astype(BF16), jnp.zeros((), BF16)) for (mine, *_), r in zip(sc, rows)]
            dps = [lax.dot_general(dom, vcats[sb], _DN["nt"], preferred_element_type=F32) for dom, (sb, _) in zip(doms, chains)]
            dss = [(p * (dp - dl_ref[r, c:c + 1])).astype(BF16) for p, dp, r, c in zip(pv, dps, rows, cols)]
            dqs = [jnp.dot(ds, kcat, preferred_element_type=F32) for ds, (_, _, kcat, _, _) in zip(dss, sc)]
            dks = [lax.dot_general(ds, qm, _DN["tn"], preferred_element_type=F32) for ds, (_, qm, *_) in zip(dss, sc)]
            dvs = [lax.dot_general(p.astype(BF16), dom, _DN["tn"], preferred_element_type=F32) for p, dom in zip(pv, doms)]
            for sb in range(ATT_BLOCKS):
                lo, hi = 2 * sb, 2 * sb + 1
                dk, dv = dks[lo] + dks[hi], dvs[lo] + dvs[hi]
                dq_ref[rows[lo], ps] = jnp.where(sc[hi][0], dqs[hi], dqs[lo])
                dkp_ref[rows[lo], ps] = dk[:QB]
                dkc_ref[rows[lo], ps] = dk[QB:]
                dvp_ref[rows[lo], ps] = dv[:QB]
                dvc_ref[rows[lo], ps] = dv[QB:]

    return _rowcall(name, body, S, ATT_TM,
                    [(q, ROW), (k, ATT_PREV), (k, ROW), (v, ATT_PREV), (v, ROW), do, (lse, ROW), (delta, ROW)],
                    [((S, BW), F32, ROW)] * 5)


def _merge_fwd(name, branches, tm=256):
    S = branches[0][0].shape[0]
    nd = len(DILATIONS)

    def body(*refs):
        ins, (y_ref, yb_ref), l_refs, stage = refs[:2 * nd], refs[2 * nd:2 * nd + 2], refs[2 * nd + 2:3 * nd + 2], refs[-1]
        os_, ls = [], []
        for b, dil in enumerate(DILATIONS):
            o, l = ins[2 * b][...], ins[2 * b + 1][...]
            os_.append(o if dil == 1 else _from_classes(stage, o, dil))
            ls.append(l if dil == 1 else _from_classes(stage, l, dil))
        m = functools.reduce(jnp.maximum, ls)
        es = [jnp.exp(l - m) for l in ls]
        den = functools.reduce(lambda a, e: a + e, es)
        y = functools.reduce(lambda a, t: a + t, [e * o for e, o in zip(es, os_)]) / den
        y_ref[...] = y
        yb_ref[...] = y.astype(BF16)
        lse = m + jnp.log(den)
        for b, dil in enumerate(DILATIONS):
            if dil == 1:
                l_refs[b][...] = lse
            else:
                for r, rows in enumerate(_to_classes(stage, lse, dil)):
                    l_refs[b][r] = rows

    ins = [(_cls_view(t, dil), _cls_kind(dil)) for pair, dil in zip(branches, DILATIONS) for t in pair]
    outs = _rowcall(name, body, S, tm, ins,
                    [((S, BW), F32, ROW), ((S, BW), BF16, ROW)] + [_cls_out(S, dil, F32) for dil in DILATIONS],
                    scratch=[_stage_spec(tm)])
    return outs[0], outs[1], [_flat(t) for t in outs[2:]]


def _delta(name, dyab, yb, ones_bd, tm=256):
    S = yb.shape[0]
    nd = len(DILATIONS)

    def body(d_ref, y_ref, o_ref, *rest):
        dl_refs, do_refs, stage = rest[:nd], rest[nd:2 * nd - 1], rest[-1]
        d = d_ref[...]
        dl = _gsum64(d * y_ref[...], o_ref[...])
        for b, dil in enumerate(DILATIONS):
            if dil == 1:
                dl_refs[b][...] = dl
            else:
                for r, rows in enumerate(_to_classes(stage, dl, dil)):
                    dl_refs[b][r] = rows
                for r, rows in enumerate(_to_classes(stage, d, dil)):
                    do_refs[b - 1][r] = rows.astype(BF16)

    outs = _rowcall(name, body, S, tm, [(dyab, ("row", BW, 1)), (yb, ROW), (ones_bd, FULL)],
                    [_cls_out(S, dil, F32) for dil in DILATIONS] + [_cls_out(S, dil, BF16) for dil in DILATIONS[1:]],
                    scratch=[_stage_spec(tm)])
    return [_flat(t) for t in outs[:nd]], [_flat(t) for t in outs[nd:]]


def _qkv_bwd(name, z, cos, sin, ones_bd, qg, kg, pieces):
    S = z.shape[0]
    nblk = S // QB

    def body(q_ref, k_ref, c_ref, s_ref, o_ref, qg_ref, kg_ref, *rest):
        pr, (dz_ref, dqg_ref, dkg_ref), stage = rest[:15], rest[15:18], rest[18]
        i = pl.program_id(0)
        c, s, ob = _tile4(c_ref[...]), _tile4(s_ref[...]), o_ref[...]
        dq = dk = dv = None
        for b, dil in enumerate(DILATIONS):
            a_q, a_kc, a_kp, a_vc, a_vp = [r[...] for r in pr[5 * b:5 * b + 5]]
            live = ((i + dil) < nblk).astype(F32)
            tq, tk, tv = a_q, a_kc + a_kp * live, a_vc + a_vp * live
            if dil > 1:
                tq, tk, tv = (_from_classes(stage, t, dil) for t in (tq, tk, tv))
            dq, dk, dv = (tq, tk, tv) if b == 0 else (dq + tq, dk + tk, dv + tv)

        def back(x, g, d_rot, acc_ref):
            r = lax.rsqrt(_gsum64(x * x, ob) * (1.0 / HEAD) + EPS)
            n = x * r
            dxn = d_rot * c + _swap32(d_rot * s)
            _acc_add(i, acc_ref, _colsum(dxn * n))
            dn = dxn * g
            return r * (dn - n * (_gsum64(dn * n, ob) * (1.0 / HEAD)))

        dz_ref[:, :BW] = back(q_ref[...], qg_ref[...], dq * QK_SCALE, dqg_ref).astype(BF16)
        dz_ref[:, BW:2 * BW] = back(k_ref[...], kg_ref[...], dk, dkg_ref).astype(BF16)
        dz_ref[:, 2 * BW:] = dv.astype(BF16)

    ins = [(z, ("row", BW, 2)), (z, ("row", BW, 3)), (cos, ROW), (sin, ROW), (ones_bd, FULL), (qg, FULL), (kg, FULL)]
    for piece, dil in zip(pieces, DILATIONS):
        a_q, a_kc, a_kp, a_vc, a_vp = (_cls_view(t, dil) for t in piece)
        own, prev = _cls_kind(dil), _cls_kind(dil, dil)
        ins += [(a_q, own), (a_kc, own), (a_kp, prev), (a_vc, own), (a_vp, prev)]
    return _rowcall(name, body, S, QB, ins,
                    [((S, 3 * BW), BF16, ROW), ((1, BW), F32, ACC), ((1, BW), F32, ACC)], scratch=[_stage_spec(QB)])


def _local_step(x0, tgt, pos, mod, wb, sp, pipe):
    S, D = x0.shape
    md = lambda l, j: mod[l, j:j + 1]
    sh_m, sc_m, g_m, sh_f, sc_f, g_f = ([md(l, j) for l in range(2)] for j in range(6))
    nm_g, nf_g = sp["norm_mix_g"], sp["norm_ffn_g"]

    inv_freq = 1.0 / (ROPE_THETA ** (jnp.arange(0, HEAD, 2, dtype=F32) / HEAD))
    ang = pos.astype(F32)[:, None] * inv_freq
    cs, sn = jnp.cos(ang), jnp.sin(ang)
    cos = jnp.concatenate([cs, cs, cs, cs], axis=1)
    sin = jnp.concatenate([-sn, sn, -sn, sn], axis=1)
    head_of = jnp.arange(BW) // HEAD
    ones_bd = (head_of[:, None] == head_of[None, :]).astype(BF16)
    qg = jnp.tile(sp["b_q_norm_g"].reshape(1, HEAD), (1, N_HEADS))
    kg = jnp.tile(sp["b_k_norm_g"].reshape(1, HEAD), (1, N_HEADS))
    vg = sp["a_vnorm_g"].reshape(1, A_GROUPS * LANES)
    ws = sp["a_spatial_w"][0]
    bias_full = jnp.repeat(sp["a_spatial_b"][0].T, LANES, axis=1)
    ffn_s = [(sp["ffn_dw_w"][l], sp["ffn_dw_b"][l:l + 1]) for l in range(2)]

    h0 = _mod_first("l0_mod", x0, nm_g[0:1], sc_m[0], sh_m[0])
    z = _matmul("l0_in", h0, wb.get("w_in", h0), "nn", F32, tm=2048)
    ya = _mixa_fwd("l0_mixa", z, vg, ws, bias_full)
    qkv = _qkv_fwd("l0_qkv", z, cos, sin, ones_bd, qg, kg)
    branches = [_attn_fwd(f"l0_att{dil}", *qkv[b], dil) for b, dil in enumerate(DILATIONS)]
    yb, yb16, lses = _merge_fwd("l0_merge", branches)
    yab = jnp.concatenate([ya, yb16], axis=1)
    y0 = _matmul("l0_out", yab, wb.get("w_out", yab), "nn", F32, tm=1024, tn=1024)
    x1, h1 = _resid_mod("l0_res1", x0, y0, g_m[0], nf_g[0:1], sc_f[0], sh_f[0])
    ffn_w = [(wb.get("up0", h1), wb.get("dn0", h1), *ffn_s[0])]
    f0, saved0 = _ffn_fwd("l0_ffn", h1, *ffn_w[0])
    x2, h2 = _resid_mod("l0_res2", x1, f0, g_f[0], nm_g[1:2], sc_m[1], sh_m[1])
    p = _matmul("l1_pw1", h2, wb.get("pw1", h2), "nn", F32, tm=2048, bias=sp["conv_pw1_b"])
    y2 = _glu31_fwd("l1_glu", p, sp["conv_dw_w"][0], sp["conv_dw_b"])
    y4 = _ln_silu_fwd("l1_ln", y2, sp["conv_ln_g"], sp["conv_ln_b"])
    y1 = _matmul("l1_pw2", y4, wb.get("pw2", y4), "nn", F32, tm=1024, tn=1024, bias=sp["conv_pw2_b"])
    x3, h3 = _resid_mod("l1_res1", x2, y1, g_m[1], nf_g[1:2], sc_f[1], sh_f[1])
    ffn_w.append((wb.get("up1", h3), wb.get("dn1", h3), *ffn_s[1]))
    f1, saved1 = _ffn_fwd("l1_ffn", h3, *ffn_w[1])
    dx4, lossv, dy, dgate_f1, _ = _loss_head("loss", x3, f1, g_f[1], tgt)

    dh, gf1 = _ffn_bwd("l1_ffn", dy, h3, *saved1, *ffn_w[1])
    tok = pipe.scatter("g1", dict(dn1=gf1["dn"], up1=gf1["up"]))
    dx3, dsh_f1, dsc_f1, dnf1, dy, dgate_m1, dpw2_b = _mod_bwd("l1_dmod2", dx4, dh, x3, nf_g[1:2], sc_f[1] + tok,
                                                             y1, g_m[1])
    dy4 = _matmul("l1_dpw2_x", dy, wb.get("pw2"), "nt", F32, tm=1024, tn=1024)
    g_pw2 = _matmul("l1_dpw2_w", y4, dy, "tn", BF16, tm=1024, tn=1024, tk=2048)
    dy2, dln_g, dln_b = _ln_silu_bwd("l1_dln", y2, dy4, sp["conv_ln_g"], sp["conv_ln_b"])
    dp, ddw_w, ddw_b, dpw1_b = _glu31_bwd("l1_dglu", p, dy2, sp["conv_dw_w"][0])
    g_pw1 = _matmul("l1_dpw1_w", h2, dp, "tn", BF16, tm=1024, tn=1024, tk=2048)
    tok = pipe.scatter("g2", dict(pw2=g_pw2, pw1=g_pw1))
    pipe.collect("g1", g_pw1)
    dh = _matmul("l1_dpw1_x", dp, wb.get("pw1"), "nt", F32, tm=1024, tn=1024, tk=2048)
    dx2, dsh_m1, dsc_m1, dnm1, dy, dgate_f0, _ = _mod_bwd("l1_dmod1", dx3, dh, x2, nm_g[1:2], sc_m[1] + tok,
                                                        f0, g_f[0])
    dh, gf0 = _ffn_bwd("l0_ffn", dy, h1, *saved0, *ffn_w[0])
    pipe.finish("g1", gf0["up"])
    pipe.collect("g2", gf0["up"])
    dx1, dsh_f0, dsc_f0, dnf0, dy, dgate_m0, _ = _mod_bwd("l0_dmod2", dx2, dh, x1, nf_g[0:1], sc_f[0], y0, g_m[0])
    dyab = _matmul("l0_dout_x", dy, wb.get("w_out"), "nt", F32, tm=1024, tn=1024)
    g_out = _matmul("l0_dout_w", yab, dy, "tn", BF16, tm=1024, tn=1024, tk=2048)
    tok = pipe.scatter("g3", dict(dn0=gf0["dn"], up0=gf0["up"], w_out=g_out))
    vg = vg + tok
    dza, dws, dbf, dvg = _mixa_bwd("l0_dmixa", z, dyab, vg, ws, bias_full)
    deltas, dos = _delta("l0_delta", dyab, yb, ones_bd)
    pieces = []
    for b, dil in enumerate(DILATIONS):
        do = (dyab, ("row", BW, 1)) if dil == 1 else (dos[b - 1], ROW)
        pieces.append(_attn_bwd(f"l0_datt{dil}", *qkv[b], do, lses[b], deltas[b], dil))
    dzb, dqg, dkg = _qkv_bwd("l0_dqkv", z, cos, sin, ones_bd, qg, kg, pieces)
    small_tok = _start_small_grads(pipe, dzb, sp, lossv, ((dgate_m0, dsh_f0, dsc_f0, dgate_f0),
                                                   (dsh_m1, dsc_m1, dgate_m1, dsh_f1, dsc_f1, dgate_f1)),
                                   (dnm1, dnf0, dnf1), dvg, dws, dbf, dqg, dkg, dpw1_b, ddw_w, ddw_b, dln_g, dln_b,
                                   dpw2_b, gf0, gf1)
    dz = jnp.concatenate([dza, dzb], axis=1)
    g_in = _matmul("l0_din_w", h0, dz, "tn", BF16, tm=1024, tn=1280, tk=2048, after=small_tok)
    tok = pipe.scatter("g4", dict(w_in=g_in))
    pipe.finish("g2", g_in)
    tok = tok + pipe.collect("g3", g_in)
    dh = _matmul("l0_din_x", dz, wb.get("w_in"), "nt", F32, tm=1024, tn=1024, tk=2560, after=small_tok)
    grad_x, dsh_m0, dsc_m0, dnm0 = _mod_bwd("l0_dmod1", dx1, dh, x0, nm_g[0:1], sc_m[0] + tok)
    return lossv, grad_x, (dsh_m0, dsc_m0, dnm0)


def _start_small_grads(pipe, after, sp, lossv, mods, norms, dvg, dws, dbf, dqg, dkg, dpw1_b, ddw_w, ddw_b, dln_g,
                       dln_b, dpw2_b, gf0, gf1):
    (dgate_m0, dsh_f0, dsc_f0, dgate_f0), (dsh_m1, dsc_m1, dgate_m1, dsh_f1, dsc_f1, dgate_f1) = mods
    dnm1, dnf0, dnf1 = norms
    zero = jnp.zeros_like(dnm1)
    dmod = jnp.stack([jnp.concatenate([zero, zero, dgate_m0, dsh_f0, dsc_f0, dgate_f0], axis=0),
                      jnp.concatenate([dsh_m1, dsc_m1, dgate_m1, dsh_f1, dsc_f1, dgate_f1], axis=0)])
    small = dict(
        norm_mix_g=jnp.concatenate([zero, dnm1], axis=0),
        norm_ffn_g=jnp.concatenate([dnf0, dnf1], axis=0),
        a_vnorm_g=dvg.reshape(1, A_GROUPS, LANES),
        a_spatial_w=dws[None],
        a_spatial_b=dbf.reshape(CHUNK, A_GROUPS, LANES).sum(-1).T[None],
        b_q_norm_g=dqg.reshape(N_HEADS, HEAD).sum(0)[None],
        b_k_norm_g=dkg.reshape(N_HEADS, HEAD).sum(0)[None],
        conv_pw1_b=dpw1_b, conv_dw_w=ddw_w[None, :sp["conv_dw_w"].shape[1]], conv_dw_b=ddw_b,
        conv_ln_g=dln_g, conv_ln_b=dln_b, conv_pw2_b=dpw2_b,
        ffn_dw_w=jnp.stack([gf0["dw_w"], gf1["dw_w"]]),
        ffn_dw_b=jnp.concatenate([gf0["dw_b"], gf1["dw_b"]], axis=0),
    )
    return pipe.start_small(dmod, lossv, small, after)


ADA_TN = 512


def _ada_fwd(name, c_all, ada_w, ada_b_sh):
    L, D, N = ada_w.shape
    B = c_all.shape[0]

    def body(c_ref, w_ref, b_ref, o_ref):
        cv = c_ref[...]
        ca = (cv * _sigmoid(cv)).astype(BF16)
        o_ref[0] = jnp.dot(ca, w_ref[0].astype(BF16), preferred_element_type=F32) + b_ref[0]

    return pl.pallas_call(
        body, name=name, grid=(L, N // ADA_TN),
        in_specs=[pl.BlockSpec((B, D), lambda l, j: (0, 0)), pl.BlockSpec((1, D, ADA_TN), lambda l, j: (l, 0, j)),
                  pl.BlockSpec((1, 1, ADA_TN), lambda l, j: (l, 0, j))],
        out_specs=pl.BlockSpec((1, B, ADA_TN), lambda l, j: (l, 0, j)),
        out_shape=jax.ShapeDtypeStruct((L, B, N), F32),
        compiler_params=_params(("parallel", "parallel")),
    )(c_all, ada_w, ada_b_sh.reshape(L, 1, N))


def _adamw_val(w, g, m, v):
    m2 = ADAM_B1 * m + (1.0 - ADAM_B1) * g
    v2 = ADAM_B2 * v + (1.0 - ADAM_B2) * (g * g)
    m_hat = m2 / (1.0 - ADAM_B1 ** ADAM_STEP)
    v_hat = v2 / (1.0 - ADAM_B2 ** ADAM_STEP)
    delta = -ADAM_LR * (m_hat / (jnp.sqrt(v_hat) + ADAM_EPS) + ADAM_WD * w)
    return delta, m2, v2


def _ada_update(name, c_all, dmod_sh, w, m, v):
    L, D, N = w.shape
    B = c_all.shape[0]

    def body(c_ref, d_ref, w_ref, m_ref, v_ref, g_ref, dl_ref, mo_ref, vo_ref):
        cv = c_ref[...]
        ca = (cv * _sigmoid(cv)).astype(BF16)
        g = lax.dot_general(ca, d_ref[0].astype(BF16), _DN["tn"], preferred_element_type=F32)
        g_ref[0] = g
        dl_ref[0], mo_ref[0], vo_ref[0] = _adamw_val(w_ref[0], g, m_ref[0], v_ref[0])

    wspec = pl.BlockSpec((1, D, ADA_TN), lambda l, j: (l, 0, j))
    return pl.pallas_call(
        body, name=name, grid=(L, N // ADA_TN),
        in_specs=[pl.BlockSpec((B, D), lambda l, j: (0, 0)), pl.BlockSpec((1, B, ADA_TN), lambda l, j: (l, 0, j)),
                  wspec, wspec, wspec],
        out_specs=[wspec] * 4, out_shape=[jax.ShapeDtypeStruct((L, D, N), F32)] * 4,
        compiler_params=_params(("parallel", "parallel")),
    )(c_all, dmod_sh, w, m, v)


def _adamw(name, w, g, m, v):
    R, C = w.shape
    tm = R
    for cand in (256, 128, 64, 32, 16, 8):
        if R % cand == 0 and cand * C * 4 <= (1 << 20):
            tm = cand
            break

    def body(w_ref, g_ref, m_ref, v_ref, d_ref, mo_ref, vo_ref):
        d_ref[...], mo_ref[...], vo_ref[...] = _adamw_val(w_ref[...], g_ref[...], m_ref[...], v_ref[...])

    return _rowcall(name, body, R, tm, [(w, ROW), (g, ROW), (m, ROW), (v, ROW)], [((R, C), F32, ROW)] * 3)


def _row_tile(rows, width, itemsize=4, limit=1 << 20):
    for cand in (512, 256, 128, 64, 32, 16):
        if rows % cand == 0 and cand * width * itemsize <= limit:
            return cand
    raise ValueError((rows, width))


def _cast_into_full(name, a, layer, q, kind, after):
    L, r, c = a.shape
    tm = _row_tile(r, c)
    if kind == "col":
        full, o_spec = (r, N_CHIPS * c), pl.BlockSpec((tm, c), lambda i, q_ref: (i, q_ref[0]))
    else:
        full, o_spec = (N_CHIPS * r, c), pl.BlockSpec((tm, c), lambda i, q_ref: (q_ref[0] * (r // tm) + i, 0))

    def body(q_ref, a_ref, after_ref, o_ref):
        o_ref[...] = a_ref[0].astype(BF16)

    return pl.pallas_call(
        body, name=name,
        grid_spec=pltpu.PrefetchScalarGridSpec(
            num_scalar_prefetch=1, grid=(r // tm,),
            in_specs=[pl.BlockSpec((1, tm, c), lambda i, q_ref: (layer, i, 0)), ANY], out_specs=o_spec),
        out_shape=jax.ShapeDtypeStruct(full, BF16), compiler_params=_params(("parallel",)),
    )(q.reshape(1).astype(jnp.int32), a, after)


def _sum4(name, g, rcv, q, kind, n):
    r, c = rcv.shape[1:]
    tm = _row_tile(r, c)
    if kind == "col":
        g_spec = pl.BlockSpec((tm, n), lambda i, q_ref: (i, q_ref[0]))
    else:
        g_spec = pl.BlockSpec((tm, c), lambda i, q_ref: (q_ref[0] * (n // tm) + i, 0))

    def body(q_ref, g_ref, r_ref, o_ref):
        acc = g_ref[...].astype(F32)
        for j in range(3):
            acc = acc + r_ref[j].astype(F32)
        o_ref[...] = acc

    return pl.pallas_call(
        body, name=name,
        grid_spec=pltpu.PrefetchScalarGridSpec(
            num_scalar_prefetch=1, grid=(r // tm,),
            in_specs=[g_spec, pl.BlockSpec((3, tm, c), lambda i, q_ref: (0, i, 0))],
            out_specs=pl.BlockSpec((tm, c), lambda i, q_ref: (i, 0))),
        out_shape=jax.ShapeDtypeStruct((r, c), F32), compiler_params=_params(("parallel",)),
    )(q.reshape(1).astype(jnp.int32), g, rcv)


def _adamw_sum(name, w, m, v, layer, mine, theirs, prev):
    L, r, c = w.shape
    tm = _row_tile(r, c, limit=3 << 19)
    lay = pl.BlockSpec((1, tm, c), lambda i: (layer, i, 0))
    flat = pl.BlockSpec((tm, c), lambda i: (i, 0))
    n_prev = 0 if prev is None else 4

    def body(w_ref, m_ref, v_ref, a_ref, b_ref, *rest):
        g_ref, d_ref, mo_ref, vo_ref = rest[n_prev:]
        g = a_ref[...] + b_ref[...]
        g_ref[0] = g
        d_ref[0], mo_ref[0], vo_ref[0] = _adamw_val(w_ref[0], g, m_ref[0], v_ref[0])

    return pl.pallas_call(
        body, name=name, grid=(r // tm,),
        in_specs=[lay, lay, lay, flat, flat] + [ANY] * n_prev, out_specs=[lay] * 4,
        out_shape=[jax.ShapeDtypeStruct((L, r, c), F32)] * 4,
        input_output_aliases={5 + k: k for k in range(n_prev)},
        compiler_params=_params(("parallel",)),
    )(w, m, v, mine, theirs, *(prev or ()))


def _sum8(name, gathered, own=None):
    R, N = gathered.shape
    P = R // 8

    def body(g_ref, *rest):
        o_ref = rest[-1]
        me = 4 * lax.axis_index("x") + 2 * lax.axis_index("y") + lax.axis_index("c")
        acc = None
        for d in range(8):
            blk = g_ref[d * P:(d + 1) * P, :]
            if own is not None:
                blk = jnp.where(me == d, rest[0][...], blk)
            acc = blk if d == 0 else acc + blk
        o_ref[...] = acc

    return pl.pallas_call(body, name=name, out_shape=jax.ShapeDtypeStruct((P, N), F32),
                          compiler_params=pltpu.CompilerParams(vmem_limit_bytes=VMEM_LIMIT),
                          )(gathered, *(() if own is None else (own,)))


ANY = pl.BlockSpec(memory_space=pl.ANY)


def _mesh_pos():
    x, y, c = lax.axis_index("x"), lax.axis_index("y"), lax.axis_index("c")
    other_chips = [(1 - x, y), (x, 1 - y), (1 - x, 1 - y)]
    return x, y, c, other_chips


def _allgather8(name, blk):
    m_per, n = blk.shape

    def body(x_ref, out_ref, send_sems, recv_sems, local_sem):
        x, y, c, chips = _mesh_pos()
        me, sibling = (x, y, c), (x, y, 1 - c)

        def rows(px, py, pc):
            return out_ref.at[pl.ds((4 * px + 2 * py + pc) * m_per, m_per), :]

        def copy(k, block, to, src=None):
            return pltpu.make_async_remote_copy(
                src_ref=rows(*block) if src is None else src, dst_ref=rows(*block),
                send_sem=send_sems.at[k], recv_sem=recv_sems.at[k], device_id=to, device_id_type=MESH)

        mine = pltpu.make_async_copy(x_ref, rows(*me), local_sem)
        mine.start()
        first = [copy(0, me, sibling, src=x_ref)]
        first += [copy(1 + j, me, (*chip, c), src=x_ref) for j, chip in enumerate(chips)]
        for cp in first:
            cp.start()
        passed = [copy(4 + j, (*chip, c), sibling) for j, chip in enumerate(chips)]
        for j, chip in enumerate(chips):
            copy(1 + j, (*chip, c), me).wait_recv()
            passed[j].start()
        copy(0, sibling, me).wait_recv()
        for j, chip in enumerate(chips):
            copy(4 + j, (*chip, 1 - c), me).wait_recv()
        for cp in first + passed:
            cp.wait_send()
        mine.wait()

    return pl.pallas_call(
        body, name=name, out_shape=jax.ShapeDtypeStruct((8 * m_per, n), blk.dtype),
        in_specs=[pl.BlockSpec(memory_space=pltpu.VMEM)], out_specs=pl.BlockSpec(memory_space=pltpu.VMEM),
        scratch_shapes=[pltpu.SemaphoreType.DMA((7,)), pltpu.SemaphoreType.DMA((7,)), pltpu.SemaphoreType.DMA],
        compiler_params=pltpu.CompilerParams(vmem_limit_bytes=VMEM_LIMIT),
    )(blk)


BIG = dict(w_in=("col", "ab_w_in", 0), w_out=("row", "ab_w_out", 0), up0=("col", "ffn_up_w", 0),
           dn0=("row", "ffn_down_w", 0), pw1=("col", "conv_pw1_w", 0), pw2=("row", "conv_pw2_w", 0),
           up1=("col", "ffn_up_w", 1), dn1=("row", "ffn_down_w", 1))
N_CHIPS = 4
HBM = pl.BlockSpec(memory_space=pltpu.HBM)
SEM = pl.BlockSpec(memory_space=pltpu.SEMAPHORE)
EFFECT = pltpu.SideEffectType.DATAFLOW_SIDE_EFFECTING


def _region(kind, ref, q, n):
    if kind == "col":
        return ref.at[:, pl.ds(q * n, n)]
    return ref.at[pl.ds(q * n, n), :]


def _gather_plan(kind, n):
    def remote(src, land, pos):
        x, y, c, chips = pos
        mine = _region(kind, land, 2 * x + y, n)
        return [(mine, mine, (*chip, c)) for chip in chips]

    return ("gather", kind, n), remote


def _scatter_plan(kind, n):
    def remote(src, land, pos):
        _, _, c, chips = pos
        return [(_region(kind, src, 2 * chip[0] + chip[1], n), land.at[j], (*chip, c)) for j, chip in enumerate(chips)]

    return ("scatter", kind, n), remote


def _everyone_plan(rows):
    def remote(src, land, pos):
        x, y, c, _ = pos
        mine = land.at[pl.ds((4 * x + 2 * y + c) * rows, rows), :]
        flip = lambda v, bit: 1 - v if bit else v
        return [(src, mine, (flip(x, k & 4), flip(y, k & 2), flip(c, k & 1))) for k in range(1, 8)]

    return ("everyone", rows), remote


def _sibling_plan():
    def remote(src, land, pos):
        x, y, c, _ = pos
        return [(src, land, (x, y, 1 - c))]

    return ("sibling",), remote


def _split_start(name, items, after=None):
    n = len(items)
    plans = [it[2] for it in items]
    n_in = 2 * n + (after is not None)

    def body(*refs):
        srcs, lands = refs[:n], refs[n:2 * n]
        sends, recvs = refs[n_in:n_in + n], refs[n_in + n:n_in + 2 * n]
        token = refs[n_in + 4 * n]
        pos = _mesh_pos()
        for a, (_, remote) in enumerate(plans):
            for k, (s, d, dev) in enumerate(remote(srcs[a], lands[a], pos)):
                pltpu.make_async_remote_copy(src_ref=s, dst_ref=d, send_sem=sends[a].at[k], recv_sem=recvs[a].at[k],
                                             device_id=dev, device_id_type=MESH).start()
        token[...] = jnp.zeros_like(token)

    sems = [pltpu.SemaphoreType.DMA((it[3],)) for it in items]
    bufs = [pltpu.HBM(it[k].shape, it[k].dtype) for k in (0, 1) for it in items]
    outs = pl.pallas_call(
        body, name=name, out_shape=[*sems, *sems, *bufs, jax.ShapeDtypeStruct((8, LANES), F32)],
        in_specs=[HBM] * (2 * n) + [ANY] * (n_in - 2 * n),
        out_specs=[SEM] * (2 * n) + [HBM] * (2 * n) + [pl.BlockSpec(memory_space=pltpu.VMEM)],
        input_output_aliases={i: 2 * n + i for i in range(2 * n)},
        compiler_params=pltpu.CompilerParams(has_side_effects=EFFECT),
    )(*[pltpu.with_memory_space_constraint(it[k], pltpu.HBM) for k in (0, 1) for it in items],
      *(() if after is None else (after,)))
    state = [(items[a][2], items[a][3], outs[2 * n + a], outs[3 * n + a], outs[a], outs[n + a]) for a in range(n)]
    return state, outs[4 * n]


def _split_wait(name, state, after):
    n = len(state)

    def body(*refs):
        srcs, lands = refs[:n], refs[n:2 * n]
        sends, recvs = refs[2 * n:3 * n], refs[3 * n:4 * n]
        pos = _mesh_pos()
        for a, ((_, remote), *_) in enumerate(state):
            for k, (s, d, dev) in enumerate(remote(srcs[a], lands[a], pos)):
                cp = pltpu.make_async_remote_copy(src_ref=s, dst_ref=d, send_sem=sends[a].at[k], recv_sem=recvs[a].at[k],
                                                  device_id=dev, device_id_type=MESH)
                cp.wait_send()
                cp.wait_recv()

    bufs = [st[k] for k in (2, 3) for st in state]
    outs = pl.pallas_call(
        body, name=name, out_shape=[pltpu.HBM(b.shape, b.dtype) for b in bufs],
        in_specs=[HBM] * (2 * n) + [SEM] * (2 * n) + [ANY], out_specs=[HBM] * (2 * n),
        input_output_aliases={i: i for i in range(2 * n)},
        compiler_params=pltpu.CompilerParams(has_side_effects=EFFECT),
    )(*bufs, *[st[k] for k in (4, 5) for st in state], after)
    return outs[:n], outs[n:]


class _Weights:
    def __init__(self, w, q, after):
        unused = jnp.zeros((16, LANES), BF16)

        def item(name, after):
            kind, pname, layer = BIG[name]
            _, r, c = w[pname].shape
            land = _cast_into_full(f"cast_{name}", w[pname], layer, q, kind, after)
            return unused, land, _gather_plan(kind, c if kind == "col" else r), N_CHIPS - 1

        first, *rest = BIG
        state1, token1 = _split_start("gw_start_first", [item(first, after)], after)
        state2, self.token = _split_start("gw_start_rest", [item(name, token1) for name in rest], token1)
        self.pending = dict(zip(BIG, state1 + state2))
        self.ready = {}

    def get(self, name, after=None):
        if name not in self.ready:
            self.ready[name] = _split_wait(f"gw_wait_{name}", [self.pending.pop(name)], after)[1][0]
        return self.ready[name]


class _GradPipe:
    def __init__(self, q, w, m, v):
        self.q, self.w, self.m, self.v = q, w, m, v
        self.stage, self.results = {}, {}

    def start_small(self, dmod, lossv, small, after):
        self.small_names = [n for n in REPLICATED if n != "ada_b"] + list(SMALL_SHARDED)
        payload = [dmod.reshape(2, -1), lossv] + [small[n] for n in self.small_names]
        self.small_shapes = [p.shape for p in payload]
        packed = _pack(payload)
        land = jnp.zeros((8 * PACK_ROWS, packed.shape[1]), F32)
        self.small_state, token = _split_start("ag_grads_start", [(packed, land, _everyone_plan(PACK_ROWS), 7)], after)
        return token

    def wait_small(self, after):
        srcs, lands = _split_wait("ag_grads_wait", self.small_state, after)
        return srcs[0], lands[0]

    def scatter(self, group, grads, after=None):
        items = []
        for name, g in grads.items():
            kind = BIG[name][0]
            rows, cols = g.shape
            n = (cols if kind == "col" else rows) // N_CHIPS
            reg = (rows, n) if kind == "col" else (n, cols)
            items.append((g, lax.empty((N_CHIPS - 1, *reg), BF16), _scatter_plan(kind, n), N_CHIPS - 1))
        state, token = _split_start(f"gs_start_{group}", items, after)
        self.stage[group] = (list(grads), state)
        return token[0, 0]

    def collect(self, group, after):
        names, state = self.stage[group]
        srcs, lands = _split_wait(f"gs_wait_{group}", state, after)
        items = []
        for name, st, g, land in zip(names, state, srcs, lands):
            _, kind, n = st[0][0]
            part = _sum4(f"sum_{name}", g, land, self.q, kind, n)
            items.append((part, lax.empty(part.shape, F32), _sibling_plan(), 1))
        state, token = _split_start(f"sw_start_{group}", items)
        self.stage[group] = (names, state)
        return token[0, 0]

    def finish(self, group, after):
        names, state = self.stage.pop(group)
        srcs, lands = _split_wait(f"sw_wait_{group}", state, after)
        for name, mine, theirs in zip(names, srcs, lands):
            _, pname, layer = BIG[name]
            self.results[pname] = _adamw_sum(f"adamw_{name}", self.w[pname], self.m[pname], self.v[pname], layer,
                                             mine, theirs, self.results.get(pname))


PACK_ROWS = 8


def _pack(arrays):
    flat = jnp.concatenate([a.reshape(-1) for a in arrays])
    n = flat.shape[0]
    padded = -(-n // (PACK_ROWS * LANES)) * (PACK_ROWS * LANES)
    return jnp.pad(flat, (0, padded - n)).reshape(PACK_ROWS, padded // PACK_ROWS)


def _unpack(packed, shapes):
    flat = packed.reshape(-1)
    out, off = [], 0
    for s in shapes:
        n = 1
        for d in s:
            n *= d
        out.append(flat[off:off + n].reshape(s))
        off += n
    return out


REPLICATED = ("ada_b", "norm_mix_g", "norm_ffn_g", "a_vnorm_g", "a_spatial_w", "a_spatial_b", "b_q_norm_g",
              "b_k_norm_g", "ffn_dw_b")
SMALL_SHARDED = ("conv_pw1_b", "conv_dw_w", "conv_dw_b", "conv_ln_g", "conv_ln_b", "conv_pw2_b", "ffn_dw_w")
WEIGHTS = ("ada_w", "ada_b", "norm_mix_g", "norm_ffn_g", "ab_w_in", "a_vnorm_g", "a_spatial_w", "a_spatial_b",
           "b_q_norm_g", "b_k_norm_g", "ab_w_out", "conv_pw1_w", "conv_pw1_b", "conv_dw_w", "conv_dw_b", "conv_ln_g",
           "conv_ln_b", "conv_pw2_w", "conv_pw2_b", "ffn_up_w", "ffn_dw_w", "ffn_dw_b", "ffn_down_w")

def kernel(x, c, positions, ada_w, ada_b, norm_mix_g, norm_ffn_g, ab_w_in, a_vnorm_g, a_spatial_w, a_spatial_b, b_q_norm_g, b_k_norm_g, ab_w_out, conv_pw1_w, conv_pw1_b, conv_dw_w, conv_dw_b, conv_ln_g, conv_ln_b, conv_pw2_w, conv_pw2_b, ffn_up_w, ffn_dw_w, ffn_dw_b, ffn_down_w, loss_target, m_ada_w, m_ada_b, m_norm_mix_g, m_norm_ffn_g, m_ab_w_in, m_a_vnorm_g, m_a_spatial_w, m_a_spatial_b, m_b_q_norm_g, m_b_k_norm_g, m_ab_w_out, m_conv_pw1_w, m_conv_pw1_b, m_conv_dw_w, m_conv_dw_b, m_conv_ln_g, m_conv_ln_b, m_conv_pw2_w, m_conv_pw2_b, m_ffn_up_w, m_ffn_dw_w, m_ffn_dw_b, m_ffn_down_w, v_ada_w, v_ada_b, v_norm_mix_g, v_norm_ffn_g, v_ab_w_in, v_a_vnorm_g, v_a_spatial_w, v_a_spatial_b, v_b_q_norm_g, v_b_k_norm_g, v_ab_w_out, v_conv_pw1_w, v_conv_pw1_b, v_conv_dw_w, v_conv_dw_b, v_conv_ln_g, v_conv_ln_b, v_conv_pw2_w, v_conv_pw2_b, v_ffn_up_w, v_ffn_dw_w, v_ffn_dw_b, v_ffn_down_w):
    w = dict(ada_w=ada_w, ada_b=ada_b, norm_mix_g=norm_mix_g, norm_ffn_g=norm_ffn_g, ab_w_in=ab_w_in, a_vnorm_g=a_vnorm_g, a_spatial_w=a_spatial_w, a_spatial_b=a_spatial_b, b_q_norm_g=b_q_norm_g, b_k_norm_g=b_k_norm_g, ab_w_out=ab_w_out, conv_pw1_w=conv_pw1_w, conv_pw1_b=conv_pw1_b, conv_dw_w=conv_dw_w, conv_dw_b=conv_dw_b, conv_ln_g=conv_ln_g, conv_ln_b=conv_ln_b, conv_pw2_w=conv_pw2_w, conv_pw2_b=conv_pw2_b, ffn_up_w=ffn_up_w, ffn_dw_w=ffn_dw_w, ffn_dw_b=ffn_dw_b, ffn_down_w=ffn_down_w)
    m = dict(ada_w=m_ada_w, ada_b=m_ada_b, norm_mix_g=m_norm_mix_g, norm_ffn_g=m_norm_ffn_g, ab_w_in=m_ab_w_in, a_vnorm_g=m_a_vnorm_g, a_spatial_w=m_a_spatial_w, a_spatial_b=m_a_spatial_b, b_q_norm_g=m_b_q_norm_g, b_k_norm_g=m_b_k_norm_g, ab_w_out=m_ab_w_out, conv_pw1_w=m_conv_pw1_w, conv_pw1_b=m_conv_pw1_b, conv_dw_w=m_conv_dw_w, conv_dw_b=m_conv_dw_b, conv_ln_g=m_conv_ln_g, conv_ln_b=m_conv_ln_b, conv_pw2_w=m_conv_pw2_w, conv_pw2_b=m_conv_pw2_b, ffn_up_w=m_ffn_up_w, ffn_dw_w=m_ffn_dw_w, ffn_dw_b=m_ffn_dw_b, ffn_down_w=m_ffn_down_w)
    v = dict(ada_w=v_ada_w, ada_b=v_ada_b, norm_mix_g=v_norm_mix_g, norm_ffn_g=v_norm_ffn_g, ab_w_in=v_ab_w_in, a_vnorm_g=v_a_vnorm_g, a_spatial_w=v_a_spatial_w, a_spatial_b=v_a_spatial_b, b_q_norm_g=v_b_q_norm_g, b_k_norm_g=v_b_k_norm_g, ab_w_out=v_ab_w_out, conv_pw1_w=v_conv_pw1_w, conv_pw1_b=v_conv_pw1_b, conv_dw_w=v_conv_dw_w, conv_dw_b=v_conv_dw_b, conv_ln_g=v_conv_ln_g, conv_ln_b=v_conv_ln_b, conv_pw2_w=v_conv_pw2_w, conv_pw2_b=v_conv_pw2_b, ffn_up_w=v_ffn_up_w, ffn_dw_w=v_ffn_dw_w, ffn_dw_b=v_ffn_dw_b, ffn_down_w=v_ffn_down_w)
    S, D = x.shape[1], x.shape[2]
    xi, yi, ci = lax.axis_index("x"), lax.axis_index("y"), lax.axis_index("c")
    q = 2 * xi + yi
    b = 2 * q + ci
    take_dev = lambda g: g.reshape(8, PACK_ROWS, -1)

    c_all = _allgather8("ag_c", c.reshape(PACK_ROWS, D // PACK_ROWS)).reshape(8, D)
    n_ada = ada_w.shape[2]
    mod_sh = _ada_fwd("ada_fwd", c_all, ada_w, lax.dynamic_slice_in_dim(ada_b, q * n_ada, n_ada, axis=1))
    sh_shapes = [mod_sh.shape] + [w[n].shape for n in SMALL_SHARDED]
    gathered_mod = _allgather8("ag_mod", _pack([mod_sh] + [w[n] for n in SMALL_SHARDED]))
    per_chip = [_unpack(blk, sh_shapes) for blk in take_dev(gathered_mod)[0::2]]
    mod_g = jnp.stack([pc[0] for pc in per_chip])
    mod_mine = lax.dynamic_index_in_dim(mod_g, b, axis=2, keepdims=False)
    mod = mod_mine.transpose(1, 0, 2).reshape(2, 6, D)
    sp = {n: jnp.concatenate([pc[1 + i] for pc in per_chip], axis=-1) for i, n in enumerate(SMALL_SHARDED)}
    sp.update({n: w[n] for n in REPLICATED if n != "ada_b"})

    wb = _Weights(w, q, gathered_mod)
    mod = mod + wb.token[0, 0]

    pipe = _GradPipe(q, w, m, v)
    lossv, grad_x, late = _local_step(x[0], loss_target[0], positions[0], mod, wb, sp, pipe)

    own, gathered = pipe.wait_small(grad_x)
    totals = _unpack(_sum8("sum_grads", gathered, own), pipe.small_shapes)
    grads = dict(zip(["ada_b", "loss_columns"] + pipe.small_names, totals))
    loss = 0.5 * jnp.sum(grads.pop("loss_columns")) / D
    late_g = _allgather8("ag_late", _pack(list(late)))
    late_tot = _unpack(_sum8("sum_late", late_g), [(3, D)])[0]
    grads["ada_b"] = grads["ada_b"].at[0, :2 * D].add(late_tot[:2].reshape(-1))
    grads["norm_mix_g"] = grads["norm_mix_g"].at[0].add(late_tot[2])
    for n in SMALL_SHARDED:
        n_sh = w[n].shape[-1]
        grads[n] = lax.dynamic_slice_in_dim(grads[n], q * n_sh, n_sh, axis=grads[n].ndim - 1)
    dmod_of = lambda packed: packed.reshape(packed.shape[0] // PACK_ROWS, -1)[:, :2 * 6 * D].reshape(-1, 2, 6 * D)
    dmod_all = jnp.where((jnp.arange(8) == b)[:, None, None], dmod_of(own), dmod_of(gathered))
    late_all = take_dev(late_g).reshape(8, -1)[:, :3 * D].reshape(8, 3, D)
    dmod_all = dmod_all.at[:, 0, :2 * D].add(late_all[:, :2].reshape(8, 2 * D))
    dmod_sh = lax.dynamic_slice_in_dim(dmod_all, q * n_ada, n_ada, axis=2).transpose(1, 0, 2)

    pipe.finish("g3", dmod_sh)
    pipe.collect("g4", dmod_sh)
    grads["ada_w"], delta_ada, m_ada, v_ada = _ada_update("ada_update", c_all, dmod_sh, ada_w, m_ada_w, v_ada_w)
    delta, new_m, new_v = dict(ada_w=delta_ada), dict(ada_w=m_ada), dict(ada_w=v_ada)
    rest = list(REPLICATED) + list(SMALL_SHARDED)
    rest_shapes = [w[n].shape for n in rest]
    outs = _adamw("adamw_small", *[_pack([src[n].reshape(w[n].shape) for n in rest]) for src in (w, grads, m, v)])
    for tgt, packed in zip((delta, new_m, new_v), outs):
        tgt.update(dict(zip(rest, _unpack(packed, rest_shapes))))
    for n in rest:
        grads[n] = grads[n].reshape(w[n].shape)
    pipe.finish("g4", outs[0])
    for n, res in pipe.results.items():
        grads[n], delta[n], new_m[n], new_v[n] = res

    return (loss, grad_x[None], *[grads[n] for n in WEIGHTS], *[delta[n] for n in WEIGHTS],
            *[new_m[n] for n in WEIGHTS], *[new_v[n] for n in WEIGHTS])
```

```python
import functools

import jax
import jax.numpy as jnp
from jax import lax
from jax.experimental import pallas as pl
from jax.experimental.pallas import tpu as pltpu

F32, BF16 = jnp.float32, jnp.bfloat16
EPS = 1e-6
NEG = -1e30
ROPE_THETA = 10000.0
LANES = 128
VMEM_LIMIT = 56 * 1024 * 1024
ADAM_LR, ADAM_B1, ADAM_B2, ADAM_EPS, ADAM_WD, ADAM_STEP = 0.001, 0.9, 0.999, 1e-08, 0.01, 10
MESH = pl.DeviceIdType.MESH


def _params(sem):
    return pltpu.CompilerParams(dimension_semantics=sem, vmem_limit_bytes=VMEM_LIMIT)


_DN = {"nn": (((1,), (0,)), ((), ())), "nt": (((1,), (1,)), ((), ())), "tn": (((0,), (0,)), ((), ()))}


def _matmul(name, a, b, mode, out_dtype, tm=512, tn=512, tk=1024, bias=None, after=None):
    if mode == "nn":
        (M, K), N = a.shape, b.shape[1]
    elif mode == "nt":
        (M, K), N = a.shape, b.shape[0]
    else:
        (K, M), N = a.shape, b.shape[1]
    tm, tn, tk = min(tm, M), min(tn, N), min(tk, K)
    assert M % tm == 0 and N % tn == 0 and K % tk == 0, (name, M, N, K, tm, tn, tk)
    nk = K // tk
    if mode == "tn":
        a_spec = pl.BlockSpec((tk, tm), lambda i, j, k: (k, i))
    else:
        a_spec = pl.BlockSpec((tm, tk), lambda i, j, k: (i, k))
    if mode == "nt":
        b_spec = pl.BlockSpec((tn, tk), lambda i, j, k: (j, k))
    else:
        b_spec = pl.BlockSpec((tk, tn), lambda i, j, k: (k, j))
    in_specs, args = [a_spec, b_spec], [a, b]
    if bias is not None:
        in_specs.append(pl.BlockSpec((1, tn), lambda i, j, k: (0, j)))
        args.append(bias)
    if after is not None:
        in_specs.append(pl.BlockSpec(memory_space=pl.ANY))
        args.append(after)
    n_in = len(args)

    def body(*refs):
        a_ref, b_ref, o_ref = refs[0], refs[1], refs[n_in]
        p = lax.dot_general(a_ref[...], b_ref[...], _DN[mode], preferred_element_type=F32)

        def finish(acc):
            if bias is not None:
                acc = acc + refs[2][...]
            o_ref[...] = acc.astype(o_ref.dtype)

        if nk == 1:
            finish(p)
        else:
            acc_ref = refs[n_in + 1]
            k = pl.program_id(2)

            @pl.when(k == 0)
            def _():
                acc_ref[...] = p

            @pl.when(k > 0)
            def _():
                acc_ref[...] += p

            @pl.when(k == nk - 1)
            def _():
                finish(acc_ref[...])

    return pl.pallas_call(
        body, name=name, grid=(M // tm, N // tn, nk), in_specs=in_specs,
        out_specs=pl.BlockSpec((tm, tn), lambda i, j, k: (i, j)),
        out_shape=jax.ShapeDtypeStruct((M, N), out_dtype),
        scratch_shapes=[pltpu.VMEM((tm, tn), F32)] if nk > 1 else [],
        compiler_params=_params(("parallel", "parallel", "arbitrary")),
    )(*args)


def _rowcall(name, body, nrows, tm, ins, outs, scratch=()):
    nblk = nrows // tm
    assert nrows % tm == 0

    def spec(kind, shape):
        k = kind[0]
        if k == "row":
            cw, cb = kind[1] or shape[-1], kind[2]
            return pl.BlockSpec((tm, cw), lambda i: (i, cb))
        if k == "prev":
            hb, cw, cb = kind[1], kind[2] or shape[-1], kind[3]
            r = tm // hb
            return pl.BlockSpec((hb, cw), lambda i: (jnp.maximum(i * r - 1, 0), cb))
        if k == "next":
            hb, cw, cb = kind[1], kind[2] or shape[-1], kind[3]
            r, last = tm // hb, nrows // hb - 1
            return pl.BlockSpec((hb, cw), lambda i: (jnp.minimum((i + 1) * r, last), cb))
        if k == "off":
            off, cw, cb = kind[1], kind[2] or shape[-1], kind[3]
            return pl.BlockSpec((tm, cw), lambda i: (jnp.clip(i + off, 0, nblk - 1), cb))
        if k == "cls":
            dil, off = kind[1], kind[2]
            return pl.BlockSpec((dil, tm // dil, shape[-1]), lambda i: (0, jnp.clip(i + off, 0, nblk - 1), 0))
        nd = len(shape)
        return pl.BlockSpec(tuple(shape), lambda i: (0,) * nd)

    has_acc = any(o[2][0] == "acc" for o in outs)
    return pl.pallas_call(
        body, name=name, grid=(nblk,),
        in_specs=[spec(kind, a.shape) for a, kind in ins],
        out_specs=[spec(kind, shape) for shape, _, kind in outs],
        out_shape=[jax.ShapeDtypeStruct(tuple(shape), dt) for shape, dt, _ in outs],
        scratch_shapes=list(scratch),
        compiler_params=_params(("arbitrary",) if has_acc else ("parallel",)),
    )(*[a for a, _ in ins])


ROW = ("row", None, 0)
FULL = ("full",)
ACC = ("acc",)


def _colsum(x):
    return jnp.sum(x, axis=0, keepdims=True)


def _acc_add(i, ref, val, rows=None):
    idx = (slice(None),) * len(ref.shape) if rows is None else rows

    @pl.when(i == 0)
    def _():
        ref[idx] = val

    @pl.when(i > 0)
    def _():
        ref[idx] = ref[idx] + val


def _sigmoid(x):
    return 1.0 / (1.0 + jnp.exp(-x))


def _gelu(x):
    return 0.5 * x * (1.0 + lax.erf(x * (2.0 ** -0.5)))


def _gelu_grad(x):
    return 0.5 * (1.0 + lax.erf(x * (2.0 ** -0.5))) + x * jnp.exp(-0.5 * x * x) * ((2.0 * jnp.pi) ** -0.5)


SUBLANES = 8


def _phases(ext, sign):
    n = ext.shape[0]
    return [ext if b == 0 else pltpu.roll(ext, b if sign > 0 else n - b, axis=0) for b in range(SUBLANES)]


def _shift_prev(phases, s, hb):
    a, b = divmod(s, SUBLANES)
    return phases[b][hb - SUBLANES * a:phases[b].shape[0] - SUBLANES * a]


def _shift_next(phases, s, tm):
    a, b = divmod(s, SUBLANES)
    return phases[b][SUBLANES * a:SUBLANES * a + tm]


def _rms_mod_val(x, g, sc, sh):
    r = lax.rsqrt(jnp.mean(x * x, axis=-1, keepdims=True) + EPS)
    return x * r * g * (1.0 + sc) + sh


def _mod_first(name, x, g, sc, sh, tm=256):
    S, D = x.shape

    def body(x_ref, g_ref, sc_ref, sh_ref, h_ref):
        h_ref[...] = _rms_mod_val(x_ref[...], g_ref[...], sc_ref[...], sh_ref[...]).astype(BF16)

    return _rowcall(name, body, S, tm, [(x, ROW), (g, FULL), (sc, FULL), (sh, FULL)], [((S, D), BF16, ROW)])[0]


def _resid_mod(name, x, y, gate, g, sc, sh, tm=256):
    S, D = x.shape

    def body(x_ref, y_ref, gate_ref, g_ref, sc_ref, sh_ref, xo_ref, h_ref):
        xn = x_ref[...] + gate_ref[...] * y_ref[...]
        xo_ref[...] = xn
        h_ref[...] = _rms_mod_val(xn, g_ref[...], sc_ref[...], sh_ref[...]).astype(BF16)

    return _rowcall(name, body, S, tm,
                    [(x, ROW), (y, ROW), (gate, FULL), (g, FULL), (sc, FULL), (sh, FULL)],
                    [((S, D), F32, ROW), ((S, D), BF16, ROW)])


def _gate_bwd_val(i, d, y_ref, gate_ref, dy_ref, dg_ref, db_ref):
    dy = d * gate_ref[...]
    dy_ref[...] = dy.astype(BF16)
    _acc_add(i, dg_ref, _colsum(d * y_ref[...]))
    _acc_add(i, db_ref, _colsum(dy))


GATE_OUTS = lambda S, D: [((S, D), BF16, ROW), ((1, D), F32, ACC), ((1, D), F32, ACC)]


def _loss_head(name, x, y, gate, tgt, tm=256):
    S, D = x.shape

    def body(x_ref, y_ref, gate_ref, t_ref, dx_ref, l_ref, dy_ref, dg_ref, db_ref):
        i = pl.program_id(0)
        err = x_ref[...] + gate_ref[...] * y_ref[...] - t_ref[...]
        d = err * (1.0 / D)
        dx_ref[...] = d
        _acc_add(i, l_ref, _colsum(err * err))
        _gate_bwd_val(i, d, y_ref, gate_ref, dy_ref, dg_ref, db_ref)

    return _rowcall(name, body, S, tm, [(x, ROW), (y, ROW), (gate, FULL), (tgt, ROW)],
                    [((S, D), F32, ROW), ((1, D), F32, ACC)] + GATE_OUTS(S, D))


def _mod_bwd(name, dxo, dh, x, g, sc, y=None, gate=None, tm=256):
    S, D = x.shape
    gated = y is not None

    def body(d_ref, dh_ref, x_ref, g_ref, sc_ref, *rest):
        dx_ref, dsh_ref, dsc_ref, dg_ref = rest[2 * gated:2 * gated + 4]
        i = pl.program_id(0)
        xv, dh_v, gv = x_ref[...], dh_ref[...], g_ref[...]
        r = lax.rsqrt(jnp.mean(xv * xv, axis=-1, keepdims=True) + EPS)
        n = xv * r
        _acc_add(i, dsh_ref, _colsum(dh_v))
        _acc_add(i, dsc_ref, _colsum(dh_v * (n * gv)))
        dy = dh_v * (1.0 + sc_ref[...])
        _acc_add(i, dg_ref, _colsum(dy * n))
        dn = dy * gv
        dx = d_ref[...] + r * (dn - n * jnp.mean(dn * n, axis=-1, keepdims=True))
        dx_ref[...] = dx
        if gated:
            _gate_bwd_val(i, dx, rest[0], rest[1], *rest[6:9])

    ins = [(dxo, ROW), (dh, ROW), (x, ROW), (g, FULL), (sc, FULL)] + ([(y, ROW), (gate, FULL)] if gated else [])
    outs = [((S, D), F32, ROW), ((1, D), F32, ACC), ((1, D), F32, ACC), ((1, D), F32, ACC)]
    return _rowcall(name, body, S, tm, ins, outs + (GATE_OUTS(S, D) if gated else []))


HB16 = 16


def _conv3_val(ph, w, b, hb):
    return w[2:3] * _shift_prev(ph, 0, hb) + w[1:2] * _shift_prev(ph, 1, hb) + w[0:1] * _shift_prev(ph, 2, hb) + b


def _halo_first(halo_ref, tile_ref, live):
    return _phases(jnp.concatenate([halo_ref[...].astype(F32) * live, tile_ref[...].astype(F32)], axis=0), 1)


def _glu3_fwd(name, u, w, b, tm=128):
    S, F2 = u.shape
    Fh = F2 // 2

    def body(ua_ref, ub_ref, ha_ref, hb_ref, w_ref, b_ref, o_ref, z_ref):
        live = (pl.program_id(0) > 0).astype(F32)
        wv, bv = w_ref[...], b_ref[...]
        za = _conv3_val(_halo_first(ha_ref, ua_ref, live), wv[:, :Fh], bv[:, :Fh], HB16)
        zb = _conv3_val(_halo_first(hb_ref, ub_ref, live), wv[:, Fh:], bv[:, Fh:], HB16)
        o_ref[...] = (za * _sigmoid(za) * zb).astype(BF16)
        z_ref[:, :Fh] = za.astype(BF16)
        z_ref[:, Fh:] = zb.astype(BF16)

    return _rowcall(name, body, S, tm,
                    [(u, ("row", Fh, 0)), (u, ("row", Fh, 1)), (u, ("prev", HB16, Fh, 0)), (u, ("prev", HB16, Fh, 1)),
                     (w, FULL), (b, FULL)],
                    [((S, Fh), BF16, ROW), ((S, F2), BF16, ROW)])


def _glu3_bwd(name, z, dhm, tm=128):
    S, F2 = z.shape
    Fh = F2 // 2

    def body(za_ref, zb_ref, d_ref, dz_ref, db_ref):
        i = pl.program_id(0)
        za, zb, d = za_ref[...].astype(F32), zb_ref[...].astype(F32), d_ref[...]
        sg = _sigmoid(za)
        da = d * zb * (sg * (1.0 + za * (1.0 - sg)))
        db = d * (za * sg)
        dz_ref[:, :Fh] = da.astype(BF16)
        dz_ref[:, Fh:] = db.astype(BF16)
        _acc_add(i, db_ref, jnp.concatenate([_colsum(da), _colsum(db)], axis=1))

    return _rowcall(name, body, S, tm, [(z, ("row", Fh, 0)), (z, ("row", Fh, 1)), (dhm, ROW)],
                    [((S, F2), BF16, ROW), ((1, F2), F32, ACC)])


def _conv3_bwd(name, dz, u, w, tm=128):
    S, F2 = dz.shape
    nblk = S // tm
    K = w.shape[0]

    def body(d_ref, n_ref, u_ref, w_ref, o_ref, dw_ref):
        i = pl.program_id(0)
        live = (i < nblk - 1).astype(F32)
        ph = _phases(jnp.concatenate([d_ref[...].astype(F32), n_ref[...].astype(F32) * live], axis=0), -1)
        wv, uv = w_ref[...], u_ref[...].astype(F32)
        shifted = [_shift_next(ph, K - 1 - k, tm) for k in range(K)]
        o_ref[...] = functools.reduce(lambda a, t: a + t, [wv[k:k + 1] * shifted[k] for k in range(K)]).astype(BF16)
        for k in range(K):
            _acc_add(i, dw_ref, _colsum(uv * shifted[k]), rows=(slice(k, k + 1), slice(None)))

        @pl.when(i == 0)
        def _():
            dw_ref[K:, :] = jnp.zeros((dw_ref.shape[0] - K, F2), F32)

    return _rowcall(name, body, S, tm, [(dz, ROW), (dz, ("next", HB16, None, 0)), (u, ROW), (w, FULL)],
                    [((S, F2), BF16, ROW), ((SUBLANES, F2), F32, ACC)])


def _ffn_fwd(name, h, w_up, w_dn, dw_w, dw_b):
    u = _matmul(f"{name}_up", h, w_up, "nn", BF16, tm=2048)
    hm, z = _glu3_fwd(f"{name}_glu", u, dw_w, dw_b)
    f = _matmul(f"{name}_dn", hm, w_dn, "nn", F32, tm=1024, tn=1024, tk=w_dn.shape[0])
    return f, (u, hm, z)


def _ffn_bwd(name, dy, h, u, hm, z, w_up, w_dn, dw_w, dw_b):
    Fh = w_dn.shape[0]
    dhm = _matmul(f"{name}_ddn_x", dy, w_dn, "nt", F32, tm=1024, tn=Fh // 2)
    g_dn = _matmul(f"{name}_ddn_w", hm, dy, "tn", BF16, tm=Fh // 2, tn=1024, tk=2048)
    dz, g_dw_b = _glu3_bwd(f"{name}_dglu", z, dhm)
    du, taps = _conv3_bwd(f"{name}_dconv", dz, u, dw_w)
    g_up = _matmul(f"{name}_dup_w", h, du, "tn", BF16, tm=1024, tn=Fh // 2, tk=2048)
    dh = _matmul(f"{name}_dup_x", du, w_up, "nt", F32, tm=1024, tn=1024, tk=Fh)
    return dh, dict(up=g_up, dn=g_dn, dw_w=taps[0:dw_w.shape[0]], dw_b=g_dw_b)


HB32 = 32


def _glu31_fwd(name, p, w, b, tm=256):
    S, D2 = p.shape
    D = D2 // 2
    K = w.shape[0]

    def body(a_ref, g_ref, ha_ref, hg_ref, w_ref, b_ref, o_ref):
        live = (pl.program_id(0) > 0).astype(F32)
        y1 = a_ref[...] * _sigmoid(g_ref[...])
        ph = _phases(jnp.concatenate([ha_ref[...] * _sigmoid(hg_ref[...]) * live, y1], axis=0), 1)
        wv = w_ref[...]
        acc = b_ref[...] + wv[K - 1:K] * y1
        for k in range(K - 1):
            acc = acc + wv[k:k + 1] * _shift_prev(ph, K - 1 - k, HB32)
        o_ref[...] = acc

    return _rowcall(name, body, S, tm,
                    [(p, ("row", D, 0)), (p, ("row", D, 1)), (p, ("prev", HB32, D, 0)), (p, ("prev", HB32, D, 1)),
                     (w, FULL), (b, FULL)],
                    [((S, D), F32, ROW)])[0]


def _ln_silu_fwd(name, y2, g, b, tm=256):
    S, D = y2.shape

    def body(y_ref, g_ref, b_ref, o_ref):
        y = y_ref[...]
        mu = jnp.mean(y, axis=-1, keepdims=True)
        yc = y - mu
        rs = lax.rsqrt(jnp.mean(yc * yc, axis=-1, keepdims=True) + EPS)
        y3 = yc * rs * g_ref[...] + b_ref[...]
        o_ref[...] = (y3 * _sigmoid(y3)).astype(BF16)

    return _rowcall(name, body, S, tm, [(y2, ROW), (g, FULL), (b, FULL)], [((S, D), BF16, ROW)])[0]


def _ln_silu_bwd(name, y2, dy4, g, b, tm=256):
    S, D = y2.shape

    def body(y_ref, d_ref, g_ref, b_ref, o_ref, dg_ref, db_ref):
        i = pl.program_id(0)
        y, gv = y_ref[...], g_ref[...]
        mu = jnp.mean(y, axis=-1, keepdims=True)
        yc = y - mu
        rs = lax.rsqrt(jnp.mean(yc * yc, axis=-1, keepdims=True) + EPS)
        n = yc * rs
        y3 = n * gv + b_ref[...]
        sg = _sigmoid(y3)
        dy3 = d_ref[...] * (sg * (1.0 + y3 * (1.0 - sg)))
        _acc_add(i, db_ref, _colsum(dy3))
        _acc_add(i, dg_ref, _colsum(dy3 * n))
        dn = dy3 * gv
        o_ref[...] = rs * (dn - jnp.mean(dn, axis=-1, keepdims=True) - n * jnp.mean(dn * n, axis=-1, keepdims=True))

    return _rowcall(name, body, S, tm, [(y2, ROW), (dy4, ROW), (g, FULL), (b, FULL)],
                    [((S, D), F32, ROW), ((1, D), F32, ACC), ((1, D), F32, ACC)])


def _glu31_bwd(name, p, dy2, w, tm=256):
    S, D2 = p.shape
    D = D2 // 2
    K = w.shape[0]
    nblk = S // tm

    conv_rows, tap_rows, tap_group, tap_unroll = 16, SUBLANES, 4, 4

    def body(a_ref, g_ref, d_ref, dn_ref, w_ref, dp_ref, dw_ref, dcb_ref, dpb_ref, ph_d, y1_ref, dy1_ref, wb_ref):
        i = pl.program_id(0)
        live_next = (i < nblk - 1).astype(F32)
        a, sg, d = a_ref[...], _sigmoid(g_ref[...]), d_ref[...]
        y1_ref[...] = a * sg
        for b, ph in enumerate(_phases(jnp.concatenate([d, dn_ref[...] * live_next], axis=0), -1)):
            ph_d[b] = ph
        taps = [divmod(K - 1 - k, SUBLANES) for k in range(K)]

        @pl.when(i == 0)
        def _():
            for k in range(K):
                wb_ref[k] = jnp.broadcast_to(w_ref[k:k + 1, :], (SUBLANES, D))

        def conv_rows_at(rb, carry):
            r0 = pl.multiple_of(rb * conv_rows, conv_rows)
            accs = [jnp.zeros((SUBLANES, D), F32) for _ in range(conv_rows // SUBLANES)]
            for k, (rows8, phase) in enumerate(taps):
                wk = wb_ref[k]
                for u in range(len(accs)):
                    accs[u] = accs[u] + wk * ph_d[phase, pl.ds(r0 + SUBLANES * (rows8 + u), SUBLANES), :]
            for u, acc in enumerate(accs):
                dy1_ref[pl.ds(r0 + SUBLANES * u, SUBLANES), :] = acc
            return carry

        lax.fori_loop(0, tm // conv_rows, conv_rows_at, 0)
        for k0 in range(0, K, tap_group):
            group = taps[k0:k0 + tap_group]

            def tap_rows_at(rb, accs, group=group):
                for u in range(tap_unroll):
                    r0 = pl.multiple_of((rb * tap_unroll + u) * tap_rows, tap_rows)
                    yv = y1_ref[pl.ds(r0, tap_rows), :]
                    accs = tuple(acc + yv * ph_d[phase, pl.ds(r0 + SUBLANES * rows8, tap_rows), :]
                                 for acc, (rows8, phase) in zip(accs, group))
                return accs

            accs = lax.fori_loop(0, tm // (tap_rows * tap_unroll), tap_rows_at,
                                 tuple(jnp.zeros((tap_rows, D), F32) for _ in group))
            for j, acc in enumerate(accs):
                _acc_add(i, dw_ref, _colsum(acc), rows=(slice(k0 + j, k0 + j + 1), slice(None)))

        @pl.when(i == 0)
        def _():
            dw_ref[K:, :] = jnp.zeros((dw_ref.shape[0] - K, D), F32)

        _acc_add(i, dcb_ref, _colsum(d))
        dy1 = dy1_ref[...]
        da = dy1 * sg
        dg = dy1 * a * sg * (1.0 - sg)
        dp_ref[:, :D] = da.astype(BF16)
        dp_ref[:, D:] = dg.astype(BF16)
        _acc_add(i, dpb_ref, jnp.concatenate([_colsum(da), _colsum(dg)], axis=1))

    return _rowcall(name, body, S, tm,
                    [(p, ("row", D, 0)), (p, ("row", D, 1)), (dy2, ROW), (dy2, ("next", HB32, None, 0)), (w, FULL)],
                    [((S, D2), BF16, ROW), ((HB32, D), F32, ACC), ((1, D), F32, ACC), ((1, D2), F32, ACC)],
                    scratch=[pltpu.VMEM((SUBLANES, tm + HB32, D), F32), pltpu.VMEM((tm, D), F32),
                             pltpu.VMEM((tm, D), F32), pltpu.VMEM((K, SUBLANES, D), F32)])


CHUNK = 128
A_GROUPS = 4


def _group_ln(gv):
    ns, rss = [], []
    for g in range(A_GROUPS):
        xg = gv[:, g * LANES:(g + 1) * LANES]
        xc = xg - jnp.mean(xg, axis=-1, keepdims=True)
        rs = lax.rsqrt(jnp.mean(xc * xc, axis=-1, keepdims=True) + EPS)
        ns.append(xc * rs)
        rss.append(jnp.broadcast_to(rs, xg.shape))
    return jnp.concatenate(ns, axis=1), jnp.concatenate(rss, axis=1)


def _tril_mask():
    r = lax.broadcasted_iota(jnp.int32, (CHUNK, CHUNK), 0)
    c = lax.broadcasted_iota(jnp.int32, (CHUNK, CHUNK), 1)
    return r >= c


def _spatial(ws_ref, x, dn):
    mask = _tril_mask()
    rows = []
    for ci in range(x.shape[0] // CHUNK):
        cols = []
        for g in range(A_GROUPS):
            wm = jnp.where(mask, ws_ref[g], 0.0).astype(BF16)
            xb = x[ci * CHUNK:(ci + 1) * CHUNK, g * LANES:(g + 1) * LANES]
            cols.append(lax.dot_general(wm, xb, dn, preferred_element_type=F32))
        rows.append(jnp.concatenate(cols, axis=1))
    return jnp.concatenate(rows, axis=0)


def _mixa_fwd(name, z, vg, ws, bias_full, tm=256):
    S = z.shape[0]
    W = A_GROUPS * LANES

    def body(u_ref, v_ref, vg_ref, ws_ref, b_ref, o_ref):
        nh, _ = _group_ln(_gelu(v_ref[...]))
        vn = (nh * vg_ref[...]).astype(BF16)
        f = _spatial(ws_ref, vn, _DN["nn"]) + jnp.concatenate([b_ref[...]] * (tm // CHUNK), axis=0)
        o_ref[...] = (_gelu(u_ref[...]) * f).astype(BF16)

    return _rowcall(name, body, S, tm,
                    [(z, ("row", W, 0)), (z, ("row", W, 1)), (vg, FULL), (ws, FULL), (bias_full, FULL)],
                    [((S, W), BF16, ROW)])[0]


def _mixa_bwd(name, z, dyab, vg, ws, bias_full, tm=256):
    S = z.shape[0]
    W = A_GROUPS * LANES
    nch = tm // CHUNK

    def body(u_ref, v_ref, d_ref, vg_ref, ws_ref, b_ref, dz_ref, dws_ref, dbf_ref, dvg_ref):
        i = pl.program_id(0)
        u, v, d, vgv = u_ref[...], v_ref[...], d_ref[...], vg_ref[...]
        nh, rs = _group_ln(_gelu(v))
        vn = (nh * vgv).astype(BF16)
        f = _spatial(ws_ref, vn, _DN["nn"]) + jnp.concatenate([b_ref[...]] * nch, axis=0)
        dz_ref[:, :W] = (d * f * _gelu_grad(u)).astype(BF16)
        df = d * _gelu(u)
        dbf = df[0:CHUNK]
        for ci in range(1, nch):
            dbf = dbf + df[ci * CHUNK:(ci + 1) * CHUNK]
        _acc_add(i, dbf_ref, dbf)
        dfb = df.astype(BF16)
        mask = _tril_mask()
        for g in range(A_GROUPS):
            acc = jnp.zeros((CHUNK, CHUNK), F32)
            for ci in range(nch):
                blk = (slice(ci * CHUNK, (ci + 1) * CHUNK), slice(g * LANES, (g + 1) * LANES))
                acc = acc + lax.dot_general(dfb[blk], vn[blk], _DN["nt"], preferred_element_type=F32)
            _acc_add(i, dws_ref, jnp.where(mask, acc, 0.0)[None], rows=(slice(g, g + 1), slice(None), slice(None)))
        dvn = _spatial(ws_ref, dfb, _DN["tn"])
        _acc_add(i, dvg_ref, _colsum(dvn * nh))
        dnh = dvn * vgv
        parts = []
        for g in range(A_GROUPS):
            cs = slice(g * LANES, (g + 1) * LANES)
            dg_, ng = dnh[:, cs], nh[:, cs]
            parts.append(dg_ - jnp.mean(dg_, axis=-1, keepdims=True) - ng * jnp.mean(dg_ * ng, axis=-1, keepdims=True))
        dz_ref[:, W:] = (rs * jnp.concatenate(parts, axis=1) * _gelu_grad(v)).astype(BF16)

    return _rowcall(name, body, S, tm,
                    [(z, ("row", W, 0)), (z, ("row", W, 1)), (dyab, ("row", W, 0)), (vg, FULL), (ws, FULL),
                     (bias_full, FULL)],
                    [((S, 2 * W), BF16, ROW), ((A_GROUPS, CHUNK, CHUNK), F32, ACC), ((CHUNK, W), F32, ACC),
                     ((1, W), F32, ACC)])


HEAD = 64
N_HEADS = 8
BW = HEAD * N_HEADS
QB = 128
DILATIONS = (1, 4, 16)
QK_SCALE = HEAD ** -0.5


def _gsum64(x, ones_bd):
    x1 = x.astype(BF16)
    r1 = x - x1.astype(F32)
    x2 = r1.astype(BF16)
    x3 = (r1 - x2.astype(F32)).astype(BF16)
    dot = lambda t: jnp.dot(t, ones_bd, preferred_element_type=F32)
    return dot(x1) + dot(x2) + dot(x3)


def _swap32(x):
    n = x.shape[-1]
    up = pltpu.roll(x, n - HEAD // 2, axis=1)
    dn = pltpu.roll(x, HEAD // 2, axis=1)
    lane = lax.broadcasted_iota(jnp.int32, x.shape, 1)
    return jnp.where((lane % HEAD) < HEAD // 2, up, dn)


def _tile4(t):
    return jnp.concatenate([t] * (BW // LANES), axis=1)


def _stage_spec(tm):
    return pltpu.VMEM((BW // LANES, tm, LANES), F32)


def _to_classes(stage, x, dil):
    tm = x.shape[0]
    for j in range(BW // LANES):
        stage[j] = x[:, j * LANES:(j + 1) * LANES]
    return [jnp.concatenate([stage.at[j][pl.ds(r, tm // dil, stride=dil), :] for j in range(BW // LANES)], axis=1)
            for r in range(dil)]


def _from_classes(stage, cls, dil):
    rows = cls.shape[1]
    for r in range(dil):
        for j in range(BW // LANES):
            stage.at[j][pl.ds(r, rows, stride=dil), :] = cls[r, :, j * LANES:(j + 1) * LANES]
    return jnp.concatenate([stage[j] for j in range(BW // LANES)], axis=1)


def _cls_view(t, dil):
    return t if dil == 1 else t.reshape(dil, t.shape[0] // dil, t.shape[1])


def _cls_kind(dil, off=0):
    return ("off", off, None, 0) if dil == 1 else ("cls", dil, off)


def _cls_out(S, dil, dtype):
    return ((S, BW) if dil == 1 else (dil, S // dil, BW), dtype, _cls_kind(dil))


def _flat(t):
    return t.reshape(-1, t.shape[-1])


def _qkv_fwd(name, z, cos, sin, ones_bd, qg, kg, tm=256):
    S = z.shape[0]
    nd = len(DILATIONS)

    def body(q_ref, k_ref, v_ref, c_ref, s_ref, o_ref, qg_ref, kg_ref, *rest):
        outs, stage = rest[:3 * nd], rest[3 * nd]
        c, s, ob = _tile4(c_ref[...]), _tile4(s_ref[...]), o_ref[...]

        def norm_rope(x, g):
            r = lax.rsqrt(_gsum64(x * x, ob) * (1.0 / HEAD) + EPS)
            xn = x * r * g
            return xn * c + _swap32(xn) * s

        vals = [norm_rope(q_ref[...], qg_ref[...]) * QK_SCALE, norm_rope(k_ref[...], kg_ref[...]), v_ref[...]]
        for a, val in enumerate(vals):
            for b, dil in enumerate(DILATIONS):
                if dil == 1:
                    outs[nd * a + b][...] = val.astype(BF16)
                else:
                    for r, rows in enumerate(_to_classes(stage, val, dil)):
                        outs[nd * a + b][r] = rows.astype(BF16)

    outs = _rowcall(name, body, S, tm,
                    [(z, ("row", BW, 2)), (z, ("row", BW, 3)), (z, ("row", BW, 4)), (cos, ROW), (sin, ROW),
                     (ones_bd, FULL), (qg, FULL), (kg, FULL)],
                    [_cls_out(S, dil, BF16) for _ in range(3) for dil in DILATIONS], scratch=[_stage_spec(tm)])
    return [[_flat(outs[nd * a + b]) for a in range(3)] for b in range(nd)]


PAIR = 2 * HEAD


ATT_BLOCKS = 2
ATT_TM = ATT_BLOCKS * QB
ATT_PREV = ("prev", QB, None, 0)


def _key_rows(prev_ref, cur_ref, sb, ps):
    before = prev_ref[:, ps] if sb == 0 else cur_ref[(sb - 1) * QB:sb * QB, ps]
    return jnp.concatenate([before, cur_ref[sb * QB:(sb + 1) * QB, ps]], axis=0)


def _pair_scores(q_ref, kp_ref, kc_ref, sb, hp, half, seg_blocks):
    ps = slice(hp * PAIR, (hp + 1) * PAIR)
    mine = (lax.broadcasted_iota(jnp.int32, (1, PAIR), 1) >= HEAD) == (half == 1)
    qm = jnp.where(mine, q_ref[sb * QB:(sb + 1) * QB, ps], jnp.zeros((), BF16))
    kcat = _key_rows(kp_ref, kc_ref, sb, ps)
    s = lax.dot_general(qm, kcat, _DN["nt"], preferred_element_type=F32)
    qi = lax.broadcasted_iota(jnp.int32, (QB, 2 * QB), 0)
    kj = lax.broadcasted_iota(jnp.int32, (QB, 2 * QB), 1)
    has_prev = ((pl.program_id(0) * ATT_BLOCKS + sb) % seg_blocks) != 0
    valid = (kj >= qi) & (kj <= qi + QB) & ((kj >= QB) | has_prev)
    return mine, qm, kcat, s, valid


def _attn_fwd(name, q, k, v, dil):
    S = q.shape[0]
    seg_blocks = S // dil // QB

    def body(q_ref, kp_ref, kc_ref, vp_ref, vc_ref, o_ref, l_ref):
        for hp in range(N_HEADS // 2):
            ps = slice(hp * PAIR, (hp + 1) * PAIR)
            chains = [(sb, half) for sb in range(ATT_BLOCKS) for half in range(2)]
            sc = [_pair_scores(q_ref, kp_ref, kc_ref, sb, hp, half, seg_blocks) for sb, half in chains]
            ss = [jnp.where(valid, s, NEG) for _, _, _, s, valid in sc]
            ms = [jnp.max(s, axis=-1, keepdims=True) for s in ss]
            pv = [jnp.exp(s - m) for s, m in zip(ss, ms)]
            dens = [jnp.sum(p, axis=-1, keepdims=True) for p in pv]
            vcats = [_key_rows(vp_ref, vc_ref, sb, ps) for sb in range(ATT_BLOCKS)]
            outs = [jnp.dot(p.astype(BF16), vcats[sb], preferred_element_type=F32) / den
                    for p, den, (sb, _) in zip(pv, dens, chains)]
            lses = [jnp.broadcast_to(m + jnp.log(den), (QB, PAIR)) for m, den in zip(ms, dens)]
            for sb in range(ATT_BLOCKS):
                rows, upper = slice(sb * QB, (sb + 1) * QB), sc[2 * sb + 1][0]
                o_ref[rows, ps] = jnp.where(upper, outs[2 * sb + 1], outs[2 * sb])
                l_ref[rows, ps] = jnp.where(upper, lses[2 * sb + 1], lses[2 * sb])

    return _rowcall(name, body, S, ATT_TM, [(q, ROW), (k, ATT_PREV), (k, ROW), (v, ATT_PREV), (v, ROW)],
                    [((S, BW), F32, ROW)] * 2)


def _attn_bwd(name, q, k, v, do, lse, delta, dil):
    S = q.shape[0]
    seg_blocks = S // dil // QB

    def body(q_ref, kp_ref, kc_ref, vp_ref, vc_ref, do_ref, l_ref, dl_ref, dq_ref, dkc_ref, dkp_ref, dvc_ref, dvp_ref):
        for hp in range(N_HEADS // 2):
            ps = slice(hp * PAIR, (hp + 1) * PAIR)
            chains = [(sb, half) for sb in range(ATT_BLOCKS) for half in range(2)]
            rows = [slice(sb * QB, (sb + 1) * QB) for sb, _ in chains]
            cols = [hp * PAIR + half * HEAD for _, half in chains]
            sc = [_pair_scores(q_ref, kp_ref, kc_ref, sb, hp, half, seg_blocks) for sb, half in chains]
            pv = [jnp.where(valid, jnp.exp(s - l_ref[r, c:c + 1]), 0.0) for (_, _, _, s, valid), r, c in zip(sc, rows, cols)]
            vcats = [_key_rows(vp_ref, vc_ref, sb, ps) for sb in range(ATT_BLOCKS)]
            doms = [jnp.where(mine, do_ref[r, ps].astype(BF16), jnp.zeros((), BF16)) for (mine, *_), r in zip(sc, rows)]
            dps = [lax.dot_general(dom, vcats[sb], _DN["nt"], preferred_element_type=F32) for dom, (sb, _) in zip(doms, chains)]
            dss = [(p * (dp - dl_ref[r, c:c + 1])).astype(BF16) for p, dp, r, c in zip(pv, dps, rows, cols)]
            dqs = [jnp.dot(ds, kcat, preferred_element_type=F32) for ds, (_, _, kcat, _, _) in zip(dss, sc)]
            dks = [lax.dot_general(ds, qm, _DN["tn"], preferred_element_type=F32) for ds, (_, qm, *_) in zip(dss, sc)]
            dvs = [lax.dot_general(p.astype(BF16), dom, _DN["tn"], preferred_element_type=F32) for p, dom in zip(pv, doms)]
            for sb in range(ATT_BLOCKS):
                lo, hi = 2 * sb, 2 * sb + 1
                dk, dv = dks[lo] + dks[hi], dvs[lo] + dvs[hi]
                dq_ref[rows[lo], ps] = jnp.where(sc[hi][0], dqs[hi], dqs[lo])
                dkp_ref[rows[lo], ps] = dk[:QB]
                dkc_ref[rows[lo], ps] = dk[QB:]
                dvp_ref[rows[lo], ps] = dv[:QB]
                dvc_ref[rows[lo], ps] = dv[QB:]

    return _rowcall(name, body, S, ATT_TM,
                    [(q, ROW), (k, ATT_PREV), (k, ROW), (v, ATT_PREV), (v, ROW), do, (lse, ROW), (delta, ROW)],
                    [((S, BW), F32, ROW)] * 5)


def _merge_fwd(name, branches, tm=256):
    S = branches[0][0].shape[0]
    nd = len(DILATIONS)

    def body(*refs):
        ins, (y_ref, yb_ref), l_refs, stage = refs[:2 * nd], refs[2 * nd:2 * nd + 2], refs[2 * nd + 2:3 * nd + 2], refs[-1]
        os_, ls = [], []
        for b, dil in enumerate(DILATIONS):
            o, l = ins[2 * b][...], ins[2 * b + 1][...]
            os_.append(o if dil == 1 else _from_classes(stage, o, dil))
            ls.append(l if dil == 1 else _from_classes(stage, l, dil))
        m = functools.reduce(jnp.maximum, ls)
        es = [jnp.exp(l - m) for l in ls]
        den = functools.reduce(lambda a, e: a + e, es)
        y = functools.reduce(lambda a, t: a + t, [e * o for e, o in zip(es, os_)]) / den
        y_ref[...] = y
        yb_ref[...] = y.astype(BF16)
        lse = m + jnp.log(den)
        for b, dil in enumerate(DILATIONS):
            if dil == 1:
                l_refs[b][...] = lse
            else:
                for r, rows in enumerate(_to_classes(stage, lse, dil)):
                    l_refs[b][r] = rows

    ins = [(_cls_view(t, dil), _cls_kind(dil)) for pair, dil in zip(branches, DILATIONS) for t in pair]
    outs = _rowcall(name, body, S, tm, ins,
                    [((S, BW), F32, ROW), ((S, BW), BF16, ROW)] + [_cls_out(S, dil, F32) for dil in DILATIONS],
                    scratch=[_stage_spec(tm)])
    return outs[0], outs[1], [_flat(t) for t in outs[2:]]


def _delta(name, dyab, yb, ones_bd, tm=256):
    S = yb.shape[0]
    nd = len(DILATIONS)

    def body(d_ref, y_ref, o_ref, *rest):
        dl_refs, do_refs, stage = rest[:nd], rest[nd:2 * nd - 1], rest[-1]
        d = d_ref[...]
        dl = _gsum64(d * y_ref[...], o_ref[...])
        for b, dil in enumerate(DILATIONS):
            if dil == 1:
                dl_refs[b][...] = dl
            else:
                for r, rows in enumerate(_to_classes(stage, dl, dil)):
                    dl_refs[b][r] = rows
                for r, rows in enumerate(_to_classes(stage, d, dil)):
                    do_refs[b - 1][r] = rows.astype(BF16)

    outs = _rowcall(name, body, S, tm, [(dyab, ("row", BW, 1)), (yb, ROW), (ones_bd, FULL)],
                    [_cls_out(S, dil, F32) for dil in DILATIONS] + [_cls_out(S, dil, BF16) for dil in DILATIONS[1:]],
                    scratch=[_stage_spec(tm)])
    return [_flat(t) for t in outs[:nd]], [_flat(t) for t in outs[nd:]]


def _qkv_bwd(name, z, cos, sin, ones_bd, qg, kg, pieces):
    S = z.shape[0]
    nblk = S // QB

    def body(q_ref, k_ref, c_ref, s_ref, o_ref, qg_ref, kg_ref, *rest):
        pr, (dz_ref, dqg_ref, dkg_ref), stage = rest[:15], rest[15:18], rest[18]
        i = pl.program_id(0)
        c, s, ob = _tile4(c_ref[...]), _tile4(s_ref[...]), o_ref[...]
        dq = dk = dv = None
        for b, dil in enumerate(DILATIONS):
            a_q, a_kc, a_kp, a_vc, a_vp = [r[...] for r in pr[5 * b:5 * b + 5]]
            live = ((i + dil) < nblk).astype(F32)
            tq, tk, tv = a_q, a_kc + a_kp * live, a_vc + a_vp * live
            if dil > 1:
                tq, tk, tv = (_from_classes(stage, t, dil) for t in (tq, tk, tv))
            dq, dk, dv = (tq, tk, tv) if b == 0 else (dq + tq, dk + tk, dv + tv)

        def back(x, g, d_rot, acc_ref):
            r = lax.rsqrt(_gsum64(x * x, ob) * (1.0 / HEAD) + EPS)
            n = x * r
            dxn = d_rot * c + _swap32(d_rot * s)
            _acc_add(i, acc_ref, _colsum(dxn * n))
            dn = dxn * g
            return r * (dn - n * (_gsum64(dn * n, ob) * (1.0 / HEAD)))

        dz_ref[:, :BW] = back(q_ref[...], qg_ref[...], dq * QK_SCALE, dqg_ref).astype(BF16)
        dz_ref[:, BW:2 * BW] = back(k_ref[...], kg_ref[...], dk, dkg_ref).astype(BF16)
        dz_ref[:, 2 * BW:] = dv.astype(BF16)

    ins = [(z, ("row", BW, 2)), (z, ("row", BW, 3)), (cos, ROW), (sin, ROW), (ones_bd, FULL), (qg, FULL), (kg, FULL)]
    for piece, dil in zip(pieces, DILATIONS):
        a_q, a_kc, a_kp, a_vc, a_vp = (_cls_view(t, dil) for t in piece)
        own, prev = _cls_kind(dil), _cls_kind(dil, dil)
        ins += [(a_q, own), (a_kc, own), (a_kp, prev), (a_vc, own), (a_vp, prev)]
    return _rowcall(name, body, S, QB, ins,
                    [((S, 3 * BW), BF16, ROW), ((1, BW), F32, ACC), ((1, BW), F32, ACC)], scratch=[_stage_spec(QB)])


def _local_step(x0, tgt, pos, mod, wb, sp, pipe):
    S, D = x0.shape
    md = lambda l, j: mod[l, j:j + 1]
    sh_m, sc_m, g_m, sh_f, sc_f, g_f = ([md(l, j) for l in range(2)] for j in range(6))
    nm_g, nf_g = sp["norm_mix_g"], sp["norm_ffn_g"]

    inv_freq = 1.0 / (ROPE_THETA ** (jnp.arange(0, HEAD, 2, dtype=F32) / HEAD))
    ang = pos.astype(F32)[:, None] * inv_freq
    cs, sn = jnp.cos(ang), jnp.sin(ang)
    cos = jnp.concatenate([cs, cs, cs, cs], axis=1)
    sin = jnp.concatenate([-sn, sn, -sn, sn], axis=1)
    head_of = jnp.arange(BW) // HEAD
    ones_bd = (head_of[:, None] == head_of[None, :]).astype(BF16)
    qg = jnp.tile(sp["b_q_norm_g"].reshape(1, HEAD), (1, N_HEADS))
    kg = jnp.tile(sp["b_k_norm_g"].reshape(1, HEAD), (1, N_HEADS))
    vg = sp["a_vnorm_g"].reshape(1, A_GROUPS * LANES)
    ws = sp["a_spatial_w"][0]
    bias_full = jnp.repeat(sp["a_spatial_b"][0].T, LANES, axis=1)
    ffn_s = [(sp["ffn_dw_w"][l], sp["ffn_dw_b"][l:l + 1]) for l in range(2)]

    h0 = _mod_first("l0_mod", x0, nm_g[0:1], sc_m[0], sh_m[0])
    z = _matmul("l0_in", h0, wb.get("w_in", h0), "nn", F32, tm=2048)
    ya = _mixa_fwd("l0_mixa", z, vg, ws, bias_full)
    qkv = _qkv_fwd("l0_qkv", z, cos, sin, ones_bd, qg, kg)
    branches = [_attn_fwd(f"l0_att{dil}", *qkv[b], dil) for b, dil in enumerate(DILATIONS)]
    yb, yb16, lses = _merge_fwd("l0_merge", branches)
    yab = jnp.concatenate([ya, yb16], axis=1)
    y0 = _matmul("l0_out", yab, wb.get("w_out", yab), "nn", F32, tm=1024, tn=1024)
    x1, h1 = _resid_mod("l0_res1", x0, y0, g_m[0], nf_g[0:1], sc_f[0], sh_f[0])
    ffn_w = [(wb.get("up0", h1), wb.get("dn0", h1), *ffn_s[0])]
    f0, saved0 = _ffn_fwd("l0_ffn", h1, *ffn_w[0])
    x2, h2 = _resid_mod("l0_res2", x1, f0, g_f[0], nm_g[1:2], sc_m[1], sh_m[1])
    p = _matmul("l1_pw1", h2, wb.get("pw1", h2), "nn", F32, tm=2048, bias=sp["conv_pw1_b"])
    y2 = _glu31_fwd("l1_glu", p, sp["conv_dw_w"][0], sp["conv_dw_b"])
    y4 = _ln_silu_fwd("l1_ln", y2, sp["conv_ln_g"], sp["conv_ln_b"])
    y1 = _matmul("l1_pw2", y4, wb.get("pw2", y4), "nn", F32, tm=1024, tn=1024, bias=sp["conv_pw2_b"])
    x3, h3 = _resid_mod("l1_res1", x2, y1, g_m[1], nf_g[1:2], sc_f[1], sh_f[1])
    ffn_w.append((wb.get("up1", h3), wb.get("dn1", h3), *ffn_s[1]))
    f1, saved1 = _ffn_fwd("l1_ffn", h3, *ffn_w[1])
    dx4, lossv, dy, dgate_f1, _ = _loss_head("loss", x3, f1, g_f[1], tgt)

    dh, gf1 = _ffn_bwd("l1_ffn", dy, h3, *saved1, *ffn_w[1])
    tok = pipe.scatter("g1", dict(dn1=gf1["dn"], up1=gf1["up"]))
    dx3, dsh_f1, dsc_f1, dnf1, dy, dgate_m1, dpw2_b = _mod_bwd("l1_dmod2", dx4, dh, x3, nf_g[1:2], sc_f[1] + tok,
                                                             y1, g_m[1])
    dy4 = _matmul("l1_dpw2_x", dy, wb.get("pw2"), "nt", F32, tm=1024, tn=1024)
    g_pw2 = _matmul("l1_dpw2_w", y4, dy, "tn", BF16, tm=1024, tn=1024, tk=2048)
    dy2, dln_g, dln_b = _ln_silu_bwd("l1_dln", y2, dy4, sp["conv_ln_g"], sp["conv_ln_b"])
    dp, ddw_w, ddw_b, dpw1_b = _glu31_bwd("l1_dglu", p, dy2, sp["conv_dw_w"][0])
    g_pw1 = _matmul("l1_dpw1_w", h2, dp, "tn", BF16, tm=1024, tn=1024, tk=2048)
    tok = pipe.scatter("g2", dict(pw2=g_pw2, pw1=g_pw1))
    pipe.collect("g1", g_pw1)
    dh = _matmul("l1_dpw1_x", dp, wb.get("pw1"), "nt", F32, tm=1024, tn=1024, tk=2048)
    dx2, dsh_m1, dsc_m1, dnm1, dy, dgate_f0, _ = _mod_bwd("l1_dmod1", dx3, dh, x2, nm_g[1:2], sc_m[1] + tok,
                                                        f0, g_f[0])
    dh, gf0 = _ffn_bwd("l0_ffn", dy, h1, *saved0, *ffn_w[0])
    tok = pipe.scatter("g3", dict(dn0=gf0["dn"], up0=gf0["up"]))
    pipe.collect("g2", gf0["up"])
    dx1, dsh_f0, dsc_f0, dnf0, dy, dgate_m0, _ = _mod_bwd("l0_dmod2", dx2, dh, x1, nf_g[0:1], sc_f[0] + tok,
                                                        y0, g_m[0])
    dyab = _matmul("l0_dout_x", dy, wb.get("w_out"), "nt", F32, tm=1024, tn=1024)
    g_out = _matmul("l0_dout_w", yab, dy, "tn", BF16, tm=1024, tn=1024, tk=2048)
    vg = vg + pipe.scatter("g4", dict(w_out=g_out))
    dza, dws, dbf, dvg = _mixa_bwd("l0_dmixa", z, dyab, vg, ws, bias_full)
    deltas, dos = _delta("l0_delta", dyab, yb, ones_bd)
    pieces = []
    for b, dil in enumerate(DILATIONS):
        do = (dyab, ("row", BW, 1)) if dil == 1 else (dos[b - 1], ROW)
        pieces.append(_attn_bwd(f"l0_datt{dil}", *qkv[b], do, lses[b], deltas[b], dil))
    dzb, dqg, dkg = _qkv_bwd("l0_dqkv", z, cos, sin, ones_bd, qg, kg, pieces)
    small_tok = _start_small_grads(pipe, dzb, sp, lossv, ((dgate_m0, dsh_f0, dsc_f0, dgate_f0),
                                                   (dsh_m1, dsc_m1, dgate_m1, dsh_f1, dsc_f1, dgate_f1)),
                                   (dnm1, dnf0, dnf1), dvg, dws, dbf, dqg, dkg, dpw1_b, ddw_w, ddw_b, dln_g, dln_b,
                                   dpw2_b, gf0, gf1)
    dz = jnp.concatenate([dza, dzb], axis=1)
    g_in = _matmul("l0_din_w", h0, dz, "tn", BF16, tm=1024, tn=1280, tk=2048, after=small_tok)
    dh = _matmul("l0_din_x", dz, wb.get("w_in"), "nt", F32, tm=1024, tn=1024, tk=2560, after=small_tok)
    grad_x, dsh_m0, dsc_m0, dnm0 = _mod_bwd("l0_dmod1", dx1, dh, x0, nm_g[0:1], sc_m[0])
    return lossv, grad_x, (dsh_m0, dsc_m0, dnm0), g_in


def _start_small_grads(pipe, after, sp, lossv, mods, norms, dvg, dws, dbf, dqg, dkg, dpw1_b, ddw_w, ddw_b, dln_g,
                       dln_b, dpw2_b, gf0, gf1):
    (dgate_m0, dsh_f0, dsc_f0, dgate_f0), (dsh_m1, dsc_m1, dgate_m1, dsh_f1, dsc_f1, dgate_f1) = mods
    dnm1, dnf0, dnf1 = norms
    zero = jnp.zeros_like(dnm1)
    dmod = jnp.stack([jnp.concatenate([zero, zero, dgate_m0, dsh_f0, dsc_f0, dgate_f0], axis=0),
                      jnp.concatenate([dsh_m1, dsc_m1, dgate_m1, dsh_f1, dsc_f1, dgate_f1], axis=0)])
    small = dict(
        norm_mix_g=jnp.concatenate([zero, dnm1], axis=0),
        norm_ffn_g=jnp.concatenate([dnf0, dnf1], axis=0),
        a_vnorm_g=dvg.reshape(1, A_GROUPS, LANES),
        a_spatial_w=dws[None],
        a_spatial_b=dbf.reshape(CHUNK, A_GROUPS, LANES).sum(-1).T[None],
        b_q_norm_g=dqg.reshape(N_HEADS, HEAD).sum(0)[None],
        b_k_norm_g=dkg.reshape(N_HEADS, HEAD).sum(0)[None],
        conv_pw1_b=dpw1_b, conv_dw_w=ddw_w[None, :sp["conv_dw_w"].shape[1]], conv_dw_b=ddw_b,
        conv_ln_g=dln_g, conv_ln_b=dln_b, conv_pw2_b=dpw2_b,
        ffn_dw_w=jnp.stack([gf0["dw_w"], gf1["dw_w"]]),
        ffn_dw_b=jnp.concatenate([gf0["dw_b"], gf1["dw_b"]], axis=0),
    )
    return pipe.start_small(dmod, lossv, small, after)


ADA_TN = 512


def _ada_fwd(name, c_all, ada_w, ada_b_sh):
    L, D, N = ada_w.shape
    B = c_all.shape[0]

    def body(c_ref, w_ref, b_ref, o_ref):
        cv = c_ref[...]
        ca = (cv * _sigmoid(cv)).astype(BF16)
        o_ref[0] = jnp.dot(ca, w_ref[0].astype(BF16), preferred_element_type=F32) + b_ref[0]

    return pl.pallas_call(
        body, name=name, grid=(L, N // ADA_TN),
        in_specs=[pl.BlockSpec((B, D), lambda l, j: (0, 0)), pl.BlockSpec((1, D, ADA_TN), lambda l, j: (l, 0, j)),
                  pl.BlockSpec((1, 1, ADA_TN), lambda l, j: (l, 0, j))],
        out_specs=pl.BlockSpec((1, B, ADA_TN), lambda l, j: (l, 0, j)),
        out_shape=jax.ShapeDtypeStruct((L, B, N), F32),
        compiler_params=_params(("parallel", "parallel")),
    )(c_all, ada_w, ada_b_sh.reshape(L, 1, N))


def _adamw_val(w, g, m, v):
    m2 = ADAM_B1 * m + (1.0 - ADAM_B1) * g
    v2 = ADAM_B2 * v + (1.0 - ADAM_B2) * (g * g)
    m_hat = m2 / (1.0 - ADAM_B1 ** ADAM_STEP)
    v_hat = v2 / (1.0 - ADAM_B2 ** ADAM_STEP)
    delta = -ADAM_LR * (m_hat / (jnp.sqrt(v_hat) + ADAM_EPS) + ADAM_WD * w)
    return delta, m2, v2


def _ada_update(name, c_all, dmod_sh, w, m, v):
    L, D, N = w.shape
    B = c_all.shape[0]

    def body(c_ref, d_ref, w_ref, m_ref, v_ref, g_ref, dl_ref, mo_ref, vo_ref):
        cv = c_ref[...]
        ca = (cv * _sigmoid(cv)).astype(BF16)
        g = lax.dot_general(ca, d_ref[0].astype(BF16), _DN["tn"], preferred_element_type=F32)
        g_ref[0] = g
        dl_ref[0], mo_ref[0], vo_ref[0] = _adamw_val(w_ref[0], g, m_ref[0], v_ref[0])

    wspec = pl.BlockSpec((1, D, ADA_TN), lambda l, j: (l, 0, j))
    return pl.pallas_call(
        body, name=name, grid=(L, N // ADA_TN),
        in_specs=[pl.BlockSpec((B, D), lambda l, j: (0, 0)), pl.BlockSpec((1, B, ADA_TN), lambda l, j: (l, 0, j)),
                  wspec, wspec, wspec],
        out_specs=[wspec] * 4, out_shape=[jax.ShapeDtypeStruct((L, D, N), F32)] * 4,
        compiler_params=_params(("parallel", "parallel")),
    )(c_all, dmod_sh, w, m, v)


def _adamw(name, w, g, m, v):
    R, C = w.shape
    tm = R
    for cand in (256, 128, 64, 32, 16, 8):
        if R % cand == 0 and cand * C * 4 <= (1 << 20):
            tm = cand
            break

    def body(w_ref, g_ref, m_ref, v_ref, d_ref, mo_ref, vo_ref):
        d_ref[...], mo_ref[...], vo_ref[...] = _adamw_val(w_ref[...], g_ref[...], m_ref[...], v_ref[...])

    return _rowcall(name, body, R, tm, [(w, ROW), (g, ROW), (m, ROW), (v, ROW)], [((R, C), F32, ROW)] * 3)


def _row_tile(rows, width, itemsize=4, limit=1 << 20):
    for cand in (512, 256, 128, 64, 32, 16):
        if rows % cand == 0 and cand * width * itemsize <= limit:
            return cand
    raise ValueError((rows, width))


def _cast_into_full(name, a, layer, q, kind, after):
    L, r, c = a.shape
    tm = _row_tile(r, c)
    if kind == "col":
        full, o_spec = (r, N_CHIPS * c), pl.BlockSpec((tm, c), lambda i, q_ref: (i, q_ref[0]))
    else:
        full, o_spec = (N_CHIPS * r, c), pl.BlockSpec((tm, c), lambda i, q_ref: (q_ref[0] * (r // tm) + i, 0))

    def body(q_ref, a_ref, after_ref, o_ref):
        o_ref[...] = a_ref[0].astype(BF16)

    return pl.pallas_call(
        body, name=name,
        grid_spec=pltpu.PrefetchScalarGridSpec(
            num_scalar_prefetch=1, grid=(r // tm,),
            in_specs=[pl.BlockSpec((1, tm, c), lambda i, q_ref: (layer, i, 0)), ANY], out_specs=o_spec),
        out_shape=jax.ShapeDtypeStruct(full, BF16), compiler_params=_params(("parallel",)),
    )(q.reshape(1).astype(jnp.int32), a, after)


def _sum4(name, g, rcv, q, kind, n):
    r, c = rcv.shape[1:]
    tm = _row_tile(r, c)
    if kind == "col":
        g_spec = pl.BlockSpec((tm, n), lambda i, q_ref: (i, q_ref[0]))
    else:
        g_spec = pl.BlockSpec((tm, c), lambda i, q_ref: (q_ref[0] * (n // tm) + i, 0))

    def body(q_ref, g_ref, r_ref, o_ref):
        acc = g_ref[...].astype(F32)
        for j in range(3):
            acc = acc + r_ref[j].astype(F32)
        o_ref[...] = acc

    return pl.pallas_call(
        body, name=name,
        grid_spec=pltpu.PrefetchScalarGridSpec(
            num_scalar_prefetch=1, grid=(r // tm,),
            in_specs=[g_spec, pl.BlockSpec((3, tm, c), lambda i, q_ref: (0, i, 0))],
            out_specs=pl.BlockSpec((tm, c), lambda i, q_ref: (i, 0))),
        out_shape=jax.ShapeDtypeStruct((r, c), F32), compiler_params=_params(("parallel",)),
    )(q.reshape(1).astype(jnp.int32), g, rcv)


def _adamw_sum(name, w, m, v, layer, mine, theirs, prev):
    L, r, c = w.shape
    tm = _row_tile(r, c, limit=3 << 19)
    lay = pl.BlockSpec((1, tm, c), lambda i: (layer, i, 0))
    flat = pl.BlockSpec((tm, c), lambda i: (i, 0))
    n_prev = 0 if prev is None else 4

    def body(w_ref, m_ref, v_ref, a_ref, b_ref, *rest):
        g_ref, d_ref, mo_ref, vo_ref = rest[n_prev:]
        g = a_ref[...] + b_ref[...]
        g_ref[0] = g
        d_ref[0], mo_ref[0], vo_ref[0] = _adamw_val(w_ref[0], g, m_ref[0], v_ref[0])

    return pl.pallas_call(
        body, name=name, grid=(r // tm,),
        in_specs=[lay, lay, lay, flat, flat] + [ANY] * n_prev, out_specs=[lay] * 4,
        out_shape=[jax.ShapeDtypeStruct((L, r, c), F32)] * 4,
        input_output_aliases={5 + k: k for k in range(n_prev)},
        compiler_params=_params(("parallel",)),
    )(w, m, v, mine, theirs, *(prev or ()))


def _sum8(name, gathered, own=None):
    R, N = gathered.shape
    P = R // 8

    def body(g_ref, *rest):
        o_ref = rest[-1]
        me = 4 * lax.axis_index("x") + 2 * lax.axis_index("y") + lax.axis_index("c")
        acc = None
        for d in range(8):
            blk = g_ref[d * P:(d + 1) * P, :]
            if own is not None:
                blk = jnp.where(me == d, rest[0][...], blk)
            acc = blk if d == 0 else acc + blk
        o_ref[...] = acc

    return pl.pallas_call(body, name=name, out_shape=jax.ShapeDtypeStruct((P, N), F32),
                          compiler_params=pltpu.CompilerParams(vmem_limit_bytes=VMEM_LIMIT),
                          )(gathered, *(() if own is None else (own,)))


ANY = pl.BlockSpec(memory_space=pl.ANY)


def _mesh_pos():
    x, y, c = lax.axis_index("x"), lax.axis_index("y"), lax.axis_index("c")
    other_chips = [(1 - x, y), (x, 1 - y), (1 - x, 1 - y)]
    return x, y, c, other_chips


def _allgather8(name, blk, after=None):
    m_per, n = blk.shape

    def body(x_ref, *rest):
        out_ref, send_sems, recv_sems, local_sem = rest[after is not None:]
        x, y, c, chips = _mesh_pos()
        me, sibling = (x, y, c), (x, y, 1 - c)

        def rows(px, py, pc):
            return out_ref.at[pl.ds((4 * px + 2 * py + pc) * m_per, m_per), :]

        def copy(k, block, to, src=None):
            return pltpu.make_async_remote_copy(
                src_ref=rows(*block) if src is None else src, dst_ref=rows(*block),
                send_sem=send_sems.at[k], recv_sem=recv_sems.at[k], device_id=to, device_id_type=MESH)

        mine = pltpu.make_async_copy(x_ref, rows(*me), local_sem)
        mine.start()
        first = [copy(0, me, sibling, src=x_ref)]
        first += [copy(1 + j, me, (*chip, c), src=x_ref) for j, chip in enumerate(chips)]
        for cp in first:
            cp.start()
        passed = [copy(4 + j, (*chip, c), sibling) for j, chip in enumerate(chips)]
        for j, chip in enumerate(chips):
            copy(1 + j, (*chip, c), me).wait_recv()
            passed[j].start()
        copy(0, sibling, me).wait_recv()
        for j, chip in enumerate(chips):
            copy(4 + j, (*chip, 1 - c), me).wait_recv()
        for cp in first + passed:
            cp.wait_send()
        mine.wait()

    return pl.pallas_call(
        body, name=name, out_shape=jax.ShapeDtypeStruct((8 * m_per, n), blk.dtype),
        in_specs=[pl.BlockSpec(memory_space=pltpu.VMEM)] + [ANY] * (after is not None),
        out_specs=pl.BlockSpec(memory_space=pltpu.VMEM),
        scratch_shapes=[pltpu.SemaphoreType.DMA((7,)), pltpu.SemaphoreType.DMA((7,)), pltpu.SemaphoreType.DMA],
        compiler_params=pltpu.CompilerParams(vmem_limit_bytes=VMEM_LIMIT),
    )(blk, *(() if after is None else (after,)))


BIG = dict(w_in=("col", "ab_w_in", 0), w_out=("row", "ab_w_out", 0), up0=("col", "ffn_up_w", 0),
           dn0=("row", "ffn_down_w", 0), pw1=("col", "conv_pw1_w", 0), pw2=("row", "conv_pw2_w", 0),
           up1=("col", "ffn_up_w", 1), dn1=("row", "ffn_down_w", 1))
N_CHIPS = 4
HBM = pl.BlockSpec(memory_space=pltpu.HBM)
SEM = pl.BlockSpec(memory_space=pltpu.SEMAPHORE)
EFFECT = pltpu.SideEffectType.DATAFLOW_SIDE_EFFECTING


def _region(kind, ref, q, n):
    if kind == "col":
        return ref.at[:, pl.ds(q * n, n)]
    return ref.at[pl.ds(q * n, n), :]


def _gather_plan(kind, n):
    def remote(src, land, pos):
        x, y, c, chips = pos
        mine = _region(kind, land, 2 * x + y, n)
        return [(mine, mine, (*chip, c)) for chip in chips]

    return ("gather", kind, n), remote


def _scatter_plan(kind, n):
    def remote(src, land, pos):
        _, _, c, chips = pos
        return [(_region(kind, src, 2 * chip[0] + chip[1], n), land.at[j], (*chip, c)) for j, chip in enumerate(chips)]

    return ("scatter", kind, n), remote


def _everyone_plan(rows):
    def remote(src, land, pos):
        x, y, c, _ = pos
        mine = land.at[pl.ds((4 * x + 2 * y + c) * rows, rows), :]
        flip = lambda v, bit: 1 - v if bit else v
        return [(src, mine, (flip(x, k & 4), flip(y, k & 2), flip(c, k & 1))) for k in range(1, 8)]

    return ("everyone", rows), remote


def _sibling_plan():
    def remote(src, land, pos):
        x, y, c, _ = pos
        return [(src, land, (x, y, 1 - c))]

    return ("sibling",), remote


def _split_start(name, items, after=None):
    n = len(items)
    plans = [it[2] for it in items]
    n_in = 2 * n + (after is not None)

    def body(*refs):
        srcs, lands = refs[:n], refs[n:2 * n]
        sends, recvs = refs[n_in:n_in + n], refs[n_in + n:n_in + 2 * n]
        token = refs[n_in + 4 * n]
        pos = _mesh_pos()
        for a, (_, remote) in enumerate(plans):
            for k, (s, d, dev) in enumerate(remote(srcs[a], lands[a], pos)):
                pltpu.make_async_remote_copy(src_ref=s, dst_ref=d, send_sem=sends[a].at[k], recv_sem=recvs[a].at[k],
                                             device_id=dev, device_id_type=MESH).start()
        token[...] = jnp.zeros_like(token)

    sems = [pltpu.SemaphoreType.DMA((it[3],)) for it in items]
    bufs = [pltpu.HBM(it[k].shape, it[k].dtype) for k in (0, 1) for it in items]
    outs = pl.pallas_call(
        body, name=name, out_shape=[*sems, *sems, *bufs, jax.ShapeDtypeStruct((8, LANES), F32)],
        in_specs=[HBM] * (2 * n) + [ANY] * (n_in - 2 * n),
        out_specs=[SEM] * (2 * n) + [HBM] * (2 * n) + [pl.BlockSpec(memory_space=pltpu.VMEM)],
        input_output_aliases={i: 2 * n + i for i in range(2 * n)},
        compiler_params=pltpu.CompilerParams(has_side_effects=EFFECT),
    )(*[pltpu.with_memory_space_constraint(it[k], pltpu.HBM) for k in (0, 1) for it in items],
      *(() if after is None else (after,)))
    state = [(items[a][2], items[a][3], outs[2 * n + a], outs[3 * n + a], outs[a], outs[n + a]) for a in range(n)]
    return state, outs[4 * n]


def _split_wait(name, state, after):
    n = len(state)

    def body(*refs):
        srcs, lands = refs[:n], refs[n:2 * n]
        sends, recvs = refs[2 * n:3 * n], refs[3 * n:4 * n]
        pos = _mesh_pos()
        for a, ((_, remote), *_) in enumerate(state):
            for k, (s, d, dev) in enumerate(remote(srcs[a], lands[a], pos)):
                cp = pltpu.make_async_remote_copy(src_ref=s, dst_ref=d, send_sem=sends[a].at[k], recv_sem=recvs[a].at[k],
                                                  device_id=dev, device_id_type=MESH)
                cp.wait_send()
                cp.wait_recv()

    bufs = [st[k] for k in (2, 3) for st in state]
    outs = pl.pallas_call(
        body, name=name, out_shape=[pltpu.HBM(b.shape, b.dtype) for b in bufs],
        in_specs=[HBM] * (2 * n) + [SEM] * (2 * n) + [ANY], out_specs=[HBM] * (2 * n),
        input_output_aliases={i: i for i in range(2 * n)},
        compiler_params=pltpu.CompilerParams(has_side_effects=EFFECT),
    )(*bufs, *[st[k] for k in (4, 5) for st in state], after)
    return outs[:n], outs[n:]


class _Weights:
    def __init__(self, w, q, after):
        unused = jnp.zeros((16, LANES), BF16)

        def item(name, after):
            kind, pname, layer = BIG[name]
            _, r, c = w[pname].shape
            land = _cast_into_full(f"cast_{name}", w[pname], layer, q, kind, after)
            return unused, land, _gather_plan(kind, c if kind == "col" else r), N_CHIPS - 1

        first, *rest = BIG
        state1, token1 = _split_start("gw_start_first", [item(first, after)], after)
        state2, self.token = _split_start("gw_start_rest", [item(name, token1) for name in rest], token1)
        self.pending = dict(zip(BIG, state1 + state2))
        self.ready = {}

    def get(self, name, after=None):
        if name not in self.ready:
            self.ready[name] = _split_wait(f"gw_wait_{name}", [self.pending.pop(name)], after)[1][0]
        return self.ready[name]


class _GradPipe:
    def __init__(self, q, w, m, v):
        self.q, self.w, self.m, self.v = q, w, m, v
        self.stage, self.results = {}, {}

    def start_small(self, dmod, lossv, small, after):
        self.small_names = [n for n in REPLICATED if n != "ada_b"] + list(SMALL_SHARDED)
        payload = [dmod.reshape(2, -1), lossv] + [small[n] for n in self.small_names]
        self.small_shapes = [p.shape for p in payload]
        packed = _pack(payload)
        land = jnp.zeros((8 * PACK_ROWS, packed.shape[1]), F32)
        self.small_state, token = _split_start("ag_grads_start", [(packed, land, _everyone_plan(PACK_ROWS), 7)], after)
        return token

    def wait_small(self, after):
        srcs, lands = _split_wait("ag_grads_wait", self.small_state, after)
        return srcs[0], lands[0]

    def scatter(self, group, grads, after=None):
        items = []
        for name, g in grads.items():
            kind = BIG[name][0]
            rows, cols = g.shape
            n = (cols if kind == "col" else rows) // N_CHIPS
            reg = (rows, n) if kind == "col" else (n, cols)
            items.append((g, lax.empty((N_CHIPS - 1, *reg), BF16), _scatter_plan(kind, n), N_CHIPS - 1))
        state, token = _split_start(f"gs_start_{group}", items, after)
        self.stage[group] = (list(grads), state)
        return token[0, 0]

    def collect(self, group, after):
        names, state = self.stage[group]
        srcs, lands = _split_wait(f"gs_wait_{group}", state, after)
        items = []
        for name, st, g, land in zip(names, state, srcs, lands):
            _, kind, n = st[0][0]
            part = _sum4(f"sum_{name}", g, land, self.q, kind, n)
            items.append((part, lax.empty(part.shape, F32), _sibling_plan(), 1))
        state, token = _split_start(f"sw_start_{group}", items)
        self.stage[group] = (names, state)
        return token

    def finish(self, group, after):
        names, state = self.stage.pop(group)
        srcs, lands = _split_wait(f"sw_wait_{group}", state, after)
        for name, mine, theirs in zip(names, srcs, lands):
            _, pname, layer = BIG[name]
            self.results[pname] = _adamw_sum(f"adamw_{name}", self.w[pname], self.m[pname], self.v[pname], layer,
                                             mine, theirs, self.results.get(pname))


PACK_ROWS = 8


def _pack(arrays):
    flat = jnp.concatenate([a.reshape(-1) for a in arrays])
    n = flat.shape[0]
    padded = -(-n // (PACK_ROWS * LANES)) * (PACK_ROWS * LANES)
    return jnp.pad(flat, (0, padded - n)).reshape(PACK_ROWS, padded // PACK_ROWS)


def _unpack(packed, shapes):
    flat = packed.reshape(-1)
    out, off = [], 0
    for s in shapes:
        n = 1
        for d in s:
            n *= d
        out.append(flat[off:off + n].reshape(s))
        off += n
    return out


REPLICATED = ("ada_b", "norm_mix_g", "norm_ffn_g", "a_vnorm_g", "a_spatial_w", "a_spatial_b", "b_q_norm_g",
              "b_k_norm_g", "ffn_dw_b")
SMALL_SHARDED = ("conv_pw1_b", "conv_dw_w", "conv_dw_b", "conv_ln_g", "conv_ln_b", "conv_pw2_b", "ffn_dw_w")
WEIGHTS = ("ada_w", "ada_b", "norm_mix_g", "norm_ffn_g", "ab_w_in", "a_vnorm_g", "a_spatial_w", "a_spatial_b",
           "b_q_norm_g", "b_k_norm_g", "ab_w_out", "conv_pw1_w", "conv_pw1_b", "conv_dw_w", "conv_dw_b", "conv_ln_g",
           "conv_ln_b", "conv_pw2_w", "conv_pw2_b", "ffn_up_w", "ffn_dw_w", "ffn_dw_b", "ffn_down_w")

def kernel(x, c, positions, ada_w, ada_b, norm_mix_g, norm_ffn_g, ab_w_in, a_vnorm_g, a_spatial_w, a_spatial_b, b_q_norm_g, b_k_norm_g, ab_w_out, conv_pw1_w, conv_pw1_b, conv_dw_w, conv_dw_b, conv_ln_g, conv_ln_b, conv_pw2_w, conv_pw2_b, ffn_up_w, ffn_dw_w, ffn_dw_b, ffn_down_w, loss_target, m_ada_w, m_ada_b, m_norm_mix_g, m_norm_ffn_g, m_ab_w_in, m_a_vnorm_g, m_a_spatial_w, m_a_spatial_b, m_b_q_norm_g, m_b_k_norm_g, m_ab_w_out, m_conv_pw1_w, m_conv_pw1_b, m_conv_dw_w, m_conv_dw_b, m_conv_ln_g, m_conv_ln_b, m_conv_pw2_w, m_conv_pw2_b, m_ffn_up_w, m_ffn_dw_w, m_ffn_dw_b, m_ffn_down_w, v_ada_w, v_ada_b, v_norm_mix_g, v_norm_ffn_g, v_ab_w_in, v_a_vnorm_g, v_a_spatial_w, v_a_spatial_b, v_b_q_norm_g, v_b_k_norm_g, v_ab_w_out, v_conv_pw1_w, v_conv_pw1_b, v_conv_dw_w, v_conv_dw_b, v_conv_ln_g, v_conv_ln_b, v_conv_pw2_w, v_conv_pw2_b, v_ffn_up_w, v_ffn_dw_w, v_ffn_dw_b, v_ffn_down_w):
    w = dict(ada_w=ada_w, ada_b=ada_b, norm_mix_g=norm_mix_g, norm_ffn_g=norm_ffn_g, ab_w_in=ab_w_in, a_vnorm_g=a_vnorm_g, a_spatial_w=a_spatial_w, a_spatial_b=a_spatial_b, b_q_norm_g=b_q_norm_g, b_k_norm_g=b_k_norm_g, ab_w_out=ab_w_out, conv_pw1_w=conv_pw1_w, conv_pw1_b=conv_pw1_b, conv_dw_w=conv_dw_w, conv_dw_b=conv_dw_b, conv_ln_g=conv_ln_g, conv_ln_b=conv_ln_b, conv_pw2_w=conv_pw2_w, conv_pw2_b=conv_pw2_b, ffn_up_w=ffn_up_w, ffn_dw_w=ffn_dw_w, ffn_dw_b=ffn_dw_b, ffn_down_w=ffn_down_w)
    m = dict(ada_w=m_ada_w, ada_b=m_ada_b, norm_mix_g=m_norm_mix_g, norm_ffn_g=m_norm_ffn_g, ab_w_in=m_ab_w_in, a_vnorm_g=m_a_vnorm_g, a_spatial_w=m_a_spatial_w, a_spatial_b=m_a_spatial_b, b_q_norm_g=m_b_q_norm_g, b_k_norm_g=m_b_k_norm_g, ab_w_out=m_ab_w_out, conv_pw1_w=m_conv_pw1_w, conv_pw1_b=m_conv_pw1_b, conv_dw_w=m_conv_dw_w, conv_dw_b=m_conv_dw_b, conv_ln_g=m_conv_ln_g, conv_ln_b=m_conv_ln_b, conv_pw2_w=m_conv_pw2_w, conv_pw2_b=m_conv_pw2_b, ffn_up_w=m_ffn_up_w, ffn_dw_w=m_ffn_dw_w, ffn_dw_b=m_ffn_dw_b, ffn_down_w=m_ffn_down_w)
    v = dict(ada_w=v_ada_w, ada_b=v_ada_b, norm_mix_g=v_norm_mix_g, norm_ffn_g=v_norm_ffn_g, ab_w_in=v_ab_w_in, a_vnorm_g=v_a_vnorm_g, a_spatial_w=v_a_spatial_w, a_spatial_b=v_a_spatial_b, b_q_norm_g=v_b_q_norm_g, b_k_norm_g=v_b_k_norm_g, ab_w_out=v_ab_w_out, conv_pw1_w=v_conv_pw1_w, conv_pw1_b=v_conv_pw1_b, conv_dw_w=v_conv_dw_w, conv_dw_b=v_conv_dw_b, conv_ln_g=v_conv_ln_g, conv_ln_b=v_conv_ln_b, conv_pw2_w=v_conv_pw2_w, conv_pw2_b=v_conv_pw2_b, ffn_up_w=v_ffn_up_w, ffn_dw_w=v_ffn_dw_w, ffn_dw_b=v_ffn_dw_b, ffn_down_w=v_ffn_down_w)
    S, D = x.shape[1], x.shape[2]
    xi, yi, ci = lax.axis_index("x"), lax.axis_index("y"), lax.axis_index("c")
    q = 2 * xi + yi
    b = 2 * q + ci
    take_dev = lambda g: g.reshape(8, PACK_ROWS, -1)

    c_all = _allgather8("ag_c", c.reshape(PACK_ROWS, D // PACK_ROWS)).reshape(8, D)
    n_ada = ada_w.shape[2]
    mod_sh = _ada_fwd("ada_fwd", c_all, ada_w, lax.dynamic_slice_in_dim(ada_b, q * n_ada, n_ada, axis=1))
    sh_shapes = [mod_sh.shape] + [w[n].shape for n in SMALL_SHARDED]
    gathered_mod = _allgather8("ag_mod", _pack([mod_sh] + [w[n] for n in SMALL_SHARDED]))
    per_chip = [_unpack(blk, sh_shapes) for blk in take_dev(gathered_mod)[0::2]]
    mod_g = jnp.stack([pc[0] for pc in per_chip])
    mod_mine = lax.dynamic_index_in_dim(mod_g, b, axis=2, keepdims=False)
    mod = mod_mine.transpose(1, 0, 2).reshape(2, 6, D)
    sp = {n: jnp.concatenate([pc[1 + i] for pc in per_chip], axis=-1) for i, n in enumerate(SMALL_SHARDED)}
    sp.update({n: w[n] for n in REPLICATED if n != "ada_b"})

    wb = _Weights(w, q, gathered_mod)
    mod = mod + wb.token[0, 0]

    pipe = _GradPipe(q, w, m, v)
    lossv, grad_x, late, g_in = _local_step(x[0], loss_target[0], positions[0], mod, wb, sp, pipe)

    pipe.finish("g1", grad_x)
    pipe.finish("g2", pipe.results["ffn_up_w"][0])
    swapped = pipe.collect("g3", pipe.results["conv_pw1_w"][0])
    swapped = pipe.collect("g4", swapped)

    own, gathered = pipe.wait_small(swapped)
    totals = _unpack(_sum8("sum_grads", gathered, own), pipe.small_shapes)
    grads = dict(zip(["ada_b", "loss_columns"] + pipe.small_names, totals))
    loss = 0.5 * jnp.sum(grads.pop("loss_columns")) / D
    late_g = _allgather8("ag_late", _pack(list(late)), gathered)
    pipe.scatter("g5", dict(w_in=g_in), late_g)
    late_tot = _unpack(_sum8("sum_late", late_g), [(3, D)])[0]
    grads["ada_b"] = grads["ada_b"].at[0, :2 * D].add(late_tot[:2].reshape(-1))
    grads["norm_mix_g"] = grads["norm_mix_g"].at[0].add(late_tot[2])
    for n in SMALL_SHARDED:
        n_sh = w[n].shape[-1]
        grads[n] = lax.dynamic_slice_in_dim(grads[n], q * n_sh, n_sh, axis=grads[n].ndim - 1)
    dmod_of = lambda packed: packed.reshape(packed.shape[0] // PACK_ROWS, -1)[:, :2 * 6 * D].reshape(-1, 2, 6 * D)
    dmod_all = jnp.where((jnp.arange(8) == b)[:, None, None], dmod_of(own), dmod_of(gathered))
    late_all = take_dev(late_g).reshape(8, -1)[:, :3 * D].reshape(8, 3, D)
    dmod_all = dmod_all.at[:, 0, :2 * D].add(late_all[:, :2].reshape(8, 2 * D))
    dmod_sh = lax.dynamic_slice_in_dim(dmod_all, q * n_ada, n_ada, axis=2).transpose(1, 0, 2)

    pipe.finish("g3", dmod_sh)
    pipe.finish("g4", pipe.results["ffn_up_w"][0])
    grads["ada_w"], delta_ada, m_ada, v_ada = _ada_update("ada_update", c_all, dmod_sh, ada_w, m_ada_w, v_ada_w)
    delta, new_m, new_v = dict(ada_w=delta_ada), dict(ada_w=m_ada), dict(ada_w=v_ada)
    rest = list(REPLICATED) + list(SMALL_SHARDED)
    rest_shapes = [w[n].shape for n in rest]
    outs = _adamw("adamw_small", *[_pack([src[n].reshape(w[n].shape) for n in rest]) for src in (w, grads, m, v)])
    for tgt, packed in zip((delta, new_m, new_v), outs):
        tgt.update(dict(zip(rest, _unpack(packed, rest_shapes))))
    for n in rest:
        grads[n] = grads[n].reshape(w[n].shape)
    pipe.finish("g5", pipe.collect("g5", outs[0]))
    for n, res in pipe.results.items():
        grads[n], delta[n], new_m[n], new_v[n] = res

    return (loss, grad_x[None], *[grads[n] for n in WEIGHTS], *[delta[n] for n in WEIGHTS],
            *[new_m[n] for n in WEIGHTS], *[new_v[n] for n in WEIGHTS])
```

```python
import functools

import jax
import jax.numpy as jnp
from jax import lax
from jax.experimental import pallas as pl
from jax.experimental.pallas import tpu as pltpu

F32, BF16 = jnp.float32, jnp.bfloat16
EPS = 1e-6
NEG = -1e30
ROPE_THETA = 10000.0
LANES = 128
VMEM_LIMIT = 56 * 1024 * 1024
ADAM_LR, ADAM_B1, ADAM_B2, ADAM_EPS, ADAM_WD, ADAM_STEP = 0.001, 0.9, 0.999, 1e-08, 0.01, 10
MESH = pl.DeviceIdType.MESH


def _params(sem):
    return pltpu.CompilerParams(dimension_semantics=sem, vmem_limit_bytes=VMEM_LIMIT)


_DN = {"nn": (((1,), (0,)), ((), ())), "nt": (((1,), (1,)), ((), ())), "tn": (((0,), (0,)), ((), ()))}


def _matmul(name, a, b, mode, out_dtype, tm=512, tn=512, tk=1024, bias=None, after=None):
    if mode == "nn":
        (M, K), N = a.shape, b.shape[1]
    elif mode == "nt":
        (M, K), N = a.shape, b.shape[0]
    else:
        (K, M), N = a.shape, b.shape[1]
    tm, tn, tk = min(tm, M), min(tn, N), min(tk, K)
    assert M % tm == 0 and N % tn == 0 and K % tk == 0, (name, M, N, K, tm, tn, tk)
    nk = K // tk
    if mode == "tn":
        a_spec = pl.BlockSpec((tk, tm), lambda i, j, k: (k, i))
    else:
        a_spec = pl.BlockSpec((tm, tk), lambda i, j, k: (i, k))
    if mode == "nt":
        b_spec = pl.BlockSpec((tn, tk), lambda i, j, k: (j, k))
    else:
        b_spec = pl.BlockSpec((tk, tn), lambda i, j, k: (k, j))
    in_specs, args = [a_spec, b_spec], [a, b]
    if bias is not None:
        in_specs.append(pl.BlockSpec((1, tn), lambda i, j, k: (0, j)))
        args.append(bias)
    if after is not None:
        in_specs.append(pl.BlockSpec(memory_space=pl.ANY))
        args.append(after)
    n_in = len(args)

    def body(*refs):
        a_ref, b_ref, o_ref = refs[0], refs[1], refs[n_in]
        p = lax.dot_general(a_ref[...], b_ref[...], _DN[mode], preferred_element_type=F32)

        def finish(acc):
            if bias is not None:
                acc = acc + refs[2][...]
            o_ref[...] = acc.astype(o_ref.dtype)

        if nk == 1:
            finish(p)
        else:
            acc_ref = refs[n_in + 1]
            k = pl.program_id(2)

            @pl.when(k == 0)
            def _():
                acc_ref[...] = p

            @pl.when(k > 0)
            def _():
                acc_ref[...] += p

            @pl.when(k == nk - 1)
            def _():
                finish(acc_ref[...])

    return pl.pallas_call(
        body, name=name, grid=(M // tm, N // tn, nk), in_specs=in_specs,
        out_specs=pl.BlockSpec((tm, tn), lambda i, j, k: (i, j)),
        out_shape=jax.ShapeDtypeStruct((M, N), out_dtype),
        scratch_shapes=[pltpu.VMEM((tm, tn), F32)] if nk > 1 else [],
        compiler_params=_params(("parallel", "parallel", "arbitrary")),
    )(*args)


def _rowcall(name, body, nrows, tm, ins, outs, scratch=()):
    nblk = nrows // tm
    assert nrows % tm == 0

    def spec(kind, shape):
        k = kind[0]
        if k == "row":
            cw, cb = kind[1] or shape[-1], kind[2]
            return pl.BlockSpec((tm, cw), lambda i: (i, cb))
        if k == "prev":
            hb, cw, cb = kind[1], kind[2] or shape[-1], kind[3]
            r = tm // hb
            return pl.BlockSpec((hb, cw), lambda i: (jnp.maximum(i * r - 1, 0), cb))
        if k == "next":
            hb, cw, cb = kind[1], kind[2] or shape[-1], kind[3]
            r, last = tm // hb, nrows // hb - 1
            return pl.BlockSpec((hb, cw), lambda i: (jnp.minimum((i + 1) * r, last), cb))
        if k == "off":
            off, cw, cb = kind[1], kind[2] or shape[-1], kind[3]
            return pl.BlockSpec((tm, cw), lambda i: (jnp.clip(i + off, 0, nblk - 1), cb))
        if k == "cls":
            dil, off = kind[1], kind[2]
            return pl.BlockSpec((dil, tm // dil, shape[-1]), lambda i: (0, jnp.clip(i + off, 0, nblk - 1), 0))
        nd = len(shape)
        return pl.BlockSpec(tuple(shape), lambda i: (0,) * nd)

    has_acc = any(o[2][0] == "acc" for o in outs)
    return pl.pallas_call(
        body, name=name, grid=(nblk,),
        in_specs=[spec(kind, a.shape) for a, kind in ins],
        out_specs=[spec(kind, shape) for shape, _, kind in outs],
        out_shape=[jax.ShapeDtypeStruct(tuple(shape), dt) for shape, dt, _ in outs],
        scratch_shapes=list(scratch),
        compiler_params=_params(("arbitrary",) if has_acc else ("parallel",)),
    )(*[a for a, _ in ins])


ROW = ("row", None, 0)
FULL = ("full",)
ACC = ("acc",)


def _colsum(x):
    return jnp.sum(x, axis=0, keepdims=True)


def _acc_add(i, ref, val, rows=None):
    idx = (slice(None),) * len(ref.shape) if rows is None else rows

    @pl.when(i == 0)
    def _():
        ref[idx] = val

    @pl.when(i > 0)
    def _():
        ref[idx] = ref[idx] + val


def _sigmoid(x):
    return 1.0 / (1.0 + jnp.exp(-x))


def _gelu(x):
    return 0.5 * x * (1.0 + lax.erf(x * (2.0 ** -0.5)))


def _gelu_grad(x):
    return 0.5 * (1.0 + lax.erf(x * (2.0 ** -0.5))) + x * jnp.exp(-0.5 * x * x) * ((2.0 * jnp.pi) ** -0.5)


SUBLANES = 8


def _phases(ext, sign):
    n = ext.shape[0]
    return [ext if b == 0 else pltpu.roll(ext, b if sign > 0 else n - b, axis=0) for b in range(SUBLANES)]


def _shift_prev(phases, s, hb):
    a, b = divmod(s, SUBLANES)
    return phases[b][hb - SUBLANES * a:phases[b].shape[0] - SUBLANES * a]


def _shift_next(phases, s, tm):
    a, b = divmod(s, SUBLANES)
    return phases[b][SUBLANES * a:SUBLANES * a + tm]


def _rms_mod_val(x, g, sc, sh):
    r = lax.rsqrt(jnp.mean(x * x, axis=-1, keepdims=True) + EPS)
    return x * r * g * (1.0 + sc) + sh


def _mod_first(name, x, g, sc, sh, tm=256):
    S, D = x.shape

    def body(x_ref, g_ref, sc_ref, sh_ref, h_ref):
        h_ref[...] = _rms_mod_val(x_ref[...], g_ref[...], sc_ref[...], sh_ref[...]).astype(BF16)

    return _rowcall(name, body, S, tm, [(x, ROW), (g, FULL), (sc, FULL), (sh, FULL)], [((S, D), BF16, ROW)])[0]


def _resid_mod(name, x, y, gate, g, sc, sh, tm=256):
    S, D = x.shape

    def body(x_ref, y_ref, gate_ref, g_ref, sc_ref, sh_ref, xo_ref, h_ref):
        xn = x_ref[...] + gate_ref[...] * y_ref[...]
        xo_ref[...] = xn
        h_ref[...] = _rms_mod_val(xn, g_ref[...], sc_ref[...], sh_ref[...]).astype(BF16)

    return _rowcall(name, body, S, tm,
                    [(x, ROW), (y, ROW), (gate, FULL), (g, FULL), (sc, FULL), (sh, FULL)],
                    [((S, D), F32, ROW), ((S, D), BF16, ROW)])


def _gate_bwd_val(i, d, y_ref, gate_ref, dy_ref, dg_ref, db_ref):
    dy = d * gate_ref[...]
    dy_ref[...] = dy.astype(BF16)
    _acc_add(i, dg_ref, _colsum(d * y_ref[...]))
    _acc_add(i, db_ref, _colsum(dy))


GATE_OUTS = lambda S, D: [((S, D), BF16, ROW), ((1, D), F32, ACC), ((1, D), F32, ACC)]


def _loss_head(name, x, y, gate, tgt, tm=256):
    S, D = x.shape

    def body(x_ref, y_ref, gate_ref, t_ref, dx_ref, l_ref, dy_ref, dg_ref, db_ref):
        i = pl.program_id(0)
        err = x_ref[...] + gate_ref[...] * y_ref[...] - t_ref[...]
        d = err * (1.0 / D)
        dx_ref[...] = d
        _acc_add(i, l_ref, _colsum(err * err))
        _gate_bwd_val(i, d, y_ref, gate_ref, dy_ref, dg_ref, db_ref)

    return _rowcall(name, body, S, tm, [(x, ROW), (y, ROW), (gate, FULL), (tgt, ROW)],
                    [((S, D), F32, ROW), ((1, D), F32, ACC)] + GATE_OUTS(S, D))


def _mod_bwd(name, dxo, dh, x, g, sc, y=None, gate=None, tm=256):
    S, D = x.shape
    gated = y is not None

    def body(d_ref, dh_ref, x_ref, g_ref, sc_ref, *rest):
        dx_ref, dsh_ref, dsc_ref, dg_ref = rest[2 * gated:2 * gated + 4]
        i = pl.program_id(0)
        xv, dh_v, gv = x_ref[...], dh_ref[...], g_ref[...]
        r = lax.rsqrt(jnp.mean(xv * xv, axis=-1, keepdims=True) + EPS)
        n = xv * r
        _acc_add(i, dsh_ref, _colsum(dh_v))
        _acc_add(i, dsc_ref, _colsum(dh_v * (n * gv)))
        dy = dh_v * (1.0 + sc_ref[...])
        _acc_add(i, dg_ref, _colsum(dy * n))
        dn = dy * gv
        dx = d_ref[...] + r * (dn - n * jnp.mean(dn * n, axis=-1, keepdims=True))
        dx_ref[...] = dx
        if gated:
            _gate_bwd_val(i, dx, rest[0], rest[1], *rest[6:9])

    ins = [(dxo, ROW), (dh, ROW), (x, ROW), (g, FULL), (sc, FULL)] + ([(y, ROW), (gate, FULL)] if gated else [])
    outs = [((S, D), F32, ROW), ((1, D), F32, ACC), ((1, D), F32, ACC), ((1, D), F32, ACC)]
    return _rowcall(name, body, S, tm, ins, outs + (GATE_OUTS(S, D) if gated else []))


HB16 = 16


def _conv3_val(ph, w, b, hb):
    return w[2:3] * _shift_prev(ph, 0, hb) + w[1:2] * _shift_prev(ph, 1, hb) + w[0:1] * _shift_prev(ph, 2, hb) + b


def _halo_first(halo_ref, tile_ref, live):
    return _phases(jnp.concatenate([halo_ref[...].astype(F32) * live, tile_ref[...].astype(F32)], axis=0), 1)


def _glu3_fwd(name, u, w, b, tm=128):
    S, F2 = u.shape
    Fh = F2 // 2

    def body(ua_ref, ub_ref, ha_ref, hb_ref, w_ref, b_ref, o_ref, z_ref):
        live = (pl.program_id(0) > 0).astype(F32)
        wv, bv = w_ref[...], b_ref[...]
        za = _conv3_val(_halo_first(ha_ref, ua_ref, live), wv[:, :Fh], bv[:, :Fh], HB16)
        zb = _conv3_val(_halo_first(hb_ref, ub_ref, live), wv[:, Fh:], bv[:, Fh:], HB16)
        o_ref[...] = (za * _sigmoid(za) * zb).astype(BF16)
        z_ref[:, :Fh] = za.astype(BF16)
        z_ref[:, Fh:] = zb.astype(BF16)

    return _rowcall(name, body, S, tm,
                    [(u, ("row", Fh, 0)), (u, ("row", Fh, 1)), (u, ("prev", HB16, Fh, 0)), (u, ("prev", HB16, Fh, 1)),
                     (w, FULL), (b, FULL)],
                    [((S, Fh), BF16, ROW), ((S, F2), BF16, ROW)])


def _glu3_bwd(name, z, dhm, tm=128):
    S, F2 = z.shape
    Fh = F2 // 2

    def body(za_ref, zb_ref, d_ref, dz_ref, db_ref):
        i = pl.program_id(0)
        za, zb, d = za_ref[...].astype(F32), zb_ref[...].astype(F32), d_ref[...]
        sg = _sigmoid(za)
        da = d * zb * (sg * (1.0 + za * (1.0 - sg)))
        db = d * (za * sg)
        dz_ref[:, :Fh] = da.astype(BF16)
        dz_ref[:, Fh:] = db.astype(BF16)
        _acc_add(i, db_ref, jnp.concatenate([_colsum(da), _colsum(db)], axis=1))

    return _rowcall(name, body, S, tm, [(z, ("row", Fh, 0)), (z, ("row", Fh, 1)), (dhm, ROW)],
                    [((S, F2), BF16, ROW), ((1, F2), F32, ACC)])


def _conv3_bwd(name, dz, u, w, tm=128):
    S, F2 = dz.shape
    nblk = S // tm
    K = w.shape[0]

    def body(d_ref, n_ref, u_ref, w_ref, o_ref, dw_ref):
        i = pl.program_id(0)
        live = (i < nblk - 1).astype(F32)
        ph = _phases(jnp.concatenate([d_ref[...].astype(F32), n_ref[...].astype(F32) * live], axis=0), -1)
        wv, uv = w_ref[...], u_ref[...].astype(F32)
        shifted = [_shift_next(ph, K - 1 - k, tm) for k in range(K)]
        o_ref[...] = functools.reduce(lambda a, t: a + t, [wv[k:k + 1] * shifted[k] for k in range(K)]).astype(BF16)
        for k in range(K):
            _acc_add(i, dw_ref, _colsum(uv * shifted[k]), rows=(slice(k, k + 1), slice(None)))

        @pl.when(i == 0)
        def _():
            dw_ref[K:, :] = jnp.zeros((dw_ref.shape[0] - K, F2), F32)

    return _rowcall(name, body, S, tm, [(dz, ROW), (dz, ("next", HB16, None, 0)), (u, ROW), (w, FULL)],
                    [((S, F2), BF16, ROW), ((SUBLANES, F2), F32, ACC)])


def _ffn_fwd(name, h, w_up, w_dn, dw_w, dw_b):
    u = _matmul(f"{name}_up", h, w_up, "nn", BF16, tm=2048)
    hm, z = _glu3_fwd(f"{name}_glu", u, dw_w, dw_b)
    f = _matmul(f"{name}_dn", hm, w_dn, "nn", F32, tm=1024, tn=1024, tk=w_dn.shape[0])
    return f, (u, hm, z)


def _ffn_bwd(name, dy, h, u, hm, z, w_up, w_dn, dw_w, dw_b):
    Fh = w_dn.shape[0]
    dhm = _matmul(f"{name}_ddn_x", dy, w_dn, "nt", F32, tm=1024, tn=Fh // 2)
    g_dn = _matmul(f"{name}_ddn_w", hm, dy, "tn", BF16, tm=Fh // 2, tn=1024, tk=2048)
    dz, g_dw_b = _glu3_bwd(f"{name}_dglu", z, dhm)
    du, taps = _conv3_bwd(f"{name}_dconv", dz, u, dw_w)
    g_up = _matmul(f"{name}_dup_w", h, du, "tn", BF16, tm=1024, tn=Fh // 2, tk=2048)
    dh = _matmul(f"{name}_dup_x", du, w_up, "nt", F32, tm=1024, tn=1024, tk=Fh)
    return dh, dict(up=g_up, dn=g_dn, dw_w=taps[0:dw_w.shape[0]], dw_b=g_dw_b)


HB32 = 32


def _glu31_fwd(name, p, w, b, tm=256):
    S, D2 = p.shape
    D = D2 // 2
    K = w.shape[0]

    def body(a_ref, g_ref, ha_ref, hg_ref, w_ref, b_ref, o_ref):
        live = (pl.program_id(0) > 0).astype(F32)
        y1 = a_ref[...] * _sigmoid(g_ref[...])
        ph = _phases(jnp.concatenate([ha_ref[...] * _sigmoid(hg_ref[...]) * live, y1], axis=0), 1)
        wv = w_ref[...]
        acc = b_ref[...] + wv[K - 1:K] * y1
        for k in range(K - 1):
            acc = acc + wv[k:k + 1] * _shift_prev(ph, K - 1 - k, HB32)
        o_ref[...] = acc

    return _rowcall(name, body, S, tm,
                    [(p, ("row", D, 0)), (p, ("row", D, 1)), (p, ("prev", HB32, D, 0)), (p, ("prev", HB32, D, 1)),
                     (w, FULL), (b, FULL)],
                    [((S, D), F32, ROW)])[0]


def _ln_silu_fwd(name, y2, g, b, tm=256):
    S, D = y2.shape

    def body(y_ref, g_ref, b_ref, o_ref):
        y = y_ref[...]
        mu = jnp.mean(y, axis=-1, keepdims=True)
        yc = y - mu
        rs = lax.rsqrt(jnp.mean(yc * yc, axis=-1, keepdims=True) + EPS)
        y3 = yc * rs * g_ref[...] + b_ref[...]
        o_ref[...] = (y3 * _sigmoid(y3)).astype(BF16)

    return _rowcall(name, body, S, tm, [(y2, ROW), (g, FULL), (b, FULL)], [((S, D), BF16, ROW)])[0]


def _ln_silu_bwd(name, y2, dy4, g, b, tm=256):
    S, D = y2.shape

    def body(y_ref, d_ref, g_ref, b_ref, o_ref, dg_ref, db_ref):
        i = pl.program_id(0)
        y, gv = y_ref[...], g_ref[...]
        mu = jnp.mean(y, axis=-1, keepdims=True)
        yc = y - mu
        rs = lax.rsqrt(jnp.mean(yc * yc, axis=-1, keepdims=True) + EPS)
        n = yc * rs
        y3 = n * gv + b_ref[...]
        sg = _sigmoid(y3)
        dy3 = d_ref[...] * (sg * (1.0 + y3 * (1.0 - sg)))
        _acc_add(i, db_ref, _colsum(dy3))
        _acc_add(i, dg_ref, _colsum(dy3 * n))
        dn = dy3 * gv
        o_ref[...] = rs * (dn - jnp.mean(dn, axis=-1, keepdims=True) - n * jnp.mean(dn * n, axis=-1, keepdims=True))

    return _rowcall(name, body, S, tm, [(y2, ROW), (dy4, ROW), (g, FULL), (b, FULL)],
                    [((S, D), F32, ROW), ((1, D), F32, ACC), ((1, D), F32, ACC)])


def _glu31_bwd(name, p, dy2, w, tm=256):
    S, D2 = p.shape
    D = D2 // 2
    K = w.shape[0]
    nblk = S // tm

    conv_rows, tap_rows, tap_group, tap_unroll = 16, SUBLANES, 4, 4

    def body(a_ref, g_ref, d_ref, dn_ref, w_ref, dp_ref, dw_ref, dcb_ref, dpb_ref, ph_d, y1_ref, dy1_ref, wb_ref):
        i = pl.program_id(0)
        live_next = (i < nblk - 1).astype(F32)
        a, sg, d = a_ref[...], _sigmoid(g_ref[...]), d_ref[...]
        y1_ref[...] = a * sg
        for b, ph in enumerate(_phases(jnp.concatenate([d, dn_ref[...] * live_next], axis=0), -1)):
            ph_d[b] = ph
        taps = [divmod(K - 1 - k, SUBLANES) for k in range(K)]

        @pl.when(i == 0)
        def _():
            for k in range(K):
                wb_ref[k] = jnp.broadcast_to(w_ref[k:k + 1, :], (SUBLANES, D))

        def conv_rows_at(rb, carry):
            r0 = pl.multiple_of(rb * conv_rows, conv_rows)
            accs = [jnp.zeros((SUBLANES, D), F32) for _ in range(conv_rows // SUBLANES)]
            for k, (rows8, phase) in enumerate(taps):
                wk = wb_ref[k]
                for u in range(len(accs)):
                    accs[u] = accs[u] + wk * ph_d[phase, pl.ds(r0 + SUBLANES * (rows8 + u), SUBLANES), :]
            for u, acc in enumerate(accs):
                dy1_ref[pl.ds(r0 + SUBLANES * u, SUBLANES), :] = acc
            return carry

        lax.fori_loop(0, tm // conv_rows, conv_rows_at, 0)
        for k0 in range(0, K, tap_group):
            group = taps[k0:k0 + tap_group]

            def tap_rows_at(rb, accs, group=group):
                for u in range(tap_unroll):
                    r0 = pl.multiple_of((rb * tap_unroll + u) * tap_rows, tap_rows)
                    yv = y1_ref[pl.ds(r0, tap_rows), :]
                    accs = tuple(acc + yv * ph_d[phase, pl.ds(r0 + SUBLANES * rows8, tap_rows), :]
                                 for acc, (rows8, phase) in zip(accs, group))
                return accs

            accs = lax.fori_loop(0, tm // (tap_rows * tap_unroll), tap_rows_at,
                                 tuple(jnp.zeros((tap_rows, D), F32) for _ in group))
            for j, acc in enumerate(accs):
                _acc_add(i, dw_ref, _colsum(acc), rows=(slice(k0 + j, k0 + j + 1), slice(None)))

        @pl.when(i == 0)
        def _():
            dw_ref[K:, :] = jnp.zeros((dw_ref.shape[0] - K, D), F32)

        _acc_add(i, dcb_ref, _colsum(d))
        dy1 = dy1_ref[...]
        da = dy1 * sg
        dg = dy1 * a * sg * (1.0 - sg)
        dp_ref[:, :D] = da.astype(BF16)
        dp_ref[:, D:] = dg.astype(BF16)
        _acc_add(i, dpb_ref, jnp.concatenate([_colsum(da), _colsum(dg)], axis=1))

    return _rowcall(name, body, S, tm,
                    [(p, ("row", D, 0)), (p, ("row", D, 1)), (dy2, ROW), (dy2, ("next", HB32, None, 0)), (w, FULL)],
                    [((S, D2), BF16, ROW), ((HB32, D), F32, ACC), ((1, D), F32, ACC), ((1, D2), F32, ACC)],
                    scratch=[pltpu.VMEM((SUBLANES, tm + HB32, D), F32), pltpu.VMEM((tm, D), F32),
                             pltpu.VMEM((tm, D), F32), pltpu.VMEM((K, SUBLANES, D), F32)])


CHUNK = 128
A_GROUPS = 4


def _group_ln(gv):
    ns, rss = [], []
    for g in range(A_GROUPS):
        xg = gv[:, g * LANES:(g + 1) * LANES]
        xc = xg - jnp.mean(xg, axis=-1, keepdims=True)
        rs = lax.rsqrt(jnp.mean(xc * xc, axis=-1, keepdims=True) + EPS)
        ns.append(xc * rs)
        rss.append(jnp.broadcast_to(rs, xg.shape))
    return jnp.concatenate(ns, axis=1), jnp.concatenate(rss, axis=1)


def _tril_mask():
    r = lax.broadcasted_iota(jnp.int32, (CHUNK, CHUNK), 0)
    c = lax.broadcasted_iota(jnp.int32, (CHUNK, CHUNK), 1)
    return r >= c


def _spatial(ws_ref, x, dn):
    mask = _tril_mask()
    rows = []
    for ci in range(x.shape[0] // CHUNK):
        cols = []
        for g in range(A_GROUPS):
            wm = jnp.where(mask, ws_ref[g], 0.0).astype(BF16)
            xb = x[ci * CHUNK:(ci + 1) * CHUNK, g * LANES:(g + 1) * LANES]
            cols.append(lax.dot_general(wm, xb, dn, preferred_element_type=F32))
        rows.append(jnp.concatenate(cols, axis=1))
    return jnp.concatenate(rows, axis=0)


def _mixa_fwd(name, z, vg, ws, bias_full, tm=256):
    S = z.shape[0]
    W = A_GROUPS * LANES

    def body(u_ref, v_ref, vg_ref, ws_ref, b_ref, o_ref):
        nh, _ = _group_ln(_gelu(v_ref[...]))
        vn = (nh * vg_ref[...]).astype(BF16)
        f = _spatial(ws_ref, vn, _DN["nn"]) + jnp.concatenate([b_ref[...]] * (tm // CHUNK), axis=0)
        o_ref[...] = (_gelu(u_ref[...]) * f).astype(BF16)

    return _rowcall(name, body, S, tm,
                    [(z, ("row", W, 0)), (z, ("row", W, 1)), (vg, FULL), (ws, FULL), (bias_full, FULL)],
                    [((S, W), BF16, ROW)])[0]


def _mixa_bwd(name, z, dyab, vg, ws, bias_full, tm=256):
    S = z.shape[0]
    W = A_GROUPS * LANES
    nch = tm // CHUNK

    def body(u_ref, v_ref, d_ref, vg_ref, ws_ref, b_ref, dz_ref, dws_ref, dbf_ref, dvg_ref):
        i = pl.program_id(0)
        u, v, d, vgv = u_ref[...], v_ref[...], d_ref[...], vg_ref[...]
        nh, rs = _group_ln(_gelu(v))
        vn = (nh * vgv).astype(BF16)
        f = _spatial(ws_ref, vn, _DN["nn"]) + jnp.concatenate([b_ref[...]] * nch, axis=0)
        dz_ref[:, :W] = (d * f * _gelu_grad(u)).astype(BF16)
        df = d * _gelu(u)
        dbf = df[0:CHUNK]
        for ci in range(1, nch):
            dbf = dbf + df[ci * CHUNK:(ci + 1) * CHUNK]
        _acc_add(i, dbf_ref, dbf)
        dfb = df.astype(BF16)
        mask = _tril_mask()
        for g in range(A_GROUPS):
            acc = jnp.zeros((CHUNK, CHUNK), F32)
            for ci in range(nch):
                blk = (slice(ci * CHUNK, (ci + 1) * CHUNK), slice(g * LANES, (g + 1) * LANES))
                acc = acc + lax.dot_general(dfb[blk], vn[blk], _DN["nt"], preferred_element_type=F32)
            _acc_add(i, dws_ref, jnp.where(mask, acc, 0.0)[None], rows=(slice(g, g + 1), slice(None), slice(None)))
        dvn = _spatial(ws_ref, dfb, _DN["tn"])
        _acc_add(i, dvg_ref, _colsum(dvn * nh))
        dnh = dvn * vgv
        parts = []
        for g in range(A_GROUPS):
            cs = slice(g * LANES, (g + 1) * LANES)
            dg_, ng = dnh[:, cs], nh[:, cs]
            parts.append(dg_ - jnp.mean(dg_, axis=-1, keepdims=True) - ng * jnp.mean(dg_ * ng, axis=-1, keepdims=True))
        dz_ref[:, W:] = (rs * jnp.concatenate(parts, axis=1) * _gelu_grad(v)).astype(BF16)

    return _rowcall(name, body, S, tm,
                    [(z, ("row", W, 0)), (z, ("row", W, 1)), (dyab, ("row", W, 0)), (vg, FULL), (ws, FULL),
                     (bias_full, FULL)],
                    [((S, 2 * W), BF16, ROW), ((A_GROUPS, CHUNK, CHUNK), F32, ACC), ((CHUNK, W), F32, ACC),
                     ((1, W), F32, ACC)])


HEAD = 64
N_HEADS = 8
BW = HEAD * N_HEADS
QB = 128
DILATIONS = (1, 4, 16)
QK_SCALE = HEAD ** -0.5


def _gsum64(x, ones_bd):
    x1 = x.astype(BF16)
    r1 = x - x1.astype(F32)
    x2 = r1.astype(BF16)
    x3 = (r1 - x2.astype(F32)).astype(BF16)
    dot = lambda t: jnp.dot(t, ones_bd, preferred_element_type=F32)
    return dot(x1) + dot(x2) + dot(x3)


def _swap32(x):
    n = x.shape[-1]
    up = pltpu.roll(x, n - HEAD // 2, axis=1)
    dn = pltpu.roll(x, HEAD // 2, axis=1)
    lane = lax.broadcasted_iota(jnp.int32, x.shape, 1)
    return jnp.where((lane % HEAD) < HEAD // 2, up, dn)


def _tile4(t):
    return jnp.concatenate([t] * (BW // LANES), axis=1)


def _stage_spec(tm):
    return pltpu.VMEM((BW // LANES, tm, LANES), F32)


def _to_classes(stage, x, dil):
    tm = x.shape[0]
    for j in range(BW // LANES):
        stage[j] = x[:, j * LANES:(j + 1) * LANES]
    return [jnp.concatenate([stage.at[j][pl.ds(r, tm // dil, stride=dil), :] for j in range(BW // LANES)], axis=1)
            for r in range(dil)]


def _from_classes(stage, cls, dil):
    rows = cls.shape[1]
    for r in range(dil):
        for j in range(BW // LANES):
            stage.at[j][pl.ds(r, rows, stride=dil), :] = cls[r, :, j * LANES:(j + 1) * LANES]
    return jnp.concatenate([stage[j] for j in range(BW // LANES)], axis=1)


def _cls_view(t, dil):
    return t if dil == 1 else t.reshape(dil, t.shape[0] // dil, t.shape[1])


def _cls_kind(dil, off=0):
    return ("off", off, None, 0) if dil == 1 else ("cls", dil, off)


def _cls_out(S, dil, dtype):
    return ((S, BW) if dil == 1 else (dil, S // dil, BW), dtype, _cls_kind(dil))


def _flat(t):
    return t.reshape(-1, t.shape[-1])


def _qkv_fwd(name, z, cos, sin, ones_bd, qg, kg, tm=256):
    S = z.shape[0]
    nd = len(DILATIONS)

    def body(q_ref, k_ref, v_ref, c_ref, s_ref, o_ref, qg_ref, kg_ref, *rest):
        outs, stage = rest[:3 * nd], rest[3 * nd]
        c, s, ob = _tile4(c_ref[...]), _tile4(s_ref[...]), o_ref[...]

        def norm_rope(x, g):
            r = lax.rsqrt(_gsum64(x * x, ob) * (1.0 / HEAD) + EPS)
            xn = x * r * g
            return xn * c + _swap32(xn) * s

        vals = [norm_rope(q_ref[...], qg_ref[...]) * QK_SCALE, norm_rope(k_ref[...], kg_ref[...]), v_ref[...]]
        for a, val in enumerate(vals):
            for b, dil in enumerate(DILATIONS):
                if dil == 1:
                    outs[nd * a + b][...] = val.astype(BF16)
                else:
                    for r, rows in enumerate(_to_classes(stage, val, dil)):
                        outs[nd * a + b][r] = rows.astype(BF16)

    outs = _rowcall(name, body, S, tm,
                    [(z, ("row", BW, 2)), (z, ("row", BW, 3)), (z, ("row", BW, 4)), (cos, ROW), (sin, ROW),
                     (ones_bd, FULL), (qg, FULL), (kg, FULL)],
                    [_cls_out(S, dil, BF16) for _ in range(3) for dil in DILATIONS], scratch=[_stage_spec(tm)])
    return [[_flat(outs[nd * a + b]) for a in range(3)] for b in range(nd)]


PAIR = 2 * HEAD


ATT_BLOCKS = 2
ATT_TM = ATT_BLOCKS * QB
ATT_PREV = ("prev", QB, None, 0)


def _key_rows(prev_ref, cur_ref, sb, ps):
    before = prev_ref[:, ps] if sb == 0 else cur_ref[(sb - 1) * QB:sb * QB, ps]
    return jnp.concatenate([before, cur_ref[sb * QB:(sb + 1) * QB, ps]], axis=0)


def _pair_scores(q_ref, kp_ref, kc_ref, sb, hp, half, seg_blocks):
    ps = slice(hp * PAIR, (hp + 1) * PAIR)
    mine = (lax.broadcasted_iota(jnp.int32, (1, PAIR), 1) >= HEAD) == (half == 1)
    qm = jnp.where(mine, q_ref[sb * QB:(sb + 1) * QB, ps], jnp.zeros((), BF16))
    kcat = _key_rows(kp_ref, kc_ref, sb, ps)
    s = lax.dot_general(qm, kcat, _DN["nt"], preferred_element_type=F32)
    qi = lax.broadcasted_iota(jnp.int32, (QB, 2 * QB), 0)
    kj = lax.broadcasted_iota(jnp.int32, (QB, 2 * QB), 1)
    has_prev = ((pl.program_id(0) * ATT_BLOCKS + sb) % seg_blocks) != 0
    valid = (kj >= qi) & (kj <= qi + QB) & ((kj >= QB) | has_prev)
    return mine, qm, kcat, s, valid


def _attn_fwd(name, q, k, v, dil):
    S = q.shape[0]
    seg_blocks = S // dil // QB

    def body(q_ref, kp_ref, kc_ref, vp_ref, vc_ref, o_ref, l_ref):
        for hp in range(N_HEADS // 2):
            ps = slice(hp * PAIR, (hp + 1) * PAIR)
            chains = [(sb, half) for sb in range(ATT_BLOCKS) for half in range(2)]
            sc = [_pair_scores(q_ref, kp_ref, kc_ref, sb, hp, half, seg_blocks) for sb, half in chains]
            ss = [jnp.where(valid, s, NEG) for _, _, _, s, valid in sc]
            ms = [jnp.max(s, axis=-1, keepdims=True) for s in ss]
            pv = [jnp.exp(s - m) for s, m in zip(ss, ms)]
            dens = [jnp.sum(p, axis=-1, keepdims=True) for p in pv]
            vcats = [_key_rows(vp_ref, vc_ref, sb, ps) for sb in range(ATT_BLOCKS)]
            outs = [jnp.dot(p.astype(BF16), vcats[sb], preferred_element_type=F32) / den
                    for p, den, (sb, _) in zip(pv, dens, chains)]
            lses = [jnp.broadcast_to(m + jnp.log(den), (QB, PAIR)) for m, den in zip(ms, dens)]
            for sb in range(ATT_BLOCKS):
                rows, upper = slice(sb * QB, (sb + 1) * QB), sc[2 * sb + 1][0]
                o_ref[rows, ps] = jnp.where(upper, outs[2 * sb + 1], outs[2 * sb])
                l_ref[rows, ps] = jnp.where(upper, lses[2 * sb + 1], lses[2 * sb])

    return _rowcall(name, body, S, ATT_TM, [(q, ROW), (k, ATT_PREV), (k, ROW), (v, ATT_PREV), (v, ROW)],
                    [((S, BW), F32, ROW)] * 2)


def _attn_bwd(name, q, k, v, do, lse, delta, dil):
    S = q.shape[0]
    seg_blocks = S // dil // QB

    def body(q_ref, kp_ref, kc_ref, vp_ref, vc_ref, do_ref, l_ref, dl_ref, dq_ref, dkc_ref, dkp_ref, dvc_ref, dvp_ref):
        for hp in range(N_HEADS // 2):
            ps = slice(hp * PAIR, (hp + 1) * PAIR)
            chains = [(sb, half) for sb in range(ATT_BLOCKS) for half in range(2)]
            rows = [slice(sb * QB, (sb + 1) * QB) for sb, _ in chains]
            cols = [hp * PAIR + half * HEAD for _, half in chains]
            sc = [_pair_scores(q_ref, kp_ref, kc_ref, sb, hp, half, seg_blocks) for sb, half in chains]
            pv = [jnp.where(valid, jnp.exp(s - l_ref[r, c:c + 1]), 0.0) for (_, _, _, s, valid), r, c in zip(sc, rows, cols)]
            vcats = [_key_rows(vp_ref, vc_ref, sb, ps) for sb in range(ATT_BLOCKS)]
            doms = [jnp.where(mine, do_ref[r, ps].astype(BF16), jnp.zeros((), BF16)) for (mine, *_), r in zip(sc, rows)]
            dps = [lax.dot_general(dom, vcats[sb], _DN["nt"], preferred_element_type=F32) for dom, (sb, _) in zip(doms, chains)]
            dss = [(p * (dp - dl_ref[r, c:c + 1])).astype(BF16) for p, dp, r, c in zip(pv, dps, rows, cols)]
            dqs = [jnp.dot(ds, kcat, preferred_element_type=F32) for ds, (_, _, kcat, _, _) in zip(dss, sc)]
            dks = [lax.dot_general(ds, qm, _DN["tn"], preferred_element_type=F32) for ds, (_, qm, *_) in zip(dss, sc)]
            dvs = [lax.dot_general(p.astype(BF16), dom, _DN["tn"], preferred_element_type=F32) for p, dom in zip(pv, doms)]
            for sb in range(ATT_BLOCKS):
                lo, hi = 2 * sb, 2 * sb + 1
                dk, dv = dks[lo] + dks[hi], dvs[lo] + dvs[hi]
                dq_ref[rows[lo], ps] = jnp.where(sc[hi][0], dqs[hi], dqs[lo])
                dkp_ref[rows[lo], ps] = dk[:QB]
                dkc_ref[rows[lo], ps] = dk[QB:]
                dvp_ref[rows[lo], ps] = dv[:QB]
                dvc_ref[rows[lo], ps] = dv[QB:]

    return _rowcall(name, body, S, ATT_TM,
                    [(q, ROW), (k, ATT_PREV), (k, ROW), (v, ATT_PREV), (v, ROW), do, (lse, ROW), (delta, ROW)],
                    [((S, BW), F32, ROW)] * 5)


def _merge_fwd(name, branches, tm=256):
    S = branches[0][0].shape[0]
    nd = len(DILATIONS)

    def body(*refs):
        ins, (y_ref, yb_ref), l_refs, stage = refs[:2 * nd], refs[2 * nd:2 * nd + 2], refs[2 * nd + 2:3 * nd + 2], refs[-1]
        os_, ls = [], []
        for b, dil in enumerate(DILATIONS):
            o, l = ins[2 * b][...], ins[2 * b + 1][...]
            os_.append(o if dil == 1 else _from_classes(stage, o, dil))
            ls.append(l if dil == 1 else _from_classes(stage, l, dil))
        m = functools.reduce(jnp.maximum, ls)
        es = [jnp.exp(l - m) for l in ls]
        den = functools.reduce(lambda a, e: a + e, es)
        y = functools.reduce(lambda a, t: a + t, [e * o for e, o in zip(es, os_)]) / den
        y_ref[...] = y
        yb_ref[...] = y.astype(BF16)
        lse = m + jnp.log(den)
        for b, dil in enumerate(DILATIONS):
            if dil == 1:
                l_refs[b][...] = lse
            else:
                for r, rows in enumerate(_to_classes(stage, lse, dil)):
                    l_refs[b][r] = rows

    ins = [(_cls_view(t, dil), _cls_kind(dil)) for pair, dil in zip(branches, DILATIONS) for t in pair]
    outs = _rowcall(name, body, S, tm, ins,
                    [((S, BW), F32, ROW), ((S, BW), BF16, ROW)] + [_cls_out(S, dil, F32) for dil in DILATIONS],
                    scratch=[_stage_spec(tm)])
    return outs[0], outs[1], [_flat(t) for t in outs[2:]]


def _delta(name, dyab, yb, ones_bd, tm=256):
    S = yb.shape[0]
    nd = len(DILATIONS)

    def body(d_ref, y_ref, o_ref, *rest):
        dl_refs, do_refs, stage = rest[:nd], rest[nd:2 * nd - 1], rest[-1]
        d = d_ref[...]
        dl = _gsum64(d * y_ref[...], o_ref[...])
        for b, dil in enumerate(DILATIONS):
            if dil == 1:
                dl_refs[b][...] = dl
            else:
                for r, rows in enumerate(_to_classes(stage, dl, dil)):
                    dl_refs[b][r] = rows
                for r, rows in enumerate(_to_classes(stage, d, dil)):
                    do_refs[b - 1][r] = rows.astype(BF16)

    outs = _rowcall(name, body, S, tm, [(dyab, ("row", BW, 1)), (yb, ROW), (ones_bd, FULL)],
                    [_cls_out(S, dil, F32) for dil in DILATIONS] + [_cls_out(S, dil, BF16) for dil in DILATIONS[1:]],
                    scratch=[_stage_spec(tm)])
    return [_flat(t) for t in outs[:nd]], [_flat(t) for t in outs[nd:]]


def _qkv_bwd(name, z, cos, sin, ones_bd, qg, kg, pieces):
    S = z.shape[0]
    nblk = S // QB

    def body(q_ref, k_ref, c_ref, s_ref, o_ref, qg_ref, kg_ref, *rest):
        pr, (dz_ref, dqg_ref, dkg_ref), stage = rest[:15], rest[15:18], rest[18]
        i = pl.program_id(0)
        c, s, ob = _tile4(c_ref[...]), _tile4(s_ref[...]), o_ref[...]
        dq = dk = dv = None
        for b, dil in enumerate(DILATIONS):
            a_q, a_kc, a_kp, a_vc, a_vp = [r[...] for r in pr[5 * b:5 * b + 5]]
            live = ((i + dil) < nblk).astype(F32)
            tq, tk, tv = a_q, a_kc + a_kp * live, a_vc + a_vp * live
            if dil > 1:
                tq, tk, tv = (_from_classes(stage, t, dil) for t in (tq, tk, tv))
            dq, dk, dv = (tq, tk, tv) if b == 0 else (dq + tq, dk + tk, dv + tv)

        def back(x, g, d_rot, acc_ref):
            r = lax.rsqrt(_gsum64(x * x, ob) * (1.0 / HEAD) + EPS)
            n = x * r
            dxn = d_rot * c + _swap32(d_rot * s)
            _acc_add(i, acc_ref, _colsum(dxn * n))
            dn = dxn * g
            return r * (dn - n * (_gsum64(dn * n, ob) * (1.0 / HEAD)))

        dz_ref[:, :BW] = back(q_ref[...], qg_ref[...], dq * QK_SCALE, dqg_ref).astype(BF16)
        dz_ref[:, BW:2 * BW] = back(k_ref[...], kg_ref[...], dk, dkg_ref).astype(BF16)
        dz_ref[:, 2 * BW:] = dv.astype(BF16)

    ins = [(z, ("row", BW, 2)), (z, ("row", BW, 3)), (cos, ROW), (sin, ROW), (ones_bd, FULL), (qg, FULL), (kg, FULL)]
    for piece, dil in zip(pieces, DILATIONS):
        a_q, a_kc, a_kp, a_vc, a_vp = (_cls_view(t, dil) for t in piece)
        own, prev = _cls_kind(dil), _cls_kind(dil, dil)
        ins += [(a_q, own), (a_kc, own), (a_kp, prev), (a_vc, own), (a_vp, prev)]
    return _rowcall(name, body, S, QB, ins,
                    [((S, 3 * BW), BF16, ROW), ((1, BW), F32, ACC), ((1, BW), F32, ACC)], scratch=[_stage_spec(QB)])


def _local_step(x0, tgt, pos, mod, wb, sp, pipe):
    S, D = x0.shape
    md = lambda l, j: mod[l, j:j + 1]
    sh_m, sc_m, g_m, sh_f, sc_f, g_f = ([md(l, j) for l in range(2)] for j in range(6))
    nm_g, nf_g = sp["norm_mix_g"], sp["norm_ffn_g"]

    inv_freq = 1.0 / (ROPE_THETA ** (jnp.arange(0, HEAD, 2, dtype=F32) / HEAD))
    ang = pos.astype(F32)[:, None] * inv_freq
    cs, sn = jnp.cos(ang), jnp.sin(ang)
    cos = jnp.concatenate([cs, cs, cs, cs], axis=1)
    sin = jnp.concatenate([-sn, sn, -sn, sn], axis=1)
    head_of = jnp.arange(BW) // HEAD
    ones_bd = (head_of[:, None] == head_of[None, :]).astype(BF16)
    qg = jnp.tile(sp["b_q_norm_g"].reshape(1, HEAD), (1, N_HEADS))
    kg = jnp.tile(sp["b_k_norm_g"].reshape(1, HEAD), (1, N_HEADS))
    vg = sp["a_vnorm_g"].reshape(1, A_GROUPS * LANES)
    ws = sp["a_spatial_w"][0]
    bias_full = jnp.repeat(sp["a_spatial_b"][0].T, LANES, axis=1)
    ffn_s = [(sp["ffn_dw_w"][l], sp["ffn_dw_b"][l:l + 1]) for l in range(2)]

    h0 = _mod_first("l0_mod", x0, nm_g[0:1], sc_m[0], sh_m[0])
    z = _matmul("l0_in", h0, wb.get("w_in", h0), "nn", F32, tm=2048)
    ya = _mixa_fwd("l0_mixa", z, vg, ws, bias_full)
    qkv = _qkv_fwd("l0_qkv", z, cos, sin, ones_bd, qg, kg)
    branches = [_attn_fwd(f"l0_att{dil}", *qkv[b], dil) for b, dil in enumerate(DILATIONS)]
    yb, yb16, lses = _merge_fwd("l0_merge", branches)
    yab = jnp.concatenate([ya, yb16], axis=1)
    y0 = _matmul("l0_out", yab, wb.get("w_out", yab), "nn", F32, tm=1024, tn=1024)
    x1, h1 = _resid_mod("l0_res1", x0, y0, g_m[0], nf_g[0:1], sc_f[0], sh_f[0])
    ffn_w = [(wb.get("up0", h1), wb.get("dn0", h1), *ffn_s[0])]
    f0, saved0 = _ffn_fwd("l0_ffn", h1, *ffn_w[0])
    x2, h2 = _resid_mod("l0_res2", x1, f0, g_f[0], nm_g[1:2], sc_m[1], sh_m[1])
    p = _matmul("l1_pw1", h2, wb.get("pw1", h2), "nn", F32, tm=2048, bias=sp["conv_pw1_b"])
    y2 = _glu31_fwd("l1_glu", p, sp["conv_dw_w"][0], sp["conv_dw_b"])
    y4 = _ln_silu_fwd("l1_ln", y2, sp["conv_ln_g"], sp["conv_ln_b"])
    y1 = _matmul("l1_pw2", y4, wb.get("pw2", y4), "nn", F32, tm=1024, tn=1024, bias=sp["conv_pw2_b"])
    x3, h3 = _resid_mod("l1_res1", x2, y1, g_m[1], nf_g[1:2], sc_f[1], sh_f[1])
    ffn_w.append((wb.get("up1", h3), wb.get("dn1", h3), *ffn_s[1]))
    f1, saved1 = _ffn_fwd("l1_ffn", h3, *ffn_w[1])
    dx4, lossv, dy, dgate_f1, _ = _loss_head("loss", x3, f1, g_f[1], tgt)

    dh, gf1 = _ffn_bwd("l1_ffn", dy, h3, *saved1, *ffn_w[1])
    tok = pipe.scatter("g1", dict(dn1=gf1["dn"], up1=gf1["up"]))
    dx3, dsh_f1, dsc_f1, dnf1, dy, dgate_m1, dpw2_b = _mod_bwd("l1_dmod2", dx4, dh, x3, nf_g[1:2], sc_f[1] + tok,
                                                             y1, g_m[1])
    dy4 = _matmul("l1_dpw2_x", dy, wb.get("pw2"), "nt", F32, tm=1024, tn=1024)
    g_pw2 = _matmul("l1_dpw2_w", y4, dy, "tn", BF16, tm=1024, tn=1024, tk=2048)
    dy2, dln_g, dln_b = _ln_silu_bwd("l1_dln", y2, dy4, sp["conv_ln_g"], sp["conv_ln_b"])
    dp, ddw_w, ddw_b, dpw1_b = _glu31_bwd("l1_dglu", p, dy2, sp["conv_dw_w"][0])
    g_pw1 = _matmul("l1_dpw1_w", h2, dp, "tn", BF16, tm=1024, tn=1024, tk=2048)
    tok = pipe.scatter("g2", dict(pw2=g_pw2, pw1=g_pw1))
    pipe.collect("g1", g_pw1)
    dh = _matmul("l1_dpw1_x", dp, wb.get("pw1"), "nt", F32, tm=1024, tn=1024, tk=2048)
    dx2, dsh_m1, dsc_m1, dnm1, dy, dgate_f0, _ = _mod_bwd("l1_dmod1", dx3, dh, x2, nm_g[1:2], sc_m[1] + tok,
                                                        f0, g_f[0])
    dh, gf0 = _ffn_bwd("l0_ffn", dy, h1, *saved0, *ffn_w[0])
    tok = pipe.scatter("g3", dict(dn0=gf0["dn"], up0=gf0["up"]))
    pipe.collect("g2", gf0["up"])
    dx1, dsh_f0, dsc_f0, dnf0, dy, dgate_m0, _ = _mod_bwd("l0_dmod2", dx2, dh, x1, nf_g[0:1], sc_f[0] + tok,
                                                        y0, g_m[0])
    dyab = _matmul("l0_dout_x", dy, wb.get("w_out"), "nt", F32, tm=1024, tn=1024)
    g_out = _matmul("l0_dout_w", yab, dy, "tn", BF16, tm=1024, tn=1024, tk=2048)
    vg = vg + pipe.scatter("g4", dict(w_out=g_out))
    dza, dws, dbf, dvg = _mixa_bwd("l0_dmixa", z, dyab, vg, ws, bias_full)
    deltas, dos = _delta("l0_delta", dyab, yb, ones_bd)
    pieces = []
    for b, dil in enumerate(DILATIONS):
        do = (dyab, ("row", BW, 1)) if dil == 1 else (dos[b - 1], ROW)
        pieces.append(_attn_bwd(f"l0_datt{dil}", *qkv[b], do, lses[b], deltas[b], dil))
    dzb, dqg, dkg = _qkv_bwd("l0_dqkv", z, cos, sin, ones_bd, qg, kg, pieces)
    small_tok = _start_small_grads(pipe, dzb, sp, lossv, ((dgate_m0, dsh_f0, dsc_f0, dgate_f0),
                                                   (dsh_m1, dsc_m1, dgate_m1, dsh_f1, dsc_f1, dgate_f1)),
                                   (dnm1, dnf0, dnf1), dvg, dws, dbf, dqg, dkg, dpw1_b, ddw_w, ddw_b, dln_g, dln_b,
                                   dpw2_b, gf0, gf1)
    dz = jnp.concatenate([dza, dzb], axis=1)
    g_in = _matmul("l0_din_w", h0, dz, "tn", BF16, tm=1024, tn=1280, tk=2048, after=small_tok)
    dh = _matmul("l0_din_x", dz, wb.get("w_in"), "nt", F32, tm=1024, tn=1024, tk=2560, after=small_tok)
    grad_x, dsh_m0, dsc_m0, dnm0 = _mod_bwd("l0_dmod1", dx1, dh, x0, nm_g[0:1], sc_m[0])
    return lossv, grad_x, (dsh_m0, dsc_m0, dnm0), g_in


def _start_small_grads(pipe, after, sp, lossv, mods, norms, dvg, dws, dbf, dqg, dkg, dpw1_b, ddw_w, ddw_b, dln_g,
                       dln_b, dpw2_b, gf0, gf1):
    (dgate_m0, dsh_f0, dsc_f0, dgate_f0), (dsh_m1, dsc_m1, dgate_m1, dsh_f1, dsc_f1, dgate_f1) = mods
    dnm1, dnf0, dnf1 = norms
    zero = jnp.zeros_like(dnm1)
    dmod = jnp.stack([jnp.concatenate([zero, zero, dgate_m0, dsh_f0, dsc_f0, dgate_f0], axis=0),
                      jnp.concatenate([dsh_m1, dsc_m1, dgate_m1, dsh_f1, dsc_f1, dgate_f1], axis=0)])
    small = dict(
        norm_mix_g=jnp.concatenate([zero, dnm1], axis=0),
        norm_ffn_g=jnp.concatenate([dnf0, dnf1], axis=0),
        a_vnorm_g=dvg.reshape(1, A_GROUPS, LANES),
        a_spatial_w=dws[None],
        a_spatial_b=dbf.reshape(CHUNK, A_GROUPS, LANES).sum(-1).T[None],
        b_q_norm_g=dqg.reshape(N_HEADS, HEAD).sum(0)[None],
        b_k_norm_g=dkg.reshape(N_HEADS, HEAD).sum(0)[None],
        conv_pw1_b=dpw1_b, conv_dw_w=ddw_w[None, :sp["conv_dw_w"].shape[1]], conv_dw_b=ddw_b,
        conv_ln_g=dln_g, conv_ln_b=dln_b, conv_pw2_b=dpw2_b,
        ffn_dw_w=jnp.stack([gf0["dw_w"], gf1["dw_w"]]),
        ffn_dw_b=jnp.concatenate([gf0["dw_b"], gf1["dw_b"]], axis=0),
    )
    return pipe.start_small(dmod, lossv, small, after)


ADA_TN = 512


def _ada_fwd(name, c_all, ada_w, ada_b_sh):
    L, D, N = ada_w.shape
    B = c_all.shape[0]

    def body(c_ref, w_ref, b_ref, o_ref):
        cv = c_ref[...]
        ca = (cv * _sigmoid(cv)).astype(BF16)
        o_ref[0] = jnp.dot(ca, w_ref[0].astype(BF16), preferred_element_type=F32) + b_ref[0]

    return pl.pallas_call(
        body, name=name, grid=(L, N // ADA_TN),
        in_specs=[pl.BlockSpec((B, D), lambda l, j: (0, 0)), pl.BlockSpec((1, D, ADA_TN), lambda l, j: (l, 0, j)),
                  pl.BlockSpec((1, 1, ADA_TN), lambda l, j: (l, 0, j))],
        out_specs=pl.BlockSpec((1, B, ADA_TN), lambda l, j: (l, 0, j)),
        out_shape=jax.ShapeDtypeStruct((L, B, N), F32),
        compiler_params=_params(("parallel", "parallel")),
    )(c_all, ada_w, ada_b_sh.reshape(L, 1, N))


def _adamw_val(w, g, m, v):
    m2 = ADAM_B1 * m + (1.0 - ADAM_B1) * g
    v2 = ADAM_B2 * v + (1.0 - ADAM_B2) * (g * g)
    m_hat = m2 / (1.0 - ADAM_B1 ** ADAM_STEP)
    v_hat = v2 / (1.0 - ADAM_B2 ** ADAM_STEP)
    delta = -ADAM_LR * (m_hat / (jnp.sqrt(v_hat) + ADAM_EPS) + ADAM_WD * w)
    return delta, m2, v2


def _ada_update(name, c_all, dmod_sh, w, m, v):
    L, D, N = w.shape
    B = c_all.shape[0]

    def body(c_ref, d_ref, w_ref, m_ref, v_ref, g_ref, dl_ref, mo_ref, vo_ref):
        cv = c_ref[...]
        ca = (cv * _sigmoid(cv)).astype(BF16)
        g = lax.dot_general(ca, d_ref[0].astype(BF16), _DN["tn"], preferred_element_type=F32)
        g_ref[0] = g
        dl_ref[0], mo_ref[0], vo_ref[0] = _adamw_val(w_ref[0], g, m_ref[0], v_ref[0])

    wspec = pl.BlockSpec((1, D, ADA_TN), lambda l, j: (l, 0, j))
    return pl.pallas_call(
        body, name=name, grid=(L, N // ADA_TN),
        in_specs=[pl.BlockSpec((B, D), lambda l, j: (0, 0)), pl.BlockSpec((1, B, ADA_TN), lambda l, j: (l, 0, j)),
                  wspec, wspec, wspec],
        out_specs=[wspec] * 4, out_shape=[jax.ShapeDtypeStruct((L, D, N), F32)] * 4,
        compiler_params=_params(("parallel", "parallel")),
    )(c_all, dmod_sh, w, m, v)


def _adamw(name, w, g, m, v):
    R, C = w.shape
    tm = R
    for cand in (256, 128, 64, 32, 16, 8):
        if R % cand == 0 and cand * C * 4 <= (1 << 20):
            tm = cand
            break

    def body(w_ref, g_ref, m_ref, v_ref, d_ref, mo_ref, vo_ref):
        d_ref[...], mo_ref[...], vo_ref[...] = _adamw_val(w_ref[...], g_ref[...], m_ref[...], v_ref[...])

    return _rowcall(name, body, R, tm, [(w, ROW), (g, ROW), (m, ROW), (v, ROW)], [((R, C), F32, ROW)] * 3)


def _row_tile(rows, width, itemsize=4, limit=1 << 20):
    for cand in (512, 256, 128, 64, 32, 16):
        if rows % cand == 0 and cand * width * itemsize <= limit:
            return cand
    raise ValueError((rows, width))


def _cast_into_full(name, a, layer, q, kind, after):
    L, r, c = a.shape
    tm = _row_tile(r, c)
    if kind == "col":
        full, o_spec = (r, N_CHIPS * c), pl.BlockSpec((tm, c), lambda i, q_ref: (i, q_ref[0]))
    else:
        full, o_spec = (N_CHIPS * r, c), pl.BlockSpec((tm, c), lambda i, q_ref: (q_ref[0] * (r // tm) + i, 0))

    def body(q_ref, a_ref, after_ref, o_ref):
        o_ref[...] = a_ref[0].astype(BF16)

    return pl.pallas_call(
        body, name=name,
        grid_spec=pltpu.PrefetchScalarGridSpec(
            num_scalar_prefetch=1, grid=(r // tm,),
            in_specs=[pl.BlockSpec((1, tm, c), lambda i, q_ref: (layer, i, 0)), ANY], out_specs=o_spec),
        out_shape=jax.ShapeDtypeStruct(full, BF16), compiler_params=_params(("parallel",)),
    )(q.reshape(1).astype(jnp.int32), a, after)


def _sum4(name, g, rcv, q, kind, n):
    r, c = rcv.shape[1:]
    tm = _row_tile(r, c)
    if kind == "col":
        g_spec = pl.BlockSpec((tm, n), lambda i, q_ref: (i, q_ref[0]))
    else:
        g_spec = pl.BlockSpec((tm, c), lambda i, q_ref: (q_ref[0] * (n // tm) + i, 0))

    def body(q_ref, g_ref, r_ref, o_ref):
        acc = g_ref[...].astype(F32)
        for j in range(3):
            acc = acc + r_ref[j].astype(F32)
        o_ref[...] = acc

    return pl.pallas_call(
        body, name=name,
        grid_spec=pltpu.PrefetchScalarGridSpec(
            num_scalar_prefetch=1, grid=(r // tm,),
            in_specs=[g_spec, pl.BlockSpec((3, tm, c), lambda i, q_ref: (0, i, 0))],
            out_specs=pl.BlockSpec((tm, c), lambda i, q_ref: (i, 0))),
        out_shape=jax.ShapeDtypeStruct((r, c), F32), compiler_params=_params(("parallel",)),
    )(q.reshape(1).astype(jnp.int32), g, rcv)


def _adamw_sum(name, w, m, v, layer, mine, theirs, prev):
    L, r, c = w.shape
    tm = _row_tile(r, c, limit=3 << 19)
    lay = pl.BlockSpec((1, tm, c), lambda i: (layer, i, 0))
    flat = pl.BlockSpec((tm, c), lambda i: (i, 0))
    n_prev = 0 if prev is None else 4

    def body(w_ref, m_ref, v_ref, a_ref, b_ref, *rest):
        g_ref, d_ref, mo_ref, vo_ref = rest[n_prev:]
        g = a_ref[...] + b_ref[...]
        g_ref[0] = g
        d_ref[0], mo_ref[0], vo_ref[0] = _adamw_val(w_ref[0], g, m_ref[0], v_ref[0])

    return pl.pallas_call(
        body, name=name, grid=(r // tm,),
        in_specs=[lay, lay, lay, flat, flat] + [ANY] * n_prev, out_specs=[lay] * 4,
        out_shape=[jax.ShapeDtypeStruct((L, r, c), F32)] * 4,
        input_output_aliases={5 + k: k for k in range(n_prev)},
        compiler_params=_params(("parallel",)),
    )(w, m, v, mine, theirs, *(prev or ()))


def _sum8(name, gathered, own=None):
    R, N = gathered.shape
    P = R // 8

    def body(g_ref, *rest):
        o_ref = rest[-1]
        me = 4 * lax.axis_index("x") + 2 * lax.axis_index("y") + lax.axis_index("c")
        acc = None
        for d in range(8):
            blk = g_ref[d * P:(d + 1) * P, :]
            if own is not None:
                blk = jnp.where(me == d, rest[0][...], blk)
            acc = blk if d == 0 else acc + blk
        o_ref[...] = acc

    return pl.pallas_call(body, name=name, out_shape=jax.ShapeDtypeStruct((P, N), F32),
                          compiler_params=pltpu.CompilerParams(vmem_limit_bytes=VMEM_LIMIT),
                          )(gathered, *(() if own is None else (own,)))


ANY = pl.BlockSpec(memory_space=pl.ANY)


def _mesh_pos():
    x, y, c = lax.axis_index("x"), lax.axis_index("y"), lax.axis_index("c")
    other_chips = [(1 - x, y), (x, 1 - y), (1 - x, 1 - y)]
    return x, y, c, other_chips


def _allgather8(name, blk, after=None):
    m_per, n = blk.shape

    def body(x_ref, *rest):
        out_ref, send_sems, recv_sems, local_sem = rest[after is not None:]
        x, y, c, chips = _mesh_pos()
        me, sibling = (x, y, c), (x, y, 1 - c)

        def rows(px, py, pc):
            return out_ref.at[pl.ds((4 * px + 2 * py + pc) * m_per, m_per), :]

        def copy(k, block, to, src=None):
            return pltpu.make_async_remote_copy(
                src_ref=rows(*block) if src is None else src, dst_ref=rows(*block),
                send_sem=send_sems.at[k], recv_sem=recv_sems.at[k], device_id=to, device_id_type=MESH)

        mine = pltpu.make_async_copy(x_ref, rows(*me), local_sem)
        mine.start()
        first = [copy(0, me, sibling, src=x_ref)]
        first += [copy(1 + j, me, (*chip, c), src=x_ref) for j, chip in enumerate(chips)]
        for cp in first:
            cp.start()
        passed = [copy(4 + j, (*chip, c), sibling) for j, chip in enumerate(chips)]
        for j, chip in enumerate(chips):
            copy(1 + j, (*chip, c), me).wait_recv()
            passed[j].start()
        copy(0, sibling, me).wait_recv()
        for j, chip in enumerate(chips):
            copy(4 + j, (*chip, 1 - c), me).wait_recv()
        for cp in first + passed:
            cp.wait_send()
        mine.wait()

    return pl.pallas_call(
        body, name=name, out_shape=jax.ShapeDtypeStruct((8 * m_per, n), blk.dtype),
        in_specs=[pl.BlockSpec(memory_space=pltpu.VMEM)] + [ANY] * (after is not None),
        out_specs=pl.BlockSpec(memory_space=pltpu.VMEM),
        scratch_shapes=[pltpu.SemaphoreType.DMA((7,)), pltpu.SemaphoreType.DMA((7,)), pltpu.SemaphoreType.DMA],
        compiler_params=pltpu.CompilerParams(vmem_limit_bytes=VMEM_LIMIT),
    )(blk, *(() if after is None else (after,)))


BIG = dict(w_in=("col", "ab_w_in", 0), w_out=("row", "ab_w_out", 0), up0=("col", "ffn_up_w", 0),
           dn0=("row", "ffn_down_w", 0), pw1=("col", "conv_pw1_w", 0), pw2=("row", "conv_pw2_w", 0),
           up1=("col", "ffn_up_w", 1), dn1=("row", "ffn_down_w", 1))
N_CHIPS = 4
HBM = pl.BlockSpec(memory_space=pltpu.HBM)
SEM = pl.BlockSpec(memory_space=pltpu.SEMAPHORE)
EFFECT = pltpu.SideEffectType.DATAFLOW_SIDE_EFFECTING


def _region(kind, ref, q, n):
    if kind == "col":
        return ref.at[:, pl.ds(q * n, n)]
    return ref.at[pl.ds(q * n, n), :]


def _gather_plan(kind, n):
    def remote(src, land, pos):
        x, y, c, chips = pos
        mine = _region(kind, land, 2 * x + y, n)
        return [(mine, mine, (*chip, c)) for chip in chips]

    return ("gather", kind, n), remote


def _scatter_plan(kind, n):
    def remote(src, land, pos):
        _, _, c, chips = pos
        return [(_region(kind, src, 2 * chip[0] + chip[1], n), land.at[j], (*chip, c)) for j, chip in enumerate(chips)]

    return ("scatter", kind, n), remote


def _everyone_plan(rows):
    def remote(src, land, pos):
        x, y, c, _ = pos
        mine = land.at[pl.ds((4 * x + 2 * y + c) * rows, rows), :]
        flip = lambda v, bit: 1 - v if bit else v
        return [(src, mine, (flip(x, k & 4), flip(y, k & 2), flip(c, k & 1))) for k in range(1, 8)]

    return ("everyone", rows), remote


def _sibling_plan():
    def remote(src, land, pos):
        x, y, c, _ = pos
        return [(src, land, (x, y, 1 - c))]

    return ("sibling",), remote


def _split_start(name, items, after=None):
    n = len(items)
    plans = [it[2] for it in items]
    n_in = 2 * n + (after is not None)

    def body(*refs):
        srcs, lands = refs[:n], refs[n:2 * n]
        sends, recvs = refs[n_in:n_in + n], refs[n_in + n:n_in + 2 * n]
        token = refs[n_in + 4 * n]
        pos = _mesh_pos()
        for a, (_, remote) in enumerate(plans):
            for k, (s, d, dev) in enumerate(remote(srcs[a], lands[a], pos)):
                pltpu.make_async_remote_copy(src_ref=s, dst_ref=d, send_sem=sends[a].at[k], recv_sem=recvs[a].at[k],
                                             device_id=dev, device_id_type=MESH).start()
        token[...] = jnp.zeros_like(token)

    sems = [pltpu.SemaphoreType.DMA((it[3],)) for it in items]
    bufs = [pltpu.HBM(it[k].shape, it[k].dtype) for k in (0, 1) for it in items]
    outs = pl.pallas_call(
        body, name=name, out_shape=[*sems, *sems, *bufs, jax.ShapeDtypeStruct((8, LANES), F32)],
        in_specs=[HBM] * (2 * n) + [ANY] * (n_in - 2 * n),
        out_specs=[SEM] * (2 * n) + [HBM] * (2 * n) + [pl.BlockSpec(memory_space=pltpu.VMEM)],
        input_output_aliases={i: 2 * n + i for i in range(2 * n)},
        compiler_params=pltpu.CompilerParams(has_side_effects=EFFECT),
    )(*[pltpu.with_memory_space_constraint(it[k], pltpu.HBM) for k in (0, 1) for it in items],
      *(() if after is None else (after,)))
    state = [(items[a][2], items[a][3], outs[2 * n + a], outs[3 * n + a], outs[a], outs[n + a]) for a in range(n)]
    return state, outs[4 * n]


def _split_wait(name, state, after):
    n = len(state)

    def body(*refs):
        srcs, lands = refs[:n], refs[n:2 * n]
        sends, recvs = refs[2 * n:3 * n], refs[3 * n:4 * n]
        pos = _mesh_pos()
        for a, ((_, remote), *_) in enumerate(state):
            for k, (s, d, dev) in enumerate(remote(srcs[a], lands[a], pos)):
                cp = pltpu.make_async_remote_copy(src_ref=s, dst_ref=d, send_sem=sends[a].at[k], recv_sem=recvs[a].at[k],
                                                  device_id=dev, device_id_type=MESH)
                cp.wait_send()
                cp.wait_recv()

    bufs = [st[k] for k in (2, 3) for st in state]
    outs = pl.pallas_call(
        body, name=name, out_shape=[pltpu.HBM(b.shape, b.dtype) for b in bufs],
        in_specs=[HBM] * (2 * n) + [SEM] * (2 * n) + [ANY], out_specs=[HBM] * (2 * n),
        input_output_aliases={i: i for i in range(2 * n)},
        compiler_params=pltpu.CompilerParams(has_side_effects=EFFECT),
    )(*bufs, *[st[k] for k in (4, 5) for st in state], after)
    return outs[:n], outs[n:]


class _Weights:
    def __init__(self, w, q, after):
        unused = jnp.zeros((16, LANES), BF16)

        def item(name, after):
            kind, pname, layer = BIG[name]
            _, r, c = w[pname].shape
            land = _cast_into_full(f"cast_{name}", w[pname], layer, q, kind, after)
            return unused, land, _gather_plan(kind, c if kind == "col" else r), N_CHIPS - 1

        first, *rest = BIG
        state1, token1 = _split_start("gw_start_first", [item(first, after)], after)
        state2, self.token = _split_start("gw_start_rest", [item(name, token1) for name in rest], token1)
        self.pending = dict(zip(BIG, state1 + state2))
        self.ready = {}

    def get(self, name, after=None):
        if name not in self.ready:
            self.ready[name] = _split_wait(f"gw_wait_{name}", [self.pending.pop(name)], after)[1][0]
        return self.ready[name]


class _GradPipe:
    def __init__(self, q, w, m, v):
        self.q, self.w, self.m, self.v = q, w, m, v
        self.stage, self.results = {}, {}

    def start_small(self, dmod, lossv, small, after):
        self.small_names = [n for n in REPLICATED if n != "ada_b"] + list(SMALL_SHARDED)
        payload = [dmod.reshape(2, -1), lossv] + [small[n] for n in self.small_names]
        self.small_shapes = [p.shape for p in payload]
        packed = _pack(payload)
        land = jnp.zeros((8 * PACK_ROWS, packed.shape[1]), F32)
        self.small_state, token = _split_start("ag_grads_start", [(packed, land, _everyone_plan(PACK_ROWS), 7)], after)
        return token

    def wait_small(self, after):
        srcs, lands = _split_wait("ag_grads_wait", self.small_state, after)
        return srcs[0], lands[0]

    def scatter(self, group, grads, after=None):
        items = []
        for name, g in grads.items():
            kind = BIG[name][0]
            rows, cols = g.shape
            n = (cols if kind == "col" else rows) // N_CHIPS
            reg = (rows, n) if kind == "col" else (n, cols)
            items.append((g, lax.empty((N_CHIPS - 1, *reg), BF16), _scatter_plan(kind, n), N_CHIPS - 1))
        state, token = _split_start(f"gs_start_{group}", items, after)
        self.stage[group] = (list(grads), state)
        return token[0, 0]

    def collect(self, group, after):
        names, state = self.stage[group]
        srcs, lands = _split_wait(f"gs_wait_{group}", state, after)
        items = []
        for name, st, g, land in zip(names, state, srcs, lands):
            _, kind, n = st[0][0]
            part = _sum4(f"sum_{name}", g, land, self.q, kind, n)
            items.append((part, lax.empty(part.shape, F32), _sibling_plan(), 1))
        state, token = _split_start(f"sw_start_{group}", items)
        self.stage[group] = (names, state)
        return token

    def finish(self, group, after):
        names, state = self.stage.pop(group)
        srcs, lands = _split_wait(f"sw_wait_{group}", state, after)
        for name, mine, theirs in zip(names, srcs, lands):
            _, pname, layer = BIG[name]
            self.results[pname] = _adamw_sum(f"adamw_{name}", self.w[pname], self.m[pname], self.v[pname], layer,
                                             mine, theirs, self.results.get(pname))


PACK_ROWS = 8


def _pack(arrays):
    flat = jnp.concatenate([a.reshape(-1) for a in arrays])
    n = flat.shape[0]
    padded = -(-n // (PACK_ROWS * LANES)) * (PACK_ROWS * LANES)
    return jnp.pad(flat, (0, padded - n)).reshape(PACK_ROWS, padded // PACK_ROWS)


def _unpack(packed, shapes):
    flat = packed.reshape(-1)
    out, off = [], 0
    for s in shapes:
        n = 1
        for d in s:
            n *= d
        out.append(flat[off:off + n].reshape(s))
        off += n
    return out


REPLICATED = ("ada_b", "norm_mix_g", "norm_ffn_g", "a_vnorm_g", "a_spatial_w", "a_spatial_b", "b_q_norm_g",
              "b_k_norm_g", "ffn_dw_b")
SMALL_SHARDED = ("conv_pw1_b", "conv_dw_w", "conv_dw_b", "conv_ln_g", "conv_ln_b", "conv_pw2_b", "ffn_dw_w")
WEIGHTS = ("ada_w", "ada_b", "norm_mix_g", "norm_ffn_g", "ab_w_in", "a_vnorm_g", "a_spatial_w", "a_spatial_b",
           "b_q_norm_g", "b_k_norm_g", "ab_w_out", "conv_pw1_w", "conv_pw1_b", "conv_dw_w", "conv_dw_b", "conv_ln_g",
           "conv_ln_b", "conv_pw2_w", "conv_pw2_b", "ffn_up_w", "ffn_dw_w", "ffn_dw_b", "ffn_down_w")

def kernel(x, c, positions, ada_w, ada_b, norm_mix_g, norm_ffn_g, ab_w_in, a_vnorm_g, a_spatial_w, a_spatial_b, b_q_norm_g, b_k_norm_g, ab_w_out, conv_pw1_w, conv_pw1_b, conv_dw_w, conv_dw_b, conv_ln_g, conv_ln_b, conv_pw2_w, conv_pw2_b, ffn_up_w, ffn_dw_w, ffn_dw_b, ffn_down_w, loss_target, m_ada_w, m_ada_b, m_norm_mix_g, m_norm_ffn_g, m_ab_w_in, m_a_vnorm_g, m_a_spatial_w, m_a_spatial_b, m_b_q_norm_g, m_b_k_norm_g, m_ab_w_out, m_conv_pw1_w, m_conv_pw1_b, m_conv_dw_w, m_conv_dw_b, m_conv_ln_g, m_conv_ln_b, m_conv_pw2_w, m_conv_pw2_b, m_ffn_up_w, m_ffn_dw_w, m_ffn_dw_b, m_ffn_down_w, v_ada_w, v_ada_b, v_norm_mix_g, v_norm_ffn_g, v_ab_w_in, v_a_vnorm_g, v_a_spatial_w, v_a_spatial_b, v_b_q_norm_g, v_b_k_norm_g, v_ab_w_out, v_conv_pw1_w, v_conv_pw1_b, v_conv_dw_w, v_conv_dw_b, v_conv_ln_g, v_conv_ln_b, v_conv_pw2_w, v_conv_pw2_b, v_ffn_up_w, v_ffn_dw_w, v_ffn_dw_b, v_ffn_down_w):
    w = dict(ada_w=ada_w, ada_b=ada_b, norm_mix_g=norm_mix_g, norm_ffn_g=norm_ffn_g, ab_w_in=ab_w_in, a_vnorm_g=a_vnorm_g, a_spatial_w=a_spatial_w, a_spatial_b=a_spatial_b, b_q_norm_g=b_q_norm_g, b_k_norm_g=b_k_norm_g, ab_w_out=ab_w_out, conv_pw1_w=conv_pw1_w, conv_pw1_b=conv_pw1_b, conv_dw_w=conv_dw_w, conv_dw_b=conv_dw_b, conv_ln_g=conv_ln_g, conv_ln_b=conv_ln_b, conv_pw2_w=conv_pw2_w, conv_pw2_b=conv_pw2_b, ffn_up_w=ffn_up_w, ffn_dw_w=ffn_dw_w, ffn_dw_b=ffn_dw_b, ffn_down_w=ffn_down_w)
    m = dict(ada_w=m_ada_w, ada_b=m_ada_b, norm_mix_g=m_norm_mix_g, norm_ffn_g=m_norm_ffn_g, ab_w_in=m_ab_w_in, a_vnorm_g=m_a_vnorm_g, a_spatial_w=m_a_spatial_w, a_spatial_b=m_a_spatial_b, b_q_norm_g=m_b_q_norm_g, b_k_norm_g=m_b_k_norm_g, ab_w_out=m_ab_w_out, conv_pw1_w=m_conv_pw1_w, conv_pw1_b=m_conv_pw1_b, conv_dw_w=m_conv_dw_w, conv_dw_b=m_conv_dw_b, conv_ln_g=m_conv_ln_g, conv_ln_b=m_conv_ln_b, conv_pw2_w=m_conv_pw2_w, conv_pw2_b=m_conv_pw2_b, ffn_up_w=m_ffn_up_w, ffn_dw_w=m_ffn_dw_w, ffn_dw_b=m_ffn_dw_b, ffn_down_w=m_ffn_down_w)
    v = dict(ada_w=v_ada_w, ada_b=v_ada_b, norm_mix_g=v_norm_mix_g, norm_ffn_g=v_norm_ffn_g, ab_w_in=v_ab_w_in, a_vnorm_g=v_a_vnorm_g, a_spatial_w=v_a_spatial_w, a_spatial_b=v_a_spatial_b, b_q_norm_g=v_b_q_norm_g, b_k_norm_g=v_b_k_norm_g, ab_w_out=v_ab_w_out, conv_pw1_w=v_conv_pw1_w, conv_pw1_b=v_conv_pw1_b, conv_dw_w=v_conv_dw_w, conv_dw_b=v_conv_dw_b, conv_ln_g=v_conv_ln_g, conv_ln_b=v_conv_ln_b, conv_pw2_w=v_conv_pw2_w, conv_pw2_b=v_conv_pw2_b, ffn_up_w=v_ffn_up_w, ffn_dw_w=v_ffn_dw_w, ffn_dw_b=v_ffn_dw_b, ffn_down_w=v_ffn_down_w)
    S, D = x.shape[1], x.shape[2]
    xi, yi, ci = lax.axis_index("x"), lax.axis_index("y"), lax.axis_index("c")
    q = 2 * xi + yi
    b = 2 * q + ci
    take_dev = lambda g: g.reshape(8, PACK_ROWS, -1)

    c_all = _allgather8("ag_c", c.reshape(PACK_ROWS, D // PACK_ROWS)).reshape(8, D)
    n_ada = ada_w.shape[2]
    mod_sh = _ada_fwd("ada_fwd", c_all, ada_w, lax.dynamic_slice_in_dim(ada_b, q * n_ada, n_ada, axis=1))
    sh_shapes = [mod_sh.shape] + [w[n].shape for n in SMALL_SHARDED]
    gathered_mod = _allgather8("ag_mod", _pack([mod_sh] + [w[n] for n in SMALL_SHARDED]))
    per_chip = [_unpack(blk, sh_shapes) for blk in take_dev(gathered_mod)[0::2]]
    mod_g = jnp.stack([pc[0] for pc in per_chip])
    mod_mine = lax.dynamic_index_in_dim(mod_g, b, axis=2, keepdims=False)
    mod = mod_mine.transpose(1, 0, 2).reshape(2, 6, D)
    sp = {n: jnp.concatenate([pc[1 + i] for pc in per_chip], axis=-1) for i, n in enumerate(SMALL_SHARDED)}
    sp.update({n: w[n] for n in REPLICATED if n != "ada_b"})

    wb = _Weights(w, q, gathered_mod)
    mod = mod + wb.token[0, 0]

    pipe = _GradPipe(q, w, m, v)
    lossv, grad_x, late, g_in = _local_step(x[0], loss_target[0], positions[0], mod, wb, sp, pipe)

    swapped = pipe.collect("g3", grad_x)
    swapped = pipe.collect("g4", swapped)

    own, gathered = pipe.wait_small(swapped)
    totals = _unpack(_sum8("sum_grads", gathered, own), pipe.small_shapes)
    grads = dict(zip(["ada_b", "loss_columns"] + pipe.small_names, totals))
    loss = 0.5 * jnp.sum(grads.pop("loss_columns")) / D
    late_g = _allgather8("ag_late", _pack(list(late)), gathered)
    pipe.scatter("g5", dict(w_in=g_in), late_g)
    pipe.finish("g1", late_g)
    pipe.finish("g2", pipe.results["ffn_up_w"][0])
    late_tot = _unpack(_sum8("sum_late", late_g), [(3, D)])[0]
    grads["ada_b"] = grads["ada_b"].at[0, :2 * D].add(late_tot[:2].reshape(-1))
    grads["norm_mix_g"] = grads["norm_mix_g"].at[0].add(late_tot[2])
    for n in SMALL_SHARDED:
        n_sh = w[n].shape[-1]
        grads[n] = lax.dynamic_slice_in_dim(grads[n], q * n_sh, n_sh, axis=grads[n].ndim - 1)
    dmod_of = lambda packed: packed.reshape(packed.shape[0] // PACK_ROWS, -1)[:, :2 * 6 * D].reshape(-1, 2, 6 * D)
    dmod_all = jnp.where((jnp.arange(8) == b)[:, None, None], dmod_of(own), dmod_of(gathered))
    late_all = take_dev(late_g).reshape(8, -1)[:, :3 * D].reshape(8, 3, D)
    dmod_all = dmod_all.at[:, 0, :2 * D].add(late_all[:, :2].reshape(8, 2 * D))
    dmod_sh = lax.dynamic_slice_in_dim(dmod_all, q * n_ada, n_ada, axis=2).transpose(1, 0, 2)

    pipe.finish("g3", pipe.results["conv_pw1_w"][0])
    pipe.finish("g4", pipe.results["ffn_up_w"][0])
    grads["ada_w"], delta_ada, m_ada, v_ada = _ada_update("ada_update", c_all, dmod_sh, ada_w, m_ada_w, v_ada_w)
    delta, new_m, new_v = dict(ada_w=delta_ada), dict(ada_w=m_ada), dict(ada_w=v_ada)
    rest = list(REPLICATED) + list(SMALL_SHARDED)
    rest_shapes = [w[n].shape for n in rest]
    outs = _adamw("adamw_small", *[_pack([src[n].reshape(w[n].shape) for n in rest]) for src in (w, grads, m, v)])
    for tgt, packed in zip((delta, new_m, new_v), outs):
        tgt.update(dict(zip(rest, _unpack(packed, rest_shapes))))
    for n in rest:
        grads[n] = grads[n].reshape(w[n].shape)
    pipe.finish("g5", pipe.collect("g5", outs[0]))
    for n, res in pipe.results.items():
        grads[n], delta[n], new_m[n], new_v[n] = res

    return (loss, grad_x[None], *[grads[n] for n in WEIGHTS], *[delta[n] for n in WEIGHTS],
            *[new_m[n] for n in WEIGHTS], *[new_v[n] for n in WEIGHTS])
```

```python
import functools

import jax
import jax.numpy as jnp
from jax import lax
from jax.experimental import pallas as pl
from jax.experimental.pallas import tpu as pltpu

F32, BF16 = jnp.float32, jnp.bfloat16
EPS = 1e-6
NEG = -1e30
ROPE_THETA = 10000.0
LANES = 128
VMEM_LIMIT = 56 * 1024 * 1024
ADAM_LR, ADAM_B1, ADAM_B2, ADAM_EPS, ADAM_WD, ADAM_STEP = 0.001, 0.9, 0.999, 1e-08, 0.01, 10
MESH = pl.DeviceIdType.MESH


def _params(sem):
    return pltpu.CompilerParams(dimension_semantics=sem, vmem_limit_bytes=VMEM_LIMIT)


_DN = {"nn": (((1,), (0,)), ((), ())), "nt": (((1,), (1,)), ((), ())), "tn": (((0,), (0,)), ((), ()))}


def _matmul(name, a, b, mode, out_dtype, tm=512, tn=512, tk=1024, bias=None, after=None):
    if mode == "nn":
        (M, K), N = a.shape, b.shape[1]
    elif mode == "nt":
        (M, K), N = a.shape, b.shape[0]
    else:
        (K, M), N = a.shape, b.shape[1]
    tm, tn, tk = min(tm, M), min(tn, N), min(tk, K)
    assert M % tm == 0 and N % tn == 0 and K % tk == 0, (name, M, N, K, tm, tn, tk)
    nk = K // tk
    if mode == "tn":
        a_spec = pl.BlockSpec((tk, tm), lambda i, j, k: (k, i))
    else:
        a_spec = pl.BlockSpec((tm, tk), lambda i, j, k: (i, k))
    if mode == "nt":
        b_spec = pl.BlockSpec((tn, tk), lambda i, j, k: (j, k))
    else:
        b_spec = pl.BlockSpec((tk, tn), lambda i, j, k: (k, j))
    in_specs, args = [a_spec, b_spec], [a, b]
    if bias is not None:
        in_specs.append(pl.BlockSpec((1, tn), lambda i, j, k: (0, j)))
        args.append(bias)
    if after is not None:
        in_specs.append(pl.BlockSpec(memory_space=pl.ANY))
        args.append(after)
    n_in = len(args)

    def body(*refs):
        a_ref, b_ref, o_ref = refs[0], refs[1], refs[n_in]
        p = lax.dot_general(a_ref[...], b_ref[...], _DN[mode], preferred_element_type=F32)

        def finish(acc):
            if bias is not None:
                acc = acc + refs[2][...]
            o_ref[...] = acc.astype(o_ref.dtype)

        if nk == 1:
            finish(p)
        else:
            acc_ref = refs[n_in + 1]
            k = pl.program_id(2)

            @pl.when(k == 0)
            def _():
                acc_ref[...] = p

            @pl.when(k > 0)
            def _():
                acc_ref[...] += p

            @pl.when(k == nk - 1)
            def _():
                finish(acc_ref[...])

    return pl.pallas_call(
        body, name=name, grid=(M // tm, N // tn, nk), in_specs=in_specs,
        out_specs=pl.BlockSpec((tm, tn), lambda i, j, k: (i, j)),
        out_shape=jax.ShapeDtypeStruct((M, N), out_dtype),
        scratch_shapes=[pltpu.VMEM((tm, tn), F32)] if nk > 1 else [],
        compiler_params=_params(("parallel", "parallel", "arbitrary")),
    )(*args)


def _rowcall(name, body, nrows, tm, ins, outs, scratch=()):
    nblk = nrows // tm
    assert nrows % tm == 0

    def spec(kind, shape):
        k = kind[0]
        if k == "row":
            cw, cb = kind[1] or shape[-1], kind[2]
            return pl.BlockSpec((tm, cw), lambda i: (i, cb))
        if k == "prev":
            hb, cw, cb = kind[1], kind[2] or shape[-1], kind[3]
            r = tm // hb
            return pl.BlockSpec((hb, cw), lambda i: (jnp.maximum(i * r - 1, 0), cb))
        if k == "next":
            hb, cw, cb = kind[1], kind[2] or shape[-1], kind[3]
            r, last = tm // hb, nrows // hb - 1
            return pl.BlockSpec((hb, cw), lambda i: (jnp.minimum((i + 1) * r, last), cb))
        if k == "off":
            off, cw, cb = kind[1], kind[2] or shape[-1], kind[3]
            return pl.BlockSpec((tm, cw), lambda i: (jnp.clip(i + off, 0, nblk - 1), cb))
        if k == "cls":
            dil, off = kind[1], kind[2]
            return pl.BlockSpec((dil, tm // dil, shape[-1]), lambda i: (0, jnp.clip(i + off, 0, nblk - 1), 0))
        nd = len(shape)
        return pl.BlockSpec(tuple(shape), lambda i: (0,) * nd)

    has_acc = any(o[2][0] == "acc" for o in outs)
    return pl.pallas_call(
        body, name=name, grid=(nblk,),
        in_specs=[spec(kind, a.shape) for a, kind in ins],
        out_specs=[spec(kind, shape) for shape, _, kind in outs],
        out_shape=[jax.ShapeDtypeStruct(tuple(shape), dt) for shape, dt, _ in outs],
        scratch_shapes=list(scratch),
        compiler_params=_params(("arbitrary",) if has_acc else ("parallel",)),
    )(*[a for a, _ in ins])


ROW = ("row", None, 0)
FULL = ("full",)
ACC = ("acc",)


def _colsum(x):
    return jnp.sum(x, axis=0, keepdims=True)


def _acc_add(i, ref, val, rows=None):
    idx = (slice(None),) * len(ref.shape) if rows is None else rows

    @pl.when(i == 0)
    def _():
        ref[idx] = val

    @pl.when(i > 0)
    def _():
        ref[idx] = ref[idx] + val


def _sigmoid(x):
    return 1.0 / (1.0 + jnp.exp(-x))


def _gelu(x):
    return 0.5 * x * (1.0 + lax.erf(x * (2.0 ** -0.5)))


def _gelu_grad(x):
    return 0.5 * (1.0 + lax.erf(x * (2.0 ** -0.5))) + x * jnp.exp(-0.5 * x * x) * ((2.0 * jnp.pi) ** -0.5)


SUBLANES = 8


def _phases(ext, sign):
    n = ext.shape[0]
    return [ext if b == 0 else pltpu.roll(ext, b if sign > 0 else n - b, axis=0) for b in range(SUBLANES)]


def _shift_prev(phases, s, hb):
    a, b = divmod(s, SUBLANES)
    return phases[b][hb - SUBLANES * a:phases[b].shape[0] - SUBLANES * a]


def _shift_next(phases, s, tm):
    a, b = divmod(s, SUBLANES)
    return phases[b][SUBLANES * a:SUBLANES * a + tm]


def _rms_mod_val(x, g, sc, sh):
    r = lax.rsqrt(jnp.mean(x * x, axis=-1, keepdims=True) + EPS)
    return x * r * g * (1.0 + sc) + sh


def _mod_first(name, x, g, sc, sh, tm=256):
    S, D = x.shape

    def body(x_ref, g_ref, sc_ref, sh_ref, h_ref):
        h_ref[...] = _rms_mod_val(x_ref[...], g_ref[...], sc_ref[...], sh_ref[...]).astype(BF16)

    return _rowcall(name, body, S, tm, [(x, ROW), (g, FULL), (sc, FULL), (sh, FULL)], [((S, D), BF16, ROW)])[0]


def _resid_mod(name, x, y, gate, g, sc, sh, tm=256):
    S, D = x.shape

    def body(x_ref, y_ref, gate_ref, g_ref, sc_ref, sh_ref, xo_ref, h_ref):
        xn = x_ref[...] + gate_ref[...] * y_ref[...]
        xo_ref[...] = xn
        h_ref[...] = _rms_mod_val(xn, g_ref[...], sc_ref[...], sh_ref[...]).astype(BF16)

    return _rowcall(name, body, S, tm,
                    [(x, ROW), (y, ROW), (gate, FULL), (g, FULL), (sc, FULL), (sh, FULL)],
                    [((S, D), F32, ROW), ((S, D), BF16, ROW)])


def _gate_bwd_val(i, d, y_ref, gate_ref, dy_ref, dg_ref, db_ref):
    dy = d * gate_ref[...]
    dy_ref[...] = dy.astype(BF16)
    _acc_add(i, dg_ref, _colsum(d * y_ref[...]))
    _acc_add(i, db_ref, _colsum(dy))


GATE_OUTS = lambda S, D: [((S, D), BF16, ROW), ((1, D), F32, ACC), ((1, D), F32, ACC)]


def _loss_head(name, x, y, gate, tgt, tm=256):
    S, D = x.shape

    def body(x_ref, y_ref, gate_ref, t_ref, dx_ref, l_ref, dy_ref, dg_ref, db_ref):
        i = pl.program_id(0)
        err = x_ref[...] + gate_ref[...] * y_ref[...] - t_ref[...]
        d = err * (1.0 / D)
        dx_ref[...] = d
        _acc_add(i, l_ref, _colsum(err * err))
        _gate_bwd_val(i, d, y_ref, gate_ref, dy_ref, dg_ref, db_ref)

    return _rowcall(name, body, S, tm, [(x, ROW), (y, ROW), (gate, FULL), (tgt, ROW)],
                    [((S, D), F32, ROW), ((1, D), F32, ACC)] + GATE_OUTS(S, D))


def _mod_bwd(name, dxo, dh, x, g, sc, y=None, gate=None, tm=256):
    S, D = x.shape
    gated = y is not None

    def body(d_ref, dh_ref, x_ref, g_ref, sc_ref, *rest):
        dx_ref, dsh_ref, dsc_ref, dg_ref = rest[2 * gated:2 * gated + 4]
        i = pl.program_id(0)
        xv, dh_v, gv = x_ref[...], dh_ref[...], g_ref[...]
        r = lax.rsqrt(jnp.mean(xv * xv, axis=-1, keepdims=True) + EPS)
        n = xv * r
        _acc_add(i, dsh_ref, _colsum(dh_v))
        _acc_add(i, dsc_ref, _colsum(dh_v * (n * gv)))
        dy = dh_v * (1.0 + sc_ref[...])
        _acc_add(i, dg_ref, _colsum(dy * n))
        dn = dy * gv
        dx = d_ref[...] + r * (dn - n * jnp.mean(dn * n, axis=-1, keepdims=True))
        dx_ref[...] = dx
        if gated:
            _gate_bwd_val(i, dx, rest[0], rest[1], *rest[6:9])

    ins = [(dxo, ROW), (dh, ROW), (x, ROW), (g, FULL), (sc, FULL)] + ([(y, ROW), (gate, FULL)] if gated else [])
    outs = [((S, D), F32, ROW), ((1, D), F32, ACC), ((1, D), F32, ACC), ((1, D), F32, ACC)]
    return _rowcall(name, body, S, tm, ins, outs + (GATE_OUTS(S, D) if gated else []))


HB16 = 16


def _conv3_val(ph, w, b, hb):
    return w[2:3] * _shift_prev(ph, 0, hb) + w[1:2] * _shift_prev(ph, 1, hb) + w[0:1] * _shift_prev(ph, 2, hb) + b


def _halo_first(halo_ref, tile_ref, live):
    return _phases(jnp.concatenate([halo_ref[...].astype(F32) * live, tile_ref[...].astype(F32)], axis=0), 1)


def _glu3_fwd(name, u, w, b, tm=128):
    S, F2 = u.shape
    Fh = F2 // 2

    def body(ua_ref, ub_ref, ha_ref, hb_ref, w_ref, b_ref, o_ref, z_ref):
        live = (pl.program_id(0) > 0).astype(F32)
        wv, bv = w_ref[...], b_ref[...]
        za = _conv3_val(_halo_first(ha_ref, ua_ref, live), wv[:, :Fh], bv[:, :Fh], HB16)
        zb = _conv3_val(_halo_first(hb_ref, ub_ref, live), wv[:, Fh:], bv[:, Fh:], HB16)
        o_ref[...] = (za * _sigmoid(za) * zb).astype(BF16)
        z_ref[:, :Fh] = za.astype(BF16)
        z_ref[:, Fh:] = zb.astype(BF16)

    return _rowcall(name, body, S, tm,
                    [(u, ("row", Fh, 0)), (u, ("row", Fh, 1)), (u, ("prev", HB16, Fh, 0)), (u, ("prev", HB16, Fh, 1)),
                     (w, FULL), (b, FULL)],
                    [((S, Fh), BF16, ROW), ((S, F2), BF16, ROW)])


def _glu3_bwd(name, z, dhm, tm=128):
    S, F2 = z.shape
    Fh = F2 // 2

    def body(za_ref, zb_ref, d_ref, dz_ref, db_ref):
        i = pl.program_id(0)
        za, zb, d = za_ref[...].astype(F32), zb_ref[...].astype(F32), d_ref[...]
        sg = _sigmoid(za)
        da = d * zb * (sg * (1.0 + za * (1.0 - sg)))
        db = d * (za * sg)
        dz_ref[:, :Fh] = da.astype(BF16)
        dz_ref[:, Fh:] = db.astype(BF16)
        _acc_add(i, db_ref, jnp.concatenate([_colsum(da), _colsum(db)], axis=1))

    return _rowcall(name, body, S, tm, [(z, ("row", Fh, 0)), (z, ("row", Fh, 1)), (dhm, ROW)],
                    [((S, F2), BF16, ROW), ((1, F2), F32, ACC)])


def _conv3_bwd(name, dz, u, w, tm=128):
    S, F2 = dz.shape
    nblk = S // tm
    K = w.shape[0]

    def body(d_ref, n_ref, u_ref, w_ref, o_ref, dw_ref):
        i = pl.program_id(0)
        live = (i < nblk - 1).astype(F32)
        ph = _phases(jnp.concatenate([d_ref[...].astype(F32), n_ref[...].astype(F32) * live], axis=0), -1)
        wv, uv = w_ref[...], u_ref[...].astype(F32)
        shifted = [_shift_next(ph, K - 1 - k, tm) for k in range(K)]
        o_ref[...] = functools.reduce(lambda a, t: a + t, [wv[k:k + 1] * shifted[k] for k in range(K)]).astype(BF16)
        for k in range(K):
            _acc_add(i, dw_ref, _colsum(uv * shifted[k]), rows=(slice(k, k + 1), slice(None)))

        @pl.when(i == 0)
        def _():
            dw_ref[K:, :] = jnp.zeros((dw_ref.shape[0] - K, F2), F32)

    return _rowcall(name, body, S, tm, [(dz, ROW), (dz, ("next", HB16, None, 0)), (u, ROW), (w, FULL)],
                    [((S, F2), BF16, ROW), ((SUBLANES, F2), F32, ACC)])


def _ffn_fwd(name, h, w_up, w_dn, dw_w, dw_b):
    u = _matmul(f"{name}_up", h, w_up, "nn", BF16, tm=2048)
    hm, z = _glu3_fwd(f"{name}_glu", u, dw_w, dw_b)
    f = _matmul(f"{name}_dn", hm, w_dn, "nn", F32, tm=1024, tn=1024, tk=w_dn.shape[0])
    return f, (u, hm, z)


def _ffn_bwd(name, dy, h, u, hm, z, w_up, w_dn, dw_w, dw_b):
    Fh = w_dn.shape[0]
    dhm = _matmul(f"{name}_ddn_x", dy, w_dn, "nt", F32, tm=1024, tn=Fh // 2)
    g_dn = _matmul(f"{name}_ddn_w", hm, dy, "tn", BF16, tm=Fh // 2, tn=1024, tk=2048)
    dz, g_dw_b = _glu3_bwd(f"{name}_dglu", z, dhm)
    du, taps = _conv3_bwd(f"{name}_dconv", dz, u, dw_w)
    g_up = _matmul(f"{name}_dup_w", h, du, "tn", BF16, tm=1024, tn=Fh // 2, tk=2048)
    dh = _matmul(f"{name}_dup_x", du, w_up, "nt", F32, tm=1024, tn=1024, tk=Fh)
    return dh, dict(up=g_up, dn=g_dn, dw_w=taps[0:dw_w.shape[0]], dw_b=g_dw_b)


HB32 = 32


def _glu31_fwd(name, p, w, b, tm=256):
    S, D2 = p.shape
    D = D2 // 2
    K = w.shape[0]

    def body(a_ref, g_ref, ha_ref, hg_ref, w_ref, b_ref, o_ref):
        live = (pl.program_id(0) > 0).astype(F32)
        y1 = a_ref[...] * _sigmoid(g_ref[...])
        ph = _phases(jnp.concatenate([ha_ref[...] * _sigmoid(hg_ref[...]) * live, y1], axis=0), 1)
        wv = w_ref[...]
        acc = b_ref[...] + wv[K - 1:K] * y1
        for k in range(K - 1):
            acc = acc + wv[k:k + 1] * _shift_prev(ph, K - 1 - k, HB32)
        o_ref[...] = acc

    return _rowcall(name, body, S, tm,
                    [(p, ("row", D, 0)), (p, ("row", D, 1)), (p, ("prev", HB32, D, 0)), (p, ("prev", HB32, D, 1)),
                     (w, FULL), (b, FULL)],
                    [((S, D), F32, ROW)])[0]


def _ln_silu_fwd(name, y2, g, b, tm=256):
    S, D = y2.shape

    def body(y_ref, g_ref, b_ref, o_ref):
        y = y_ref[...]
        mu = jnp.mean(y, axis=-1, keepdims=True)
        yc = y - mu
        rs = lax.rsqrt(jnp.mean(yc * yc, axis=-1, keepdims=True) + EPS)
        y3 = yc * rs * g_ref[...] + b_ref[...]
        o_ref[...] = (y3 * _sigmoid(y3)).astype(BF16)

    return _rowcall(name, body, S, tm, [(y2, ROW), (g, FULL), (b, FULL)], [((S, D), BF16, ROW)])[0]


def _ln_silu_bwd(name, y2, dy4, g, b, tm=256):
    S, D = y2.shape

    def body(y_ref, d_ref, g_ref, b_ref, o_ref, dg_ref, db_ref):
        i = pl.program_id(0)
        y, gv = y_ref[...], g_ref[...]
        mu = jnp.mean(y, axis=-1, keepdims=True)
        yc = y - mu
        rs = lax.rsqrt(jnp.mean(yc * yc, axis=-1, keepdims=True) + EPS)
        n = yc * rs
        y3 = n * gv + b_ref[...]
        sg = _sigmoid(y3)
        dy3 = d_ref[...] * (sg * (1.0 + y3 * (1.0 - sg)))
        _acc_add(i, db_ref, _colsum(dy3))
        _acc_add(i, dg_ref, _colsum(dy3 * n))
        dn = dy3 * gv
        o_ref[...] = rs * (dn - jnp.mean(dn, axis=-1, keepdims=True) - n * jnp.mean(dn * n, axis=-1, keepdims=True))

    return _rowcall(name, body, S, tm, [(y2, ROW), (dy4, ROW), (g, FULL), (b, FULL)],
                    [((S, D), F32, ROW), ((1, D), F32, ACC), ((1, D), F32, ACC)])


def _glu31_bwd(name, p, dy2, w, tm=256):
    S, D2 = p.shape
    D = D2 // 2
    K = w.shape[0]
    nblk = S // tm

    conv_rows, tap_rows, tap_group, tap_unroll = 16, SUBLANES, 4, 4

    def body(a_ref, g_ref, d_ref, dn_ref, w_ref, dp_ref, dw_ref, dcb_ref, dpb_ref, ph_d, y1_ref, dy1_ref, wb_ref):
        i = pl.program_id(0)
        live_next = (i < nblk - 1).astype(F32)
        a, sg, d = a_ref[...], _sigmoid(g_ref[...]), d_ref[...]
        y1_ref[...] = a * sg
        for b, ph in enumerate(_phases(jnp.concatenate([d, dn_ref[...] * live_next], axis=0), -1)):
            ph_d[b] = ph
        taps = [divmod(K - 1 - k, SUBLANES) for k in range(K)]

        @pl.when(i == 0)
        def _():
            for k in range(K):
                wb_ref[k] = jnp.broadcast_to(w_ref[k:k + 1, :], (SUBLANES, D))

        def conv_rows_at(rb, carry):
            r0 = pl.multiple_of(rb * conv_rows, conv_rows)
            accs = [jnp.zeros((SUBLANES, D), F32) for _ in range(conv_rows // SUBLANES)]
            for k, (rows8, phase) in enumerate(taps):
                wk = wb_ref[k]
                for u in range(len(accs)):
                    accs[u] = accs[u] + wk * ph_d[phase, pl.ds(r0 + SUBLANES * (rows8 + u), SUBLANES), :]
            for u, acc in enumerate(accs):
                dy1_ref[pl.ds(r0 + SUBLANES * u, SUBLANES), :] = acc
            return carry

        lax.fori_loop(0, tm // conv_rows, conv_rows_at, 0)
        for k0 in range(0, K, tap_group):
            group = taps[k0:k0 + tap_group]

            def tap_rows_at(rb, accs, group=group):
                for u in range(tap_unroll):
                    r0 = pl.multiple_of((rb * tap_unroll + u) * tap_rows, tap_rows)
                    yv = y1_ref[pl.ds(r0, tap_rows), :]
                    accs = tuple(acc + yv * ph_d[phase, pl.ds(r0 + SUBLANES * rows8, tap_rows), :]
                                 for acc, (rows8, phase) in zip(accs, group))
                return accs

            accs = lax.fori_loop(0, tm // (tap_rows * tap_unroll), tap_rows_at,
                                 tuple(jnp.zeros((tap_rows, D), F32) for _ in group))
            for j, acc in enumerate(accs):
                _acc_add(i, dw_ref, _colsum(acc), rows=(slice(k0 + j, k0 + j + 1), slice(None)))

        @pl.when(i == 0)
        def _():
            dw_ref[K:, :] = jnp.zeros((dw_ref.shape[0] - K, D), F32)

        _acc_add(i, dcb_ref, _colsum(d))
        dy1 = dy1_ref[...]
        da = dy1 * sg
        dg = dy1 * a * sg * (1.0 - sg)
        dp_ref[:, :D] = da.astype(BF16)
        dp_ref[:, D:] = dg.astype(BF16)
        _acc_add(i, dpb_ref, jnp.concatenate([_colsum(da), _colsum(dg)], axis=1))

    return _rowcall(name, body, S, tm,
                    [(p, ("row", D, 0)), (p, ("row", D, 1)), (dy2, ROW), (dy2, ("next", HB32, None, 0)), (w, FULL)],
                    [((S, D2), BF16, ROW), ((HB32, D), F32, ACC), ((1, D), F32, ACC), ((1, D2), F32, ACC)],
                    scratch=[pltpu.VMEM((SUBLANES, tm + HB32, D), F32), pltpu.VMEM((tm, D), F32),
                             pltpu.VMEM((tm, D), F32), pltpu.VMEM((K, SUBLANES, D), F32)])


CHUNK = 128
A_GROUPS = 4


def _group_ln(gv):
    ns, rss = [], []
    for g in range(A_GROUPS):
        xg = gv[:, g * LANES:(g + 1) * LANES]
        xc = xg - jnp.mean(xg, axis=-1, keepdims=True)
        rs = lax.rsqrt(jnp.mean(xc * xc, axis=-1, keepdims=True) + EPS)
        ns.append(xc * rs)
        rss.append(jnp.broadcast_to(rs, xg.shape))
    return jnp.concatenate(ns, axis=1), jnp.concatenate(rss, axis=1)


def _tril_mask():
    r = lax.broadcasted_iota(jnp.int32, (CHUNK, CHUNK), 0)
    c = lax.broadcasted_iota(jnp.int32, (CHUNK, CHUNK), 1)
    return r >= c


def _spatial(ws_ref, x, dn):
    mask = _tril_mask()
    rows = []
    for ci in range(x.shape[0] // CHUNK):
        cols = []
        for g in range(A_GROUPS):
            wm = jnp.where(mask, ws_ref[g], 0.0).astype(BF16)
            xb = x[ci * CHUNK:(ci + 1) * CHUNK, g * LANES:(g + 1) * LANES]
            cols.append(lax.dot_general(wm, xb, dn, preferred_element_type=F32))
        rows.append(jnp.concatenate(cols, axis=1))
    return jnp.concatenate(rows, axis=0)


def _mixa_fwd(name, z, vg, ws, bias_full, tm=256):
    S = z.shape[0]
    W = A_GROUPS * LANES

    def body(u_ref, v_ref, vg_ref, ws_ref, b_ref, o_ref):
        nh, _ = _group_ln(_gelu(v_ref[...]))
        vn = (nh * vg_ref[...]).astype(BF16)
        f = _spatial(ws_ref, vn, _DN["nn"]) + jnp.concatenate([b_ref[...]] * (tm // CHUNK), axis=0)
        o_ref[...] = (_gelu(u_ref[...]) * f).astype(BF16)

    return _rowcall(name, body, S, tm,
                    [(z, ("row", W, 0)), (z, ("row", W, 1)), (vg, FULL), (ws, FULL), (bias_full, FULL)],
                    [((S, W), BF16, ROW)])[0]


def _mixa_bwd(name, z, dyab, vg, ws, bias_full, tm=256):
    S = z.shape[0]
    W = A_GROUPS * LANES
    nch = tm // CHUNK

    def body(u_ref, v_ref, d_ref, vg_ref, ws_ref, b_ref, dz_ref, dws_ref, dbf_ref, dvg_ref):
        i = pl.program_id(0)
        u, v, d, vgv = u_ref[...], v_ref[...], d_ref[...], vg_ref[...]
        nh, rs = _group_ln(_gelu(v))
        vn = (nh * vgv).astype(BF16)
        f = _spatial(ws_ref, vn, _DN["nn"]) + jnp.concatenate([b_ref[...]] * nch, axis=0)
        dz_ref[:, :W] = (d * f * _gelu_grad(u)).astype(BF16)
        df = d * _gelu(u)
        dbf = df[0:CHUNK]
        for ci in range(1, nch):
            dbf = dbf + df[ci * CHUNK:(ci + 1) * CHUNK]
        _acc_add(i, dbf_ref, dbf)
        dfb = df.astype(BF16)
        mask = _tril_mask()
        for g in range(A_GROUPS):
            acc = jnp.zeros((CHUNK, CHUNK), F32)
            for ci in range(nch):
                blk = (slice(ci * CHUNK, (ci + 1) * CHUNK), slice(g * LANES, (g + 1) * LANES))
                acc = acc + lax.dot_general(dfb[blk], vn[blk], _DN["nt"], preferred_element_type=F32)
            _acc_add(i, dws_ref, jnp.where(mask, acc, 0.0)[None], rows=(slice(g, g + 1), slice(None), slice(None)))
        dvn = _spatial(ws_ref, dfb, _DN["tn"])
        _acc_add(i, dvg_ref, _colsum(dvn * nh))
        dnh = dvn * vgv
        parts = []
        for g in range(A_GROUPS):
            cs = slice(g * LANES, (g + 1) * LANES)
            dg_, ng = dnh[:, cs], nh[:, cs]
            parts.append(dg_ - jnp.mean(dg_, axis=-1, keepdims=True) - ng * jnp.mean(dg_ * ng, axis=-1, keepdims=True))
        dz_ref[:, W:] = (rs * jnp.concatenate(parts, axis=1) * _gelu_grad(v)).astype(BF16)

    return _rowcall(name, body, S, tm,
                    [(z, ("row", W, 0)), (z, ("row", W, 1)), (dyab, ("row", W, 0)), (vg, FULL), (ws, FULL),
                     (bias_full, FULL)],
                    [((S, 2 * W), BF16, ROW), ((A_GROUPS, CHUNK, CHUNK), F32, ACC), ((CHUNK, W), F32, ACC),
                     ((1, W), F32, ACC)])


HEAD = 64
N_HEADS = 8
BW = HEAD * N_HEADS
QB = 128
DILATIONS = (1, 4, 16)
QK_SCALE = HEAD ** -0.5


def _gsum64(x, ones_bd):
    x1 = x.astype(BF16)
    r1 = x - x1.astype(F32)
    x2 = r1.astype(BF16)
    x3 = (r1 - x2.astype(F32)).astype(BF16)
    dot = lambda t: jnp.dot(t, ones_bd, preferred_element_type=F32)
    return dot(x1) + dot(x2) + dot(x3)


def _swap32(x):
    n = x.shape[-1]
    up = pltpu.roll(x, n - HEAD // 2, axis=1)
    dn = pltpu.roll(x, HEAD // 2, axis=1)
    lane = lax.broadcasted_iota(jnp.int32, x.shape, 1)
    return jnp.where((lane % HEAD) < HEAD // 2, up, dn)


def _tile4(t):
    return jnp.concatenate([t] * (BW // LANES), axis=1)


def _stage_spec(tm):
    return pltpu.VMEM((BW // LANES, tm, LANES), F32)


def _to_classes(stage, x, dil):
    tm = x.shape[0]
    for j in range(BW // LANES):
        stage[j] = x[:, j * LANES:(j + 1) * LANES]
    return [jnp.concatenate([stage.at[j][pl.ds(r, tm // dil, stride=dil), :] for j in range(BW // LANES)], axis=1)
            for r in range(dil)]


def _from_classes(stage, cls, dil):
    rows = cls.shape[1]
    for r in range(dil):
        for j in range(BW // LANES):
            stage.at[j][pl.ds(r, rows, stride=dil), :] = cls[r, :, j * LANES:(j + 1) * LANES]
    return jnp.concatenate([stage[j] for j in range(BW // LANES)], axis=1)


def _cls_view(t, dil):
    return t if dil == 1 else t.reshape(dil, t.shape[0] // dil, t.shape[1])


def _cls_kind(dil, off=0):
    return ("off", off, None, 0) if dil == 1 else ("cls", dil, off)


def _cls_out(S, dil, dtype):
    return ((S, BW) if dil == 1 else (dil, S // dil, BW), dtype, _cls_kind(dil))


def _flat(t):
    return t.reshape(-1, t.shape[-1])


def _qkv_fwd(name, z, cos, sin, ones_bd, qg, kg, tm=256):
    S = z.shape[0]
    nd = len(DILATIONS)

    def body(q_ref, k_ref, v_ref, c_ref, s_ref, o_ref, qg_ref, kg_ref, *rest):
        outs, stage = rest[:3 * nd], rest[3 * nd]
        c, s, ob = _tile4(c_ref[...]), _tile4(s_ref[...]), o_ref[...]

        def norm_rope(x, g):
            r = lax.rsqrt(_gsum64(x * x, ob) * (1.0 / HEAD) + EPS)
            xn = x * r * g
            return xn * c + _swap32(xn) * s

        vals = [norm_rope(q_ref[...], qg_ref[...]) * QK_SCALE, norm_rope(k_ref[...], kg_ref[...]), v_ref[...]]
        for a, val in enumerate(vals):
            for b, dil in enumerate(DILATIONS):
                if dil == 1:
                    outs[nd * a + b][...] = val.astype(BF16)
                else:
                    for r, rows in enumerate(_to_classes(stage, val, dil)):
                        outs[nd * a + b][r] = rows.astype(BF16)

    outs = _rowcall(name, body, S, tm,
                    [(z, ("row", BW, 2)), (z, ("row", BW, 3)), (z, ("row", BW, 4)), (cos, ROW), (sin, ROW),
                     (ones_bd, FULL), (qg, FULL), (kg, FULL)],
                    [_cls_out(S, dil, BF16) for _ in range(3) for dil in DILATIONS], scratch=[_stage_spec(tm)])
    return [[_flat(outs[nd * a + b]) for a in range(3)] for b in range(nd)]


PAIR = 2 * HEAD


ATT_BLOCKS = 2
ATT_TM = ATT_BLOCKS * QB
ATT_PREV = ("prev", QB, None, 0)


def _key_rows(prev_ref, cur_ref, sb, ps):
    before = prev_ref[:, ps] if sb == 0 else cur_ref[(sb - 1) * QB:sb * QB, ps]
    return jnp.concatenate([before, cur_ref[sb * QB:(sb + 1) * QB, ps]], axis=0)


def _pair_scores(q_ref, kp_ref, kc_ref, sb, hp, half, seg_blocks):
    ps = slice(hp * PAIR, (hp + 1) * PAIR)
    mine = (lax.broadcasted_iota(jnp.int32, (1, PAIR), 1) >= HEAD) == (half == 1)
    qm = jnp.where(mine, q_ref[sb * QB:(sb + 1) * QB, ps], jnp.zeros((), BF16))
    kcat = _key_rows(kp_ref, kc_ref, sb, ps)
    s = lax.dot_general(qm, kcat, _DN["nt"], preferred_element_type=F32)
    qi = lax.broadcasted_iota(jnp.int32, (QB, 2 * QB), 0)
    kj = lax.broadcasted_iota(jnp.int32, (QB, 2 * QB), 1)
    has_prev = ((pl.program_id(0) * ATT_BLOCKS + sb) % seg_blocks) != 0
    valid = (kj >= qi) & (kj <= qi + QB) & ((kj >= QB) | has_prev)
    return mine, qm, kcat, s, valid


def _attn_fwd(name, q, k, v, dil):
    S = q.shape[0]
    seg_blocks = S // dil // QB

    def body(q_ref, kp_ref, kc_ref, vp_ref, vc_ref, o_ref, l_ref):
        for hp in range(N_HEADS // 2):
            ps = slice(hp * PAIR, (hp + 1) * PAIR)
            chains = [(sb, half) for sb in range(ATT_BLOCKS) for half in range(2)]
            sc = [_pair_scores(q_ref, kp_ref, kc_ref, sb, hp, half, seg_blocks) for sb, half in chains]
            ss = [jnp.where(valid, s, NEG) for _, _, _, s, valid in sc]
            ms = [jnp.max(s, axis=-1, keepdims=True) for s in ss]
            pv = [jnp.exp(s - m) for s, m in zip(ss, ms)]
            dens = [jnp.sum(p, axis=-1, keepdims=True) for p in pv]
            vcats = [_key_rows(vp_ref, vc_ref, sb, ps) for sb in range(ATT_BLOCKS)]
            outs = [jnp.dot(p.astype(BF16), vcats[sb], preferred_element_type=F32) / den
                    for p, den, (sb, _) in zip(pv, dens, chains)]
            lses = [jnp.broadcast_to(m + jnp.log(den), (QB, PAIR)) for m, den in zip(ms, dens)]
            for sb in range(ATT_BLOCKS):
                rows, upper = slice(sb * QB, (sb + 1) * QB), sc[2 * sb + 1][0]
                o_ref[rows, ps] = jnp.where(upper, outs[2 * sb + 1], outs[2 * sb])
                l_ref[rows, ps] = jnp.where(upper, lses[2 * sb + 1], lses[2 * sb])

    return _rowcall(name, body, S, ATT_TM, [(q, ROW), (k, ATT_PREV), (k, ROW), (v, ATT_PREV), (v, ROW)],
                    [((S, BW), F32, ROW)] * 2)


def _attn_bwd(name, q, k, v, do, lse, delta, dil):
    S = q.shape[0]
    seg_blocks = S // dil // QB

    def body(q_ref, kp_ref, kc_ref, vp_ref, vc_ref, do_ref, l_ref, dl_ref, dq_ref, dkc_ref, dkp_ref, dvc_ref, dvp_ref):
        for hp in range(N_HEADS // 2):
            ps = slice(hp * PAIR, (hp + 1) * PAIR)
            chains = [(sb, half) for sb in range(ATT_BLOCKS) for half in range(2)]
            rows = [slice(sb * QB, (sb + 1) * QB) for sb, _ in chains]
            cols = [hp * PAIR + half * HEAD for _, half in chains]
            sc = [_pair_scores(q_ref, kp_ref, kc_ref, sb, hp, half, seg_blocks) for sb, half in chains]
            pv = [jnp.where(valid, jnp.exp(s - l_ref[r, c:c + 1]), 0.0) for (_, _, _, s, valid), r, c in zip(sc, rows, cols)]
            vcats = [_key_rows(vp_ref, vc_ref, sb, ps) for sb in range(ATT_BLOCKS)]
            doms = [jnp.where(mine, do_ref[r, ps].astype(BF16), jnp.zeros((), BF16)) for (mine, *_), r in zip(sc, rows)]
            dps = [lax.dot_general(dom, vcats[sb], _DN["nt"], preferred_element_type=F32) for dom, (sb, _) in zip(doms, chains)]
            dss = [(p * (dp - dl_ref[r, c:c + 1])).astype(BF16) for p, dp, r, c in zip(pv, dps, rows, cols)]
            dqs = [jnp.dot(ds, kcat, preferred_element_type=F32) for ds, (_, _, kcat, _, _) in zip(dss, sc)]
            dks = [lax.dot_general(ds, qm, _DN["tn"], preferred_element_type=F32) for ds, (_, qm, *_) in zip(dss, sc)]
            dvs = [lax.dot_general(p.astype(BF16), dom, _DN["tn"], preferred_element_type=F32) for p, dom in zip(pv, doms)]
            for sb in range(ATT_BLOCKS):
                lo, hi = 2 * sb, 2 * sb + 1
                dk, dv = dks[lo] + dks[hi], dvs[lo] + dvs[hi]
                dq_ref[rows[lo], ps] = jnp.where(sc[hi][0], dqs[hi], dqs[lo])
                dkp_ref[rows[lo], ps] = dk[:QB]
                dkc_ref[rows[lo], ps] = dk[QB:]
                dvp_ref[rows[lo], ps] = dv[:QB]
                dvc_ref[rows[lo], ps] = dv[QB:]

    return _rowcall(name, body, S, ATT_TM,
                    [(q, ROW), (k, ATT_PREV), (k, ROW), (v, ATT_PREV), (v, ROW), do, (lse, ROW), (delta, ROW)],
                    [((S, BW), F32, ROW)] * 5)


def _merge_fwd(name, branches, tm=256):
    S = branches[0][0].shape[0]
    nd = len(DILATIONS)

    def body(*refs):
        ins, (y_ref, yb_ref), l_refs, stage = refs[:2 * nd], refs[2 * nd:2 * nd + 2], refs[2 * nd + 2:3 * nd + 2], refs[-1]
        os_, ls = [], []
        for b, dil in enumerate(DILATIONS):
            o, l = ins[2 * b][...], ins[2 * b + 1][...]
            os_.append(o if dil == 1 else _from_classes(stage, o, dil))
            ls.append(l if dil == 1 else _from_classes(stage, l, dil))
        m = functools.reduce(jnp.maximum, ls)
        es = [jnp.exp(l - m) for l in ls]
        den = functools.reduce(lambda a, e: a + e, es)
        y = functools.reduce(lambda a, t: a + t, [e * o for e, o in zip(es, os_)]) / den
        y_ref[...] = y
        yb_ref[...] = y.astype(BF16)
        lse = m + jnp.log(den)
        for b, dil in enumerate(DILATIONS):
            if dil == 1:
                l_refs[b][...] = lse
            else:
                for r, rows in enumerate(_to_classes(stage, lse, dil)):
                    l_refs[b][r] = rows

    ins = [(_cls_view(t, dil), _cls_kind(dil)) for pair, dil in zip(branches, DILATIONS) for t in pair]
    outs = _rowcall(name, body, S, tm, ins,
                    [((S, BW), F32, ROW), ((S, BW), BF16, ROW)] + [_cls_out(S, dil, F32) for dil in DILATIONS],
                    scratch=[_stage_spec(tm)])
    return outs[0], outs[1], [_flat(t) for t in outs[2:]]


def _delta(name, dyab, yb, ones_bd, tm=256):
    S = yb.shape[0]
    nd = len(DILATIONS)

    def body(d_ref, y_ref, o_ref, *rest):
        dl_refs, do_refs, stage = rest[:nd], rest[nd:2 * nd - 1], rest[-1]
        d = d_ref[...]
        dl = _gsum64(d * y_ref[...], o_ref[...])
        for b, dil in enumerate(DILATIONS):
            if dil == 1:
                dl_refs[b][...] = dl
            else:
                for r, rows in enumerate(_to_classes(stage, dl, dil)):
                    dl_refs[b][r] = rows
                for r, rows in enumerate(_to_classes(stage, d, dil)):
                    do_refs[b - 1][r] = rows.astype(BF16)

    outs = _rowcall(name, body, S, tm, [(dyab, ("row", BW, 1)), (yb, ROW), (ones_bd, FULL)],
                    [_cls_out(S, dil, F32) for dil in DILATIONS] + [_cls_out(S, dil, BF16) for dil in DILATIONS[1:]],
                    scratch=[_stage_spec(tm)])
    return [_flat(t) for t in outs[:nd]], [_flat(t) for t in outs[nd:]]


def _qkv_bwd(name, z, cos, sin, ones_bd, qg, kg, pieces):
    S = z.shape[0]
    nblk = S // QB

    def body(q_ref, k_ref, c_ref, s_ref, o_ref, qg_ref, kg_ref, *rest):
        pr, (dz_ref, dqg_ref, dkg_ref), stage = rest[:15], rest[15:18], rest[18]
        i = pl.program_id(0)
        c, s, ob = _tile4(c_ref[...]), _tile4(s_ref[...]), o_ref[...]
        dq = dk = dv = None
        for b, dil in enumerate(DILATIONS):
            a_q, a_kc, a_kp, a_vc, a_vp = [r[...] for r in pr[5 * b:5 * b + 5]]
            live = ((i + dil) < nblk).astype(F32)
            tq, tk, tv = a_q, a_kc + a_kp * live, a_vc + a_vp * live
            if dil > 1:
                tq, tk, tv = (_from_classes(stage, t, dil) for t in (tq, tk, tv))
            dq, dk, dv = (tq, tk, tv) if b == 0 else (dq + tq, dk + tk, dv + tv)

        def back(x, g, d_rot, acc_ref):
            r = lax.rsqrt(_gsum64(x * x, ob) * (1.0 / HEAD) + EPS)
            n = x * r
            dxn = d_rot * c + _swap32(d_rot * s)
            _acc_add(i, acc_ref, _colsum(dxn * n))
            dn = dxn * g
            return r * (dn - n * (_gsum64(dn * n, ob) * (1.0 / HEAD)))

        dz_ref[:, :BW] = back(q_ref[...], qg_ref[...], dq * QK_SCALE, dqg_ref).astype(BF16)
        dz_ref[:, BW:2 * BW] = back(k_ref[...], kg_ref[...], dk, dkg_ref).astype(BF16)
        dz_ref[:, 2 * BW:] = dv.astype(BF16)

    ins = [(z, ("row", BW, 2)), (z, ("row", BW, 3)), (cos, ROW), (sin, ROW), (ones_bd, FULL), (qg, FULL), (kg, FULL)]
    for piece, dil in zip(pieces, DILATIONS):
        a_q, a_kc, a_kp, a_vc, a_vp = (_cls_view(t, dil) for t in piece)
        own, prev = _cls_kind(dil), _cls_kind(dil, dil)
        ins += [(a_q, own), (a_kc, own), (a_kp, prev), (a_vc, own), (a_vp, prev)]
    return _rowcall(name, body, S, QB, ins,
                    [((S, 3 * BW), BF16, ROW), ((1, BW), F32, ACC), ((1, BW), F32, ACC)], scratch=[_stage_spec(QB)])


def _local_step(x0, tgt, pos, mod, wb, sp, pipe):
    S, D = x0.shape
    md = lambda l, j: mod[l, j:j + 1]
    sh_m, sc_m, g_m, sh_f, sc_f, g_f = ([md(l, j) for l in range(2)] for j in range(6))
    nm_g, nf_g = sp["norm_mix_g"], sp["norm_ffn_g"]

    inv_freq = 1.0 / (ROPE_THETA ** (jnp.arange(0, HEAD, 2, dtype=F32) / HEAD))
    ang = pos.astype(F32)[:, None] * inv_freq
    cs, sn = jnp.cos(ang), jnp.sin(ang)
    cos = jnp.concatenate([cs, cs, cs, cs], axis=1)
    sin = jnp.concatenate([-sn, sn, -sn, sn], axis=1)
    head_of = jnp.arange(BW) // HEAD
    ones_bd = (head_of[:, None] == head_of[None, :]).astype(BF16)
    qg = jnp.tile(sp["b_q_norm_g"].reshape(1, HEAD), (1, N_HEADS))
    kg = jnp.tile(sp["b_k_norm_g"].reshape(1, HEAD), (1, N_HEADS))
    vg = sp["a_vnorm_g"].reshape(1, A_GROUPS * LANES)
    ws = sp["a_spatial_w"][0]
    bias_full = jnp.repeat(sp["a_spatial_b"][0].T, LANES, axis=1)
    ffn_s = [(sp["ffn_dw_w"][l], sp["ffn_dw_b"][l:l + 1]) for l in range(2)]

    h0 = _mod_first("l0_mod", x0, nm_g[0:1], sc_m[0], sh_m[0])
    z = _matmul("l0_in", h0, wb.get("w_in", h0), "nn", F32, tm=2048)
    ya = _mixa_fwd("l0_mixa", z, vg, ws, bias_full)
    qkv = _qkv_fwd("l0_qkv", z, cos, sin, ones_bd, qg, kg)
    branches = [_attn_fwd(f"l0_att{dil}", *qkv[b], dil) for b, dil in enumerate(DILATIONS)]
    yb, yb16, lses = _merge_fwd("l0_merge", branches)
    yab = jnp.concatenate([ya, yb16], axis=1)
    y0 = _matmul("l0_out", yab, wb.get("w_out", yab), "nn", F32, tm=1024, tn=1024)
    x1, h1 = _resid_mod("l0_res1", x0, y0, g_m[0], nf_g[0:1], sc_f[0], sh_f[0])
    ffn_w = [(wb.get("up0", h1), wb.get("dn0", h1), *ffn_s[0])]
    f0, saved0 = _ffn_fwd("l0_ffn", h1, *ffn_w[0])
    x2, h2 = _resid_mod("l0_res2", x1, f0, g_f[0], nm_g[1:2], sc_m[1], sh_m[1])
    p = _matmul("l1_pw1", h2, wb.get("pw1", h2), "nn", F32, tm=2048, bias=sp["conv_pw1_b"])
    y2 = _glu31_fwd("l1_glu", p, sp["conv_dw_w"][0], sp["conv_dw_b"])
    y4 = _ln_silu_fwd("l1_ln", y2, sp["conv_ln_g"], sp["conv_ln_b"])
    y1 = _matmul("l1_pw2", y4, wb.get("pw2", y4), "nn", F32, tm=1024, tn=1024, bias=sp["conv_pw2_b"])
    x3, h3 = _resid_mod("l1_res1", x2, y1, g_m[1], nf_g[1:2], sc_f[1], sh_f[1])
    ffn_w.append((wb.get("up1", h3), wb.get("dn1", h3), *ffn_s[1]))
    f1, saved1 = _ffn_fwd("l1_ffn", h3, *ffn_w[1])
    dx4, lossv, dy, dgate_f1, _ = _loss_head("loss", x3, f1, g_f[1], tgt)

    dh, gf1 = _ffn_bwd("l1_ffn", dy, h3, *saved1, *ffn_w[1])
    tok = pipe.scatter("g1", dict(dn1=gf1["dn"], up1=gf1["up"]))
    dx3, dsh_f1, dsc_f1, dnf1, dy, dgate_m1, dpw2_b = _mod_bwd("l1_dmod2", dx4, dh, x3, nf_g[1:2], sc_f[1] + tok,
                                                             y1, g_m[1])
    dy4 = _matmul("l1_dpw2_x", dy, wb.get("pw2"), "nt", F32, tm=1024, tn=1024)
    g_pw2 = _matmul("l1_dpw2_w", y4, dy, "tn", BF16, tm=1024, tn=1024, tk=2048)
    dy2, dln_g, dln_b = _ln_silu_bwd("l1_dln", y2, dy4, sp["conv_ln_g"], sp["conv_ln_b"])
    dp, ddw_w, ddw_b, dpw1_b = _glu31_bwd("l1_dglu", p, dy2, sp["conv_dw_w"][0])
    g_pw1 = _matmul("l1_dpw1_w", h2, dp, "tn", BF16, tm=1024, tn=1024, tk=2048)
    tok = pipe.scatter("g2", dict(pw2=g_pw2, pw1=g_pw1))
    pipe.collect("g1", g_pw1)
    dh = _matmul("l1_dpw1_x", dp, wb.get("pw1"), "nt", F32, tm=1024, tn=1024, tk=2048)
    dx2, dsh_m1, dsc_m1, dnm1, dy, dgate_f0, _ = _mod_bwd("l1_dmod1", dx3, dh, x2, nm_g[1:2], sc_m[1] + tok,
                                                        f0, g_f[0])
    dh, gf0 = _ffn_bwd("l0_ffn", dy, h1, *saved0, *ffn_w[0])
    tok = pipe.scatter("g3", dict(dn0=gf0["dn"], up0=gf0["up"]))
    pipe.collect("g2", gf0["up"])
    dx1, dsh_f0, dsc_f0, dnf0, dy, dgate_m0, _ = _mod_bwd("l0_dmod2", dx2, dh, x1, nf_g[0:1], sc_f[0] + tok,
                                                        y0, g_m[0])
    dyab = _matmul("l0_dout_x", dy, wb.get("w_out"), "nt", F32, tm=1024, tn=1024)
    g_out = _matmul("l0_dout_w", yab, dy, "tn", BF16, tm=1024, tn=1024, tk=2048)
    vg = vg + pipe.scatter("g4", dict(w_out=g_out))
    dza, dws, dbf, dvg = _mixa_bwd("l0_dmixa", z, dyab, vg, ws, bias_full)
    deltas, dos = _delta("l0_delta", dyab, yb, ones_bd)
    pieces = []
    for b, dil in enumerate(DILATIONS):
        do = (dyab, ("row", BW, 1)) if dil == 1 else (dos[b - 1], ROW)
        pieces.append(_attn_bwd(f"l0_datt{dil}", *qkv[b], do, lses[b], deltas[b], dil))
    dzb, dqg, dkg = _qkv_bwd("l0_dqkv", z, cos, sin, ones_bd, qg, kg, pieces)
    small_tok = _start_small_grads(pipe, dzb, sp, lossv, ((dgate_m0, dsh_f0, dsc_f0, dgate_f0),
                                                   (dsh_m1, dsc_m1, dgate_m1, dsh_f1, dsc_f1, dgate_f1)),
                                   (dnm1, dnf0, dnf1), dvg, dws, dbf, dqg, dkg, dpw1_b, ddw_w, ddw_b, dln_g, dln_b,
                                   dpw2_b, gf0, gf1)
    dz = jnp.concatenate([dza, dzb], axis=1)
    g_in = _matmul("l0_din_w", h0, dz, "tn", BF16, tm=1024, tn=1280, tk=2048, after=small_tok)
    tok = pipe.scatter("g5", dict(w_in=g_in))
    dh = _matmul("l0_din_x", dz, wb.get("w_in"), "nt", F32, tm=1024, tn=1024, tk=2560, after=small_tok)
    grad_x, dsh_m0, dsc_m0, dnm0 = _mod_bwd("l0_dmod1", dx1, dh, x0, nm_g[0:1], sc_m[0] + tok)
    return lossv, grad_x, (dsh_m0, dsc_m0, dnm0)


def _start_small_grads(pipe, after, sp, lossv, mods, norms, dvg, dws, dbf, dqg, dkg, dpw1_b, ddw_w, ddw_b, dln_g,
                       dln_b, dpw2_b, gf0, gf1):
    (dgate_m0, dsh_f0, dsc_f0, dgate_f0), (dsh_m1, dsc_m1, dgate_m1, dsh_f1, dsc_f1, dgate_f1) = mods
    dnm1, dnf0, dnf1 = norms
    zero = jnp.zeros_like(dnm1)
    dmod = jnp.stack([jnp.concatenate([zero, zero, dgate_m0, dsh_f0, dsc_f0, dgate_f0], axis=0),
                      jnp.concatenate([dsh_m1, dsc_m1, dgate_m1, dsh_f1, dsc_f1, dgate_f1], axis=0)])
    small = dict(
        norm_mix_g=jnp.concatenate([zero, dnm1], axis=0),
        norm_ffn_g=jnp.concatenate([dnf0, dnf1], axis=0),
        a_vnorm_g=dvg.reshape(1, A_GROUPS, LANES),
        a_spatial_w=dws[None],
        a_spatial_b=dbf.reshape(CHUNK, A_GROUPS, LANES).sum(-1).T[None],
        b_q_norm_g=dqg.reshape(N_HEADS, HEAD).sum(0)[None],
        b_k_norm_g=dkg.reshape(N_HEADS, HEAD).sum(0)[None],
        conv_pw1_b=dpw1_b, conv_dw_w=ddw_w[None, :sp["conv_dw_w"].shape[1]], conv_dw_b=ddw_b,
        conv_ln_g=dln_g, conv_ln_b=dln_b, conv_pw2_b=dpw2_b,
        ffn_dw_w=jnp.stack([gf0["dw_w"], gf1["dw_w"]]),
        ffn_dw_b=jnp.concatenate([gf0["dw_b"], gf1["dw_b"]], axis=0),
    )
    return pipe.start_small(dmod, lossv, small, after)


ADA_TN = 512


def _ada_fwd(name, c_all, ada_w, ada_b_sh):
    L, D, N = ada_w.shape
    B = c_all.shape[0]

    def body(c_ref, w_ref, b_ref, o_ref):
        cv = c_ref[...]
        ca = (cv * _sigmoid(cv)).astype(BF16)
        o_ref[0] = jnp.dot(ca, w_ref[0].astype(BF16), preferred_element_type=F32) + b_ref[0]

    return pl.pallas_call(
        body, name=name, grid=(L, N // ADA_TN),
        in_specs=[pl.BlockSpec((B, D), lambda l, j: (0, 0)), pl.BlockSpec((1, D, ADA_TN), lambda l, j: (l, 0, j)),
                  pl.BlockSpec((1, 1, ADA_TN), lambda l, j: (l, 0, j))],
        out_specs=pl.BlockSpec((1, B, ADA_TN), lambda l, j: (l, 0, j)),
        out_shape=jax.ShapeDtypeStruct((L, B, N), F32),
        compiler_params=_params(("parallel", "parallel")),
    )(c_all, ada_w, ada_b_sh.reshape(L, 1, N))


def _adamw_val(w, g, m, v):
    m2 = ADAM_B1 * m + (1.0 - ADAM_B1) * g
    v2 = ADAM_B2 * v + (1.0 - ADAM_B2) * (g * g)
    m_hat = m2 / (1.0 - ADAM_B1 ** ADAM_STEP)
    v_hat = v2 / (1.0 - ADAM_B2 ** ADAM_STEP)
    delta = -ADAM_LR * (m_hat / (jnp.sqrt(v_hat) + ADAM_EPS) + ADAM_WD * w)
    return delta, m2, v2


def _ada_update(name, c_all, dmod_sh, w, m, v):
    L, D, N = w.shape
    B = c_all.shape[0]

    def body(c_ref, d_ref, w_ref, m_ref, v_ref, g_ref, dl_ref, mo_ref, vo_ref):
        cv = c_ref[...]
        ca = (cv * _sigmoid(cv)).astype(BF16)
        g = lax.dot_general(ca, d_ref[0].astype(BF16), _DN["tn"], preferred_element_type=F32)
        g_ref[0] = g
        dl_ref[0], mo_ref[0], vo_ref[0] = _adamw_val(w_ref[0], g, m_ref[0], v_ref[0])

    wspec = pl.BlockSpec((1, D, ADA_TN), lambda l, j: (l, 0, j))
    return pl.pallas_call(
        body, name=name, grid=(L, N // ADA_TN),
        in_specs=[pl.BlockSpec((B, D), lambda l, j: (0, 0)), pl.BlockSpec((1, B, ADA_TN), lambda l, j: (l, 0, j)),
                  wspec, wspec, wspec],
        out_specs=[wspec] * 4, out_shape=[jax.ShapeDtypeStruct((L, D, N), F32)] * 4,
        compiler_params=_params(("parallel", "parallel")),
    )(c_all, dmod_sh, w, m, v)


def _adamw(name, w, g, m, v):
    R, C = w.shape
    tm = R
    for cand in (256, 128, 64, 32, 16, 8):
        if R % cand == 0 and cand * C * 4 <= (1 << 20):
            tm = cand
            break

    def body(w_ref, g_ref, m_ref, v_ref, d_ref, mo_ref, vo_ref):
        d_ref[...], mo_ref[...], vo_ref[...] = _adamw_val(w_ref[...], g_ref[...], m_ref[...], v_ref[...])

    return _rowcall(name, body, R, tm, [(w, ROW), (g, ROW), (m, ROW), (v, ROW)], [((R, C), F32, ROW)] * 3)


def _row_tile(rows, width, itemsize=4, limit=1 << 20):
    for cand in (512, 256, 128, 64, 32, 16):
        if rows % cand == 0 and cand * width * itemsize <= limit:
            return cand
    raise ValueError((rows, width))


def _cast_into_full(name, a, layer, q, kind, after):
    L, r, c = a.shape
    tm = _row_tile(r, c)
    if kind == "col":
        full, o_spec = (r, N_CHIPS * c), pl.BlockSpec((tm, c), lambda i, q_ref: (i, q_ref[0]))
    else:
        full, o_spec = (N_CHIPS * r, c), pl.BlockSpec((tm, c), lambda i, q_ref: (q_ref[0] * (r // tm) + i, 0))

    def body(q_ref, a_ref, after_ref, o_ref):
        o_ref[...] = a_ref[0].astype(BF16)

    return pl.pallas_call(
        body, name=name,
        grid_spec=pltpu.PrefetchScalarGridSpec(
            num_scalar_prefetch=1, grid=(r // tm,),
            in_specs=[pl.BlockSpec((1, tm, c), lambda i, q_ref: (layer, i, 0)), ANY], out_specs=o_spec),
        out_shape=jax.ShapeDtypeStruct(full, BF16), compiler_params=_params(("parallel",)),
    )(q.reshape(1).astype(jnp.int32), a, after)


def _sum4(name, g, rcv, q, kind, n):
    r, c = rcv.shape[1:]
    tm = _row_tile(r, c)
    if kind == "col":
        g_spec = pl.BlockSpec((tm, n), lambda i, q_ref: (i, q_ref[0]))
    else:
        g_spec = pl.BlockSpec((tm, c), lambda i, q_ref: (q_ref[0] * (n // tm) + i, 0))

    def body(q_ref, g_ref, r_ref, o_ref):
        acc = g_ref[...].astype(F32)
        for j in range(3):
            acc = acc + r_ref[j].astype(F32)
        o_ref[...] = acc

    return pl.pallas_call(
        body, name=name,
        grid_spec=pltpu.PrefetchScalarGridSpec(
            num_scalar_prefetch=1, grid=(r // tm,),
            in_specs=[g_spec, pl.BlockSpec((3, tm, c), lambda i, q_ref: (0, i, 0))],
            out_specs=pl.BlockSpec((tm, c), lambda i, q_ref: (i, 0))),
        out_shape=jax.ShapeDtypeStruct((r, c), F32), compiler_params=_params(("parallel",)),
    )(q.reshape(1).astype(jnp.int32), g, rcv)


def _adamw_sum(name, w, m, v, layer, mine, theirs, prev):
    L, r, c = w.shape
    tm = _row_tile(r, c, limit=3 << 19)
    lay = pl.BlockSpec((1, tm, c), lambda i: (layer, i, 0))
    flat = pl.BlockSpec((tm, c), lambda i: (i, 0))
    n_prev = 0 if prev is None else 4

    def body(w_ref, m_ref, v_ref, a_ref, b_ref, *rest):
        g_ref, d_ref, mo_ref, vo_ref = rest[n_prev:]
        g = a_ref[...] + b_ref[...]
        g_ref[0] = g
        d_ref[0], mo_ref[0], vo_ref[0] = _adamw_val(w_ref[0], g, m_ref[0], v_ref[0])

    return pl.pallas_call(
        body, name=name, grid=(r // tm,),
        in_specs=[lay, lay, lay, flat, flat] + [ANY] * n_prev, out_specs=[lay] * 4,
        out_shape=[jax.ShapeDtypeStruct((L, r, c), F32)] * 4,
        input_output_aliases={5 + k: k for k in range(n_prev)},
        compiler_params=_params(("parallel",)),
    )(w, m, v, mine, theirs, *(prev or ()))


def _sum8(name, gathered, own=None):
    R, N = gathered.shape
    P = R // 8

    def body(g_ref, *rest):
        o_ref = rest[-1]
        me = 4 * lax.axis_index("x") + 2 * lax.axis_index("y") + lax.axis_index("c")
        acc = None
        for d in range(8):
            blk = g_ref[d * P:(d + 1) * P, :]
            if own is not None:
                blk = jnp.where(me == d, rest[0][...], blk)
            acc = blk if d == 0 else acc + blk
        o_ref[...] = acc

    return pl.pallas_call(body, name=name, out_shape=jax.ShapeDtypeStruct((P, N), F32),
                          compiler_params=pltpu.CompilerParams(vmem_limit_bytes=VMEM_LIMIT),
                          )(gathered, *(() if own is None else (own,)))


ANY = pl.BlockSpec(memory_space=pl.ANY)


def _mesh_pos():
    x, y, c = lax.axis_index("x"), lax.axis_index("y"), lax.axis_index("c")
    other_chips = [(1 - x, y), (x, 1 - y), (1 - x, 1 - y)]
    return x, y, c, other_chips


def _allgather8(name, blk, after=None):
    m_per, n = blk.shape

    def body(x_ref, *rest):
        out_ref, send_sems, recv_sems, local_sem = rest[after is not None:]
        x, y, c, chips = _mesh_pos()
        me, sibling = (x, y, c), (x, y, 1 - c)

        def rows(px, py, pc):
            return out_ref.at[pl.ds((4 * px + 2 * py + pc) * m_per, m_per), :]

        def copy(k, block, to, src=None):
            return pltpu.make_async_remote_copy(
                src_ref=rows(*block) if src is None else src, dst_ref=rows(*block),
                send_sem=send_sems.at[k], recv_sem=recv_sems.at[k], device_id=to, device_id_type=MESH)

        mine = pltpu.make_async_copy(x_ref, rows(*me), local_sem)
        mine.start()
        first = [copy(0, me, sibling, src=x_ref)]
        first += [copy(1 + j, me, (*chip, c), src=x_ref) for j, chip in enumerate(chips)]
        for cp in first:
            cp.start()
        passed = [copy(4 + j, (*chip, c), sibling) for j, chip in enumerate(chips)]
        for j, chip in enumerate(chips):
            copy(1 + j, (*chip, c), me).wait_recv()
            passed[j].start()
        copy(0, sibling, me).wait_recv()
        for j, chip in enumerate(chips):
            copy(4 + j, (*chip, 1 - c), me).wait_recv()
        for cp in first + passed:
            cp.wait_send()
        mine.wait()

    return pl.pallas_call(
        body, name=name, out_shape=jax.ShapeDtypeStruct((8 * m_per, n), blk.dtype),
        in_specs=[pl.BlockSpec(memory_space=pltpu.VMEM)] + [ANY] * (after is not None),
        out_specs=pl.BlockSpec(memory_space=pltpu.VMEM),
        scratch_shapes=[pltpu.SemaphoreType.DMA((7,)), pltpu.SemaphoreType.DMA((7,)), pltpu.SemaphoreType.DMA],
        compiler_params=pltpu.CompilerParams(vmem_limit_bytes=VMEM_LIMIT),
    )(blk, *(() if after is None else (after,)))


BIG = dict(w_in=("col", "ab_w_in", 0), w_out=("row", "ab_w_out", 0), up0=("col", "ffn_up_w", 0),
           dn0=("row", "ffn_down_w", 0), pw1=("col", "conv_pw1_w", 0), pw2=("row", "conv_pw2_w", 0),
           up1=("col", "ffn_up_w", 1), dn1=("row", "ffn_down_w", 1))
N_CHIPS = 4
HBM = pl.BlockSpec(memory_space=pltpu.HBM)
SEM = pl.BlockSpec(memory_space=pltpu.SEMAPHORE)
EFFECT = pltpu.SideEffectType.DATAFLOW_SIDE_EFFECTING


def _region(kind, ref, q, n):
    if kind == "col":
        return ref.at[:, pl.ds(q * n, n)]
    return ref.at[pl.ds(q * n, n), :]


def _gather_plan(kind, n):
    def remote(src, land, pos):
        x, y, c, chips = pos
        mine = _region(kind, land, 2 * x + y, n)
        return [(mine, mine, (*chip, c)) for chip in chips]

    return ("gather", kind, n), remote


def _scatter_plan(kind, n):
    def remote(src, land, pos):
        _, _, c, chips = pos
        return [(_region(kind, src, 2 * chip[0] + chip[1], n), land.at[j], (*chip, c)) for j, chip in enumerate(chips)]

    return ("scatter", kind, n), remote


def _everyone_plan(rows):
    def remote(src, land, pos):
        x, y, c, _ = pos
        mine = land.at[pl.ds((4 * x + 2 * y + c) * rows, rows), :]
        flip = lambda v, bit: 1 - v if bit else v
        return [(src, mine, (flip(x, k & 4), flip(y, k & 2), flip(c, k & 1))) for k in range(1, 8)]

    return ("everyone", rows), remote


def _sibling_plan():
    def remote(src, land, pos):
        x, y, c, _ = pos
        return [(src, land, (x, y, 1 - c))]

    return ("sibling",), remote


def _split_start(name, items, after=None):
    n = len(items)
    plans = [it[2] for it in items]
    n_in = 2 * n + (after is not None)

    def body(*refs):
        srcs, lands = refs[:n], refs[n:2 * n]
        sends, recvs = refs[n_in:n_in + n], refs[n_in + n:n_in + 2 * n]
        token = refs[n_in + 4 * n]
        pos = _mesh_pos()
        for a, (_, remote) in enumerate(plans):
            for k, (s, d, dev) in enumerate(remote(srcs[a], lands[a], pos)):
                pltpu.make_async_remote_copy(src_ref=s, dst_ref=d, send_sem=sends[a].at[k], recv_sem=recvs[a].at[k],
                                             device_id=dev, device_id_type=MESH).start()
        token[...] = jnp.zeros_like(token)

    sems = [pltpu.SemaphoreType.DMA((it[3],)) for it in items]
    bufs = [pltpu.HBM(it[k].shape, it[k].dtype) for k in (0, 1) for it in items]
    outs = pl.pallas_call(
        body, name=name, out_shape=[*sems, *sems, *bufs, jax.ShapeDtypeStruct((8, LANES), F32)],
        in_specs=[HBM] * (2 * n) + [ANY] * (n_in - 2 * n),
        out_specs=[SEM] * (2 * n) + [HBM] * (2 * n) + [pl.BlockSpec(memory_space=pltpu.VMEM)],
        input_output_aliases={i: 2 * n + i for i in range(2 * n)},
        compiler_params=pltpu.CompilerParams(has_side_effects=EFFECT),
    )(*[pltpu.with_memory_space_constraint(it[k], pltpu.HBM) for k in (0, 1) for it in items],
      *(() if after is None else (after,)))
    state = [(items[a][2], items[a][3], outs[2 * n + a], outs[3 * n + a], outs[a], outs[n + a]) for a in range(n)]
    return state, outs[4 * n]


def _split_wait(name, state, after):
    n = len(state)

    def body(*refs):
        srcs, lands = refs[:n], refs[n:2 * n]
        sends, recvs = refs[2 * n:3 * n], refs[3 * n:4 * n]
        pos = _mesh_pos()
        for a, ((_, remote), *_) in enumerate(state):
            for k, (s, d, dev) in enumerate(remote(srcs[a], lands[a], pos)):
                cp = pltpu.make_async_remote_copy(src_ref=s, dst_ref=d, send_sem=sends[a].at[k], recv_sem=recvs[a].at[k],
                                                  device_id=dev, device_id_type=MESH)
                cp.wait_send()
                cp.wait_recv()

    bufs = [st[k] for k in (2, 3) for st in state]
    outs = pl.pallas_call(
        body, name=name, out_shape=[pltpu.HBM(b.shape, b.dtype) for b in bufs],
        in_specs=[HBM] * (2 * n) + [SEM] * (2 * n) + [ANY], out_specs=[HBM] * (2 * n),
        input_output_aliases={i: i for i in range(2 * n)},
        compiler_params=pltpu.CompilerParams(has_side_effects=EFFECT),
    )(*bufs, *[st[k] for k in (4, 5) for st in state], after)
    return outs[:n], outs[n:]


class _Weights:
    def __init__(self, w, q, after):
        unused = jnp.zeros((16, LANES), BF16)

        def item(name, after):
            kind, pname, layer = BIG[name]
            _, r, c = w[pname].shape
            land = _cast_into_full(f"cast_{name}", w[pname], layer, q, kind, after)
            return unused, land, _gather_plan(kind, c if kind == "col" else r), N_CHIPS - 1

        first, *rest = BIG
        state1, token1 = _split_start("gw_start_first", [item(first, after)], after)
        state2, self.token = _split_start("gw_start_rest", [item(name, token1) for name in rest], token1)
        self.pending = dict(zip(BIG, state1 + state2))
        self.ready = {}

    def get(self, name, after=None):
        if name not in self.ready:
            self.ready[name] = _split_wait(f"gw_wait_{name}", [self.pending.pop(name)], after)[1][0]
        return self.ready[name]


class _GradPipe:
    def __init__(self, q, w, m, v):
        self.q, self.w, self.m, self.v = q, w, m, v
        self.stage, self.results = {}, {}

    def start_small(self, dmod, lossv, small, after):
        self.small_names = [n for n in REPLICATED if n != "ada_b"] + list(SMALL_SHARDED)
        payload = [dmod.reshape(2, -1), lossv] + [small[n] for n in self.small_names]
        self.small_shapes = [p.shape for p in payload]
        packed = _pack(payload)
        land = jnp.zeros((8 * PACK_ROWS, packed.shape[1]), F32)
        self.small_state, token = _split_start("ag_grads_start", [(packed, land, _everyone_plan(PACK_ROWS), 7)], after)
        return token

    def wait_small(self, after):
        srcs, lands = _split_wait("ag_grads_wait", self.small_state, after)
        return srcs[0], lands[0]

    def scatter(self, group, grads, after=None):
        items = []
        for name, g in grads.items():
            kind = BIG[name][0]
            rows, cols = g.shape
            n = (cols if kind == "col" else rows) // N_CHIPS
            reg = (rows, n) if kind == "col" else (n, cols)
            items.append((g, lax.empty((N_CHIPS - 1, *reg), BF16), _scatter_plan(kind, n), N_CHIPS - 1))
        state, token = _split_start(f"gs_start_{group}", items, after)
        self.stage[group] = (list(grads), state)
        return token[0, 0]

    def collect(self, group, after):
        names, state = self.stage[group]
        srcs, lands = _split_wait(f"gs_wait_{group}", state, after)
        items = []
        for name, st, g, land in zip(names, state, srcs, lands):
            _, kind, n = st[0][0]
            part = _sum4(f"sum_{name}", g, land, self.q, kind, n)
            items.append((part, lax.empty(part.shape, F32), _sibling_plan(), 1))
        state, token = _split_start(f"sw_start_{group}", items)
        self.stage[group] = (names, state)
        return token

    def finish(self, group, after):
        names, state = self.stage.pop(group)
        srcs, lands = _split_wait(f"sw_wait_{group}", state, after)
        for name, mine, theirs in zip(names, srcs, lands):
            _, pname, layer = BIG[name]
            self.results[pname] = _adamw_sum(f"adamw_{name}", self.w[pname], self.m[pname], self.v[pname], layer,
                                             mine, theirs, self.results.get(pname))


PACK_ROWS = 8


def _pack(arrays):
    flat = jnp.concatenate([a.reshape(-1) for a in arrays])
    n = flat.shape[0]
    padded = -(-n // (PACK_ROWS * LANES)) * (PACK_ROWS * LANES)
    return jnp.pad(flat, (0, padded - n)).reshape(PACK_ROWS, padded // PACK_ROWS)


def _unpack(packed, shapes):
    flat = packed.reshape(-1)
    out, off = [], 0
    for s in shapes:
        n = 1
        for d in s:
            n *= d
        out.append(flat[off:off + n].reshape(s))
        off += n
    return out


REPLICATED = ("ada_b", "norm_mix_g", "norm_ffn_g", "a_vnorm_g", "a_spatial_w", "a_spatial_b", "b_q_norm_g",
              "b_k_norm_g", "ffn_dw_b")
SMALL_SHARDED = ("conv_pw1_b", "conv_dw_w", "conv_dw_b", "conv_ln_g", "conv_ln_b", "conv_pw2_b", "ffn_dw_w")
WEIGHTS = ("ada_w", "ada_b", "norm_mix_g", "norm_ffn_g", "ab_w_in", "a_vnorm_g", "a_spatial_w", "a_spatial_b",
           "b_q_norm_g", "b_k_norm_g", "ab_w_out", "conv_pw1_w", "conv_pw1_b", "conv_dw_w", "conv_dw_b", "conv_ln_g",
           "conv_ln_b", "conv_pw2_w", "conv_pw2_b", "ffn_up_w", "ffn_dw_w", "ffn_dw_b", "ffn_down_w")

def kernel(x, c, positions, ada_w, ada_b, norm_mix_g, norm_ffn_g, ab_w_in, a_vnorm_g, a_spatial_w, a_spatial_b, b_q_norm_g, b_k_norm_g, ab_w_out, conv_pw1_w, conv_pw1_b, conv_dw_w, conv_dw_b, conv_ln_g, conv_ln_b, conv_pw2_w, conv_pw2_b, ffn_up_w, ffn_dw_w, ffn_dw_b, ffn_down_w, loss_target, m_ada_w, m_ada_b, m_norm_mix_g, m_norm_ffn_g, m_ab_w_in, m_a_vnorm_g, m_a_spatial_w, m_a_spatial_b, m_b_q_norm_g, m_b_k_norm_g, m_ab_w_out, m_conv_pw1_w, m_conv_pw1_b, m_conv_dw_w, m_conv_dw_b, m_conv_ln_g, m_conv_ln_b, m_conv_pw2_w, m_conv_pw2_b, m_ffn_up_w, m_ffn_dw_w, m_ffn_dw_b, m_ffn_down_w, v_ada_w, v_ada_b, v_norm_mix_g, v_norm_ffn_g, v_ab_w_in, v_a_vnorm_g, v_a_spatial_w, v_a_spatial_b, v_b_q_norm_g, v_b_k_norm_g, v_ab_w_out, v_conv_pw1_w, v_conv_pw1_b, v_conv_dw_w, v_conv_dw_b, v_conv_ln_g, v_conv_ln_b, v_conv_pw2_w, v_conv_pw2_b, v_ffn_up_w, v_ffn_dw_w, v_ffn_dw_b, v_ffn_down_w):
    w = dict(ada_w=ada_w, ada_b=ada_b, norm_mix_g=norm_mix_g, norm_ffn_g=norm_ffn_g, ab_w_in=ab_w_in, a_vnorm_g=a_vnorm_g, a_spatial_w=a_spatial_w, a_spatial_b=a_spatial_b, b_q_norm_g=b_q_norm_g, b_k_norm_g=b_k_norm_g, ab_w_out=ab_w_out, conv_pw1_w=conv_pw1_w, conv_pw1_b=conv_pw1_b, conv_dw_w=conv_dw_w, conv_dw_b=conv_dw_b, conv_ln_g=conv_ln_g, conv_ln_b=conv_ln_b, conv_pw2_w=conv_pw2_w, conv_pw2_b=conv_pw2_b, ffn_up_w=ffn_up_w, ffn_dw_w=ffn_dw_w, ffn_dw_b=ffn_dw_b, ffn_down_w=ffn_down_w)
    m = dict(ada_w=m_ada_w, ada_b=m_ada_b, norm_mix_g=m_norm_mix_g, norm_ffn_g=m_norm_ffn_g, ab_w_in=m_ab_w_in, a_vnorm_g=m_a_vnorm_g, a_spatial_w=m_a_spatial_w, a_spatial_b=m_a_spatial_b, b_q_norm_g=m_b_q_norm_g, b_k_norm_g=m_b_k_norm_g, ab_w_out=m_ab_w_out, conv_pw1_w=m_conv_pw1_w, conv_pw1_b=m_conv_pw1_b, conv_dw_w=m_conv_dw_w, conv_dw_b=m_conv_dw_b, conv_ln_g=m_conv_ln_g, conv_ln_b=m_conv_ln_b, conv_pw2_w=m_conv_pw2_w, conv_pw2_b=m_conv_pw2_b, ffn_up_w=m_ffn_up_w, ffn_dw_w=m_ffn_dw_w, ffn_dw_b=m_ffn_dw_b, ffn_down_w=m_ffn_down_w)
    v = dict(ada_w=v_ada_w, ada_b=v_ada_b, norm_mix_g=v_norm_mix_g, norm_ffn_g=v_norm_ffn_g, ab_w_in=v_ab_w_in, a_vnorm_g=v_a_vnorm_g, a_spatial_w=v_a_spatial_w, a_spatial_b=v_a_spatial_b, b_q_norm_g=v_b_q_norm_g, b_k_norm_g=v_b_k_norm_g, ab_w_out=v_ab_w_out, conv_pw1_w=v_conv_pw1_w, conv_pw1_b=v_conv_pw1_b, conv_dw_w=v_conv_dw_w, conv_dw_b=v_conv_dw_b, conv_ln_g=v_conv_ln_g, conv_ln_b=v_conv_ln_b, conv_pw2_w=v_conv_pw2_w, conv_pw2_b=v_conv_pw2_b, ffn_up_w=v_ffn_up_w, ffn_dw_w=v_ffn_dw_w, ffn_dw_b=v_ffn_dw_b, ffn_down_w=v_ffn_down_w)
    S, D = x.shape[1], x.shape[2]
    xi, yi, ci = lax.axis_index("x"), lax.axis_index("y"), lax.axis_index("c")
    q = 2 * xi + yi
    b = 2 * q + ci
    take_dev = lambda g: g.reshape(8, PACK_ROWS, -1)

    c_all = _allgather8("ag_c", c.reshape(PACK_ROWS, D // PACK_ROWS)).reshape(8, D)
    n_ada = ada_w.shape[2]
    mod_sh = _ada_fwd("ada_fwd", c_all, ada_w, lax.dynamic_slice_in_dim(ada_b, q * n_ada, n_ada, axis=1))
    sh_shapes = [mod_sh.shape] + [w[n].shape for n in SMALL_SHARDED]
    gathered_mod = _allgather8("ag_mod", _pack([mod_sh] + [w[n] for n in SMALL_SHARDED]))
    per_chip = [_unpack(blk, sh_shapes) for blk in take_dev(gathered_mod)[0::2]]
    mod_g = jnp.stack([pc[0] for pc in per_chip])
    mod_mine = lax.dynamic_index_in_dim(mod_g, b, axis=2, keepdims=False)
    mod = mod_mine.transpose(1, 0, 2).reshape(2, 6, D)
    sp = {n: jnp.concatenate([pc[1 + i] for pc in per_chip], axis=-1) for i, n in enumerate(SMALL_SHARDED)}
    sp.update({n: w[n] for n in REPLICATED if n != "ada_b"})

    wb = _Weights(w, q, gathered_mod)
    mod = mod + wb.token[0, 0]

    pipe = _GradPipe(q, w, m, v)
    lossv, grad_x, late = _local_step(x[0], loss_target[0], positions[0], mod, wb, sp, pipe)

    pipe.finish("g1", grad_x)
    pipe.finish("g2", pipe.results["ffn_up_w"][0])
    swapped = pipe.collect("g3", pipe.results["conv_pw1_w"][0])
    swapped = pipe.collect("g4", swapped)

    own, gathered = pipe.wait_small(swapped)
    totals = _unpack(_sum8("sum_grads", gathered, own), pipe.small_shapes)
    grads = dict(zip(["ada_b", "loss_columns"] + pipe.small_names, totals))
    loss = 0.5 * jnp.sum(grads.pop("loss_columns")) / D
    late_g = _allgather8("ag_late", _pack(list(late)), gathered)
    late_tot = _unpack(_sum8("sum_late", late_g), [(3, D)])[0]
    grads["ada_b"] = grads["ada_b"].at[0, :2 * D].add(late_tot[:2].reshape(-1))
    grads["norm_mix_g"] = grads["norm_mix_g"].at[0].add(late_tot[2])
    for n in SMALL_SHARDED:
        n_sh = w[n].shape[-1]
        grads[n] = lax.dynamic_slice_in_dim(grads[n], q * n_sh, n_sh, axis=grads[n].ndim - 1)
    dmod_of = lambda packed: packed.reshape(packed.shape[0] // PACK_ROWS, -1)[:, :2 * 6 * D].reshape(-1, 2, 6 * D)
    dmod_all = jnp.where((jnp.arange(8) == b)[:, None, None], dmod_of(own), dmod_of(gathered))
    late_all = take_dev(late_g).reshape(8, -1)[:, :3 * D].reshape(8, 3, D)
    dmod_all = dmod_all.at[:, 0, :2 * D].add(late_all[:, :2].reshape(8, 2 * D))
    dmod_sh = lax.dynamic_slice_in_dim(dmod_all, q * n_ada, n_ada, axis=2).transpose(1, 0, 2)

    pipe.finish("g3", dmod_sh)
    pipe.finish("g4", pipe.results["ffn_up_w"][0])
    grads["ada_w"], delta_ada, m_ada, v_ada = _ada_update("ada_update", c_all, dmod_sh, ada_w, m_ada_w, v_ada_w)
    delta, new_m, new_v = dict(ada_w=delta_ada), dict(ada_w=m_ada), dict(ada_w=v_ada)
    rest = list(REPLICATED) + list(SMALL_SHARDED)
    rest_shapes = [w[n].shape for n in rest]
    outs = _adamw("adamw_small", *[_pack([src[n].reshape(w[n].shape) for n in rest]) for src in (w, grads, m, v)])
    for tgt, packed in zip((delta, new_m, new_v), outs):
        tgt.update(dict(zip(rest, _unpack(packed, rest_shapes))))
    for n in rest:
        grads[n] = grads[n].reshape(w[n].shape)
    pipe.finish("g5", pipe.collect("g5", outs[0]))
    for n, res in pipe.results.items():
        grads[n], delta[n], new_m[n], new_v[n] = res

    return (loss, grad_x[None], *[grads[n] for n in WEIGHTS], *[delta[n] for n in WEIGHTS],
            *[new_m[n] for n in WEIGHTS], *[new_v[n] for n in WEIGHTS])
```

```python
import functools

import jax
import jax.numpy as jnp
from jax import lax
from jax.experimental import pallas as pl
from jax.experimental.pallas import tpu as pltpu

F32, BF16 = jnp.float32, jnp.bfloat16
EPS = 1e-6
NEG = -1e30
ROPE_THETA = 10000.0
LANES = 128
VMEM_LIMIT = 56 * 1024 * 1024
ADAM_LR, ADAM_B1, ADAM_B2, ADAM_EPS, ADAM_WD, ADAM_STEP = 0.001, 0.9, 0.999, 1e-08, 0.01, 10
MESH = pl.DeviceIdType.MESH


def _params(sem):
    return pltpu.CompilerParams(dimension_semantics=sem, vmem_limit_bytes=VMEM_LIMIT)


_DN = {"nn": (((1,), (0,)), ((), ())), "nt": (((1,), (1,)), ((), ())), "tn": (((0,), (0,)), ((), ()))}


def _matmul(name, a, b, mode, out_dtype, tm=512, tn=512, tk=1024, bias=None, after=None):
    if mode == "nn":
        (M, K), N = a.shape, b.shape[1]
    elif mode == "nt":
        (M, K), N = a.shape, b.shape[0]
    else:
        (K, M), N = a.shape, b.shape[1]
    tm, tn, tk = min(tm, M), min(tn, N), min(tk, K)
    assert M % tm == 0 and N % tn == 0 and K % tk == 0, (name, M, N, K, tm, tn, tk)
    nk = K // tk
    if mode == "tn":
        a_spec = pl.BlockSpec((tk, tm), lambda i, j, k: (k, i))
    else:
        a_spec = pl.BlockSpec((tm, tk), lambda i, j, k: (i, k))
    if mode == "nt":
        b_spec = pl.BlockSpec((tn, tk), lambda i, j, k: (j, k))
    else:
        b_spec = pl.BlockSpec((tk, tn), lambda i, j, k: (k, j))
    in_specs, args = [a_spec, b_spec], [a, b]
    if bias is not None:
        in_specs.append(pl.BlockSpec((1, tn), lambda i, j, k: (0, j)))
        args.append(bias)
    if after is not None:
        in_specs.append(pl.BlockSpec(memory_space=pl.ANY))
        args.append(after)
    n_in = len(args)

    def body(*refs):
        a_ref, b_ref, o_ref = refs[0], refs[1], refs[n_in]
        p = lax.dot_general(a_ref[...], b_ref[...], _DN[mode], preferred_element_type=F32)

        def finish(acc):
            if bias is not None:
                acc = acc + refs[2][...]
            o_ref[...] = acc.astype(o_ref.dtype)

        if nk == 1:
            finish(p)
        else:
            acc_ref = refs[n_in + 1]
            k = pl.program_id(2)

            @pl.when(k == 0)
            def _():
                acc_ref[...] = p

            @pl.when(k > 0)
            def _():
                acc_ref[...] += p

            @pl.when(k == nk - 1)
            def _():
                finish(acc_ref[...])

    return pl.pallas_call(
        body, name=name, grid=(M // tm, N // tn, nk), in_specs=in_specs,
        out_specs=pl.BlockSpec((tm, tn), lambda i, j, k: (i, j)),
        out_shape=jax.ShapeDtypeStruct((M, N), out_dtype),
        scratch_shapes=[pltpu.VMEM((tm, tn), F32)] if nk > 1 else [],
        compiler_params=_params(("parallel", "parallel", "arbitrary")),
    )(*args)


def _rowcall(name, body, nrows, tm, ins, outs, scratch=()):
    nblk = nrows // tm
    assert nrows % tm == 0

    def spec(kind, shape):
        k = kind[0]
        if k == "row":
            cw, cb = kind[1] or shape[-1], kind[2]
            return pl.BlockSpec((tm, cw), lambda i: (i, cb))
        if k == "prev":
            hb, cw, cb = kind[1], kind[2] or shape[-1], kind[3]
            r = tm // hb
            return pl.BlockSpec((hb, cw), lambda i: (jnp.maximum(i * r - 1, 0), cb))
        if k == "next":
            hb, cw, cb = kind[1], kind[2] or shape[-1], kind[3]
            r, last = tm // hb, nrows // hb - 1
            return pl.BlockSpec((hb, cw), lambda i: (jnp.minimum((i + 1) * r, last), cb))
        if k == "off":
            off, cw, cb = kind[1], kind[2] or shape[-1], kind[3]
            return pl.BlockSpec((tm, cw), lambda i: (jnp.clip(i + off, 0, nblk - 1), cb))
        if k == "cls":
            dil, off = kind[1], kind[2]
            return pl.BlockSpec((dil, tm // dil, shape[-1]), lambda i: (0, jnp.clip(i + off, 0, nblk - 1), 0))
        nd = len(shape)
        return pl.BlockSpec(tuple(shape), lambda i: (0,) * nd)

    has_acc = any(o[2][0] == "acc" for o in outs)
    return pl.pallas_call(
        body, name=name, grid=(nblk,),
        in_specs=[spec(kind, a.shape) for a, kind in ins],
        out_specs=[spec(kind, shape) for shape, _, kind in outs],
        out_shape=[jax.ShapeDtypeStruct(tuple(shape), dt) for shape, dt, _ in outs],
        scratch_shapes=list(scratch),
        compiler_params=_params(("arbitrary",) if has_acc else ("parallel",)),
    )(*[a for a, _ in ins])


ROW = ("row", None, 0)
FULL = ("full",)
ACC = ("acc",)


def _colsum(x):
    return jnp.sum(x, axis=0, keepdims=True)


def _acc_add(i, ref, val, rows=None):
    idx = (slice(None),) * len(ref.shape) if rows is None else rows

    @pl.when(i == 0)
    def _():
        ref[idx] = val

    @pl.when(i > 0)
    def _():
        ref[idx] = ref[idx] + val


def _sigmoid(x):
    return 1.0 / (1.0 + jnp.exp(-x))


def _gelu(x):
    return 0.5 * x * (1.0 + lax.erf(x * (2.0 ** -0.5)))


def _gelu_grad(x):
    return 0.5 * (1.0 + lax.erf(x * (2.0 ** -0.5))) + x * jnp.exp(-0.5 * x * x) * ((2.0 * jnp.pi) ** -0.5)


SUBLANES = 8


def _phases(ext, sign):
    n = ext.shape[0]
    return [ext if b == 0 else pltpu.roll(ext, b if sign > 0 else n - b, axis=0) for b in range(SUBLANES)]


def _shift_prev(phases, s, hb):
    a, b = divmod(s, SUBLANES)
    return phases[b][hb - SUBLANES * a:phases[b].shape[0] - SUBLANES * a]


def _shift_next(phases, s, tm):
    a, b = divmod(s, SUBLANES)
    return phases[b][SUBLANES * a:SUBLANES * a + tm]


def _rms_mod_val(x, g, sc, sh):
    r = lax.rsqrt(jnp.mean(x * x, axis=-1, keepdims=True) + EPS)
    return x * r * g * (1.0 + sc) + sh


def _mod_first(name, x, g, sc, sh, tm=512):
    S, D = x.shape

    def body(x_ref, g_ref, sc_ref, sh_ref, h_ref):
        h_ref[...] = _rms_mod_val(x_ref[...], g_ref[...], sc_ref[...], sh_ref[...]).astype(BF16)

    return _rowcall(name, body, S, tm, [(x, ROW), (g, FULL), (sc, FULL), (sh, FULL)], [((S, D), BF16, ROW)])[0]


def _resid_mod(name, x, y, gate, g, sc, sh, tm=512):
    S, D = x.shape

    def body(x_ref, y_ref, gate_ref, g_ref, sc_ref, sh_ref, xo_ref, h_ref):
        xn = x_ref[...] + gate_ref[...] * y_ref[...]
        xo_ref[...] = xn
        h_ref[...] = _rms_mod_val(xn, g_ref[...], sc_ref[...], sh_ref[...]).astype(BF16)

    return _rowcall(name, body, S, tm,
                    [(x, ROW), (y, ROW), (gate, FULL), (g, FULL), (sc, FULL), (sh, FULL)],
                    [((S, D), F32, ROW), ((S, D), BF16, ROW)])


def _gate_bwd_val(i, d, y_ref, gate_ref, dy_ref, dg_ref, db_ref):
    dy = d * gate_ref[...]
    dy_ref[...] = dy.astype(BF16)
    _acc_add(i, dg_ref, _colsum(d * y_ref[...]))
    _acc_add(i, db_ref, _colsum(dy))


GATE_OUTS = lambda S, D: [((S, D), BF16, ROW), ((1, D), F32, ACC), ((1, D), F32, ACC)]


def _loss_head(name, x, y, gate, tgt, tm=512):
    S, D = x.shape

    def body(x_ref, y_ref, gate_ref, t_ref, dx_ref, l_ref, dy_ref, dg_ref, db_ref):
        i = pl.program_id(0)
        err = x_ref[...] + gate_ref[...] * y_ref[...] - t_ref[...]
        d = err * (1.0 / D)
        dx_ref[...] = d
        _acc_add(i, l_ref, _colsum(err * err))
        _gate_bwd_val(i, d, y_ref, gate_ref, dy_ref, dg_ref, db_ref)

    return _rowcall(name, body, S, tm, [(x, ROW), (y, ROW), (gate, FULL), (tgt, ROW)],
                    [((S, D), F32, ROW), ((1, D), F32, ACC)] + GATE_OUTS(S, D))


def _mod_bwd(name, dxo, dh, x, g, sc, y=None, gate=None, tm=512):
    S, D = x.shape
    gated = y is not None

    def body(d_ref, dh_ref, x_ref, g_ref, sc_ref, *rest):
        dx_ref, dsh_ref, dsc_ref, dg_ref = rest[2 * gated:2 * gated + 4]
        i = pl.program_id(0)
        xv, dh_v, gv = x_ref[...], dh_ref[...], g_ref[...]
        r = lax.rsqrt(jnp.mean(xv * xv, axis=-1, keepdims=True) + EPS)
        n = xv * r
        _acc_add(i, dsh_ref, _colsum(dh_v))
        _acc_add(i, dsc_ref, _colsum(dh_v * (n * gv)))
        dy = dh_v * (1.0 + sc_ref[...])
        _acc_add(i, dg_ref, _colsum(dy * n))
        dn = dy * gv
        dx = d_ref[...] + r * (dn - n * jnp.mean(dn * n, axis=-1, keepdims=True))
        dx_ref[...] = dx
        if gated:
            _gate_bwd_val(i, dx, rest[0], rest[1], *rest[6:9])

    ins = [(dxo, ROW), (dh, ROW), (x, ROW), (g, FULL), (sc, FULL)] + ([(y, ROW), (gate, FULL)] if gated else [])
    outs = [((S, D), F32, ROW), ((1, D), F32, ACC), ((1, D), F32, ACC), ((1, D), F32, ACC)]
    return _rowcall(name, body, S, tm, ins, outs + (GATE_OUTS(S, D) if gated else []))


HB16 = 16


def _conv3_val(ph, w, b, hb):
    return w[2:3] * _shift_prev(ph, 0, hb) + w[1:2] * _shift_prev(ph, 1, hb) + w[0:1] * _shift_prev(ph, 2, hb) + b


def _halo_first(halo_ref, tile_ref, live):
    return _phases(jnp.concatenate([halo_ref[...].astype(F32) * live, tile_ref[...].astype(F32)], axis=0), 1)


def _glu3_fwd(name, u, w, b, tm=128):
    S, F2 = u.shape
    Fh = F2 // 2

    def body(ua_ref, ub_ref, ha_ref, hb_ref, w_ref, b_ref, o_ref, z_ref):
        live = (pl.program_id(0) > 0).astype(F32)
        wv, bv = w_ref[...], b_ref[...]
        za = _conv3_val(_halo_first(ha_ref, ua_ref, live), wv[:, :Fh], bv[:, :Fh], HB16)
        zb = _conv3_val(_halo_first(hb_ref, ub_ref, live), wv[:, Fh:], bv[:, Fh:], HB16)
        o_ref[...] = (za * _sigmoid(za) * zb).astype(BF16)
        z_ref[:, :Fh] = za.astype(BF16)
        z_ref[:, Fh:] = zb.astype(BF16)

    return _rowcall(name, body, S, tm,
                    [(u, ("row", Fh, 0)), (u, ("row", Fh, 1)), (u, ("prev", HB16, Fh, 0)), (u, ("prev", HB16, Fh, 1)),
                     (w, FULL), (b, FULL)],
                    [((S, Fh), BF16, ROW), ((S, F2), BF16, ROW)])


def _glu3_bwd(name, z, dhm, tm=128):
    S, F2 = z.shape
    Fh = F2 // 2

    def body(za_ref, zb_ref, d_ref, dz_ref, db_ref):
        i = pl.program_id(0)
        za, zb, d = za_ref[...].astype(F32), zb_ref[...].astype(F32), d_ref[...]
        sg = _sigmoid(za)
        da = d * zb * (sg * (1.0 + za * (1.0 - sg)))
        db = d * (za * sg)
        dz_ref[:, :Fh] = da.astype(BF16)
        dz_ref[:, Fh:] = db.astype(BF16)
        _acc_add(i, db_ref, jnp.concatenate([_colsum(da), _colsum(db)], axis=1))

    return _rowcall(name, body, S, tm, [(z, ("row", Fh, 0)), (z, ("row", Fh, 1)), (dhm, ROW)],
                    [((S, F2), BF16, ROW), ((1, F2), F32, ACC)])


def _conv3_bwd(name, dz, u, w, tm=128):
    S, F2 = dz.shape
    nblk = S // tm
    K = w.shape[0]

    def body(d_ref, n_ref, u_ref, w_ref, o_ref, dw_ref):
        i = pl.program_id(0)
        live = (i < nblk - 1).astype(F32)
        ph = _phases(jnp.concatenate([d_ref[...].astype(F32), n_ref[...].astype(F32) * live], axis=0), -1)
        wv, uv = w_ref[...], u_ref[...].astype(F32)
        shifted = [_shift_next(ph, K - 1 - k, tm) for k in range(K)]
        o_ref[...] = functools.reduce(lambda a, t: a + t, [wv[k:k + 1] * shifted[k] for k in range(K)]).astype(BF16)
        for k in range(K):
            _acc_add(i, dw_ref, _colsum(uv * shifted[k]), rows=(slice(k, k + 1), slice(None)))

        @pl.when(i == 0)
        def _():
            dw_ref[K:, :] = jnp.zeros((dw_ref.shape[0] - K, F2), F32)

    return _rowcall(name, body, S, tm, [(dz, ROW), (dz, ("next", HB16, None, 0)), (u, ROW), (w, FULL)],
                    [((S, F2), BF16, ROW), ((SUBLANES, F2), F32, ACC)])


def _ffn_fwd(name, h, w_up, w_dn, dw_w, dw_b):
    u = _matmul(f"{name}_up", h, w_up, "nn", BF16, tm=2048)
    hm, z = _glu3_fwd(f"{name}_glu", u, dw_w, dw_b)
    f = _matmul(f"{name}_dn", hm, w_dn, "nn", F32, tm=1024, tn=1024, tk=w_dn.shape[0])
    return f, (u, hm, z)


def _ffn_bwd(name, dy, h, u, hm, z, w_up, w_dn, dw_w, dw_b):
    Fh = w_dn.shape[0]
    dhm = _matmul(f"{name}_ddn_x", dy, w_dn, "nt", F32, tm=1024, tn=Fh // 2)
    g_dn = _matmul(f"{name}_ddn_w", hm, dy, "tn", BF16, tm=Fh // 2, tn=1024, tk=2048)
    dz, g_dw_b = _glu3_bwd(f"{name}_dglu", z, dhm)
    du, taps = _conv3_bwd(f"{name}_dconv", dz, u, dw_w)
    g_up = _matmul(f"{name}_dup_w", h, du, "tn", BF16, tm=1024, tn=Fh // 2, tk=2048)
    dh = _matmul(f"{name}_dup_x", du, w_up, "nt", F32, tm=1024, tn=1024, tk=Fh)
    return dh, dict(up=g_up, dn=g_dn, dw_w=taps[0:dw_w.shape[0]], dw_b=g_dw_b)


HB32 = 32


def _glu31_fwd(name, p, w, b, tm=256):
    S, D2 = p.shape
    D = D2 // 2
    K = w.shape[0]

    def body(a_ref, g_ref, ha_ref, hg_ref, w_ref, b_ref, o_ref):
        live = (pl.program_id(0) > 0).astype(F32)
        y1 = a_ref[...] * _sigmoid(g_ref[...])
        ph = _phases(jnp.concatenate([ha_ref[...] * _sigmoid(hg_ref[...]) * live, y1], axis=0), 1)
        wv = w_ref[...]
        acc = b_ref[...] + wv[K - 1:K] * y1
        for k in range(K - 1):
            acc = acc + wv[k:k + 1] * _shift_prev(ph, K - 1 - k, HB32)
        o_ref[...] = acc

    return _rowcall(name, body, S, tm,
                    [(p, ("row", D, 0)), (p, ("row", D, 1)), (p, ("prev", HB32, D, 0)), (p, ("prev", HB32, D, 1)),
                     (w, FULL), (b, FULL)],
                    [((S, D), F32, ROW)])[0]


def _ln_silu_fwd(name, y2, g, b, tm=512):
    S, D = y2.shape

    def body(y_ref, g_ref, b_ref, o_ref):
        y = y_ref[...]
        mu = jnp.mean(y, axis=-1, keepdims=True)
        yc = y - mu
        rs = lax.rsqrt(jnp.mean(yc * yc, axis=-1, keepdims=True) + EPS)
        y3 = yc * rs * g_ref[...] + b_ref[...]
        o_ref[...] = (y3 * _sigmoid(y3)).astype(BF16)

    return _rowcall(name, body, S, tm, [(y2, ROW), (g, FULL), (b, FULL)], [((S, D), BF16, ROW)])[0]


def _ln_silu_bwd(name, y2, dy4, g, b, tm=512):
    S, D = y2.shape

    def body(y_ref, d_ref, g_ref, b_ref, o_ref, dg_ref, db_ref):
        i = pl.program_id(0)
        y, gv = y_ref[...], g_ref[...]
        mu = jnp.mean(y, axis=-1, keepdims=True)
        yc = y - mu
        rs = lax.rsqrt(jnp.mean(yc * yc, axis=-1, keepdims=True) + EPS)
        n = yc * rs
        y3 = n * gv + b_ref[...]
        sg = _sigmoid(y3)
        dy3 = d_ref[...] * (sg * (1.0 + y3 * (1.0 - sg)))
        _acc_add(i, db_ref, _colsum(dy3))
        _acc_add(i, dg_ref, _colsum(dy3 * n))
        dn = dy3 * gv
        o_ref[...] = rs * (dn - jnp.mean(dn, axis=-1, keepdims=True) - n * jnp.mean(dn * n, axis=-1, keepdims=True))

    return _rowcall(name, body, S, tm, [(y2, ROW), (dy4, ROW), (g, FULL), (b, FULL)],
                    [((S, D), F32, ROW), ((1, D), F32, ACC), ((1, D), F32, ACC)])


def _glu31_bwd(name, p, dy2, w, tm=256):
    S, D2 = p.shape
    D = D2 // 2
    K = w.shape[0]
    nblk = S // tm

    conv_rows, tap_rows, tap_group, tap_unroll = 16, SUBLANES, 4, 4

    def body(a_ref, g_ref, d_ref, dn_ref, w_ref, dp_ref, dw_ref, dcb_ref, dpb_ref, ph_d, y1_ref, dy1_ref, wb_ref):
        i = pl.program_id(0)
        live_next = (i < nblk - 1).astype(F32)
        a, sg, d = a_ref[...], _sigmoid(g_ref[...]), d_ref[...]
        y1_ref[...] = a * sg
        for b, ph in enumerate(_phases(jnp.concatenate([d, dn_ref[...] * live_next], axis=0), -1)):
            ph_d[b] = ph
        taps = [divmod(K - 1 - k, SUBLANES) for k in range(K)]

        @pl.when(i == 0)
        def _():
            for k in range(K):
                wb_ref[k] = jnp.broadcast_to(w_ref[k:k + 1, :], (SUBLANES, D))

        def conv_rows_at(rb, carry):
            r0 = pl.multiple_of(rb * conv_rows, conv_rows)
            accs = [jnp.zeros((SUBLANES, D), F32) for _ in range(conv_rows // SUBLANES)]
            for k, (rows8, phase) in enumerate(taps):
                wk = wb_ref[k]
                for u in range(len(accs)):
                    accs[u] = accs[u] + wk * ph_d[phase, pl.ds(r0 + SUBLANES * (rows8 + u), SUBLANES), :]
            for u, acc in enumerate(accs):
                dy1_ref[pl.ds(r0 + SUBLANES * u, SUBLANES), :] = acc
            return carry

        lax.fori_loop(0, tm // conv_rows, conv_rows_at, 0)
        for k0 in range(0, K, tap_group):
            group = taps[k0:k0 + tap_group]

            def tap_rows_at(rb, accs, group=group):
                for u in range(tap_unroll):
                    r0 = pl.multiple_of((rb * tap_unroll + u) * tap_rows, tap_rows)
                    yv = y1_ref[pl.ds(r0, tap_rows), :]
                    accs = tuple(acc + yv * ph_d[phase, pl.ds(r0 + SUBLANES * rows8, tap_rows), :]
                                 for acc, (rows8, phase) in zip(accs, group))
                return accs

            accs = lax.fori_loop(0, tm // (tap_rows * tap_unroll), tap_rows_at,
                                 tuple(jnp.zeros((tap_rows, D), F32) for _ in group))
            for j, acc in enumerate(accs):
                _acc_add(i, dw_ref, _colsum(acc), rows=(slice(k0 + j, k0 + j + 1), slice(None)))

        @pl.when(i == 0)
        def _():
            dw_ref[K:, :] = jnp.zeros((dw_ref.shape[0] - K, D), F32)

        _acc_add(i, dcb_ref, _colsum(d))
        dy1 = dy1_ref[...]
        da = dy1 * sg
        dg = dy1 * a * sg * (1.0 - sg)
        dp_ref[:, :D] = da.astype(BF16)
        dp_ref[:, D:] = dg.astype(BF16)
        _acc_add(i, dpb_ref, jnp.concatenate([_colsum(da), _colsum(dg)], axis=1))

    return _rowcall(name, body, S, tm,
                    [(p, ("row", D, 0)), (p, ("row", D, 1)), (dy2, ROW), (dy2, ("next", HB32, None, 0)), (w, FULL)],
                    [((S, D2), BF16, ROW), ((HB32, D), F32, ACC), ((1, D), F32, ACC), ((1, D2), F32, ACC)],
                    scratch=[pltpu.VMEM((SUBLANES, tm + HB32, D), F32), pltpu.VMEM((tm, D), F32),
                             pltpu.VMEM((tm, D), F32), pltpu.VMEM((K, SUBLANES, D), F32)])


CHUNK = 128
A_GROUPS = 4


def _group_ln(gv):
    ns, rss = [], []
    for g in range(A_GROUPS):
        xg = gv[:, g * LANES:(g + 1) * LANES]
        xc = xg - jnp.mean(xg, axis=-1, keepdims=True)
        rs = lax.rsqrt(jnp.mean(xc * xc, axis=-1, keepdims=True) + EPS)
        ns.append(xc * rs)
        rss.append(jnp.broadcast_to(rs, xg.shape))
    return jnp.concatenate(ns, axis=1), jnp.concatenate(rss, axis=1)


def _tril_mask():
    r = lax.broadcasted_iota(jnp.int32, (CHUNK, CHUNK), 0)
    c = lax.broadcasted_iota(jnp.int32, (CHUNK, CHUNK), 1)
    return r >= c


def _spatial(ws_ref, x, dn):
    mask = _tril_mask()
    rows = []
    for ci in range(x.shape[0] // CHUNK):
        cols = []
        for g in range(A_GROUPS):
            wm = jnp.where(mask, ws_ref[g], 0.0).astype(BF16)
            xb = x[ci * CHUNK:(ci + 1) * CHUNK, g * LANES:(g + 1) * LANES]
            cols.append(lax.dot_general(wm, xb, dn, preferred_element_type=F32))
        rows.append(jnp.concatenate(cols, axis=1))
    return jnp.concatenate(rows, axis=0)


def _mixa_fwd(name, z, vg, ws, bias_full, tm=256):
    S = z.shape[0]
    W = A_GROUPS * LANES

    def body(u_ref, v_ref, vg_ref, ws_ref, b_ref, o_ref):
        nh, _ = _group_ln(_gelu(v_ref[...]))
        vn = (nh * vg_ref[...]).astype(BF16)
        f = _spatial(ws_ref, vn, _DN["nn"]) + jnp.concatenate([b_ref[...]] * (tm // CHUNK), axis=0)
        o_ref[...] = (_gelu(u_ref[...]) * f).astype(BF16)

    return _rowcall(name, body, S, tm,
                    [(z, ("row", W, 0)), (z, ("row", W, 1)), (vg, FULL), (ws, FULL), (bias_full, FULL)],
                    [((S, W), BF16, ROW)])[0]


def _mixa_bwd(name, z, dyab, vg, ws, bias_full, tm=256):
    S = z.shape[0]
    W = A_GROUPS * LANES
    nch = tm // CHUNK

    def body(u_ref, v_ref, d_ref, vg_ref, ws_ref, b_ref, dz_ref, dws_ref, dbf_ref, dvg_ref):
        i = pl.program_id(0)
        u, v, d, vgv = u_ref[...], v_ref[...], d_ref[...], vg_ref[...]
        nh, rs = _group_ln(_gelu(v))
        vn = (nh * vgv).astype(BF16)
        f = _spatial(ws_ref, vn, _DN["nn"]) + jnp.concatenate([b_ref[...]] * nch, axis=0)
        dz_ref[:, :W] = (d * f * _gelu_grad(u)).astype(BF16)
        df = d * _gelu(u)
        dbf = df[0:CHUNK]
        for ci in range(1, nch):
            dbf = dbf + df[ci * CHUNK:(ci + 1) * CHUNK]
        _acc_add(i, dbf_ref, dbf)
        dfb = df.astype(BF16)
        mask = _tril_mask()
        for g in range(A_GROUPS):
            acc = jnp.zeros((CHUNK, CHUNK), F32)
            for ci in range(nch):
                blk = (slice(ci * CHUNK, (ci + 1) * CHUNK), slice(g * LANES, (g + 1) * LANES))
                acc = acc + lax.dot_general(dfb[blk], vn[blk], _DN["nt"], preferred_element_type=F32)
            _acc_add(i, dws_ref, jnp.where(mask, acc, 0.0)[None], rows=(slice(g, g + 1), slice(None), slice(None)))
        dvn = _spatial(ws_ref, dfb, _DN["tn"])
        _acc_add(i, dvg_ref, _colsum(dvn * nh))
        dnh = dvn * vgv
        parts = []
        for g in range(A_GROUPS):
            cs = slice(g * LANES, (g + 1) * LANES)
            dg_, ng = dnh[:, cs], nh[:, cs]
            parts.append(dg_ - jnp.mean(dg_, axis=-1, keepdims=True) - ng * jnp.mean(dg_ * ng, axis=-1, keepdims=True))
        dz_ref[:, W:] = (rs * jnp.concatenate(parts, axis=1) * _gelu_grad(v)).astype(BF16)

    return _rowcall(name, body, S, tm,
                    [(z, ("row", W, 0)), (z, ("row", W, 1)), (dyab, ("row", W, 0)), (vg, FULL), (ws, FULL),
                     (bias_full, FULL)],
                    [((S, 2 * W), BF16, ROW), ((A_GROUPS, CHUNK, CHUNK), F32, ACC), ((CHUNK, W), F32, ACC),
                     ((1, W), F32, ACC)])


HEAD = 64
N_HEADS = 8
BW = HEAD * N_HEADS
QB = 128
DILATIONS = (1, 4, 16)
QK_SCALE = HEAD ** -0.5


def _gsum64(x, ones_bd):
    x1 = x.astype(BF16)
    r1 = x - x1.astype(F32)
    x2 = r1.astype(BF16)
    x3 = (r1 - x2.astype(F32)).astype(BF16)
    dot = lambda t: jnp.dot(t, ones_bd, preferred_element_type=F32)
    return dot(x1) + dot(x2) + dot(x3)


def _swap32(x):
    n = x.shape[-1]
    up = pltpu.roll(x, n - HEAD // 2, axis=1)
    dn = pltpu.roll(x, HEAD // 2, axis=1)
    lane = lax.broadcasted_iota(jnp.int32, x.shape, 1)
    return jnp.where((lane % HEAD) < HEAD // 2, up, dn)


def _tile4(t):
    return jnp.concatenate([t] * (BW // LANES), axis=1)


def _stage_spec(tm):
    return pltpu.VMEM((BW // LANES, tm, LANES), F32)


def _to_classes(stage, x, dil):
    tm = x.shape[0]
    for j in range(BW // LANES):
        stage[j] = x[:, j * LANES:(j + 1) * LANES]
    return [jnp.concatenate([stage.at[j][pl.ds(r, tm // dil, stride=dil), :] for j in range(BW // LANES)], axis=1)
            for r in range(dil)]


def _from_classes(stage, cls, dil):
    rows = cls.shape[1]
    for r in range(dil):
        for j in range(BW // LANES):
            stage.at[j][pl.ds(r, rows, stride=dil), :] = cls[r, :, j * LANES:(j + 1) * LANES]
    return jnp.concatenate([stage[j] for j in range(BW // LANES)], axis=1)


def _cls_view(t, dil):
    return t if dil == 1 else t.reshape(dil, t.shape[0] // dil, t.shape[1])


def _cls_kind(dil, off=0):
    return ("off", off, None, 0) if dil == 1 else ("cls", dil, off)


def _cls_out(S, dil, dtype):
    return ((S, BW) if dil == 1 else (dil, S // dil, BW), dtype, _cls_kind(dil))


def _flat(t):
    return t.reshape(-1, t.shape[-1])


def _qkv_fwd(name, z, cos, sin, ones_bd, qg, kg, tm=256):
    S = z.shape[0]
    nd = len(DILATIONS)

    def body(q_ref, k_ref, v_ref, c_ref, s_ref, o_ref, qg_ref, kg_ref, *rest):
        outs, stage = rest[:3 * nd], rest[3 * nd]
        c, s, ob = _tile4(c_ref[...]), _tile4(s_ref[...]), o_ref[...]

        def norm_rope(x, g):
            r = lax.rsqrt(_gsum64(x * x, ob) * (1.0 / HEAD) + EPS)
            xn = x * r * g
            return xn * c + _swap32(xn) * s

        vals = [norm_rope(q_ref[...], qg_ref[...]) * QK_SCALE, norm_rope(k_ref[...], kg_ref[...]), v_ref[...]]
        for a, val in enumerate(vals):
            for b, dil in enumerate(DILATIONS):
                if dil == 1:
                    outs[nd * a + b][...] = val.astype(BF16)
                else:
                    for r, rows in enumerate(_to_classes(stage, val, dil)):
                        outs[nd * a + b][r] = rows.astype(BF16)

    outs = _rowcall(name, body, S, tm,
                    [(z, ("row", BW, 2)), (z, ("row", BW, 3)), (z, ("row", BW, 4)), (cos, ROW), (sin, ROW),
                     (ones_bd, FULL), (qg, FULL), (kg, FULL)],
                    [_cls_out(S, dil, BF16) for _ in range(3) for dil in DILATIONS], scratch=[_stage_spec(tm)])
    return [[_flat(outs[nd * a + b]) for a in range(3)] for b in range(nd)]


PAIR = 2 * HEAD


ATT_BLOCKS = 2
ATT_TM = ATT_BLOCKS * QB
ATT_PREV = ("prev", QB, None, 0)


def _key_rows(prev_ref, cur_ref, sb, ps):
    before = prev_ref[:, ps] if sb == 0 else cur_ref[(sb - 1) * QB:sb * QB, ps]
    return jnp.concatenate([before, cur_ref[sb * QB:(sb + 1) * QB, ps]], axis=0)


def _pair_scores(q_ref, kp_ref, kc_ref, sb, hp, half, seg_blocks):
    ps = slice(hp * PAIR, (hp + 1) * PAIR)
    mine = (lax.broadcasted_iota(jnp.int32, (1, PAIR), 1) >= HEAD) == (half == 1)
    qm = jnp.where(mine, q_ref[sb * QB:(sb + 1) * QB, ps], jnp.zeros((), BF16))
    kcat = _key_rows(kp_ref, kc_ref, sb, ps)
    s = lax.dot_general(qm, kcat, _DN["nt"], preferred_element_type=F32)
    qi = lax.broadcasted_iota(jnp.int32, (QB, 2 * QB), 0)
    kj = lax.broadcasted_iota(jnp.int32, (QB, 2 * QB), 1)
    has_prev = ((pl.program_id(0) * ATT_BLOCKS + sb) % seg_blocks) != 0
    valid = (kj >= qi) & (kj <= qi + QB) & ((kj >= QB) | has_prev)
    return mine, qm, kcat, s, valid


def _attn_fwd(name, q, k, v, dil):
    S = q.shape[0]
    seg_blocks = S // dil // QB

    def body(q_ref, kp_ref, kc_ref, vp_ref, vc_ref, o_ref, l_ref):
        for hp in range(N_HEADS // 2):
            ps = slice(hp * PAIR, (hp + 1) * PAIR)
            chains = [(sb, half) for sb in range(ATT_BLOCKS) for half in range(2)]
            sc = [_pair_scores(q_ref, kp_ref, kc_ref, sb, hp, half, seg_blocks) for sb, half in chains]
            ss = [jnp.where(valid, s, NEG) for _, _, _, s, valid in sc]
            ms = [jnp.max(s, axis=-1, keepdims=True) for s in ss]
            pv = [jnp.exp(s - m) for s, m in zip(ss, ms)]
            dens = [jnp.sum(p, axis=-1, keepdims=True) for p in pv]
            vcats = [_key_rows(vp_ref, vc_ref, sb, ps) for sb in range(ATT_BLOCKS)]
            outs = [jnp.dot(p.astype(BF16), vcats[sb], preferred_element_type=F32) / den
                    for p, den, (sb, _) in zip(pv, dens, chains)]
            lses = [jnp.broadcast_to(m + jnp.log(den), (QB, PAIR)) for m, den in zip(ms, dens)]
            for sb in range(ATT_BLOCKS):
                rows, upper = slice(sb * QB, (sb + 1) * QB), sc[2 * sb + 1][0]
                o_ref[rows, ps] = jnp.where(upper, outs[2 * sb + 1], outs[2 * sb])
                l_ref[rows, ps] = jnp.where(upper, lses[2 * sb + 1], lses[2 * sb])

    return _rowcall(name, body, S, ATT_TM, [(q, ROW), (k, ATT_PREV), (k, ROW), (v, ATT_PREV), (v, ROW)],
                    [((S, BW), F32, ROW)] * 2)


def _attn_bwd(name, q, k, v, do, lse, delta, dil):
    S = q.shape[0]
    seg_blocks = S // dil // QB

    def body(q_ref, kp_ref, kc_ref, vp_ref, vc_ref, do_ref, l_ref, dl_ref, dq_ref, dkc_ref, dkp_ref, dvc_ref, dvp_ref):
        for hp in range(N_HEADS // 2):
            ps = slice(hp * PAIR, (hp + 1) * PAIR)
            chains = [(sb, half) for sb in range(ATT_BLOCKS) for half in range(2)]
            rows = [slice(sb * QB, (sb + 1) * QB) for sb, _ in chains]
            cols = [hp * PAIR + half * HEAD for _, half in chains]
            sc = [_pair_scores(q_ref, kp_ref, kc_ref, sb, hp, half, seg_blocks) for sb, half in chains]
            pv = [jnp.where(valid, jnp.exp(s - l_ref[r, c:c + 1]), 0.0) for (_, _, _, s, valid), r, c in zip(sc, rows, cols)]
            vcats = [_key_rows(vp_ref, vc_ref, sb, ps) for sb in range(ATT_BLOCKS)]
            doms = [jnp.where(mine, do_ref[r, ps].astype(BF16), jnp.zeros((), BF16)) for (mine, *_), r in zip(sc, rows)]
            dps = [lax.dot_general(dom, vcats[sb], _DN["nt"], preferred_element_type=F32) for dom, (sb, _) in zip(doms, chains)]
            dss = [(p * (dp - dl_ref[r, c:c + 1])).astype(BF16) for p, dp, r, c in zip(pv, dps, rows, cols)]
            dqs = [jnp.dot(ds, kcat, preferred_element_type=F32) for ds, (_, _, kcat, _, _) in zip(dss, sc)]
            dks = [lax.dot_general(ds, qm, _DN["tn"], preferred_element_type=F32) for ds, (_, qm, *_) in zip(dss, sc)]
            dvs = [lax.dot_general(p.astype(BF16), dom, _DN["tn"], preferred_element_type=F32) for p, dom in zip(pv, doms)]
            for sb in range(ATT_BLOCKS):
                lo, hi = 2 * sb, 2 * sb + 1
                dk, dv = dks[lo] + dks[hi], dvs[lo] + dvs[hi]
                dq_ref[rows[lo], ps] = jnp.where(sc[hi][0], dqs[hi], dqs[lo])
                dkp_ref[rows[lo], ps] = dk[:QB]
                dkc_ref[rows[lo], ps] = dk[QB:]
                dvp_ref[rows[lo], ps] = dv[:QB]
                dvc_ref[rows[lo], ps] = dv[QB:]

    return _rowcall(name, body, S, ATT_TM,
                    [(q, ROW), (k, ATT_PREV), (k, ROW), (v, ATT_PREV), (v, ROW), do, (lse, ROW), (delta, ROW)],
                    [((S, BW), F32, ROW)] * 5)


def _merge_fwd(name, branches, tm=256):
    S = branches[0][0].shape[0]
    nd = len(DILATIONS)

    def body(*refs):
        ins, (y_ref, yb_ref), l_refs, stage = refs[:2 * nd], refs[2 * nd:2 * nd + 2], refs[2 * nd + 2:3 * nd + 2], refs[-1]
        os_, ls = [], []
        for b, dil in enumerate(DILATIONS):
            o, l = ins[2 * b][...], ins[2 * b + 1][...]
            os_.append(o if dil == 1 else _from_classes(stage, o, dil))
            ls.append(l if dil == 1 else _from_classes(stage, l, dil))
        m = functools.reduce(jnp.maximum, ls)
        es = [jnp.exp(l - m) for l in ls]
        den = functools.reduce(lambda a, e: a + e, es)
        y = functools.reduce(lambda a, t: a + t, [e * o for e, o in zip(es, os_)]) / den
        y_ref[...] = y
        yb_ref[...] = y.astype(BF16)
        lse = m + jnp.log(den)
        for b, dil in enumerate(DILATIONS):
            if dil == 1:
                l_refs[b][...] = lse
            else:
                for r, rows in enumerate(_to_classes(stage, lse, dil)):
                    l_refs[b][r] = rows

    ins = [(_cls_view(t, dil), _cls_kind(dil)) for pair, dil in zip(branches, DILATIONS) for t in pair]
    outs = _rowcall(name, body, S, tm, ins,
                    [((S, BW), F32, ROW), ((S, BW), BF16, ROW)] + [_cls_out(S, dil, F32) for dil in DILATIONS],
                    scratch=[_stage_spec(tm)])
    return outs[0], outs[1], [_flat(t) for t in outs[2:]]


def _delta(name, dyab, yb, ones_bd, tm=256):
    S = yb.shape[0]
    nd = len(DILATIONS)

    def body(d_ref, y_ref, o_ref, *rest):
        dl_refs, do_refs, stage = rest[:nd], rest[nd:2 * nd - 1], rest[-1]
        d = d_ref[...]
        dl = _gsum64(d * y_ref[...], o_ref[...])
        for b, dil in enumerate(DILATIONS):
            if dil == 1:
                dl_refs[b][...] = dl
            else:
                for r, rows in enumerate(_to_classes(stage, dl, dil)):
                    dl_refs[b][r] = rows
                for r, rows in enumerate(_to_classes(stage, d, dil)):
                    do_refs[b - 1][r] = rows.astype(BF16)

    outs = _rowcall(name, body, S, tm, [(dyab, ("row", BW, 1)), (yb, ROW), (ones_bd, FULL)],
                    [_cls_out(S, dil, F32) for dil in DILATIONS] + [_cls_out(S, dil, BF16) for dil in DILATIONS[1:]],
                    scratch=[_stage_spec(tm)])
    return [_flat(t) for t in outs[:nd]], [_flat(t) for t in outs[nd:]]


def _qkv_bwd(name, z, cos, sin, ones_bd, qg, kg, pieces):
    S = z.shape[0]
    nblk = S // QB

    def body(q_ref, k_ref, c_ref, s_ref, o_ref, qg_ref, kg_ref, *rest):
        pr, (dz_ref, dqg_ref, dkg_ref), stage = rest[:15], rest[15:18], rest[18]
        i = pl.program_id(0)
        c, s, ob = _tile4(c_ref[...]), _tile4(s_ref[...]), o_ref[...]
        dq = dk = dv = None
        for b, dil in enumerate(DILATIONS):
            a_q, a_kc, a_kp, a_vc, a_vp = [r[...] for r in pr[5 * b:5 * b + 5]]
            live = ((i + dil) < nblk).astype(F32)
            tq, tk, tv = a_q, a_kc + a_kp * live, a_vc + a_vp * live
            if dil > 1:
                tq, tk, tv = (_from_classes(stage, t, dil) for t in (tq, tk, tv))
            dq, dk, dv = (tq, tk, tv) if b == 0 else (dq + tq, dk + tk, dv + tv)

        def back(x, g, d_rot, acc_ref):
            r = lax.rsqrt(_gsum64(x * x, ob) * (1.0 / HEAD) + EPS)
            n = x * r
            dxn = d_rot * c + _swap32(d_rot * s)
            _acc_add(i, acc_ref, _colsum(dxn * n))
            dn = dxn * g
            return r * (dn - n * (_gsum64(dn * n, ob) * (1.0 / HEAD)))

        dz_ref[:, :BW] = back(q_ref[...], qg_ref[...], dq * QK_SCALE, dqg_ref).astype(BF16)
        dz_ref[:, BW:2 * BW] = back(k_ref[...], kg_ref[...], dk, dkg_ref).astype(BF16)
        dz_ref[:, 2 * BW:] = dv.astype(BF16)

    ins = [(z, ("row", BW, 2)), (z, ("row", BW, 3)), (cos, ROW), (sin, ROW), (ones_bd, FULL), (qg, FULL), (kg, FULL)]
    for piece, dil in zip(pieces, DILATIONS):
        a_q, a_kc, a_kp, a_vc, a_vp = (_cls_view(t, dil) for t in piece)
        own, prev = _cls_kind(dil), _cls_kind(dil, dil)
        ins += [(a_q, own), (a_kc, own), (a_kp, prev), (a_vc, own), (a_vp, prev)]
    return _rowcall(name, body, S, QB, ins,
                    [((S, 3 * BW), BF16, ROW), ((1, BW), F32, ACC), ((1, BW), F32, ACC)], scratch=[_stage_spec(QB)])


def _local_step(x0, tgt, pos, mod, wb, sp, pipe):
    S, D = x0.shape
    md = lambda l, j: mod[l, j:j + 1]
    sh_m, sc_m, g_m, sh_f, sc_f, g_f = ([md(l, j) for l in range(2)] for j in range(6))
    nm_g, nf_g = sp["norm_mix_g"], sp["norm_ffn_g"]

    inv_freq = 1.0 / (ROPE_THETA ** (jnp.arange(0, HEAD, 2, dtype=F32) / HEAD))
    ang = pos.astype(F32)[:, None] * inv_freq
    cs, sn = jnp.cos(ang), jnp.sin(ang)
    cos = jnp.concatenate([cs, cs, cs, cs], axis=1)
    sin = jnp.concatenate([-sn, sn, -sn, sn], axis=1)
    head_of = jnp.arange(BW) // HEAD
    ones_bd = (head_of[:, None] == head_of[None, :]).astype(BF16)
    qg = jnp.tile(sp["b_q_norm_g"].reshape(1, HEAD), (1, N_HEADS))
    kg = jnp.tile(sp["b_k_norm_g"].reshape(1, HEAD), (1, N_HEADS))
    vg = sp["a_vnorm_g"].reshape(1, A_GROUPS * LANES)
    ws = sp["a_spatial_w"][0]
    bias_full = jnp.repeat(sp["a_spatial_b"][0].T, LANES, axis=1)
    ffn_s = [(sp["ffn_dw_w"][l], sp["ffn_dw_b"][l:l + 1]) for l in range(2)]

    h0 = _mod_first("l0_mod", x0, nm_g[0:1], sc_m[0], sh_m[0])
    z = _matmul("l0_in", h0, wb.get("w_in", h0), "nn", F32, tm=2048)
    ya = _mixa_fwd("l0_mixa", z, vg, ws, bias_full)
    qkv = _qkv_fwd("l0_qkv", z, cos, sin, ones_bd, qg, kg)
    branches = [_attn_fwd(f"l0_att{dil}", *qkv[b], dil) for b, dil in enumerate(DILATIONS)]
    yb, yb16, lses = _merge_fwd("l0_merge", branches)
    yab = jnp.concatenate([ya, yb16], axis=1)
    y0 = _matmul("l0_out", yab, wb.get("w_out", yab), "nn", F32, tm=1024, tn=1024)
    x1, h1 = _resid_mod("l0_res1", x0, y0, g_m[0], nf_g[0:1], sc_f[0], sh_f[0])
    ffn_w = [(wb.get("up0", h1), wb.get("dn0", h1), *ffn_s[0])]
    f0, saved0 = _ffn_fwd("l0_ffn", h1, *ffn_w[0])
    x2, h2 = _resid_mod("l0_res2", x1, f0, g_f[0], nm_g[1:2], sc_m[1], sh_m[1])
    p = _matmul("l1_pw1", h2, wb.get("pw1", h2), "nn", F32, tm=2048, bias=sp["conv_pw1_b"])
    y2 = _glu31_fwd("l1_glu", p, sp["conv_dw_w"][0], sp["conv_dw_b"])
    y4 = _ln_silu_fwd("l1_ln", y2, sp["conv_ln_g"], sp["conv_ln_b"])
    y1 = _matmul("l1_pw2", y4, wb.get("pw2", y4), "nn", F32, tm=1024, tn=1024, bias=sp["conv_pw2_b"])
    x3, h3 = _resid_mod("l1_res1", x2, y1, g_m[1], nf_g[1:2], sc_f[1], sh_f[1])
    ffn_w.append((wb.get("up1", h3), wb.get("dn1", h3), *ffn_s[1]))
    f1, saved1 = _ffn_fwd("l1_ffn", h3, *ffn_w[1])
    dx4, lossv, dy, dgate_f1, _ = _loss_head("loss", x3, f1, g_f[1], tgt)

    dh, gf1 = _ffn_bwd("l1_ffn", dy, h3, *saved1, *ffn_w[1])
    tok = pipe.scatter("g1", dict(dn1=gf1["dn"], up1=gf1["up"]))
    dx3, dsh_f1, dsc_f1, dnf1, dy, dgate_m1, dpw2_b = _mod_bwd("l1_dmod2", dx4, dh, x3, nf_g[1:2], sc_f[1] + tok,
                                                             y1, g_m[1])
    dy4 = _matmul("l1_dpw2_x", dy, wb.get("pw2"), "nt", F32, tm=1024, tn=1024)
    g_pw2 = _matmul("l1_dpw2_w", y4, dy, "tn", BF16, tm=1024, tn=1024, tk=2048)
    dy2, dln_g, dln_b = _ln_silu_bwd("l1_dln", y2, dy4, sp["conv_ln_g"], sp["conv_ln_b"])
    dp, ddw_w, ddw_b, dpw1_b = _glu31_bwd("l1_dglu", p, dy2, sp["conv_dw_w"][0])
    g_pw1 = _matmul("l1_dpw1_w", h2, dp, "tn", BF16, tm=1024, tn=1024, tk=2048)
    tok = pipe.scatter("g2", dict(pw2=g_pw2, pw1=g_pw1))
    pipe.collect("g1", g_pw1)
    dh = _matmul("l1_dpw1_x", dp, wb.get("pw1"), "nt", F32, tm=1024, tn=1024, tk=2048)
    dx2, dsh_m1, dsc_m1, dnm1, dy, dgate_f0, _ = _mod_bwd("l1_dmod1", dx3, dh, x2, nm_g[1:2], sc_m[1] + tok,
                                                        f0, g_f[0])
    dh, gf0 = _ffn_bwd("l0_ffn", dy, h1, *saved0, *ffn_w[0])
    tok = pipe.scatter("g3", dict(dn0=gf0["dn"], up0=gf0["up"]))
    pipe.collect("g2", gf0["up"])
    dx1, dsh_f0, dsc_f0, dnf0, dy, dgate_m0, _ = _mod_bwd("l0_dmod2", dx2, dh, x1, nf_g[0:1], sc_f[0] + tok,
                                                        y0, g_m[0])
    dyab = _matmul("l0_dout_x", dy, wb.get("w_out"), "nt", F32, tm=1024, tn=1024)
    g_out = _matmul("l0_dout_w", yab, dy, "tn", BF16, tm=1024, tn=1024, tk=2048)
    vg = vg + pipe.scatter("g4", dict(w_out=g_out))
    dza, dws, dbf, dvg = _mixa_bwd("l0_dmixa", z, dyab, vg, ws, bias_full)
    deltas, dos = _delta("l0_delta", dyab, yb, ones_bd)
    pieces = []
    for b, dil in enumerate(DILATIONS):
        do = (dyab, ("row", BW, 1)) if dil == 1 else (dos[b - 1], ROW)
        pieces.append(_attn_bwd(f"l0_datt{dil}", *qkv[b], do, lses[b], deltas[b], dil))
    dzb, dqg, dkg = _qkv_bwd("l0_dqkv", z, cos, sin, ones_bd, qg, kg, pieces)
    small_tok = _start_small_grads(pipe, dzb, sp, lossv, ((dgate_m0, dsh_f0, dsc_f0, dgate_f0),
                                                   (dsh_m1, dsc_m1, dgate_m1, dsh_f1, dsc_f1, dgate_f1)),
                                   (dnm1, dnf0, dnf1), dvg, dws, dbf, dqg, dkg, dpw1_b, ddw_w, ddw_b, dln_g, dln_b,
                                   dpw2_b, gf0, gf1)
    dz = jnp.concatenate([dza, dzb], axis=1)
    g_in = _matmul("l0_din_w", h0, dz, "tn", BF16, tm=1024, tn=1280, tk=2048, after=small_tok)
    tok = pipe.scatter("g5", dict(w_in=g_in))
    dh = _matmul("l0_din_x", dz, wb.get("w_in"), "nt", F32, tm=1024, tn=1024, tk=2560, after=small_tok)
    grad_x, dsh_m0, dsc_m0, dnm0 = _mod_bwd("l0_dmod1", dx1, dh, x0, nm_g[0:1], sc_m[0] + tok)
    return lossv, grad_x, (dsh_m0, dsc_m0, dnm0)


def _start_small_grads(pipe, after, sp, lossv, mods, norms, dvg, dws, dbf, dqg, dkg, dpw1_b, ddw_w, ddw_b, dln_g,
                       dln_b, dpw2_b, gf0, gf1):
    (dgate_m0, dsh_f0, dsc_f0, dgate_f0), (dsh_m1, dsc_m1, dgate_m1, dsh_f1, dsc_f1, dgate_f1) = mods
    dnm1, dnf0, dnf1 = norms
    zero = jnp.zeros_like(dnm1)
    dmod = jnp.stack([jnp.concatenate([zero, zero, dgate_m0, dsh_f0, dsc_f0, dgate_f0], axis=0),
                      jnp.concatenate([dsh_m1, dsc_m1, dgate_m1, dsh_f1, dsc_f1, dgate_f1], axis=0)])
    small = dict(
        norm_mix_g=jnp.concatenate([zero, dnm1], axis=0),
        norm_ffn_g=jnp.concatenate([dnf0, dnf1], axis=0),
        a_vnorm_g=dvg.reshape(1, A_GROUPS, LANES),
        a_spatial_w=dws[None],
        a_spatial_b=dbf.reshape(CHUNK, A_GROUPS, LANES).sum(-1).T[None],
        b_q_norm_g=dqg.reshape(N_HEADS, HEAD).sum(0)[None],
        b_k_norm_g=dkg.reshape(N_HEADS, HEAD).sum(0)[None],
        conv_pw1_b=dpw1_b, conv_dw_w=ddw_w[None, :sp["conv_dw_w"].shape[1]], conv_dw_b=ddw_b,
        conv_ln_g=dln_g, conv_ln_b=dln_b, conv_pw2_b=dpw2_b,
        ffn_dw_w=jnp.stack([gf0["dw_w"], gf1["dw_w"]]),
        ffn_dw_b=jnp.concatenate([gf0["dw_b"], gf1["dw_b"]], axis=0),
    )
    return pipe.start_small(dmod, lossv, small, after)


ADA_TN = 512


def _ada_fwd(name, c_all, ada_w, ada_b_sh):
    L, D, N = ada_w.shape
    B = c_all.shape[0]

    def body(c_ref, w_ref, b_ref, o_ref):
        cv = c_ref[...]
        ca = (cv * _sigmoid(cv)).astype(BF16)
        o_ref[0] = jnp.dot(ca, w_ref[0].astype(BF16), preferred_element_type=F32) + b_ref[0]

    return pl.pallas_call(
        body, name=name, grid=(L, N // ADA_TN),
        in_specs=[pl.BlockSpec((B, D), lambda l, j: (0, 0)), pl.BlockSpec((1, D, ADA_TN), lambda l, j: (l, 0, j)),
                  pl.BlockSpec((1, 1, ADA_TN), lambda l, j: (l, 0, j))],
        out_specs=pl.BlockSpec((1, B, ADA_TN), lambda l, j: (l, 0, j)),
        out_shape=jax.ShapeDtypeStruct((L, B, N), F32),
        compiler_params=_params(("parallel", "parallel")),
    )(c_all, ada_w, ada_b_sh.reshape(L, 1, N))


def _adamw_val(w, g, m, v):
    m2 = ADAM_B1 * m + (1.0 - ADAM_B1) * g
    v2 = ADAM_B2 * v + (1.0 - ADAM_B2) * (g * g)
    m_hat = m2 / (1.0 - ADAM_B1 ** ADAM_STEP)
    v_hat = v2 / (1.0 - ADAM_B2 ** ADAM_STEP)
    delta = -ADAM_LR * (m_hat / (jnp.sqrt(v_hat) + ADAM_EPS) + ADAM_WD * w)
    return delta, m2, v2


def _ada_update(name, c_all, dmod_sh, w, m, v):
    L, D, N = w.shape
    B = c_all.shape[0]

    def body(c_ref, d_ref, w_ref, m_ref, v_ref, g_ref, dl_ref, mo_ref, vo_ref):
        cv = c_ref[...]
        ca = (cv * _sigmoid(cv)).astype(BF16)
        g = lax.dot_general(ca, d_ref[0].astype(BF16), _DN["tn"], preferred_element_type=F32)
        g_ref[0] = g
        dl_ref[0], mo_ref[0], vo_ref[0] = _adamw_val(w_ref[0], g, m_ref[0], v_ref[0])

    wspec = pl.BlockSpec((1, D, ADA_TN), lambda l, j: (l, 0, j))
    return pl.pallas_call(
        body, name=name, grid=(L, N // ADA_TN),
        in_specs=[pl.BlockSpec((B, D), lambda l, j: (0, 0)), pl.BlockSpec((1, B, ADA_TN), lambda l, j: (l, 0, j)),
                  wspec, wspec, wspec],
        out_specs=[wspec] * 4, out_shape=[jax.ShapeDtypeStruct((L, D, N), F32)] * 4,
        compiler_params=_params(("parallel", "parallel")),
    )(c_all, dmod_sh, w, m, v)


def _adamw(name, w, g, m, v):
    R, C = w.shape
    tm = R
    for cand in (256, 128, 64, 32, 16, 8):
        if R % cand == 0 and cand * C * 4 <= (1 << 20):
            tm = cand
            break

    def body(w_ref, g_ref, m_ref, v_ref, d_ref, mo_ref, vo_ref):
        d_ref[...], mo_ref[...], vo_ref[...] = _adamw_val(w_ref[...], g_ref[...], m_ref[...], v_ref[...])

    return _rowcall(name, body, R, tm, [(w, ROW), (g, ROW), (m, ROW), (v, ROW)], [((R, C), F32, ROW)] * 3)


def _row_tile(rows, width, itemsize=4, limit=1 << 20):
    for cand in range(512, 0, -16):
        if rows % cand == 0 and cand * width * itemsize <= limit:
            return cand
    raise ValueError((rows, width))


def _cast_into_full(name, a, layer, q, kind, after):
    L, r, c = a.shape
    tm = _row_tile(r, c)
    if kind == "col":
        full, o_spec = (r, N_CHIPS * c), pl.BlockSpec((tm, c), lambda i, q_ref: (i, q_ref[0]))
    else:
        full, o_spec = (N_CHIPS * r, c), pl.BlockSpec((tm, c), lambda i, q_ref: (q_ref[0] * (r // tm) + i, 0))

    def body(q_ref, a_ref, after_ref, o_ref):
        o_ref[...] = a_ref[0].astype(BF16)

    return pl.pallas_call(
        body, name=name,
        grid_spec=pltpu.PrefetchScalarGridSpec(
            num_scalar_prefetch=1, grid=(r // tm,),
            in_specs=[pl.BlockSpec((1, tm, c), lambda i, q_ref: (layer, i, 0)), ANY], out_specs=o_spec),
        out_shape=jax.ShapeDtypeStruct(full, BF16), compiler_params=_params(("parallel",)),
    )(q.reshape(1).astype(jnp.int32), a, after)


def _sum4(name, g, rcv, q, kind, n):
    r, c = rcv.shape[1:]
    tm = _row_tile(r, c)
    if kind == "col":
        g_spec = pl.BlockSpec((tm, n), lambda i, q_ref: (i, q_ref[0]))
    else:
        g_spec = pl.BlockSpec((tm, c), lambda i, q_ref: (q_ref[0] * (n // tm) + i, 0))

    def body(q_ref, g_ref, r_ref, o_ref):
        acc = g_ref[...].astype(F32)
        for j in range(3):
            acc = acc + r_ref[j].astype(F32)
        o_ref[...] = acc

    return pl.pallas_call(
        body, name=name,
        grid_spec=pltpu.PrefetchScalarGridSpec(
            num_scalar_prefetch=1, grid=(r // tm,),
            in_specs=[g_spec, pl.BlockSpec((3, tm, c), lambda i, q_ref: (0, i, 0))],
            out_specs=pl.BlockSpec((tm, c), lambda i, q_ref: (i, 0))),
        out_shape=jax.ShapeDtypeStruct((r, c), F32), compiler_params=_params(("parallel",)),
    )(q.reshape(1).astype(jnp.int32), g, rcv)


def _adamw_sum(name, w, m, v, layer, mine, theirs, prev):
    L, r, c = w.shape
    tm = _row_tile(r, c, limit=3 << 19)
    lay = pl.BlockSpec((1, tm, c), lambda i: (layer, i, 0))
    flat = pl.BlockSpec((tm, c), lambda i: (i, 0))
    n_prev = 0 if prev is None else 4

    def body(w_ref, m_ref, v_ref, a_ref, b_ref, *rest):
        g_ref, d_ref, mo_ref, vo_ref = rest[n_prev:]
        g = a_ref[...] + b_ref[...]
        g_ref[0] = g
        d_ref[0], mo_ref[0], vo_ref[0] = _adamw_val(w_ref[0], g, m_ref[0], v_ref[0])

    return pl.pallas_call(
        body, name=name, grid=(r // tm,),
        in_specs=[lay, lay, lay, flat, flat] + [ANY] * n_prev, out_specs=[lay] * 4,
        out_shape=[jax.ShapeDtypeStruct((L, r, c), F32)] * 4,
        input_output_aliases={5 + k: k for k in range(n_prev)},
        compiler_params=_params(("parallel",)),
    )(w, m, v, mine, theirs, *(prev or ()))


def _sum8(name, gathered, own=None):
    R, N = gathered.shape
    P = R // 8

    def body(g_ref, *rest):
        o_ref = rest[-1]
        me = 4 * lax.axis_index("x") + 2 * lax.axis_index("y") + lax.axis_index("c")
        acc = None
        for d in range(8):
            blk = g_ref[d * P:(d + 1) * P, :]
            if own is not None:
                blk = jnp.where(me == d, rest[0][...], blk)
            acc = blk if d == 0 else acc + blk
        o_ref[...] = acc

    return pl.pallas_call(body, name=name, out_shape=jax.ShapeDtypeStruct((P, N), F32),
                          compiler_params=pltpu.CompilerParams(vmem_limit_bytes=VMEM_LIMIT),
                          )(gathered, *(() if own is None else (own,)))


ANY = pl.BlockSpec(memory_space=pl.ANY)


def _mesh_pos():
    x, y, c = lax.axis_index("x"), lax.axis_index("y"), lax.axis_index("c")
    other_chips = [(1 - x, y), (x, 1 - y), (1 - x, 1 - y)]
    return x, y, c, other_chips


def _allgather8(name, blk, after=None):
    m_per, n = blk.shape

    def body(x_ref, *rest):
        out_ref, send_sems, recv_sems, local_sem = rest[after is not None:]
        x, y, c, chips = _mesh_pos()
        me, sibling = (x, y, c), (x, y, 1 - c)

        def rows(px, py, pc):
            return out_ref.at[pl.ds((4 * px + 2 * py + pc) * m_per, m_per), :]

        def copy(k, block, to, src=None):
            return pltpu.make_async_remote_copy(
                src_ref=rows(*block) if src is None else src, dst_ref=rows(*block),
                send_sem=send_sems.at[k], recv_sem=recv_sems.at[k], device_id=to, device_id_type=MESH)

        mine = pltpu.make_async_copy(x_ref, rows(*me), local_sem)
        mine.start()
        first = [copy(0, me, sibling, src=x_ref)]
        first += [copy(1 + j, me, (*chip, c), src=x_ref) for j, chip in enumerate(chips)]
        for cp in first:
            cp.start()
        passed = [copy(4 + j, (*chip, c), sibling) for j, chip in enumerate(chips)]
        for j, chip in enumerate(chips):
            copy(1 + j, (*chip, c), me).wait_recv()
            passed[j].start()
        copy(0, sibling, me).wait_recv()
        for j, chip in enumerate(chips):
            copy(4 + j, (*chip, 1 - c), me).wait_recv()
        for cp in first + passed:
            cp.wait_send()
        mine.wait()

    return pl.pallas_call(
        body, name=name, out_shape=jax.ShapeDtypeStruct((8 * m_per, n), blk.dtype),
        in_specs=[pl.BlockSpec(memory_space=pltpu.VMEM)] + [ANY] * (after is not None),
        out_specs=pl.BlockSpec(memory_space=pltpu.VMEM),
        scratch_shapes=[pltpu.SemaphoreType.DMA((7,)), pltpu.SemaphoreType.DMA((7,)), pltpu.SemaphoreType.DMA],
        compiler_params=pltpu.CompilerParams(vmem_limit_bytes=VMEM_LIMIT),
    )(blk, *(() if after is None else (after,)))


BIG = dict(w_in=("col", "ab_w_in", 0), w_out=("row", "ab_w_out", 0), up0=("col", "ffn_up_w", 0),
           dn0=("row", "ffn_down_w", 0), pw1=("col", "conv_pw1_w", 0), pw2=("row", "conv_pw2_w", 0),
           up1=("col", "ffn_up_w", 1), dn1=("row", "ffn_down_w", 1))
N_CHIPS = 4
HBM = pl.BlockSpec(memory_space=pltpu.HBM)
SEM = pl.BlockSpec(memory_space=pltpu.SEMAPHORE)
EFFECT = pltpu.SideEffectType.DATAFLOW_SIDE_EFFECTING


def _region(kind, ref, q, n):
    if kind == "col":
        return ref.at[:, pl.ds(q * n, n)]
    return ref.at[pl.ds(q * n, n), :]


def _gather_plan(kind, n):
    def remote(src, land, pos):
        x, y, c, chips = pos
        mine = _region(kind, land, 2 * x + y, n)
        return [(mine, mine, (*chip, c)) for chip in chips]

    return ("gather", kind, n), remote


def _scatter_plan(kind, n):
    def remote(src, land, pos):
        _, _, c, chips = pos
        return [(_region(kind, src, 2 * chip[0] + chip[1], n), land.at[j], (*chip, c)) for j, chip in enumerate(chips)]

    return ("scatter", kind, n), remote


def _everyone_plan(rows):
    def remote(src, land, pos):
        x, y, c, _ = pos
        mine = land.at[pl.ds((4 * x + 2 * y + c) * rows, rows), :]
        flip = lambda v, bit: 1 - v if bit else v
        return [(src, mine, (flip(x, k & 4), flip(y, k & 2), flip(c, k & 1))) for k in range(1, 8)]

    return ("everyone", rows), remote


def _sibling_plan():
    def remote(src, land, pos):
        x, y, c, _ = pos
        return [(src, land, (x, y, 1 - c))]

    return ("sibling",), remote


def _split_start(name, items, after=None):
    n = len(items)
    plans = [it[2] for it in items]
    n_in = 2 * n + (after is not None)

    def body(*refs):
        srcs, lands = refs[:n], refs[n:2 * n]
        sends, recvs = refs[n_in:n_in + n], refs[n_in + n:n_in + 2 * n]
        token = refs[n_in + 4 * n]
        pos = _mesh_pos()
        for a, (_, remote) in enumerate(plans):
            for k, (s, d, dev) in enumerate(remote(srcs[a], lands[a], pos)):
                pltpu.make_async_remote_copy(src_ref=s, dst_ref=d, send_sem=sends[a].at[k], recv_sem=recvs[a].at[k],
                                             device_id=dev, device_id_type=MESH).start()
        token[...] = jnp.zeros_like(token)

    sems = [pltpu.SemaphoreType.DMA((it[3],)) for it in items]
    bufs = [pltpu.HBM(it[k].shape, it[k].dtype) for k in (0, 1) for it in items]
    outs = pl.pallas_call(
        body, name=name, out_shape=[*sems, *sems, *bufs, jax.ShapeDtypeStruct((8, LANES), F32)],
        in_specs=[HBM] * (2 * n) + [ANY] * (n_in - 2 * n),
        out_specs=[SEM] * (2 * n) + [HBM] * (2 * n) + [pl.BlockSpec(memory_space=pltpu.VMEM)],
        input_output_aliases={i: 2 * n + i for i in range(2 * n)},
        compiler_params=pltpu.CompilerParams(has_side_effects=EFFECT),
    )(*[pltpu.with_memory_space_constraint(it[k], pltpu.HBM) for k in (0, 1) for it in items],
      *(() if after is None else (after,)))
    state = [(items[a][2], items[a][3], outs[2 * n + a], outs[3 * n + a], outs[a], outs[n + a]) for a in range(n)]
    return state, outs[4 * n]


def _split_wait(name, state, after):
    n = len(state)

    def body(*refs):
        srcs, lands = refs[:n], refs[n:2 * n]
        sends, recvs = refs[2 * n:3 * n], refs[3 * n:4 * n]
        pos = _mesh_pos()
        for a, ((_, remote), *_) in enumerate(state):
            for k, (s, d, dev) in enumerate(remote(srcs[a], lands[a], pos)):
                cp = pltpu.make_async_remote_copy(src_ref=s, dst_ref=d, send_sem=sends[a].at[k], recv_sem=recvs[a].at[k],
                                                  device_id=dev, device_id_type=MESH)
                cp.wait_send()
                cp.wait_recv()

    bufs = [st[k] for k in (2, 3) for st in state]
    outs = pl.pallas_call(
        body, name=name, out_shape=[pltpu.HBM(b.shape, b.dtype) for b in bufs],
        in_specs=[HBM] * (2 * n) + [SEM] * (2 * n) + [ANY], out_specs=[HBM] * (2 * n),
        input_output_aliases={i: i for i in range(2 * n)},
        compiler_params=pltpu.CompilerParams(has_side_effects=EFFECT),
    )(*bufs, *[st[k] for k in (4, 5) for st in state], after)
    return outs[:n], outs[n:]


class _Weights:
    def __init__(self, w, q, after):
        unused = jnp.zeros((16, LANES), BF16)

        def item(name, after):
            kind, pname, layer = BIG[name]
            _, r, c = w[pname].shape
            land = _cast_into_full(f"cast_{name}", w[pname], layer, q, kind, after)
            return unused, land, _gather_plan(kind, c if kind == "col" else r), N_CHIPS - 1

        first, *rest = BIG
        state1, token1 = _split_start("gw_start_first", [item(first, after)], after)
        state2, self.token = _split_start("gw_start_rest", [item(name, token1) for name in rest], token1)
        self.pending = dict(zip(BIG, state1 + state2))
        self.ready = {}

    def get(self, name, after=None):
        if name not in self.ready:
            self.ready[name] = _split_wait(f"gw_wait_{name}", [self.pending.pop(name)], after)[1][0]
        return self.ready[name]


class _GradPipe:
    def __init__(self, q, w, m, v):
        self.q, self.w, self.m, self.v = q, w, m, v
        self.stage, self.results = {}, {}

    def start_small(self, dmod, lossv, small, after):
        self.small_names = [n for n in REPLICATED if n != "ada_b"] + list(SMALL_SHARDED)
        payload = [dmod.reshape(2, -1), lossv] + [small[n] for n in self.small_names]
        self.small_shapes = [p.shape for p in payload]
        packed = _pack(payload)
        land = jnp.zeros((8 * PACK_ROWS, packed.shape[1]), F32)
        self.small_state, token = _split_start("ag_grads_start", [(packed, land, _everyone_plan(PACK_ROWS), 7)], after)
        return token

    def wait_small(self, after):
        srcs, lands = _split_wait("ag_grads_wait", self.small_state, after)
        return srcs[0], lands[0]

    def scatter(self, group, grads, after=None):
        items = []
        for name, g in grads.items():
            kind = BIG[name][0]
            rows, cols = g.shape
            n = (cols if kind == "col" else rows) // N_CHIPS
            reg = (rows, n) if kind == "col" else (n, cols)
            items.append((g, lax.empty((N_CHIPS - 1, *reg), BF16), _scatter_plan(kind, n), N_CHIPS - 1))
        state, token = _split_start(f"gs_start_{group}", items, after)
        self.stage[group] = (list(grads), state)
        return token[0, 0]

    def collect(self, group, after):
        names, state = self.stage[group]
        srcs, lands = _split_wait(f"gs_wait_{group}", state, after)
        items = []
        for name, st, g, land in zip(names, state, srcs, lands):
            _, kind, n = st[0][0]
            part = _sum4(f"sum_{name}", g, land, self.q, kind, n)
            items.append((part, lax.empty(part.shape, F32), _sibling_plan(), 1))
        state, token = _split_start(f"sw_start_{group}", items)
        self.stage[group] = (names, state)
        return token

    def finish(self, group, after):
        names, state = self.stage.pop(group)
        srcs, lands = _split_wait(f"sw_wait_{group}", state, after)
        for name, mine, theirs in zip(names, srcs, lands):
            _, pname, layer = BIG[name]
            self.results[pname] = _adamw_sum(f"adamw_{name}", self.w[pname], self.m[pname], self.v[pname], layer,
                                             mine, theirs, self.results.get(pname))


PACK_ROWS = 8


def _pack(arrays):
    flat = jnp.concatenate([a.reshape(-1) for a in arrays])
    n = flat.shape[0]
    padded = -(-n // (PACK_ROWS * LANES)) * (PACK_ROWS * LANES)
    return jnp.pad(flat, (0, padded - n)).reshape(PACK_ROWS, padded // PACK_ROWS)


def _unpack(packed, shapes):
    flat = packed.reshape(-1)
    out, off = [], 0
    for s in shapes:
        n = 1
        for d in s:
            n *= d
        out.append(flat[off:off + n].reshape(s))
        off += n
    return out


REPLICATED = ("ada_b", "norm_mix_g", "norm_ffn_g", "a_vnorm_g", "a_spatial_w", "a_spatial_b", "b_q_norm_g",
              "b_k_norm_g", "ffn_dw_b")
SMALL_SHARDED = ("conv_pw1_b", "conv_dw_w", "conv_dw_b", "conv_ln_g", "conv_ln_b", "conv_pw2_b", "ffn_dw_w")
WEIGHTS = ("ada_w", "ada_b", "norm_mix_g", "norm_ffn_g", "ab_w_in", "a_vnorm_g", "a_spatial_w", "a_spatial_b",
           "b_q_norm_g", "b_k_norm_g", "ab_w_out", "conv_pw1_w", "conv_pw1_b", "conv_dw_w", "conv_dw_b", "conv_ln_g",
           "conv_ln_b", "conv_pw2_w", "conv_pw2_b", "ffn_up_w", "ffn_dw_w", "ffn_dw_b", "ffn_down_w")

def kernel(x, c, positions, ada_w, ada_b, norm_mix_g, norm_ffn_g, ab_w_in, a_vnorm_g, a_spatial_w, a_spatial_b, b_q_norm_g, b_k_norm_g, ab_w_out, conv_pw1_w, conv_pw1_b, conv_dw_w, conv_dw_b, conv_ln_g, conv_ln_b, conv_pw2_w, conv_pw2_b, ffn_up_w, ffn_dw_w, ffn_dw_b, ffn_down_w, loss_target, m_ada_w, m_ada_b, m_norm_mix_g, m_norm_ffn_g, m_ab_w_in, m_a_vnorm_g, m_a_spatial_w, m_a_spatial_b, m_b_q_norm_g, m_b_k_norm_g, m_ab_w_out, m_conv_pw1_w, m_conv_pw1_b, m_conv_dw_w, m_conv_dw_b, m_conv_ln_g, m_conv_ln_b, m_conv_pw2_w, m_conv_pw2_b, m_ffn_up_w, m_ffn_dw_w, m_ffn_dw_b, m_ffn_down_w, v_ada_w, v_ada_b, v_norm_mix_g, v_norm_ffn_g, v_ab_w_in, v_a_vnorm_g, v_a_spatial_w, v_a_spatial_b, v_b_q_norm_g, v_b_k_norm_g, v_ab_w_out, v_conv_pw1_w, v_conv_pw1_b, v_conv_dw_w, v_conv_dw_b, v_conv_ln_g, v_conv_ln_b, v_conv_pw2_w, v_conv_pw2_b, v_ffn_up_w, v_ffn_dw_w, v_ffn_dw_b, v_ffn_down_w):
    w = dict(ada_w=ada_w, ada_b=ada_b, norm_mix_g=norm_mix_g, norm_ffn_g=norm_ffn_g, ab_w_in=ab_w_in, a_vnorm_g=a_vnorm_g, a_spatial_w=a_spatial_w, a_spatial_b=a_spatial_b, b_q_norm_g=b_q_norm_g, b_k_norm_g=b_k_norm_g, ab_w_out=ab_w_out, conv_pw1_w=conv_pw1_w, conv_pw1_b=conv_pw1_b, conv_dw_w=conv_dw_w, conv_dw_b=conv_dw_b, conv_ln_g=conv_ln_g, conv_ln_b=conv_ln_b, conv_pw2_w=conv_pw2_w, conv_pw2_b=conv_pw2_b, ffn_up_w=ffn_up_w, ffn_dw_w=ffn_dw_w, ffn_dw_b=ffn_dw_b, ffn_down_w=ffn_down_w)
    m = dict(ada_w=m_ada_w, ada_b=m_ada_b, norm_mix_g=m_norm_mix_g, norm_ffn_g=m_norm_ffn_g, ab_w_in=m_ab_w_in, a_vnorm_g=m_a_vnorm_g, a_spatial_w=m_a_spatial_w, a_spatial_b=m_a_spatial_b, b_q_norm_g=m_b_q_norm_g, b_k_norm_g=m_b_k_norm_g, ab_w_out=m_ab_w_out, conv_pw1_w=m_conv_pw1_w, conv_pw1_b=m_conv_pw1_b, conv_dw_w=m_conv_dw_w, conv_dw_b=m_conv_dw_b, conv_ln_g=m_conv_ln_g, conv_ln_b=m_conv_ln_b, conv_pw2_w=m_conv_pw2_w, conv_pw2_b=m_conv_pw2_b, ffn_up_w=m_ffn_up_w, ffn_dw_w=m_ffn_dw_w, ffn_dw_b=m_ffn_dw_b, ffn_down_w=m_ffn_down_w)
    v = dict(ada_w=v_ada_w, ada_b=v_ada_b, norm_mix_g=v_norm_mix_g, norm_ffn_g=v_norm_ffn_g, ab_w_in=v_ab_w_in, a_vnorm_g=v_a_vnorm_g, a_spatial_w=v_a_spatial_w, a_spatial_b=v_a_spatial_b, b_q_norm_g=v_b_q_norm_g, b_k_norm_g=v_b_k_norm_g, ab_w_out=v_ab_w_out, conv_pw1_w=v_conv_pw1_w, conv_pw1_b=v_conv_pw1_b, conv_dw_w=v_conv_dw_w, conv_dw_b=v_conv_dw_b, conv_ln_g=v_conv_ln_g, conv_ln_b=v_conv_ln_b, conv_pw2_w=v_conv_pw2_w, conv_pw2_b=v_conv_pw2_b, ffn_up_w=v_ffn_up_w, ffn_dw_w=v_ffn_dw_w, ffn_dw_b=v_ffn_dw_b, ffn_down_w=v_ffn_down_w)
    S, D = x.shape[1], x.shape[2]
    xi, yi, ci = lax.axis_index("x"), lax.axis_index("y"), lax.axis_index("c")
    q = 2 * xi + yi
    b = 2 * q + ci
    take_dev = lambda g: g.reshape(8, PACK_ROWS, -1)

    c_all = _allgather8("ag_c", c.reshape(PACK_ROWS, D // PACK_ROWS)).reshape(8, D)
    n_ada = ada_w.shape[2]
    mod_sh = _ada_fwd("ada_fwd", c_all, ada_w, lax.dynamic_slice_in_dim(ada_b, q * n_ada, n_ada, axis=1))
    sh_shapes = [mod_sh.shape] + [w[n].shape for n in SMALL_SHARDED]
    gathered_mod = _allgather8("ag_mod", _pack([mod_sh] + [w[n] for n in SMALL_SHARDED]))
    per_chip = [_unpack(blk, sh_shapes) for blk in take_dev(gathered_mod)[0::2]]
    mod_g = jnp.stack([pc[0] for pc in per_chip])
    mod_mine = lax.dynamic_index_in_dim(mod_g, b, axis=2, keepdims=False)
    mod = mod_mine.transpose(1, 0, 2).reshape(2, 6, D)
    sp = {n: jnp.concatenate([pc[1 + i] for pc in per_chip], axis=-1) for i, n in enumerate(SMALL_SHARDED)}
    sp.update({n: w[n] for n in REPLICATED if n != "ada_b"})

    wb = _Weights(w, q, gathered_mod)
    mod = mod + wb.token[0, 0]

    pipe = _GradPipe(q, w, m, v)
    lossv, grad_x, late = _local_step(x[0], loss_target[0], positions[0], mod, wb, sp, pipe)

    pipe.finish("g1", grad_x)
    pipe.finish("g2", pipe.results["ffn_up_w"][0])
    swapped = pipe.collect("g3", pipe.results["conv_pw1_w"][0])
    swapped = pipe.collect("g4", swapped)

    own, gathered = pipe.wait_small(swapped)
    totals = _unpack(_sum8("sum_grads", gathered, own), pipe.small_shapes)
    grads = dict(zip(["ada_b", "loss_columns"] + pipe.small_names, totals))
    loss = 0.5 * jnp.sum(grads.pop("loss_columns")) / D
    late_g = _allgather8("ag_late", _pack(list(late)), gathered)
    late_tot = _unpack(_sum8("sum_late", late_g), [(3, D)])[0]
    grads["ada_b"] = grads["ada_b"].at[0, :2 * D].add(late_tot[:2].reshape(-1))
    grads["norm_mix_g"] = grads["norm_mix_g"].at[0].add(late_tot[2])
    for n in SMALL_SHARDED:
        n_sh = w[n].shape[-1]
        grads[n] = lax.dynamic_slice_in_dim(grads[n], q * n_sh, n_sh, axis=grads[n].ndim - 1)
    dmod_of = lambda packed: packed.reshape(packed.shape[0] // PACK_ROWS, -1)[:, :2 * 6 * D].reshape(-1, 2, 6 * D)
    dmod_all = jnp.where((jnp.arange(8) == b)[:, None, None], dmod_of(own), dmod_of(gathered))
    late_all = take_dev(late_g).reshape(8, -1)[:, :3 * D].reshape(8, 3, D)
    dmod_all = dmod_all.at[:, 0, :2 * D].add(late_all[:, :2].reshape(8, 2 * D))
    dmod_sh = lax.dynamic_slice_in_dim(dmod_all, q * n_ada, n_ada, axis=2).transpose(1, 0, 2)

    pipe.finish("g3", dmod_sh)
    pipe.finish("g4", pipe.results["ffn_up_w"][0])
    grads["ada_w"], delta_ada, m_ada, v_ada = _ada_update("ada_update", c_all, dmod_sh, ada_w, m_ada_w, v_ada_w)
    delta, new_m, new_v = dict(ada_w=delta_ada), dict(ada_w=m_ada), dict(ada_w=v_ada)
    rest = list(REPLICATED) + list(SMALL_SHARDED)
    rest_shapes = [w[n].shape for n in rest]
    outs = _adamw("adamw_small", *[_pack([src[n].reshape(w[n].shape) for n in rest]) for src in (w, grads, m, v)])
    for tgt, packed in zip((delta, new_m, new_v), outs):
        tgt.update(dict(zip(rest, _unpack(packed, rest_shapes))))
    for n in rest:
        grads[n] = grads[n].reshape(w[n].shape)
    pipe.finish("g5", pipe.collect("g5", outs[0]))
    for n, res in pipe.results.items():
        grads[n], delta[n], new_m[n], new_v[n] = res

    return (loss, grad_x[None], *[grads[n] for n in WEIGHTS], *[delta[n] for n in WEIGHTS],
            *[new_m[n] for n in WEIGHTS], *[new_v[n] for n in WEIGHTS])
```

```python
import functools

import jax
import jax.numpy as jnp
from jax import lax
from jax.experimental import pallas as pl
from jax.experimental.pallas import tpu as pltpu

F32, BF16 = jnp.float32, jnp.bfloat16
EPS = 1e-6
NEG = -1e30
ROPE_THETA = 10000.0
LANES = 128
VMEM_LIMIT = 56 * 1024 * 1024
ADAM_LR, ADAM_B1, ADAM_B2, ADAM_EPS, ADAM_WD, ADAM_STEP = 0.001, 0.9, 0.999, 1e-08, 0.01, 10
MESH = pl.DeviceIdType.MESH


def _params(sem):
    return pltpu.CompilerParams(dimension_semantics=sem, vmem_limit_bytes=VMEM_LIMIT)


_DN = {"nn": (((1,), (0,)), ((), ())), "nt": (((1,), (1,)), ((), ())), "tn": (((0,), (0,)), ((), ()))}


def _matmul(name, a, b, mode, out_dtype, tm=512, tn=512, tk=1024, bias=None, after=None):
    if mode == "nn":
        (M, K), N = a.shape, b.shape[1]
    elif mode == "nt":
        (M, K), N = a.shape, b.shape[0]
    else:
        (K, M), N = a.shape, b.shape[1]
    tm, tn, tk = min(tm, M), min(tn, N), min(tk, K)
    assert M % tm == 0 and N % tn == 0 and K % tk == 0, (name, M, N, K, tm, tn, tk)
    nk = K // tk
    if mode == "tn":
        a_spec = pl.BlockSpec((tk, tm), lambda i, j, k: (k, i))
    else:
        a_spec = pl.BlockSpec((tm, tk), lambda i, j, k: (i, k))
    if mode == "nt":
        b_spec = pl.BlockSpec((tn, tk), lambda i, j, k: (j, k))
    else:
        b_spec = pl.BlockSpec((tk, tn), lambda i, j, k: (k, j))
    in_specs, args = [a_spec, b_spec], [a, b]
    if bias is not None:
        in_specs.append(pl.BlockSpec((1, tn), lambda i, j, k: (0, j)))
        args.append(bias)
    if after is not None:
        in_specs.append(pl.BlockSpec(memory_space=pl.ANY))
        args.append(after)
    n_in = len(args)

    def body(*refs):
        a_ref, b_ref, o_ref = refs[0], refs[1], refs[n_in]
        p = lax.dot_general(a_ref[...], b_ref[...], _DN[mode], preferred_element_type=F32)

        def finish(acc):
            if bias is not None:
                acc = acc + refs[2][...]
            o_ref[...] = acc.astype(o_ref.dtype)

        if nk == 1:
            finish(p)
        else:
            acc_ref = refs[n_in + 1]
            k = pl.program_id(2)

            @pl.when(k == 0)
            def _():
                acc_ref[...] = p

            @pl.when(k > 0)
            def _():
                acc_ref[...] += p

            @pl.when(k == nk - 1)
            def _():
                finish(acc_ref[...])

    return pl.pallas_call(
        body, name=name, grid=(M // tm, N // tn, nk), in_specs=in_specs,
        out_specs=pl.BlockSpec((tm, tn), lambda i, j, k: (i, j)),
        out_shape=jax.ShapeDtypeStruct((M, N), out_dtype),
        scratch_shapes=[pltpu.VMEM((tm, tn), F32)] if nk > 1 else [],
        compiler_params=_params(("parallel", "parallel", "arbitrary")),
    )(*args)


def _rowcall(name, body, nrows, tm, ins, outs, scratch=()):
    nblk = nrows // tm
    assert nrows % tm == 0

    def spec(kind, shape):
        k = kind[0]
        if k == "row":
            cw, cb = kind[1] or shape[-1], kind[2]
            return pl.BlockSpec((tm, cw), lambda i: (i, cb))
        if k == "prev":
            hb, cw, cb = kind[1], kind[2] or shape[-1], kind[3]
            r = tm // hb
            return pl.BlockSpec((hb, cw), lambda i: (jnp.maximum(i * r - 1, 0), cb))
        if k == "next":
            hb, cw, cb = kind[1], kind[2] or shape[-1], kind[3]
            r, last = tm // hb, nrows // hb - 1
            return pl.BlockSpec((hb, cw), lambda i: (jnp.minimum((i + 1) * r, last), cb))
        if k == "off":
            off, cw, cb = kind[1], kind[2] or shape[-1], kind[3]
            return pl.BlockSpec((tm, cw), lambda i: (jnp.clip(i + off, 0, nblk - 1), cb))
        if k == "cls":
            dil, off = kind[1], kind[2]
            return pl.BlockSpec((dil, tm // dil, shape[-1]), lambda i: (0, jnp.clip(i + off, 0, nblk - 1), 0))
        nd = len(shape)
        return pl.BlockSpec(tuple(shape), lambda i: (0,) * nd)

    has_acc = any(o[2][0] == "acc" for o in outs)
    return pl.pallas_call(
        body, name=name, grid=(nblk,),
        in_specs=[spec(kind, a.shape) for a, kind in ins],
        out_specs=[spec(kind, shape) for shape, _, kind in outs],
        out_shape=[jax.ShapeDtypeStruct(tuple(shape), dt) for shape, dt, _ in outs],
        scratch_shapes=list(scratch),
        compiler_params=_params(("arbitrary",) if has_acc else ("parallel",)),
    )(*[a for a, _ in ins])


ROW = ("row", None, 0)
FULL = ("full",)
ACC = ("acc",)


def _colsum(x):
    return jnp.sum(x, axis=0, keepdims=True)


def _acc_add(i, ref, val, rows=None):
    idx = (slice(None),) * len(ref.shape) if rows is None else rows

    @pl.when(i == 0)
    def _():
        ref[idx] = val

    @pl.when(i > 0)
    def _():
        ref[idx] = ref[idx] + val


def _sigmoid(x):
    return 1.0 / (1.0 + jnp.exp(-x))


def _gelu(x):
    return 0.5 * x * (1.0 + lax.erf(x * (2.0 ** -0.5)))


def _gelu_grad(x):
    return 0.5 * (1.0 + lax.erf(x * (2.0 ** -0.5))) + x * jnp.exp(-0.5 * x * x) * ((2.0 * jnp.pi) ** -0.5)


SUBLANES = 8


def _phases(ext, sign):
    n = ext.shape[0]
    return [ext if b == 0 else pltpu.roll(ext, b if sign > 0 else n - b, axis=0) for b in range(SUBLANES)]


def _shift_prev(phases, s, hb):
    a, b = divmod(s, SUBLANES)
    return phases[b][hb - SUBLANES * a:phases[b].shape[0] - SUBLANES * a]


def _shift_next(phases, s, tm):
    a, b = divmod(s, SUBLANES)
    return phases[b][SUBLANES * a:SUBLANES * a + tm]


def _rms_mod_val(x, g, sc, sh):
    r = lax.rsqrt(jnp.mean(x * x, axis=-1, keepdims=True) + EPS)
    return x * r * g * (1.0 + sc) + sh


def _mod_first(name, x, g, sc, sh, tm=512):
    S, D = x.shape

    def body(x_ref, g_ref, sc_ref, sh_ref, h_ref):
        h_ref[...] = _rms_mod_val(x_ref[...], g_ref[...], sc_ref[...], sh_ref[...]).astype(BF16)

    return _rowcall(name, body, S, tm, [(x, ROW), (g, FULL), (sc, FULL), (sh, FULL)], [((S, D), BF16, ROW)])[0]


def _resid_mod(name, x, y, gate, g, sc, sh, tm=512):
    S, D = x.shape

    def body(x_ref, y_ref, gate_ref, g_ref, sc_ref, sh_ref, xo_ref, h_ref):
        xn = x_ref[...] + gate_ref[...] * y_ref[...]
        xo_ref[...] = xn
        h_ref[...] = _rms_mod_val(xn, g_ref[...], sc_ref[...], sh_ref[...]).astype(BF16)

    return _rowcall(name, body, S, tm,
                    [(x, ROW), (y, ROW), (gate, FULL), (g, FULL), (sc, FULL), (sh, FULL)],
                    [((S, D), F32, ROW), ((S, D), BF16, ROW)])


def _gate_bwd_val(i, d, y_ref, gate_ref, dy_ref, dg_ref, db_ref):
    dy = d * gate_ref[...]
    dy_ref[...] = dy.astype(BF16)
    _acc_add(i, dg_ref, _colsum(d * y_ref[...]))
    _acc_add(i, db_ref, _colsum(dy))


GATE_OUTS = lambda S, D: [((S, D), BF16, ROW), ((1, D), F32, ACC), ((1, D), F32, ACC)]


def _loss_head(name, x, y, gate, tgt, tm=512):
    S, D = x.shape

    def body(x_ref, y_ref, gate_ref, t_ref, dx_ref, l_ref, dy_ref, dg_ref, db_ref):
        i = pl.program_id(0)
        err = x_ref[...] + gate_ref[...] * y_ref[...] - t_ref[...]
        d = err * (1.0 / D)
        dx_ref[...] = d
        _acc_add(i, l_ref, _colsum(err * err))
        _gate_bwd_val(i, d, y_ref, gate_ref, dy_ref, dg_ref, db_ref)

    return _rowcall(name, body, S, tm, [(x, ROW), (y, ROW), (gate, FULL), (tgt, ROW)],
                    [((S, D), F32, ROW), ((1, D), F32, ACC)] + GATE_OUTS(S, D))


def _mod_bwd(name, dxo, dh, x, g, sc, y=None, gate=None, tm=512):
    S, D = x.shape
    gated = y is not None

    def body(d_ref, dh_ref, x_ref, g_ref, sc_ref, *rest):
        dx_ref, dsh_ref, dsc_ref, dg_ref = rest[2 * gated:2 * gated + 4]
        i = pl.program_id(0)
        xv, dh_v, gv = x_ref[...], dh_ref[...], g_ref[...]
        r = lax.rsqrt(jnp.mean(xv * xv, axis=-1, keepdims=True) + EPS)
        n = xv * r
        _acc_add(i, dsh_ref, _colsum(dh_v))
        _acc_add(i, dsc_ref, _colsum(dh_v * (n * gv)))
        dy = dh_v * (1.0 + sc_ref[...])
        _acc_add(i, dg_ref, _colsum(dy * n))
        dn = dy * gv
        dx = d_ref[...] + r * (dn - n * jnp.mean(dn * n, axis=-1, keepdims=True))
        dx_ref[...] = dx
        if gated:
            _gate_bwd_val(i, dx, rest[0], rest[1], *rest[6:9])

    ins = [(dxo, ROW), (dh, ROW), (x, ROW), (g, FULL), (sc, FULL)] + ([(y, ROW), (gate, FULL)] if gated else [])
    outs = [((S, D), F32, ROW), ((1, D), F32, ACC), ((1, D), F32, ACC), ((1, D), F32, ACC)]
    return _rowcall(name, body, S, tm, ins, outs + (GATE_OUTS(S, D) if gated else []))


HB16 = 16


def _conv3_val(ph, w, b, hb):
    return w[2:3] * _shift_prev(ph, 0, hb) + w[1:2] * _shift_prev(ph, 1, hb) + w[0:1] * _shift_prev(ph, 2, hb) + b


def _halo_first(halo_ref, tile_ref, live):
    return _phases(jnp.concatenate([halo_ref[...].astype(F32) * live, tile_ref[...].astype(F32)], axis=0), 1)


def _glu3_fwd(name, u, w, b, tm=128):
    S, F2 = u.shape
    Fh = F2 // 2

    def body(ua_ref, ub_ref, ha_ref, hb_ref, w_ref, b_ref, o_ref, z_ref):
        live = (pl.program_id(0) > 0).astype(F32)
        wv, bv = w_ref[...], b_ref[...]
        za = _conv3_val(_halo_first(ha_ref, ua_ref, live), wv[:, :Fh], bv[:, :Fh], HB16)
        zb = _conv3_val(_halo_first(hb_ref, ub_ref, live), wv[:, Fh:], bv[:, Fh:], HB16)
        o_ref[...] = (za * _sigmoid(za) * zb).astype(BF16)
        z_ref[:, :Fh] = za.astype(BF16)
        z_ref[:, Fh:] = zb.astype(BF16)

    return _rowcall(name, body, S, tm,
                    [(u, ("row", Fh, 0)), (u, ("row", Fh, 1)), (u, ("prev", HB16, Fh, 0)), (u, ("prev", HB16, Fh, 1)),
                     (w, FULL), (b, FULL)],
                    [((S, Fh), BF16, ROW), ((S, F2), BF16, ROW)])


def _glu3_bwd(name, z, dhm, tm=128):
    S, F2 = z.shape
    Fh = F2 // 2

    def body(za_ref, zb_ref, d_ref, dz_ref, db_ref):
        i = pl.program_id(0)
        za, zb, d = za_ref[...].astype(F32), zb_ref[...].astype(F32), d_ref[...]
        sg = _sigmoid(za)
        da = d * zb * (sg * (1.0 + za * (1.0 - sg)))
        db = d * (za * sg)
        dz_ref[:, :Fh] = da.astype(BF16)
        dz_ref[:, Fh:] = db.astype(BF16)
        _acc_add(i, db_ref, jnp.concatenate([_colsum(da), _colsum(db)], axis=1))

    return _rowcall(name, body, S, tm, [(z, ("row", Fh, 0)), (z, ("row", Fh, 1)), (dhm, ROW)],
                    [((S, F2), BF16, ROW), ((1, F2), F32, ACC)])


def _conv3_bwd(name, dz, u, w, tm=128):
    S, F2 = dz.shape
    nblk = S // tm
    K = w.shape[0]

    def body(d_ref, n_ref, u_ref, w_ref, o_ref, dw_ref):
        i = pl.program_id(0)
        live = (i < nblk - 1).astype(F32)
        ph = _phases(jnp.concatenate([d_ref[...].astype(F32), n_ref[...].astype(F32) * live], axis=0), -1)
        wv, uv = w_ref[...], u_ref[...].astype(F32)
        shifted = [_shift_next(ph, K - 1 - k, tm) for k in range(K)]
        o_ref[...] = functools.reduce(lambda a, t: a + t, [wv[k:k + 1] * shifted[k] for k in range(K)]).astype(BF16)
        for k in range(K):
            _acc_add(i, dw_ref, _colsum(uv * shifted[k]), rows=(slice(k, k + 1), slice(None)))

        @pl.when(i == 0)
        def _():
            dw_ref[K:, :] = jnp.zeros((dw_ref.shape[0] - K, F2), F32)

    return _rowcall(name, body, S, tm, [(dz, ROW), (dz, ("next", HB16, None, 0)), (u, ROW), (w, FULL)],
                    [((S, F2), BF16, ROW), ((SUBLANES, F2), F32, ACC)])


def _ffn_fwd(name, h, w_up, w_dn, dw_w, dw_b):
    u = _matmul(f"{name}_up", h, w_up, "nn", BF16, tm=2048)
    hm, z = _glu3_fwd(f"{name}_glu", u, dw_w, dw_b)
    f = _matmul(f"{name}_dn", hm, w_dn, "nn", F32, tm=1024, tn=1024, tk=w_dn.shape[0])
    return f, (u, hm, z)


def _ffn_bwd(name, dy, h, u, hm, z, w_up, w_dn, dw_w, dw_b):
    Fh = w_dn.shape[0]
    dhm = _matmul(f"{name}_ddn_x", dy, w_dn, "nt", F32, tm=1024, tn=Fh // 2)
    g_dn = _matmul(f"{name}_ddn_w", hm, dy, "tn", BF16, tm=Fh // 2, tn=1024, tk=2048)
    dz, g_dw_b = _glu3_bwd(f"{name}_dglu", z, dhm)
    du, taps = _conv3_bwd(f"{name}_dconv", dz, u, dw_w)
    g_up = _matmul(f"{name}_dup_w", h, du, "tn", BF16, tm=1024, tn=Fh // 2, tk=2048)
    dh = _matmul(f"{name}_dup_x", du, w_up, "nt", F32, tm=1024, tn=1024, tk=Fh)
    return dh, dict(up=g_up, dn=g_dn, dw_w=taps[0:dw_w.shape[0]], dw_b=g_dw_b)


HB32 = 32


def _glu31_fwd(name, p, w, b, tm=256):
    S, D2 = p.shape
    D = D2 // 2
    K = w.shape[0]

    def body(a_ref, g_ref, ha_ref, hg_ref, w_ref, b_ref, o_ref):
        live = (pl.program_id(0) > 0).astype(F32)
        y1 = a_ref[...] * _sigmoid(g_ref[...])
        ph = _phases(jnp.concatenate([ha_ref[...] * _sigmoid(hg_ref[...]) * live, y1], axis=0), 1)
        wv = w_ref[...]
        acc = b_ref[...] + wv[K - 1:K] * y1
        for k in range(K - 1):
            acc = acc + wv[k:k + 1] * _shift_prev(ph, K - 1 - k, HB32)
        o_ref[...] = acc

    return _rowcall(name, body, S, tm,
                    [(p, ("row", D, 0)), (p, ("row", D, 1)), (p, ("prev", HB32, D, 0)), (p, ("prev", HB32, D, 1)),
                     (w, FULL), (b, FULL)],
                    [((S, D), F32, ROW)])[0]


def _ln_silu_fwd(name, y2, g, b, tm=512):
    S, D = y2.shape

    def body(y_ref, g_ref, b_ref, o_ref):
        y = y_ref[...]
        mu = jnp.mean(y, axis=-1, keepdims=True)
        yc = y - mu
        rs = lax.rsqrt(jnp.mean(yc * yc, axis=-1, keepdims=True) + EPS)
        y3 = yc * rs * g_ref[...] + b_ref[...]
        o_ref[...] = (y3 * _sigmoid(y3)).astype(BF16)

    return _rowcall(name, body, S, tm, [(y2, ROW), (g, FULL), (b, FULL)], [((S, D), BF16, ROW)])[0]


def _ln_silu_bwd(name, y2, dy4, g, b, tm=512):
    S, D = y2.shape

    def body(y_ref, d_ref, g_ref, b_ref, o_ref, dg_ref, db_ref):
        i = pl.program_id(0)
        y, gv = y_ref[...], g_ref[...]
        mu = jnp.mean(y, axis=-1, keepdims=True)
        yc = y - mu
        rs = lax.rsqrt(jnp.mean(yc * yc, axis=-1, keepdims=True) + EPS)
        n = yc * rs
        y3 = n * gv + b_ref[...]
        sg = _sigmoid(y3)
        dy3 = d_ref[...] * (sg * (1.0 + y3 * (1.0 - sg)))
        _acc_add(i, db_ref, _colsum(dy3))
        _acc_add(i, dg_ref, _colsum(dy3 * n))
        dn = dy3 * gv
        o_ref[...] = rs * (dn - jnp.mean(dn, axis=-1, keepdims=True) - n * jnp.mean(dn * n, axis=-1, keepdims=True))

    return _rowcall(name, body, S, tm, [(y2, ROW), (dy4, ROW), (g, FULL), (b, FULL)],
                    [((S, D), F32, ROW), ((1, D), F32, ACC), ((1, D), F32, ACC)])


def _glu31_bwd(name, p, dy2, w, tm=256):
    S, D2 = p.shape
    D = D2 // 2
    K = w.shape[0]
    nblk = S // tm

    conv_rows, tap_rows, tap_group, tap_unroll = 16, SUBLANES, 4, 4

    def body(a_ref, g_ref, d_ref, dn_ref, w_ref, dp_ref, dw_ref, dcb_ref, dpb_ref, ph_d, y1_ref, dy1_ref, wb_ref):
        i = pl.program_id(0)
        live_next = (i < nblk - 1).astype(F32)
        a, sg, d = a_ref[...], _sigmoid(g_ref[...]), d_ref[...]
        y1_ref[...] = a * sg
        for b, ph in enumerate(_phases(jnp.concatenate([d, dn_ref[...] * live_next], axis=0), -1)):
            ph_d[b] = ph
        taps = [divmod(K - 1 - k, SUBLANES) for k in range(K)]

        @pl.when(i == 0)
        def _():
            for k in range(K):
                wb_ref[k] = jnp.broadcast_to(w_ref[k:k + 1, :], (SUBLANES, D))

        def conv_rows_at(rb, carry):
            r0 = pl.multiple_of(rb * conv_rows, conv_rows)
            accs = [jnp.zeros((SUBLANES, D), F32) for _ in range(conv_rows // SUBLANES)]
            for k, (rows8, phase) in enumerate(taps):
                wk = wb_ref[k]
                for u in range(len(accs)):
                    accs[u] = accs[u] + wk * ph_d[phase, pl.ds(r0 + SUBLANES * (rows8 + u), SUBLANES), :]
            for u, acc in enumerate(accs):
                dy1_ref[pl.ds(r0 + SUBLANES * u, SUBLANES), :] = acc
            return carry

        lax.fori_loop(0, tm // conv_rows, conv_rows_at, 0)
        for k0 in range(0, K, tap_group):
            group = taps[k0:k0 + tap_group]

            def tap_rows_at(rb, accs, group=group):
                for u in range(tap_unroll):
                    r0 = pl.multiple_of((rb * tap_unroll + u) * tap_rows, tap_rows)
                    yv = y1_ref[pl.ds(r0, tap_rows), :]
                    accs = tuple(acc + yv * ph_d[phase, pl.ds(r0 + SUBLANES * rows8, tap_rows), :]
                                 for acc, (rows8, phase) in zip(accs, group))
                return accs

            accs = lax.fori_loop(0, tm // (tap_rows * tap_unroll), tap_rows_at,
                                 tuple(jnp.zeros((tap_rows, D), F32) for _ in group))
            for j, acc in enumerate(accs):
                _acc_add(i, dw_ref, _colsum(acc), rows=(slice(k0 + j, k0 + j + 1), slice(None)))

        @pl.when(i == 0)
        def _():
            dw_ref[K:, :] = jnp.zeros((dw_ref.shape[0] - K, D), F32)

        _acc_add(i, dcb_ref, _colsum(d))
        dy1 = dy1_ref[...]
        da = dy1 * sg
        dg = dy1 * a * sg * (1.0 - sg)
        dp_ref[:, :D] = da.astype(BF16)
        dp_ref[:, D:] = dg.astype(BF16)
        _acc_add(i, dpb_ref, jnp.concatenate([_colsum(da), _colsum(dg)], axis=1))

    return _rowcall(name, body, S, tm,
                    [(p, ("row", D, 0)), (p, ("row", D, 1)), (dy2, ROW), (dy2, ("next", HB32, None, 0)), (w, FULL)],
                    [((S, D2), BF16, ROW), ((HB32, D), F32, ACC), ((1, D), F32, ACC), ((1, D2), F32, ACC)],
                    scratch=[pltpu.VMEM((SUBLANES, tm + HB32, D), F32), pltpu.VMEM((tm, D), F32),
                             pltpu.VMEM((tm, D), F32), pltpu.VMEM((K, SUBLANES, D), F32)])


CHUNK = 128
A_GROUPS = 4


def _group_ln(gv):
    ns, rss = [], []
    for g in range(A_GROUPS):
        xg = gv[:, g * LANES:(g + 1) * LANES]
        xc = xg - jnp.mean(xg, axis=-1, keepdims=True)
        rs = lax.rsqrt(jnp.mean(xc * xc, axis=-1, keepdims=True) + EPS)
        ns.append(xc * rs)
        rss.append(jnp.broadcast_to(rs, xg.shape))
    return jnp.concatenate(ns, axis=1), jnp.concatenate(rss, axis=1)


def _tril_mask():
    r = lax.broadcasted_iota(jnp.int32, (CHUNK, CHUNK), 0)
    c = lax.broadcasted_iota(jnp.int32, (CHUNK, CHUNK), 1)
    return r >= c


def _spatial(ws_ref, x, dn):
    mask = _tril_mask()
    rows = []
    for ci in range(x.shape[0] // CHUNK):
        cols = []
        for g in range(A_GROUPS):
            wm = jnp.where(mask, ws_ref[g], 0.0).astype(BF16)
            xb = x[ci * CHUNK:(ci + 1) * CHUNK, g * LANES:(g + 1) * LANES]
            cols.append(lax.dot_general(wm, xb, dn, preferred_element_type=F32))
        rows.append(jnp.concatenate(cols, axis=1))
    return jnp.concatenate(rows, axis=0)


def _mixa_fwd(name, z, vg, ws, bias_full, tm=256):
    S = z.shape[0]
    W = A_GROUPS * LANES

    def body(u_ref, v_ref, vg_ref, ws_ref, b_ref, o_ref):
        nh, _ = _group_ln(_gelu(v_ref[...]))
        vn = (nh * vg_ref[...]).astype(BF16)
        f = _spatial(ws_ref, vn, _DN["nn"]) + jnp.concatenate([b_ref[...]] * (tm // CHUNK), axis=0)
        o_ref[...] = (_gelu(u_ref[...]) * f).astype(BF16)

    return _rowcall(name, body, S, tm,
                    [(z, ("row", W, 0)), (z, ("row", W, 1)), (vg, FULL), (ws, FULL), (bias_full, FULL)],
                    [((S, W), BF16, ROW)])[0]


def _mixa_bwd(name, z, dyab, vg, ws, bias_full, tm=256):
    S = z.shape[0]
    W = A_GROUPS * LANES
    nch = tm // CHUNK

    def body(u_ref, v_ref, d_ref, vg_ref, ws_ref, b_ref, dz_ref, dws_ref, dbf_ref, dvg_ref):
        i = pl.program_id(0)
        u, v, d, vgv = u_ref[...], v_ref[...], d_ref[...], vg_ref[...]
        nh, rs = _group_ln(_gelu(v))
        vn = (nh * vgv).astype(BF16)
        f = _spatial(ws_ref, vn, _DN["nn"]) + jnp.concatenate([b_ref[...]] * nch, axis=0)
        dz_ref[:, :W] = (d * f * _gelu_grad(u)).astype(BF16)
        df = d * _gelu(u)
        dbf = df[0:CHUNK]
        for ci in range(1, nch):
            dbf = dbf + df[ci * CHUNK:(ci + 1) * CHUNK]
        _acc_add(i, dbf_ref, dbf)
        dfb = df.astype(BF16)
        mask = _tril_mask()
        for g in range(A_GROUPS):
            acc = jnp.zeros((CHUNK, CHUNK), F32)
            for ci in range(nch):
                blk = (slice(ci * CHUNK, (ci + 1) * CHUNK), slice(g * LANES, (g + 1) * LANES))
                acc = acc + lax.dot_general(dfb[blk], vn[blk], _DN["nt"], preferred_element_type=F32)
            _acc_add(i, dws_ref, jnp.where(mask, acc, 0.0)[None], rows=(slice(g, g + 1), slice(None), slice(None)))
        dvn = _spatial(ws_ref, dfb, _DN["tn"])
        _acc_add(i, dvg_ref, _colsum(dvn * nh))
        dnh = dvn * vgv
        parts = []
        for g in range(A_GROUPS):
            cs = slice(g * LANES, (g + 1) * LANES)
            dg_, ng = dnh[:, cs], nh[:, cs]
            parts.append(dg_ - jnp.mean(dg_, axis=-1, keepdims=True) - ng * jnp.mean(dg_ * ng, axis=-1, keepdims=True))
        dz_ref[:, W:] = (rs * jnp.concatenate(parts, axis=1) * _gelu_grad(v)).astype(BF16)

    return _rowcall(name, body, S, tm,
                    [(z, ("row", W, 0)), (z, ("row", W, 1)), (dyab, ("row", W, 0)), (vg, FULL), (ws, FULL),
                     (bias_full, FULL)],
                    [((S, 2 * W), BF16, ROW), ((A_GROUPS, CHUNK, CHUNK), F32, ACC), ((CHUNK, W), F32, ACC),
                     ((1, W), F32, ACC)])


HEAD = 64
N_HEADS = 8
BW = HEAD * N_HEADS
QB = 128
DILATIONS = (1, 4, 16)
QK_SCALE = HEAD ** -0.5


def _gsum64(x, ones_bd):
    x1 = x.astype(BF16)
    r1 = x - x1.astype(F32)
    x2 = r1.astype(BF16)
    x3 = (r1 - x2.astype(F32)).astype(BF16)
    dot = lambda t: jnp.dot(t, ones_bd, preferred_element_type=F32)
    return dot(x1) + dot(x2) + dot(x3)


def _swap32(x):
    n = x.shape[-1]
    up = pltpu.roll(x, n - HEAD // 2, axis=1)
    dn = pltpu.roll(x, HEAD // 2, axis=1)
    lane = lax.broadcasted_iota(jnp.int32, x.shape, 1)
    return jnp.where((lane % HEAD) < HEAD // 2, up, dn)


def _tile4(t):
    return jnp.concatenate([t] * (BW // LANES), axis=1)


def _stage_spec(tm):
    return pltpu.VMEM((BW // LANES, tm, LANES), F32)


def _to_classes(stage, x, dil):
    tm = x.shape[0]
    for j in range(BW // LANES):
        stage[j] = x[:, j * LANES:(j + 1) * LANES]
    return [jnp.concatenate([stage.at[j][pl.ds(r, tm // dil, stride=dil), :] for j in range(BW // LANES)], axis=1)
            for r in range(dil)]


def _from_classes(stage, cls, dil):
    rows = cls.shape[1]
    for r in range(dil):
        for j in range(BW // LANES):
            stage.at[j][pl.ds(r, rows, stride=dil), :] = cls[r, :, j * LANES:(j + 1) * LANES]
    return jnp.concatenate([stage[j] for j in range(BW // LANES)], axis=1)


def _cls_view(t, dil):
    return t if dil == 1 else t.reshape(dil, t.shape[0] // dil, t.shape[1])


def _cls_kind(dil, off=0):
    return ("off", off, None, 0) if dil == 1 else ("cls", dil, off)


def _cls_out(S, dil, dtype):
    return ((S, BW) if dil == 1 else (dil, S // dil, BW), dtype, _cls_kind(dil))


def _flat(t):
    return t.reshape(-1, t.shape[-1])


def _qkv_fwd(name, z, cos, sin, ones_bd, qg, kg, tm=512):
    S = z.shape[0]
    nd = len(DILATIONS)

    def body(q_ref, k_ref, v_ref, c_ref, s_ref, o_ref, qg_ref, kg_ref, *rest):
        outs, stage = rest[:3 * nd], rest[3 * nd]
        c, s, ob = _tile4(c_ref[...]), _tile4(s_ref[...]), o_ref[...]

        def norm_rope(x, g):
            r = lax.rsqrt(_gsum64(x * x, ob) * (1.0 / HEAD) + EPS)
            xn = x * r * g
            return xn * c + _swap32(xn) * s

        vals = [norm_rope(q_ref[...], qg_ref[...]) * QK_SCALE, norm_rope(k_ref[...], kg_ref[...]), v_ref[...]]
        for a, val in enumerate(vals):
            for b, dil in enumerate(DILATIONS):
                if dil == 1:
                    outs[nd * a + b][...] = val.astype(BF16)
                else:
                    for r, rows in enumerate(_to_classes(stage, val, dil)):
                        outs[nd * a + b][r] = rows.astype(BF16)

    outs = _rowcall(name, body, S, tm,
                    [(z, ("row", BW, 2)), (z, ("row", BW, 3)), (z, ("row", BW, 4)), (cos, ROW), (sin, ROW),
                     (ones_bd, FULL), (qg, FULL), (kg, FULL)],
                    [_cls_out(S, dil, BF16) for _ in range(3) for dil in DILATIONS], scratch=[_stage_spec(tm)])
    return [[_flat(outs[nd * a + b]) for a in range(3)] for b in range(nd)]


PAIR = 2 * HEAD


ATT_BLOCKS = 2
ATT_TM = ATT_BLOCKS * QB
ATT_PREV = ("prev", QB, None, 0)


def _key_rows(prev_ref, cur_ref, sb, ps):
    before = prev_ref[:, ps] if sb == 0 else cur_ref[(sb - 1) * QB:sb * QB, ps]
    return jnp.concatenate([before, cur_ref[sb * QB:(sb + 1) * QB, ps]], axis=0)


def _pair_scores(q_ref, kp_ref, kc_ref, sb, hp, half, seg_blocks):
    ps = slice(hp * PAIR, (hp + 1) * PAIR)
    mine = (lax.broadcasted_iota(jnp.int32, (1, PAIR), 1) >= HEAD) == (half == 1)
    qm = jnp.where(mine, q_ref[sb * QB:(sb + 1) * QB, ps], jnp.zeros((), BF16))
    kcat = _key_rows(kp_ref, kc_ref, sb, ps)
    s = lax.dot_general(qm, kcat, _DN["nt"], preferred_element_type=F32)
    qi = lax.broadcasted_iota(jnp.int32, (QB, 2 * QB), 0)
    kj = lax.broadcasted_iota(jnp.int32, (QB, 2 * QB), 1)
    has_prev = ((pl.program_id(0) * ATT_BLOCKS + sb) % seg_blocks) != 0
    valid = (kj >= qi) & (kj <= qi + QB) & ((kj >= QB) | has_prev)
    return mine, qm, kcat, s, valid


def _attn_fwd(name, q, k, v, dil):
    S = q.shape[0]
    seg_blocks = S // dil // QB

    def body(q_ref, kp_ref, kc_ref, vp_ref, vc_ref, o_ref, l_ref):
        for hp in range(N_HEADS // 2):
            ps = slice(hp * PAIR, (hp + 1) * PAIR)
            chains = [(sb, half) for sb in range(ATT_BLOCKS) for half in range(2)]
            sc = [_pair_scores(q_ref, kp_ref, kc_ref, sb, hp, half, seg_blocks) for sb, half in chains]
            ss = [jnp.where(valid, s, NEG) for _, _, _, s, valid in sc]
            ms = [jnp.max(s, axis=-1, keepdims=True) for s in ss]
            pv = [jnp.exp(s - m) for s, m in zip(ss, ms)]
            dens = [jnp.sum(p, axis=-1, keepdims=True) for p in pv]
            vcats = [_key_rows(vp_ref, vc_ref, sb, ps) for sb in range(ATT_BLOCKS)]
            outs = [jnp.dot(p.astype(BF16), vcats[sb], preferred_element_type=F32) / den
                    for p, den, (sb, _) in zip(pv, dens, chains)]
            lses = [jnp.broadcast_to(m + jnp.log(den), (QB, PAIR)) for m, den in zip(ms, dens)]
            for sb in range(ATT_BLOCKS):
                rows, upper = slice(sb * QB, (sb + 1) * QB), sc[2 * sb + 1][0]
                o_ref[rows, ps] = jnp.where(upper, outs[2 * sb + 1], outs[2 * sb])
                l_ref[rows, ps] = jnp.where(upper, lses[2 * sb + 1], lses[2 * sb])

    return _rowcall(name, body, S, ATT_TM, [(q, ROW), (k, ATT_PREV), (k, ROW), (v, ATT_PREV), (v, ROW)],
                    [((S, BW), F32, ROW)] * 2)


def _attn_bwd(name, q, k, v, do, lse, delta, dil):
    S = q.shape[0]
    seg_blocks = S // dil // QB

    def body(q_ref, kp_ref, kc_ref, vp_ref, vc_ref, do_ref, l_ref, dl_ref, dq_ref, dkc_ref, dkp_ref, dvc_ref, dvp_ref):
        for hp in range(N_HEADS // 2):
            ps = slice(hp * PAIR, (hp + 1) * PAIR)
            chains = [(sb, half) for sb in range(ATT_BLOCKS) for half in range(2)]
            rows = [slice(sb * QB, (sb + 1) * QB) for sb, _ in chains]
            cols = [hp * PAIR + half * HEAD for _, half in chains]
            sc = [_pair_scores(q_ref, kp_ref, kc_ref, sb, hp, half, seg_blocks) for sb, half in chains]
            pv = [jnp.where(valid, jnp.exp(s - l_ref[r, c:c + 1]), 0.0) for (_, _, _, s, valid), r, c in zip(sc, rows, cols)]
            vcats = [_key_rows(vp_ref, vc_ref, sb, ps) for sb in range(ATT_BLOCKS)]
            doms = [jnp.where(mine, do_ref[r, ps].astype(BF16), jnp.zeros((), BF16)) for (mine, *_), r in zip(sc, rows)]
            dps = [lax.dot_general(dom, vcats[sb], _DN["nt"], preferred_element_type=F32) for dom, (sb, _) in zip(doms, chains)]
            dss = [(p * (dp - dl_ref[r, c:c + 1])).astype(BF16) for p, dp, r, c in zip(pv, dps, rows, cols)]
            dqs = [jnp.dot(ds, kcat, preferred_element_type=F32) for ds, (_, _, kcat, _, _) in zip(dss, sc)]
            dks = [lax.dot_general(ds, qm, _DN["tn"], preferred_element_type=F32) for ds, (_, qm, *_) in zip(dss, sc)]
            dvs = [lax.dot_general(p.astype(BF16), dom, _DN["tn"], preferred_element_type=F32) for p, dom in zip(pv, doms)]
            for sb in range(ATT_BLOCKS):
                lo, hi = 2 * sb, 2 * sb + 1
                dk, dv = dks[lo] + dks[hi], dvs[lo] + dvs[hi]
                dq_ref[rows[lo], ps] = jnp.where(sc[hi][0], dqs[hi], dqs[lo])
                dkp_ref[rows[lo], ps] = dk[:QB]
                dkc_ref[rows[lo], ps] = dk[QB:]
                dvp_ref[rows[lo], ps] = dv[:QB]
                dvc_ref[rows[lo], ps] = dv[QB:]

    return _rowcall(name, body, S, ATT_TM,
                    [(q, ROW), (k, ATT_PREV), (k, ROW), (v, ATT_PREV), (v, ROW), do, (lse, ROW), (delta, ROW)],
                    [((S, BW), F32, ROW)] * 5)


def _merge_fwd(name, branches, tm=512):
    S = branches[0][0].shape[0]
    nd = len(DILATIONS)

    def body(*refs):
        ins, (y_ref, yb_ref), l_refs, stage = refs[:2 * nd], refs[2 * nd:2 * nd + 2], refs[2 * nd + 2:3 * nd + 2], refs[-1]
        os_, ls = [], []
        for b, dil in enumerate(DILATIONS):
            o, l = ins[2 * b][...], ins[2 * b + 1][...]
            os_.append(o if dil == 1 else _from_classes(stage, o, dil))
            ls.append(l if dil == 1 else _from_classes(stage, l, dil))
        m = functools.reduce(jnp.maximum, ls)
        es = [jnp.exp(l - m) for l in ls]
        den = functools.reduce(lambda a, e: a + e, es)
        y = functools.reduce(lambda a, t: a + t, [e * o for e, o in zip(es, os_)]) / den
        y_ref[...] = y
        yb_ref[...] = y.astype(BF16)
        lse = m + jnp.log(den)
        for b, dil in enumerate(DILATIONS):
            if dil == 1:
                l_refs[b][...] = lse
            else:
                for r, rows in enumerate(_to_classes(stage, lse, dil)):
                    l_refs[b][r] = rows

    ins = [(_cls_view(t, dil), _cls_kind(dil)) for pair, dil in zip(branches, DILATIONS) for t in pair]
    outs = _rowcall(name, body, S, tm, ins,
                    [((S, BW), F32, ROW), ((S, BW), BF16, ROW)] + [_cls_out(S, dil, F32) for dil in DILATIONS],
                    scratch=[_stage_spec(tm)])
    return outs[0], outs[1], [_flat(t) for t in outs[2:]]


def _delta(name, dyab, yb, ones_bd, tm=512):
    S = yb.shape[0]
    nd = len(DILATIONS)

    def body(d_ref, y_ref, o_ref, *rest):
        dl_refs, do_refs, stage = rest[:nd], rest[nd:2 * nd - 1], rest[-1]
        d = d_ref[...]
        dl = _gsum64(d * y_ref[...], o_ref[...])
        for b, dil in enumerate(DILATIONS):
            if dil == 1:
                dl_refs[b][...] = dl
            else:
                for r, rows in enumerate(_to_classes(stage, dl, dil)):
                    dl_refs[b][r] = rows
                for r, rows in enumerate(_to_classes(stage, d, dil)):
                    do_refs[b - 1][r] = rows.astype(BF16)

    outs = _rowcall(name, body, S, tm, [(dyab, ("row", BW, 1)), (yb, ROW), (ones_bd, FULL)],
                    [_cls_out(S, dil, F32) for dil in DILATIONS] + [_cls_out(S, dil, BF16) for dil in DILATIONS[1:]],
                    scratch=[_stage_spec(tm)])
    return [_flat(t) for t in outs[:nd]], [_flat(t) for t in outs[nd:]]


def _qkv_bwd(name, z, cos, sin, ones_bd, qg, kg, pieces):
    S = z.shape[0]
    nblk = S // QB

    def body(q_ref, k_ref, c_ref, s_ref, o_ref, qg_ref, kg_ref, *rest):
        pr, (dz_ref, dqg_ref, dkg_ref), stage = rest[:15], rest[15:18], rest[18]
        i = pl.program_id(0)
        c, s, ob = _tile4(c_ref[...]), _tile4(s_ref[...]), o_ref[...]
        dq = dk = dv = None
        for b, dil in enumerate(DILATIONS):
            a_q, a_kc, a_kp, a_vc, a_vp = [r[...] for r in pr[5 * b:5 * b + 5]]
            live = ((i + dil) < nblk).astype(F32)
            tq, tk, tv = a_q, a_kc + a_kp * live, a_vc + a_vp * live
            if dil > 1:
                tq, tk, tv = (_from_classes(stage, t, dil) for t in (tq, tk, tv))
            dq, dk, dv = (tq, tk, tv) if b == 0 else (dq + tq, dk + tk, dv + tv)

        def back(x, g, d_rot, acc_ref):
            r = lax.rsqrt(_gsum64(x * x, ob) * (1.0 / HEAD) + EPS)
            n = x * r
            dxn = d_rot * c + _swap32(d_rot * s)
            _acc_add(i, acc_ref, _colsum(dxn * n))
            dn = dxn * g
            return r * (dn - n * (_gsum64(dn * n, ob) * (1.0 / HEAD)))

        dz_ref[:, :BW] = back(q_ref[...], qg_ref[...], dq * QK_SCALE, dqg_ref).astype(BF16)
        dz_ref[:, BW:2 * BW] = back(k_ref[...], kg_ref[...], dk, dkg_ref).astype(BF16)
        dz_ref[:, 2 * BW:] = dv.astype(BF16)

    ins = [(z, ("row", BW, 2)), (z, ("row", BW, 3)), (cos, ROW), (sin, ROW), (ones_bd, FULL), (qg, FULL), (kg, FULL)]
    for piece, dil in zip(pieces, DILATIONS):
        a_q, a_kc, a_kp, a_vc, a_vp = (_cls_view(t, dil) for t in piece)
        own, prev = _cls_kind(dil), _cls_kind(dil, dil)
        ins += [(a_q, own), (a_kc, own), (a_kp, prev), (a_vc, own), (a_vp, prev)]
    return _rowcall(name, body, S, QB, ins,
                    [((S, 3 * BW), BF16, ROW), ((1, BW), F32, ACC), ((1, BW), F32, ACC)], scratch=[_stage_spec(QB)])


def _local_step(x0, tgt, pos, mod, wb, sp, pipe):
    S, D = x0.shape
    md = lambda l, j: mod[l, j:j + 1]
    sh_m, sc_m, g_m, sh_f, sc_f, g_f = ([md(l, j) for l in range(2)] for j in range(6))
    nm_g, nf_g = sp["norm_mix_g"], sp["norm_ffn_g"]

    inv_freq = 1.0 / (ROPE_THETA ** (jnp.arange(0, HEAD, 2, dtype=F32) / HEAD))
    ang = pos.astype(F32)[:, None] * inv_freq
    cs, sn = jnp.cos(ang), jnp.sin(ang)
    cos = jnp.concatenate([cs, cs, cs, cs], axis=1)
    sin = jnp.concatenate([-sn, sn, -sn, sn], axis=1)
    head_of = jnp.arange(BW) // HEAD
    ones_bd = (head_of[:, None] == head_of[None, :]).astype(BF16)
    qg = jnp.tile(sp["b_q_norm_g"].reshape(1, HEAD), (1, N_HEADS))
    kg = jnp.tile(sp["b_k_norm_g"].reshape(1, HEAD), (1, N_HEADS))
    vg = sp["a_vnorm_g"].reshape(1, A_GROUPS * LANES)
    ws = sp["a_spatial_w"][0]
    bias_full = jnp.repeat(sp["a_spatial_b"][0].T, LANES, axis=1)
    ffn_s = [(sp["ffn_dw_w"][l], sp["ffn_dw_b"][l:l + 1]) for l in range(2)]

    h0 = _mod_first("l0_mod", x0, nm_g[0:1], sc_m[0], sh_m[0])
    z = _matmul("l0_in", h0, wb.get("w_in", h0), "nn", F32, tm=2048)
    ya = _mixa_fwd("l0_mixa", z, vg, ws, bias_full)
    qkv = _qkv_fwd("l0_qkv", z, cos, sin, ones_bd, qg, kg)
    branches = [_attn_fwd(f"l0_att{dil}", *qkv[b], dil) for b, dil in enumerate(DILATIONS)]
    yb, yb16, lses = _merge_fwd("l0_merge", branches)
    yab = jnp.concatenate([ya, yb16], axis=1)
    y0 = _matmul("l0_out", yab, wb.get("w_out", yab), "nn", F32, tm=1024, tn=1024)
    x1, h1 = _resid_mod("l0_res1", x0, y0, g_m[0], nf_g[0:1], sc_f[0], sh_f[0])
    ffn_w = [(wb.get("up0", h1), wb.get("dn0", h1), *ffn_s[0])]
    f0, saved0 = _ffn_fwd("l0_ffn", h1, *ffn_w[0])
    x2, h2 = _resid_mod("l0_res2", x1, f0, g_f[0], nm_g[1:2], sc_m[1], sh_m[1])
    p = _matmul("l1_pw1", h2, wb.get("pw1", h2), "nn", F32, tm=2048, bias=sp["conv_pw1_b"])
    y2 = _glu31_fwd("l1_glu", p, sp["conv_dw_w"][0], sp["conv_dw_b"])
    y4 = _ln_silu_fwd("l1_ln", y2, sp["conv_ln_g"], sp["conv_ln_b"])
    y1 = _matmul("l1_pw2", y4, wb.get("pw2", y4), "nn", F32, tm=1024, tn=1024, bias=sp["conv_pw2_b"])
    x3, h3 = _resid_mod("l1_res1", x2, y1, g_m[1], nf_g[1:2], sc_f[1], sh_f[1])
    ffn_w.append((wb.get("up1", h3), wb.get("dn1", h3), *ffn_s[1]))
    f1, saved1 = _ffn_fwd("l1_ffn", h3, *ffn_w[1])
    dx4, lossv, dy, dgate_f1, _ = _loss_head("loss", x3, f1, g_f[1], tgt)

    dh, gf1 = _ffn_bwd("l1_ffn", dy, h3, *saved1, *ffn_w[1])
    tok = pipe.scatter("g1", dict(dn1=gf1["dn"], up1=gf1["up"]))
    dx3, dsh_f1, dsc_f1, dnf1, dy, dgate_m1, dpw2_b = _mod_bwd("l1_dmod2", dx4, dh, x3, nf_g[1:2], sc_f[1] + tok,
                                                             y1, g_m[1])
    dy4 = _matmul("l1_dpw2_x", dy, wb.get("pw2"), "nt", F32, tm=1024, tn=1024)
    g_pw2 = _matmul("l1_dpw2_w", y4, dy, "tn", BF16, tm=1024, tn=1024, tk=2048)
    dy2, dln_g, dln_b = _ln_silu_bwd("l1_dln", y2, dy4, sp["conv_ln_g"], sp["conv_ln_b"])
    dp, ddw_w, ddw_b, dpw1_b = _glu31_bwd("l1_dglu", p, dy2, sp["conv_dw_w"][0])
    g_pw1 = _matmul("l1_dpw1_w", h2, dp, "tn", BF16, tm=1024, tn=1024, tk=2048)
    tok = pipe.scatter("g2", dict(pw2=g_pw2, pw1=g_pw1))
    pipe.collect("g1", g_pw1)
    dh = _matmul("l1_dpw1_x", dp, wb.get("pw1"), "nt", F32, tm=1024, tn=1024, tk=2048)
    dx2, dsh_m1, dsc_m1, dnm1, dy, dgate_f0, _ = _mod_bwd("l1_dmod1", dx3, dh, x2, nm_g[1:2], sc_m[1] + tok,
                                                        f0, g_f[0])
    dh, gf0 = _ffn_bwd("l0_ffn", dy, h1, *saved0, *ffn_w[0])
    tok = pipe.scatter("g3", dict(dn0=gf0["dn"], up0=gf0["up"]))
    pipe.collect("g2", gf0["up"])
    dx1, dsh_f0, dsc_f0, dnf0, dy, dgate_m0, _ = _mod_bwd("l0_dmod2", dx2, dh, x1, nf_g[0:1], sc_f[0] + tok,
                                                        y0, g_m[0])
    dyab = _matmul("l0_dout_x", dy, wb.get("w_out"), "nt", F32, tm=1024, tn=1024)
    g_out = _matmul("l0_dout_w", yab, dy, "tn", BF16, tm=1024, tn=1024, tk=2048)
    vg = vg + pipe.scatter("g4", dict(w_out=g_out))
    dza, dws, dbf, dvg = _mixa_bwd("l0_dmixa", z, dyab, vg, ws, bias_full)
    deltas, dos = _delta("l0_delta", dyab, yb, ones_bd)
    pieces = []
    for b, dil in enumerate(DILATIONS):
        do = (dyab, ("row", BW, 1)) if dil == 1 else (dos[b - 1], ROW)
        pieces.append(_attn_bwd(f"l0_datt{dil}", *qkv[b], do, lses[b], deltas[b], dil))
    dzb, dqg, dkg = _qkv_bwd("l0_dqkv", z, cos, sin, ones_bd, qg, kg, pieces)
    small_tok = _start_small_grads(pipe, dzb, sp, lossv, ((dgate_m0, dsh_f0, dsc_f0, dgate_f0),
                                                   (dsh_m1, dsc_m1, dgate_m1, dsh_f1, dsc_f1, dgate_f1)),
                                   (dnm1, dnf0, dnf1), dvg, dws, dbf, dqg, dkg, dpw1_b, ddw_w, ddw_b, dln_g, dln_b,
                                   dpw2_b, gf0, gf1)
    dz = jnp.concatenate([dza, dzb], axis=1)
    g_in = _matmul("l0_din_w", h0, dz, "tn", BF16, tm=1024, tn=1280, tk=2048, after=small_tok)
    tok = pipe.scatter("g5", dict(w_in=g_in))
    dh = _matmul("l0_din_x", dz, wb.get("w_in"), "nt", F32, tm=1024, tn=1024, tk=2560, after=small_tok)
    grad_x, dsh_m0, dsc_m0, dnm0 = _mod_bwd("l0_dmod1", dx1, dh, x0, nm_g[0:1], sc_m[0] + tok)
    return lossv, grad_x, (dsh_m0, dsc_m0, dnm0)


def _start_small_grads(pipe, after, sp, lossv, mods, norms, dvg, dws, dbf, dqg, dkg, dpw1_b, ddw_w, ddw_b, dln_g,
                       dln_b, dpw2_b, gf0, gf1):
    (dgate_m0, dsh_f0, dsc_f0, dgate_f0), (dsh_m1, dsc_m1, dgate_m1, dsh_f1, dsc_f1, dgate_f1) = mods
    dnm1, dnf0, dnf1 = norms
    zero = jnp.zeros_like(dnm1)
    dmod = jnp.stack([jnp.concatenate([zero, zero, dgate_m0, dsh_f0, dsc_f0, dgate_f0], axis=0),
                      jnp.concatenate([dsh_m1, dsc_m1, dgate_m1, dsh_f1, dsc_f1, dgate_f1], axis=0)])
    small = dict(
        norm_mix_g=jnp.concatenate([zero, dnm1], axis=0),
        norm_ffn_g=jnp.concatenate([dnf0, dnf1], axis=0),
        a_vnorm_g=dvg.reshape(1, A_GROUPS, LANES),
        a_spatial_w=dws[None],
        a_spatial_b=dbf.reshape(CHUNK, A_GROUPS, LANES).sum(-1).T[None],
        b_q_norm_g=dqg.reshape(N_HEADS, HEAD).sum(0)[None],
        b_k_norm_g=dkg.reshape(N_HEADS, HEAD).sum(0)[None],
        conv_pw1_b=dpw1_b, conv_dw_w=ddw_w[None, :sp["conv_dw_w"].shape[1]], conv_dw_b=ddw_b,
        conv_ln_g=dln_g, conv_ln_b=dln_b, conv_pw2_b=dpw2_b,
        ffn_dw_w=jnp.stack([gf0["dw_w"], gf1["dw_w"]]),
        ffn_dw_b=jnp.concatenate([gf0["dw_b"], gf1["dw_b"]], axis=0),
    )
    return pipe.start_small(dmod, lossv, small, after)


ADA_TN = 512


def _ada_fwd(name, c_all, ada_w, ada_b_sh):
    L, D, N = ada_w.shape
    B = c_all.shape[0]

    def body(c_ref, w_ref, b_ref, o_ref):
        cv = c_ref[...]
        ca = (cv * _sigmoid(cv)).astype(BF16)
        o_ref[0] = jnp.dot(ca, w_ref[0].astype(BF16), preferred_element_type=F32) + b_ref[0]

    return pl.pallas_call(
        body, name=name, grid=(L, N // ADA_TN),
        in_specs=[pl.BlockSpec((B, D), lambda l, j: (0, 0)), pl.BlockSpec((1, D, ADA_TN), lambda l, j: (l, 0, j)),
                  pl.BlockSpec((1, 1, ADA_TN), lambda l, j: (l, 0, j))],
        out_specs=pl.BlockSpec((1, B, ADA_TN), lambda l, j: (l, 0, j)),
        out_shape=jax.ShapeDtypeStruct((L, B, N), F32),
        compiler_params=_params(("parallel", "parallel")),
    )(c_all, ada_w, ada_b_sh.reshape(L, 1, N))


def _adamw_val(w, g, m, v):
    m2 = ADAM_B1 * m + (1.0 - ADAM_B1) * g
    v2 = ADAM_B2 * v + (1.0 - ADAM_B2) * (g * g)
    m_hat = m2 / (1.0 - ADAM_B1 ** ADAM_STEP)
    v_hat = v2 / (1.0 - ADAM_B2 ** ADAM_STEP)
    delta = -ADAM_LR * (m_hat / (jnp.sqrt(v_hat) + ADAM_EPS) + ADAM_WD * w)
    return delta, m2, v2


def _ada_update(name, c_all, dmod_sh, w, m, v):
    L, D, N = w.shape
    B = c_all.shape[0]

    def body(c_ref, d_ref, w_ref, m_ref, v_ref, g_ref, dl_ref, mo_ref, vo_ref):
        cv = c_ref[...]
        ca = (cv * _sigmoid(cv)).astype(BF16)
        g = lax.dot_general(ca, d_ref[0].astype(BF16), _DN["tn"], preferred_element_type=F32)
        g_ref[0] = g
        dl_ref[0], mo_ref[0], vo_ref[0] = _adamw_val(w_ref[0], g, m_ref[0], v_ref[0])

    wspec = pl.BlockSpec((1, D, ADA_TN), lambda l, j: (l, 0, j))
    return pl.pallas_call(
        body, name=name, grid=(L, N // ADA_TN),
        in_specs=[pl.BlockSpec((B, D), lambda l, j: (0, 0)), pl.BlockSpec((1, B, ADA_TN), lambda l, j: (l, 0, j)),
                  wspec, wspec, wspec],
        out_specs=[wspec] * 4, out_shape=[jax.ShapeDtypeStruct((L, D, N), F32)] * 4,
        compiler_params=_params(("parallel", "parallel")),
    )(c_all, dmod_sh, w, m, v)


def _adamw(name, w, g, m, v):
    R, C = w.shape
    tm = R
    for cand in (256, 128, 64, 32, 16, 8):
        if R % cand == 0 and cand * C * 4 <= (1 << 20):
            tm = cand
            break

    def body(w_ref, g_ref, m_ref, v_ref, d_ref, mo_ref, vo_ref):
        d_ref[...], mo_ref[...], vo_ref[...] = _adamw_val(w_ref[...], g_ref[...], m_ref[...], v_ref[...])

    return _rowcall(name, body, R, tm, [(w, ROW), (g, ROW), (m, ROW), (v, ROW)], [((R, C), F32, ROW)] * 3)


def _row_tile(rows, width, itemsize=4, limit=1 << 20):
    for cand in range(512, 0, -16):
        if rows % cand == 0 and cand * width * itemsize <= limit:
            return cand
    raise ValueError((rows, width))


def _cast_into_full(name, a, layer, q, kind, after):
    L, r, c = a.shape
    tm = _row_tile(r, c)
    if kind == "col":
        full, o_spec = (r, N_CHIPS * c), pl.BlockSpec((tm, c), lambda i, q_ref: (i, q_ref[0]))
    else:
        full, o_spec = (N_CHIPS * r, c), pl.BlockSpec((tm, c), lambda i, q_ref: (q_ref[0] * (r // tm) + i, 0))

    def body(q_ref, a_ref, after_ref, o_ref):
        o_ref[...] = a_ref[0].astype(BF16)

    return pl.pallas_call(
        body, name=name,
        grid_spec=pltpu.PrefetchScalarGridSpec(
            num_scalar_prefetch=1, grid=(r // tm,),
            in_specs=[pl.BlockSpec((1, tm, c), lambda i, q_ref: (layer, i, 0)), ANY], out_specs=o_spec),
        out_shape=jax.ShapeDtypeStruct(full, BF16), compiler_params=_params(("parallel",)),
    )(q.reshape(1).astype(jnp.int32), a, after)


def _sum4(name, g, rcv, q, kind, n):
    r, c = rcv.shape[1:]
    tm = _row_tile(r, c)
    if kind == "col":
        g_spec = pl.BlockSpec((tm, n), lambda i, q_ref: (i, q_ref[0]))
    else:
        g_spec = pl.BlockSpec((tm, c), lambda i, q_ref: (q_ref[0] * (n // tm) + i, 0))

    def body(q_ref, g_ref, r_ref, o_ref):
        acc = g_ref[...].astype(F32)
        for j in range(3):
            acc = acc + r_ref[j].astype(F32)
        o_ref[...] = acc

    return pl.pallas_call(
        body, name=name,
        grid_spec=pltpu.PrefetchScalarGridSpec(
            num_scalar_prefetch=1, grid=(r // tm,),
            in_specs=[g_spec, pl.BlockSpec((3, tm, c), lambda i, q_ref: (0, i, 0))],
            out_specs=pl.BlockSpec((tm, c), lambda i, q_ref: (i, 0))),
        out_shape=jax.ShapeDtypeStruct((r, c), F32), compiler_params=_params(("parallel",)),
    )(q.reshape(1).astype(jnp.int32), g, rcv)


def _adamw_sum(name, w, m, v, layer, mine, theirs, prev):
    L, r, c = w.shape
    tm = _row_tile(r, c, limit=3 << 19)
    lay = pl.BlockSpec((1, tm, c), lambda i: (layer, i, 0))
    flat = pl.BlockSpec((tm, c), lambda i: (i, 0))
    n_prev = 0 if prev is None else 4

    def body(w_ref, m_ref, v_ref, a_ref, b_ref, *rest):
        g_ref, d_ref, mo_ref, vo_ref = rest[n_prev:]
        g = a_ref[...] + b_ref[...]
        g_ref[0] = g
        d_ref[0], mo_ref[0], vo_ref[0] = _adamw_val(w_ref[0], g, m_ref[0], v_ref[0])

    return pl.pallas_call(
        body, name=name, grid=(r // tm,),
        in_specs=[lay, lay, lay, flat, flat] + [ANY] * n_prev, out_specs=[lay] * 4,
        out_shape=[jax.ShapeDtypeStruct((L, r, c), F32)] * 4,
        input_output_aliases={5 + k: k for k in range(n_prev)},
        compiler_params=_params(("parallel",)),
    )(w, m, v, mine, theirs, *(prev or ()))


def _sum8(name, gathered, own=None):
    R, N = gathered.shape
    P = R // 8

    def body(g_ref, *rest):
        o_ref = rest[-1]
        me = 4 * lax.axis_index("x") + 2 * lax.axis_index("y") + lax.axis_index("c")
        acc = None
        for d in range(8):
            blk = g_ref[d * P:(d + 1) * P, :]
            if own is not None:
                blk = jnp.where(me == d, rest[0][...], blk)
            acc = blk if d == 0 else acc + blk
        o_ref[...] = acc

    return pl.pallas_call(body, name=name, out_shape=jax.ShapeDtypeStruct((P, N), F32),
                          compiler_params=pltpu.CompilerParams(vmem_limit_bytes=VMEM_LIMIT),
                          )(gathered, *(() if own is None else (own,)))


ANY = pl.BlockSpec(memory_space=pl.ANY)


def _mesh_pos():
    x, y, c = lax.axis_index("x"), lax.axis_index("y"), lax.axis_index("c")
    other_chips = [(1 - x, y), (x, 1 - y), (1 - x, 1 - y)]
    return x, y, c, other_chips


def _allgather8(name, blk, after=None):
    m_per, n = blk.shape

    def body(x_ref, *rest):
        out_ref, send_sems, recv_sems, local_sem = rest[after is not None:]
        x, y, c, chips = _mesh_pos()
        me, sibling = (x, y, c), (x, y, 1 - c)

        def rows(px, py, pc):
            return out_ref.at[pl.ds((4 * px + 2 * py + pc) * m_per, m_per), :]

        def copy(k, block, to, src=None):
            return pltpu.make_async_remote_copy(
                src_ref=rows(*block) if src is None else src, dst_ref=rows(*block),
                send_sem=send_sems.at[k], recv_sem=recv_sems.at[k], device_id=to, device_id_type=MESH)

        mine = pltpu.make_async_copy(x_ref, rows(*me), local_sem)
        mine.start()
        first = [copy(0, me, sibling, src=x_ref)]
        first += [copy(1 + j, me, (*chip, c), src=x_ref) for j, chip in enumerate(chips)]
        for cp in first:
            cp.start()
        passed = [copy(4 + j, (*chip, c), sibling) for j, chip in enumerate(chips)]
        for j, chip in enumerate(chips):
            copy(1 + j, (*chip, c), me).wait_recv()
            passed[j].start()
        copy(0, sibling, me).wait_recv()
        for j, chip in enumerate(chips):
            copy(4 + j, (*chip, 1 - c), me).wait_recv()
        for cp in first + passed:
            cp.wait_send()
        mine.wait()

    return pl.pallas_call(
        body, name=name, out_shape=jax.ShapeDtypeStruct((8 * m_per, n), blk.dtype),
        in_specs=[pl.BlockSpec(memory_space=pltpu.VMEM)] + [ANY] * (after is not None),
        out_specs=pl.BlockSpec(memory_space=pltpu.VMEM),
        scratch_shapes=[pltpu.SemaphoreType.DMA((7,)), pltpu.SemaphoreType.DMA((7,)), pltpu.SemaphoreType.DMA],
        compiler_params=pltpu.CompilerParams(vmem_limit_bytes=VMEM_LIMIT),
    )(blk, *(() if after is None else (after,)))


BIG = dict(w_in=("col", "ab_w_in", 0), w_out=("row", "ab_w_out", 0), up0=("col", "ffn_up_w", 0),
           dn0=("row", "ffn_down_w", 0), pw1=("col", "conv_pw1_w", 0), pw2=("row", "conv_pw2_w", 0),
           up1=("col", "ffn_up_w", 1), dn1=("row", "ffn_down_w", 1))
N_CHIPS = 4
HBM = pl.BlockSpec(memory_space=pltpu.HBM)
SEM = pl.BlockSpec(memory_space=pltpu.SEMAPHORE)
EFFECT = pltpu.SideEffectType.DATAFLOW_SIDE_EFFECTING


def _region(kind, ref, q, n):
    if kind == "col":
        return ref.at[:, pl.ds(q * n, n)]
    return ref.at[pl.ds(q * n, n), :]


def _gather_plan(kind, n):
    def remote(src, land, pos):
        x, y, c, chips = pos
        mine = _region(kind, land, 2 * x + y, n)
        return [(mine, mine, (*chip, c)) for chip in chips]

    return ("gather", kind, n), remote


def _scatter_plan(kind, n):
    def remote(src, land, pos):
        _, _, c, chips = pos
        return [(_region(kind, src, 2 * chip[0] + chip[1], n), land.at[j], (*chip, c)) for j, chip in enumerate(chips)]

    return ("scatter", kind, n), remote


def _everyone_plan(rows):
    def remote(src, land, pos):
        x, y, c, _ = pos
        mine = land.at[pl.ds((4 * x + 2 * y + c) * rows, rows), :]
        flip = lambda v, bit: 1 - v if bit else v
        return [(src, mine, (flip(x, k & 4), flip(y, k & 2), flip(c, k & 1))) for k in range(1, 8)]

    return ("everyone", rows), remote


def _sibling_plan():
    def remote(src, land, pos):
        x, y, c, _ = pos
        return [(src, land, (x, y, 1 - c))]

    return ("sibling",), remote


def _split_start(name, items, after=None):
    n = len(items)
    plans = [it[2] for it in items]
    n_in = 2 * n + (after is not None)

    def body(*refs):
        srcs, lands = refs[:n], refs[n:2 * n]
        sends, recvs = refs[n_in:n_in + n], refs[n_in + n:n_in + 2 * n]
        token = refs[n_in + 4 * n]
        pos = _mesh_pos()
        for a, (_, remote) in enumerate(plans):
            for k, (s, d, dev) in enumerate(remote(srcs[a], lands[a], pos)):
                pltpu.make_async_remote_copy(src_ref=s, dst_ref=d, send_sem=sends[a].at[k], recv_sem=recvs[a].at[k],
                                             device_id=dev, device_id_type=MESH).start()
        token[...] = jnp.zeros_like(token)

    sems = [pltpu.SemaphoreType.DMA((it[3],)) for it in items]
    bufs = [pltpu.HBM(it[k].shape, it[k].dtype) for k in (0, 1) for it in items]
    outs = pl.pallas_call(
        body, name=name, out_shape=[*sems, *sems, *bufs, jax.ShapeDtypeStruct((8, LANES), F32)],
        in_specs=[HBM] * (2 * n) + [ANY] * (n_in - 2 * n),
        out_specs=[SEM] * (2 * n) + [HBM] * (2 * n) + [pl.BlockSpec(memory_space=pltpu.VMEM)],
        input_output_aliases={i: 2 * n + i for i in range(2 * n)},
        compiler_params=pltpu.CompilerParams(has_side_effects=EFFECT),
    )(*[pltpu.with_memory_space_constraint(it[k], pltpu.HBM) for k in (0, 1) for it in items],
      *(() if after is None else (after,)))
    state = [(items[a][2], items[a][3], outs[2 * n + a], outs[3 * n + a], outs[a], outs[n + a]) for a in range(n)]
    return state, outs[4 * n]


def _split_wait(name, state, after):
    n = len(state)

    def body(*refs):
        srcs, lands = refs[:n], refs[n:2 * n]
        sends, recvs = refs[2 * n:3 * n], refs[3 * n:4 * n]
        pos = _mesh_pos()
        for a, ((_, remote), *_) in enumerate(state):
            for k, (s, d, dev) in enumerate(remote(srcs[a], lands[a], pos)):
                cp = pltpu.make_async_remote_copy(src_ref=s, dst_ref=d, send_sem=sends[a].at[k], recv_sem=recvs[a].at[k],
                                                  device_id=dev, device_id_type=MESH)
                cp.wait_send()
                cp.wait_recv()

    bufs = [st[k] for k in (2, 3) for st in state]
    outs = pl.pallas_call(
        body, name=name, out_shape=[pltpu.HBM(b.shape, b.dtype) for b in bufs],
        in_specs=[HBM] * (2 * n) + [SEM] * (2 * n) + [ANY], out_specs=[HBM] * (2 * n),
        input_output_aliases={i: i for i in range(2 * n)},
        compiler_params=pltpu.CompilerParams(has_side_effects=EFFECT),
    )(*bufs, *[st[k] for k in (4, 5) for st in state], after)
    return outs[:n], outs[n:]


class _Weights:
    def __init__(self, w, q, after):
        unused = jnp.zeros((16, LANES), BF16)

        def item(name, after):
            kind, pname, layer = BIG[name]
            _, r, c = w[pname].shape
            land = _cast_into_full(f"cast_{name}", w[pname], layer, q, kind, after)
            return unused, land, _gather_plan(kind, c if kind == "col" else r), N_CHIPS - 1

        first, *rest = BIG
        state1, token1 = _split_start("gw_start_first", [item(first, after)], after)
        state2, self.token = _split_start("gw_start_rest", [item(name, token1) for name in rest], token1)
        self.pending = dict(zip(BIG, state1 + state2))
        self.ready = {}

    def get(self, name, after=None):
        if name not in self.ready:
            self.ready[name] = _split_wait(f"gw_wait_{name}", [self.pending.pop(name)], after)[1][0]
        return self.ready[name]


class _GradPipe:
    def __init__(self, q, w, m, v):
        self.q, self.w, self.m, self.v = q, w, m, v
        self.stage, self.results = {}, {}

    def start_small(self, dmod, lossv, small, after):
        self.small_names = [n for n in REPLICATED if n != "ada_b"] + list(SMALL_SHARDED)
        payload = [dmod.reshape(2, -1), lossv] + [small[n] for n in self.small_names]
        self.small_shapes = [p.shape for p in payload]
        packed = _pack(payload)
        land = jnp.zeros((8 * PACK_ROWS, packed.shape[1]), F32)
        self.small_state, token = _split_start("ag_grads_start", [(packed, land, _everyone_plan(PACK_ROWS), 7)], after)
        return token

    def wait_small(self, after):
        srcs, lands = _split_wait("ag_grads_wait", self.small_state, after)
        return srcs[0], lands[0]

    def scatter(self, group, grads, after=None):
        items = []
        for name, g in grads.items():
            kind = BIG[name][0]
            rows, cols = g.shape
            n = (cols if kind == "col" else rows) // N_CHIPS
            reg = (rows, n) if kind == "col" else (n, cols)
            items.append((g, lax.empty((N_CHIPS - 1, *reg), BF16), _scatter_plan(kind, n), N_CHIPS - 1))
        state, token = _split_start(f"gs_start_{group}", items, after)
        self.stage[group] = (list(grads), state)
        return token[0, 0]

    def collect(self, group, after):
        names, state = self.stage[group]
        srcs, lands = _split_wait(f"gs_wait_{group}", state, after)
        items = []
        for name, st, g, land in zip(names, state, srcs, lands):
            _, kind, n = st[0][0]
            part = _sum4(f"sum_{name}", g, land, self.q, kind, n)
            items.append((part, lax.empty(part.shape, F32), _sibling_plan(), 1))
        state, token = _split_start(f"sw_start_{group}", items)
        self.stage[group] = (names, state)
        return token

    def finish(self, group, after):
        names, state = self.stage.pop(group)
        srcs, lands = _split_wait(f"sw_wait_{group}", state, after)
        for name, mine, theirs in zip(names, srcs, lands):
            _, pname, layer = BIG[name]
            self.results[pname] = _adamw_sum(f"adamw_{name}", self.w[pname], self.m[pname], self.v[pname], layer,
                                             mine, theirs, self.results.get(pname))


PACK_ROWS = 8


def _pack(arrays):
    flat = jnp.concatenate([a.reshape(-1) for a in arrays])
    n = flat.shape[0]
    padded = -(-n // (PACK_ROWS * LANES)) * (PACK_ROWS * LANES)
    return jnp.pad(flat, (0, padded - n)).reshape(PACK_ROWS, padded // PACK_ROWS)


def _unpack(packed, shapes):
    flat = packed.reshape(-1)
    out, off = [], 0
    for s in shapes:
        n = 1
        for d in s:
            n *= d
        out.append(flat[off:off + n].reshape(s))
        off += n
    return out


REPLICATED = ("ada_b", "norm_mix_g", "norm_ffn_g", "a_vnorm_g", "a_spatial_w", "a_spatial_b", "b_q_norm_g",
              "b_k_norm_g", "ffn_dw_b")
SMALL_SHARDED = ("conv_pw1_b", "conv_dw_w", "conv_dw_b", "conv_ln_g", "conv_ln_b", "conv_pw2_b", "ffn_dw_w")
WEIGHTS = ("ada_w", "ada_b", "norm_mix_g", "norm_ffn_g", "ab_w_in", "a_vnorm_g", "a_spatial_w", "a_spatial_b",
           "b_q_norm_g", "b_k_norm_g", "ab_w_out", "conv_pw1_w", "conv_pw1_b", "conv_dw_w", "conv_dw_b", "conv_ln_g",
           "conv_ln_b", "conv_pw2_w", "conv_pw2_b", "ffn_up_w", "ffn_dw_w", "ffn_dw_b", "ffn_down_w")

def kernel(x, c, positions, ada_w, ada_b, norm_mix_g, norm_ffn_g, ab_w_in, a_vnorm_g, a_spatial_w, a_spatial_b, b_q_norm_g, b_k_norm_g, ab_w_out, conv_pw1_w, conv_pw1_b, conv_dw_w, conv_dw_b, conv_ln_g, conv_ln_b, conv_pw2_w, conv_pw2_b, ffn_up_w, ffn_dw_w, ffn_dw_b, ffn_down_w, loss_target, m_ada_w, m_ada_b, m_norm_mix_g, m_norm_ffn_g, m_ab_w_in, m_a_vnorm_g, m_a_spatial_w, m_a_spatial_b, m_b_q_norm_g, m_b_k_norm_g, m_ab_w_out, m_conv_pw1_w, m_conv_pw1_b, m_conv_dw_w, m_conv_dw_b, m_conv_ln_g, m_conv_ln_b, m_conv_pw2_w, m_conv_pw2_b, m_ffn_up_w, m_ffn_dw_w, m_ffn_dw_b, m_ffn_down_w, v_ada_w, v_ada_b, v_norm_mix_g, v_norm_ffn_g, v_ab_w_in, v_a_vnorm_g, v_a_spatial_w, v_a_spatial_b, v_b_q_norm_g, v_b_k_norm_g, v_ab_w_out, v_conv_pw1_w, v_conv_pw1_b, v_conv_dw_w, v_conv_dw_b, v_conv_ln_g, v_conv_ln_b, v_conv_pw2_w, v_conv_pw2_b, v_ffn_up_w, v_ffn_dw_w, v_ffn_dw_b, v_ffn_down_w):
    w = dict(ada_w=ada_w, ada_b=ada_b, norm_mix_g=norm_mix_g, norm_ffn_g=norm_ffn_g, ab_w_in=ab_w_in, a_vnorm_g=a_vnorm_g, a_spatial_w=a_spatial_w, a_spatial_b=a_spatial_b, b_q_norm_g=b_q_norm_g, b_k_norm_g=b_k_norm_g, ab_w_out=ab_w_out, conv_pw1_w=conv_pw1_w, conv_pw1_b=conv_pw1_b, conv_dw_w=conv_dw_w, conv_dw_b=conv_dw_b, conv_ln_g=conv_ln_g, conv_ln_b=conv_ln_b, conv_pw2_w=conv_pw2_w, conv_pw2_b=conv_pw2_b, ffn_up_w=ffn_up_w, ffn_dw_w=ffn_dw_w, ffn_dw_b=ffn_dw_b, ffn_down_w=ffn_down_w)
    m = dict(ada_w=m_ada_w, ada_b=m_ada_b, norm_mix_g=m_norm_mix_g, norm_ffn_g=m_norm_ffn_g, ab_w_in=m_ab_w_in, a_vnorm_g=m_a_vnorm_g, a_spatial_w=m_a_spatial_w, a_spatial_b=m_a_spatial_b, b_q_norm_g=m_b_q_norm_g, b_k_norm_g=m_b_k_norm_g, ab_w_out=m_ab_w_out, conv_pw1_w=m_conv_pw1_w, conv_pw1_b=m_conv_pw1_b, conv_dw_w=m_conv_dw_w, conv_dw_b=m_conv_dw_b, conv_ln_g=m_conv_ln_g, conv_ln_b=m_conv_ln_b, conv_pw2_w=m_conv_pw2_w, conv_pw2_b=m_conv_pw2_b, ffn_up_w=m_ffn_up_w, ffn_dw_w=m_ffn_dw_w, ffn_dw_b=m_ffn_dw_b, ffn_down_w=m_ffn_down_w)
    v = dict(ada_w=v_ada_w, ada_b=v_ada_b, norm_mix_g=v_norm_mix_g, norm_ffn_g=v_norm_ffn_g, ab_w_in=v_ab_w_in, a_vnorm_g=v_a_vnorm_g, a_spatial_w=v_a_spatial_w, a_spatial_b=v_a_spatial_b, b_q_norm_g=v_b_q_norm_g, b_k_norm_g=v_b_k_norm_g, ab_w_out=v_ab_w_out, conv_pw1_w=v_conv_pw1_w, conv_pw1_b=v_conv_pw1_b, conv_dw_w=v_conv_dw_w, conv_dw_b=v_conv_dw_b, conv_ln_g=v_conv_ln_g, conv_ln_b=v_conv_ln_b, conv_pw2_w=v_conv_pw2_w, conv_pw2_b=v_conv_pw2_b, ffn_up_w=v_ffn_up_w, ffn_dw_w=v_ffn_dw_w, ffn_dw_b=v_ffn_dw_b, ffn_down_w=v_ffn_down_w)
    S, D = x.shape[1], x.shape[2]
    xi, yi, ci = lax.axis_index("x"), lax.axis_index("y"), lax.axis_index("c")
    q = 2 * xi + yi
    b = 2 * q + ci
    take_dev = lambda g: g.reshape(8, PACK_ROWS, -1)

    c_all = _allgather8("ag_c", c.reshape(PACK_ROWS, D // PACK_ROWS)).reshape(8, D)
    n_ada = ada_w.shape[2]
    mod_sh = _ada_fwd("ada_fwd", c_all, ada_w, lax.dynamic_slice_in_dim(ada_b, q * n_ada, n_ada, axis=1))
    sh_shapes = [mod_sh.shape] + [w[n].shape for n in SMALL_SHARDED]
    gathered_mod = _allgather8("ag_mod", _pack([mod_sh] + [w[n] for n in SMALL_SHARDED]))
    per_chip = [_unpack(blk, sh_shapes) for blk in take_dev(gathered_mod)[0::2]]
    mod_g = jnp.stack([pc[0] for pc in per_chip])
    mod_mine = lax.dynamic_index_in_dim(mod_g, b, axis=2, keepdims=False)
    mod = mod_mine.transpose(1, 0, 2).reshape(2, 6, D)
    sp = {n: jnp.concatenate([pc[1 + i] for pc in per_chip], axis=-1) for i, n in enumerate(SMALL_SHARDED)}
    sp.update({n: w[n] for n in REPLICATED if n != "ada_b"})

    wb = _Weights(w, q, gathered_mod)
    mod = mod + wb.token[0, 0]

    pipe = _GradPipe(q, w, m, v)
    lossv, grad_x, late = _local_step(x[0], loss_target[0], positions[0], mod, wb, sp, pipe)

    pipe.finish("g1", grad_x)
    pipe.finish("g2", pipe.results["ffn_up_w"][0])
    swapped = pipe.collect("g3", pipe.results["conv_pw1_w"][0])
    swapped = pipe.collect("g4", swapped)

    own, gathered = pipe.wait_small(swapped)
    totals = _unpack(_sum8("sum_grads", gathered, own), pipe.small_shapes)
    grads = dict(zip(["ada_b", "loss_columns"] + pipe.small_names, totals))
    loss = 0.5 * jnp.sum(grads.pop("loss_columns")) / D
    late_g = _allgather8("ag_late", _pack(list(late)), gathered)
    late_tot = _unpack(_sum8("sum_late", late_g), [(3, D)])[0]
    grads["ada_b"] = grads["ada_b"].at[0, :2 * D].add(late_tot[:2].reshape(-1))
    grads["norm_mix_g"] = grads["norm_mix_g"].at[0].add(late_tot[2])
    for n in SMALL_SHARDED:
        n_sh = w[n].shape[-1]
        grads[n] = lax.dynamic_slice_in_dim(grads[n], q * n_sh, n_sh, axis=grads[n].ndim - 1)
    dmod_of = lambda packed: packed.reshape(packed.shape[0] // PACK_ROWS, -1)[:, :2 * 6 * D].reshape(-1, 2, 6 * D)
    dmod_all = jnp.where((jnp.arange(8) == b)[:, None, None], dmod_of(own), dmod_of(gathered))
    late_all = take_dev(late_g).reshape(8, -1)[:, :3 * D].reshape(8, 3, D)
    dmod_all = dmod_all.at[:, 0, :2 * D].add(late_all[:, :2].reshape(8, 2 * D))
    dmod_sh = lax.dynamic_slice_in_dim(dmod_all, q * n_ada, n_ada, axis=2).transpose(1, 0, 2)

    pipe.finish("g3", dmod_sh)
    pipe.finish("g4", pipe.results["ffn_up_w"][0])
    grads["ada_w"], delta_ada, m_ada, v_ada = _ada_update("ada_update", c_all, dmod_sh, ada_w, m_ada_w, v_ada_w)
    delta, new_m, new_v = dict(ada_w=delta_ada), dict(ada_w=m_ada), dict(ada_w=v_ada)
    rest = list(REPLICATED) + list(SMALL_SHARDED)
    rest_shapes = [w[n].shape for n in rest]
    outs = _adamw("adamw_small", *[_pack([src[n].reshape(w[n].shape) for n in rest]) for src in (w, grads, m, v)])
    for tgt, packed in zip((delta, new_m, new_v), outs):
        tgt.update(dict(zip(rest, _unpack(packed, rest_shapes))))
    for n in rest:
        grads[n] = grads[n].reshape(w[n].shape)
    pipe.finish("g5", pipe.collect("g5", outs[0]))
    for n, res in pipe.results.items():
        grads[n], delta[n], new_m[n], new_v[n] = res

    return (loss, grad_x[None], *[grads[n] for n in WEIGHTS], *[delta[n] for n in WEIGHTS],
            *[new_m[n] for n in WEIGHTS], *[new_v[n] for n in WEIGHTS])
```

```python
import functools

import jax
import jax.numpy as jnp
from jax import lax
from jax.experimental import pallas as pl
from jax.experimental.pallas import tpu as pltpu

F32, BF16 = jnp.float32, jnp.bfloat16
EPS = 1e-6
NEG = -1e30
ROPE_THETA = 10000.0
LANES = 128
VMEM_LIMIT = 56 * 1024 * 1024
ADAM_LR, ADAM_B1, ADAM_B2, ADAM_EPS, ADAM_WD, ADAM_STEP = 0.001, 0.9, 0.999, 1e-08, 0.01, 10
MESH = pl.DeviceIdType.MESH


def _params(sem):
    return pltpu.CompilerParams(dimension_semantics=sem, vmem_limit_bytes=VMEM_LIMIT)


_DN = {"nn": (((1,), (0,)), ((), ())), "nt": (((1,), (1,)), ((), ())), "tn": (((0,), (0,)), ((), ()))}


def _matmul(name, a, b, mode, out_dtype, tm=512, tn=512, tk=1024, bias=None, after=None):
    if mode == "nn":
        (M, K), N = a.shape, b.shape[1]
    elif mode == "nt":
        (M, K), N = a.shape, b.shape[0]
    else:
        (K, M), N = a.shape, b.shape[1]
    tm, tn, tk = min(tm, M), min(tn, N), min(tk, K)
    assert M % tm == 0 and N % tn == 0 and K % tk == 0, (name, M, N, K, tm, tn, tk)
    nk = K // tk
    if mode == "tn":
        a_spec = pl.BlockSpec((tk, tm), lambda i, j, k: (k, i))
    else:
        a_spec = pl.BlockSpec((tm, tk), lambda i, j, k: (i, k))
    if mode == "nt":
        b_spec = pl.BlockSpec((tn, tk), lambda i, j, k: (j, k))
    else:
        b_spec = pl.BlockSpec((tk, tn), lambda i, j, k: (k, j))
    in_specs, args = [a_spec, b_spec], [a, b]
    if bias is not None:
        in_specs.append(pl.BlockSpec((1, tn), lambda i, j, k: (0, j)))
        args.append(bias)
    if after is not None:
        in_specs.append(pl.BlockSpec(memory_space=pl.ANY))
        args.append(after)
    n_in = len(args)

    def body(*refs):
        a_ref, b_ref, o_ref = refs[0], refs[1], refs[n_in]
        p = lax.dot_general(a_ref[...], b_ref[...], _DN[mode], preferred_element_type=F32)

        def finish(acc):
            if bias is not None:
                acc = acc + refs[2][...]
            o_ref[...] = acc.astype(o_ref.dtype)

        if nk == 1:
            finish(p)
        else:
            acc_ref = refs[n_in + 1]
            k = pl.program_id(2)

            @pl.when(k == 0)
            def _():
                acc_ref[...] = p

            @pl.when(k > 0)
            def _():
                acc_ref[...] += p

            @pl.when(k == nk - 1)
            def _():
                finish(acc_ref[...])

    return pl.pallas_call(
        body, name=name, grid=(M // tm, N // tn, nk), in_specs=in_specs,
        out_specs=pl.BlockSpec((tm, tn), lambda i, j, k: (i, j)),
        out_shape=jax.ShapeDtypeStruct((M, N), out_dtype),
        scratch_shapes=[pltpu.VMEM((tm, tn), F32)] if nk > 1 else [],
        compiler_params=_params(("parallel", "parallel", "arbitrary")),
    )(*args)


def _rowcall(name, body, nrows, tm, ins, outs, scratch=()):
    nblk = nrows // tm
    assert nrows % tm == 0

    def spec(kind, shape):
        k = kind[0]
        if k == "row":
            cw, cb = kind[1] or shape[-1], kind[2]
            return pl.BlockSpec((tm, cw), lambda i: (i, cb))
        if k == "prev":
            hb, cw, cb = kind[1], kind[2] or shape[-1], kind[3]
            r = tm // hb
            return pl.BlockSpec((hb, cw), lambda i: (jnp.maximum(i * r - 1, 0), cb))
        if k == "next":
            hb, cw, cb = kind[1], kind[2] or shape[-1], kind[3]
            r, last = tm // hb, nrows // hb - 1
            return pl.BlockSpec((hb, cw), lambda i: (jnp.minimum((i + 1) * r, last), cb))
        if k == "off":
            off, cw, cb = kind[1], kind[2] or shape[-1], kind[3]
            return pl.BlockSpec((tm, cw), lambda i: (jnp.clip(i + off, 0, nblk - 1), cb))
        if k == "cls":
            dil, off = kind[1], kind[2]
            return pl.BlockSpec((dil, tm // dil, shape[-1]), lambda i: (0, jnp.clip(i + off, 0, nblk - 1), 0))
        nd = len(shape)
        return pl.BlockSpec(tuple(shape), lambda i: (0,) * nd)

    has_acc = any(o[2][0] == "acc" for o in outs)
    return pl.pallas_call(
        body, name=name, grid=(nblk,),
        in_specs=[spec(kind, a.shape) for a, kind in ins],
        out_specs=[spec(kind, shape) for shape, _, kind in outs],
        out_shape=[jax.ShapeDtypeStruct(tuple(shape), dt) for shape, dt, _ in outs],
        scratch_shapes=list(scratch),
        compiler_params=_params(("arbitrary",) if has_acc else ("parallel",)),
    )(*[a for a, _ in ins])


ROW = ("row", None, 0)
FULL = ("full",)
ACC = ("acc",)


def _colsum(x):
    return jnp.sum(x, axis=0, keepdims=True)


def _acc_add(i, ref, val, rows=None):
    idx = (slice(None),) * len(ref.shape) if rows is None else rows

    @pl.when(i == 0)
    def _():
        ref[idx] = val

    @pl.when(i > 0)
    def _():
        ref[idx] = ref[idx] + val


def _sigmoid(x):
    return 1.0 / (1.0 + jnp.exp(-x))


def _gelu(x):
    return 0.5 * x * (1.0 + lax.erf(x * (2.0 ** -0.5)))


def _gelu_grad(x):
    return 0.5 * (1.0 + lax.erf(x * (2.0 ** -0.5))) + x * jnp.exp(-0.5 * x * x) * ((2.0 * jnp.pi) ** -0.5)


SUBLANES = 8


def _phases(ext, sign):
    n = ext.shape[0]
    return [ext if b == 0 else pltpu.roll(ext, b if sign > 0 else n - b, axis=0) for b in range(SUBLANES)]


def _shift_prev(phases, s, hb):
    a, b = divmod(s, SUBLANES)
    return phases[b][hb - SUBLANES * a:phases[b].shape[0] - SUBLANES * a]


def _shift_next(phases, s, tm):
    a, b = divmod(s, SUBLANES)
    return phases[b][SUBLANES * a:SUBLANES * a + tm]


def _rms_mod_val(x, g, sc, sh):
    r = lax.rsqrt(jnp.mean(x * x, axis=-1, keepdims=True) + EPS)
    return x * r * g * (1.0 + sc) + sh


def _mod_first(name, x, g, sc, sh, tm=512):
    S, D = x.shape

    def body(x_ref, g_ref, sc_ref, sh_ref, h_ref):
        h_ref[...] = _rms_mod_val(x_ref[...], g_ref[...], sc_ref[...], sh_ref[...]).astype(BF16)

    return _rowcall(name, body, S, tm, [(x, ROW), (g, FULL), (sc, FULL), (sh, FULL)], [((S, D), BF16, ROW)])[0]


def _resid_mod(name, x, y, gate, g, sc, sh, tm=512):
    S, D = x.shape

    def body(x_ref, y_ref, gate_ref, g_ref, sc_ref, sh_ref, xo_ref, h_ref):
        xn = x_ref[...] + gate_ref[...] * y_ref[...]
        xo_ref[...] = xn
        h_ref[...] = _rms_mod_val(xn, g_ref[...], sc_ref[...], sh_ref[...]).astype(BF16)

    return _rowcall(name, body, S, tm,
                    [(x, ROW), (y, ROW), (gate, FULL), (g, FULL), (sc, FULL), (sh, FULL)],
                    [((S, D), F32, ROW), ((S, D), BF16, ROW)])


def _gate_bwd_val(i, d, y_ref, gate_ref, dy_ref, dg_ref, db_ref):
    dy = d * gate_ref[...]
    dy_ref[...] = dy.astype(BF16)
    _acc_add(i, dg_ref, _colsum(d * y_ref[...]))
    _acc_add(i, db_ref, _colsum(dy))


GATE_OUTS = lambda S, D: [((S, D), BF16, ROW), ((1, D), F32, ACC), ((1, D), F32, ACC)]


def _loss_head(name, x, y, gate, tgt, tm=512):
    S, D = x.shape

    def body(x_ref, y_ref, gate_ref, t_ref, dx_ref, l_ref, dy_ref, dg_ref, db_ref):
        i = pl.program_id(0)
        err = x_ref[...] + gate_ref[...] * y_ref[...] - t_ref[...]
        d = err * (1.0 / D)
        dx_ref[...] = d
        _acc_add(i, l_ref, _colsum(err * err))
        _gate_bwd_val(i, d, y_ref, gate_ref, dy_ref, dg_ref, db_ref)

    return _rowcall(name, body, S, tm, [(x, ROW), (y, ROW), (gate, FULL), (tgt, ROW)],
                    [((S, D), F32, ROW), ((1, D), F32, ACC)] + GATE_OUTS(S, D))


def _mod_bwd(name, dxo, dh, x, g, sc, y=None, gate=None, tm=512):
    S, D = x.shape
    gated = y is not None

    def body(d_ref, dh_ref, x_ref, g_ref, sc_ref, *rest):
        dx_ref, dsh_ref, dsc_ref, dg_ref = rest[2 * gated:2 * gated + 4]
        i = pl.program_id(0)
        xv, dh_v, gv = x_ref[...], dh_ref[...], g_ref[...]
        r = lax.rsqrt(jnp.mean(xv * xv, axis=-1, keepdims=True) + EPS)
        n = xv * r
        _acc_add(i, dsh_ref, _colsum(dh_v))
        _acc_add(i, dsc_ref, _colsum(dh_v * (n * gv)))
        dy = dh_v * (1.0 + sc_ref[...])
        _acc_add(i, dg_ref, _colsum(dy * n))
        dn = dy * gv
        dx = d_ref[...] + r * (dn - n * jnp.mean(dn * n, axis=-1, keepdims=True))
        dx_ref[...] = dx
        if gated:
            _gate_bwd_val(i, dx, rest[0], rest[1], *rest[6:9])

    ins = [(dxo, ROW), (dh, ROW), (x, ROW), (g, FULL), (sc, FULL)] + ([(y, ROW), (gate, FULL)] if gated else [])
    outs = [((S, D), F32, ROW), ((1, D), F32, ACC), ((1, D), F32, ACC), ((1, D), F32, ACC)]
    return _rowcall(name, body, S, tm, ins, outs + (GATE_OUTS(S, D) if gated else []))


HB16 = 16


def _conv3_val(ph, w, b, hb):
    return w[2:3] * _shift_prev(ph, 0, hb) + w[1:2] * _shift_prev(ph, 1, hb) + w[0:1] * _shift_prev(ph, 2, hb) + b


def _halo_first(halo_ref, tile_ref, live):
    return _phases(jnp.concatenate([halo_ref[...].astype(F32) * live, tile_ref[...].astype(F32)], axis=0), 1)


def _glu3_fwd(name, u, w, b, tm=128):
    S, F2 = u.shape
    Fh = F2 // 2

    def body(ua_ref, ub_ref, ha_ref, hb_ref, w_ref, b_ref, o_ref, z_ref):
        live = (pl.program_id(0) > 0).astype(F32)
        wv, bv = w_ref[...], b_ref[...]
        za = _conv3_val(_halo_first(ha_ref, ua_ref, live), wv[:, :Fh], bv[:, :Fh], HB16)
        zb = _conv3_val(_halo_first(hb_ref, ub_ref, live), wv[:, Fh:], bv[:, Fh:], HB16)
        o_ref[...] = (za * _sigmoid(za) * zb).astype(BF16)
        z_ref[:, :Fh] = za.astype(BF16)
        z_ref[:, Fh:] = zb.astype(BF16)

    return _rowcall(name, body, S, tm,
                    [(u, ("row", Fh, 0)), (u, ("row", Fh, 1)), (u, ("prev", HB16, Fh, 0)), (u, ("prev", HB16, Fh, 1)),
                     (w, FULL), (b, FULL)],
                    [((S, Fh), BF16, ROW), ((S, F2), BF16, ROW)])


def _glu3_bwd(name, z, dhm, tm=128):
    S, F2 = z.shape
    Fh = F2 // 2

    def body(za_ref, zb_ref, d_ref, dz_ref, db_ref):
        i = pl.program_id(0)
        za, zb, d = za_ref[...].astype(F32), zb_ref[...].astype(F32), d_ref[...]
        sg = _sigmoid(za)
        da = d * zb * (sg * (1.0 + za * (1.0 - sg)))
        db = d * (za * sg)
        dz_ref[:, :Fh] = da.astype(BF16)
        dz_ref[:, Fh:] = db.astype(BF16)
        _acc_add(i, db_ref, jnp.concatenate([_colsum(da), _colsum(db)], axis=1))

    return _rowcall(name, body, S, tm, [(z, ("row", Fh, 0)), (z, ("row", Fh, 1)), (dhm, ROW)],
                    [((S, F2), BF16, ROW), ((1, F2), F32, ACC)])


def _conv3_bwd(name, dz, u, w, tm=128):
    S, F2 = dz.shape
    nblk = S // tm
    K = w.shape[0]

    def body(d_ref, n_ref, u_ref, w_ref, o_ref, dw_ref):
        i = pl.program_id(0)
        live = (i < nblk - 1).astype(F32)
        ph = _phases(jnp.concatenate([d_ref[...].astype(F32), n_ref[...].astype(F32) * live], axis=0), -1)
        wv, uv = w_ref[...], u_ref[...].astype(F32)
        shifted = [_shift_next(ph, K - 1 - k, tm) for k in range(K)]
        o_ref[...] = functools.reduce(lambda a, t: a + t, [wv[k:k + 1] * shifted[k] for k in range(K)]).astype(BF16)
        for k in range(K):
            _acc_add(i, dw_ref, _colsum(uv * shifted[k]), rows=(slice(k, k + 1), slice(None)))

        @pl.when(i == 0)
        def _():
            dw_ref[K:, :] = jnp.zeros((dw_ref.shape[0] - K, F2), F32)

    return _rowcall(name, body, S, tm, [(dz, ROW), (dz, ("next", HB16, None, 0)), (u, ROW), (w, FULL)],
                    [((S, F2), BF16, ROW), ((SUBLANES, F2), F32, ACC)])


def _ffn_fwd(name, h, w_up, w_dn, dw_w, dw_b):
    u = _matmul(f"{name}_up", h, w_up, "nn", BF16, tm=2048)
    hm, z = _glu3_fwd(f"{name}_glu", u, dw_w, dw_b)
    f = _matmul(f"{name}_dn", hm, w_dn, "nn", F32, tm=1024, tn=1024, tk=w_dn.shape[0])
    return f, (u, hm, z)


def _ffn_bwd(name, dy, h, u, hm, z, w_up, w_dn, dw_w, dw_b):
    Fh = w_dn.shape[0]
    dhm = _matmul(f"{name}_ddn_x", dy, w_dn, "nt", F32, tm=1024, tn=Fh // 2)
    g_dn = _matmul(f"{name}_ddn_w", hm, dy, "tn", BF16, tm=Fh // 2, tn=1024, tk=2048)
    dz, g_dw_b = _glu3_bwd(f"{name}_dglu", z, dhm)
    du, taps = _conv3_bwd(f"{name}_dconv", dz, u, dw_w)
    g_up = _matmul(f"{name}_dup_w", h, du, "tn", BF16, tm=1024, tn=Fh // 2, tk=2048)
    dh = _matmul(f"{name}_dup_x", du, w_up, "nt", F32, tm=1024, tn=1024, tk=Fh)
    return dh, dict(up=g_up, dn=g_dn, dw_w=taps[0:dw_w.shape[0]], dw_b=g_dw_b)


HB32 = 32


def _glu31_fwd(name, p, w, b, tm=256):
    S, D2 = p.shape
    D = D2 // 2
    K = w.shape[0]

    def body(a_ref, g_ref, ha_ref, hg_ref, w_ref, b_ref, o_ref):
        live = (pl.program_id(0) > 0).astype(F32)
        y1 = a_ref[...] * _sigmoid(g_ref[...])
        ph = _phases(jnp.concatenate([ha_ref[...] * _sigmoid(hg_ref[...]) * live, y1], axis=0), 1)
        wv = w_ref[...]
        acc = b_ref[...] + wv[K - 1:K] * y1
        for k in range(K - 1):
            acc = acc + wv[k:k + 1] * _shift_prev(ph, K - 1 - k, HB32)
        o_ref[...] = acc

    return _rowcall(name, body, S, tm,
                    [(p, ("row", D, 0)), (p, ("row", D, 1)), (p, ("prev", HB32, D, 0)), (p, ("prev", HB32, D, 1)),
                     (w, FULL), (b, FULL)],
                    [((S, D), F32, ROW)])[0]


def _ln_silu_fwd(name, y2, g, b, tm=512):
    S, D = y2.shape

    def body(y_ref, g_ref, b_ref, o_ref):
        y = y_ref[...]
        mu = jnp.mean(y, axis=-1, keepdims=True)
        yc = y - mu
        rs = lax.rsqrt(jnp.mean(yc * yc, axis=-1, keepdims=True) + EPS)
        y3 = yc * rs * g_ref[...] + b_ref[...]
        o_ref[...] = (y3 * _sigmoid(y3)).astype(BF16)

    return _rowcall(name, body, S, tm, [(y2, ROW), (g, FULL), (b, FULL)], [((S, D), BF16, ROW)])[0]


def _ln_silu_bwd(name, y2, dy4, g, b, tm=512):
    S, D = y2.shape

    def body(y_ref, d_ref, g_ref, b_ref, o_ref, dg_ref, db_ref):
        i = pl.program_id(0)
        y, gv = y_ref[...], g_ref[...]
        mu = jnp.mean(y, axis=-1, keepdims=True)
        yc = y - mu
        rs = lax.rsqrt(jnp.mean(yc * yc, axis=-1, keepdims=True) + EPS)
        n = yc * rs
        y3 = n * gv + b_ref[...]
        sg = _sigmoid(y3)
        dy3 = d_ref[...] * (sg * (1.0 + y3 * (1.0 - sg)))
        _acc_add(i, db_ref, _colsum(dy3))
        _acc_add(i, dg_ref, _colsum(dy3 * n))
        dn = dy3 * gv
        o_ref[...] = rs * (dn - jnp.mean(dn, axis=-1, keepdims=True) - n * jnp.mean(dn * n, axis=-1, keepdims=True))

    return _rowcall(name, body, S, tm, [(y2, ROW), (dy4, ROW), (g, FULL), (b, FULL)],
                    [((S, D), F32, ROW), ((1, D), F32, ACC), ((1, D), F32, ACC)])


def _glu31_bwd(name, p, dy2, w, tm=256):
    S, D2 = p.shape
    D = D2 // 2
    K = w.shape[0]
    nblk = S // tm

    conv_rows, tap_rows, tap_group, tap_unroll = 16, SUBLANES, 4, 4

    def body(a_ref, g_ref, d_ref, dn_ref, w_ref, dp_ref, dw_ref, dcb_ref, dpb_ref, ph_d, y1_ref, dy1_ref, wb_ref):
        i = pl.program_id(0)
        live_next = (i < nblk - 1).astype(F32)
        a, sg, d = a_ref[...], _sigmoid(g_ref[...]), d_ref[...]
        y1_ref[...] = a * sg
        for b, ph in enumerate(_phases(jnp.concatenate([d, dn_ref[...] * live_next], axis=0), -1)):
            ph_d[b] = ph
        taps = [divmod(K - 1 - k, SUBLANES) for k in range(K)]

        @pl.when(i == 0)
        def _():
            for k in range(K):
                wb_ref[k] = jnp.broadcast_to(w_ref[k:k + 1, :], (SUBLANES, D))

        def conv_rows_at(rb, carry):
            r0 = pl.multiple_of(rb * conv_rows, conv_rows)
            accs = [jnp.zeros((SUBLANES, D), F32) for _ in range(conv_rows // SUBLANES)]
            for k, (rows8, phase) in enumerate(taps):
                wk = wb_ref[k]
                for u in range(len(accs)):
                    accs[u] = accs[u] + wk * ph_d[phase, pl.ds(r0 + SUBLANES * (rows8 + u), SUBLANES), :]
            for u, acc in enumerate(accs):
                dy1_ref[pl.ds(r0 + SUBLANES * u, SUBLANES), :] = acc
            return carry

        lax.fori_loop(0, tm // conv_rows, conv_rows_at, 0)
        for k0 in range(0, K, tap_group):
            group = taps[k0:k0 + tap_group]

            def tap_rows_at(rb, accs, group=group):
                for u in range(tap_unroll):
                    r0 = pl.multiple_of((rb * tap_unroll + u) * tap_rows, tap_rows)
                    yv = y1_ref[pl.ds(r0, tap_rows), :]
                    accs = tuple(acc + yv * ph_d[phase, pl.ds(r0 + SUBLANES * rows8, tap_rows), :]
                                 for acc, (rows8, phase) in zip(accs, group))
                return accs

            accs = lax.fori_loop(0, tm // (tap_rows * tap_unroll), tap_rows_at,
                                 tuple(jnp.zeros((tap_rows, D), F32) for _ in group))
            for j, acc in enumerate(accs):
                _acc_add(i, dw_ref, _colsum(acc), rows=(slice(k0 + j, k0 + j + 1), slice(None)))

        @pl.when(i == 0)
        def _():
            dw_ref[K:, :] = jnp.zeros((dw_ref.shape[0] - K, D), F32)

        _acc_add(i, dcb_ref, _colsum(d))
        dy1 = dy1_ref[...]
        da = dy1 * sg
        dg = dy1 * a * sg * (1.0 - sg)
        dp_ref[:, :D] = da.astype(BF16)
        dp_ref[:, D:] = dg.astype(BF16)
        _acc_add(i, dpb_ref, jnp.concatenate([_colsum(da), _colsum(dg)], axis=1))

    return _rowcall(name, body, S, tm,
                    [(p, ("row", D, 0)), (p, ("row", D, 1)), (dy2, ROW), (dy2, ("next", HB32, None, 0)), (w, FULL)],
                    [((S, D2), BF16, ROW), ((HB32, D), F32, ACC), ((1, D), F32, ACC), ((1, D2), F32, ACC)],
                    scratch=[pltpu.VMEM((SUBLANES, tm + HB32, D), F32), pltpu.VMEM((tm, D), F32),
                             pltpu.VMEM((tm, D), F32), pltpu.VMEM((K, SUBLANES, D), F32)])


CHUNK = 128
A_GROUPS = 4


def _group_ln(gv):
    ns, rss = [], []
    for g in range(A_GROUPS):
        xg = gv[:, g * LANES:(g + 1) * LANES]
        xc = xg - jnp.mean(xg, axis=-1, keepdims=True)
        rs = lax.rsqrt(jnp.mean(xc * xc, axis=-1, keepdims=True) + EPS)
        ns.append(xc * rs)
        rss.append(jnp.broadcast_to(rs, xg.shape))
    return jnp.concatenate(ns, axis=1), jnp.concatenate(rss, axis=1)


def _tril_mask():
    r = lax.broadcasted_iota(jnp.int32, (CHUNK, CHUNK), 0)
    c = lax.broadcasted_iota(jnp.int32, (CHUNK, CHUNK), 1)
    return r >= c


def _spatial(ws_ref, x, dn):
    mask = _tril_mask()
    rows = []
    for ci in range(x.shape[0] // CHUNK):
        cols = []
        for g in range(A_GROUPS):
            wm = jnp.where(mask, ws_ref[g], 0.0).astype(BF16)
            xb = x[ci * CHUNK:(ci + 1) * CHUNK, g * LANES:(g + 1) * LANES]
            cols.append(lax.dot_general(wm, xb, dn, preferred_element_type=F32))
        rows.append(jnp.concatenate(cols, axis=1))
    return jnp.concatenate(rows, axis=0)


def _mixa_fwd(name, z, vg, ws, bias_full, tm=512):
    S = z.shape[0]
    W = A_GROUPS * LANES

    def body(u_ref, v_ref, vg_ref, ws_ref, b_ref, o_ref):
        nh, _ = _group_ln(_gelu(v_ref[...]))
        vn = (nh * vg_ref[...]).astype(BF16)
        f = _spatial(ws_ref, vn, _DN["nn"]) + jnp.concatenate([b_ref[...]] * (tm // CHUNK), axis=0)
        o_ref[...] = (_gelu(u_ref[...]) * f).astype(BF16)

    return _rowcall(name, body, S, tm,
                    [(z, ("row", W, 0)), (z, ("row", W, 1)), (vg, FULL), (ws, FULL), (bias_full, FULL)],
                    [((S, W), BF16, ROW)])[0]


def _mixa_bwd(name, z, dyab, vg, ws, bias_full, tm=512):
    S = z.shape[0]
    W = A_GROUPS * LANES
    nch = tm // CHUNK

    def body(u_ref, v_ref, d_ref, vg_ref, ws_ref, b_ref, dz_ref, dws_ref, dbf_ref, dvg_ref):
        i = pl.program_id(0)
        u, v, d, vgv = u_ref[...], v_ref[...], d_ref[...], vg_ref[...]
        nh, rs = _group_ln(_gelu(v))
        vn = (nh * vgv).astype(BF16)
        f = _spatial(ws_ref, vn, _DN["nn"]) + jnp.concatenate([b_ref[...]] * nch, axis=0)
        dz_ref[:, :W] = (d * f * _gelu_grad(u)).astype(BF16)
        df = d * _gelu(u)
        dbf = df[0:CHUNK]
        for ci in range(1, nch):
            dbf = dbf + df[ci * CHUNK:(ci + 1) * CHUNK]
        _acc_add(i, dbf_ref, dbf)
        dfb = df.astype(BF16)
        mask = _tril_mask()
        for g in range(A_GROUPS):
            acc = jnp.zeros((CHUNK, CHUNK), F32)
            for ci in range(nch):
                blk = (slice(ci * CHUNK, (ci + 1) * CHUNK), slice(g * LANES, (g + 1) * LANES))
                acc = acc + lax.dot_general(dfb[blk], vn[blk], _DN["nt"], preferred_element_type=F32)
            _acc_add(i, dws_ref, jnp.where(mask, acc, 0.0)[None], rows=(slice(g, g + 1), slice(None), slice(None)))
        dvn = _spatial(ws_ref, dfb, _DN["tn"])
        _acc_add(i, dvg_ref, _colsum(dvn * nh))
        dnh = dvn * vgv
        parts = []
        for g in range(A_GROUPS):
            cs = slice(g * LANES, (g + 1) * LANES)
            dg_, ng = dnh[:, cs], nh[:, cs]
            parts.append(dg_ - jnp.mean(dg_, axis=-1, keepdims=True) - ng * jnp.mean(dg_ * ng, axis=-1, keepdims=True))
        dz_ref[:, W:] = (rs * jnp.concatenate(parts, axis=1) * _gelu_grad(v)).astype(BF16)

    return _rowcall(name, body, S, tm,
                    [(z, ("row", W, 0)), (z, ("row", W, 1)), (dyab, ("row", W, 0)), (vg, FULL), (ws, FULL),
                     (bias_full, FULL)],
                    [((S, 2 * W), BF16, ROW), ((A_GROUPS, CHUNK, CHUNK), F32, ACC), ((CHUNK, W), F32, ACC),
                     ((1, W), F32, ACC)])


HEAD = 64
N_HEADS = 8
BW = HEAD * N_HEADS
QB = 128
DILATIONS = (1, 4, 16)
QK_SCALE = HEAD ** -0.5


def _gsum64(x, ones_bd):
    x1 = x.astype(BF16)
    r1 = x - x1.astype(F32)
    x2 = r1.astype(BF16)
    x3 = (r1 - x2.astype(F32)).astype(BF16)
    dot = lambda t: jnp.dot(t, ones_bd, preferred_element_type=F32)
    return dot(x1) + dot(x2) + dot(x3)


def _swap32(x):
    n = x.shape[-1]
    up = pltpu.roll(x, n - HEAD // 2, axis=1)
    dn = pltpu.roll(x, HEAD // 2, axis=1)
    lane = lax.broadcasted_iota(jnp.int32, x.shape, 1)
    return jnp.where((lane % HEAD) < HEAD // 2, up, dn)


def _tile4(t):
    return jnp.concatenate([t] * (BW // LANES), axis=1)


def _stage_spec(tm):
    return pltpu.VMEM((BW // LANES, tm, LANES), F32)


def _to_classes(stage, x, dil):
    tm = x.shape[0]
    for j in range(BW // LANES):
        stage[j] = x[:, j * LANES:(j + 1) * LANES]
    return [jnp.concatenate([stage.at[j][pl.ds(r, tm // dil, stride=dil), :] for j in range(BW // LANES)], axis=1)
            for r in range(dil)]


def _from_classes(stage, cls, dil):
    rows = cls.shape[1]
    for r in range(dil):
        for j in range(BW // LANES):
            stage.at[j][pl.ds(r, rows, stride=dil), :] = cls[r, :, j * LANES:(j + 1) * LANES]
    return jnp.concatenate([stage[j] for j in range(BW // LANES)], axis=1)


def _cls_view(t, dil):
    return t if dil == 1 else t.reshape(dil, t.shape[0] // dil, t.shape[1])


def _cls_kind(dil, off=0):
    return ("off", off, None, 0) if dil == 1 else ("cls", dil, off)


def _cls_out(S, dil, dtype):
    return ((S, BW) if dil == 1 else (dil, S // dil, BW), dtype, _cls_kind(dil))


def _flat(t):
    return t.reshape(-1, t.shape[-1])


def _qkv_fwd(name, z, cos, sin, ones_bd, qg, kg, tm=512):
    S = z.shape[0]
    nd = len(DILATIONS)

    def body(q_ref, k_ref, v_ref, c_ref, s_ref, o_ref, qg_ref, kg_ref, *rest):
        outs, stage = rest[:3 * nd], rest[3 * nd]
        c, s, ob = _tile4(c_ref[...]), _tile4(s_ref[...]), o_ref[...]

        def norm_rope(x, g):
            r = lax.rsqrt(_gsum64(x * x, ob) * (1.0 / HEAD) + EPS)
            xn = x * r * g
            return xn * c + _swap32(xn) * s

        vals = [norm_rope(q_ref[...], qg_ref[...]) * QK_SCALE, norm_rope(k_ref[...], kg_ref[...]), v_ref[...]]
        for a, val in enumerate(vals):
            for b, dil in enumerate(DILATIONS):
                if dil == 1:
                    outs[nd * a + b][...] = val.astype(BF16)
                else:
                    for r, rows in enumerate(_to_classes(stage, val, dil)):
                        outs[nd * a + b][r] = rows.astype(BF16)

    outs = _rowcall(name, body, S, tm,
                    [(z, ("row", BW, 2)), (z, ("row", BW, 3)), (z, ("row", BW, 4)), (cos, ROW), (sin, ROW),
                     (ones_bd, FULL), (qg, FULL), (kg, FULL)],
                    [_cls_out(S, dil, BF16) for _ in range(3) for dil in DILATIONS], scratch=[_stage_spec(tm)])
    return [[_flat(outs[nd * a + b]) for a in range(3)] for b in range(nd)]


PAIR = 2 * HEAD


ATT_BLOCKS = 4
ATT_TM = ATT_BLOCKS * QB
ATT_PREV = ("prev", QB, None, 0)


def _key_rows(prev_ref, cur_ref, sb, ps):
    before = prev_ref[:, ps] if sb == 0 else cur_ref[(sb - 1) * QB:sb * QB, ps]
    return jnp.concatenate([before, cur_ref[sb * QB:(sb + 1) * QB, ps]], axis=0)


def _pair_scores(q_ref, kp_ref, kc_ref, sb, hp, half, seg_blocks):
    ps = slice(hp * PAIR, (hp + 1) * PAIR)
    mine = (lax.broadcasted_iota(jnp.int32, (1, PAIR), 1) >= HEAD) == (half == 1)
    qm = jnp.where(mine, q_ref[sb * QB:(sb + 1) * QB, ps], jnp.zeros((), BF16))
    kcat = _key_rows(kp_ref, kc_ref, sb, ps)
    s = lax.dot_general(qm, kcat, _DN["nt"], preferred_element_type=F32)
    qi = lax.broadcasted_iota(jnp.int32, (QB, 2 * QB), 0)
    kj = lax.broadcasted_iota(jnp.int32, (QB, 2 * QB), 1)
    has_prev = ((pl.program_id(0) * ATT_BLOCKS + sb) % seg_blocks) != 0
    valid = (kj >= qi) & (kj <= qi + QB) & ((kj >= QB) | has_prev)
    return mine, qm, kcat, s, valid


def _attn_fwd(name, q, k, v, dil):
    S = q.shape[0]
    seg_blocks = S // dil // QB

    def body(q_ref, kp_ref, kc_ref, vp_ref, vc_ref, o_ref, l_ref):
        for hp in range(N_HEADS // 2):
            ps = slice(hp * PAIR, (hp + 1) * PAIR)
            chains = [(sb, half) for sb in range(ATT_BLOCKS) for half in range(2)]
            sc = [_pair_scores(q_ref, kp_ref, kc_ref, sb, hp, half, seg_blocks) for sb, half in chains]
            ss = [jnp.where(valid, s, NEG) for _, _, _, s, valid in sc]
            ms = [jnp.max(s, axis=-1, keepdims=True) for s in ss]
            pv = [jnp.exp(s - m) for s, m in zip(ss, ms)]
            dens = [jnp.sum(p, axis=-1, keepdims=True) for p in pv]
            vcats = [_key_rows(vp_ref, vc_ref, sb, ps) for sb in range(ATT_BLOCKS)]
            outs = [jnp.dot(p.astype(BF16), vcats[sb], preferred_element_type=F32) / den
                    for p, den, (sb, _) in zip(pv, dens, chains)]
            lses = [jnp.broadcast_to(m + jnp.log(den), (QB, PAIR)) for m, den in zip(ms, dens)]
            for sb in range(ATT_BLOCKS):
                rows, upper = slice(sb * QB, (sb + 1) * QB), sc[2 * sb + 1][0]
                o_ref[rows, ps] = jnp.where(upper, outs[2 * sb + 1], outs[2 * sb])
                l_ref[rows, ps] = jnp.where(upper, lses[2 * sb + 1], lses[2 * sb])

    return _rowcall(name, body, S, ATT_TM, [(q, ROW), (k, ATT_PREV), (k, ROW), (v, ATT_PREV), (v, ROW)],
                    [((S, BW), F32, ROW)] * 2)


def _attn_bwd(name, q, k, v, do, lse, delta, dil):
    S = q.shape[0]
    seg_blocks = S // dil // QB

    def body(q_ref, kp_ref, kc_ref, vp_ref, vc_ref, do_ref, l_ref, dl_ref, dq_ref, dkc_ref, dkp_ref, dvc_ref, dvp_ref):
        for hp in range(N_HEADS // 2):
            ps = slice(hp * PAIR, (hp + 1) * PAIR)
            chains = [(sb, half) for sb in range(ATT_BLOCKS) for half in range(2)]
            rows = [slice(sb * QB, (sb + 1) * QB) for sb, _ in chains]
            cols = [hp * PAIR + half * HEAD for _, half in chains]
            sc = [_pair_scores(q_ref, kp_ref, kc_ref, sb, hp, half, seg_blocks) for sb, half in chains]
            pv = [jnp.where(valid, jnp.exp(s - l_ref[r, c:c + 1]), 0.0) for (_, _, _, s, valid), r, c in zip(sc, rows, cols)]
            vcats = [_key_rows(vp_ref, vc_ref, sb, ps) for sb in range(ATT_BLOCKS)]
            doms = [jnp.where(mine, do_ref[r, ps].astype(BF16), jnp.zeros((), BF16)) for (mine, *_), r in zip(sc, rows)]
            dps = [lax.dot_general(dom, vcats[sb], _DN["nt"], preferred_element_type=F32) for dom, (sb, _) in zip(doms, chains)]
            dss = [(p * (dp - dl_ref[r, c:c + 1])).astype(BF16) for p, dp, r, c in zip(pv, dps, rows, cols)]
            dqs = [jnp.dot(ds, kcat, preferred_element_type=F32) for ds, (_, _, kcat, _, _) in zip(dss, sc)]
            dks = [lax.dot_general(ds, qm, _DN["tn"], preferred_element_type=F32) for ds, (_, qm, *_) in zip(dss, sc)]
            dvs = [lax.dot_general(p.astype(BF16), dom, _DN["tn"], preferred_element_type=F32) for p, dom in zip(pv, doms)]
            for sb in range(ATT_BLOCKS):
                lo, hi = 2 * sb, 2 * sb + 1
                dk, dv = dks[lo] + dks[hi], dvs[lo] + dvs[hi]
                dq_ref[rows[lo], ps] = jnp.where(sc[hi][0], dqs[hi], dqs[lo])
                dkp_ref[rows[lo], ps] = dk[:QB]
                dkc_ref[rows[lo], ps] = dk[QB:]
                dvp_ref[rows[lo], ps] = dv[:QB]
                dvc_ref[rows[lo], ps] = dv[QB:]

    return _rowcall(name, body, S, ATT_TM,
                    [(q, ROW), (k, ATT_PREV), (k, ROW), (v, ATT_PREV), (v, ROW), do, (lse, ROW), (delta, ROW)],
                    [((S, BW), F32, ROW)] * 5)


def _merge_fwd(name, branches, tm=512):
    S = branches[0][0].shape[0]
    nd = len(DILATIONS)

    def body(*refs):
        ins, (y_ref, yb_ref), l_refs, stage = refs[:2 * nd], refs[2 * nd:2 * nd + 2], refs[2 * nd + 2:3 * nd + 2], refs[-1]
        os_, ls = [], []
        for b, dil in enumerate(DILATIONS):
            o, l = ins[2 * b][...], ins[2 * b + 1][...]
            os_.append(o if dil == 1 else _from_classes(stage, o, dil))
            ls.append(l if dil == 1 else _from_classes(stage, l, dil))
        m = functools.reduce(jnp.maximum, ls)
        es = [jnp.exp(l - m) for l in ls]
        den = functools.reduce(lambda a, e: a + e, es)
        y = functools.reduce(lambda a, t: a + t, [e * o for e, o in zip(es, os_)]) / den
        y_ref[...] = y
        yb_ref[...] = y.astype(BF16)
        lse = m + jnp.log(den)
        for b, dil in enumerate(DILATIONS):
            if dil == 1:
                l_refs[b][...] = lse
            else:
                for r, rows in enumerate(_to_classes(stage, lse, dil)):
                    l_refs[b][r] = rows

    ins = [(_cls_view(t, dil), _cls_kind(dil)) for pair, dil in zip(branches, DILATIONS) for t in pair]
    outs = _rowcall(name, body, S, tm, ins,
                    [((S, BW), F32, ROW), ((S, BW), BF16, ROW)] + [_cls_out(S, dil, F32) for dil in DILATIONS],
                    scratch=[_stage_spec(tm)])
    return outs[0], outs[1], [_flat(t) for t in outs[2:]]


def _delta(name, dyab, yb, ones_bd, tm=512):
    S = yb.shape[0]
    nd = len(DILATIONS)

    def body(d_ref, y_ref, o_ref, *rest):
        dl_refs, do_refs, stage = rest[:nd], rest[nd:2 * nd - 1], rest[-1]
        d = d_ref[...]
        dl = _gsum64(d * y_ref[...], o_ref[...])
        for b, dil in enumerate(DILATIONS):
            if dil == 1:
                dl_refs[b][...] = dl
            else:
                for r, rows in enumerate(_to_classes(stage, dl, dil)):
                    dl_refs[b][r] = rows
                for r, rows in enumerate(_to_classes(stage, d, dil)):
                    do_refs[b - 1][r] = rows.astype(BF16)

    outs = _rowcall(name, body, S, tm, [(dyab, ("row", BW, 1)), (yb, ROW), (ones_bd, FULL)],
                    [_cls_out(S, dil, F32) for dil in DILATIONS] + [_cls_out(S, dil, BF16) for dil in DILATIONS[1:]],
                    scratch=[_stage_spec(tm)])
    return [_flat(t) for t in outs[:nd]], [_flat(t) for t in outs[nd:]]


def _qkv_bwd(name, z, cos, sin, ones_bd, qg, kg, pieces):
    S = z.shape[0]
    nblk = S // QB

    def body(q_ref, k_ref, c_ref, s_ref, o_ref, qg_ref, kg_ref, *rest):
        pr, (dz_ref, dqg_ref, dkg_ref), stage = rest[:15], rest[15:18], rest[18]
        i = pl.program_id(0)
        c, s, ob = _tile4(c_ref[...]), _tile4(s_ref[...]), o_ref[...]
        dq = dk = dv = None
        for b, dil in enumerate(DILATIONS):
            a_q, a_kc, a_kp, a_vc, a_vp = [r[...] for r in pr[5 * b:5 * b + 5]]
            live = ((i + dil) < nblk).astype(F32)
            tq, tk, tv = a_q, a_kc + a_kp * live, a_vc + a_vp * live
            if dil > 1:
                tq, tk, tv = (_from_classes(stage, t, dil) for t in (tq, tk, tv))
            dq, dk, dv = (tq, tk, tv) if b == 0 else (dq + tq, dk + tk, dv + tv)

        def back(x, g, d_rot, acc_ref):
            r = lax.rsqrt(_gsum64(x * x, ob) * (1.0 / HEAD) + EPS)
            n = x * r
            dxn = d_rot * c + _swap32(d_rot * s)
            _acc_add(i, acc_ref, _colsum(dxn * n))
            dn = dxn * g
            return r * (dn - n * (_gsum64(dn * n, ob) * (1.0 / HEAD)))

        dz_ref[:, :BW] = back(q_ref[...], qg_ref[...], dq * QK_SCALE, dqg_ref).astype(BF16)
        dz_ref[:, BW:2 * BW] = back(k_ref[...], kg_ref[...], dk, dkg_ref).astype(BF16)
        dz_ref[:, 2 * BW:] = dv.astype(BF16)

    ins = [(z, ("row", BW, 2)), (z, ("row", BW, 3)), (cos, ROW), (sin, ROW), (ones_bd, FULL), (qg, FULL), (kg, FULL)]
    for piece, dil in zip(pieces, DILATIONS):
        a_q, a_kc, a_kp, a_vc, a_vp = (_cls_view(t, dil) for t in piece)
        own, prev = _cls_kind(dil), _cls_kind(dil, dil)
        ins += [(a_q, own), (a_kc, own), (a_kp, prev), (a_vc, own), (a_vp, prev)]
    return _rowcall(name, body, S, QB, ins,
                    [((S, 3 * BW), BF16, ROW), ((1, BW), F32, ACC), ((1, BW), F32, ACC)], scratch=[_stage_spec(QB)])


def _local_step(x0, tgt, pos, mod, wb, sp, pipe):
    S, D = x0.shape
    md = lambda l, j: mod[l, j:j + 1]
    sh_m, sc_m, g_m, sh_f, sc_f, g_f = ([md(l, j) for l in range(2)] for j in range(6))
    nm_g, nf_g = sp["norm_mix_g"], sp["norm_ffn_g"]

    inv_freq = 1.0 / (ROPE_THETA ** (jnp.arange(0, HEAD, 2, dtype=F32) / HEAD))
    ang = pos.astype(F32)[:, None] * inv_freq
    cs, sn = jnp.cos(ang), jnp.sin(ang)
    cos = jnp.concatenate([cs, cs, cs, cs], axis=1)
    sin = jnp.concatenate([-sn, sn, -sn, sn], axis=1)
    head_of = jnp.arange(BW) // HEAD
    ones_bd = (head_of[:, None] == head_of[None, :]).astype(BF16)
    qg = jnp.tile(sp["b_q_norm_g"].reshape(1, HEAD), (1, N_HEADS))
    kg = jnp.tile(sp["b_k_norm_g"].reshape(1, HEAD), (1, N_HEADS))
    vg = sp["a_vnorm_g"].reshape(1, A_GROUPS * LANES)
    ws = sp["a_spatial_w"][0]
    bias_full = jnp.repeat(sp["a_spatial_b"][0].T, LANES, axis=1)
    ffn_s = [(sp["ffn_dw_w"][l], sp["ffn_dw_b"][l:l + 1]) for l in range(2)]

    h0 = _mod_first("l0_mod", x0, nm_g[0:1], sc_m[0], sh_m[0])
    z = _matmul("l0_in", h0, wb.get("w_in", h0), "nn", F32, tm=2048)
    ya = _mixa_fwd("l0_mixa", z, vg, ws, bias_full)
    qkv = _qkv_fwd("l0_qkv", z, cos, sin, ones_bd, qg, kg)
    branches = [_attn_fwd(f"l0_att{dil}", *qkv[b], dil) for b, dil in enumerate(DILATIONS)]
    yb, yb16, lses = _merge_fwd("l0_merge", branches)
    yab = jnp.concatenate([ya, yb16], axis=1)
    y0 = _matmul("l0_out", yab, wb.get("w_out", yab), "nn", F32, tm=1024, tn=1024)
    x1, h1 = _resid_mod("l0_res1", x0, y0, g_m[0], nf_g[0:1], sc_f[0], sh_f[0])
    ffn_w = [(wb.get("up0", h1), wb.get("dn0", h1), *ffn_s[0])]
    f0, saved0 = _ffn_fwd("l0_ffn", h1, *ffn_w[0])
    x2, h2 = _resid_mod("l0_res2", x1, f0, g_f[0], nm_g[1:2], sc_m[1], sh_m[1])
    p = _matmul("l1_pw1", h2, wb.get("pw1", h2), "nn", F32, tm=2048, bias=sp["conv_pw1_b"])
    y2 = _glu31_fwd("l1_glu", p, sp["conv_dw_w"][0], sp["conv_dw_b"])
    y4 = _ln_silu_fwd("l1_ln", y2, sp["conv_ln_g"], sp["conv_ln_b"])
    y1 = _matmul("l1_pw2", y4, wb.get("pw2", y4), "nn", F32, tm=1024, tn=1024, bias=sp["conv_pw2_b"])
    x3, h3 = _resid_mod("l1_res1", x2, y1, g_m[1], nf_g[1:2], sc_f[1], sh_f[1])
    ffn_w.append((wb.get("up1", h3), wb.get("dn1", h3), *ffn_s[1]))
    f1, saved1 = _ffn_fwd("l1_ffn", h3, *ffn_w[1])
    dx4, lossv, dy, dgate_f1, _ = _loss_head("loss", x3, f1, g_f[1], tgt)

    dh, gf1 = _ffn_bwd("l1_ffn", dy, h3, *saved1, *ffn_w[1])
    tok = pipe.scatter("g1", dict(dn1=gf1["dn"], up1=gf1["up"]))
    dx3, dsh_f1, dsc_f1, dnf1, dy, dgate_m1, dpw2_b = _mod_bwd("l1_dmod2", dx4, dh, x3, nf_g[1:2], sc_f[1] + tok,
                                                             y1, g_m[1])
    dy4 = _matmul("l1_dpw2_x", dy, wb.get("pw2"), "nt", F32, tm=1024, tn=1024)
    g_pw2 = _matmul("l1_dpw2_w", y4, dy, "tn", BF16, tm=1024, tn=1024, tk=2048)
    dy2, dln_g, dln_b = _ln_silu_bwd("l1_dln", y2, dy4, sp["conv_ln_g"], sp["conv_ln_b"])
    dp, ddw_w, ddw_b, dpw1_b = _glu31_bwd("l1_dglu", p, dy2, sp["conv_dw_w"][0])
    g_pw1 = _matmul("l1_dpw1_w", h2, dp, "tn", BF16, tm=1024, tn=1024, tk=2048)
    tok = pipe.scatter("g2", dict(pw2=g_pw2, pw1=g_pw1))
    pipe.collect("g1", g_pw1)
    dh = _matmul("l1_dpw1_x", dp, wb.get("pw1"), "nt", F32, tm=1024, tn=1024, tk=2048)
    dx2, dsh_m1, dsc_m1, dnm1, dy, dgate_f0, _ = _mod_bwd("l1_dmod1", dx3, dh, x2, nm_g[1:2], sc_m[1] + tok,
                                                        f0, g_f[0])
    dh, gf0 = _ffn_bwd("l0_ffn", dy, h1, *saved0, *ffn_w[0])
    tok = pipe.scatter("g3", dict(dn0=gf0["dn"], up0=gf0["up"]))
    pipe.collect("g2", gf0["up"])
    dx1, dsh_f0, dsc_f0, dnf0, dy, dgate_m0, _ = _mod_bwd("l0_dmod2", dx2, dh, x1, nf_g[0:1], sc_f[0] + tok,
                                                        y0, g_m[0])
    dyab = _matmul("l0_dout_x", dy, wb.get("w_out"), "nt", F32, tm=1024, tn=1024)
    g_out = _matmul("l0_dout_w", yab, dy, "tn", BF16, tm=1024, tn=1024, tk=2048)
    vg = vg + pipe.scatter("g4", dict(w_out=g_out))
    dza, dws, dbf, dvg = _mixa_bwd("l0_dmixa", z, dyab, vg, ws, bias_full)
    deltas, dos = _delta("l0_delta", dyab, yb, ones_bd)
    pieces = []
    for b, dil in enumerate(DILATIONS):
        do = (dyab, ("row", BW, 1)) if dil == 1 else (dos[b - 1], ROW)
        pieces.append(_attn_bwd(f"l0_datt{dil}", *qkv[b], do, lses[b], deltas[b], dil))
    dzb, dqg, dkg = _qkv_bwd("l0_dqkv", z, cos, sin, ones_bd, qg, kg, pieces)
    small_tok = _start_small_grads(pipe, dzb, sp, lossv, ((dgate_m0, dsh_f0, dsc_f0, dgate_f0),
                                                   (dsh_m1, dsc_m1, dgate_m1, dsh_f1, dsc_f1, dgate_f1)),
                                   (dnm1, dnf0, dnf1), dvg, dws, dbf, dqg, dkg, dpw1_b, ddw_w, ddw_b, dln_g, dln_b,
                                   dpw2_b, gf0, gf1)
    dz = jnp.concatenate([dza, dzb], axis=1)
    g_in = _matmul("l0_din_w", h0, dz, "tn", BF16, tm=1024, tn=1280, tk=2048, after=small_tok)
    tok = pipe.scatter("g5", dict(w_in=g_in))
    dh = _matmul("l0_din_x", dz, wb.get("w_in"), "nt", F32, tm=1024, tn=1024, tk=2560, after=small_tok)
    grad_x, dsh_m0, dsc_m0, dnm0 = _mod_bwd("l0_dmod1", dx1, dh, x0, nm_g[0:1], sc_m[0] + tok)
    return lossv, grad_x, (dsh_m0, dsc_m0, dnm0)


def _start_small_grads(pipe, after, sp, lossv, mods, norms, dvg, dws, dbf, dqg, dkg, dpw1_b, ddw_w, ddw_b, dln_g,
                       dln_b, dpw2_b, gf0, gf1):
    (dgate_m0, dsh_f0, dsc_f0, dgate_f0), (dsh_m1, dsc_m1, dgate_m1, dsh_f1, dsc_f1, dgate_f1) = mods
    dnm1, dnf0, dnf1 = norms
    zero = jnp.zeros_like(dnm1)
    dmod = jnp.stack([jnp.concatenate([zero, zero, dgate_m0, dsh_f0, dsc_f0, dgate_f0], axis=0),
                      jnp.concatenate([dsh_m1, dsc_m1, dgate_m1, dsh_f1, dsc_f1, dgate_f1], axis=0)])
    small = dict(
        norm_mix_g=jnp.concatenate([zero, dnm1], axis=0),
        norm_ffn_g=jnp.concatenate([dnf0, dnf1], axis=0),
        a_vnorm_g=dvg.reshape(1, A_GROUPS, LANES),
        a_spatial_w=dws[None],
        a_spatial_b=dbf.reshape(CHUNK, A_GROUPS, LANES).sum(-1).T[None],
        b_q_norm_g=dqg.reshape(N_HEADS, HEAD).sum(0)[None],
        b_k_norm_g=dkg.reshape(N_HEADS, HEAD).sum(0)[None],
        conv_pw1_b=dpw1_b, conv_dw_w=ddw_w[None, :sp["conv_dw_w"].shape[1]], conv_dw_b=ddw_b,
        conv_ln_g=dln_g, conv_ln_b=dln_b, conv_pw2_b=dpw2_b,
        ffn_dw_w=jnp.stack([gf0["dw_w"], gf1["dw_w"]]),
        ffn_dw_b=jnp.concatenate([gf0["dw_b"], gf1["dw_b"]], axis=0),
    )
    return pipe.start_small(dmod, lossv, small, after)


ADA_TN = 512


def _ada_fwd(name, c_all, ada_w, ada_b_sh):
    L, D, N = ada_w.shape
    B = c_all.shape[0]

    def body(c_ref, w_ref, b_ref, o_ref):
        cv = c_ref[...]
        ca = (cv * _sigmoid(cv)).astype(BF16)
        o_ref[0] = jnp.dot(ca, w_ref[0].astype(BF16), preferred_element_type=F32) + b_ref[0]

    return pl.pallas_call(
        body, name=name, grid=(L, N // ADA_TN),
        in_specs=[pl.BlockSpec((B, D), lambda l, j: (0, 0)), pl.BlockSpec((1, D, ADA_TN), lambda l, j: (l, 0, j)),
                  pl.BlockSpec((1, 1, ADA_TN), lambda l, j: (l, 0, j))],
        out_specs=pl.BlockSpec((1, B, ADA_TN), lambda l, j: (l, 0, j)),
        out_shape=jax.ShapeDtypeStruct((L, B, N), F32),
        compiler_params=_params(("parallel", "parallel")),
    )(c_all, ada_w, ada_b_sh.reshape(L, 1, N))


def _adamw_val(w, g, m, v):
    m2 = ADAM_B1 * m + (1.0 - ADAM_B1) * g
    v2 = ADAM_B2 * v + (1.0 - ADAM_B2) * (g * g)
    m_hat = m2 / (1.0 - ADAM_B1 ** ADAM_STEP)
    v_hat = v2 / (1.0 - ADAM_B2 ** ADAM_STEP)
    delta = -ADAM_LR * (m_hat / (jnp.sqrt(v_hat) + ADAM_EPS) + ADAM_WD * w)
    return delta, m2, v2


def _ada_update(name, c_all, dmod_sh, w, m, v):
    L, D, N = w.shape
    B = c_all.shape[0]

    def body(c_ref, d_ref, w_ref, m_ref, v_ref, g_ref, dl_ref, mo_ref, vo_ref):
        cv = c_ref[...]
        ca = (cv * _sigmoid(cv)).astype(BF16)
        g = lax.dot_general(ca, d_ref[0].astype(BF16), _DN["tn"], preferred_element_type=F32)
        g_ref[0] = g
        dl_ref[0], mo_ref[0], vo_ref[0] = _adamw_val(w_ref[0], g, m_ref[0], v_ref[0])

    wspec = pl.BlockSpec((1, D, ADA_TN), lambda l, j: (l, 0, j))
    return pl.pallas_call(
        body, name=name, grid=(L, N // ADA_TN),
        in_specs=[pl.BlockSpec((B, D), lambda l, j: (0, 0)), pl.BlockSpec((1, B, ADA_TN), lambda l, j: (l, 0, j)),
                  wspec, wspec, wspec],
        out_specs=[wspec] * 4, out_shape=[jax.ShapeDtypeStruct((L, D, N), F32)] * 4,
        compiler_params=_params(("parallel", "parallel")),
    )(c_all, dmod_sh, w, m, v)


def _adamw(name, w, g, m, v):
    R, C = w.shape
    tm = R
    for cand in (256, 128, 64, 32, 16, 8):
        if R % cand == 0 and cand * C * 4 <= (1 << 20):
            tm = cand
            break

    def body(w_ref, g_ref, m_ref, v_ref, d_ref, mo_ref, vo_ref):
        d_ref[...], mo_ref[...], vo_ref[...] = _adamw_val(w_ref[...], g_ref[...], m_ref[...], v_ref[...])

    return _rowcall(name, body, R, tm, [(w, ROW), (g, ROW), (m, ROW), (v, ROW)], [((R, C), F32, ROW)] * 3)


def _row_tile(rows, width, itemsize=4, limit=1 << 20):
    for cand in range(512, 0, -16):
        if rows % cand == 0 and cand * width * itemsize <= limit:
            return cand
    raise ValueError((rows, width))


def _cast_into_full(name, a, layer, q, kind, after):
    L, r, c = a.shape
    tm = _row_tile(r, c, limit=2 << 20)
    if kind == "col":
        full, o_spec = (r, N_CHIPS * c), pl.BlockSpec((tm, c), lambda i, q_ref: (i, q_ref[0]))
    else:
        full, o_spec = (N_CHIPS * r, c), pl.BlockSpec((tm, c), lambda i, q_ref: (q_ref[0] * (r // tm) + i, 0))

    def body(q_ref, a_ref, after_ref, o_ref):
        o_ref[...] = a_ref[0].astype(BF16)

    return pl.pallas_call(
        body, name=name,
        grid_spec=pltpu.PrefetchScalarGridSpec(
            num_scalar_prefetch=1, grid=(r // tm,),
            in_specs=[pl.BlockSpec((1, tm, c), lambda i, q_ref: (layer, i, 0)), ANY], out_specs=o_spec),
        out_shape=jax.ShapeDtypeStruct(full, BF16), compiler_params=_params(("parallel",)),
    )(q.reshape(1).astype(jnp.int32), a, after)


def _sum4(name, g, rcv, q, kind, n):
    r, c = rcv.shape[1:]
    tm = _row_tile(r, c, limit=2 << 20)
    if kind == "col":
        g_spec = pl.BlockSpec((tm, n), lambda i, q_ref: (i, q_ref[0]))
    else:
        g_spec = pl.BlockSpec((tm, c), lambda i, q_ref: (q_ref[0] * (n // tm) + i, 0))

    def body(q_ref, g_ref, r_ref, o_ref):
        acc = g_ref[...].astype(F32)
        for j in range(3):
            acc = acc + r_ref[j].astype(F32)
        o_ref[...] = acc

    return pl.pallas_call(
        body, name=name,
        grid_spec=pltpu.PrefetchScalarGridSpec(
            num_scalar_prefetch=1, grid=(r // tm,),
            in_specs=[g_spec, pl.BlockSpec((3, tm, c), lambda i, q_ref: (0, i, 0))],
            out_specs=pl.BlockSpec((tm, c), lambda i, q_ref: (i, 0))),
        out_shape=jax.ShapeDtypeStruct((r, c), F32), compiler_params=_params(("parallel",)),
    )(q.reshape(1).astype(jnp.int32), g, rcv)


def _adamw_sum(name, w, m, v, layer, mine, theirs, prev):
    L, r, c = w.shape
    tm = _row_tile(r, c, limit=3 << 19)
    lay = pl.BlockSpec((1, tm, c), lambda i: (layer, i, 0))
    flat = pl.BlockSpec((tm, c), lambda i: (i, 0))
    n_prev = 0 if prev is None else 4

    def body(w_ref, m_ref, v_ref, a_ref, b_ref, *rest):
        g_ref, d_ref, mo_ref, vo_ref = rest[n_prev:]
        g = a_ref[...] + b_ref[...]
        g_ref[0] = g
        d_ref[0], mo_ref[0], vo_ref[0] = _adamw_val(w_ref[0], g, m_ref[0], v_ref[0])

    return pl.pallas_call(
        body, name=name, grid=(r // tm,),
        in_specs=[lay, lay, lay, flat, flat] + [ANY] * n_prev, out_specs=[lay] * 4,
        out_shape=[jax.ShapeDtypeStruct((L, r, c), F32)] * 4,
        input_output_aliases={5 + k: k for k in range(n_prev)},
        compiler_params=_params(("parallel",)),
    )(w, m, v, mine, theirs, *(prev or ()))


def _sum8(name, gathered, own=None):
    R, N = gathered.shape
    P = R // 8

    def body(g_ref, *rest):
        o_ref = rest[-1]
        me = 4 * lax.axis_index("x") + 2 * lax.axis_index("y") + lax.axis_index("c")
        acc = None
        for d in range(8):
            blk = g_ref[d * P:(d + 1) * P, :]
            if own is not None:
                blk = jnp.where(me == d, rest[0][...], blk)
            acc = blk if d == 0 else acc + blk
        o_ref[...] = acc

    return pl.pallas_call(body, name=name, out_shape=jax.ShapeDtypeStruct((P, N), F32),
                          compiler_params=pltpu.CompilerParams(vmem_limit_bytes=VMEM_LIMIT),
                          )(gathered, *(() if own is None else (own,)))


ANY = pl.BlockSpec(memory_space=pl.ANY)


def _mesh_pos():
    x, y, c = lax.axis_index("x"), lax.axis_index("y"), lax.axis_index("c")
    other_chips = [(1 - x, y), (x, 1 - y), (1 - x, 1 - y)]
    return x, y, c, other_chips


def _allgather8(name, blk, after=None):
    m_per, n = blk.shape

    def body(x_ref, *rest):
        out_ref, send_sems, recv_sems, local_sem = rest[after is not None:]
        x, y, c, chips = _mesh_pos()
        me, sibling = (x, y, c), (x, y, 1 - c)

        def rows(px, py, pc):
            return out_ref.at[pl.ds((4 * px + 2 * py + pc) * m_per, m_per), :]

        def copy(k, block, to, src=None):
            return pltpu.make_async_remote_copy(
                src_ref=rows(*block) if src is None else src, dst_ref=rows(*block),
                send_sem=send_sems.at[k], recv_sem=recv_sems.at[k], device_id=to, device_id_type=MESH)

        mine = pltpu.make_async_copy(x_ref, rows(*me), local_sem)
        mine.start()
        first = [copy(0, me, sibling, src=x_ref)]
        first += [copy(1 + j, me, (*chip, c), src=x_ref) for j, chip in enumerate(chips)]
        for cp in first:
            cp.start()
        passed = [copy(4 + j, (*chip, c), sibling) for j, chip in enumerate(chips)]
        for j, chip in enumerate(chips):
            copy(1 + j, (*chip, c), me).wait_recv()
            passed[j].start()
        copy(0, sibling, me).wait_recv()
        for j, chip in enumerate(chips):
            copy(4 + j, (*chip, 1 - c), me).wait_recv()
        for cp in first + passed:
            cp.wait_send()
        mine.wait()

    return pl.pallas_call(
        body, name=name, out_shape=jax.ShapeDtypeStruct((8 * m_per, n), blk.dtype),
        in_specs=[pl.BlockSpec(memory_space=pltpu.VMEM)] + [ANY] * (after is not None),
        out_specs=pl.BlockSpec(memory_space=pltpu.VMEM),
        scratch_shapes=[pltpu.SemaphoreType.DMA((7,)), pltpu.SemaphoreType.DMA((7,)), pltpu.SemaphoreType.DMA],
        compiler_params=pltpu.CompilerParams(vmem_limit_bytes=VMEM_LIMIT),
    )(blk, *(() if after is None else (after,)))


BIG = dict(w_in=("col", "ab_w_in", 0), w_out=("row", "ab_w_out", 0), up0=("col", "ffn_up_w", 0),
           dn0=("row", "ffn_down_w", 0), pw1=("col", "conv_pw1_w", 0), pw2=("row", "conv_pw2_w", 0),
           up1=("col", "ffn_up_w", 1), dn1=("row", "ffn_down_w", 1))
N_CHIPS = 4
HBM = pl.BlockSpec(memory_space=pltpu.HBM)
SEM = pl.BlockSpec(memory_space=pltpu.SEMAPHORE)
EFFECT = pltpu.SideEffectType.DATAFLOW_SIDE_EFFECTING


def _region(kind, ref, q, n):
    if kind == "col":
        return ref.at[:, pl.ds(q * n, n)]
    return ref.at[pl.ds(q * n, n), :]


def _gather_plan(kind, n):
    def remote(src, land, pos):
        x, y, c, chips = pos
        mine = _region(kind, land, 2 * x + y, n)
        return [(mine, mine, (*chip, c)) for chip in chips]

    return ("gather", kind, n), remote


def _scatter_plan(kind, n):
    def remote(src, land, pos):
        _, _, c, chips = pos
        return [(_region(kind, src, 2 * chip[0] + chip[1], n), land.at[j], (*chip, c)) for j, chip in enumerate(chips)]

    return ("scatter", kind, n), remote


def _everyone_plan(rows):
    def remote(src, land, pos):
        x, y, c, _ = pos
        mine = land.at[pl.ds((4 * x + 2 * y + c) * rows, rows), :]
        flip = lambda v, bit: 1 - v if bit else v
        return [(src, mine, (flip(x, k & 4), flip(y, k & 2), flip(c, k & 1))) for k in range(1, 8)]

    return ("everyone", rows), remote


def _sibling_plan():
    def remote(src, land, pos):
        x, y, c, _ = pos
        return [(src, land, (x, y, 1 - c))]

    return ("sibling",), remote


def _split_start(name, items, after=None):
    n = len(items)
    plans = [it[2] for it in items]
    n_in = 2 * n + (after is not None)

    def body(*refs):
        srcs, lands = refs[:n], refs[n:2 * n]
        sends, recvs = refs[n_in:n_in + n], refs[n_in + n:n_in + 2 * n]
        token = refs[n_in + 4 * n]
        pos = _mesh_pos()
        for a, (_, remote) in enumerate(plans):
            for k, (s, d, dev) in enumerate(remote(srcs[a], lands[a], pos)):
                pltpu.make_async_remote_copy(src_ref=s, dst_ref=d, send_sem=sends[a].at[k], recv_sem=recvs[a].at[k],
                                             device_id=dev, device_id_type=MESH).start()
        token[...] = jnp.zeros_like(token)

    sems = [pltpu.SemaphoreType.DMA((it[3],)) for it in items]
    bufs = [pltpu.HBM(it[k].shape, it[k].dtype) for k in (0, 1) for it in items]
    outs = pl.pallas_call(
        body, name=name, out_shape=[*sems, *sems, *bufs, jax.ShapeDtypeStruct((8, LANES), F32)],
        in_specs=[HBM] * (2 * n) + [ANY] * (n_in - 2 * n),
        out_specs=[SEM] * (2 * n) + [HBM] * (2 * n) + [pl.BlockSpec(memory_space=pltpu.VMEM)],
        input_output_aliases={i: 2 * n + i for i in range(2 * n)},
        compiler_params=pltpu.CompilerParams(has_side_effects=EFFECT),
    )(*[pltpu.with_memory_space_constraint(it[k], pltpu.HBM) for k in (0, 1) for it in items],
      *(() if after is None else (after,)))
    state = [(items[a][2], items[a][3], outs[2 * n + a], outs[3 * n + a], outs[a], outs[n + a]) for a in range(n)]
    return state, outs[4 * n]


def _split_wait(name, state, after):
    n = len(state)

    def body(*refs):
        srcs, lands = refs[:n], refs[n:2 * n]
        sends, recvs = refs[2 * n:3 * n], refs[3 * n:4 * n]
        pos = _mesh_pos()
        for a, ((_, remote), *_) in enumerate(state):
            for k, (s, d, dev) in enumerate(remote(srcs[a], lands[a], pos)):
                cp = pltpu.make_async_remote_copy(src_ref=s, dst_ref=d, send_sem=sends[a].at[k], recv_sem=recvs[a].at[k],
                                                  device_id=dev, device_id_type=MESH)
                cp.wait_send()
                cp.wait_recv()

    bufs = [st[k] for k in (2, 3) for st in state]
    outs = pl.pallas_call(
        body, name=name, out_shape=[pltpu.HBM(b.shape, b.dtype) for b in bufs],
        in_specs=[HBM] * (2 * n) + [SEM] * (2 * n) + [ANY], out_specs=[HBM] * (2 * n),
        input_output_aliases={i: i for i in range(2 * n)},
        compiler_params=pltpu.CompilerParams(has_side_effects=EFFECT),
    )(*bufs, *[st[k] for k in (4, 5) for st in state], after)
    return outs[:n], outs[n:]


class _Weights:
    def __init__(self, w, q, after):
        unused = jnp.zeros((16, LANES), BF16)

        def item(name, after):
            kind, pname, layer = BIG[name]
            _, r, c = w[pname].shape
            land = _cast_into_full(f"cast_{name}", w[pname], layer, q, kind, after)
            return unused, land, _gather_plan(kind, c if kind == "col" else r), N_CHIPS - 1

        first, *rest = BIG
        state1, token1 = _split_start("gw_start_first", [item(first, after)], after)
        state2, self.token = _split_start("gw_start_rest", [item(name, token1) for name in rest], token1)
        self.pending = dict(zip(BIG, state1 + state2))
        self.ready = {}

    def get(self, name, after=None):
        if name not in self.ready:
            self.ready[name] = _split_wait(f"gw_wait_{name}", [self.pending.pop(name)], after)[1][0]
        return self.ready[name]


class _GradPipe:
    def __init__(self, q, w, m, v):
        self.q, self.w, self.m, self.v = q, w, m, v
        self.stage, self.results = {}, {}

    def start_small(self, dmod, lossv, small, after):
        self.small_names = [n for n in REPLICATED if n != "ada_b"] + list(SMALL_SHARDED)
        payload = [dmod.reshape(2, -1), lossv] + [small[n] for n in self.small_names]
        self.small_shapes = [p.shape for p in payload]
        packed = _pack(payload)
        land = jnp.zeros((8 * PACK_ROWS, packed.shape[1]), F32)
        self.small_state, token = _split_start("ag_grads_start", [(packed, land, _everyone_plan(PACK_ROWS), 7)], after)
        return token

    def wait_small(self, after):
        srcs, lands = _split_wait("ag_grads_wait", self.small_state, after)
        return srcs[0], lands[0]

    def scatter(self, group, grads, after=None):
        items = []
        for name, g in grads.items():
            kind = BIG[name][0]
            rows, cols = g.shape
            n = (cols if kind == "col" else rows) // N_CHIPS
            reg = (rows, n) if kind == "col" else (n, cols)
            items.append((g, lax.empty((N_CHIPS - 1, *reg), BF16), _scatter_plan(kind, n), N_CHIPS - 1))
        state, token = _split_start(f"gs_start_{group}", items, after)
        self.stage[group] = (list(grads), state)
        return token[0, 0]

    def collect(self, group, after):
        names, state = self.stage[group]
        srcs, lands = _split_wait(f"gs_wait_{group}", state, after)
        items = []
        for name, st, g, land in zip(names, state, srcs, lands):
            _, kind, n = st[0][0]
            part = _sum4(f"sum_{name}", g, land, self.q, kind, n)
            items.append((part, lax.empty(part.shape, F32), _sibling_plan(), 1))
        state, token = _split_start(f"sw_start_{group}", items)
        self.stage[group] = (names, state)
        return token

    def finish(self, group, after):
        names, state = self.stage.pop(group)
        srcs, lands = _split_wait(f"sw_wait_{group}", state, after)
        for name, mine, theirs in zip(names, srcs, lands):
            _, pname, layer = BIG[name]
            self.results[pname] = _adamw_sum(f"adamw_{name}", self.w[pname], self.m[pname], self.v[pname], layer,
                                             mine, theirs, self.results.get(pname))


PACK_ROWS = 8


def _pack(arrays):
    flat = jnp.concatenate([a.reshape(-1) for a in arrays])
    n = flat.shape[0]
    padded = -(-n // (PACK_ROWS * LANES)) * (PACK_ROWS * LANES)
    return jnp.pad(flat, (0, padded - n)).reshape(PACK_ROWS, padded // PACK_ROWS)


def _unpack(packed, shapes):
    flat = packed.reshape(-1)
    out, off = [], 0
    for s in shapes:
        n = 1
        for d in s:
            n *= d
        out.append(flat[off:off + n].reshape(s))
        off += n
    return out


REPLICATED = ("ada_b", "norm_mix_g", "norm_ffn_g", "a_vnorm_g", "a_spatial_w", "a_spatial_b", "b_q_norm_g",
              "b_k_norm_g", "ffn_dw_b")
SMALL_SHARDED = ("conv_pw1_b", "conv_dw_w", "conv_dw_b", "conv_ln_g", "conv_ln_b", "conv_pw2_b", "ffn_dw_w")
WEIGHTS = ("ada_w", "ada_b", "norm_mix_g", "norm_ffn_g", "ab_w_in", "a_vnorm_g", "a_spatial_w", "a_spatial_b",
           "b_q_norm_g", "b_k_norm_g", "ab_w_out", "conv_pw1_w", "conv_pw1_b", "conv_dw_w", "conv_dw_b", "conv_ln_g",
           "conv_ln_b", "conv_pw2_w", "conv_pw2_b", "ffn_up_w", "ffn_dw_w", "ffn_dw_b", "ffn_down_w")

def kernel(x, c, positions, ada_w, ada_b, norm_mix_g, norm_ffn_g, ab_w_in, a_vnorm_g, a_spatial_w, a_spatial_b, b_q_norm_g, b_k_norm_g, ab_w_out, conv_pw1_w, conv_pw1_b, conv_dw_w, conv_dw_b, conv_ln_g, conv_ln_b, conv_pw2_w, conv_pw2_b, ffn_up_w, ffn_dw_w, ffn_dw_b, ffn_down_w, loss_target, m_ada_w, m_ada_b, m_norm_mix_g, m_norm_ffn_g, m_ab_w_in, m_a_vnorm_g, m_a_spatial_w, m_a_spatial_b, m_b_q_norm_g, m_b_k_norm_g, m_ab_w_out, m_conv_pw1_w, m_conv_pw1_b, m_conv_dw_w, m_conv_dw_b, m_conv_ln_g, m_conv_ln_b, m_conv_pw2_w, m_conv_pw2_b, m_ffn_up_w, m_ffn_dw_w, m_ffn_dw_b, m_ffn_down_w, v_ada_w, v_ada_b, v_norm_mix_g, v_norm_ffn_g, v_ab_w_in, v_a_vnorm_g, v_a_spatial_w, v_a_spatial_b, v_b_q_norm_g, v_b_k_norm_g, v_ab_w_out, v_conv_pw1_w, v_conv_pw1_b, v_conv_dw_w, v_conv_dw_b, v_conv_ln_g, v_conv_ln_b, v_conv_pw2_w, v_conv_pw2_b, v_ffn_up_w, v_ffn_dw_w, v_ffn_dw_b, v_ffn_down_w):
    w = dict(ada_w=ada_w, ada_b=ada_b, norm_mix_g=norm_mix_g, norm_ffn_g=norm_ffn_g, ab_w_in=ab_w_in, a_vnorm_g=a_vnorm_g, a_spatial_w=a_spatial_w, a_spatial_b=a_spatial_b, b_q_norm_g=b_q_norm_g, b_k_norm_g=b_k_norm_g, ab_w_out=ab_w_out, conv_pw1_w=conv_pw1_w, conv_pw1_b=conv_pw1_b, conv_dw_w=conv_dw_w, conv_dw_b=conv_dw_b, conv_ln_g=conv_ln_g, conv_ln_b=conv_ln_b, conv_pw2_w=conv_pw2_w, conv_pw2_b=conv_pw2_b, ffn_up_w=ffn_up_w, ffn_dw_w=ffn_dw_w, ffn_dw_b=ffn_dw_b, ffn_down_w=ffn_down_w)
    m = dict(ada_w=m_ada_w, ada_b=m_ada_b, norm_mix_g=m_norm_mix_g, norm_ffn_g=m_norm_ffn_g, ab_w_in=m_ab_w_in, a_vnorm_g=m_a_vnorm_g, a_spatial_w=m_a_spatial_w, a_spatial_b=m_a_spatial_b, b_q_norm_g=m_b_q_norm_g, b_k_norm_g=m_b_k_norm_g, ab_w_out=m_ab_w_out, conv_pw1_w=m_conv_pw1_w, conv_pw1_b=m_conv_pw1_b, conv_dw_w=m_conv_dw_w, conv_dw_b=m_conv_dw_b, conv_ln_g=m_conv_ln_g, conv_ln_b=m_conv_ln_b, conv_pw2_w=m_conv_pw2_w, conv_pw2_b=m_conv_pw2_b, ffn_up_w=m_ffn_up_w, ffn_dw_w=m_ffn_dw_w, ffn_dw_b=m_ffn_dw_b, ffn_down_w=m_ffn_down_w)
    v = dict(ada_w=v_ada_w, ada_b=v_ada_b, norm_mix_g=v_norm_mix_g, norm_ffn_g=v_norm_ffn_g, ab_w_in=v_ab_w_in, a_vnorm_g=v_a_vnorm_g, a_spatial_w=v_a_spatial_w, a_spatial_b=v_a_spatial_b, b_q_norm_g=v_b_q_norm_g, b_k_norm_g=v_b_k_norm_g, ab_w_out=v_ab_w_out, conv_pw1_w=v_conv_pw1_w, conv_pw1_b=v_conv_pw1_b, conv_dw_w=v_conv_dw_w, conv_dw_b=v_conv_dw_b, conv_ln_g=v_conv_ln_g, conv_ln_b=v_conv_ln_b, conv_pw2_w=v_conv_pw2_w, conv_pw2_b=v_conv_pw2_b, ffn_up_w=v_ffn_up_w, ffn_dw_w=v_ffn_dw_w, ffn_dw_b=v_ffn_dw_b, ffn_down_w=v_ffn_down_w)
    S, D = x.shape[1], x.shape[2]
    xi, yi, ci = lax.axis_index("x"), lax.axis_index("y"), lax.axis_index("c")
    q = 2 * xi + yi
    b = 2 * q + ci
    take_dev = lambda g: g.reshape(8, PACK_ROWS, -1)

    c_all = _allgather8("ag_c", c.reshape(PACK_ROWS, D // PACK_ROWS)).reshape(8, D)
    n_ada = ada_w.shape[2]
    mod_sh = _ada_fwd("ada_fwd", c_all, ada_w, lax.dynamic_slice_in_dim(ada_b, q * n_ada, n_ada, axis=1))
    sh_shapes = [mod_sh.shape] + [w[n].shape for n in SMALL_SHARDED]
    gathered_mod = _allgather8("ag_mod", _pack([mod_sh] + [w[n] for n in SMALL_SHARDED]))
    per_chip = [_unpack(blk, sh_shapes) for blk in take_dev(gathered_mod)[0::2]]
    mod_g = jnp.stack([pc[0] for pc in per_chip])
    mod_mine = lax.dynamic_index_in_dim(mod_g, b, axis=2, keepdims=False)
    mod = mod_mine.transpose(1, 0, 2).reshape(2, 6, D)
    sp = {n: jnp.concatenate([pc[1 + i] for pc in per_chip], axis=-1) for i, n in enumerate(SMALL_SHARDED)}
    sp.update({n: w[n] for n in REPLICATED if n != "ada_b"})

    wb = _Weights(w, q, gathered_mod)
    mod = mod + wb.token[0, 0]

    pipe = _GradPipe(q, w, m, v)
    lossv, grad_x, late = _local_step(x[0], loss_target[0], positions[0], mod, wb, sp, pipe)

    pipe.finish("g1", grad_x)
    pipe.finish("g2", pipe.results["ffn_up_w"][0])
    swapped = pipe.collect("g3", pipe.results["conv_pw1_w"][0])
    swapped = pipe.collect("g4", swapped)

    own, gathered = pipe.wait_small(swapped)
    totals = _unpack(_sum8("sum_grads", gathered, own), pipe.small_shapes)
    grads = dict(zip(["ada_b", "loss_columns"] + pipe.small_names, totals))
    loss = 0.5 * jnp.sum(grads.pop("loss_columns")) / D
    late_g = _allgather8("ag_late", _pack(list(late)), gathered)
    late_tot = _unpack(_sum8("sum_late", late_g), [(3, D)])[0]
    grads["ada_b"] = grads["ada_b"].at[0, :2 * D].add(late_tot[:2].reshape(-1))
    grads["norm_mix_g"] = grads["norm_mix_g"].at[0].add(late_tot[2])
    for n in SMALL_SHARDED:
        n_sh = w[n].shape[-1]
        grads[n] = lax.dynamic_slice_in_dim(grads[n], q * n_sh, n_sh, axis=grads[n].ndim - 1)
    dmod_of = lambda packed: packed.reshape(packed.shape[0] // PACK_ROWS, -1)[:, :2 * 6 * D].reshape(-1, 2, 6 * D)
    dmod_all = jnp.where((jnp.arange(8) == b)[:, None, None], dmod_of(own), dmod_of(gathered))
    late_all = take_dev(late_g).reshape(8, -1)[:, :3 * D].reshape(8, 3, D)
    dmod_all = dmod_all.at[:, 0, :2 * D].add(late_all[:, :2].reshape(8, 2 * D))
    dmod_sh = lax.dynamic_slice_in_dim(dmod_all, q * n_ada, n_ada, axis=2).transpose(1, 0, 2)

    pipe.finish("g3", dmod_sh)
    pipe.finish("g4", pipe.results["ffn_up_w"][0])
    grads["ada_w"], delta_ada, m_ada, v_ada = _ada_update("ada_update", c_all, dmod_sh, ada_w, m_ada_w, v_ada_w)
    delta, new_m, new_v = dict(ada_w=delta_ada), dict(ada_w=m_ada), dict(ada_w=v_ada)
    rest = list(REPLICATED) + list(SMALL_SHARDED)
    rest_shapes = [w[n].shape for n in rest]
    outs = _adamw("adamw_small", *[_pack([src[n].reshape(w[n].shape) for n in rest]) for src in (w, grads, m, v)])
    for tgt, packed in zip((delta, new_m, new_v), outs):
        tgt.update(dict(zip(rest, _unpack(packed, rest_shapes))))
    for n in rest:
        grads[n] = grads[n].reshape(w[n].shape)
    pipe.finish("g5", pipe.collect("g5", outs[0]))
    for n, res in pipe.results.items():
        grads[n], delta[n], new_m[n], new_v[n] = res

    return (loss, grad_x[None], *[grads[n] for n in WEIGHTS], *[delta[n] for n in WEIGHTS],
            *[new_m[n] for n in WEIGHTS], *[new_v[n] for n in WEIGHTS])
```

```python
import functools

import jax
import jax.numpy as jnp
from jax import lax
from jax.experimental import pallas as pl
from jax.experimental.pallas import tpu as pltpu

F32, BF16 = jnp.float32, jnp.bfloat16
EPS = 1e-6
NEG = -1e30
ROPE_THETA = 10000.0
LANES = 128
VMEM_LIMIT = 56 * 1024 * 1024
ADAM_LR, ADAM_B1, ADAM_B2, ADAM_EPS, ADAM_WD, ADAM_STEP = 0.001, 0.9, 0.999, 1e-08, 0.01, 10
MESH = pl.DeviceIdType.MESH


def _params(sem):
    return pltpu.CompilerParams(dimension_semantics=sem, vmem_limit_bytes=VMEM_LIMIT)


_DN = {"nn": (((1,), (0,)), ((), ())), "nt": (((1,), (1,)), ((), ())), "tn": (((0,), (0,)), ((), ()))}


def _matmul(name, a, b, mode, out_dtype, tm=512, tn=512, tk=1024, bias=None, after=None):
    if mode == "nn":
        (M, K), N = a.shape, b.shape[1]
    elif mode == "nt":
        (M, K), N = a.shape, b.shape[0]
    else:
        (K, M), N = a.shape, b.shape[1]
    tm, tn, tk = min(tm, M), min(tn, N), min(tk, K)
    assert M % tm == 0 and N % tn == 0 and K % tk == 0, (name, M, N, K, tm, tn, tk)
    nk = K // tk
    if mode == "tn":
        a_spec = pl.BlockSpec((tk, tm), lambda i, j, k: (k, i))
    else:
        a_spec = pl.BlockSpec((tm, tk), lambda i, j, k: (i, k))
    if mode == "nt":
        b_spec = pl.BlockSpec((tn, tk), lambda i, j, k: (j, k))
    else:
        b_spec = pl.BlockSpec((tk, tn), lambda i, j, k: (k, j))
    in_specs, args = [a_spec, b_spec], [a, b]
    if bias is not None:
        in_specs.append(pl.BlockSpec((1, tn), lambda i, j, k: (0, j)))
        args.append(bias)
    if after is not None:
        in_specs.append(pl.BlockSpec(memory_space=pl.ANY))
        args.append(after)
    n_in = len(args)

    def body(*refs):
        a_ref, b_ref, o_ref = refs[0], refs[1], refs[n_in]
        p = lax.dot_general(a_ref[...], b_ref[...], _DN[mode], preferred_element_type=F32)

        def finish(acc):
            if bias is not None:
                acc = acc + refs[2][...]
            o_ref[...] = acc.astype(o_ref.dtype)

        if nk == 1:
            finish(p)
        else:
            acc_ref = refs[n_in + 1]
            k = pl.program_id(2)

            @pl.when(k == 0)
            def _():
                acc_ref[...] = p

            @pl.when(k > 0)
            def _():
                acc_ref[...] += p

            @pl.when(k == nk - 1)
            def _():
                finish(acc_ref[...])

    return pl.pallas_call(
        body, name=name, grid=(M // tm, N // tn, nk), in_specs=in_specs,
        out_specs=pl.BlockSpec((tm, tn), lambda i, j, k: (i, j)),
        out_shape=jax.ShapeDtypeStruct((M, N), out_dtype),
        scratch_shapes=[pltpu.VMEM((tm, tn), F32)] if nk > 1 else [],
        compiler_params=_params(("parallel", "parallel", "arbitrary")),
    )(*args)


def _rowcall(name, body, nrows, tm, ins, outs, scratch=()):
    nblk = nrows // tm
    assert nrows % tm == 0

    def spec(kind, shape):
        k = kind[0]
        if k == "row":
            cw, cb = kind[1] or shape[-1], kind[2]
            return pl.BlockSpec((tm, cw), lambda i: (i, cb))
        if k == "prev":
            hb, cw, cb = kind[1], kind[2] or shape[-1], kind[3]
            r = tm // hb
            return pl.BlockSpec((hb, cw), lambda i: (jnp.maximum(i * r - 1, 0), cb))
        if k == "next":
            hb, cw, cb = kind[1], kind[2] or shape[-1], kind[3]
            r, last = tm // hb, nrows // hb - 1
            return pl.BlockSpec((hb, cw), lambda i: (jnp.minimum((i + 1) * r, last), cb))
        if k == "off":
            off, cw, cb = kind[1], kind[2] or shape[-1], kind[3]
            return pl.BlockSpec((tm, cw), lambda i: (jnp.clip(i + off, 0, nblk - 1), cb))
        if k == "cls":
            dil, off = kind[1], kind[2]
            return pl.BlockSpec((dil, tm // dil, shape[-1]), lambda i: (0, jnp.clip(i + off, 0, nblk - 1), 0))
        nd = len(shape)
        return pl.BlockSpec(tuple(shape), lambda i: (0,) * nd)

    has_acc = any(o[2][0] == "acc" for o in outs)
    return pl.pallas_call(
        body, name=name, grid=(nblk,),
        in_specs=[spec(kind, a.shape) for a, kind in ins],
        out_specs=[spec(kind, shape) for shape, _, kind in outs],
        out_shape=[jax.ShapeDtypeStruct(tuple(shape), dt) for shape, dt, _ in outs],
        scratch_shapes=list(scratch),
        compiler_params=_params(("arbitrary",) if has_acc else ("parallel",)),
    )(*[a for a, _ in ins])


ROW = ("row", None, 0)
FULL = ("full",)
ACC = ("acc",)


def _colsum(x):
    return jnp.sum(x, axis=0, keepdims=True)


def _acc_add(i, ref, val, rows=None):
    idx = (slice(None),) * len(ref.shape) if rows is None else rows

    @pl.when(i == 0)
    def _():
        ref[idx] = val

    @pl.when(i > 0)
    def _():
        ref[idx] = ref[idx] + val


def _sigmoid(x):
    return 1.0 / (1.0 + jnp.exp(-x))


def _gelu(x):
    return 0.5 * x * (1.0 + lax.erf(x * (2.0 ** -0.5)))


def _gelu_grad(x):
    return 0.5 * (1.0 + lax.erf(x * (2.0 ** -0.5))) + x * jnp.exp(-0.5 * x * x) * ((2.0 * jnp.pi) ** -0.5)


SUBLANES = 8


def _phases(ext, sign):
    n = ext.shape[0]
    return [ext if b == 0 else pltpu.roll(ext, b if sign > 0 else n - b, axis=0) for b in range(SUBLANES)]


def _shift_prev(phases, s, hb):
    a, b = divmod(s, SUBLANES)
    return phases[b][hb - SUBLANES * a:phases[b].shape[0] - SUBLANES * a]


def _shift_next(phases, s, tm):
    a, b = divmod(s, SUBLANES)
    return phases[b][SUBLANES * a:SUBLANES * a + tm]


def _rms_mod_val(x, g, sc, sh):
    r = lax.rsqrt(jnp.mean(x * x, axis=-1, keepdims=True) + EPS)
    return x * r * g * (1.0 + sc) + sh


def _mod_first(name, x, g, sc, sh, tm=512):
    S, D = x.shape

    def body(x_ref, g_ref, sc_ref, sh_ref, h_ref):
        h_ref[...] = _rms_mod_val(x_ref[...], g_ref[...], sc_ref[...], sh_ref[...]).astype(BF16)

    return _rowcall(name, body, S, tm, [(x, ROW), (g, FULL), (sc, FULL), (sh, FULL)], [((S, D), BF16, ROW)])[0]


def _resid_mod(name, x, y, gate, g, sc, sh, tm=512):
    S, D = x.shape

    def body(x_ref, y_ref, gate_ref, g_ref, sc_ref, sh_ref, xo_ref, h_ref):
        xn = x_ref[...] + gate_ref[...] * y_ref[...]
        xo_ref[...] = xn
        h_ref[...] = _rms_mod_val(xn, g_ref[...], sc_ref[...], sh_ref[...]).astype(BF16)

    return _rowcall(name, body, S, tm,
                    [(x, ROW), (y, ROW), (gate, FULL), (g, FULL), (sc, FULL), (sh, FULL)],
                    [((S, D), F32, ROW), ((S, D), BF16, ROW)])


def _gate_bwd_val(i, d, y_ref, gate_ref, dy_ref, dg_ref, db_ref):
    dy = d * gate_ref[...]
    dy_ref[...] = dy.astype(BF16)
    _acc_add(i, dg_ref, _colsum(d * y_ref[...]))
    _acc_add(i, db_ref, _colsum(dy))


GATE_OUTS = lambda S, D: [((S, D), BF16, ROW), ((1, D), F32, ACC), ((1, D), F32, ACC)]


def _loss_head(name, x, y, gate, tgt, tm=512):
    S, D = x.shape

    def body(x_ref, y_ref, gate_ref, t_ref, dx_ref, l_ref, dy_ref, dg_ref, db_ref):
        i = pl.program_id(0)
        err = x_ref[...] + gate_ref[...] * y_ref[...] - t_ref[...]
        d = err * (1.0 / D)
        dx_ref[...] = d
        _acc_add(i, l_ref, _colsum(err * err))
        _gate_bwd_val(i, d, y_ref, gate_ref, dy_ref, dg_ref, db_ref)

    return _rowcall(name, body, S, tm, [(x, ROW), (y, ROW), (gate, FULL), (tgt, ROW)],
                    [((S, D), F32, ROW), ((1, D), F32, ACC)] + GATE_OUTS(S, D))


def _mod_bwd(name, dxo, dh, x, g, sc, y=None, gate=None, tm=512):
    S, D = x.shape
    gated = y is not None

    def body(d_ref, dh_ref, x_ref, g_ref, sc_ref, *rest):
        dx_ref, dsh_ref, dsc_ref, dg_ref = rest[2 * gated:2 * gated + 4]
        i = pl.program_id(0)
        xv, dh_v, gv = x_ref[...], dh_ref[...], g_ref[...]
        r = lax.rsqrt(jnp.mean(xv * xv, axis=-1, keepdims=True) + EPS)
        n = xv * r
        _acc_add(i, dsh_ref, _colsum(dh_v))
        _acc_add(i, dsc_ref, _colsum(dh_v * (n * gv)))
        dy = dh_v * (1.0 + sc_ref[...])
        _acc_add(i, dg_ref, _colsum(dy * n))
        dn = dy * gv
        dx = d_ref[...] + r * (dn - n * jnp.mean(dn * n, axis=-1, keepdims=True))
        dx_ref[...] = dx
        if gated:
            _gate_bwd_val(i, dx, rest[0], rest[1], *rest[6:9])

    ins = [(dxo, ROW), (dh, ROW), (x, ROW), (g, FULL), (sc, FULL)] + ([(y, ROW), (gate, FULL)] if gated else [])
    outs = [((S, D), F32, ROW), ((1, D), F32, ACC), ((1, D), F32, ACC), ((1, D), F32, ACC)]
    return _rowcall(name, body, S, tm, ins, outs + (GATE_OUTS(S, D) if gated else []))


HB16 = 16


def _conv3_val(ph, w, b, hb):
    return w[2:3] * _shift_prev(ph, 0, hb) + w[1:2] * _shift_prev(ph, 1, hb) + w[0:1] * _shift_prev(ph, 2, hb) + b


def _halo_first(halo_ref, tile_ref, live):
    return _phases(jnp.concatenate([halo_ref[...].astype(F32) * live, tile_ref[...].astype(F32)], axis=0), 1)


def _glu3_fwd(name, u, w, b, tm=256):
    S, F2 = u.shape
    Fh = F2 // 2

    def body(ua_ref, ub_ref, ha_ref, hb_ref, w_ref, b_ref, o_ref, z_ref):
        live = (pl.program_id(0) > 0).astype(F32)
        wv, bv = w_ref[...], b_ref[...]
        za = _conv3_val(_halo_first(ha_ref, ua_ref, live), wv[:, :Fh], bv[:, :Fh], HB16)
        zb = _conv3_val(_halo_first(hb_ref, ub_ref, live), wv[:, Fh:], bv[:, Fh:], HB16)
        o_ref[...] = (za * _sigmoid(za) * zb).astype(BF16)
        z_ref[:, :Fh] = za.astype(BF16)
        z_ref[:, Fh:] = zb.astype(BF16)

    return _rowcall(name, body, S, tm,
                    [(u, ("row", Fh, 0)), (u, ("row", Fh, 1)), (u, ("prev", HB16, Fh, 0)), (u, ("prev", HB16, Fh, 1)),
                     (w, FULL), (b, FULL)],
                    [((S, Fh), BF16, ROW), ((S, F2), BF16, ROW)])


def _glu3_bwd(name, z, dhm, tm=256):
    S, F2 = z.shape
    Fh = F2 // 2

    def body(za_ref, zb_ref, d_ref, dz_ref, db_ref):
        i = pl.program_id(0)
        za, zb, d = za_ref[...].astype(F32), zb_ref[...].astype(F32), d_ref[...]
        sg = _sigmoid(za)
        da = d * zb * (sg * (1.0 + za * (1.0 - sg)))
        db = d * (za * sg)
        dz_ref[:, :Fh] = da.astype(BF16)
        dz_ref[:, Fh:] = db.astype(BF16)
        _acc_add(i, db_ref, jnp.concatenate([_colsum(da), _colsum(db)], axis=1))

    return _rowcall(name, body, S, tm, [(z, ("row", Fh, 0)), (z, ("row", Fh, 1)), (dhm, ROW)],
                    [((S, F2), BF16, ROW), ((1, F2), F32, ACC)])


def _conv3_bwd(name, dz, u, w, tm=256):
    S, F2 = dz.shape
    nblk = S // tm
    K = w.shape[0]

    def body(d_ref, n_ref, u_ref, w_ref, o_ref, dw_ref):
        i = pl.program_id(0)
        live = (i < nblk - 1).astype(F32)
        ph = _phases(jnp.concatenate([d_ref[...].astype(F32), n_ref[...].astype(F32) * live], axis=0), -1)
        wv, uv = w_ref[...], u_ref[...].astype(F32)
        shifted = [_shift_next(ph, K - 1 - k, tm) for k in range(K)]
        o_ref[...] = functools.reduce(lambda a, t: a + t, [wv[k:k + 1] * shifted[k] for k in range(K)]).astype(BF16)
        for k in range(K):
            _acc_add(i, dw_ref, _colsum(uv * shifted[k]), rows=(slice(k, k + 1), slice(None)))

        @pl.when(i == 0)
        def _():
            dw_ref[K:, :] = jnp.zeros((dw_ref.shape[0] - K, F2), F32)

    return _rowcall(name, body, S, tm, [(dz, ROW), (dz, ("next", HB16, None, 0)), (u, ROW), (w, FULL)],
                    [((S, F2), BF16, ROW), ((SUBLANES, F2), F32, ACC)])


def _ffn_fwd(name, h, w_up, w_dn, dw_w, dw_b):
    u = _matmul(f"{name}_up", h, w_up, "nn", BF16, tm=2048)
    hm, z = _glu3_fwd(f"{name}_glu", u, dw_w, dw_b)
    f = _matmul(f"{name}_dn", hm, w_dn, "nn", F32, tm=1024, tn=1024, tk=w_dn.shape[0])
    return f, (u, hm, z)


def _ffn_bwd(name, dy, h, u, hm, z, w_up, w_dn, dw_w, dw_b):
    Fh = w_dn.shape[0]
    dhm = _matmul(f"{name}_ddn_x", dy, w_dn, "nt", F32, tm=1024, tn=Fh // 2)
    g_dn = _matmul(f"{name}_ddn_w", hm, dy, "tn", BF16, tm=Fh // 2, tn=1024, tk=2048)
    dz, g_dw_b = _glu3_bwd(f"{name}_dglu", z, dhm)
    du, taps = _conv3_bwd(f"{name}_dconv", dz, u, dw_w)
    g_up = _matmul(f"{name}_dup_w", h, du, "tn", BF16, tm=1024, tn=Fh // 2, tk=2048)
    dh = _matmul(f"{name}_dup_x", du, w_up, "nt", F32, tm=1024, tn=1024, tk=Fh)
    return dh, dict(up=g_up, dn=g_dn, dw_w=taps[0:dw_w.shape[0]], dw_b=g_dw_b)


HB32 = 32


def _glu31_fwd(name, p, w, b, tm=256):
    S, D2 = p.shape
    D = D2 // 2
    K = w.shape[0]

    def body(a_ref, g_ref, ha_ref, hg_ref, w_ref, b_ref, o_ref):
        live = (pl.program_id(0) > 0).astype(F32)
        y1 = a_ref[...] * _sigmoid(g_ref[...])
        ph = _phases(jnp.concatenate([ha_ref[...] * _sigmoid(hg_ref[...]) * live, y1], axis=0), 1)
        wv = w_ref[...]
        acc = b_ref[...] + wv[K - 1:K] * y1
        for k in range(K - 1):
            acc = acc + wv[k:k + 1] * _shift_prev(ph, K - 1 - k, HB32)
        o_ref[...] = acc

    return _rowcall(name, body, S, tm,
                    [(p, ("row", D, 0)), (p, ("row", D, 1)), (p, ("prev", HB32, D, 0)), (p, ("prev", HB32, D, 1)),
                     (w, FULL), (b, FULL)],
                    [((S, D), F32, ROW)])[0]


def _ln_silu_fwd(name, y2, g, b, tm=512):
    S, D = y2.shape

    def body(y_ref, g_ref, b_ref, o_ref):
        y = y_ref[...]
        mu = jnp.mean(y, axis=-1, keepdims=True)
        yc = y - mu
        rs = lax.rsqrt(jnp.mean(yc * yc, axis=-1, keepdims=True) + EPS)
        y3 = yc * rs * g_ref[...] + b_ref[...]
        o_ref[...] = (y3 * _sigmoid(y3)).astype(BF16)

    return _rowcall(name, body, S, tm, [(y2, ROW), (g, FULL), (b, FULL)], [((S, D), BF16, ROW)])[0]


def _ln_silu_bwd(name, y2, dy4, g, b, tm=512):
    S, D = y2.shape

    def body(y_ref, d_ref, g_ref, b_ref, o_ref, dg_ref, db_ref):
        i = pl.program_id(0)
        y, gv = y_ref[...], g_ref[...]
        mu = jnp.mean(y, axis=-1, keepdims=True)
        yc = y - mu
        rs = lax.rsqrt(jnp.mean(yc * yc, axis=-1, keepdims=True) + EPS)
        n = yc * rs
        y3 = n * gv + b_ref[...]
        sg = _sigmoid(y3)
        dy3 = d_ref[...] * (sg * (1.0 + y3 * (1.0 - sg)))
        _acc_add(i, db_ref, _colsum(dy3))
        _acc_add(i, dg_ref, _colsum(dy3 * n))
        dn = dy3 * gv
        o_ref[...] = rs * (dn - jnp.mean(dn, axis=-1, keepdims=True) - n * jnp.mean(dn * n, axis=-1, keepdims=True))

    return _rowcall(name, body, S, tm, [(y2, ROW), (dy4, ROW), (g, FULL), (b, FULL)],
                    [((S, D), F32, ROW), ((1, D), F32, ACC), ((1, D), F32, ACC)])


def _glu31_bwd(name, p, dy2, w, tm=256):
    S, D2 = p.shape
    D = D2 // 2
    K = w.shape[0]
    nblk = S // tm

    conv_rows, tap_rows, tap_group, tap_unroll = 16, SUBLANES, 4, 4

    def body(a_ref, g_ref, d_ref, dn_ref, w_ref, dp_ref, dw_ref, dcb_ref, dpb_ref, ph_d, y1_ref, dy1_ref, wb_ref):
        i = pl.program_id(0)
        live_next = (i < nblk - 1).astype(F32)
        a, sg, d = a_ref[...], _sigmoid(g_ref[...]), d_ref[...]
        y1_ref[...] = a * sg
        for b, ph in enumerate(_phases(jnp.concatenate([d, dn_ref[...] * live_next], axis=0), -1)):
            ph_d[b] = ph
        taps = [divmod(K - 1 - k, SUBLANES) for k in range(K)]

        @pl.when(i == 0)
        def _():
            for k in range(K):
                wb_ref[k] = jnp.broadcast_to(w_ref[k:k + 1, :], (SUBLANES, D))

        def conv_rows_at(rb, carry):
            r0 = pl.multiple_of(rb * conv_rows, conv_rows)
            accs = [jnp.zeros((SUBLANES, D), F32) for _ in range(conv_rows // SUBLANES)]
            for k, (rows8, phase) in enumerate(taps):
                wk = wb_ref[k]
                for u in range(len(accs)):
                    accs[u] = accs[u] + wk * ph_d[phase, pl.ds(r0 + SUBLANES * (rows8 + u), SUBLANES), :]
            for u, acc in enumerate(accs):
                dy1_ref[pl.ds(r0 + SUBLANES * u, SUBLANES), :] = acc
            return carry

        lax.fori_loop(0, tm // conv_rows, conv_rows_at, 0)
        for k0 in range(0, K, tap_group):
            group = taps[k0:k0 + tap_group]

            def tap_rows_at(rb, accs, group=group):
                for u in range(tap_unroll):
                    r0 = pl.multiple_of((rb * tap_unroll + u) * tap_rows, tap_rows)
                    yv = y1_ref[pl.ds(r0, tap_rows), :]
                    accs = tuple(acc + yv * ph_d[phase, pl.ds(r0 + SUBLANES * rows8, tap_rows), :]
                                 for acc, (rows8, phase) in zip(accs, group))
                return accs

            accs = lax.fori_loop(0, tm // (tap_rows * tap_unroll), tap_rows_at,
                                 tuple(jnp.zeros((tap_rows, D), F32) for _ in group))
            for j, acc in enumerate(accs):
                _acc_add(i, dw_ref, _colsum(acc), rows=(slice(k0 + j, k0 + j + 1), slice(None)))

        @pl.when(i == 0)
        def _():
            dw_ref[K:, :] = jnp.zeros((dw_ref.shape[0] - K, D), F32)

        _acc_add(i, dcb_ref, _colsum(d))
        dy1 = dy1_ref[...]
        da = dy1 * sg
        dg = dy1 * a * sg * (1.0 - sg)
        dp_ref[:, :D] = da.astype(BF16)
        dp_ref[:, D:] = dg.astype(BF16)
        _acc_add(i, dpb_ref, jnp.concatenate([_colsum(da), _colsum(dg)], axis=1))

    return _rowcall(name, body, S, tm,
                    [(p, ("row", D, 0)), (p, ("row", D, 1)), (dy2, ROW), (dy2, ("next", HB32, None, 0)), (w, FULL)],
                    [((S, D2), BF16, ROW), ((HB32, D), F32, ACC), ((1, D), F32, ACC), ((1, D2), F32, ACC)],
                    scratch=[pltpu.VMEM((SUBLANES, tm + HB32, D), F32), pltpu.VMEM((tm, D), F32),
                             pltpu.VMEM((tm, D), F32), pltpu.VMEM((K, SUBLANES, D), F32)])


CHUNK = 128
A_GROUPS = 4


def _group_ln(gv):
    ns, rss = [], []
    for g in range(A_GROUPS):
        xg = gv[:, g * LANES:(g + 1) * LANES]
        xc = xg - jnp.mean(xg, axis=-1, keepdims=True)
        rs = lax.rsqrt(jnp.mean(xc * xc, axis=-1, keepdims=True) + EPS)
        ns.append(xc * rs)
        rss.append(jnp.broadcast_to(rs, xg.shape))
    return jnp.concatenate(ns, axis=1), jnp.concatenate(rss, axis=1)


def _tril_mask():
    r = lax.broadcasted_iota(jnp.int32, (CHUNK, CHUNK), 0)
    c = lax.broadcasted_iota(jnp.int32, (CHUNK, CHUNK), 1)
    return r >= c


def _spatial(ws_ref, x, dn):
    mask = _tril_mask()
    rows = []
    for ci in range(x.shape[0] // CHUNK):
        cols = []
        for g in range(A_GROUPS):
            wm = jnp.where(mask, ws_ref[g], 0.0).astype(BF16)
            xb = x[ci * CHUNK:(ci + 1) * CHUNK, g * LANES:(g + 1) * LANES]
            cols.append(lax.dot_general(wm, xb, dn, preferred_element_type=F32))
        rows.append(jnp.concatenate(cols, axis=1))
    return jnp.concatenate(rows, axis=0)


def _mixa_fwd(name, z, vg, ws, bias_full, tm=512):
    S = z.shape[0]
    W = A_GROUPS * LANES

    def body(u_ref, v_ref, vg_ref, ws_ref, b_ref, o_ref):
        nh, _ = _group_ln(_gelu(v_ref[...]))
        vn = (nh * vg_ref[...]).astype(BF16)
        f = _spatial(ws_ref, vn, _DN["nn"]) + jnp.concatenate([b_ref[...]] * (tm // CHUNK), axis=0)
        o_ref[...] = (_gelu(u_ref[...]) * f).astype(BF16)

    return _rowcall(name, body, S, tm,
                    [(z, ("row", W, 0)), (z, ("row", W, 1)), (vg, FULL), (ws, FULL), (bias_full, FULL)],
                    [((S, W), BF16, ROW)])[0]


def _mixa_bwd(name, z, dyab, vg, ws, bias_full, tm=512):
    S = z.shape[0]
    W = A_GROUPS * LANES
    nch = tm // CHUNK

    def body(u_ref, v_ref, d_ref, vg_ref, ws_ref, b_ref, dz_ref, dws_ref, dbf_ref, dvg_ref):
        i = pl.program_id(0)
        u, v, d, vgv = u_ref[...], v_ref[...], d_ref[...], vg_ref[...]
        nh, rs = _group_ln(_gelu(v))
        vn = (nh * vgv).astype(BF16)
        f = _spatial(ws_ref, vn, _DN["nn"]) + jnp.concatenate([b_ref[...]] * nch, axis=0)
        dz_ref[:, :W] = (d * f * _gelu_grad(u)).astype(BF16)
        df = d * _gelu(u)
        dbf = df[0:CHUNK]
        for ci in range(1, nch):
            dbf = dbf + df[ci * CHUNK:(ci + 1) * CHUNK]
        _acc_add(i, dbf_ref, dbf)
        dfb = df.astype(BF16)
        mask = _tril_mask()
        for g in range(A_GROUPS):
            acc = jnp.zeros((CHUNK, CHUNK), F32)
            for ci in range(nch):
                blk = (slice(ci * CHUNK, (ci + 1) * CHUNK), slice(g * LANES, (g + 1) * LANES))
                acc = acc + lax.dot_general(dfb[blk], vn[blk], _DN["nt"], preferred_element_type=F32)
            _acc_add(i, dws_ref, jnp.where(mask, acc, 0.0)[None], rows=(slice(g, g + 1), slice(None), slice(None)))
        dvn = _spatial(ws_ref, dfb, _DN["tn"])
        _acc_add(i, dvg_ref, _colsum(dvn * nh))
        dnh = dvn * vgv
        parts = []
        for g in range(A_GROUPS):
            cs = slice(g * LANES, (g + 1) * LANES)
            dg_, ng = dnh[:, cs], nh[:, cs]
            parts.append(dg_ - jnp.mean(dg_, axis=-1, keepdims=True) - ng * jnp.mean(dg_ * ng, axis=-1, keepdims=True))
        dz_ref[:, W:] = (rs * jnp.concatenate(parts, axis=1) * _gelu_grad(v)).astype(BF16)

    return _rowcall(name, body, S, tm,
                    [(z, ("row", W, 0)), (z, ("row", W, 1)), (dyab, ("row", W, 0)), (vg, FULL), (ws, FULL),
                     (bias_full, FULL)],
                    [((S, 2 * W), BF16, ROW), ((A_GROUPS, CHUNK, CHUNK), F32, ACC), ((CHUNK, W), F32, ACC),
                     ((1, W), F32, ACC)])


HEAD = 64
N_HEADS = 8
BW = HEAD * N_HEADS
QB = 128
DILATIONS = (1, 4, 16)
QK_SCALE = HEAD ** -0.5


def _gsum64(x, ones_bd):
    x1 = x.astype(BF16)
    r1 = x - x1.astype(F32)
    x2 = r1.astype(BF16)
    x3 = (r1 - x2.astype(F32)).astype(BF16)
    dot = lambda t: jnp.dot(t, ones_bd, preferred_element_type=F32)
    return dot(x1) + dot(x2) + dot(x3)


def _swap32(x):
    n = x.shape[-1]
    up = pltpu.roll(x, n - HEAD // 2, axis=1)
    dn = pltpu.roll(x, HEAD // 2, axis=1)
    lane = lax.broadcasted_iota(jnp.int32, x.shape, 1)
    return jnp.where((lane % HEAD) < HEAD // 2, up, dn)


def _tile4(t):
    return jnp.concatenate([t] * (BW // LANES), axis=1)


def _stage_spec(tm):
    return pltpu.VMEM((BW // LANES, tm, LANES), F32)


def _to_classes(stage, x, dil):
    tm = x.shape[0]
    for j in range(BW // LANES):
        stage[j] = x[:, j * LANES:(j + 1) * LANES]
    return [jnp.concatenate([stage.at[j][pl.ds(r, tm // dil, stride=dil), :] for j in range(BW // LANES)], axis=1)
            for r in range(dil)]


def _from_classes(stage, cls, dil):
    rows = cls.shape[1]
    for r in range(dil):
        for j in range(BW // LANES):
            stage.at[j][pl.ds(r, rows, stride=dil), :] = cls[r, :, j * LANES:(j + 1) * LANES]
    return jnp.concatenate([stage[j] for j in range(BW // LANES)], axis=1)


def _cls_view(t, dil):
    return t if dil == 1 else t.reshape(dil, t.shape[0] // dil, t.shape[1])


def _cls_kind(dil, off=0):
    return ("off", off, None, 0) if dil == 1 else ("cls", dil, off)


def _cls_out(S, dil, dtype):
    return ((S, BW) if dil == 1 else (dil, S // dil, BW), dtype, _cls_kind(dil))


def _flat(t):
    return t.reshape(-1, t.shape[-1])


def _qkv_fwd(name, z, cos, sin, ones_bd, qg, kg, tm=512):
    S = z.shape[0]
    nd = len(DILATIONS)

    def body(q_ref, k_ref, v_ref, c_ref, s_ref, o_ref, qg_ref, kg_ref, *rest):
        outs, stage = rest[:3 * nd], rest[3 * nd]
        c, s, ob = _tile4(c_ref[...]), _tile4(s_ref[...]), o_ref[...]

        def norm_rope(x, g):
            r = lax.rsqrt(_gsum64(x * x, ob) * (1.0 / HEAD) + EPS)
            xn = x * r * g
            return xn * c + _swap32(xn) * s

        vals = [norm_rope(q_ref[...], qg_ref[...]) * QK_SCALE, norm_rope(k_ref[...], kg_ref[...]), v_ref[...]]
        for a, val in enumerate(vals):
            for b, dil in enumerate(DILATIONS):
                if dil == 1:
                    outs[nd * a + b][...] = val.astype(BF16)
                else:
                    for r, rows in enumerate(_to_classes(stage, val, dil)):
                        outs[nd * a + b][r] = rows.astype(BF16)

    outs = _rowcall(name, body, S, tm,
                    [(z, ("row", BW, 2)), (z, ("row", BW, 3)), (z, ("row", BW, 4)), (cos, ROW), (sin, ROW),
                     (ones_bd, FULL), (qg, FULL), (kg, FULL)],
                    [_cls_out(S, dil, BF16) for _ in range(3) for dil in DILATIONS], scratch=[_stage_spec(tm)])
    return [[_flat(outs[nd * a + b]) for a in range(3)] for b in range(nd)]


PAIR = 2 * HEAD


ATT_BLOCKS = 4
ATT_TM = ATT_BLOCKS * QB
ATT_PREV = ("prev", QB, None, 0)


def _key_rows(prev_ref, cur_ref, sb, ps):
    before = prev_ref[:, ps] if sb == 0 else cur_ref[(sb - 1) * QB:sb * QB, ps]
    return jnp.concatenate([before, cur_ref[sb * QB:(sb + 1) * QB, ps]], axis=0)


def _pair_scores(q_ref, kp_ref, kc_ref, sb, hp, half, seg_blocks):
    ps = slice(hp * PAIR, (hp + 1) * PAIR)
    mine = (lax.broadcasted_iota(jnp.int32, (1, PAIR), 1) >= HEAD) == (half == 1)
    qm = jnp.where(mine, q_ref[sb * QB:(sb + 1) * QB, ps], jnp.zeros((), BF16))
    kcat = _key_rows(kp_ref, kc_ref, sb, ps)
    s = lax.dot_general(qm, kcat, _DN["nt"], preferred_element_type=F32)
    qi = lax.broadcasted_iota(jnp.int32, (QB, 2 * QB), 0)
    kj = lax.broadcasted_iota(jnp.int32, (QB, 2 * QB), 1)
    has_prev = ((pl.program_id(0) * ATT_BLOCKS + sb) % seg_blocks) != 0
    valid = (kj >= qi) & (kj <= qi + QB) & ((kj >= QB) | has_prev)
    return mine, qm, kcat, s, valid


def _attn_fwd(name, q, k, v, dil):
    S = q.shape[0]
    seg_blocks = S // dil // QB

    def body(q_ref, kp_ref, kc_ref, vp_ref, vc_ref, o_ref, l_ref):
        for hp in range(N_HEADS // 2):
            ps = slice(hp * PAIR, (hp + 1) * PAIR)
            chains = [(sb, half) for sb in range(ATT_BLOCKS) for half in range(2)]
            sc = [_pair_scores(q_ref, kp_ref, kc_ref, sb, hp, half, seg_blocks) for sb, half in chains]
            ss = [jnp.where(valid, s, NEG) for _, _, _, s, valid in sc]
            ms = [jnp.max(s, axis=-1, keepdims=True) for s in ss]
            pv = [jnp.exp(s - m) for s, m in zip(ss, ms)]
            dens = [jnp.sum(p, axis=-1, keepdims=True) for p in pv]
            vcats = [_key_rows(vp_ref, vc_ref, sb, ps) for sb in range(ATT_BLOCKS)]
            outs = [jnp.dot(p.astype(BF16), vcats[sb], preferred_element_type=F32) / den
                    for p, den, (sb, _) in zip(pv, dens, chains)]
            lses = [jnp.broadcast_to(m + jnp.log(den), (QB, PAIR)) for m, den in zip(ms, dens)]
            for sb in range(ATT_BLOCKS):
                rows, upper = slice(sb * QB, (sb + 1) * QB), sc[2 * sb + 1][0]
                o_ref[rows, ps] = jnp.where(upper, outs[2 * sb + 1], outs[2 * sb])
                l_ref[rows, ps] = jnp.where(upper, lses[2 * sb + 1], lses[2 * sb])

    return _rowcall(name, body, S, ATT_TM, [(q, ROW), (k, ATT_PREV), (k, ROW), (v, ATT_PREV), (v, ROW)],
                    [((S, BW), F32, ROW)] * 2)


def _attn_bwd(name, q, k, v, do, lse, delta, dil):
    S = q.shape[0]
    seg_blocks = S // dil // QB

    def body(q_ref, kp_ref, kc_ref, vp_ref, vc_ref, do_ref, l_ref, dl_ref, dq_ref, dkc_ref, dkp_ref, dvc_ref, dvp_ref):
        for hp in range(N_HEADS // 2):
            ps = slice(hp * PAIR, (hp + 1) * PAIR)
            chains = [(sb, half) for sb in range(ATT_BLOCKS) for half in range(2)]
            rows = [slice(sb * QB, (sb + 1) * QB) for sb, _ in chains]
            cols = [hp * PAIR + half * HEAD for _, half in chains]
            sc = [_pair_scores(q_ref, kp_ref, kc_ref, sb, hp, half, seg_blocks) for sb, half in chains]
            pv = [jnp.where(valid, jnp.exp(s - l_ref[r, c:c + 1]), 0.0) for (_, _, _, s, valid), r, c in zip(sc, rows, cols)]
            vcats = [_key_rows(vp_ref, vc_ref, sb, ps) for sb in range(ATT_BLOCKS)]
            doms = [jnp.where(mine, do_ref[r, ps].astype(BF16), jnp.zeros((), BF16)) for (mine, *_), r in zip(sc, rows)]
            dps = [lax.dot_general(dom, vcats[sb], _DN["nt"], preferred_element_type=F32) for dom, (sb, _) in zip(doms, chains)]
            dss = [(p * (dp - dl_ref[r, c:c + 1])).astype(BF16) for p, dp, r, c in zip(pv, dps, rows, cols)]
            dqs = [jnp.dot(ds, kcat, preferred_element_type=F32) for ds, (_, _, kcat, _, _) in zip(dss, sc)]
            dks = [lax.dot_general(ds, qm, _DN["tn"], preferred_element_type=F32) for ds, (_, qm, *_) in zip(dss, sc)]
            dvs = [lax.dot_general(p.astype(BF16), dom, _DN["tn"], preferred_element_type=F32) for p, dom in zip(pv, doms)]
            for sb in range(ATT_BLOCKS):
                lo, hi = 2 * sb, 2 * sb + 1
                dk, dv = dks[lo] + dks[hi], dvs[lo] + dvs[hi]
                dq_ref[rows[lo], ps] = jnp.where(sc[hi][0], dqs[hi], dqs[lo])
                dkp_ref[rows[lo], ps] = dk[:QB]
                dkc_ref[rows[lo], ps] = dk[QB:]
                dvp_ref[rows[lo], ps] = dv[:QB]
                dvc_ref[rows[lo], ps] = dv[QB:]

    return _rowcall(name, body, S, ATT_TM,
                    [(q, ROW), (k, ATT_PREV), (k, ROW), (v, ATT_PREV), (v, ROW), do, (lse, ROW), (delta, ROW)],
                    [((S, BW), F32, ROW)] * 5)


def _merge_fwd(name, branches, tm=512):
    S = branches[0][0].shape[0]
    nd = len(DILATIONS)

    def body(*refs):
        ins, (y_ref, yb_ref), l_refs, stage = refs[:2 * nd], refs[2 * nd:2 * nd + 2], refs[2 * nd + 2:3 * nd + 2], refs[-1]
        os_, ls = [], []
        for b, dil in enumerate(DILATIONS):
            o, l = ins[2 * b][...], ins[2 * b + 1][...]
            os_.append(o if dil == 1 else _from_classes(stage, o, dil))
            ls.append(l if dil == 1 else _from_classes(stage, l, dil))
        m = functools.reduce(jnp.maximum, ls)
        es = [jnp.exp(l - m) for l in ls]
        den = functools.reduce(lambda a, e: a + e, es)
        y = functools.reduce(lambda a, t: a + t, [e * o for e, o in zip(es, os_)]) / den
        y_ref[...] = y
        yb_ref[...] = y.astype(BF16)
        lse = m + jnp.log(den)
        for b, dil in enumerate(DILATIONS):
            if dil == 1:
                l_refs[b][...] = lse
            else:
                for r, rows in enumerate(_to_classes(stage, lse, dil)):
                    l_refs[b][r] = rows

    ins = [(_cls_view(t, dil), _cls_kind(dil)) for pair, dil in zip(branches, DILATIONS) for t in pair]
    outs = _rowcall(name, body, S, tm, ins,
                    [((S, BW), F32, ROW), ((S, BW), BF16, ROW)] + [_cls_out(S, dil, F32) for dil in DILATIONS],
                    scratch=[_stage_spec(tm)])
    return outs[0], outs[1], [_flat(t) for t in outs[2:]]


def _delta(name, dyab, yb, ones_bd, tm=512):
    S = yb.shape[0]
    nd = len(DILATIONS)

    def body(d_ref, y_ref, o_ref, *rest):
        dl_refs, do_refs, stage = rest[:nd], rest[nd:2 * nd - 1], rest[-1]
        d = d_ref[...]
        dl = _gsum64(d * y_ref[...], o_ref[...])
        for b, dil in enumerate(DILATIONS):
            if dil == 1:
                dl_refs[b][...] = dl
            else:
                for r, rows in enumerate(_to_classes(stage, dl, dil)):
                    dl_refs[b][r] = rows
                for r, rows in enumerate(_to_classes(stage, d, dil)):
                    do_refs[b - 1][r] = rows.astype(BF16)

    outs = _rowcall(name, body, S, tm, [(dyab, ("row", BW, 1)), (yb, ROW), (ones_bd, FULL)],
                    [_cls_out(S, dil, F32) for dil in DILATIONS] + [_cls_out(S, dil, BF16) for dil in DILATIONS[1:]],
                    scratch=[_stage_spec(tm)])
    return [_flat(t) for t in outs[:nd]], [_flat(t) for t in outs[nd:]]


def _qkv_bwd(name, z, cos, sin, ones_bd, qg, kg, pieces):
    S = z.shape[0]
    nblk = S // QB

    def body(q_ref, k_ref, c_ref, s_ref, o_ref, qg_ref, kg_ref, *rest):
        pr, (dz_ref, dqg_ref, dkg_ref), stage = rest[:15], rest[15:18], rest[18]
        i = pl.program_id(0)
        c, s, ob = _tile4(c_ref[...]), _tile4(s_ref[...]), o_ref[...]
        dq = dk = dv = None
        for b, dil in enumerate(DILATIONS):
            a_q, a_kc, a_kp, a_vc, a_vp = [r[...] for r in pr[5 * b:5 * b + 5]]
            live = ((i + dil) < nblk).astype(F32)
            tq, tk, tv = a_q, a_kc + a_kp * live, a_vc + a_vp * live
            if dil > 1:
                tq, tk, tv = (_from_classes(stage, t, dil) for t in (tq, tk, tv))
            dq, dk, dv = (tq, tk, tv) if b == 0 else (dq + tq, dk + tk, dv + tv)

        def back(x, g, d_rot, acc_ref):
            r = lax.rsqrt(_gsum64(x * x, ob) * (1.0 / HEAD) + EPS)
            n = x * r
            dxn = d_rot * c + _swap32(d_rot * s)
            _acc_add(i, acc_ref, _colsum(dxn * n))
            dn = dxn * g
            return r * (dn - n * (_gsum64(dn * n, ob) * (1.0 / HEAD)))

        dz_ref[:, :BW] = back(q_ref[...], qg_ref[...], dq * QK_SCALE, dqg_ref).astype(BF16)
        dz_ref[:, BW:2 * BW] = back(k_ref[...], kg_ref[...], dk, dkg_ref).astype(BF16)
        dz_ref[:, 2 * BW:] = dv.astype(BF16)

    ins = [(z, ("row", BW, 2)), (z, ("row", BW, 3)), (cos, ROW), (sin, ROW), (ones_bd, FULL), (qg, FULL), (kg, FULL)]
    for piece, dil in zip(pieces, DILATIONS):
        a_q, a_kc, a_kp, a_vc, a_vp = (_cls_view(t, dil) for t in piece)
        own, prev = _cls_kind(dil), _cls_kind(dil, dil)
        ins += [(a_q, own), (a_kc, own), (a_kp, prev), (a_vc, own), (a_vp, prev)]
    return _rowcall(name, body, S, QB, ins,
                    [((S, 3 * BW), BF16, ROW), ((1, BW), F32, ACC), ((1, BW), F32, ACC)], scratch=[_stage_spec(QB)])


def _local_step(x0, tgt, pos, mod, wb, sp, pipe):
    S, D = x0.shape
    md = lambda l, j: mod[l, j:j + 1]
    sh_m, sc_m, g_m, sh_f, sc_f, g_f = ([md(l, j) for l in range(2)] for j in range(6))
    nm_g, nf_g = sp["norm_mix_g"], sp["norm_ffn_g"]

    inv_freq = 1.0 / (ROPE_THETA ** (jnp.arange(0, HEAD, 2, dtype=F32) / HEAD))
    ang = pos.astype(F32)[:, None] * inv_freq
    cs, sn = jnp.cos(ang), jnp.sin(ang)
    cos = jnp.concatenate([cs, cs, cs, cs], axis=1)
    sin = jnp.concatenate([-sn, sn, -sn, sn], axis=1)
    head_of = jnp.arange(BW) // HEAD
    ones_bd = (head_of[:, None] == head_of[None, :]).astype(BF16)
    qg = jnp.tile(sp["b_q_norm_g"].reshape(1, HEAD), (1, N_HEADS))
    kg = jnp.tile(sp["b_k_norm_g"].reshape(1, HEAD), (1, N_HEADS))
    vg = sp["a_vnorm_g"].reshape(1, A_GROUPS * LANES)
    ws = sp["a_spatial_w"][0]
    bias_full = jnp.repeat(sp["a_spatial_b"][0].T, LANES, axis=1)
    ffn_s = [(sp["ffn_dw_w"][l], sp["ffn_dw_b"][l:l + 1]) for l in range(2)]

    h0 = _mod_first("l0_mod", x0, nm_g[0:1], sc_m[0], sh_m[0])
    z = _matmul("l0_in", h0, wb.get("w_in", h0), "nn", F32, tm=2048)
    ya = _mixa_fwd("l0_mixa", z, vg, ws, bias_full)
    qkv = _qkv_fwd("l0_qkv", z, cos, sin, ones_bd, qg, kg)
    branches = [_attn_fwd(f"l0_att{dil}", *qkv[b], dil) for b, dil in enumerate(DILATIONS)]
    yb, yb16, lses = _merge_fwd("l0_merge", branches)
    yab = jnp.concatenate([ya, yb16], axis=1)
    y0 = _matmul("l0_out", yab, wb.get("w_out", yab), "nn", F32, tm=1024, tn=1024)
    x1, h1 = _resid_mod("l0_res1", x0, y0, g_m[0], nf_g[0:1], sc_f[0], sh_f[0])
    ffn_w = [(wb.get("up0", h1), wb.get("dn0", h1), *ffn_s[0])]
    f0, saved0 = _ffn_fwd("l0_ffn", h1, *ffn_w[0])
    x2, h2 = _resid_mod("l0_res2", x1, f0, g_f[0], nm_g[1:2], sc_m[1], sh_m[1])
    p = _matmul("l1_pw1", h2, wb.get("pw1", h2), "nn", F32, tm=2048, bias=sp["conv_pw1_b"])
    y2 = _glu31_fwd("l1_glu", p, sp["conv_dw_w"][0], sp["conv_dw_b"])
    y4 = _ln_silu_fwd("l1_ln", y2, sp["conv_ln_g"], sp["conv_ln_b"])
    y1 = _matmul("l1_pw2", y4, wb.get("pw2", y4), "nn", F32, tm=1024, tn=1024, bias=sp["conv_pw2_b"])
    x3, h3 = _resid_mod("l1_res1", x2, y1, g_m[1], nf_g[1:2], sc_f[1], sh_f[1])
    ffn_w.append((wb.get("up1", h3), wb.get("dn1", h3), *ffn_s[1]))
    f1, saved1 = _ffn_fwd("l1_ffn", h3, *ffn_w[1])
    dx4, lossv, dy, dgate_f1, _ = _loss_head("loss", x3, f1, g_f[1], tgt)

    dh, gf1 = _ffn_bwd("l1_ffn", dy, h3, *saved1, *ffn_w[1])
    tok = pipe.scatter("g1", dict(dn1=gf1["dn"], up1=gf1["up"]))
    dx3, dsh_f1, dsc_f1, dnf1, dy, dgate_m1, dpw2_b = _mod_bwd("l1_dmod2", dx4, dh, x3, nf_g[1:2], sc_f[1] + tok,
                                                             y1, g_m[1])
    dy4 = _matmul("l1_dpw2_x", dy, wb.get("pw2"), "nt", F32, tm=1024, tn=1024)
    g_pw2 = _matmul("l1_dpw2_w", y4, dy, "tn", BF16, tm=1024, tn=1024, tk=2048)
    dy2, dln_g, dln_b = _ln_silu_bwd("l1_dln", y2, dy4, sp["conv_ln_g"], sp["conv_ln_b"])
    dp, ddw_w, ddw_b, dpw1_b = _glu31_bwd("l1_dglu", p, dy2, sp["conv_dw_w"][0])
    g_pw1 = _matmul("l1_dpw1_w", h2, dp, "tn", BF16, tm=1024, tn=1024, tk=2048)
    tok = pipe.scatter("g2", dict(pw2=g_pw2, pw1=g_pw1))
    pipe.collect("g1", g_pw1)
    dh = _matmul("l1_dpw1_x", dp, wb.get("pw1"), "nt", F32, tm=1024, tn=1024, tk=2048)
    dx2, dsh_m1, dsc_m1, dnm1, dy, dgate_f0, _ = _mod_bwd("l1_dmod1", dx3, dh, x2, nm_g[1:2], sc_m[1] + tok,
                                                        f0, g_f[0])
    dh, gf0 = _ffn_bwd("l0_ffn", dy, h1, *saved0, *ffn_w[0])
    tok = pipe.scatter("g3", dict(dn0=gf0["dn"], up0=gf0["up"]))
    pipe.collect("g2", gf0["up"])
    dx1, dsh_f0, dsc_f0, dnf0, dy, dgate_m0, _ = _mod_bwd("l0_dmod2", dx2, dh, x1, nf_g[0:1], sc_f[0] + tok,
                                                        y0, g_m[0])
    dyab = _matmul("l0_dout_x", dy, wb.get("w_out"), "nt", F32, tm=1024, tn=1024)
    g_out = _matmul("l0_dout_w", yab, dy, "tn", BF16, tm=1024, tn=1024, tk=2048)
    vg = vg + pipe.scatter("g4", dict(w_out=g_out))
    dza, dws, dbf, dvg = _mixa_bwd("l0_dmixa", z, dyab, vg, ws, bias_full)
    deltas, dos = _delta("l0_delta", dyab, yb, ones_bd)
    pieces = []
    for b, dil in enumerate(DILATIONS):
        do = (dyab, ("row", BW, 1)) if dil == 1 else (dos[b - 1], ROW)
        pieces.append(_attn_bwd(f"l0_datt{dil}", *qkv[b], do, lses[b], deltas[b], dil))
    dzb, dqg, dkg = _qkv_bwd("l0_dqkv", z, cos, sin, ones_bd, qg, kg, pieces)
    small_tok = _start_small_grads(pipe, dzb, sp, lossv, ((dgate_m0, dsh_f0, dsc_f0, dgate_f0),
                                                   (dsh_m1, dsc_m1, dgate_m1, dsh_f1, dsc_f1, dgate_f1)),
                                   (dnm1, dnf0, dnf1), dvg, dws, dbf, dqg, dkg, dpw1_b, ddw_w, ddw_b, dln_g, dln_b,
                                   dpw2_b, gf0, gf1)
    dz = jnp.concatenate([dza, dzb], axis=1)
    g_in = _matmul("l0_din_w", h0, dz, "tn", BF16, tm=1024, tn=1280, tk=2048, after=small_tok)
    tok = pipe.scatter("g5", dict(w_in=g_in))
    dh = _matmul("l0_din_x", dz, wb.get("w_in"), "nt", F32, tm=1024, tn=1024, tk=2560, after=small_tok)
    grad_x, dsh_m0, dsc_m0, dnm0 = _mod_bwd("l0_dmod1", dx1, dh, x0, nm_g[0:1], sc_m[0] + tok)
    return lossv, grad_x, (dsh_m0, dsc_m0, dnm0)


def _start_small_grads(pipe, after, sp, lossv, mods, norms, dvg, dws, dbf, dqg, dkg, dpw1_b, ddw_w, ddw_b, dln_g,
                       dln_b, dpw2_b, gf0, gf1):
    (dgate_m0, dsh_f0, dsc_f0, dgate_f0), (dsh_m1, dsc_m1, dgate_m1, dsh_f1, dsc_f1, dgate_f1) = mods
    dnm1, dnf0, dnf1 = norms
    zero = jnp.zeros_like(dnm1)
    dmod = jnp.stack([jnp.concatenate([zero, zero, dgate_m0, dsh_f0, dsc_f0, dgate_f0], axis=0),
                      jnp.concatenate([dsh_m1, dsc_m1, dgate_m1, dsh_f1, dsc_f1, dgate_f1], axis=0)])
    small = dict(
        norm_mix_g=jnp.concatenate([zero, dnm1], axis=0),
        norm_ffn_g=jnp.concatenate([dnf0, dnf1], axis=0),
        a_vnorm_g=dvg.reshape(1, A_GROUPS, LANES),
        a_spatial_w=dws[None],
        a_spatial_b=dbf.reshape(CHUNK, A_GROUPS, LANES).sum(-1).T[None],
        b_q_norm_g=dqg.reshape(N_HEADS, HEAD).sum(0)[None],
        b_k_norm_g=dkg.reshape(N_HEADS, HEAD).sum(0)[None],
        conv_pw1_b=dpw1_b, conv_dw_w=ddw_w[None, :sp["conv_dw_w"].shape[1]], conv_dw_b=ddw_b,
        conv_ln_g=dln_g, conv_ln_b=dln_b, conv_pw2_b=dpw2_b,
        ffn_dw_w=jnp.stack([gf0["dw_w"], gf1["dw_w"]]),
        ffn_dw_b=jnp.concatenate([gf0["dw_b"], gf1["dw_b"]], axis=0),
    )
    return pipe.start_small(dmod, lossv, small, after)


ADA_TN = 512


def _ada_fwd(name, c_all, ada_w, ada_b_sh):
    L, D, N = ada_w.shape
    B = c_all.shape[0]

    def body(c_ref, w_ref, b_ref, o_ref):
        cv = c_ref[...]
        ca = (cv * _sigmoid(cv)).astype(BF16)
        o_ref[0] = jnp.dot(ca, w_ref[0].astype(BF16), preferred_element_type=F32) + b_ref[0]

    return pl.pallas_call(
        body, name=name, grid=(L, N // ADA_TN),
        in_specs=[pl.BlockSpec((B, D), lambda l, j: (0, 0)), pl.BlockSpec((1, D, ADA_TN), lambda l, j: (l, 0, j)),
                  pl.BlockSpec((1, 1, ADA_TN), lambda l, j: (l, 0, j))],
        out_specs=pl.BlockSpec((1, B, ADA_TN), lambda l, j: (l, 0, j)),
        out_shape=jax.ShapeDtypeStruct((L, B, N), F32),
        compiler_params=_params(("parallel", "parallel")),
    )(c_all, ada_w, ada_b_sh.reshape(L, 1, N))


def _adamw_val(w, g, m, v):
    m2 = ADAM_B1 * m + (1.0 - ADAM_B1) * g
    v2 = ADAM_B2 * v + (1.0 - ADAM_B2) * (g * g)
    m_hat = m2 / (1.0 - ADAM_B1 ** ADAM_STEP)
    v_hat = v2 / (1.0 - ADAM_B2 ** ADAM_STEP)
    delta = -ADAM_LR * (m_hat / (jnp.sqrt(v_hat) + ADAM_EPS) + ADAM_WD * w)
    return delta, m2, v2


def _ada_update(name, c_all, dmod_sh, w, m, v):
    L, D, N = w.shape
    B = c_all.shape[0]

    def body(c_ref, d_ref, w_ref, m_ref, v_ref, g_ref, dl_ref, mo_ref, vo_ref):
        cv = c_ref[...]
        ca = (cv * _sigmoid(cv)).astype(BF16)
        g = lax.dot_general(ca, d_ref[0].astype(BF16), _DN["tn"], preferred_element_type=F32)
        g_ref[0] = g
        dl_ref[0], mo_ref[0], vo_ref[0] = _adamw_val(w_ref[0], g, m_ref[0], v_ref[0])

    wspec = pl.BlockSpec((1, D, ADA_TN), lambda l, j: (l, 0, j))
    return pl.pallas_call(
        body, name=name, grid=(L, N // ADA_TN),
        in_specs=[pl.BlockSpec((B, D), lambda l, j: (0, 0)), pl.BlockSpec((1, B, ADA_TN), lambda l, j: (l, 0, j)),
                  wspec, wspec, wspec],
        out_specs=[wspec] * 4, out_shape=[jax.ShapeDtypeStruct((L, D, N), F32)] * 4,
        compiler_params=_params(("parallel", "parallel")),
    )(c_all, dmod_sh, w, m, v)


def _adamw(name, w, g, m, v):
    R, C = w.shape
    tm = R
    for cand in (256, 128, 64, 32, 16, 8):
        if R % cand == 0 and cand * C * 4 <= (1 << 20):
            tm = cand
            break

    def body(w_ref, g_ref, m_ref, v_ref, d_ref, mo_ref, vo_ref):
        d_ref[...], mo_ref[...], vo_ref[...] = _adamw_val(w_ref[...], g_ref[...], m_ref[...], v_ref[...])

    return _rowcall(name, body, R, tm, [(w, ROW), (g, ROW), (m, ROW), (v, ROW)], [((R, C), F32, ROW)] * 3)


def _row_tile(rows, width, itemsize=4, limit=1 << 20):
    for cand in range(512, 0, -16):
        if rows % cand == 0 and cand * width * itemsize <= limit:
            return cand
    raise ValueError((rows, width))


def _cast_into_full(name, a, layer, q, kind, after):
    L, r, c = a.shape
    tm = _row_tile(r, c, limit=2 << 20)
    if kind == "col":
        full, o_spec = (r, N_CHIPS * c), pl.BlockSpec((tm, c), lambda i, q_ref: (i, q_ref[0]))
    else:
        full, o_spec = (N_CHIPS * r, c), pl.BlockSpec((tm, c), lambda i, q_ref: (q_ref[0] * (r // tm) + i, 0))

    def body(q_ref, a_ref, after_ref, o_ref):
        o_ref[...] = a_ref[0].astype(BF16)

    return pl.pallas_call(
        body, name=name,
        grid_spec=pltpu.PrefetchScalarGridSpec(
            num_scalar_prefetch=1, grid=(r // tm,),
            in_specs=[pl.BlockSpec((1, tm, c), lambda i, q_ref: (layer, i, 0)), ANY], out_specs=o_spec),
        out_shape=jax.ShapeDtypeStruct(full, BF16), compiler_params=_params(("parallel",)),
    )(q.reshape(1).astype(jnp.int32), a, after)


def _sum4(name, g, rcv, q, kind, n):
    r, c = rcv.shape[1:]
    tm = _row_tile(r, c, limit=2 << 20)
    if kind == "col":
        g_spec = pl.BlockSpec((tm, n), lambda i, q_ref: (i, q_ref[0]))
    else:
        g_spec = pl.BlockSpec((tm, c), lambda i, q_ref: (q_ref[0] * (n // tm) + i, 0))

    def body(q_ref, g_ref, r_ref, o_ref):
        acc = g_ref[...].astype(F32)
        for j in range(3):
            acc = acc + r_ref[j].astype(F32)
        o_ref[...] = acc

    return pl.pallas_call(
        body, name=name,
        grid_spec=pltpu.PrefetchScalarGridSpec(
            num_scalar_prefetch=1, grid=(r // tm,),
            in_specs=[g_spec, pl.BlockSpec((3, tm, c), lambda i, q_ref: (0, i, 0))],
            out_specs=pl.BlockSpec((tm, c), lambda i, q_ref: (i, 0))),
        out_shape=jax.ShapeDtypeStruct((r, c), F32), compiler_params=_params(("parallel",)),
    )(q.reshape(1).astype(jnp.int32), g, rcv)


def _adamw_sum(name, w, m, v, layer, mine, theirs, prev):
    L, r, c = w.shape
    tm = _row_tile(r, c, limit=3 << 19)
    lay = pl.BlockSpec((1, tm, c), lambda i: (layer, i, 0))
    flat = pl.BlockSpec((tm, c), lambda i: (i, 0))
    n_prev = 0 if prev is None else 4

    def body(w_ref, m_ref, v_ref, a_ref, b_ref, *rest):
        g_ref, d_ref, mo_ref, vo_ref = rest[n_prev:]
        g = a_ref[...] + b_ref[...]
        g_ref[0] = g
        d_ref[0], mo_ref[0], vo_ref[0] = _adamw_val(w_ref[0], g, m_ref[0], v_ref[0])

    return pl.pallas_call(
        body, name=name, grid=(r // tm,),
        in_specs=[lay, lay, lay, flat, flat] + [ANY] * n_prev, out_specs=[lay] * 4,
        out_shape=[jax.ShapeDtypeStruct((L, r, c), F32)] * 4,
        input_output_aliases={5 + k: k for k in range(n_prev)},
        compiler_params=_params(("parallel",)),
    )(w, m, v, mine, theirs, *(prev or ()))


def _sum8(name, gathered, own=None):
    R, N = gathered.shape
    P = R // 8

    def body(g_ref, *rest):
        o_ref = rest[-1]
        me = 4 * lax.axis_index("x") + 2 * lax.axis_index("y") + lax.axis_index("c")
        acc = None
        for d in range(8):
            blk = g_ref[d * P:(d + 1) * P, :]
            if own is not None:
                blk = jnp.where(me == d, rest[0][...], blk)
            acc = blk if d == 0 else acc + blk
        o_ref[...] = acc

    return pl.pallas_call(body, name=name, out_shape=jax.ShapeDtypeStruct((P, N), F32),
                          compiler_params=pltpu.CompilerParams(vmem_limit_bytes=VMEM_LIMIT),
                          )(gathered, *(() if own is None else (own,)))


ANY = pl.BlockSpec(memory_space=pl.ANY)


def _mesh_pos():
    x, y, c = lax.axis_index("x"), lax.axis_index("y"), lax.axis_index("c")
    other_chips = [(1 - x, y), (x, 1 - y), (1 - x, 1 - y)]
    return x, y, c, other_chips


def _allgather8(name, blk, after=None):
    m_per, n = blk.shape

    def body(x_ref, *rest):
        out_ref, send_sems, recv_sems, local_sem = rest[after is not None:]
        x, y, c, chips = _mesh_pos()
        me, sibling = (x, y, c), (x, y, 1 - c)

        def rows(px, py, pc):
            return out_ref.at[pl.ds((4 * px + 2 * py + pc) * m_per, m_per), :]

        def copy(k, block, to, src=None):
            return pltpu.make_async_remote_copy(
                src_ref=rows(*block) if src is None else src, dst_ref=rows(*block),
                send_sem=send_sems.at[k], recv_sem=recv_sems.at[k], device_id=to, device_id_type=MESH)

        mine = pltpu.make_async_copy(x_ref, rows(*me), local_sem)
        mine.start()
        first = [copy(0, me, sibling, src=x_ref)]
        first += [copy(1 + j, me, (*chip, c), src=x_ref) for j, chip in enumerate(chips)]
        for cp in first:
            cp.start()
        passed = [copy(4 + j, (*chip, c), sibling) for j, chip in enumerate(chips)]
        for j, chip in enumerate(chips):
            copy(1 + j, (*chip, c), me).wait_recv()
            passed[j].start()
        copy(0, sibling, me).wait_recv()
        for j, chip in enumerate(chips):
            copy(4 + j, (*chip, 1 - c), me).wait_recv()
        for cp in first + passed:
            cp.wait_send()
        mine.wait()

    return pl.pallas_call(
        body, name=name, out_shape=jax.ShapeDtypeStruct((8 * m_per, n), blk.dtype),
        in_specs=[pl.BlockSpec(memory_space=pltpu.VMEM)] + [ANY] * (after is not None),
        out_specs=pl.BlockSpec(memory_space=pltpu.VMEM),
        scratch_shapes=[pltpu.SemaphoreType.DMA((7,)), pltpu.SemaphoreType.DMA((7,)), pltpu.SemaphoreType.DMA],
        compiler_params=pltpu.CompilerParams(vmem_limit_bytes=VMEM_LIMIT),
    )(blk, *(() if after is None else (after,)))


BIG = dict(w_in=("col", "ab_w_in", 0), w_out=("row", "ab_w_out", 0), up0=("col", "ffn_up_w", 0),
           dn0=("row", "ffn_down_w", 0), pw1=("col", "conv_pw1_w", 0), pw2=("row", "conv_pw2_w", 0),
           up1=("col", "ffn_up_w", 1), dn1=("row", "ffn_down_w", 1))
N_CHIPS = 4
HBM = pl.BlockSpec(memory_space=pltpu.HBM)
SEM = pl.BlockSpec(memory_space=pltpu.SEMAPHORE)
EFFECT = pltpu.SideEffectType.DATAFLOW_SIDE_EFFECTING


def _region(kind, ref, q, n):
    if kind == "col":
        return ref.at[:, pl.ds(q * n, n)]
    return ref.at[pl.ds(q * n, n), :]


def _gather_plan(kind, n):
    def remote(src, land, pos):
        x, y, c, chips = pos
        mine = _region(kind, land, 2 * x + y, n)
        return [(mine, mine, (*chip, c)) for chip in chips]

    return ("gather", kind, n), remote


def _scatter_plan(kind, n):
    def remote(src, land, pos):
        _, _, c, chips = pos
        return [(_region(kind, src, 2 * chip[0] + chip[1], n), land.at[j], (*chip, c)) for j, chip in enumerate(chips)]

    return ("scatter", kind, n), remote


def _everyone_plan(rows):
    def remote(src, land, pos):
        x, y, c, _ = pos
        mine = land.at[pl.ds((4 * x + 2 * y + c) * rows, rows), :]
        flip = lambda v, bit: 1 - v if bit else v
        return [(src, mine, (flip(x, k & 4), flip(y, k & 2), flip(c, k & 1))) for k in range(1, 8)]

    return ("everyone", rows), remote


def _sibling_plan():
    def remote(src, land, pos):
        x, y, c, _ = pos
        return [(src, land, (x, y, 1 - c))]

    return ("sibling",), remote


def _split_start(name, items, after=None):
    n = len(items)
    plans = [it[2] for it in items]
    n_in = 2 * n + (after is not None)

    def body(*refs):
        srcs, lands = refs[:n], refs[n:2 * n]
        sends, recvs = refs[n_in:n_in + n], refs[n_in + n:n_in + 2 * n]
        token = refs[n_in + 4 * n]
        pos = _mesh_pos()
        for a, (_, remote) in enumerate(plans):
            for k, (s, d, dev) in enumerate(remote(srcs[a], lands[a], pos)):
                pltpu.make_async_remote_copy(src_ref=s, dst_ref=d, send_sem=sends[a].at[k], recv_sem=recvs[a].at[k],
                                             device_id=dev, device_id_type=MESH).start()
        token[...] = jnp.zeros_like(token)

    sems = [pltpu.SemaphoreType.DMA((it[3],)) for it in items]
    bufs = [pltpu.HBM(it[k].shape, it[k].dtype) for k in (0, 1) for it in items]
    outs = pl.pallas_call(
        body, name=name, out_shape=[*sems, *sems, *bufs, jax.ShapeDtypeStruct((8, LANES), F32)],
        in_specs=[HBM] * (2 * n) + [ANY] * (n_in - 2 * n),
        out_specs=[SEM] * (2 * n) + [HBM] * (2 * n) + [pl.BlockSpec(memory_space=pltpu.VMEM)],
        input_output_aliases={i: 2 * n + i for i in range(2 * n)},
        compiler_params=pltpu.CompilerParams(has_side_effects=EFFECT),
    )(*[pltpu.with_memory_space_constraint(it[k], pltpu.HBM) for k in (0, 1) for it in items],
      *(() if after is None else (after,)))
    state = [(items[a][2], items[a][3], outs[2 * n + a], outs[3 * n + a], outs[a], outs[n + a]) for a in range(n)]
    return state, outs[4 * n]


def _split_wait(name, state, after):
    n = len(state)

    def body(*refs):
        srcs, lands = refs[:n], refs[n:2 * n]
        sends, recvs = refs[2 * n:3 * n], refs[3 * n:4 * n]
        pos = _mesh_pos()
        for a, ((_, remote), *_) in enumerate(state):
            for k, (s, d, dev) in enumerate(remote(srcs[a], lands[a], pos)):
                cp = pltpu.make_async_remote_copy(src_ref=s, dst_ref=d, send_sem=sends[a].at[k], recv_sem=recvs[a].at[k],
                                                  device_id=dev, device_id_type=MESH)
                cp.wait_send()
                cp.wait_recv()

    bufs = [st[k] for k in (2, 3) for st in state]
    outs = pl.pallas_call(
        body, name=name, out_shape=[pltpu.HBM(b.shape, b.dtype) for b in bufs],
        in_specs=[HBM] * (2 * n) + [SEM] * (2 * n) + [ANY], out_specs=[HBM] * (2 * n),
        input_output_aliases={i: i for i in range(2 * n)},
        compiler_params=pltpu.CompilerParams(has_side_effects=EFFECT),
    )(*bufs, *[st[k] for k in (4, 5) for st in state], after)
    return outs[:n], outs[n:]


class _Weights:
    def __init__(self, w, q, after):
        unused = jnp.zeros((16, LANES), BF16)

        def item(name, after):
            kind, pname, layer = BIG[name]
            _, r, c = w[pname].shape
            land = _cast_into_full(f"cast_{name}", w[pname], layer, q, kind, after)
            return unused, land, _gather_plan(kind, c if kind == "col" else r), N_CHIPS - 1

        first, *rest = BIG
        state1, token1 = _split_start("gw_start_first", [item(first, after)], after)
        state2, self.token = _split_start("gw_start_rest", [item(name, token1) for name in rest], token1)
        self.pending = dict(zip(BIG, state1 + state2))
        self.ready = {}

    def get(self, name, after=None):
        if name not in self.ready:
            self.ready[name] = _split_wait(f"gw_wait_{name}", [self.pending.pop(name)], after)[1][0]
        return self.ready[name]


class _GradPipe:
    def __init__(self, q, w, m, v):
        self.q, self.w, self.m, self.v = q, w, m, v
        self.stage, self.results = {}, {}

    def start_small(self, dmod, lossv, small, after):
        self.small_names = [n for n in REPLICATED if n != "ada_b"] + list(SMALL_SHARDED)
        payload = [dmod.reshape(2, -1), lossv] + [small[n] for n in self.small_names]
        self.small_shapes = [p.shape for p in payload]
        packed = _pack(payload)
        land = jnp.zeros((8 * PACK_ROWS, packed.shape[1]), F32)
        self.small_state, token = _split_start("ag_grads_start", [(packed, land, _everyone_plan(PACK_ROWS), 7)], after)
        return token

    def wait_small(self, after):
        srcs, lands = _split_wait("ag_grads_wait", self.small_state, after)
        return srcs[0], lands[0]

    def scatter(self, group, grads, after=None):
        items = []
        for name, g in grads.items():
            kind = BIG[name][0]
            rows, cols = g.shape
            n = (cols if kind == "col" else rows) // N_CHIPS
            reg = (rows, n) if kind == "col" else (n, cols)
            items.append((g, lax.empty((N_CHIPS - 1, *reg), BF16), _scatter_plan(kind, n), N_CHIPS - 1))
        state, token = _split_start(f"gs_start_{group}", items, after)
        self.stage[group] = (list(grads), state)
        return token[0, 0]

    def collect(self, group, after):
        names, state = self.stage[group]
        srcs, lands = _split_wait(f"gs_wait_{group}", state, after)
        items = []
        for name, st, g, land in zip(names, state, srcs, lands):
            _, kind, n = st[0][0]
            part = _sum4(f"sum_{name}", g, land, self.q, kind, n)
            items.append((part, lax.empty(part.shape, F32), _sibling_plan(), 1))
        state, token = _split_start(f"sw_start_{group}", items)
        self.stage[group] = (names, state)
        return token

    def finish(self, group, after):
        names, state = self.stage.pop(group)
        srcs, lands = _split_wait(f"sw_wait_{group}", state, after)
        for name, mine, theirs in zip(names, srcs, lands):
            _, pname, layer = BIG[name]
            self.results[pname] = _adamw_sum(f"adamw_{name}", self.w[pname], self.m[pname], self.v[pname], layer,
                                             mine, theirs, self.results.get(pname))


PACK_ROWS = 8


def _pack(arrays):
    flat = jnp.concatenate([a.reshape(-1) for a in arrays])
    n = flat.shape[0]
    padded = -(-n // (PACK_ROWS * LANES)) * (PACK_ROWS * LANES)
    return jnp.pad(flat, (0, padded - n)).reshape(PACK_ROWS, padded // PACK_ROWS)


def _unpack(packed, shapes):
    flat = packed.reshape(-1)
    out, off = [], 0
    for s in shapes:
        n = 1
        for d in s:
            n *= d
        out.append(flat[off:off + n].reshape(s))
        off += n
    return out


REPLICATED = ("ada_b", "norm_mix_g", "norm_ffn_g", "a_vnorm_g", "a_spatial_w", "a_spatial_b", "b_q_norm_g",
              "b_k_norm_g", "ffn_dw_b")
SMALL_SHARDED = ("conv_pw1_b", "conv_dw_w", "conv_dw_b", "conv_ln_g", "conv_ln_b", "conv_pw2_b", "ffn_dw_w")
WEIGHTS = ("ada_w", "ada_b", "norm_mix_g", "norm_ffn_g", "ab_w_in", "a_vnorm_g", "a_spatial_w", "a_spatial_b",
           "b_q_norm_g", "b_k_norm_g", "ab_w_out", "conv_pw1_w", "conv_pw1_b", "conv_dw_w", "conv_dw_b", "conv_ln_g",
           "conv_ln_b", "conv_pw2_w", "conv_pw2_b", "ffn_up_w", "ffn_dw_w", "ffn_dw_b", "ffn_down_w")

def kernel(x, c, positions, ada_w, ada_b, norm_mix_g, norm_ffn_g, ab_w_in, a_vnorm_g, a_spatial_w, a_spatial_b, b_q_norm_g, b_k_norm_g, ab_w_out, conv_pw1_w, conv_pw1_b, conv_dw_w, conv_dw_b, conv_ln_g, conv_ln_b, conv_pw2_w, conv_pw2_b, ffn_up_w, ffn_dw_w, ffn_dw_b, ffn_down_w, loss_target, m_ada_w, m_ada_b, m_norm_mix_g, m_norm_ffn_g, m_ab_w_in, m_a_vnorm_g, m_a_spatial_w, m_a_spatial_b, m_b_q_norm_g, m_b_k_norm_g, m_ab_w_out, m_conv_pw1_w, m_conv_pw1_b, m_conv_dw_w, m_conv_dw_b, m_conv_ln_g, m_conv_ln_b, m_conv_pw2_w, m_conv_pw2_b, m_ffn_up_w, m_ffn_dw_w, m_ffn_dw_b, m_ffn_down_w, v_ada_w, v_ada_b, v_norm_mix_g, v_norm_ffn_g, v_ab_w_in, v_a_vnorm_g, v_a_spatial_w, v_a_spatial_b, v_b_q_norm_g, v_b_k_norm_g, v_ab_w_out, v_conv_pw1_w, v_conv_pw1_b, v_conv_dw_w, v_conv_dw_b, v_conv_ln_g, v_conv_ln_b, v_conv_pw2_w, v_conv_pw2_b, v_ffn_up_w, v_ffn_dw_w, v_ffn_dw_b, v_ffn_down_w):
    w = dict(ada_w=ada_w, ada_b=ada_b, norm_mix_g=norm_mix_g, norm_ffn_g=norm_ffn_g, ab_w_in=ab_w_in, a_vnorm_g=a_vnorm_g, a_spatial_w=a_spatial_w, a_spatial_b=a_spatial_b, b_q_norm_g=b_q_norm_g, b_k_norm_g=b_k_norm_g, ab_w_out=ab_w_out, conv_pw1_w=conv_pw1_w, conv_pw1_b=conv_pw1_b, conv_dw_w=conv_dw_w, conv_dw_b=conv_dw_b, conv_ln_g=conv_ln_g, conv_ln_b=conv_ln_b, conv_pw2_w=conv_pw2_w, conv_pw2_b=conv_pw2_b, ffn_up_w=ffn_up_w, ffn_dw_w=ffn_dw_w, ffn_dw_b=ffn_dw_b, ffn_down_w=ffn_down_w)
    m = dict(ada_w=m_ada_w, ada_b=m_ada_b, norm_mix_g=m_norm_mix_g, norm_ffn_g=m_norm_ffn_g, ab_w_in=m_ab_w_in, a_vnorm_g=m_a_vnorm_g, a_spatial_w=m_a_spatial_w, a_spatial_b=m_a_spatial_b, b_q_norm_g=m_b_q_norm_g, b_k_norm_g=m_b_k_norm_g, ab_w_out=m_ab_w_out, conv_pw1_w=m_conv_pw1_w, conv_pw1_b=m_conv_pw1_b, conv_dw_w=m_conv_dw_w, conv_dw_b=m_conv_dw_b, conv_ln_g=m_conv_ln_g, conv_ln_b=m_conv_ln_b, conv_pw2_w=m_conv_pw2_w, conv_pw2_b=m_conv_pw2_b, ffn_up_w=m_ffn_up_w, ffn_dw_w=m_ffn_dw_w, ffn_dw_b=m_ffn_dw_b, ffn_down_w=m_ffn_down_w)
    v = dict(ada_w=v_ada_w, ada_b=v_ada_b, norm_mix_g=v_norm_mix_g, norm_ffn_g=v_norm_ffn_g, ab_w_in=v_ab_w_in, a_vnorm_g=v_a_vnorm_g, a_spatial_w=v_a_spatial_w, a_spatial_b=v_a_spatial_b, b_q_norm_g=v_b_q_norm_g, b_k_norm_g=v_b_k_norm_g, ab_w_out=v_ab_w_out, conv_pw1_w=v_conv_pw1_w, conv_pw1_b=v_conv_pw1_b, conv_dw_w=v_conv_dw_w, conv_dw_b=v_conv_dw_b, conv_ln_g=v_conv_ln_g, conv_ln_b=v_conv_ln_b, conv_pw2_w=v_conv_pw2_w, conv_pw2_b=v_conv_pw2_b, ffn_up_w=v_ffn_up_w, ffn_dw_w=v_ffn_dw_w, ffn_dw_b=v_ffn_dw_b, ffn_down_w=v_ffn_down_w)
    S, D = x.shape[1], x.shape[2]
    xi, yi, ci = lax.axis_index("x"), lax.axis_index("y"), lax.axis_index("c")
    q = 2 * xi + yi
    b = 2 * q + ci
    take_dev = lambda g: g.reshape(8, PACK_ROWS, -1)

    c_all = _allgather8("ag_c", c.reshape(PACK_ROWS, D // PACK_ROWS)).reshape(8, D)
    n_ada = ada_w.shape[2]
    mod_sh = _ada_fwd("ada_fwd", c_all, ada_w, lax.dynamic_slice_in_dim(ada_b, q * n_ada, n_ada, axis=1))
    sh_shapes = [mod_sh.shape] + [w[n].shape for n in SMALL_SHARDED]
    gathered_mod = _allgather8("ag_mod", _pack([mod_sh] + [w[n] for n in SMALL_SHARDED]))
    per_chip = [_unpack(blk, sh_shapes) for blk in take_dev(gathered_mod)[0::2]]
    mod_g = jnp.stack([pc[0] for pc in per_chip])
    mod_mine = lax.dynamic_index_in_dim(mod_g, b, axis=2, keepdims=False)
    mod = mod_mine.transpose(1, 0, 2).reshape(2, 6, D)
    sp = {n: jnp.concatenate([pc[1 + i] for pc in per_chip], axis=-1) for i, n in enumerate(SMALL_SHARDED)}
    sp.update({n: w[n] for n in REPLICATED if n != "ada_b"})

    wb = _Weights(w, q, gathered_mod)
    mod = mod + wb.token[0, 0]

    pipe = _GradPipe(q, w, m, v)
    lossv, grad_x, late = _local_step(x[0], loss_target[0], positions[0], mod, wb, sp, pipe)

    pipe.finish("g1", grad_x)
    pipe.finish("g2", pipe.results["ffn_up_w"][0])
    swapped = pipe.collect("g3", pipe.results["conv_pw1_w"][0])
    swapped = pipe.collect("g4", swapped)

    own, gathered = pipe.wait_small(swapped)
    totals = _unpack(_sum8("sum_grads", gathered, own), pipe.small_shapes)
    grads = dict(zip(["ada_b", "loss_columns"] + pipe.small_names, totals))
    loss = 0.5 * jnp.sum(grads.pop("loss_columns")) / D
    late_g = _allgather8("ag_late", _pack(list(late)), gathered)
    late_tot = _unpack(_sum8("sum_late", late_g), [(3, D)])[0]
    grads["ada_b"] = grads["ada_b"].at[0, :2 * D].add(late_tot[:2].reshape(-1))
    grads["norm_mix_g"] = grads["norm_mix_g"].at[0].add(late_tot[2])
    for n in SMALL_SHARDED:
        n_sh = w[n].shape[-1]
        grads[n] = lax.dynamic_slice_in_dim(grads[n], q * n_sh, n_sh, axis=grads[n].ndim - 1)
    dmod_of = lambda packed: packed.reshape(packed.shape[0] // PACK_ROWS, -1)[:, :2 * 6 * D].reshape(-1, 2, 6 * D)
    dmod_all = jnp.where((jnp.arange(8) == b)[:, None, None], dmod_of(own), dmod_of(gathered))
    late_all = take_dev(late_g).reshape(8, -1)[:, :3 * D].reshape(8, 3, D)
    dmod_all = dmod_all.at[:, 0, :2 * D].add(late_all[:, :2].reshape(8, 2 * D))
    dmod_sh = lax.dynamic_slice_in_dim(dmod_all, q * n_ada, n_ada, axis=2).transpose(1, 0, 2)

    pipe.finish("g3", dmod_sh)
    pipe.finish("g4", pipe.results["ffn_up_w"][0])
    grads["ada_w"], delta_ada, m_ada, v_ada = _ada_update("ada_update", c_all, dmod_sh, ada_w, m_ada_w, v_ada_w)
    delta, new_m, new_v = dict(ada_w=delta_ada), dict(ada_w=m_ada), dict(ada_w=v_ada)
    rest = list(REPLICATED) + list(SMALL_SHARDED)
    rest_shapes = [w[n].shape for n in rest]
    outs = _adamw("adamw_small", *[_pack([src[n].reshape(w[n].shape) for n in rest]) for src in (w, grads, m, v)])
    for tgt, packed in zip((delta, new_m, new_v), outs):
        tgt.update(dict(zip(rest, _unpack(packed, rest_shapes))))
    for n in rest:
        grads[n] = grads[n].reshape(w[n].shape)
    pipe.finish("g5", pipe.collect("g5", outs[0]))
    for n, res in pipe.results.items():
        grads[n], delta[n], new_m[n], new_v[n] = res

    return (loss, grad_x[None], *[grads[n] for n in WEIGHTS], *[delta[n] for n in WEIGHTS],
            *[new_m[n] for n in WEIGHTS], *[new_v[n] for n in WEIGHTS])
```

```python
import functools

import jax
import jax.numpy as jnp
from jax import lax
from jax.experimental import pallas as pl
from jax.experimental.pallas import tpu as pltpu

F32, BF16 = jnp.float32, jnp.bfloat16
EPS = 1e-6
NEG = -1e30
ROPE_THETA = 10000.0
LANES = 128
VMEM_LIMIT = 56 * 1024 * 1024
ADAM_LR, ADAM_B1, ADAM_B2, ADAM_EPS, ADAM_WD, ADAM_STEP = 0.001, 0.9, 0.999, 1e-08, 0.01, 10
MESH = pl.DeviceIdType.MESH


def _params(sem):
    return pltpu.CompilerParams(dimension_semantics=sem, vmem_limit_bytes=VMEM_LIMIT)


_DN = {"nn": (((1,), (0,)), ((), ())), "nt": (((1,), (1,)), ((), ())), "tn": (((0,), (0,)), ((), ()))}


def _matmul(name, a, b, mode, out_dtype, tm=512, tn=512, tk=1024, bias=None, after=None):
    if mode == "nn":
        (M, K), N = a.shape, b.shape[1]
    elif mode == "nt":
        (M, K), N = a.shape, b.shape[0]
    else:
        (K, M), N = a.shape, b.shape[1]
    tm, tn, tk = min(tm, M), min(tn, N), min(tk, K)
    assert M % tm == 0 and N % tn == 0 and K % tk == 0, (name, M, N, K, tm, tn, tk)
    nk = K // tk
    if mode == "tn":
        a_spec = pl.BlockSpec((tk, tm), lambda i, j, k: (k, i))
    else:
        a_spec = pl.BlockSpec((tm, tk), lambda i, j, k: (i, k))
    if mode == "nt":
        b_spec = pl.BlockSpec((tn, tk), lambda i, j, k: (j, k))
    else:
        b_spec = pl.BlockSpec((tk, tn), lambda i, j, k: (k, j))
    in_specs, args = [a_spec, b_spec], [a, b]
    if bias is not None:
        in_specs.append(pl.BlockSpec((1, tn), lambda i, j, k: (0, j)))
        args.append(bias)
    if after is not None:
        in_specs.append(pl.BlockSpec(memory_space=pl.ANY))
        args.append(after)
    n_in = len(args)

    def body(*refs):
        a_ref, b_ref, o_ref = refs[0], refs[1], refs[n_in]
        p = lax.dot_general(a_ref[...], b_ref[...], _DN[mode], preferred_element_type=F32)

        def finish(acc):
            if bias is not None:
                acc = acc + refs[2][...]
            o_ref[...] = acc.astype(o_ref.dtype)

        if nk == 1:
            finish(p)
        else:
            acc_ref = refs[n_in + 1]
            k = pl.program_id(2)

            @pl.when(k == 0)
            def _():
                acc_ref[...] = p

            @pl.when(k > 0)
            def _():
                acc_ref[...] += p

            @pl.when(k == nk - 1)
            def _():
                finish(acc_ref[...])

    return pl.pallas_call(
        body, name=name, grid=(M // tm, N // tn, nk), in_specs=in_specs,
        out_specs=pl.BlockSpec((tm, tn), lambda i, j, k: (i, j)),
        out_shape=jax.ShapeDtypeStruct((M, N), out_dtype),
        scratch_shapes=[pltpu.VMEM((tm, tn), F32)] if nk > 1 else [],
        compiler_params=_params(("parallel", "parallel", "arbitrary")),
    )(*args)


def _rowcall(name, body, nrows, tm, ins, outs, scratch=()):
    nblk = nrows // tm
    assert nrows % tm == 0

    def spec(kind, shape):
        k = kind[0]
        if k == "row":
            cw, cb = kind[1] or shape[-1], kind[2]
            return pl.BlockSpec((tm, cw), lambda i: (i, cb))
        if k == "prev":
            hb, cw, cb = kind[1], kind[2] or shape[-1], kind[3]
            r = tm // hb
            return pl.BlockSpec((hb, cw), lambda i: (jnp.maximum(i * r - 1, 0), cb))
        if k == "next":
            hb, cw, cb = kind[1], kind[2] or shape[-1], kind[3]
            r, last = tm // hb, nrows // hb - 1
            return pl.BlockSpec((hb, cw), lambda i: (jnp.minimum((i + 1) * r, last), cb))
        if k == "off":
            off, cw, cb = kind[1], kind[2] or shape[-1], kind[3]
            return pl.BlockSpec((tm, cw), lambda i: (jnp.clip(i + off, 0, nblk - 1), cb))
        if k == "cls":
            dil, off = kind[1], kind[2]
            return pl.BlockSpec((dil, tm // dil, shape[-1]), lambda i: (0, jnp.clip(i + off, 0, nblk - 1), 0))
        nd = len(shape)
        return pl.BlockSpec(tuple(shape), lambda i: (0,) * nd)

    has_acc = any(o[2][0] == "acc" for o in outs)
    return pl.pallas_call(
        body, name=name, grid=(nblk,),
        in_specs=[spec(kind, a.shape) for a, kind in ins],
        out_specs=[spec(kind, shape) for shape, _, kind in outs],
        out_shape=[jax.ShapeDtypeStruct(tuple(shape), dt) for shape, dt, _ in outs],
        scratch_shapes=list(scratch),
        compiler_params=_params(("arbitrary",) if has_acc else ("parallel",)),
    )(*[a for a, _ in ins])


ROW = ("row", None, 0)
FULL = ("full",)
ACC = ("acc",)


def _colsum(x):
    return jnp.sum(x, axis=0, keepdims=True)


def _acc_add(i, ref, val, rows=None):
    idx = (slice(None),) * len(ref.shape) if rows is None else rows

    @pl.when(i == 0)
    def _():
        ref[idx] = val

    @pl.when(i > 0)
    def _():
        ref[idx] = ref[idx] + val


def _sigmoid(x):
    return 1.0 / (1.0 + jnp.exp(-x))


def _gelu(x):
    return 0.5 * x * (1.0 + lax.erf(x * (2.0 ** -0.5)))


def _gelu_grad(x):
    return 0.5 * (1.0 + lax.erf(x * (2.0 ** -0.5))) + x * jnp.exp(-0.5 * x * x) * ((2.0 * jnp.pi) ** -0.5)


SUBLANES = 8


def _phases(ext, sign):
    n = ext.shape[0]
    return [ext if b == 0 else pltpu.roll(ext, b if sign > 0 else n - b, axis=0) for b in range(SUBLANES)]


def _shift_prev(phases, s, hb):
    a, b = divmod(s, SUBLANES)
    return phases[b][hb - SUBLANES * a:phases[b].shape[0] - SUBLANES * a]


def _shift_next(phases, s, tm):
    a, b = divmod(s, SUBLANES)
    return phases[b][SUBLANES * a:SUBLANES * a + tm]


def _rms_mod_val(x, g, sc, sh):
    r = lax.rsqrt(jnp.mean(x * x, axis=-1, keepdims=True) + EPS)
    return x * r * g * (1.0 + sc) + sh


def _mod_first(name, x, g, sc, sh, tm=512):
    S, D = x.shape

    def body(x_ref, g_ref, sc_ref, sh_ref, h_ref):
        h_ref[...] = _rms_mod_val(x_ref[...], g_ref[...], sc_ref[...], sh_ref[...]).astype(BF16)

    return _rowcall(name, body, S, tm, [(x, ROW), (g, FULL), (sc, FULL), (sh, FULL)], [((S, D), BF16, ROW)])[0]


def _resid_mod(name, x, y, gate, g, sc, sh, tm=512):
    S, D = x.shape

    def body(x_ref, y_ref, gate_ref, g_ref, sc_ref, sh_ref, xo_ref, h_ref):
        xn = x_ref[...] + gate_ref[...] * y_ref[...]
        xo_ref[...] = xn
        h_ref[...] = _rms_mod_val(xn, g_ref[...], sc_ref[...], sh_ref[...]).astype(BF16)

    return _rowcall(name, body, S, tm,
                    [(x, ROW), (y, ROW), (gate, FULL), (g, FULL), (sc, FULL), (sh, FULL)],
                    [((S, D), F32, ROW), ((S, D), BF16, ROW)])


def _gate_bwd_val(i, d, y_ref, gate_ref, dy_ref, dg_ref, db_ref):
    dy = d * gate_ref[...]
    dy_ref[...] = dy.astype(BF16)
    _acc_add(i, dg_ref, _colsum(d * y_ref[...]))
    _acc_add(i, db_ref, _colsum(dy))


GATE_OUTS = lambda S, D: [((S, D), BF16, ROW), ((1, D), F32, ACC), ((1, D), F32, ACC)]


def _loss_head(name, x, y, gate, tgt, tm=512):
    S, D = x.shape

    def body(x_ref, y_ref, gate_ref, t_ref, dx_ref, l_ref, dy_ref, dg_ref, db_ref):
        i = pl.program_id(0)
        err = x_ref[...] + gate_ref[...] * y_ref[...] - t_ref[...]
        d = err * (1.0 / D)
        dx_ref[...] = d
        _acc_add(i, l_ref, _colsum(err * err))
        _gate_bwd_val(i, d, y_ref, gate_ref, dy_ref, dg_ref, db_ref)

    return _rowcall(name, body, S, tm, [(x, ROW), (y, ROW), (gate, FULL), (tgt, ROW)],
                    [((S, D), F32, ROW), ((1, D), F32, ACC)] + GATE_OUTS(S, D))


def _mod_bwd(name, dxo, dh, x, g, sc, y=None, gate=None, tm=512):
    S, D = x.shape
    gated = y is not None

    def body(d_ref, dh_ref, x_ref, g_ref, sc_ref, *rest):
        dx_ref, dsh_ref, dsc_ref, dg_ref = rest[2 * gated:2 * gated + 4]
        i = pl.program_id(0)
        xv, dh_v, gv = x_ref[...], dh_ref[...], g_ref[...]
        r = lax.rsqrt(jnp.mean(xv * xv, axis=-1, keepdims=True) + EPS)
        n = xv * r
        _acc_add(i, dsh_ref, _colsum(dh_v))
        _acc_add(i, dsc_ref, _colsum(dh_v * (n * gv)))
        dy = dh_v * (1.0 + sc_ref[...])
        _acc_add(i, dg_ref, _colsum(dy * n))
        dn = dy * gv
        dx = d_ref[...] + r * (dn - n * jnp.mean(dn * n, axis=-1, keepdims=True))
        dx_ref[...] = dx
        if gated:
            _gate_bwd_val(i, dx, rest[0], rest[1], *rest[6:9])

    ins = [(dxo, ROW), (dh, ROW), (x, ROW), (g, FULL), (sc, FULL)] + ([(y, ROW), (gate, FULL)] if gated else [])
    outs = [((S, D), F32, ROW), ((1, D), F32, ACC), ((1, D), F32, ACC), ((1, D), F32, ACC)]
    return _rowcall(name, body, S, tm, ins, outs + (GATE_OUTS(S, D) if gated else []))


HB16 = 16


def _conv3_val(ph, w, b, hb):
    return w[2:3] * _shift_prev(ph, 0, hb) + w[1:2] * _shift_prev(ph, 1, hb) + w[0:1] * _shift_prev(ph, 2, hb) + b


def _halo_first(halo_ref, tile_ref, live):
    return _phases(jnp.concatenate([halo_ref[...].astype(F32) * live, tile_ref[...].astype(F32)], axis=0), 1)


def _glu3_fwd(name, u, w, b, tm=256):
    S, F2 = u.shape
    Fh = F2 // 2

    def body(ua_ref, ub_ref, ha_ref, hb_ref, w_ref, b_ref, o_ref, z_ref):
        live = (pl.program_id(0) > 0).astype(F32)
        wv, bv = w_ref[...], b_ref[...]
        za = _conv3_val(_halo_first(ha_ref, ua_ref, live), wv[:, :Fh], bv[:, :Fh], HB16)
        zb = _conv3_val(_halo_first(hb_ref, ub_ref, live), wv[:, Fh:], bv[:, Fh:], HB16)
        o_ref[...] = (za * _sigmoid(za) * zb).astype(BF16)
        z_ref[:, :Fh] = za.astype(BF16)
        z_ref[:, Fh:] = zb.astype(BF16)

    return _rowcall(name, body, S, tm,
                    [(u, ("row", Fh, 0)), (u, ("row", Fh, 1)), (u, ("prev", HB16, Fh, 0)), (u, ("prev", HB16, Fh, 1)),
                     (w, FULL), (b, FULL)],
                    [((S, Fh), BF16, ROW), ((S, F2), BF16, ROW)])


def _glu3_bwd(name, z, dhm, tm=256):
    S, F2 = z.shape
    Fh = F2 // 2

    def body(za_ref, zb_ref, d_ref, dz_ref, db_ref):
        i = pl.program_id(0)
        za, zb, d = za_ref[...].astype(F32), zb_ref[...].astype(F32), d_ref[...]
        sg = _sigmoid(za)
        da = d * zb * (sg * (1.0 + za * (1.0 - sg)))
        db = d * (za * sg)
        dz_ref[:, :Fh] = da.astype(BF16)
        dz_ref[:, Fh:] = db.astype(BF16)
        _acc_add(i, db_ref, jnp.concatenate([_colsum(da), _colsum(db)], axis=1))

    return _rowcall(name, body, S, tm, [(z, ("row", Fh, 0)), (z, ("row", Fh, 1)), (dhm, ROW)],
                    [((S, F2), BF16, ROW), ((1, F2), F32, ACC)])


def _conv3_bwd(name, dz, u, w, tm=256):
    S, F2 = dz.shape
    nblk = S // tm
    K = w.shape[0]

    def body(d_ref, n_ref, u_ref, w_ref, o_ref, dw_ref):
        i = pl.program_id(0)
        live = (i < nblk - 1).astype(F32)
        ph = _phases(jnp.concatenate([d_ref[...].astype(F32), n_ref[...].astype(F32) * live], axis=0), -1)
        wv, uv = w_ref[...], u_ref[...].astype(F32)
        shifted = [_shift_next(ph, K - 1 - k, tm) for k in range(K)]
        o_ref[...] = functools.reduce(lambda a, t: a + t, [wv[k:k + 1] * shifted[k] for k in range(K)]).astype(BF16)
        for k in range(K):
            _acc_add(i, dw_ref, _colsum(uv * shifted[k]), rows=(slice(k, k + 1), slice(None)))

        @pl.when(i == 0)
        def _():
            dw_ref[K:, :] = jnp.zeros((dw_ref.shape[0] - K, F2), F32)

    return _rowcall(name, body, S, tm, [(dz, ROW), (dz, ("next", HB16, None, 0)), (u, ROW), (w, FULL)],
                    [((S, F2), BF16, ROW), ((SUBLANES, F2), F32, ACC)])


def _ffn_fwd(name, h, w_up, w_dn, dw_w, dw_b):
    u = _matmul(f"{name}_up", h, w_up, "nn", BF16, tm=2048)
    hm, z = _glu3_fwd(f"{name}_glu", u, dw_w, dw_b)
    f = _matmul(f"{name}_dn", hm, w_dn, "nn", F32, tm=1024, tn=1024, tk=w_dn.shape[0])
    return f, (u, hm, z)


def _ffn_bwd(name, dy, h, u, hm, z, w_up, w_dn, dw_w, dw_b):
    Fh = w_dn.shape[0]
    dhm = _matmul(f"{name}_ddn_x", dy, w_dn, "nt", F32, tm=1024, tn=Fh // 2)
    g_dn = _matmul(f"{name}_ddn_w", hm, dy, "tn", BF16, tm=Fh // 2, tn=1024, tk=2048)
    dz, g_dw_b = _glu3_bwd(f"{name}_dglu", z, dhm)
    du, taps = _conv3_bwd(f"{name}_dconv", dz, u, dw_w)
    g_up = _matmul(f"{name}_dup_w", h, du, "tn", BF16, tm=1024, tn=Fh // 2, tk=2048)
    dh = _matmul(f"{name}_dup_x", du, w_up, "nt", F32, tm=1024, tn=1024, tk=Fh)
    return dh, dict(up=g_up, dn=g_dn, dw_w=taps[0:dw_w.shape[0]], dw_b=g_dw_b)


HB32 = 32


def _glu31_fwd(name, p, w, b, tm=256):
    S, D2 = p.shape
    D = D2 // 2
    K = w.shape[0]

    def body(a_ref, g_ref, ha_ref, hg_ref, w_ref, b_ref, o_ref):
        live = (pl.program_id(0) > 0).astype(F32)
        y1 = a_ref[...] * _sigmoid(g_ref[...])
        ph = _phases(jnp.concatenate([ha_ref[...] * _sigmoid(hg_ref[...]) * live, y1], axis=0), 1)
        wv = w_ref[...]
        acc = b_ref[...] + wv[K - 1:K] * y1
        for k in range(K - 1):
            acc = acc + wv[k:k + 1] * _shift_prev(ph, K - 1 - k, HB32)
        o_ref[...] = acc

    return _rowcall(name, body, S, tm,
                    [(p, ("row", D, 0)), (p, ("row", D, 1)), (p, ("prev", HB32, D, 0)), (p, ("prev", HB32, D, 1)),
                     (w, FULL), (b, FULL)],
                    [((S, D), F32, ROW)])[0]


def _ln_silu_fwd(name, y2, g, b, tm=512):
    S, D = y2.shape

    def body(y_ref, g_ref, b_ref, o_ref):
        y = y_ref[...]
        mu = jnp.mean(y, axis=-1, keepdims=True)
        yc = y - mu
        rs = lax.rsqrt(jnp.mean(yc * yc, axis=-1, keepdims=True) + EPS)
        y3 = yc * rs * g_ref[...] + b_ref[...]
        o_ref[...] = (y3 * _sigmoid(y3)).astype(BF16)

    return _rowcall(name, body, S, tm, [(y2, ROW), (g, FULL), (b, FULL)], [((S, D), BF16, ROW)])[0]


def _ln_silu_bwd(name, y2, dy4, g, b, tm=512):
    S, D = y2.shape

    def body(y_ref, d_ref, g_ref, b_ref, o_ref, dg_ref, db_ref):
        i = pl.program_id(0)
        y, gv = y_ref[...], g_ref[...]
        mu = jnp.mean(y, axis=-1, keepdims=True)
        yc = y - mu
        rs = lax.rsqrt(jnp.mean(yc * yc, axis=-1, keepdims=True) + EPS)
        n = yc * rs
        y3 = n * gv + b_ref[...]
        sg = _sigmoid(y3)
        dy3 = d_ref[...] * (sg * (1.0 + y3 * (1.0 - sg)))
        _acc_add(i, db_ref, _colsum(dy3))
        _acc_add(i, dg_ref, _colsum(dy3 * n))
        dn = dy3 * gv
        o_ref[...] = rs * (dn - jnp.mean(dn, axis=-1, keepdims=True) - n * jnp.mean(dn * n, axis=-1, keepdims=True))

    return _rowcall(name, body, S, tm, [(y2, ROW), (dy4, ROW), (g, FULL), (b, FULL)],
                    [((S, D), F32, ROW), ((1, D), F32, ACC), ((1, D), F32, ACC)])


def _glu31_bwd(name, p, dy2, w, tm=512):
    S, D2 = p.shape
    D = D2 // 2
    K = w.shape[0]
    nblk = S // tm

    conv_rows, tap_rows, tap_group, tap_unroll = 16, SUBLANES, 4, 4

    def body(a_ref, g_ref, d_ref, dn_ref, w_ref, dp_ref, dw_ref, dcb_ref, dpb_ref, ph_d, y1_ref, dy1_ref, wb_ref):
        i = pl.program_id(0)
        live_next = (i < nblk - 1).astype(F32)
        a, sg, d = a_ref[...], _sigmoid(g_ref[...]), d_ref[...]
        y1_ref[...] = a * sg
        for b, ph in enumerate(_phases(jnp.concatenate([d, dn_ref[...] * live_next], axis=0), -1)):
            ph_d[b] = ph
        taps = [divmod(K - 1 - k, SUBLANES) for k in range(K)]

        @pl.when(i == 0)
        def _():
            for k in range(K):
                wb_ref[k] = jnp.broadcast_to(w_ref[k:k + 1, :], (SUBLANES, D))

        def conv_rows_at(rb, carry):
            r0 = pl.multiple_of(rb * conv_rows, conv_rows)
            accs = [jnp.zeros((SUBLANES, D), F32) for _ in range(conv_rows // SUBLANES)]
            for k, (rows8, phase) in enumerate(taps):
                wk = wb_ref[k]
                for u in range(len(accs)):
                    accs[u] = accs[u] + wk * ph_d[phase, pl.ds(r0 + SUBLANES * (rows8 + u), SUBLANES), :]
            for u, acc in enumerate(accs):
                dy1_ref[pl.ds(r0 + SUBLANES * u, SUBLANES), :] = acc
            return carry

        lax.fori_loop(0, tm // conv_rows, conv_rows_at, 0)
        for k0 in range(0, K, tap_group):
            group = taps[k0:k0 + tap_group]

            def tap_rows_at(rb, accs, group=group):
                for u in range(tap_unroll):
                    r0 = pl.multiple_of((rb * tap_unroll + u) * tap_rows, tap_rows)
                    yv = y1_ref[pl.ds(r0, tap_rows), :]
                    accs = tuple(acc + yv * ph_d[phase, pl.ds(r0 + SUBLANES * rows8, tap_rows), :]
                                 for acc, (rows8, phase) in zip(accs, group))
                return accs

            accs = lax.fori_loop(0, tm // (tap_rows * tap_unroll), tap_rows_at,
                                 tuple(jnp.zeros((tap_rows, D), F32) for _ in group))
            for j, acc in enumerate(accs):
                _acc_add(i, dw_ref, _colsum(acc), rows=(slice(k0 + j, k0 + j + 1), slice(None)))

        @pl.when(i == 0)
        def _():
            dw_ref[K:, :] = jnp.zeros((dw_ref.shape[0] - K, D), F32)

        _acc_add(i, dcb_ref, _colsum(d))
        dy1 = dy1_ref[...]
        da = dy1 * sg
        dg = dy1 * a * sg * (1.0 - sg)
        dp_ref[:, :D] = da.astype(BF16)
        dp_ref[:, D:] = dg.astype(BF16)
        _acc_add(i, dpb_ref, jnp.concatenate([_colsum(da), _colsum(dg)], axis=1))

    return _rowcall(name, body, S, tm,
                    [(p, ("row", D, 0)), (p, ("row", D, 1)), (dy2, ROW), (dy2, ("next", HB32, None, 0)), (w, FULL)],
                    [((S, D2), BF16, ROW), ((HB32, D), F32, ACC), ((1, D), F32, ACC), ((1, D2), F32, ACC)],
                    scratch=[pltpu.VMEM((SUBLANES, tm + HB32, D), F32), pltpu.VMEM((tm, D), F32),
                             pltpu.VMEM((tm, D), F32), pltpu.VMEM((K, SUBLANES, D), F32)])


CHUNK = 128
A_GROUPS = 4


def _group_ln(gv):
    ns, rss = [], []
    for g in range(A_GROUPS):
        xg = gv[:, g * LANES:(g + 1) * LANES]
        xc = xg - jnp.mean(xg, axis=-1, keepdims=True)
        rs = lax.rsqrt(jnp.mean(xc * xc, axis=-1, keepdims=True) + EPS)
        ns.append(xc * rs)
        rss.append(jnp.broadcast_to(rs, xg.shape))
    return jnp.concatenate(ns, axis=1), jnp.concatenate(rss, axis=1)


def _tril_mask():
    r = lax.broadcasted_iota(jnp.int32, (CHUNK, CHUNK), 0)
    c = lax.broadcasted_iota(jnp.int32, (CHUNK, CHUNK), 1)
    return r >= c


def _spatial(ws_ref, x, dn):
    mask = _tril_mask()
    rows = []
    for ci in range(x.shape[0] // CHUNK):
        cols = []
        for g in range(A_GROUPS):
            wm = jnp.where(mask, ws_ref[g], 0.0).astype(BF16)
            xb = x[ci * CHUNK:(ci + 1) * CHUNK, g * LANES:(g + 1) * LANES]
            cols.append(lax.dot_general(wm, xb, dn, preferred_element_type=F32))
        rows.append(jnp.concatenate(cols, axis=1))
    return jnp.concatenate(rows, axis=0)


def _mixa_fwd(name, z, vg, ws, bias_full, tm=512):
    S = z.shape[0]
    W = A_GROUPS * LANES

    def body(u_ref, v_ref, vg_ref, ws_ref, b_ref, o_ref):
        nh, _ = _group_ln(_gelu(v_ref[...]))
        vn = (nh * vg_ref[...]).astype(BF16)
        f = _spatial(ws_ref, vn, _DN["nn"]) + jnp.concatenate([b_ref[...]] * (tm // CHUNK), axis=0)
        o_ref[...] = (_gelu(u_ref[...]) * f).astype(BF16)

    return _rowcall(name, body, S, tm,
                    [(z, ("row", W, 0)), (z, ("row", W, 1)), (vg, FULL), (ws, FULL), (bias_full, FULL)],
                    [((S, W), BF16, ROW)])[0]


def _mixa_bwd(name, z, dyab, vg, ws, bias_full, tm=512):
    S = z.shape[0]
    W = A_GROUPS * LANES
    nch = tm // CHUNK

    def body(u_ref, v_ref, d_ref, vg_ref, ws_ref, b_ref, dz_ref, dws_ref, dbf_ref, dvg_ref):
        i = pl.program_id(0)
        u, v, d, vgv = u_ref[...], v_ref[...], d_ref[...], vg_ref[...]
        nh, rs = _group_ln(_gelu(v))
        vn = (nh * vgv).astype(BF16)
        f = _spatial(ws_ref, vn, _DN["nn"]) + jnp.concatenate([b_ref[...]] * nch, axis=0)
        dz_ref[:, :W] = (d * f * _gelu_grad(u)).astype(BF16)
        df = d * _gelu(u)
        dbf = df[0:CHUNK]
        for ci in range(1, nch):
            dbf = dbf + df[ci * CHUNK:(ci + 1) * CHUNK]
        _acc_add(i, dbf_ref, dbf)
        dfb = df.astype(BF16)
        mask = _tril_mask()
        for g in range(A_GROUPS):
            acc = jnp.zeros((CHUNK, CHUNK), F32)
            for ci in range(nch):
                blk = (slice(ci * CHUNK, (ci + 1) * CHUNK), slice(g * LANES, (g + 1) * LANES))
                acc = acc + lax.dot_general(dfb[blk], vn[blk], _DN["nt"], preferred_element_type=F32)
            _acc_add(i, dws_ref, jnp.where(mask, acc, 0.0)[None], rows=(slice(g, g + 1), slice(None), slice(None)))
        dvn = _spatial(ws_ref, dfb, _DN["tn"])
        _acc_add(i, dvg_ref, _colsum(dvn * nh))
        dnh = dvn * vgv
        parts = []
        for g in range(A_GROUPS):
            cs = slice(g * LANES, (g + 1) * LANES)
            dg_, ng = dnh[:, cs], nh[:, cs]
            parts.append(dg_ - jnp.mean(dg_, axis=-1, keepdims=True) - ng * jnp.mean(dg_ * ng, axis=-1, keepdims=True))
        dz_ref[:, W:] = (rs * jnp.concatenate(parts, axis=1) * _gelu_grad(v)).astype(BF16)

    return _rowcall(name, body, S, tm,
                    [(z, ("row", W, 0)), (z, ("row", W, 1)), (dyab, ("row", W, 0)), (vg, FULL), (ws, FULL),
                     (bias_full, FULL)],
                    [((S, 2 * W), BF16, ROW), ((A_GROUPS, CHUNK, CHUNK), F32, ACC), ((CHUNK, W), F32, ACC),
                     ((1, W), F32, ACC)])


HEAD = 64
N_HEADS = 8
BW = HEAD * N_HEADS
QB = 128
DILATIONS = (1, 4, 16)
QK_SCALE = HEAD ** -0.5


def _gsum64(x, ones_bd):
    x1 = x.astype(BF16)
    r1 = x - x1.astype(F32)
    x2 = r1.astype(BF16)
    x3 = (r1 - x2.astype(F32)).astype(BF16)
    dot = lambda t: jnp.dot(t, ones_bd, preferred_element_type=F32)
    return dot(x1) + dot(x2) + dot(x3)


def _swap32(x):
    n = x.shape[-1]
    up = pltpu.roll(x, n - HEAD // 2, axis=1)
    dn = pltpu.roll(x, HEAD // 2, axis=1)
    lane = lax.broadcasted_iota(jnp.int32, x.shape, 1)
    return jnp.where((lane % HEAD) < HEAD // 2, up, dn)


def _tile4(t):
    return jnp.concatenate([t] * (BW // LANES), axis=1)


def _stage_spec(tm):
    return pltpu.VMEM((BW // LANES, tm, LANES), F32)


def _to_classes(stage, x, dil):
    tm = x.shape[0]
    for j in range(BW // LANES):
        stage[j] = x[:, j * LANES:(j + 1) * LANES]
    return [jnp.concatenate([stage.at[j][pl.ds(r, tm // dil, stride=dil), :] for j in range(BW // LANES)], axis=1)
            for r in range(dil)]


def _from_classes(stage, cls, dil):
    rows = cls.shape[1]
    for r in range(dil):
        for j in range(BW // LANES):
            stage.at[j][pl.ds(r, rows, stride=dil), :] = cls[r, :, j * LANES:(j + 1) * LANES]
    return jnp.concatenate([stage[j] for j in range(BW // LANES)], axis=1)


def _cls_view(t, dil):
    return t if dil == 1 else t.reshape(dil, t.shape[0] // dil, t.shape[1])


def _cls_kind(dil, off=0):
    return ("off", off, None, 0) if dil == 1 else ("cls", dil, off)


def _cls_out(S, dil, dtype):
    return ((S, BW) if dil == 1 else (dil, S // dil, BW), dtype, _cls_kind(dil))


def _flat(t):
    return t.reshape(-1, t.shape[-1])


def _qkv_fwd(name, z, cos, sin, ones_bd, qg, kg, tm=512):
    S = z.shape[0]
    nd = len(DILATIONS)

    def body(q_ref, k_ref, v_ref, c_ref, s_ref, o_ref, qg_ref, kg_ref, *rest):
        outs, stage = rest[:3 * nd], rest[3 * nd]
        c, s, ob = _tile4(c_ref[...]), _tile4(s_ref[...]), o_ref[...]

        def norm_rope(x, g):
            r = lax.rsqrt(_gsum64(x * x, ob) * (1.0 / HEAD) + EPS)
            xn = x * r * g
            return xn * c + _swap32(xn) * s

        vals = [norm_rope(q_ref[...], qg_ref[...]) * QK_SCALE, norm_rope(k_ref[...], kg_ref[...]), v_ref[...]]
        for a, val in enumerate(vals):
            for b, dil in enumerate(DILATIONS):
                if dil == 1:
                    outs[nd * a + b][...] = val.astype(BF16)
                else:
                    for r, rows in enumerate(_to_classes(stage, val, dil)):
                        outs[nd * a + b][r] = rows.astype(BF16)

    outs = _rowcall(name, body, S, tm,
                    [(z, ("row", BW, 2)), (z, ("row", BW, 3)), (z, ("row", BW, 4)), (cos, ROW), (sin, ROW),
                     (ones_bd, FULL), (qg, FULL), (kg, FULL)],
                    [_cls_out(S, dil, BF16) for _ in range(3) for dil in DILATIONS], scratch=[_stage_spec(tm)])
    return [[_flat(outs[nd * a + b]) for a in range(3)] for b in range(nd)]


PAIR = 2 * HEAD


ATT_BLOCKS = 4
ATT_TM = ATT_BLOCKS * QB
ATT_PREV = ("prev", QB, None, 0)


def _key_rows(prev_ref, cur_ref, sb, ps):
    before = prev_ref[:, ps] if sb == 0 else cur_ref[(sb - 1) * QB:sb * QB, ps]
    return jnp.concatenate([before, cur_ref[sb * QB:(sb + 1) * QB, ps]], axis=0)


def _pair_scores(q_ref, kp_ref, kc_ref, sb, hp, half, seg_blocks):
    ps = slice(hp * PAIR, (hp + 1) * PAIR)
    mine = (lax.broadcasted_iota(jnp.int32, (1, PAIR), 1) >= HEAD) == (half == 1)
    qm = jnp.where(mine, q_ref[sb * QB:(sb + 1) * QB, ps], jnp.zeros((), BF16))
    kcat = _key_rows(kp_ref, kc_ref, sb, ps)
    s = lax.dot_general(qm, kcat, _DN["nt"], preferred_element_type=F32)
    qi = lax.broadcasted_iota(jnp.int32, (QB, 2 * QB), 0)
    kj = lax.broadcasted_iota(jnp.int32, (QB, 2 * QB), 1)
    has_prev = ((pl.program_id(0) * ATT_BLOCKS + sb) % seg_blocks) != 0
    valid = (kj >= qi) & (kj <= qi + QB) & ((kj >= QB) | has_prev)
    return mine, qm, kcat, s, valid


def _attn_fwd(name, q, k, v, dil):
    S = q.shape[0]
    seg_blocks = S // dil // QB

    def body(q_ref, kp_ref, kc_ref, vp_ref, vc_ref, o_ref, l_ref):
        for hp in range(N_HEADS // 2):
            ps = slice(hp * PAIR, (hp + 1) * PAIR)
            chains = [(sb, half) for sb in range(ATT_BLOCKS) for half in range(2)]
            sc = [_pair_scores(q_ref, kp_ref, kc_ref, sb, hp, half, seg_blocks) for sb, half in chains]
            ss = [jnp.where(valid, s, NEG) for _, _, _, s, valid in sc]
            ms = [jnp.max(s, axis=-1, keepdims=True) for s in ss]
            pv = [jnp.exp(s - m) for s, m in zip(ss, ms)]
            dens = [jnp.sum(p, axis=-1, keepdims=True) for p in pv]
            vcats = [_key_rows(vp_ref, vc_ref, sb, ps) for sb in range(ATT_BLOCKS)]
            outs = [jnp.dot(p.astype(BF16), vcats[sb], preferred_element_type=F32) / den
                    for p, den, (sb, _) in zip(pv, dens, chains)]
            lses = [jnp.broadcast_to(m + jnp.log(den), (QB, PAIR)) for m, den in zip(ms, dens)]
            for sb in range(ATT_BLOCKS):
                rows, upper = slice(sb * QB, (sb + 1) * QB), sc[2 * sb + 1][0]
                o_ref[rows, ps] = jnp.where(upper, outs[2 * sb + 1], outs[2 * sb])
                l_ref[rows, ps] = jnp.where(upper, lses[2 * sb + 1], lses[2 * sb])

    return _rowcall(name, body, S, ATT_TM, [(q, ROW), (k, ATT_PREV), (k, ROW), (v, ATT_PREV), (v, ROW)],
                    [((S, BW), F32, ROW)] * 2)


def _attn_bwd(name, q, k, v, do, lse, delta, dil):
    S = q.shape[0]
    seg_blocks = S // dil // QB

    def body(q_ref, kp_ref, kc_ref, vp_ref, vc_ref, do_ref, l_ref, dl_ref, dq_ref, dkc_ref, dkp_ref, dvc_ref, dvp_ref):
        for hp in range(N_HEADS // 2):
            ps = slice(hp * PAIR, (hp + 1) * PAIR)
            chains = [(sb, half) for sb in range(ATT_BLOCKS) for half in range(2)]
            rows = [slice(sb * QB, (sb + 1) * QB) for sb, _ in chains]
            cols = [hp * PAIR + half * HEAD for _, half in chains]
            sc = [_pair_scores(q_ref, kp_ref, kc_ref, sb, hp, half, seg_blocks) for sb, half in chains]
            pv = [jnp.where(valid, jnp.exp(s - l_ref[r, c:c + 1]), 0.0) for (_, _, _, s, valid), r, c in zip(sc, rows, cols)]
            vcats = [_key_rows(vp_ref, vc_ref, sb, ps) for sb in range(ATT_BLOCKS)]
            doms = [jnp.where(mine, do_ref[r, ps].astype(BF16), jnp.zeros((), BF16)) for (mine, *_), r in zip(sc, rows)]
            dps = [lax.dot_general(dom, vcats[sb], _DN["nt"], preferred_element_type=F32) for dom, (sb, _) in zip(doms, chains)]
            dss = [(p * (dp - dl_ref[r, c:c + 1])).astype(BF16) for p, dp, r, c in zip(pv, dps, rows, cols)]
            dqs = [jnp.dot(ds, kcat, preferred_element_type=F32) for ds, (_, _, kcat, _, _) in zip(dss, sc)]
            dks = [lax.dot_general(ds, qm, _DN["tn"], preferred_element_type=F32) for ds, (_, qm, *_) in zip(dss, sc)]
            dvs = [lax.dot_general(p.astype(BF16), dom, _DN["tn"], preferred_element_type=F32) for p, dom in zip(pv, doms)]
            for sb in range(ATT_BLOCKS):
                lo, hi = 2 * sb, 2 * sb + 1
                dk, dv = dks[lo] + dks[hi], dvs[lo] + dvs[hi]
                dq_ref[rows[lo], ps] = jnp.where(sc[hi][0], dqs[hi], dqs[lo])
                dkp_ref[rows[lo], ps] = dk[:QB]
                dkc_ref[rows[lo], ps] = dk[QB:]
                dvp_ref[rows[lo], ps] = dv[:QB]
                dvc_ref[rows[lo], ps] = dv[QB:]

    return _rowcall(name, body, S, ATT_TM,
                    [(q, ROW), (k, ATT_PREV), (k, ROW), (v, ATT_PREV), (v, ROW), do, (lse, ROW), (delta, ROW)],
                    [((S, BW), F32, ROW)] * 5)


def _merge_fwd(name, branches, tm=512):
    S = branches[0][0].shape[0]
    nd = len(DILATIONS)

    def body(*refs):
        ins, (y_ref, yb_ref), l_refs, stage = refs[:2 * nd], refs[2 * nd:2 * nd + 2], refs[2 * nd + 2:3 * nd + 2], refs[-1]
        os_, ls = [], []
        for b, dil in enumerate(DILATIONS):
            o, l = ins[2 * b][...], ins[2 * b + 1][...]
            os_.append(o if dil == 1 else _from_classes(stage, o, dil))
            ls.append(l if dil == 1 else _from_classes(stage, l, dil))
        m = functools.reduce(jnp.maximum, ls)
        es = [jnp.exp(l - m) for l in ls]
        den = functools.reduce(lambda a, e: a + e, es)
        y = functools.reduce(lambda a, t: a + t, [e * o for e, o in zip(es, os_)]) / den
        y_ref[...] = y
        yb_ref[...] = y.astype(BF16)
        lse = m + jnp.log(den)
        for b, dil in enumerate(DILATIONS):
            if dil == 1:
                l_refs[b][...] = lse
            else:
                for r, rows in enumerate(_to_classes(stage, lse, dil)):
                    l_refs[b][r] = rows

    ins = [(_cls_view(t, dil), _cls_kind(dil)) for pair, dil in zip(branches, DILATIONS) for t in pair]
    outs = _rowcall(name, body, S, tm, ins,
                    [((S, BW), F32, ROW), ((S, BW), BF16, ROW)] + [_cls_out(S, dil, F32) for dil in DILATIONS],
                    scratch=[_stage_spec(tm)])
    return outs[0], outs[1], [_flat(t) for t in outs[2:]]


def _delta(name, dyab, yb, ones_bd, tm=512):
    S = yb.shape[0]
    nd = len(DILATIONS)

    def body(d_ref, y_ref, o_ref, *rest):
        dl_refs, do_refs, stage = rest[:nd], rest[nd:2 * nd - 1], rest[-1]
        d = d_ref[...]
        dl = _gsum64(d * y_ref[...], o_ref[...])
        for b, dil in enumerate(DILATIONS):
            if dil == 1:
                dl_refs[b][...] = dl
            else:
                for r, rows in enumerate(_to_classes(stage, dl, dil)):
                    dl_refs[b][r] = rows
                for r, rows in enumerate(_to_classes(stage, d, dil)):
                    do_refs[b - 1][r] = rows.astype(BF16)

    outs = _rowcall(name, body, S, tm, [(dyab, ("row", BW, 1)), (yb, ROW), (ones_bd, FULL)],
                    [_cls_out(S, dil, F32) for dil in DILATIONS] + [_cls_out(S, dil, BF16) for dil in DILATIONS[1:]],
                    scratch=[_stage_spec(tm)])
    return [_flat(t) for t in outs[:nd]], [_flat(t) for t in outs[nd:]]


def _qkv_bwd(name, z, cos, sin, ones_bd, qg, kg, pieces):
    S = z.shape[0]
    nblk = S // QB

    def body(q_ref, k_ref, c_ref, s_ref, o_ref, qg_ref, kg_ref, *rest):
        pr, (dz_ref, dqg_ref, dkg_ref), stage = rest[:15], rest[15:18], rest[18]
        i = pl.program_id(0)
        c, s, ob = _tile4(c_ref[...]), _tile4(s_ref[...]), o_ref[...]
        dq = dk = dv = None
        for b, dil in enumerate(DILATIONS):
            a_q, a_kc, a_kp, a_vc, a_vp = [r[...] for r in pr[5 * b:5 * b + 5]]
            live = ((i + dil) < nblk).astype(F32)
            tq, tk, tv = a_q, a_kc + a_kp * live, a_vc + a_vp * live
            if dil > 1:
                tq, tk, tv = (_from_classes(stage, t, dil) for t in (tq, tk, tv))
            dq, dk, dv = (tq, tk, tv) if b == 0 else (dq + tq, dk + tk, dv + tv)

        def back(x, g, d_rot, acc_ref):
            r = lax.rsqrt(_gsum64(x * x, ob) * (1.0 / HEAD) + EPS)
            n = x * r
            dxn = d_rot * c + _swap32(d_rot * s)
            _acc_add(i, acc_ref, _colsum(dxn * n))
            dn = dxn * g
            return r * (dn - n * (_gsum64(dn * n, ob) * (1.0 / HEAD)))

        dz_ref[:, :BW] = back(q_ref[...], qg_ref[...], dq * QK_SCALE, dqg_ref).astype(BF16)
        dz_ref[:, BW:2 * BW] = back(k_ref[...], kg_ref[...], dk, dkg_ref).astype(BF16)
        dz_ref[:, 2 * BW:] = dv.astype(BF16)

    ins = [(z, ("row", BW, 2)), (z, ("row", BW, 3)), (cos, ROW), (sin, ROW), (ones_bd, FULL), (qg, FULL), (kg, FULL)]
    for piece, dil in zip(pieces, DILATIONS):
        a_q, a_kc, a_kp, a_vc, a_vp = (_cls_view(t, dil) for t in piece)
        own, prev = _cls_kind(dil), _cls_kind(dil, dil)
        ins += [(a_q, own), (a_kc, own), (a_kp, prev), (a_vc, own), (a_vp, prev)]
    return _rowcall(name, body, S, QB, ins,
                    [((S, 3 * BW), BF16, ROW), ((1, BW), F32, ACC), ((1, BW), F32, ACC)], scratch=[_stage_spec(QB)])


def _local_step(x0, tgt, pos, mod, wb, sp, pipe):
    S, D = x0.shape
    md = lambda l, j: mod[l, j:j + 1]
    sh_m, sc_m, g_m, sh_f, sc_f, g_f = ([md(l, j) for l in range(2)] for j in range(6))
    nm_g, nf_g = sp["norm_mix_g"], sp["norm_ffn_g"]

    inv_freq = 1.0 / (ROPE_THETA ** (jnp.arange(0, HEAD, 2, dtype=F32) / HEAD))
    ang = pos.astype(F32)[:, None] * inv_freq
    cs, sn = jnp.cos(ang), jnp.sin(ang)
    cos = jnp.concatenate([cs, cs, cs, cs], axis=1)
    sin = jnp.concatenate([-sn, sn, -sn, sn], axis=1)
    head_of = jnp.arange(BW) // HEAD
    ones_bd = (head_of[:, None] == head_of[None, :]).astype(BF16)
    qg = jnp.tile(sp["b_q_norm_g"].reshape(1, HEAD), (1, N_HEADS))
    kg = jnp.tile(sp["b_k_norm_g"].reshape(1, HEAD), (1, N_HEADS))
    vg = sp["a_vnorm_g"].reshape(1, A_GROUPS * LANES)
    ws = sp["a_spatial_w"][0]
    bias_full = jnp.repeat(sp["a_spatial_b"][0].T, LANES, axis=1)
    ffn_s = [(sp["ffn_dw_w"][l], sp["ffn_dw_b"][l:l + 1]) for l in range(2)]

    h0 = _mod_first("l0_mod", x0, nm_g[0:1], sc_m[0], sh_m[0])
    z = _matmul("l0_in", h0, wb.get("w_in", h0), "nn", F32, tm=2048)
    ya = _mixa_fwd("l0_mixa", z, vg, ws, bias_full)
    qkv = _qkv_fwd("l0_qkv", z, cos, sin, ones_bd, qg, kg)
    branches = [_attn_fwd(f"l0_att{dil}", *qkv[b], dil) for b, dil in enumerate(DILATIONS)]
    yb, yb16, lses = _merge_fwd("l0_merge", branches)
    yab = jnp.concatenate([ya, yb16], axis=1)
    y0 = _matmul("l0_out", yab, wb.get("w_out", yab), "nn", F32, tm=1024, tn=1024)
    x1, h1 = _resid_mod("l0_res1", x0, y0, g_m[0], nf_g[0:1], sc_f[0], sh_f[0])
    ffn_w = [(wb.get("up0", h1), wb.get("dn0", h1), *ffn_s[0])]
    f0, saved0 = _ffn_fwd("l0_ffn", h1, *ffn_w[0])
    x2, h2 = _resid_mod("l0_res2", x1, f0, g_f[0], nm_g[1:2], sc_m[1], sh_m[1])
    p = _matmul("l1_pw1", h2, wb.get("pw1", h2), "nn", F32, tm=2048, bias=sp["conv_pw1_b"])
    y2 = _glu31_fwd("l1_glu", p, sp["conv_dw_w"][0], sp["conv_dw_b"])
    y4 = _ln_silu_fwd("l1_ln", y2, sp["conv_ln_g"], sp["conv_ln_b"])
    y1 = _matmul("l1_pw2", y4, wb.get("pw2", y4), "nn", F32, tm=1024, tn=1024, bias=sp["conv_pw2_b"])
    x3, h3 = _resid_mod("l1_res1", x2, y1, g_m[1], nf_g[1:2], sc_f[1], sh_f[1])
    ffn_w.append((wb.get("up1", h3), wb.get("dn1", h3), *ffn_s[1]))
    f1, saved1 = _ffn_fwd("l1_ffn", h3, *ffn_w[1])
    dx4, lossv, dy, dgate_f1, _ = _loss_head("loss", x3, f1, g_f[1], tgt)

    dh, gf1 = _ffn_bwd("l1_ffn", dy, h3, *saved1, *ffn_w[1])
    tok = pipe.scatter("g1", dict(dn1=gf1["dn"], up1=gf1["up"]))
    dx3, dsh_f1, dsc_f1, dnf1, dy, dgate_m1, dpw2_b = _mod_bwd("l1_dmod2", dx4, dh, x3, nf_g[1:2], sc_f[1] + tok,
                                                             y1, g_m[1])
    dy4 = _matmul("l1_dpw2_x", dy, wb.get("pw2"), "nt", F32, tm=1024, tn=1024)
    g_pw2 = _matmul("l1_dpw2_w", y4, dy, "tn", BF16, tm=1024, tn=1024, tk=2048)
    dy2, dln_g, dln_b = _ln_silu_bwd("l1_dln", y2, dy4, sp["conv_ln_g"], sp["conv_ln_b"])
    dp, ddw_w, ddw_b, dpw1_b = _glu31_bwd("l1_dglu", p, dy2, sp["conv_dw_w"][0])
    g_pw1 = _matmul("l1_dpw1_w", h2, dp, "tn", BF16, tm=1024, tn=1024, tk=2048)
    tok = pipe.scatter("g2", dict(pw2=g_pw2, pw1=g_pw1))
    pipe.collect("g1", g_pw1)
    dh = _matmul("l1_dpw1_x", dp, wb.get("pw1"), "nt", F32, tm=1024, tn=1024, tk=2048)
    dx2, dsh_m1, dsc_m1, dnm1, dy, dgate_f0, _ = _mod_bwd("l1_dmod1", dx3, dh, x2, nm_g[1:2], sc_m[1] + tok,
                                                        f0, g_f[0])
    dh, gf0 = _ffn_bwd("l0_ffn", dy, h1, *saved0, *ffn_w[0])
    tok = pipe.scatter("g3", dict(dn0=gf0["dn"], up0=gf0["up"]))
    pipe.collect("g2", gf0["up"])
    dx1, dsh_f0, dsc_f0, dnf0, dy, dgate_m0, _ = _mod_bwd("l0_dmod2", dx2, dh, x1, nf_g[0:1], sc_f[0] + tok,
                                                        y0, g_m[0])
    dyab = _matmul("l0_dout_x", dy, wb.get("w_out"), "nt", F32, tm=1024, tn=1024)
    g_out = _matmul("l0_dout_w", yab, dy, "tn", BF16, tm=1024, tn=1024, tk=2048)
    vg = vg + pipe.scatter("g4", dict(w_out=g_out))
    dza, dws, dbf, dvg = _mixa_bwd("l0_dmixa", z, dyab, vg, ws, bias_full)
    deltas, dos = _delta("l0_delta", dyab, yb, ones_bd)
    pieces = []
    for b, dil in enumerate(DILATIONS):
        do = (dyab, ("row", BW, 1)) if dil == 1 else (dos[b - 1], ROW)
        pieces.append(_attn_bwd(f"l0_datt{dil}", *qkv[b], do, lses[b], deltas[b], dil))
    dzb, dqg, dkg = _qkv_bwd("l0_dqkv", z, cos, sin, ones_bd, qg, kg, pieces)
    small_tok = _start_small_grads(pipe, dzb, sp, lossv, ((dgate_m0, dsh_f0, dsc_f0, dgate_f0),
                                                   (dsh_m1, dsc_m1, dgate_m1, dsh_f1, dsc_f1, dgate_f1)),
                                   (dnm1, dnf0, dnf1), dvg, dws, dbf, dqg, dkg, dpw1_b, ddw_w, ddw_b, dln_g, dln_b,
                                   dpw2_b, gf0, gf1)
    dz = jnp.concatenate([dza, dzb], axis=1)
    g_in = _matmul("l0_din_w", h0, dz, "tn", BF16, tm=1024, tn=1280, tk=2048, after=small_tok)
    tok = pipe.scatter("g5", dict(w_in=g_in))
    dh = _matmul("l0_din_x", dz, wb.get("w_in"), "nt", F32, tm=1024, tn=1024, tk=2560, after=small_tok)
    grad_x, dsh_m0, dsc_m0, dnm0 = _mod_bwd("l0_dmod1", dx1, dh, x0, nm_g[0:1], sc_m[0] + tok)
    return lossv, grad_x, (dsh_m0, dsc_m0, dnm0)


def _start_small_grads(pipe, after, sp, lossv, mods, norms, dvg, dws, dbf, dqg, dkg, dpw1_b, ddw_w, ddw_b, dln_g,
                       dln_b, dpw2_b, gf0, gf1):
    (dgate_m0, dsh_f0, dsc_f0, dgate_f0), (dsh_m1, dsc_m1, dgate_m1, dsh_f1, dsc_f1, dgate_f1) = mods
    dnm1, dnf0, dnf1 = norms
    zero = jnp.zeros_like(dnm1)
    dmod = jnp.stack([jnp.concatenate([zero, zero, dgate_m0, dsh_f0, dsc_f0, dgate_f0], axis=0),
                      jnp.concatenate([dsh_m1, dsc_m1, dgate_m1, dsh_f1, dsc_f1, dgate_f1], axis=0)])
    small = dict(
        norm_mix_g=jnp.concatenate([zero, dnm1], axis=0),
        norm_ffn_g=jnp.concatenate([dnf0, dnf1], axis=0),
        a_vnorm_g=dvg.reshape(1, A_GROUPS, LANES),
        a_spatial_w=dws[None],
        a_spatial_b=dbf.reshape(CHUNK, A_GROUPS, LANES).sum(-1).T[None],
        b_q_norm_g=dqg.reshape(N_HEADS, HEAD).sum(0)[None],
        b_k_norm_g=dkg.reshape(N_HEADS, HEAD).sum(0)[None],
        conv_pw1_b=dpw1_b, conv_dw_w=ddw_w[None, :sp["conv_dw_w"].shape[1]], conv_dw_b=ddw_b,
        conv_ln_g=dln_g, conv_ln_b=dln_b, conv_pw2_b=dpw2_b,
        ffn_dw_w=jnp.stack([gf0["dw_w"], gf1["dw_w"]]),
        ffn_dw_b=jnp.concatenate([gf0["dw_b"], gf1["dw_b"]], axis=0),
    )
    return pipe.start_small(dmod, lossv, small, after)


ADA_TN = 512


def _ada_fwd(name, c_all, ada_w, ada_b_sh):
    L, D, N = ada_w.shape
    B = c_all.shape[0]

    def body(c_ref, w_ref, b_ref, o_ref):
        cv = c_ref[...]
        ca = (cv * _sigmoid(cv)).astype(BF16)
        o_ref[0] = jnp.dot(ca, w_ref[0].astype(BF16), preferred_element_type=F32) + b_ref[0]

    return pl.pallas_call(
        body, name=name, grid=(L, N // ADA_TN),
        in_specs=[pl.BlockSpec((B, D), lambda l, j: (0, 0)), pl.BlockSpec((1, D, ADA_TN), lambda l, j: (l, 0, j)),
                  pl.BlockSpec((1, 1, ADA_TN), lambda l, j: (l, 0, j))],
        out_specs=pl.BlockSpec((1, B, ADA_TN), lambda l, j: (l, 0, j)),
        out_shape=jax.ShapeDtypeStruct((L, B, N), F32),
        compiler_params=_params(("parallel", "parallel")),
    )(c_all, ada_w, ada_b_sh.reshape(L, 1, N))


def _adamw_val(w, g, m, v):
    m2 = ADAM_B1 * m + (1.0 - ADAM_B1) * g
    v2 = ADAM_B2 * v + (1.0 - ADAM_B2) * (g * g)
    m_hat = m2 / (1.0 - ADAM_B1 ** ADAM_STEP)
    v_hat = v2 / (1.0 - ADAM_B2 ** ADAM_STEP)
    delta = -ADAM_LR * (m_hat / (jnp.sqrt(v_hat) + ADAM_EPS) + ADAM_WD * w)
    return delta, m2, v2


def _ada_update(name, c_all, dmod_sh, w, m, v):
    L, D, N = w.shape
    B = c_all.shape[0]

    def body(c_ref, d_ref, w_ref, m_ref, v_ref, g_ref, dl_ref, mo_ref, vo_ref):
        cv = c_ref[...]
        ca = (cv * _sigmoid(cv)).astype(BF16)
        g = lax.dot_general(ca, d_ref[0].astype(BF16), _DN["tn"], preferred_element_type=F32)
        g_ref[0] = g
        dl_ref[0], mo_ref[0], vo_ref[0] = _adamw_val(w_ref[0], g, m_ref[0], v_ref[0])

    wspec = pl.BlockSpec((1, D, ADA_TN), lambda l, j: (l, 0, j))
    return pl.pallas_call(
        body, name=name, grid=(L, N // ADA_TN),
        in_specs=[pl.BlockSpec((B, D), lambda l, j: (0, 0)), pl.BlockSpec((1, B, ADA_TN), lambda l, j: (l, 0, j)),
                  wspec, wspec, wspec],
        out_specs=[wspec] * 4, out_shape=[jax.ShapeDtypeStruct((L, D, N), F32)] * 4,
        compiler_params=_params(("parallel", "parallel")),
    )(c_all, dmod_sh, w, m, v)


def _adamw(name, w, g, m, v):
    R, C = w.shape
    tm = R
    for cand in (256, 128, 64, 32, 16, 8):
        if R % cand == 0 and cand * C * 4 <= (1 << 20):
            tm = cand
            break

    def body(w_ref, g_ref, m_ref, v_ref, d_ref, mo_ref, vo_ref):
        d_ref[...], mo_ref[...], vo_ref[...] = _adamw_val(w_ref[...], g_ref[...], m_ref[...], v_ref[...])

    return _rowcall(name, body, R, tm, [(w, ROW), (g, ROW), (m, ROW), (v, ROW)], [((R, C), F32, ROW)] * 3)


def _row_tile(rows, width, itemsize=4, limit=1 << 20):
    for cand in range(512, 0, -16):
        if rows % cand == 0 and cand * width * itemsize <= limit:
            return cand
    raise ValueError((rows, width))


def _cast_into_full(name, a, layer, q, kind, after):
    L, r, c = a.shape
    tm = _row_tile(r, c, limit=2 << 20)
    if kind == "col":
        full, o_spec = (r, N_CHIPS * c), pl.BlockSpec((tm, c), lambda i, q_ref: (i, q_ref[0]))
    else:
        full, o_spec = (N_CHIPS * r, c), pl.BlockSpec((tm, c), lambda i, q_ref: (q_ref[0] * (r // tm) + i, 0))

    def body(q_ref, a_ref, after_ref, o_ref):
        o_ref[...] = a_ref[0].astype(BF16)

    return pl.pallas_call(
        body, name=name,
        grid_spec=pltpu.PrefetchScalarGridSpec(
            num_scalar_prefetch=1, grid=(r // tm,),
            in_specs=[pl.BlockSpec((1, tm, c), lambda i, q_ref: (layer, i, 0)), ANY], out_specs=o_spec),
        out_shape=jax.ShapeDtypeStruct(full, BF16), compiler_params=_params(("parallel",)),
    )(q.reshape(1).astype(jnp.int32), a, after)


def _sum4(name, g, rcv, q, kind, n):
    r, c = rcv.shape[1:]
    tm = _row_tile(r, c, limit=2 << 20)
    if kind == "col":
        g_spec = pl.BlockSpec((tm, n), lambda i, q_ref: (i, q_ref[0]))
    else:
        g_spec = pl.BlockSpec((tm, c), lambda i, q_ref: (q_ref[0] * (n // tm) + i, 0))

    def body(q_ref, g_ref, r_ref, o_ref):
        acc = g_ref[...].astype(F32)
        for j in range(3):
            acc = acc + r_ref[j].astype(F32)
        o_ref[...] = acc

    return pl.pallas_call(
        body, name=name,
        grid_spec=pltpu.PrefetchScalarGridSpec(
            num_scalar_prefetch=1, grid=(r // tm,),
            in_specs=[g_spec, pl.BlockSpec((3, tm, c), lambda i, q_ref: (0, i, 0))],
            out_specs=pl.BlockSpec((tm, c), lambda i, q_ref: (i, 0))),
        out_shape=jax.ShapeDtypeStruct((r, c), F32), compiler_params=_params(("parallel",)),
    )(q.reshape(1).astype(jnp.int32), g, rcv)


def _adamw_sum(name, w, m, v, layer, mine, theirs, prev):
    L, r, c = w.shape
    tm = _row_tile(r, c, limit=3 << 19)
    lay = pl.BlockSpec((1, tm, c), lambda i: (layer, i, 0))
    flat = pl.BlockSpec((tm, c), lambda i: (i, 0))
    n_prev = 0 if prev is None else 4

    def body(w_ref, m_ref, v_ref, a_ref, b_ref, *rest):
        g_ref, d_ref, mo_ref, vo_ref = rest[n_prev:]
        g = a_ref[...] + b_ref[...]
        g_ref[0] = g
        d_ref[0], mo_ref[0], vo_ref[0] = _adamw_val(w_ref[0], g, m_ref[0], v_ref[0])

    return pl.pallas_call(
        body, name=name, grid=(r // tm,),
        in_specs=[lay, lay, lay, flat, flat] + [ANY] * n_prev, out_specs=[lay] * 4,
        out_shape=[jax.ShapeDtypeStruct((L, r, c), F32)] * 4,
        input_output_aliases={5 + k: k for k in range(n_prev)},
        compiler_params=_params(("parallel",)),
    )(w, m, v, mine, theirs, *(prev or ()))


def _sum8(name, gathered, own=None):
    R, N = gathered.shape
    P = R // 8

    def body(g_ref, *rest):
        o_ref = rest[-1]
        me = 4 * lax.axis_index("x") + 2 * lax.axis_index("y") + lax.axis_index("c")
        acc = None
        for d in range(8):
            blk = g_ref[d * P:(d + 1) * P, :]
            if own is not None:
                blk = jnp.where(me == d, rest[0][...], blk)
            acc = blk if d == 0 else acc + blk
        o_ref[...] = acc

    return pl.pallas_call(body, name=name, out_shape=jax.ShapeDtypeStruct((P, N), F32),
                          compiler_params=pltpu.CompilerParams(vmem_limit_bytes=VMEM_LIMIT),
                          )(gathered, *(() if own is None else (own,)))


ANY = pl.BlockSpec(memory_space=pl.ANY)


def _mesh_pos():
    x, y, c = lax.axis_index("x"), lax.axis_index("y"), lax.axis_index("c")
    other_chips = [(1 - x, y), (x, 1 - y), (1 - x, 1 - y)]
    return x, y, c, other_chips


def _allgather8(name, blk, after=None):
    m_per, n = blk.shape

    def body(x_ref, *rest):
        out_ref, send_sems, recv_sems, local_sem = rest[after is not None:]
        x, y, c, chips = _mesh_pos()
        me, sibling = (x, y, c), (x, y, 1 - c)

        def rows(px, py, pc):
            return out_ref.at[pl.ds((4 * px + 2 * py + pc) * m_per, m_per), :]

        def copy(k, block, to, src=None):
            return pltpu.make_async_remote_copy(
                src_ref=rows(*block) if src is None else src, dst_ref=rows(*block),
                send_sem=send_sems.at[k], recv_sem=recv_sems.at[k], device_id=to, device_id_type=MESH)

        mine = pltpu.make_async_copy(x_ref, rows(*me), local_sem)
        mine.start()
        first = [copy(0, me, sibling, src=x_ref)]
        first += [copy(1 + j, me, (*chip, c), src=x_ref) for j, chip in enumerate(chips)]
        for cp in first:
            cp.start()
        passed = [copy(4 + j, (*chip, c), sibling) for j, chip in enumerate(chips)]
        for j, chip in enumerate(chips):
            copy(1 + j, (*chip, c), me).wait_recv()
            passed[j].start()
        copy(0, sibling, me).wait_recv()
        for j, chip in enumerate(chips):
            copy(4 + j, (*chip, 1 - c), me).wait_recv()
        for cp in first + passed:
            cp.wait_send()
        mine.wait()

    return pl.pallas_call(
        body, name=name, out_shape=jax.ShapeDtypeStruct((8 * m_per, n), blk.dtype),
        in_specs=[pl.BlockSpec(memory_space=pltpu.VMEM)] + [ANY] * (after is not None),
        out_specs=pl.BlockSpec(memory_space=pltpu.VMEM),
        scratch_shapes=[pltpu.SemaphoreType.DMA((7,)), pltpu.SemaphoreType.DMA((7,)), pltpu.SemaphoreType.DMA],
        compiler_params=pltpu.CompilerParams(vmem_limit_bytes=VMEM_LIMIT),
    )(blk, *(() if after is None else (after,)))


BIG = dict(w_in=("col", "ab_w_in", 0), w_out=("row", "ab_w_out", 0), up0=("col", "ffn_up_w", 0),
           dn0=("row", "ffn_down_w", 0), pw1=("col", "conv_pw1_w", 0), pw2=("row", "conv_pw2_w", 0),
           up1=("col", "ffn_up_w", 1), dn1=("row", "ffn_down_w", 1))
N_CHIPS = 4
HBM = pl.BlockSpec(memory_space=pltpu.HBM)
SEM = pl.BlockSpec(memory_space=pltpu.SEMAPHORE)
EFFECT = pltpu.SideEffectType.DATAFLOW_SIDE_EFFECTING


def _region(kind, ref, q, n):
    if kind == "col":
        return ref.at[:, pl.ds(q * n, n)]
    return ref.at[pl.ds(q * n, n), :]


def _gather_plan(kind, n):
    def remote(src, land, pos):
        x, y, c, chips = pos
        mine = _region(kind, land, 2 * x + y, n)
        return [(mine, mine, (*chip, c)) for chip in chips]

    return ("gather", kind, n), remote


def _scatter_plan(kind, n):
    def remote(src, land, pos):
        _, _, c, chips = pos
        return [(_region(kind, src, 2 * chip[0] + chip[1], n), land.at[j], (*chip, c)) for j, chip in enumerate(chips)]

    return ("scatter", kind, n), remote


def _everyone_plan(rows):
    def remote(src, land, pos):
        x, y, c, _ = pos
        mine = land.at[pl.ds((4 * x + 2 * y + c) * rows, rows), :]
        flip = lambda v, bit: 1 - v if bit else v
        return [(src, mine, (flip(x, k & 4), flip(y, k & 2), flip(c, k & 1))) for k in range(1, 8)]

    return ("everyone", rows), remote


def _sibling_plan():
    def remote(src, land, pos):
        x, y, c, _ = pos
        return [(src, land, (x, y, 1 - c))]

    return ("sibling",), remote


def _split_start(name, items, after=None):
    n = len(items)
    plans = [it[2] for it in items]
    n_in = 2 * n + (after is not None)

    def body(*refs):
        srcs, lands = refs[:n], refs[n:2 * n]
        sends, recvs = refs[n_in:n_in + n], refs[n_in + n:n_in + 2 * n]
        token = refs[n_in + 4 * n]
        pos = _mesh_pos()
        for a, (_, remote) in enumerate(plans):
            for k, (s, d, dev) in enumerate(remote(srcs[a], lands[a], pos)):
                pltpu.make_async_remote_copy(src_ref=s, dst_ref=d, send_sem=sends[a].at[k], recv_sem=recvs[a].at[k],
                                             device_id=dev, device_id_type=MESH).start()
        token[...] = jnp.zeros_like(token)

    sems = [pltpu.SemaphoreType.DMA((it[3],)) for it in items]
    bufs = [pltpu.HBM(it[k].shape, it[k].dtype) for k in (0, 1) for it in items]
    outs = pl.pallas_call(
        body, name=name, out_shape=[*sems, *sems, *bufs, jax.ShapeDtypeStruct((8, LANES), F32)],
        in_specs=[HBM] * (2 * n) + [ANY] * (n_in - 2 * n),
        out_specs=[SEM] * (2 * n) + [HBM] * (2 * n) + [pl.BlockSpec(memory_space=pltpu.VMEM)],
        input_output_aliases={i: 2 * n + i for i in range(2 * n)},
        compiler_params=pltpu.CompilerParams(has_side_effects=EFFECT),
    )(*[pltpu.with_memory_space_constraint(it[k], pltpu.HBM) for k in (0, 1) for it in items],
      *(() if after is None else (after,)))
    state = [(items[a][2], items[a][3], outs[2 * n + a], outs[3 * n + a], outs[a], outs[n + a]) for a in range(n)]
    return state, outs[4 * n]


def _split_wait(name, state, after):
    n = len(state)

    def body(*refs):
        srcs, lands = refs[:n], refs[n:2 * n]
        sends, recvs = refs[2 * n:3 * n], refs[3 * n:4 * n]
        pos = _mesh_pos()
        for a, ((_, remote), *_) in enumerate(state):
            for k, (s, d, dev) in enumerate(remote(srcs[a], lands[a], pos)):
                cp = pltpu.make_async_remote_copy(src_ref=s, dst_ref=d, send_sem=sends[a].at[k], recv_sem=recvs[a].at[k],
                                                  device_id=dev, device_id_type=MESH)
                cp.wait_send()
                cp.wait_recv()

    bufs = [st[k] for k in (2, 3) for st in state]
    outs = pl.pallas_call(
        body, name=name, out_shape=[pltpu.HBM(b.shape, b.dtype) for b in bufs],
        in_specs=[HBM] * (2 * n) + [SEM] * (2 * n) + [ANY], out_specs=[HBM] * (2 * n),
        input_output_aliases={i: i for i in range(2 * n)},
        compiler_params=pltpu.CompilerParams(has_side_effects=EFFECT),
    )(*bufs, *[st[k] for k in (4, 5) for st in state], after)
    return outs[:n], outs[n:]


class _Weights:
    def __init__(self, w, q, after):
        unused = jnp.zeros((16, LANES), BF16)

        def item(name, after):
            kind, pname, layer = BIG[name]
            _, r, c = w[pname].shape
            land = _cast_into_full(f"cast_{name}", w[pname], layer, q, kind, after)
            return unused, land, _gather_plan(kind, c if kind == "col" else r), N_CHIPS - 1

        first, *rest = BIG
        state1, token1 = _split_start("gw_start_first", [item(first, after)], after)
        state2, self.token = _split_start("gw_start_rest", [item(name, token1) for name in rest], token1)
        self.pending = dict(zip(BIG, state1 + state2))
        self.ready = {}

    def get(self, name, after=None):
        if name not in self.ready:
            self.ready[name] = _split_wait(f"gw_wait_{name}", [self.pending.pop(name)], after)[1][0]
        return self.ready[name]


class _GradPipe:
    def __init__(self, q, w, m, v):
        self.q, self.w, self.m, self.v = q, w, m, v
        self.stage, self.results = {}, {}

    def start_small(self, dmod, lossv, small, after):
        self.small_names = [n for n in REPLICATED if n != "ada_b"] + list(SMALL_SHARDED)
        payload = [dmod.reshape(2, -1), lossv] + [small[n] for n in self.small_names]
        self.small_shapes = [p.shape for p in payload]
        packed = _pack(payload)
        land = jnp.zeros((8 * PACK_ROWS, packed.shape[1]), F32)
        self.small_state, token = _split_start("ag_grads_start", [(packed, land, _everyone_plan(PACK_ROWS), 7)], after)
        return token

    def wait_small(self, after):
        srcs, lands = _split_wait("ag_grads_wait", self.small_state, after)
        return srcs[0], lands[0]

    def scatter(self, group, grads, after=None):
        items = []
        for name, g in grads.items():
            kind = BIG[name][0]
            rows, cols = g.shape
            n = (cols if kind == "col" else rows) // N_CHIPS
            reg = (rows, n) if kind == "col" else (n, cols)
            items.append((g, lax.empty((N_CHIPS - 1, *reg), BF16), _scatter_plan(kind, n), N_CHIPS - 1))
        state, token = _split_start(f"gs_start_{group}", items, after)
        self.stage[group] = (list(grads), state)
        return token[0, 0]

    def collect(self, group, after):
        names, state = self.stage[group]
        srcs, lands = _split_wait(f"gs_wait_{group}", state, after)
        items = []
        for name, st, g, land in zip(names, state, srcs, lands):
            _, kind, n = st[0][0]
            part = _sum4(f"sum_{name}", g, land, self.q, kind, n)
            items.append((part, lax.empty(part.shape, F32), _sibling_plan(), 1))
        state, token = _split_start(f"sw_start_{group}", items)
        self.stage[group] = (names, state)
        return token

    def finish(self, group, after):
        names, state = self.stage.pop(group)
        srcs, lands = _split_wait(f"sw_wait_{group}", state, after)
        for name, mine, theirs in zip(names, srcs, lands):
            _, pname, layer = BIG[name]
            self.results[pname] = _adamw_sum(f"adamw_{name}", self.w[pname], self.m[pname], self.v[pname], layer,
                                             mine, theirs, self.results.get(pname))


PACK_ROWS = 8


def _pack(arrays):
    flat = jnp.concatenate([a.reshape(-1) for a in arrays])
    n = flat.shape[0]
    padded = -(-n // (PACK_ROWS * LANES)) * (PACK_ROWS * LANES)
    return jnp.pad(flat, (0, padded - n)).reshape(PACK_ROWS, padded // PACK_ROWS)


def _unpack(packed, shapes):
    flat = packed.reshape(-1)
    out, off = [], 0
    for s in shapes:
        n = 1
        for d in s:
            n *= d
        out.append(flat[off:off + n].reshape(s))
        off += n
    return out


REPLICATED = ("ada_b", "norm_mix_g", "norm_ffn_g", "a_vnorm_g", "a_spatial_w", "a_spatial_b", "b_q_norm_g",
              "b_k_norm_g", "ffn_dw_b")
SMALL_SHARDED = ("conv_pw1_b", "conv_dw_w", "conv_dw_b", "conv_ln_g", "conv_ln_b", "conv_pw2_b", "ffn_dw_w")
WEIGHTS = ("ada_w", "ada_b", "norm_mix_g", "norm_ffn_g", "ab_w_in", "a_vnorm_g", "a_spatial_w", "a_spatial_b",
           "b_q_norm_g", "b_k_norm_g", "ab_w_out", "conv_pw1_w", "conv_pw1_b", "conv_dw_w", "conv_dw_b", "conv_ln_g",
           "conv_ln_b", "conv_pw2_w", "conv_pw2_b", "ffn_up_w", "ffn_dw_w", "ffn_dw_b", "ffn_down_w")

def kernel(x, c, positions, ada_w, ada_b, norm_mix_g, norm_ffn_g, ab_w_in, a_vnorm_g, a_spatial_w, a_spatial_b, b_q_norm_g, b_k_norm_g, ab_w_out, conv_pw1_w, conv_pw1_b, conv_dw_w, conv_dw_b, conv_ln_g, conv_ln_b, conv_pw2_w, conv_pw2_b, ffn_up_w, ffn_dw_w, ffn_dw_b, ffn_down_w, loss_target, m_ada_w, m_ada_b, m_norm_mix_g, m_norm_ffn_g, m_ab_w_in, m_a_vnorm_g, m_a_spatial_w, m_a_spatial_b, m_b_q_norm_g, m_b_k_norm_g, m_ab_w_out, m_conv_pw1_w, m_conv_pw1_b, m_conv_dw_w, m_conv_dw_b, m_conv_ln_g, m_conv_ln_b, m_conv_pw2_w, m_conv_pw2_b, m_ffn_up_w, m_ffn_dw_w, m_ffn_dw_b, m_ffn_down_w, v_ada_w, v_ada_b, v_norm_mix_g, v_norm_ffn_g, v_ab_w_in, v_a_vnorm_g, v_a_spatial_w, v_a_spatial_b, v_b_q_norm_g, v_b_k_norm_g, v_ab_w_out, v_conv_pw1_w, v_conv_pw1_b, v_conv_dw_w, v_conv_dw_b, v_conv_ln_g, v_conv_ln_b, v_conv_pw2_w, v_conv_pw2_b, v_ffn_up_w, v_ffn_dw_w, v_ffn_dw_b, v_ffn_down_w):
    w = dict(ada_w=ada_w, ada_b=ada_b, norm_mix_g=norm_mix_g, norm_ffn_g=norm_ffn_g, ab_w_in=ab_w_in, a_vnorm_g=a_vnorm_g, a_spatial_w=a_spatial_w, a_spatial_b=a_spatial_b, b_q_norm_g=b_q_norm_g, b_k_norm_g=b_k_norm_g, ab_w_out=ab_w_out, conv_pw1_w=conv_pw1_w, conv_pw1_b=conv_pw1_b, conv_dw_w=conv_dw_w, conv_dw_b=conv_dw_b, conv_ln_g=conv_ln_g, conv_ln_b=conv_ln_b, conv_pw2_w=conv_pw2_w, conv_pw2_b=conv_pw2_b, ffn_up_w=ffn_up_w, ffn_dw_w=ffn_dw_w, ffn_dw_b=ffn_dw_b, ffn_down_w=ffn_down_w)
    m = dict(ada_w=m_ada_w, ada_b=m_ada_b, norm_mix_g=m_norm_mix_g, norm_ffn_g=m_norm_ffn_g, ab_w_in=m_ab_w_in, a_vnorm_g=m_a_vnorm_g, a_spatial_w=m_a_spatial_w, a_spatial_b=m_a_spatial_b, b_q_norm_g=m_b_q_norm_g, b_k_norm_g=m_b_k_norm_g, ab_w_out=m_ab_w_out, conv_pw1_w=m_conv_pw1_w, conv_pw1_b=m_conv_pw1_b, conv_dw_w=m_conv_dw_w, conv_dw_b=m_conv_dw_b, conv_ln_g=m_conv_ln_g, conv_ln_b=m_conv_ln_b, conv_pw2_w=m_conv_pw2_w, conv_pw2_b=m_conv_pw2_b, ffn_up_w=m_ffn_up_w, ffn_dw_w=m_ffn_dw_w, ffn_dw_b=m_ffn_dw_b, ffn_down_w=m_ffn_down_w)
    v = dict(ada_w=v_ada_w, ada_b=v_ada_b, norm_mix_g=v_norm_mix_g, norm_ffn_g=v_norm_ffn_g, ab_w_in=v_ab_w_in, a_vnorm_g=v_a_vnorm_g, a_spatial_w=v_a_spatial_w, a_spatial_b=v_a_spatial_b, b_q_norm_g=v_b_q_norm_g, b_k_norm_g=v_b_k_norm_g, ab_w_out=v_ab_w_out, conv_pw1_w=v_conv_pw1_w, conv_pw1_b=v_conv_pw1_b, conv_dw_w=v_conv_dw_w, conv_dw_b=v_conv_dw_b, conv_ln_g=v_conv_ln_g, conv_ln_b=v_conv_ln_b, conv_pw2_w=v_conv_pw2_w, conv_pw2_b=v_conv_pw2_b, ffn_up_w=v_ffn_up_w, ffn_dw_w=v_ffn_dw_w, ffn_dw_b=v_ffn_dw_b, ffn_down_w=v_ffn_down_w)
    S, D = x.shape[1], x.shape[2]
    xi, yi, ci = lax.axis_index("x"), lax.axis_index("y"), lax.axis_index("c")
    q = 2 * xi + yi
    b = 2 * q + ci
    take_dev = lambda g: g.reshape(8, PACK_ROWS, -1)

    c_all = _allgather8("ag_c", c.reshape(PACK_ROWS, D // PACK_ROWS)).reshape(8, D)
    n_ada = ada_w.shape[2]
    mod_sh = _ada_fwd("ada_fwd", c_all, ada_w, lax.dynamic_slice_in_dim(ada_b, q * n_ada, n_ada, axis=1))
    sh_shapes = [mod_sh.shape] + [w[n].shape for n in SMALL_SHARDED]
    gathered_mod = _allgather8("ag_mod", _pack([mod_sh] + [w[n] for n in SMALL_SHARDED]))
    per_chip = [_unpack(blk, sh_shapes) for blk in take_dev(gathered_mod)[0::2]]
    mod_g = jnp.stack([pc[0] for pc in per_chip])
    mod_mine = lax.dynamic_index_in_dim(mod_g, b, axis=2, keepdims=False)
    mod = mod_mine.transpose(1, 0, 2).reshape(2, 6, D)
    sp = {n: jnp.concatenate([pc[1 + i] for pc in per_chip], axis=-1) for i, n in enumerate(SMALL_SHARDED)}
    sp.update({n: w[n] for n in REPLICATED if n != "ada_b"})

    wb = _Weights(w, q, gathered_mod)
    mod = mod + wb.token[0, 0]

    pipe = _GradPipe(q, w, m, v)
    lossv, grad_x, late = _local_step(x[0], loss_target[0], positions[0], mod, wb, sp, pipe)

    pipe.finish("g1", grad_x)
    pipe.finish("g2", pipe.results["ffn_up_w"][0])
    swapped = pipe.collect("g3", pipe.results["conv_pw1_w"][0])
    swapped = pipe.collect("g4", swapped)

    own, gathered = pipe.wait_small(swapped)
    totals = _unpack(_sum8("sum_grads", gathered, own), pipe.small_shapes)
    grads = dict(zip(["ada_b", "loss_columns"] + pipe.small_names, totals))
    loss = 0.5 * jnp.sum(grads.pop("loss_columns")) / D
    late_g = _allgather8("ag_late", _pack(list(late)), gathered)
    late_tot = _unpack(_sum8("sum_late", late_g), [(3, D)])[0]
    grads["ada_b"] = grads["ada_b"].at[0, :2 * D].add(late_tot[:2].reshape(-1))
    grads["norm_mix_g"] = grads["norm_mix_g"].at[0].add(late_tot[2])
    for n in SMALL_SHARDED:
        n_sh = w[n].shape[-1]
        grads[n] = lax.dynamic_slice_in_dim(grads[n], q * n_sh, n_sh, axis=grads[n].ndim - 1)
    dmod_of = lambda packed: packed.reshape(packed.shape[0] // PACK_ROWS, -1)[:, :2 * 6 * D].reshape(-1, 2, 6 * D)
    dmod_all = jnp.where((jnp.arange(8) == b)[:, None, None], dmod_of(own), dmod_of(gathered))
    late_all = take_dev(late_g).reshape(8, -1)[:, :3 * D].reshape(8, 3, D)
    dmod_all = dmod_all.at[:, 0, :2 * D].add(late_all[:, :2].reshape(8, 2 * D))
    dmod_sh = lax.dynamic_slice_in_dim(dmod_all, q * n_ada, n_ada, axis=2).transpose(1, 0, 2)

    pipe.finish("g3", dmod_sh)
    pipe.finish("g4", pipe.results["ffn_up_w"][0])
    grads["ada_w"], delta_ada, m_ada, v_ada = _ada_update("ada_update", c_all, dmod_sh, ada_w, m_ada_w, v_ada_w)
    delta, new_m, new_v = dict(ada_w=delta_ada), dict(ada_w=m_ada), dict(ada_w=v_ada)
    rest = list(REPLICATED) + list(SMALL_SHARDED)
    rest_shapes = [w[n].shape for n in rest]
    outs = _adamw("adamw_small", *[_pack([src[n].reshape(w[n].shape) for n in rest]) for src in (w, grads, m, v)])
    for tgt, packed in zip((delta, new_m, new_v), outs):
        tgt.update(dict(zip(rest, _unpack(packed, rest_shapes))))
    for n in rest:
        grads[n] = grads[n].reshape(w[n].shape)
    pipe.finish("g5", pipe.collect("g5", outs[0]))
    for n, res in pipe.results.items():
        grads[n], delta[n], new_m[n], new_v[n] = res

    return (loss, grad_x[None], *[grads[n] for n in WEIGHTS], *[delta[n] for n in WEIGHTS],
            *[new_m[n] for n in WEIGHTS], *[new_v[n] for n in WEIGHTS])
```

```python
import functools

import jax
import jax.numpy as jnp
from jax import lax
from jax.experimental import pallas as pl
from jax.experimental.pallas import tpu as pltpu

F32, BF16 = jnp.float32, jnp.bfloat16
EPS = 1e-6
NEG = -1e30
ROPE_THETA = 10000.0
LANES = 128
VMEM_LIMIT = 56 * 1024 * 1024
ADAM_LR, ADAM_B1, ADAM_B2, ADAM_EPS, ADAM_WD, ADAM_STEP = 0.001, 0.9, 0.999, 1e-08, 0.01, 10
MESH = pl.DeviceIdType.MESH


def _params(sem):
    return pltpu.CompilerParams(dimension_semantics=sem, vmem_limit_bytes=VMEM_LIMIT)


_DN = {"nn": (((1,), (0,)), ((), ())), "nt": (((1,), (1,)), ((), ())), "tn": (((0,), (0,)), ((), ()))}


def _matmul(name, a, b, mode, out_dtype, tm=512, tn=512, tk=1024, bias=None, after=None):
    if mode == "nn":
        (M, K), N = a.shape, b.shape[1]
    elif mode == "nt":
        (M, K), N = a.shape, b.shape[0]
    else:
        (K, M), N = a.shape, b.shape[1]
    tm, tn, tk = min(tm, M), min(tn, N), min(tk, K)
    assert M % tm == 0 and N % tn == 0 and K % tk == 0, (name, M, N, K, tm, tn, tk)
    nk = K // tk
    if mode == "tn":
        a_spec = pl.BlockSpec((tk, tm), lambda i, j, k: (k, i))
    else:
        a_spec = pl.BlockSpec((tm, tk), lambda i, j, k: (i, k))
    if mode == "nt":
        b_spec = pl.BlockSpec((tn, tk), lambda i, j, k: (j, k))
    else:
        b_spec = pl.BlockSpec((tk, tn), lambda i, j, k: (k, j))
    in_specs, args = [a_spec, b_spec], [a, b]
    if bias is not None:
        in_specs.append(pl.BlockSpec((1, tn), lambda i, j, k: (0, j)))
        args.append(bias)
    if after is not None:
        in_specs.append(pl.BlockSpec(memory_space=pl.ANY))
        args.append(after)
    n_in = len(args)

    def body(*refs):
        a_ref, b_ref, o_ref = refs[0], refs[1], refs[n_in]
        p = lax.dot_general(a_ref[...], b_ref[...], _DN[mode], preferred_element_type=F32)

        def finish(acc):
            if bias is not None:
                acc = acc + refs[2][...]
            o_ref[...] = acc.astype(o_ref.dtype)

        if nk == 1:
            finish(p)
        else:
            acc_ref = refs[n_in + 1]
            k = pl.program_id(2)

            @pl.when(k == 0)
            def _():
                acc_ref[...] = p

            @pl.when(k > 0)
            def _():
                acc_ref[...] += p

            @pl.when(k == nk - 1)
            def _():
                finish(acc_ref[...])

    return pl.pallas_call(
        body, name=name, grid=(M // tm, N // tn, nk), in_specs=in_specs,
        out_specs=pl.BlockSpec((tm, tn), lambda i, j, k: (i, j)),
        out_shape=jax.ShapeDtypeStruct((M, N), out_dtype),
        scratch_shapes=[pltpu.VMEM((tm, tn), F32)] if nk > 1 else [],
        compiler_params=_params(("parallel", "parallel", "arbitrary")),
    )(*args)


def _rowcall(name, body, nrows, tm, ins, outs, scratch=()):
    nblk = nrows // tm
    assert nrows % tm == 0

    def spec(kind, shape):
        k = kind[0]
        if k == "row":
            cw, cb = kind[1] or shape[-1], kind[2]
            return pl.BlockSpec((tm, cw), lambda i: (i, cb))
        if k == "prev":
            hb, cw, cb = kind[1], kind[2] or shape[-1], kind[3]
            r = tm // hb
            return pl.BlockSpec((hb, cw), lambda i: (jnp.maximum(i * r - 1, 0), cb))
        if k == "next":
            hb, cw, cb = kind[1], kind[2] or shape[-1], kind[3]
            r, last = tm // hb, nrows // hb - 1
            return pl.BlockSpec((hb, cw), lambda i: (jnp.minimum((i + 1) * r, last), cb))
        if k == "off":
            off, cw, cb = kind[1], kind[2] or shape[-1], kind[3]
            return pl.BlockSpec((tm, cw), lambda i: (jnp.clip(i + off, 0, nblk - 1), cb))
        if k == "cls":
            dil, off = kind[1], kind[2]
            return pl.BlockSpec((dil, tm // dil, shape[-1]), lambda i: (0, jnp.clip(i + off, 0, nblk - 1), 0))
        nd = len(shape)
        return pl.BlockSpec(tuple(shape), lambda i: (0,) * nd)

    has_acc = any(o[2][0] == "acc" for o in outs)
    return pl.pallas_call(
        body, name=name, grid=(nblk,),
        in_specs=[spec(kind, a.shape) for a, kind in ins],
        out_specs=[spec(kind, shape) for shape, _, kind in outs],
        out_shape=[jax.ShapeDtypeStruct(tuple(shape), dt) for shape, dt, _ in outs],
        scratch_shapes=list(scratch),
        compiler_params=_params(("arbitrary",) if has_acc else ("parallel",)),
    )(*[a for a, _ in ins])


ROW = ("row", None, 0)
FULL = ("full",)
ACC = ("acc",)


def _colsum(x):
    return jnp.sum(x, axis=0, keepdims=True)


def _acc_add(i, ref, val, rows=None):
    idx = (slice(None),) * len(ref.shape) if rows is None else rows

    @pl.when(i == 0)
    def _():
        ref[idx] = val

    @pl.when(i > 0)
    def _():
        ref[idx] = ref[idx] + val


def _sigmoid(x):
    return 1.0 / (1.0 + jnp.exp(-x))


def _gelu(x):
    return 0.5 * x * (1.0 + lax.erf(x * (2.0 ** -0.5)))


def _gelu_grad(x):
    return 0.5 * (1.0 + lax.erf(x * (2.0 ** -0.5))) + x * jnp.exp(-0.5 * x * x) * ((2.0 * jnp.pi) ** -0.5)


SUBLANES = 8


def _phases(ext, sign):
    n = ext.shape[0]
    return [ext if b == 0 else pltpu.roll(ext, b if sign > 0 else n - b, axis=0) for b in range(SUBLANES)]


def _shift_prev(phases, s, hb):
    a, b = divmod(s, SUBLANES)
    return phases[b][hb - SUBLANES * a:phases[b].shape[0] - SUBLANES * a]


def _shift_next(phases, s, tm):
    a, b = divmod(s, SUBLANES)
    return phases[b][SUBLANES * a:SUBLANES * a + tm]


def _rms_mod_val(x, g, sc, sh):
    r = lax.rsqrt(jnp.mean(x * x, axis=-1, keepdims=True) + EPS)
    return x * r * g * (1.0 + sc) + sh


def _mod_first(name, x, g, sc, sh, tm=512):
    S, D = x.shape

    def body(x_ref, g_ref, sc_ref, sh_ref, h_ref):
        h_ref[...] = _rms_mod_val(x_ref[...], g_ref[...], sc_ref[...], sh_ref[...]).astype(BF16)

    return _rowcall(name, body, S, tm, [(x, ROW), (g, FULL), (sc, FULL), (sh, FULL)], [((S, D), BF16, ROW)])[0]


def _resid_mod(name, x, y, gate, g, sc, sh, tm=512):
    S, D = x.shape

    def body(x_ref, y_ref, gate_ref, g_ref, sc_ref, sh_ref, xo_ref, h_ref):
        xn = x_ref[...] + gate_ref[...] * y_ref[...]
        xo_ref[...] = xn
        h_ref[...] = _rms_mod_val(xn, g_ref[...], sc_ref[...], sh_ref[...]).astype(BF16)

    return _rowcall(name, body, S, tm,
                    [(x, ROW), (y, ROW), (gate, FULL), (g, FULL), (sc, FULL), (sh, FULL)],
                    [((S, D), F32, ROW), ((S, D), BF16, ROW)])


def _gate_bwd_val(i, d, y_ref, gate_ref, dy_ref, dg_ref, db_ref):
    dy = d * gate_ref[...]
    dy_ref[...] = dy.astype(BF16)
    _acc_add(i, dg_ref, _colsum(d * y_ref[...]))
    _acc_add(i, db_ref, _colsum(dy))


GATE_OUTS = lambda S, D: [((S, D), BF16, ROW), ((1, D), F32, ACC), ((1, D), F32, ACC)]


def _loss_head(name, x, y, gate, tgt, tm=512):
    S, D = x.shape

    def body(x_ref, y_ref, gate_ref, t_ref, dx_ref, l_ref, dy_ref, dg_ref, db_ref):
        i = pl.program_id(0)
        err = x_ref[...] + gate_ref[...] * y_ref[...] - t_ref[...]
        d = err * (1.0 / D)
        dx_ref[...] = d
        _acc_add(i, l_ref, _colsum(err * err))
        _gate_bwd_val(i, d, y_ref, gate_ref, dy_ref, dg_ref, db_ref)

    return _rowcall(name, body, S, tm, [(x, ROW), (y, ROW), (gate, FULL), (tgt, ROW)],
                    [((S, D), F32, ROW), ((1, D), F32, ACC)] + GATE_OUTS(S, D))


def _mod_bwd(name, dxo, dh, x, g, sc, y=None, gate=None, tm=512):
    S, D = x.shape
    gated = y is not None

    def body(d_ref, dh_ref, x_ref, g_ref, sc_ref, *rest):
        dx_ref, dsh_ref, dsc_ref, dg_ref = rest[2 * gated:2 * gated + 4]
        i = pl.program_id(0)
        xv, dh_v, gv = x_ref[...], dh_ref[...], g_ref[...]
        r = lax.rsqrt(jnp.mean(xv * xv, axis=-1, keepdims=True) + EPS)
        n = xv * r
        _acc_add(i, dsh_ref, _colsum(dh_v))
        _acc_add(i, dsc_ref, _colsum(dh_v * (n * gv)))
        dy = dh_v * (1.0 + sc_ref[...])
        _acc_add(i, dg_ref, _colsum(dy * n))
        dn = dy * gv
        dx = d_ref[...] + r * (dn - n * jnp.mean(dn * n, axis=-1, keepdims=True))
        dx_ref[...] = dx
        if gated:
            _gate_bwd_val(i, dx, rest[0], rest[1], *rest[6:9])

    ins = [(dxo, ROW), (dh, ROW), (x, ROW), (g, FULL), (sc, FULL)] + ([(y, ROW), (gate, FULL)] if gated else [])
    outs = [((S, D), F32, ROW), ((1, D), F32, ACC), ((1, D), F32, ACC), ((1, D), F32, ACC)]
    return _rowcall(name, body, S, tm, ins, outs + (GATE_OUTS(S, D) if gated else []))


HB16 = 16


def _conv3_val(ph, w, b, hb):
    return w[2:3] * _shift_prev(ph, 0, hb) + w[1:2] * _shift_prev(ph, 1, hb) + w[0:1] * _shift_prev(ph, 2, hb) + b


def _halo_first(halo_ref, tile_ref, live):
    return _phases(jnp.concatenate([halo_ref[...].astype(F32) * live, tile_ref[...].astype(F32)], axis=0), 1)


def _glu3_fwd(name, u, w, b, tm=256):
    S, F2 = u.shape
    Fh = F2 // 2

    def body(ua_ref, ub_ref, ha_ref, hb_ref, w_ref, b_ref, o_ref, z_ref):
        live = (pl.program_id(0) > 0).astype(F32)
        wv, bv = w_ref[...], b_ref[...]
        za = _conv3_val(_halo_first(ha_ref, ua_ref, live), wv[:, :Fh], bv[:, :Fh], HB16)
        zb = _conv3_val(_halo_first(hb_ref, ub_ref, live), wv[:, Fh:], bv[:, Fh:], HB16)
        o_ref[...] = (za * _sigmoid(za) * zb).astype(BF16)
        z_ref[:, :Fh] = za.astype(BF16)
        z_ref[:, Fh:] = zb.astype(BF16)

    return _rowcall(name, body, S, tm,
                    [(u, ("row", Fh, 0)), (u, ("row", Fh, 1)), (u, ("prev", HB16, Fh, 0)), (u, ("prev", HB16, Fh, 1)),
                     (w, FULL), (b, FULL)],
                    [((S, Fh), BF16, ROW), ((S, F2), BF16, ROW)])


def _glu3_bwd(name, z, dhm, tm=256):
    S, F2 = z.shape
    Fh = F2 // 2

    def body(za_ref, zb_ref, d_ref, dz_ref, db_ref):
        i = pl.program_id(0)
        za, zb, d = za_ref[...].astype(F32), zb_ref[...].astype(F32), d_ref[...]
        sg = _sigmoid(za)
        da = d * zb * (sg * (1.0 + za * (1.0 - sg)))
        db = d * (za * sg)
        dz_ref[:, :Fh] = da.astype(BF16)
        dz_ref[:, Fh:] = db.astype(BF16)
        _acc_add(i, db_ref, jnp.concatenate([_colsum(da), _colsum(db)], axis=1))

    return _rowcall(name, body, S, tm, [(z, ("row", Fh, 0)), (z, ("row", Fh, 1)), (dhm, ROW)],
                    [((S, F2), BF16, ROW), ((1, F2), F32, ACC)])


def _conv3_bwd(name, dz, u, w, tm=256):
    S, F2 = dz.shape
    nblk = S // tm
    K = w.shape[0]

    def body(d_ref, n_ref, u_ref, w_ref, o_ref, dw_ref):
        i = pl.program_id(0)
        live = (i < nblk - 1).astype(F32)
        ph = _phases(jnp.concatenate([d_ref[...].astype(F32), n_ref[...].astype(F32) * live], axis=0), -1)
        wv, uv = w_ref[...], u_ref[...].astype(F32)
        shifted = [_shift_next(ph, K - 1 - k, tm) for k in range(K)]
        o_ref[...] = functools.reduce(lambda a, t: a + t, [wv[k:k + 1] * shifted[k] for k in range(K)]).astype(BF16)
        for k in range(K):
            _acc_add(i, dw_ref, _colsum(uv * shifted[k]), rows=(slice(k, k + 1), slice(None)))

        @pl.when(i == 0)
        def _():
            dw_ref[K:, :] = jnp.zeros((dw_ref.shape[0] - K, F2), F32)

    return _rowcall(name, body, S, tm, [(dz, ROW), (dz, ("next", HB16, None, 0)), (u, ROW), (w, FULL)],
                    [((S, F2), BF16, ROW), ((SUBLANES, F2), F32, ACC)])


def _ffn_fwd(name, h, w_up, w_dn, dw_w, dw_b):
    u = _matmul(f"{name}_up", h, w_up, "nn", BF16, tm=2048)
    hm, z = _glu3_fwd(f"{name}_glu", u, dw_w, dw_b)
    f = _matmul(f"{name}_dn", hm, w_dn, "nn", F32, tm=1024, tn=1024, tk=w_dn.shape[0])
    return f, (u, hm, z)


def _ffn_bwd(name, dy, h, u, hm, z, w_up, w_dn, dw_w, dw_b):
    Fh = w_dn.shape[0]
    dhm = _matmul(f"{name}_ddn_x", dy, w_dn, "nt", F32, tm=1024, tn=Fh // 2)
    g_dn = _matmul(f"{name}_ddn_w", hm, dy, "tn", BF16, tm=Fh // 2, tn=1024, tk=2048)
    dz, g_dw_b = _glu3_bwd(f"{name}_dglu", z, dhm)
    du, taps = _conv3_bwd(f"{name}_dconv", dz, u, dw_w)
    g_up = _matmul(f"{name}_dup_w", h, du, "tn", BF16, tm=1024, tn=Fh // 2, tk=2048)
    dh = _matmul(f"{name}_dup_x", du, w_up, "nt", F32, tm=1024, tn=1024, tk=Fh)
    return dh, dict(up=g_up, dn=g_dn, dw_w=taps[0:dw_w.shape[0]], dw_b=g_dw_b)


HB32 = 32


def _glu31_fwd(name, p, w, b, tm=512):
    S, D2 = p.shape
    D = D2 // 2
    K = w.shape[0]

    def body(a_ref, g_ref, ha_ref, hg_ref, w_ref, b_ref, o_ref):
        live = (pl.program_id(0) > 0).astype(F32)
        y1 = a_ref[...] * _sigmoid(g_ref[...])
        ph = _phases(jnp.concatenate([ha_ref[...] * _sigmoid(hg_ref[...]) * live, y1], axis=0), 1)
        wv = w_ref[...]
        acc = b_ref[...] + wv[K - 1:K] * y1
        for k in range(K - 1):
            acc = acc + wv[k:k + 1] * _shift_prev(ph, K - 1 - k, HB32)
        o_ref[...] = acc

    return _rowcall(name, body, S, tm,
                    [(p, ("row", D, 0)), (p, ("row", D, 1)), (p, ("prev", HB32, D, 0)), (p, ("prev", HB32, D, 1)),
                     (w, FULL), (b, FULL)],
                    [((S, D), F32, ROW)])[0]


def _ln_silu_fwd(name, y2, g, b, tm=512):
    S, D = y2.shape

    def body(y_ref, g_ref, b_ref, o_ref):
        y = y_ref[...]
        mu = jnp.mean(y, axis=-1, keepdims=True)
        yc = y - mu
        rs = lax.rsqrt(jnp.mean(yc * yc, axis=-1, keepdims=True) + EPS)
        y3 = yc * rs * g_ref[...] + b_ref[...]
        o_ref[...] = (y3 * _sigmoid(y3)).astype(BF16)

    return _rowcall(name, body, S, tm, [(y2, ROW), (g, FULL), (b, FULL)], [((S, D), BF16, ROW)])[0]


def _ln_silu_bwd(name, y2, dy4, g, b, tm=512):
    S, D = y2.shape

    def body(y_ref, d_ref, g_ref, b_ref, o_ref, dg_ref, db_ref):
        i = pl.program_id(0)
        y, gv = y_ref[...], g_ref[...]
        mu = jnp.mean(y, axis=-1, keepdims=True)
        yc = y - mu
        rs = lax.rsqrt(jnp.mean(yc * yc, axis=-1, keepdims=True) + EPS)
        n = yc * rs
        y3 = n * gv + b_ref[...]
        sg = _sigmoid(y3)
        dy3 = d_ref[...] * (sg * (1.0 + y3 * (1.0 - sg)))
        _acc_add(i, db_ref, _colsum(dy3))
        _acc_add(i, dg_ref, _colsum(dy3 * n))
        dn = dy3 * gv
        o_ref[...] = rs * (dn - jnp.mean(dn, axis=-1, keepdims=True) - n * jnp.mean(dn * n, axis=-1, keepdims=True))

    return _rowcall(name, body, S, tm, [(y2, ROW), (dy4, ROW), (g, FULL), (b, FULL)],
                    [((S, D), F32, ROW), ((1, D), F32, ACC), ((1, D), F32, ACC)])


def _glu31_bwd(name, p, dy2, w, tm=512):
    S, D2 = p.shape
    D = D2 // 2
    K = w.shape[0]
    nblk = S // tm

    conv_rows, tap_rows, tap_group, tap_unroll = 16, SUBLANES, 4, 4

    def body(a_ref, g_ref, d_ref, dn_ref, w_ref, dp_ref, dw_ref, dcb_ref, dpb_ref, ph_d, y1_ref, dy1_ref, wb_ref):
        i = pl.program_id(0)
        live_next = (i < nblk - 1).astype(F32)
        a, sg, d = a_ref[...], _sigmoid(g_ref[...]), d_ref[...]
        y1_ref[...] = a * sg
        for b, ph in enumerate(_phases(jnp.concatenate([d, dn_ref[...] * live_next], axis=0), -1)):
            ph_d[b] = ph
        taps = [divmod(K - 1 - k, SUBLANES) for k in range(K)]

        @pl.when(i == 0)
        def _():
            for k in range(K):
                wb_ref[k] = jnp.broadcast_to(w_ref[k:k + 1, :], (SUBLANES, D))

        def conv_rows_at(rb, carry):
            r0 = pl.multiple_of(rb * conv_rows, conv_rows)
            accs = [jnp.zeros((SUBLANES, D), F32) for _ in range(conv_rows // SUBLANES)]
            for k, (rows8, phase) in enumerate(taps):
                wk = wb_ref[k]
                for u in range(len(accs)):
                    accs[u] = accs[u] + wk * ph_d[phase, pl.ds(r0 + SUBLANES * (rows8 + u), SUBLANES), :]
            for u, acc in enumerate(accs):
                dy1_ref[pl.ds(r0 + SUBLANES * u, SUBLANES), :] = acc
            return carry

        lax.fori_loop(0, tm // conv_rows, conv_rows_at, 0)
        for k0 in range(0, K, tap_group):
            group = taps[k0:k0 + tap_group]

            def tap_rows_at(rb, accs, group=group):
                for u in range(tap_unroll):
                    r0 = pl.multiple_of((rb * tap_unroll + u) * tap_rows, tap_rows)
                    yv = y1_ref[pl.ds(r0, tap_rows), :]
                    accs = tuple(acc + yv * ph_d[phase, pl.ds(r0 + SUBLANES * rows8, tap_rows), :]
                                 for acc, (rows8, phase) in zip(accs, group))
                return accs

            accs = lax.fori_loop(0, tm // (tap_rows * tap_unroll), tap_rows_at,
                                 tuple(jnp.zeros((tap_rows, D), F32) for _ in group))
            for j, acc in enumerate(accs):
                _acc_add(i, dw_ref, _colsum(acc), rows=(slice(k0 + j, k0 + j + 1), slice(None)))

        @pl.when(i == 0)
        def _():
            dw_ref[K:, :] = jnp.zeros((dw_ref.shape[0] - K, D), F32)

        _acc_add(i, dcb_ref, _colsum(d))
        dy1 = dy1_ref[...]
        da = dy1 * sg
        dg = dy1 * a * sg * (1.0 - sg)
        dp_ref[:, :D] = da.astype(BF16)
        dp_ref[:, D:] = dg.astype(BF16)
        _acc_add(i, dpb_ref, jnp.concatenate([_colsum(da), _colsum(dg)], axis=1))

    return _rowcall(name, body, S, tm,
                    [(p, ("row", D, 0)), (p, ("row", D, 1)), (dy2, ROW), (dy2, ("next", HB32, None, 0)), (w, FULL)],
                    [((S, D2), BF16, ROW), ((HB32, D), F32, ACC), ((1, D), F32, ACC), ((1, D2), F32, ACC)],
                    scratch=[pltpu.VMEM((SUBLANES, tm + HB32, D), F32), pltpu.VMEM((tm, D), F32),
                             pltpu.VMEM((tm, D), F32), pltpu.VMEM((K, SUBLANES, D), F32)])


CHUNK = 128
A_GROUPS = 4


def _group_ln(gv):
    ns, rss = [], []
    for g in range(A_GROUPS):
        xg = gv[:, g * LANES:(g + 1) * LANES]
        xc = xg - jnp.mean(xg, axis=-1, keepdims=True)
        rs = lax.rsqrt(jnp.mean(xc * xc, axis=-1, keepdims=True) + EPS)
        ns.append(xc * rs)
        rss.append(jnp.broadcast_to(rs, xg.shape))
    return jnp.concatenate(ns, axis=1), jnp.concatenate(rss, axis=1)


def _tril_mask():
    r = lax.broadcasted_iota(jnp.int32, (CHUNK, CHUNK), 0)
    c = lax.broadcasted_iota(jnp.int32, (CHUNK, CHUNK), 1)
    return r >= c


def _spatial(ws_ref, x, dn):
    mask = _tril_mask()
    rows = []
    for ci in range(x.shape[0] // CHUNK):
        cols = []
        for g in range(A_GROUPS):
            wm = jnp.where(mask, ws_ref[g], 0.0).astype(BF16)
            xb = x[ci * CHUNK:(ci + 1) * CHUNK, g * LANES:(g + 1) * LANES]
            cols.append(lax.dot_general(wm, xb, dn, preferred_element_type=F32))
        rows.append(jnp.concatenate(cols, axis=1))
    return jnp.concatenate(rows, axis=0)


def _mixa_fwd(name, z, vg, ws, bias_full, tm=512):
    S = z.shape[0]
    W = A_GROUPS * LANES

    def body(u_ref, v_ref, vg_ref, ws_ref, b_ref, o_ref):
        nh, _ = _group_ln(_gelu(v_ref[...]))
        vn = (nh * vg_ref[...]).astype(BF16)
        f = _spatial(ws_ref, vn, _DN["nn"]) + jnp.concatenate([b_ref[...]] * (tm // CHUNK), axis=0)
        o_ref[...] = (_gelu(u_ref[...]) * f).astype(BF16)

    return _rowcall(name, body, S, tm,
                    [(z, ("row", W, 0)), (z, ("row", W, 1)), (vg, FULL), (ws, FULL), (bias_full, FULL)],
                    [((S, W), BF16, ROW)])[0]


def _mixa_bwd(name, z, dyab, vg, ws, bias_full, tm=512):
    S = z.shape[0]
    W = A_GROUPS * LANES
    nch = tm // CHUNK

    def body(u_ref, v_ref, d_ref, vg_ref, ws_ref, b_ref, dz_ref, dws_ref, dbf_ref, dvg_ref):
        i = pl.program_id(0)
        u, v, d, vgv = u_ref[...], v_ref[...], d_ref[...], vg_ref[...]
        nh, rs = _group_ln(_gelu(v))
        vn = (nh * vgv).astype(BF16)
        f = _spatial(ws_ref, vn, _DN["nn"]) + jnp.concatenate([b_ref[...]] * nch, axis=0)
        dz_ref[:, :W] = (d * f * _gelu_grad(u)).astype(BF16)
        df = d * _gelu(u)
        dbf = df[0:CHUNK]
        for ci in range(1, nch):
            dbf = dbf + df[ci * CHUNK:(ci + 1) * CHUNK]
        _acc_add(i, dbf_ref, dbf)
        dfb = df.astype(BF16)
        mask = _tril_mask()
        for g in range(A_GROUPS):
            acc = jnp.zeros((CHUNK, CHUNK), F32)
            for ci in range(nch):
                blk = (slice(ci * CHUNK, (ci + 1) * CHUNK), slice(g * LANES, (g + 1) * LANES))
                acc = acc + lax.dot_general(dfb[blk], vn[blk], _DN["nt"], preferred_element_type=F32)
            _acc_add(i, dws_ref, jnp.where(mask, acc, 0.0)[None], rows=(slice(g, g + 1), slice(None), slice(None)))
        dvn = _spatial(ws_ref, dfb, _DN["tn"])
        _acc_add(i, dvg_ref, _colsum(dvn * nh))
        dnh = dvn * vgv
        parts = []
        for g in range(A_GROUPS):
            cs = slice(g * LANES, (g + 1) * LANES)
            dg_, ng = dnh[:, cs], nh[:, cs]
            parts.append(dg_ - jnp.mean(dg_, axis=-1, keepdims=True) - ng * jnp.mean(dg_ * ng, axis=-1, keepdims=True))
        dz_ref[:, W:] = (rs * jnp.concatenate(parts, axis=1) * _gelu_grad(v)).astype(BF16)

    return _rowcall(name, body, S, tm,
                    [(z, ("row", W, 0)), (z, ("row", W, 1)), (dyab, ("row", W, 0)), (vg, FULL), (ws, FULL),
                     (bias_full, FULL)],
                    [((S, 2 * W), BF16, ROW), ((A_GROUPS, CHUNK, CHUNK), F32, ACC), ((CHUNK, W), F32, ACC),
                     ((1, W), F32, ACC)])


HEAD = 64
N_HEADS = 8
BW = HEAD * N_HEADS
QB = 128
DILATIONS = (1, 4, 16)
QK_SCALE = HEAD ** -0.5


def _gsum64(x, ones_bd):
    x1 = x.astype(BF16)
    r1 = x - x1.astype(F32)
    x2 = r1.astype(BF16)
    x3 = (r1 - x2.astype(F32)).astype(BF16)
    dot = lambda t: jnp.dot(t, ones_bd, preferred_element_type=F32)
    return dot(x1) + dot(x2) + dot(x3)


def _swap32(x):
    n = x.shape[-1]
    up = pltpu.roll(x, n - HEAD // 2, axis=1)
    dn = pltpu.roll(x, HEAD // 2, axis=1)
    lane = lax.broadcasted_iota(jnp.int32, x.shape, 1)
    return jnp.where((lane % HEAD) < HEAD // 2, up, dn)


def _tile4(t):
    return jnp.concatenate([t] * (BW // LANES), axis=1)


def _stage_spec(tm):
    return pltpu.VMEM((BW // LANES, tm, LANES), F32)


def _to_classes(stage, x, dil):
    tm = x.shape[0]
    for j in range(BW // LANES):
        stage[j] = x[:, j * LANES:(j + 1) * LANES]
    return [jnp.concatenate([stage.at[j][pl.ds(r, tm // dil, stride=dil), :] for j in range(BW // LANES)], axis=1)
            for r in range(dil)]


def _from_classes(stage, cls, dil):
    rows = cls.shape[1]
    for r in range(dil):
        for j in range(BW // LANES):
            stage.at[j][pl.ds(r, rows, stride=dil), :] = cls[r, :, j * LANES:(j + 1) * LANES]
    return jnp.concatenate([stage[j] for j in range(BW // LANES)], axis=1)


def _cls_view(t, dil):
    return t if dil == 1 else t.reshape(dil, t.shape[0] // dil, t.shape[1])


def _cls_kind(dil, off=0):
    return ("off", off, None, 0) if dil == 1 else ("cls", dil, off)


def _cls_out(S, dil, dtype):
    return ((S, BW) if dil == 1 else (dil, S // dil, BW), dtype, _cls_kind(dil))


def _flat(t):
    return t.reshape(-1, t.shape[-1])


def _qkv_fwd(name, z, cos, sin, ones_bd, qg, kg, tm=512):
    S = z.shape[0]
    nd = len(DILATIONS)

    def body(q_ref, k_ref, v_ref, c_ref, s_ref, o_ref, qg_ref, kg_ref, *rest):
        outs, stage = rest[:3 * nd], rest[3 * nd]
        c, s, ob = _tile4(c_ref[...]), _tile4(s_ref[...]), o_ref[...]

        def norm_rope(x, g):
            r = lax.rsqrt(_gsum64(x * x, ob) * (1.0 / HEAD) + EPS)
            xn = x * r * g
            return xn * c + _swap32(xn) * s

        vals = [norm_rope(q_ref[...], qg_ref[...]) * QK_SCALE, norm_rope(k_ref[...], kg_ref[...]), v_ref[...]]
        for a, val in enumerate(vals):
            for b, dil in enumerate(DILATIONS):
                if dil == 1:
                    outs[nd * a + b][...] = val.astype(BF16)
                else:
                    for r, rows in enumerate(_to_classes(stage, val, dil)):
                        outs[nd * a + b][r] = rows.astype(BF16)

    outs = _rowcall(name, body, S, tm,
                    [(z, ("row", BW, 2)), (z, ("row", BW, 3)), (z, ("row", BW, 4)), (cos, ROW), (sin, ROW),
                     (ones_bd, FULL), (qg, FULL), (kg, FULL)],
                    [_cls_out(S, dil, BF16) for _ in range(3) for dil in DILATIONS], scratch=[_stage_spec(tm)])
    return [[_flat(outs[nd * a + b]) for a in range(3)] for b in range(nd)]


PAIR = 2 * HEAD


ATT_BLOCKS = 4
ATT_TM = ATT_BLOCKS * QB
ATT_PREV = ("prev", QB, None, 0)


def _key_rows(prev_ref, cur_ref, sb, ps):
    before = prev_ref[:, ps] if sb == 0 else cur_ref[(sb - 1) * QB:sb * QB, ps]
    return jnp.concatenate([before, cur_ref[sb * QB:(sb + 1) * QB, ps]], axis=0)


def _pair_scores(q_ref, kp_ref, kc_ref, sb, hp, half, seg_blocks):
    ps = slice(hp * PAIR, (hp + 1) * PAIR)
    mine = (lax.broadcasted_iota(jnp.int32, (1, PAIR), 1) >= HEAD) == (half == 1)
    qm = jnp.where(mine, q_ref[sb * QB:(sb + 1) * QB, ps], jnp.zeros((), BF16))
    kcat = _key_rows(kp_ref, kc_ref, sb, ps)
    s = lax.dot_general(qm, kcat, _DN["nt"], preferred_element_type=F32)
    qi = lax.broadcasted_iota(jnp.int32, (QB, 2 * QB), 0)
    kj = lax.broadcasted_iota(jnp.int32, (QB, 2 * QB), 1)
    has_prev = ((pl.program_id(0) * ATT_BLOCKS + sb) % seg_blocks) != 0
    valid = (kj >= qi) & (kj <= qi + QB) & ((kj >= QB) | has_prev)
    return mine, qm, kcat, s, valid


def _attn_fwd(name, q, k, v, dil):
    S = q.shape[0]
    seg_blocks = S // dil // QB

    def body(q_ref, kp_ref, kc_ref, vp_ref, vc_ref, o_ref, l_ref):
        for hp in range(N_HEADS // 2):
            ps = slice(hp * PAIR, (hp + 1) * PAIR)
            chains = [(sb, half) for sb in range(ATT_BLOCKS) for half in range(2)]
            sc = [_pair_scores(q_ref, kp_ref, kc_ref, sb, hp, half, seg_blocks) for sb, half in chains]
            ss = [jnp.where(valid, s, NEG) for _, _, _, s, valid in sc]
            ms = [jnp.max(s, axis=-1, keepdims=True) for s in ss]
            pv = [jnp.exp(s - m) for s, m in zip(ss, ms)]
            dens = [jnp.sum(p, axis=-1, keepdims=True) for p in pv]
            vcats = [_key_rows(vp_ref, vc_ref, sb, ps) for sb in range(ATT_BLOCKS)]
            outs = [jnp.dot(p.astype(BF16), vcats[sb], preferred_element_type=F32) / den
                    for p, den, (sb, _) in zip(pv, dens, chains)]
            lses = [jnp.broadcast_to(m + jnp.log(den), (QB, PAIR)) for m, den in zip(ms, dens)]
            for sb in range(ATT_BLOCKS):
                rows, upper = slice(sb * QB, (sb + 1) * QB), sc[2 * sb + 1][0]
                o_ref[rows, ps] = jnp.where(upper, outs[2 * sb + 1], outs[2 * sb])
                l_ref[rows, ps] = jnp.where(upper, lses[2 * sb + 1], lses[2 * sb])

    return _rowcall(name, body, S, ATT_TM, [(q, ROW), (k, ATT_PREV), (k, ROW), (v, ATT_PREV), (v, ROW)],
                    [((S, BW), F32, ROW)] * 2)


def _attn_bwd(name, q, k, v, do, lse, delta, dil):
    S = q.shape[0]
    seg_blocks = S // dil // QB

    def body(q_ref, kp_ref, kc_ref, vp_ref, vc_ref, do_ref, l_ref, dl_ref, dq_ref, dkc_ref, dkp_ref, dvc_ref, dvp_ref):
        for hp in range(N_HEADS // 2):
            ps = slice(hp * PAIR, (hp + 1) * PAIR)
            chains = [(sb, half) for sb in range(ATT_BLOCKS) for half in range(2)]
            rows = [slice(sb * QB, (sb + 1) * QB) for sb, _ in chains]
            cols = [hp * PAIR + half * HEAD for _, half in chains]
            sc = [_pair_scores(q_ref, kp_ref, kc_ref, sb, hp, half, seg_blocks) for sb, half in chains]
            pv = [jnp.where(valid, jnp.exp(s - l_ref[r, c:c + 1]), 0.0) for (_, _, _, s, valid), r, c in zip(sc, rows, cols)]
            vcats = [_key_rows(vp_ref, vc_ref, sb, ps) for sb in range(ATT_BLOCKS)]
            doms = [jnp.where(mine, do_ref[r, ps].astype(BF16), jnp.zeros((), BF16)) for (mine, *_), r in zip(sc, rows)]
            dps = [lax.dot_general(dom, vcats[sb], _DN["nt"], preferred_element_type=F32) for dom, (sb, _) in zip(doms, chains)]
            dss = [(p * (dp - dl_ref[r, c:c + 1])).astype(BF16) for p, dp, r, c in zip(pv, dps, rows, cols)]
            dqs = [jnp.dot(ds, kcat, preferred_element_type=F32) for ds, (_, _, kcat, _, _) in zip(dss, sc)]
            dks = [lax.dot_general(ds, qm, _DN["tn"], preferred_element_type=F32) for ds, (_, qm, *_) in zip(dss, sc)]
            dvs = [lax.dot_general(p.astype(BF16), dom, _DN["tn"], preferred_element_type=F32) for p, dom in zip(pv, doms)]
            for sb in range(ATT_BLOCKS):
                lo, hi = 2 * sb, 2 * sb + 1
                dk, dv = dks[lo] + dks[hi], dvs[lo] + dvs[hi]
                dq_ref[rows[lo], ps] = jnp.where(sc[hi][0], dqs[hi], dqs[lo])
                dkp_ref[rows[lo], ps] = dk[:QB]
                dkc_ref[rows[lo], ps] = dk[QB:]
                dvp_ref[rows[lo], ps] = dv[:QB]
                dvc_ref[rows[lo], ps] = dv[QB:]

    return _rowcall(name, body, S, ATT_TM,
                    [(q, ROW), (k, ATT_PREV), (k, ROW), (v, ATT_PREV), (v, ROW), do, (lse, ROW), (delta, ROW)],
                    [((S, BW), F32, ROW)] * 5)


def _merge_fwd(name, branches, tm=512):
    S = branches[0][0].shape[0]
    nd = len(DILATIONS)

    def body(*refs):
        ins, (y_ref, yb_ref), l_refs, stage = refs[:2 * nd], refs[2 * nd:2 * nd + 2], refs[2 * nd + 2:3 * nd + 2], refs[-1]
        os_, ls = [], []
        for b, dil in enumerate(DILATIONS):
            o, l = ins[2 * b][...], ins[2 * b + 1][...]
            os_.append(o if dil == 1 else _from_classes(stage, o, dil))
            ls.append(l if dil == 1 else _from_classes(stage, l, dil))
        m = functools.reduce(jnp.maximum, ls)
        es = [jnp.exp(l - m) for l in ls]
        den = functools.reduce(lambda a, e: a + e, es)
        y = functools.reduce(lambda a, t: a + t, [e * o for e, o in zip(es, os_)]) / den
        y_ref[...] = y
        yb_ref[...] = y.astype(BF16)
        lse = m + jnp.log(den)
        for b, dil in enumerate(DILATIONS):
            if dil == 1:
                l_refs[b][...] = lse
            else:
                for r, rows in enumerate(_to_classes(stage, lse, dil)):
                    l_refs[b][r] = rows

    ins = [(_cls_view(t, dil), _cls_kind(dil)) for pair, dil in zip(branches, DILATIONS) for t in pair]
    outs = _rowcall(name, body, S, tm, ins,
                    [((S, BW), F32, ROW), ((S, BW), BF16, ROW)] + [_cls_out(S, dil, F32) for dil in DILATIONS],
                    scratch=[_stage_spec(tm)])
    return outs[0], outs[1], [_flat(t) for t in outs[2:]]


def _delta(name, dyab, yb, ones_bd, tm=512):
    S = yb.shape[0]
    nd = len(DILATIONS)

    def body(d_ref, y_ref, o_ref, *rest):
        dl_refs, do_refs, stage = rest[:nd], rest[nd:2 * nd - 1], rest[-1]
        d = d_ref[...]
        dl = _gsum64(d * y_ref[...], o_ref[...])
        for b, dil in enumerate(DILATIONS):
            if dil == 1:
                dl_refs[b][...] = dl
            else:
                for r, rows in enumerate(_to_classes(stage, dl, dil)):
                    dl_refs[b][r] = rows
                for r, rows in enumerate(_to_classes(stage, d, dil)):
                    do_refs[b - 1][r] = rows.astype(BF16)

    outs = _rowcall(name, body, S, tm, [(dyab, ("row", BW, 1)), (yb, ROW), (ones_bd, FULL)],
                    [_cls_out(S, dil, F32) for dil in DILATIONS] + [_cls_out(S, dil, BF16) for dil in DILATIONS[1:]],
                    scratch=[_stage_spec(tm)])
    return [_flat(t) for t in outs[:nd]], [_flat(t) for t in outs[nd:]]


def _qkv_bwd(name, z, cos, sin, ones_bd, qg, kg, pieces):
    S = z.shape[0]
    nblk = S // QB

    def body(q_ref, k_ref, c_ref, s_ref, o_ref, qg_ref, kg_ref, *rest):
        pr, (dz_ref, dqg_ref, dkg_ref), stage = rest[:15], rest[15:18], rest[18]
        i = pl.program_id(0)
        c, s, ob = _tile4(c_ref[...]), _tile4(s_ref[...]), o_ref[...]
        dq = dk = dv = None
        for b, dil in enumerate(DILATIONS):
            a_q, a_kc, a_kp, a_vc, a_vp = [r[...] for r in pr[5 * b:5 * b + 5]]
            live = ((i + dil) < nblk).astype(F32)
            tq, tk, tv = a_q, a_kc + a_kp * live, a_vc + a_vp * live
            if dil > 1:
                tq, tk, tv = (_from_classes(stage, t, dil) for t in (tq, tk, tv))
            dq, dk, dv = (tq, tk, tv) if b == 0 else (dq + tq, dk + tk, dv + tv)

        def back(x, g, d_rot, acc_ref):
            r = lax.rsqrt(_gsum64(x * x, ob) * (1.0 / HEAD) + EPS)
            n = x * r
            dxn = d_rot * c + _swap32(d_rot * s)
            _acc_add(i, acc_ref, _colsum(dxn * n))
            dn = dxn * g
            return r * (dn - n * (_gsum64(dn * n, ob) * (1.0 / HEAD)))

        dz_ref[:, :BW] = back(q_ref[...], qg_ref[...], dq * QK_SCALE, dqg_ref).astype(BF16)
        dz_ref[:, BW:2 * BW] = back(k_ref[...], kg_ref[...], dk, dkg_ref).astype(BF16)
        dz_ref[:, 2 * BW:] = dv.astype(BF16)

    ins = [(z, ("row", BW, 2)), (z, ("row", BW, 3)), (cos, ROW), (sin, ROW), (ones_bd, FULL), (qg, FULL), (kg, FULL)]
    for piece, dil in zip(pieces, DILATIONS):
        a_q, a_kc, a_kp, a_vc, a_vp = (_cls_view(t, dil) for t in piece)
        own, prev = _cls_kind(dil), _cls_kind(dil, dil)
        ins += [(a_q, own), (a_kc, own), (a_kp, prev), (a_vc, own), (a_vp, prev)]
    return _rowcall(name, body, S, QB, ins,
                    [((S, 3 * BW), BF16, ROW), ((1, BW), F32, ACC), ((1, BW), F32, ACC)], scratch=[_stage_spec(QB)])


def _local_step(x0, tgt, pos, mod, wb, sp, pipe):
    S, D = x0.shape
    md = lambda l, j: mod[l, j:j + 1]
    sh_m, sc_m, g_m, sh_f, sc_f, g_f = ([md(l, j) for l in range(2)] for j in range(6))
    nm_g, nf_g = sp["norm_mix_g"], sp["norm_ffn_g"]

    inv_freq = 1.0 / (ROPE_THETA ** (jnp.arange(0, HEAD, 2, dtype=F32) / HEAD))
    ang = pos.astype(F32)[:, None] * inv_freq
    cs, sn = jnp.cos(ang), jnp.sin(ang)
    cos = jnp.concatenate([cs, cs, cs, cs], axis=1)
    sin = jnp.concatenate([-sn, sn, -sn, sn], axis=1)
    head_of = jnp.arange(BW) // HEAD
    ones_bd = (head_of[:, None] == head_of[None, :]).astype(BF16)
    qg = jnp.tile(sp["b_q_norm_g"].reshape(1, HEAD), (1, N_HEADS))
    kg = jnp.tile(sp["b_k_norm_g"].reshape(1, HEAD), (1, N_HEADS))
    vg = sp["a_vnorm_g"].reshape(1, A_GROUPS * LANES)
    ws = sp["a_spatial_w"][0]
    bias_full = jnp.repeat(sp["a_spatial_b"][0].T, LANES, axis=1)
    ffn_s = [(sp["ffn_dw_w"][l], sp["ffn_dw_b"][l:l + 1]) for l in range(2)]

    h0 = _mod_first("l0_mod", x0, nm_g[0:1], sc_m[0], sh_m[0])
    z = _matmul("l0_in", h0, wb.get("w_in", h0), "nn", F32, tm=2048)
    ya = _mixa_fwd("l0_mixa", z, vg, ws, bias_full)
    qkv = _qkv_fwd("l0_qkv", z, cos, sin, ones_bd, qg, kg)
    branches = [_attn_fwd(f"l0_att{dil}", *qkv[b], dil) for b, dil in enumerate(DILATIONS)]
    yb, yb16, lses = _merge_fwd("l0_merge", branches)
    yab = jnp.concatenate([ya, yb16], axis=1)
    y0 = _matmul("l0_out", yab, wb.get("w_out", yab), "nn", F32, tm=1024, tn=1024)
    x1, h1 = _resid_mod("l0_res1", x0, y0, g_m[0], nf_g[0:1], sc_f[0], sh_f[0])
    ffn_w = [(wb.get("up0", h1), wb.get("dn0", h1), *ffn_s[0])]
    f0, saved0 = _ffn_fwd("l0_ffn", h1, *ffn_w[0])
    x2, h2 = _resid_mod("l0_res2", x1, f0, g_f[0], nm_g[1:2], sc_m[1], sh_m[1])
    p = _matmul("l1_pw1", h2, wb.get("pw1", h2), "nn", F32, tm=2048, bias=sp["conv_pw1_b"])
    y2 = _glu31_fwd("l1_glu", p, sp["conv_dw_w"][0], sp["conv_dw_b"])
    y4 = _ln_silu_fwd("l1_ln", y2, sp["conv_ln_g"], sp["conv_ln_b"])
    y1 = _matmul("l1_pw2", y4, wb.get("pw2", y4), "nn", F32, tm=1024, tn=1024, bias=sp["conv_pw2_b"])
    x3, h3 = _resid_mod("l1_res1", x2, y1, g_m[1], nf_g[1:2], sc_f[1], sh_f[1])
    ffn_w.append((wb.get("up1", h3), wb.get("dn1", h3), *ffn_s[1]))
    f1, saved1 = _ffn_fwd("l1_ffn", h3, *ffn_w[1])
    dx4, lossv, dy, dgate_f1, _ = _loss_head("loss", x3, f1, g_f[1], tgt)

    dh, gf1 = _ffn_bwd("l1_ffn", dy, h3, *saved1, *ffn_w[1])
    tok = pipe.scatter("g1", dict(dn1=gf1["dn"], up1=gf1["up"]))
    dx3, dsh_f1, dsc_f1, dnf1, dy, dgate_m1, dpw2_b = _mod_bwd("l1_dmod2", dx4, dh, x3, nf_g[1:2], sc_f[1] + tok,
                                                             y1, g_m[1])
    dy4 = _matmul("l1_dpw2_x", dy, wb.get("pw2"), "nt", F32, tm=1024, tn=1024)
    g_pw2 = _matmul("l1_dpw2_w", y4, dy, "tn", BF16, tm=1024, tn=1024, tk=2048)
    dy2, dln_g, dln_b = _ln_silu_bwd("l1_dln", y2, dy4, sp["conv_ln_g"], sp["conv_ln_b"])
    dp, ddw_w, ddw_b, dpw1_b = _glu31_bwd("l1_dglu", p, dy2, sp["conv_dw_w"][0])
    g_pw1 = _matmul("l1_dpw1_w", h2, dp, "tn", BF16, tm=1024, tn=1024, tk=2048)
    tok = pipe.scatter("g2", dict(pw2=g_pw2, pw1=g_pw1))
    pipe.collect("g1", g_pw1)
    dh = _matmul("l1_dpw1_x", dp, wb.get("pw1"), "nt", F32, tm=1024, tn=1024, tk=2048)
    dx2, dsh_m1, dsc_m1, dnm1, dy, dgate_f0, _ = _mod_bwd("l1_dmod1", dx3, dh, x2, nm_g[1:2], sc_m[1] + tok,
                                                        f0, g_f[0])
    dh, gf0 = _ffn_bwd("l0_ffn", dy, h1, *saved0, *ffn_w[0])
    tok = pipe.scatter("g3", dict(dn0=gf0["dn"], up0=gf0["up"]))
    pipe.collect("g2", gf0["up"])
    dx1, dsh_f0, dsc_f0, dnf0, dy, dgate_m0, _ = _mod_bwd("l0_dmod2", dx2, dh, x1, nf_g[0:1], sc_f[0] + tok,
                                                        y0, g_m[0])
    dyab = _matmul("l0_dout_x", dy, wb.get("w_out"), "nt", F32, tm=1024, tn=1024)
    g_out = _matmul("l0_dout_w", yab, dy, "tn", BF16, tm=1024, tn=1024, tk=2048)
    vg = vg + pipe.scatter("g4", dict(w_out=g_out))
    dza, dws, dbf, dvg = _mixa_bwd("l0_dmixa", z, dyab, vg, ws, bias_full)
    deltas, dos = _delta("l0_delta", dyab, yb, ones_bd)
    pieces = []
    for b, dil in enumerate(DILATIONS):
        do = (dyab, ("row", BW, 1)) if dil == 1 else (dos[b - 1], ROW)
        pieces.append(_attn_bwd(f"l0_datt{dil}", *qkv[b], do, lses[b], deltas[b], dil))
    dzb, dqg, dkg = _qkv_bwd("l0_dqkv", z, cos, sin, ones_bd, qg, kg, pieces)
    small_tok = _start_small_grads(pipe, dzb, sp, lossv, ((dgate_m0, dsh_f0, dsc_f0, dgate_f0),
                                                   (dsh_m1, dsc_m1, dgate_m1, dsh_f1, dsc_f1, dgate_f1)),
                                   (dnm1, dnf0, dnf1), dvg, dws, dbf, dqg, dkg, dpw1_b, ddw_w, ddw_b, dln_g, dln_b,
                                   dpw2_b, gf0, gf1)
    dz = jnp.concatenate([dza, dzb], axis=1)
    g_in = _matmul("l0_din_w", h0, dz, "tn", BF16, tm=1024, tn=1280, tk=2048, after=small_tok)
    tok = pipe.scatter("g5", dict(w_in=g_in))
    dh = _matmul("l0_din_x", dz, wb.get("w_in"), "nt", F32, tm=1024, tn=1024, tk=2560, after=small_tok)
    grad_x, dsh_m0, dsc_m0, dnm0 = _mod_bwd("l0_dmod1", dx1, dh, x0, nm_g[0:1], sc_m[0] + tok)
    return lossv, grad_x, (dsh_m0, dsc_m0, dnm0)


def _start_small_grads(pipe, after, sp, lossv, mods, norms, dvg, dws, dbf, dqg, dkg, dpw1_b, ddw_w, ddw_b, dln_g,
                       dln_b, dpw2_b, gf0, gf1):
    (dgate_m0, dsh_f0, dsc_f0, dgate_f0), (dsh_m1, dsc_m1, dgate_m1, dsh_f1, dsc_f1, dgate_f1) = mods
    dnm1, dnf0, dnf1 = norms
    zero = jnp.zeros_like(dnm1)
    dmod = jnp.stack([jnp.concatenate([zero, zero, dgate_m0, dsh_f0, dsc_f0, dgate_f0], axis=0),
                      jnp.concatenate([dsh_m1, dsc_m1, dgate_m1, dsh_f1, dsc_f1, dgate_f1], axis=0)])
    small = dict(
        norm_mix_g=jnp.concatenate([zero, dnm1], axis=0),
        norm_ffn_g=jnp.concatenate([dnf0, dnf1], axis=0),
        a_vnorm_g=dvg.reshape(1, A_GROUPS, LANES),
        a_spatial_w=dws[None],
        a_spatial_b=dbf.reshape(CHUNK, A_GROUPS, LANES).sum(-1).T[None],
        b_q_norm_g=dqg.reshape(N_HEADS, HEAD).sum(0)[None],
        b_k_norm_g=dkg.reshape(N_HEADS, HEAD).sum(0)[None],
        conv_pw1_b=dpw1_b, conv_dw_w=ddw_w[None, :sp["conv_dw_w"].shape[1]], conv_dw_b=ddw_b,
        conv_ln_g=dln_g, conv_ln_b=dln_b, conv_pw2_b=dpw2_b,
        ffn_dw_w=jnp.stack([gf0["dw_w"], gf1["dw_w"]]),
        ffn_dw_b=jnp.concatenate([gf0["dw_b"], gf1["dw_b"]], axis=0),
    )
    return pipe.start_small(dmod, lossv, small, after)


ADA_TN = 512


def _ada_fwd(name, c_all, ada_w, ada_b_sh):
    L, D, N = ada_w.shape
    B = c_all.shape[0]

    def body(c_ref, w_ref, b_ref, o_ref):
        cv = c_ref[...]
        ca = (cv * _sigmoid(cv)).astype(BF16)
        o_ref[0] = jnp.dot(ca, w_ref[0].astype(BF16), preferred_element_type=F32) + b_ref[0]

    return pl.pallas_call(
        body, name=name, grid=(L, N // ADA_TN),
        in_specs=[pl.BlockSpec((B, D), lambda l, j: (0, 0)), pl.BlockSpec((1, D, ADA_TN), lambda l, j: (l, 0, j)),
                  pl.BlockSpec((1, 1, ADA_TN), lambda l, j: (l, 0, j))],
        out_specs=pl.BlockSpec((1, B, ADA_TN), lambda l, j: (l, 0, j)),
        out_shape=jax.ShapeDtypeStruct((L, B, N), F32),
        compiler_params=_params(("parallel", "parallel")),
    )(c_all, ada_w, ada_b_sh.reshape(L, 1, N))


def _adamw_val(w, g, m, v):
    m2 = ADAM_B1 * m + (1.0 - ADAM_B1) * g
    v2 = ADAM_B2 * v + (1.0 - ADAM_B2) * (g * g)
    m_hat = m2 / (1.0 - ADAM_B1 ** ADAM_STEP)
    v_hat = v2 / (1.0 - ADAM_B2 ** ADAM_STEP)
    delta = -ADAM_LR * (m_hat / (jnp.sqrt(v_hat) + ADAM_EPS) + ADAM_WD * w)
    return delta, m2, v2


def _ada_update(name, c_all, dmod_sh, w, m, v):
    L, D, N = w.shape
    B = c_all.shape[0]

    def body(c_ref, d_ref, w_ref, m_ref, v_ref, g_ref, dl_ref, mo_ref, vo_ref):
        cv = c_ref[...]
        ca = (cv * _sigmoid(cv)).astype(BF16)
        g = lax.dot_general(ca, d_ref[0].astype(BF16), _DN["tn"], preferred_element_type=F32)
        g_ref[0] = g
        dl_ref[0], mo_ref[0], vo_ref[0] = _adamw_val(w_ref[0], g, m_ref[0], v_ref[0])

    wspec = pl.BlockSpec((1, D, ADA_TN), lambda l, j: (l, 0, j))
    return pl.pallas_call(
        body, name=name, grid=(L, N // ADA_TN),
        in_specs=[pl.BlockSpec((B, D), lambda l, j: (0, 0)), pl.BlockSpec((1, B, ADA_TN), lambda l, j: (l, 0, j)),
                  wspec, wspec, wspec],
        out_specs=[wspec] * 4, out_shape=[jax.ShapeDtypeStruct((L, D, N), F32)] * 4,
        compiler_params=_params(("parallel", "parallel")),
    )(c_all, dmod_sh, w, m, v)


def _adamw(name, w, g, m, v):
    R, C = w.shape
    tm = R
    for cand in (256, 128, 64, 32, 16, 8):
        if R % cand == 0 and cand * C * 4 <= (1 << 20):
            tm = cand
            break

    def body(w_ref, g_ref, m_ref, v_ref, d_ref, mo_ref, vo_ref):
        d_ref[...], mo_ref[...], vo_ref[...] = _adamw_val(w_ref[...], g_ref[...], m_ref[...], v_ref[...])

    return _rowcall(name, body, R, tm, [(w, ROW), (g, ROW), (m, ROW), (v, ROW)], [((R, C), F32, ROW)] * 3)


def _row_tile(rows, width, itemsize=4, limit=1 << 20):
    for cand in range(512, 0, -16):
        if rows % cand == 0 and cand * width * itemsize <= limit:
            return cand
    raise ValueError((rows, width))


def _cast_into_full(name, a, layer, q, kind, after):
    L, r, c = a.shape
    tm = _row_tile(r, c, limit=2 << 20)
    if kind == "col":
        full, o_spec = (r, N_CHIPS * c), pl.BlockSpec((tm, c), lambda i, q_ref: (i, q_ref[0]))
    else:
        full, o_spec = (N_CHIPS * r, c), pl.BlockSpec((tm, c), lambda i, q_ref: (q_ref[0] * (r // tm) + i, 0))

    def body(q_ref, a_ref, after_ref, o_ref):
        o_ref[...] = a_ref[0].astype(BF16)

    return pl.pallas_call(
        body, name=name,
        grid_spec=pltpu.PrefetchScalarGridSpec(
            num_scalar_prefetch=1, grid=(r // tm,),
            in_specs=[pl.BlockSpec((1, tm, c), lambda i, q_ref: (layer, i, 0)), ANY], out_specs=o_spec),
        out_shape=jax.ShapeDtypeStruct(full, BF16), compiler_params=_params(("parallel",)),
    )(q.reshape(1).astype(jnp.int32), a, after)


def _sum4(name, g, rcv, q, kind, n):
    r, c = rcv.shape[1:]
    tm = _row_tile(r, c, limit=2 << 20)
    if kind == "col":
        g_spec = pl.BlockSpec((tm, n), lambda i, q_ref: (i, q_ref[0]))
    else:
        g_spec = pl.BlockSpec((tm, c), lambda i, q_ref: (q_ref[0] * (n // tm) + i, 0))

    def body(q_ref, g_ref, r_ref, o_ref):
        acc = g_ref[...].astype(F32)
        for j in range(3):
            acc = acc + r_ref[j].astype(F32)
        o_ref[...] = acc

    return pl.pallas_call(
        body, name=name,
        grid_spec=pltpu.PrefetchScalarGridSpec(
            num_scalar_prefetch=1, grid=(r // tm,),
            in_specs=[g_spec, pl.BlockSpec((3, tm, c), lambda i, q_ref: (0, i, 0))],
            out_specs=pl.BlockSpec((tm, c), lambda i, q_ref: (i, 0))),
        out_shape=jax.ShapeDtypeStruct((r, c), F32), compiler_params=_params(("parallel",)),
    )(q.reshape(1).astype(jnp.int32), g, rcv)


def _adamw_sum(name, w, m, v, layer, mine, theirs, prev):
    L, r, c = w.shape
    tm = _row_tile(r, c, limit=3 << 19)
    lay = pl.BlockSpec((1, tm, c), lambda i: (layer, i, 0))
    flat = pl.BlockSpec((tm, c), lambda i: (i, 0))
    n_prev = 0 if prev is None else 4

    def body(w_ref, m_ref, v_ref, a_ref, b_ref, *rest):
        g_ref, d_ref, mo_ref, vo_ref = rest[n_prev:]
        g = a_ref[...] + b_ref[...]
        g_ref[0] = g
        d_ref[0], mo_ref[0], vo_ref[0] = _adamw_val(w_ref[0], g, m_ref[0], v_ref[0])

    return pl.pallas_call(
        body, name=name, grid=(r // tm,),
        in_specs=[lay, lay, lay, flat, flat] + [ANY] * n_prev, out_specs=[lay] * 4,
        out_shape=[jax.ShapeDtypeStruct((L, r, c), F32)] * 4,
        input_output_aliases={5 + k: k for k in range(n_prev)},
        compiler_params=_params(("parallel",)),
    )(w, m, v, mine, theirs, *(prev or ()))


def _sum8(name, gathered, own=None):
    R, N = gathered.shape
    P = R // 8

    def body(g_ref, *rest):
        o_ref = rest[-1]
        me = 4 * lax.axis_index("x") + 2 * lax.axis_index("y") + lax.axis_index("c")
        acc = None
        for d in range(8):
            blk = g_ref[d * P:(d + 1) * P, :]
            if own is not None:
                blk = jnp.where(me == d, rest[0][...], blk)
            acc = blk if d == 0 else acc + blk
        o_ref[...] = acc

    return pl.pallas_call(body, name=name, out_shape=jax.ShapeDtypeStruct((P, N), F32),
                          compiler_params=pltpu.CompilerParams(vmem_limit_bytes=VMEM_LIMIT),
                          )(gathered, *(() if own is None else (own,)))


ANY = pl.BlockSpec(memory_space=pl.ANY)


def _mesh_pos():
    x, y, c = lax.axis_index("x"), lax.axis_index("y"), lax.axis_index("c")
    other_chips = [(1 - x, y), (x, 1 - y), (1 - x, 1 - y)]
    return x, y, c, other_chips


def _allgather8(name, blk, after=None):
    m_per, n = blk.shape

    def body(x_ref, *rest):
        out_ref, send_sems, recv_sems, local_sem = rest[after is not None:]
        x, y, c, chips = _mesh_pos()
        me, sibling = (x, y, c), (x, y, 1 - c)

        def rows(px, py, pc):
            return out_ref.at[pl.ds((4 * px + 2 * py + pc) * m_per, m_per), :]

        def copy(k, block, to, src=None):
            return pltpu.make_async_remote_copy(
                src_ref=rows(*block) if src is None else src, dst_ref=rows(*block),
                send_sem=send_sems.at[k], recv_sem=recv_sems.at[k], device_id=to, device_id_type=MESH)

        mine = pltpu.make_async_copy(x_ref, rows(*me), local_sem)
        mine.start()
        first = [copy(0, me, sibling, src=x_ref)]
        first += [copy(1 + j, me, (*chip, c), src=x_ref) for j, chip in enumerate(chips)]
        for cp in first:
            cp.start()
        passed = [copy(4 + j, (*chip, c), sibling) for j, chip in enumerate(chips)]
        for j, chip in enumerate(chips):
            copy(1 + j, (*chip, c), me).wait_recv()
            passed[j].start()
        copy(0, sibling, me).wait_recv()
        for j, chip in enumerate(chips):
            copy(4 + j, (*chip, 1 - c), me).wait_recv()
        for cp in first + passed:
            cp.wait_send()
        mine.wait()

    return pl.pallas_call(
        body, name=name, out_shape=jax.ShapeDtypeStruct((8 * m_per, n), blk.dtype),
        in_specs=[pl.BlockSpec(memory_space=pltpu.VMEM)] + [ANY] * (after is not None),
        out_specs=pl.BlockSpec(memory_space=pltpu.VMEM),
        scratch_shapes=[pltpu.SemaphoreType.DMA((7,)), pltpu.SemaphoreType.DMA((7,)), pltpu.SemaphoreType.DMA],
        compiler_params=pltpu.CompilerParams(vmem_limit_bytes=VMEM_LIMIT),
    )(blk, *(() if after is None else (after,)))


BIG = dict(w_in=("col", "ab_w_in", 0), w_out=("row", "ab_w_out", 0), up0=("col", "ffn_up_w", 0),
           dn0=("row", "ffn_down_w", 0), pw1=("col", "conv_pw1_w", 0), pw2=("row", "conv_pw2_w", 0),
           up1=("col", "ffn_up_w", 1), dn1=("row", "ffn_down_w", 1))
N_CHIPS = 4
HBM = pl.BlockSpec(memory_space=pltpu.HBM)
SEM = pl.BlockSpec(memory_space=pltpu.SEMAPHORE)
EFFECT = pltpu.SideEffectType.DATAFLOW_SIDE_EFFECTING


def _region(kind, ref, q, n):
    if kind == "col":
        return ref.at[:, pl.ds(q * n, n)]
    return ref.at[pl.ds(q * n, n), :]


def _gather_plan(kind, n):
    def remote(src, land, pos):
        x, y, c, chips = pos
        mine = _region(kind, land, 2 * x + y, n)
        return [(mine, mine, (*chip, c)) for chip in chips]

    return ("gather", kind, n), remote


def _scatter_plan(kind, n):
    def remote(src, land, pos):
        _, _, c, chips = pos
        return [(_region(kind, src, 2 * chip[0] + chip[1], n), land.at[j], (*chip, c)) for j, chip in enumerate(chips)]

    return ("scatter", kind, n), remote


def _everyone_plan(rows):
    def remote(src, land, pos):
        x, y, c, _ = pos
        mine = land.at[pl.ds((4 * x + 2 * y + c) * rows, rows), :]
        flip = lambda v, bit: 1 - v if bit else v
        return [(src, mine, (flip(x, k & 4), flip(y, k & 2), flip(c, k & 1))) for k in range(1, 8)]

    return ("everyone", rows), remote


def _sibling_plan():
    def remote(src, land, pos):
        x, y, c, _ = pos
        return [(src, land, (x, y, 1 - c))]

    return ("sibling",), remote


def _split_start(name, items, after=None):
    n = len(items)
    plans = [it[2] for it in items]
    n_in = 2 * n + (after is not None)

    def body(*refs):
        srcs, lands = refs[:n], refs[n:2 * n]
        sends, recvs = refs[n_in:n_in + n], refs[n_in + n:n_in + 2 * n]
        token = refs[n_in + 4 * n]
        pos = _mesh_pos()
        for a, (_, remote) in enumerate(plans):
            for k, (s, d, dev) in enumerate(remote(srcs[a], lands[a], pos)):
                pltpu.make_async_remote_copy(src_ref=s, dst_ref=d, send_sem=sends[a].at[k], recv_sem=recvs[a].at[k],
                                             device_id=dev, device_id_type=MESH).start()
        token[...] = jnp.zeros_like(token)

    sems = [pltpu.SemaphoreType.DMA((it[3],)) for it in items]
    bufs = [pltpu.HBM(it[k].shape, it[k].dtype) for k in (0, 1) for it in items]
    outs = pl.pallas_call(
        body, name=name, out_shape=[*sems, *sems, *bufs, jax.ShapeDtypeStruct((8, LANES), F32)],
        in_specs=[HBM] * (2 * n) + [ANY] * (n_in - 2 * n),
        out_specs=[SEM] * (2 * n) + [HBM] * (2 * n) + [pl.BlockSpec(memory_space=pltpu.VMEM)],
        input_output_aliases={i: 2 * n + i for i in range(2 * n)},
        compiler_params=pltpu.CompilerParams(has_side_effects=EFFECT),
    )(*[pltpu.with_memory_space_constraint(it[k], pltpu.HBM) for k in (0, 1) for it in items],
      *(() if after is None else (after,)))
    state = [(items[a][2], items[a][3], outs[2 * n + a], outs[3 * n + a], outs[a], outs[n + a]) for a in range(n)]
    return state, outs[4 * n]


def _split_wait(name, state, after):
    n = len(state)

    def body(*refs):
        srcs, lands = refs[:n], refs[n:2 * n]
        sends, recvs = refs[2 * n:3 * n], refs[3 * n:4 * n]
        pos = _mesh_pos()
        for a, ((_, remote), *_) in enumerate(state):
            for k, (s, d, dev) in enumerate(remote(srcs[a], lands[a], pos)):
                cp = pltpu.make_async_remote_copy(src_ref=s, dst_ref=d, send_sem=sends[a].at[k], recv_sem=recvs[a].at[k],
                                                  device_id=dev, device_id_type=MESH)
                cp.wait_send()
                cp.wait_recv()

    bufs = [st[k] for k in (2, 3) for st in state]
    outs = pl.pallas_call(
        body, name=name, out_shape=[pltpu.HBM(b.shape, b.dtype) for b in bufs],
        in_specs=[HBM] * (2 * n) + [SEM] * (2 * n) + [ANY], out_specs=[HBM] * (2 * n),
        input_output_aliases={i: i for i in range(2 * n)},
        compiler_params=pltpu.CompilerParams(has_side_effects=EFFECT),
    )(*bufs, *[st[k] for k in (4, 5) for st in state], after)
    return outs[:n], outs[n:]


class _Weights:
    def __init__(self, w, q, after):
        unused = jnp.zeros((16, LANES), BF16)

        def item(name, after):
            kind, pname, layer = BIG[name]
            _, r, c = w[pname].shape
            land = _cast_into_full(f"cast_{name}", w[pname], layer, q, kind, after)
            return unused, land, _gather_plan(kind, c if kind == "col" else r), N_CHIPS - 1

        first, *rest = BIG
        state1, token1 = _split_start("gw_start_first", [item(first, after)], after)
        state2, self.token = _split_start("gw_start_rest", [item(name, token1) for name in rest], token1)
        self.pending = dict(zip(BIG, state1 + state2))
        self.ready = {}

    def get(self, name, after=None):
        if name not in self.ready:
            self.ready[name] = _split_wait(f"gw_wait_{name}", [self.pending.pop(name)], after)[1][0]
        return self.ready[name]


class _GradPipe:
    def __init__(self, q, w, m, v):
        self.q, self.w, self.m, self.v = q, w, m, v
        self.stage, self.results = {}, {}

    def start_small(self, dmod, lossv, small, after):
        self.small_names = [n for n in REPLICATED if n != "ada_b"] + list(SMALL_SHARDED)
        payload = [dmod.reshape(2, -1), lossv] + [small[n] for n in self.small_names]
        self.small_shapes = [p.shape for p in payload]
        packed = _pack(payload)
        land = jnp.zeros((8 * PACK_ROWS, packed.shape[1]), F32)
        self.small_state, token = _split_start("ag_grads_start", [(packed, land, _everyone_plan(PACK_ROWS), 7)], after)
        return token

    def wait_small(self, after):
        srcs, lands = _split_wait("ag_grads_wait", self.small_state, after)
        return srcs[0], lands[0]

    def scatter(self, group, grads, after=None):
        items = []
        for name, g in grads.items():
            kind = BIG[name][0]
            rows, cols = g.shape
            n = (cols if kind == "col" else rows) // N_CHIPS
            reg = (rows, n) if kind == "col" else (n, cols)
            items.append((g, lax.empty((N_CHIPS - 1, *reg), BF16), _scatter_plan(kind, n), N_CHIPS - 1))
        state, token = _split_start(f"gs_start_{group}", items, after)
        self.stage[group] = (list(grads), state)
        return token[0, 0]

    def collect(self, group, after):
        names, state = self.stage[group]
        srcs, lands = _split_wait(f"gs_wait_{group}", state, after)
        items = []
        for name, st, g, land in zip(names, state, srcs, lands):
            _, kind, n = st[0][0]
            part = _sum4(f"sum_{name}", g, land, self.q, kind, n)
            items.append((part, lax.empty(part.shape, F32), _sibling_plan(), 1))
        state, token = _split_start(f"sw_start_{group}", items)
        self.stage[group] = (names, state)
        return token

    def finish(self, group, after):
        names, state = self.stage.pop(group)
        srcs, lands = _split_wait(f"sw_wait_{group}", state, after)
        for name, mine, theirs in zip(names, srcs, lands):
            _, pname, layer = BIG[name]
            self.results[pname] = _adamw_sum(f"adamw_{name}", self.w[pname], self.m[pname], self.v[pname], layer,
                                             mine, theirs, self.results.get(pname))


PACK_ROWS = 8


def _pack(arrays):
    flat = jnp.concatenate([a.reshape(-1) for a in arrays])
    n = flat.shape[0]
    padded = -(-n // (PACK_ROWS * LANES)) * (PACK_ROWS * LANES)
    return jnp.pad(flat, (0, padded - n)).reshape(PACK_ROWS, padded // PACK_ROWS)


def _unpack(packed, shapes):
    flat = packed.reshape(-1)
    out, off = [], 0
    for s in shapes:
        n = 1
        for d in s:
            n *= d
        out.append(flat[off:off + n].reshape(s))
        off += n
    return out


REPLICATED = ("ada_b", "norm_mix_g", "norm_ffn_g", "a_vnorm_g", "a_spatial_w", "a_spatial_b", "b_q_norm_g",
              "b_k_norm_g", "ffn_dw_b")
SMALL_SHARDED = ("conv_pw1_b", "conv_dw_w", "conv_dw_b", "conv_ln_g", "conv_ln_b", "conv_pw2_b", "ffn_dw_w")
WEIGHTS = ("ada_w", "ada_b", "norm_mix_g", "norm_ffn_g", "ab_w_in", "a_vnorm_g", "a_spatial_w", "a_spatial_b",
           "b_q_norm_g", "b_k_norm_g", "ab_w_out", "conv_pw1_w", "conv_pw1_b", "conv_dw_w", "conv_dw_b", "conv_ln_g",
           "conv_ln_b", "conv_pw2_w", "conv_pw2_b", "ffn_up_w", "ffn_dw_w", "ffn_dw_b", "ffn_down_w")

def kernel(x, c, positions, ada_w, ada_b, norm_mix_g, norm_ffn_g, ab_w_in, a_vnorm_g, a_spatial_w, a_spatial_b, b_q_norm_g, b_k_norm_g, ab_w_out, conv_pw1_w, conv_pw1_b, conv_dw_w, conv_dw_b, conv_ln_g, conv_ln_b, conv_pw2_w, conv_pw2_b, ffn_up_w, ffn_dw_w, ffn_dw_b, ffn_down_w, loss_target, m_ada_w, m_ada_b, m_norm_mix_g, m_norm_ffn_g, m_ab_w_in, m_a_vnorm_g, m_a_spatial_w, m_a_spatial_b, m_b_q_norm_g, m_b_k_norm_g, m_ab_w_out, m_conv_pw1_w, m_conv_pw1_b, m_conv_dw_w, m_conv_dw_b, m_conv_ln_g, m_conv_ln_b, m_conv_pw2_w, m_conv_pw2_b, m_ffn_up_w, m_ffn_dw_w, m_ffn_dw_b, m_ffn_down_w, v_ada_w, v_ada_b, v_norm_mix_g, v_norm_ffn_g, v_ab_w_in, v_a_vnorm_g, v_a_spatial_w, v_a_spatial_b, v_b_q_norm_g, v_b_k_norm_g, v_ab_w_out, v_conv_pw1_w, v_conv_pw1_b, v_conv_dw_w, v_conv_dw_b, v_conv_ln_g, v_conv_ln_b, v_conv_pw2_w, v_conv_pw2_b, v_ffn_up_w, v_ffn_dw_w, v_ffn_dw_b, v_ffn_down_w):
    w = dict(ada_w=ada_w, ada_b=ada_b, norm_mix_g=norm_mix_g, norm_ffn_g=norm_ffn_g, ab_w_in=ab_w_in, a_vnorm_g=a_vnorm_g, a_spatial_w=a_spatial_w, a_spatial_b=a_spatial_b, b_q_norm_g=b_q_norm_g, b_k_norm_g=b_k_norm_g, ab_w_out=ab_w_out, conv_pw1_w=conv_pw1_w, conv_pw1_b=conv_pw1_b, conv_dw_w=conv_dw_w, conv_dw_b=conv_dw_b, conv_ln_g=conv_ln_g, conv_ln_b=conv_ln_b, conv_pw2_w=conv_pw2_w, conv_pw2_b=conv_pw2_b, ffn_up_w=ffn_up_w, ffn_dw_w=ffn_dw_w, ffn_dw_b=ffn_dw_b, ffn_down_w=ffn_down_w)
    m = dict(ada_w=m_ada_w, ada_b=m_ada_b, norm_mix_g=m_norm_mix_g, norm_ffn_g=m_norm_ffn_g, ab_w_in=m_ab_w_in, a_vnorm_g=m_a_vnorm_g, a_spatial_w=m_a_spatial_w, a_spatial_b=m_a_spatial_b, b_q_norm_g=m_b_q_norm_g, b_k_norm_g=m_b_k_norm_g, ab_w_out=m_ab_w_out, conv_pw1_w=m_conv_pw1_w, conv_pw1_b=m_conv_pw1_b, conv_dw_w=m_conv_dw_w, conv_dw_b=m_conv_dw_b, conv_ln_g=m_conv_ln_g, conv_ln_b=m_conv_ln_b, conv_pw2_w=m_conv_pw2_w, conv_pw2_b=m_conv_pw2_b, ffn_up_w=m_ffn_up_w, ffn_dw_w=m_ffn_dw_w, ffn_dw_b=m_ffn_dw_b, ffn_down_w=m_ffn_down_w)
    v = dict(ada_w=v_ada_w, ada_b=v_ada_b, norm_mix_g=v_norm_mix_g, norm_ffn_g=v_norm_ffn_g, ab_w_in=v_ab_w_in, a_vnorm_g=v_a_vnorm_g, a_spatial_w=v_a_spatial_w, a_spatial_b=v_a_spatial_b, b_q_norm_g=v_b_q_norm_g, b_k_norm_g=v_b_k_norm_g, ab_w_out=v_ab_w_out, conv_pw1_w=v_conv_pw1_w, conv_pw1_b=v_conv_pw1_b, conv_dw_w=v_conv_dw_w, conv_dw_b=v_conv_dw_b, conv_ln_g=v_conv_ln_g, conv_ln_b=v_conv_ln_b, conv_pw2_w=v_conv_pw2_w, conv_pw2_b=v_conv_pw2_b, ffn_up_w=v_ffn_up_w, ffn_dw_w=v_ffn_dw_w, ffn_dw_b=v_ffn_dw_b, ffn_down_w=v_ffn_down_w)
    S, D = x.shape[1], x.shape[2]
    xi, yi, ci = lax.axis_index("x"), lax.axis_index("y"), lax.axis_index("c")
    q = 2 * xi + yi
    b = 2 * q + ci
    take_dev = lambda g: g.reshape(8, PACK_ROWS, -1)

    c_all = _allgather8("ag_c", c.reshape(PACK_ROWS, D // PACK_ROWS)).reshape(8, D)
    n_ada = ada_w.shape[2]
    mod_sh = _ada_fwd("ada_fwd", c_all, ada_w, lax.dynamic_slice_in_dim(ada_b, q * n_ada, n_ada, axis=1))
    sh_shapes = [mod_sh.shape] + [w[n].shape for n in SMALL_SHARDED]
    gathered_mod = _allgather8("ag_mod", _pack([mod_sh] + [w[n] for n in SMALL_SHARDED]))
    per_chip = [_unpack(blk, sh_shapes) for blk in take_dev(gathered_mod)[0::2]]
    mod_g = jnp.stack([pc[0] for pc in per_chip])
    mod_mine = lax.dynamic_index_in_dim(mod_g, b, axis=2, keepdims=False)
    mod = mod_mine.transpose(1, 0, 2).reshape(2, 6, D)
    sp = {n: jnp.concatenate([pc[1 + i] for pc in per_chip], axis=-1) for i, n in enumerate(SMALL_SHARDED)}
    sp.update({n: w[n] for n in REPLICATED if n != "ada_b"})

    wb = _Weights(w, q, gathered_mod)
    mod = mod + wb.token[0, 0]

    pipe = _GradPipe(q, w, m, v)
    lossv, grad_x, late = _local_step(x[0], loss_target[0], positions[0], mod, wb, sp, pipe)

    pipe.finish("g1", grad_x)
    pipe.finish("g2", pipe.results["ffn_up_w"][0])
    swapped = pipe.collect("g3", pipe.results["conv_pw1_w"][0])
    swapped = pipe.collect("g4", swapped)

    own, gathered = pipe.wait_small(swapped)
    totals = _unpack(_sum8("sum_grads", gathered, own), pipe.small_shapes)
    grads = dict(zip(["ada_b", "loss_columns"] + pipe.small_names, totals))
    loss = 0.5 * jnp.sum(grads.pop("loss_columns")) / D
    late_g = _allgather8("ag_late", _pack(list(late)), gathered)
    late_tot = _unpack(_sum8("sum_late", late_g), [(3, D)])[0]
    grads["ada_b"] = grads["ada_b"].at[0, :2 * D].add(late_tot[:2].reshape(-1))
    grads["norm_mix_g"] = grads["norm_mix_g"].at[0].add(late_tot[2])
    for n in SMALL_SHARDED:
        n_sh = w[n].shape[-1]
        grads[n] = lax.dynamic_slice_in_dim(grads[n], q * n_sh, n_sh, axis=grads[n].ndim - 1)
    dmod_of = lambda packed: packed.reshape(packed.shape[0] // PACK_ROWS, -1)[:, :2 * 6 * D].reshape(-1, 2, 6 * D)
    dmod_all = jnp.where((jnp.arange(8) == b)[:, None, None], dmod_of(own), dmod_of(gathered))
    late_all = take_dev(late_g).reshape(8, -1)[:, :3 * D].reshape(8, 3, D)
    dmod_all = dmod_all.at[:, 0, :2 * D].add(late_all[:, :2].reshape(8, 2 * D))
    dmod_sh = lax.dynamic_slice_in_dim(dmod_all, q * n_ada, n_ada, axis=2).transpose(1, 0, 2)

    pipe.finish("g3", dmod_sh)
    pipe.finish("g4", pipe.results["ffn_up_w"][0])
    grads["ada_w"], delta_ada, m_ada, v_ada = _ada_update("ada_update", c_all, dmod_sh, ada_w, m_ada_w, v_ada_w)
    delta, new_m, new_v = dict(ada_w=delta_ada), dict(ada_w=m_ada), dict(ada_w=v_ada)
    rest = list(REPLICATED) + list(SMALL_SHARDED)
    rest_shapes = [w[n].shape for n in rest]
    outs = _adamw("adamw_small", *[_pack([src[n].reshape(w[n].shape) for n in rest]) for src in (w, grads, m, v)])
    for tgt, packed in zip((delta, new_m, new_v), outs):
        tgt.update(dict(zip(rest, _unpack(packed, rest_shapes))))
    for n in rest:
        grads[n] = grads[n].reshape(w[n].shape)
    pipe.finish("g5", pipe.collect("g5", outs[0]))
    for n, res in pipe.results.items():
        grads[n], delta[n], new_m[n], new_v[n] = res

    return (loss, grad_x[None], *[grads[n] for n in WEIGHTS], *[delta[n] for n in WEIGHTS],
            *[new_m[n] for n in WEIGHTS], *[new_v[n] for n in WEIGHTS])
```
